```python
import math
import jax
import jax.numpy as jnp
from jax import lax
import numpy as np

D_MODEL = 1024
BATCH = 8
SEQ = 8192
DEPTH = 1

HEAD_DIM = 64
A_Q_HEADS = 8
A_KV_HEADS = 2
B_Q_HEADS = 8
B_KV_HEADS = 2
BRANCH_WIDTH = A_Q_HEADS * HEAD_DIM
D_FF = 4 * D_MODEL
GRID_W = 64
Q_BLOCK = 128
WINDOW = 128
BAND_BLOCK = WINDOW
ROPE_THETA = 10000.0
AXIAL_THETA = 10000.0
NORM_EPS = 1e-6
NEG_INF = -1e30

A_Q_W = A_Q_HEADS * HEAD_DIM
A_KV_W = A_KV_HEADS * HEAD_DIM
B_Q_W = B_Q_HEADS * HEAD_DIM
B_KV_W = B_KV_HEADS * HEAD_DIM
IN_SPLITS = [A_Q_W, A_KV_W, A_KV_W, B_Q_W, B_KV_W, B_KV_W, D_MODEL, D_MODEL]
IN_WIDTH = sum(IN_SPLITS)

kernel_name = "hybrid_gated_axial_window_attention_block"


def rms_norm(x, g):
    xf = x.astype(jnp.float32)
    y = xf * lax.rsqrt(jnp.mean(xf * xf, axis=-1, keepdims=True) + NORM_EPS)
    return (y * g.astype(jnp.float32)).astype(x.dtype)


def rope_cos_sin(pos, dim, theta):
    inv = theta ** (-jnp.arange(0, dim, 2, dtype=jnp.float32) / dim)
    ang = pos.astype(jnp.float32)[:, None] * inv[None, :]
    return jnp.cos(ang), jnp.sin(ang)


def apply_rope(x, cos, sin):
    xf = x.astype(jnp.float32)
    half = xf.shape[-1] // 2
    x1, x2 = xf[..., :half], xf[..., half:]
    c = cos[None, :, None, :]
    s = sin[None, :, None, :]
    return jnp.concatenate([x1 * c - x2 * s, x1 * s + x2 * c], axis=-1).astype(x.dtype)


def apply_axial_rope(x, row, col):
    half = x.shape[-1] // 2
    cr, sr = rope_cos_sin(row, half, AXIAL_THETA)
    cc, sc = rope_cos_sin(col, half, AXIAL_THETA)
    return jnp.concatenate([apply_rope(x[..., :half], cr, sr),
                            apply_rope(x[..., half:], cc, sc)], axis=-1)


def global_attention(q, k, v):
    b, s, hq, dh = q.shape
    hkv = k.shape[2]
    g = hq // hkv
    nb = s // Q_BLOCK
    scale = dh ** -0.5
    qb = q.reshape(b, nb, Q_BLOCK, hkv, g, dh).transpose(1, 0, 2, 3, 4, 5)

    def one_block(qblk):
        sc = jnp.einsum('bqkgd,bskd->bkgqs', qblk, k).astype(jnp.float32) * scale
        p = jax.nn.softmax(sc, axis=-1).astype(v.dtype)
        return jnp.einsum('bkgqs,bskd->bqkgd', p, v)

    o = lax.map(one_block, qb)
    return o.transpose(1, 0, 2, 3, 4, 5).reshape(b, s, hq * dh)


def window_sink_attention(q, k, v, sink):
    b, s, hq, dh = q.shape
    hkv = k.shape[2]
    g = hq // hkv
    nb = s // BAND_BLOCK
    scale = dh ** -0.5
    pad = ((0, 0), (BAND_BLOCK, BAND_BLOCK), (0, 0), (0, 0))
    kr = jnp.pad(k, pad).reshape(b, nb + 2, BAND_BLOCK, hkv, dh)
    vr = jnp.pad(v, pad).reshape(b, nb + 2, BAND_BLOCK, hkv, dh)
    kb = jnp.concatenate([kr[:, :-2], kr[:, 1:-1], kr[:, 2:]], axis=2)
    vb = jnp.concatenate([vr[:, :-2], vr[:, 1:-1], vr[:, 2:]], axis=2)
    qb = q.reshape(b, nb, BAND_BLOCK, hkv, g, dh)
    sc = jnp.einsum('bnqkgd,bnskd->bnkgqs', qb, kb).astype(jnp.float32) * scale
    blk = jnp.arange(nb, dtype=jnp.int32)[:, None] * BAND_BLOCK
    qpos = blk + jnp.arange(BAND_BLOCK, dtype=jnp.int32)[None, :]
    kpos = blk - BAND_BLOCK + jnp.arange(3 * BAND_BLOCK, dtype=jnp.int32)[None, :]
    valid = (jnp.abs(kpos[:, None, :] - qpos[:, :, None]) <= WINDOW) \
        & (kpos[:, None, :] >= 0) & (kpos[:, None, :] < s)
    sc = jnp.where(valid[None, :, None, None], sc, NEG_INF)
    sink_l = jnp.broadcast_to(sink.astype(jnp.float32).reshape(1, 1, hkv, g, 1, 1),
                              sc.shape[:-1] + (1,))
    p = jax.nn.softmax(jnp.concatenate([sc, sink_l], axis=-1), axis=-1)[..., :-1]
    o = jnp.einsum('bnkgqs,bnskd->bnqkgd', p.astype(v.dtype), vb)
    return o.reshape(b, s, hq * dh)


def _fwd_setup_inputs(seed: int = 0) -> dict:
    key = jax.random.key(seed)
    ks = jax.random.split(key, 16)
    f32 = jnp.float32
    d = D_MODEL

    def nrm(k, shape, scale):
        return jax.random.normal(k, shape, f32) * scale

    return {
        "x": nrm(ks[0], (BATCH, SEQ, d), 1.0),
        "c": nrm(ks[1], (BATCH, d), 1.0),
        "w_ada": nrm(ks[2], (DEPTH, d, 6 * d), 0.02),
        "b_ada": nrm(ks[3], (DEPTH, 6 * d), 0.02),
        "norm1_g": 1.0 + nrm(ks[4], (DEPTH, d), 0.02),
        "w_in": nrm(ks[5], (DEPTH, d, IN_WIDTH), d ** -0.5),
        "q_norm_a": 1.0 + nrm(ks[6], (DEPTH, HEAD_DIM), 0.02),
        "k_norm_a": 1.0 + nrm(ks[7], (DEPTH, HEAD_DIM), 0.02),
        "sink_b": nrm(ks[8], (DEPTH, B_Q_HEADS), 0.5),
        "w_branch": nrm(ks[9], (DEPTH, 2, BRANCH_WIDTH, d), BRANCH_WIDTH ** -0.5),
        "w_out": nrm(ks[10], (DEPTH, d, d), d ** -0.5),
        "norm2_g": 1.0 + nrm(ks[11], (DEPTH, d), 0.02),
        "w_mlp_in": nrm(ks[12], (DEPTH, d, D_FF), d ** -0.5),
        "w_mlp_out": nrm(ks[13], (DEPTH, D_FF, d), D_FF ** -0.5),
        "final_g": 1.0 + nrm(ks[14], (d,), 0.02),
    }


def _fwd_reference(x, c, w_ada, b_ada, norm1_g, w_in, q_norm_a, k_norm_a, sink_b,
              w_branch, w_out, norm2_g, w_mlp_in, w_mlp_out, final_g):
    b, s, d = x.shape
    rows = s // GRID_W
    t = jnp.arange(s, dtype=jnp.int32)
    row_ids = jnp.repeat(jnp.arange(rows, dtype=jnp.int32), GRID_W)
    col_ids = jnp.tile(jnp.arange(GRID_W, dtype=jnp.int32), rows)
    cos1, sin1 = rope_cos_sin(t, HEAD_DIM, ROPE_THETA)
    offsets = np.cumsum(IN_SPLITS)[:-1].tolist()

    for l in range(DEPTH):
        mod = jax.nn.silu(c) @ w_ada[l] + b_ada[l]
        shift1, scale1, gate1, shift2, scale2, gate2 = jnp.split(mod, 6, axis=-1)

        h = rms_norm(x, norm1_g[l]) * (1.0 + scale1[:, None]) + shift1[:, None]
        proj = h @ w_in[l]
        qa, ka, va, qb, kb, vb, ga, gb = jnp.split(proj, offsets, axis=-1)

        qa = rms_norm(qa.reshape(b, s, A_Q_HEADS, HEAD_DIM), q_norm_a[l])
        ka = rms_norm(ka.reshape(b, s, A_KV_HEADS, HEAD_DIM), k_norm_a[l])
        qa = apply_axial_rope(qa, row_ids, col_ids)
        ka = apply_axial_rope(ka, row_ids, col_ids)
        ya = global_attention(qa, ka, va.reshape(b, s, A_KV_HEADS, HEAD_DIM))

        qb = apply_rope(qb.reshape(b, s, B_Q_HEADS, HEAD_DIM), cos1, sin1)
        kb = apply_rope(kb.reshape(b, s, B_KV_HEADS, HEAD_DIM), cos1, sin1)
        yb = window_sink_attention(qb, kb, vb.reshape(b, s, B_KV_HEADS, HEAD_DIM), sink_b[l])

        ua = ya @ w_branch[l, 0]
        ub = yb @ w_branch[l, 1]
        merged = jax.nn.sigmoid(ga) * ua + jax.nn.sigmoid(gb) * ub
        x = x + gate1[:, None] * (merged @ w_out[l])

        h2 = rms_norm(x, norm2_g[l]) * (1.0 + scale2[:, None]) + shift2[:, None]
        hid = jnp.square(jax.nn.relu(h2 @ w_mlp_in[l]))
        x = x + gate2[:, None] * (hid @ w_mlp_out[l])

    return rms_norm(x, final_g)


import jax as _jax
import jax.numpy as _jnp

TWIN_FORMAT = 'train_step'
FWD_PARAMS = ['x', 'c', 'w_ada', 'b_ada', 'norm1_g', 'w_in', 'q_norm_a', 'k_norm_a', 'sink_b', 'w_branch', 'w_out', 'norm2_g', 'w_mlp_in', 'w_mlp_out', 'final_g']
TWIN_WEIGHTS = ['w_ada', 'b_ada', 'norm1_g', 'w_in', 'q_norm_a', 'k_norm_a', 'sink_b', 'w_branch', 'w_out', 'norm2_g', 'w_mlp_in', 'w_mlp_out', 'final_g']
TWIN_DIFF_INPUT = 'x'
TWIN_INPUTS = ['x', 'c', 'w_ada', 'b_ada', 'norm1_g', 'w_in', 'q_norm_a', 'k_norm_a', 'sink_b', 'w_branch', 'w_out', 'norm2_g', 'w_mlp_in', 'w_mlp_out', 'final_g', 'loss_target', 'm_w_ada', 'm_b_ada', 'm_norm1_g', 'm_w_in', 'm_q_norm_a', 'm_k_norm_a', 'm_sink_b', 'm_w_branch', 'm_w_out', 'm_norm2_g', 'm_w_mlp_in', 'm_w_mlp_out', 'm_final_g', 'v_w_ada', 'v_b_ada', 'v_norm1_g', 'v_w_in', 'v_q_norm_a', 'v_k_norm_a', 'v_sink_b', 'v_w_branch', 'v_w_out', 'v_norm2_g', 'v_w_mlp_in', 'v_w_mlp_out', 'v_final_g']
TWIN_OUTPUTS = ['loss', 'grad_x', 'grad_w_ada', 'grad_b_ada', 'grad_norm1_g', 'grad_w_in', 'grad_q_norm_a', 'grad_k_norm_a', 'grad_sink_b', 'grad_w_branch', 'grad_w_out', 'grad_norm2_g', 'grad_w_mlp_in', 'grad_w_mlp_out', 'grad_final_g', 'delta_w_ada', 'delta_b_ada', 'delta_norm1_g', 'delta_w_in', 'delta_q_norm_a', 'delta_k_norm_a', 'delta_sink_b', 'delta_w_branch', 'delta_w_out', 'delta_norm2_g', 'delta_w_mlp_in', 'delta_w_mlp_out', 'delta_final_g', 'new_m_w_ada', 'new_m_b_ada', 'new_m_norm1_g', 'new_m_w_in', 'new_m_q_norm_a', 'new_m_k_norm_a', 'new_m_sink_b', 'new_m_w_branch', 'new_m_w_out', 'new_m_norm2_g', 'new_m_w_mlp_in', 'new_m_w_mlp_out', 'new_m_final_g', 'new_v_w_ada', 'new_v_b_ada', 'new_v_norm1_g', 'new_v_w_in', 'new_v_q_norm_a', 'new_v_k_norm_a', 'new_v_sink_b', 'new_v_w_branch', 'new_v_w_out', 'new_v_norm2_g', 'new_v_w_mlp_in', 'new_v_w_mlp_out', 'new_v_final_g']
TWIN_LEAF_KINDS = {'loss': 'loss', 'grad_x': 'grad_x', 'grad_w_ada': 'grad_w', 'grad_b_ada': 'grad_w', 'grad_norm1_g': 'grad_w', 'grad_w_in': 'grad_w', 'grad_q_norm_a': 'grad_w', 'grad_k_norm_a': 'grad_w', 'grad_sink_b': 'grad_w', 'grad_w_branch': 'grad_w', 'grad_w_out': 'grad_w', 'grad_norm2_g': 'grad_w', 'grad_w_mlp_in': 'grad_w', 'grad_w_mlp_out': 'grad_w', 'grad_final_g': 'grad_w', 'delta_w_ada': 'delta_w', 'delta_b_ada': 'delta_w', 'delta_norm1_g': 'delta_w', 'delta_w_in': 'delta_w', 'delta_q_norm_a': 'delta_w', 'delta_k_norm_a': 'delta_w', 'delta_sink_b': 'delta_w', 'delta_w_branch': 'delta_w', 'delta_w_out': 'delta_w', 'delta_norm2_g': 'delta_w', 'delta_w_mlp_in': 'delta_w', 'delta_w_mlp_out': 'delta_w', 'delta_final_g': 'delta_w', 'new_m_w_ada': 'new_m', 'new_m_b_ada': 'new_m', 'new_m_norm1_g': 'new_m', 'new_m_w_in': 'new_m', 'new_m_q_norm_a': 'new_m', 'new_m_k_norm_a': 'new_m', 'new_m_sink_b': 'new_m', 'new_m_w_branch': 'new_m', 'new_m_w_out': 'new_m', 'new_m_norm2_g': 'new_m', 'new_m_w_mlp_in': 'new_m', 'new_m_w_mlp_out': 'new_m', 'new_m_final_g': 'new_m', 'new_v_w_ada': 'new_v', 'new_v_b_ada': 'new_v', 'new_v_norm1_g': 'new_v', 'new_v_w_in': 'new_v', 'new_v_q_norm_a': 'new_v', 'new_v_k_norm_a': 'new_v', 'new_v_sink_b': 'new_v', 'new_v_w_branch': 'new_v', 'new_v_w_out': 'new_v', 'new_v_norm2_g': 'new_v', 'new_v_w_mlp_in': 'new_v', 'new_v_w_mlp_out': 'new_v', 'new_v_final_g': 'new_v'}


def _forward(args):
    return _fwd_reference(*[args[k] for k in FWD_PARAMS])


def _output_shape():
    def fwd():
        inp = _fwd_setup_inputs(0)
        return _fwd_reference(*[inp[k] for k in FWD_PARAMS])
    out = _jax.eval_shape(fwd)
    return out.shape, out.dtype

N_MICROBATCH = 1
ADAM_LR = 0.001
ADAM_B1 = 0.9
ADAM_B2 = 0.999
ADAM_EPS = 1e-08
ADAM_WD = 0.01
ADAM_STEP = 10
PER_EXAMPLE_BATCH_AXIS = {'x': 0, 'c': 0, 'loss_target': 0}
SHARED_INPUTS = []
_WEIGHT_DTYPES = {'w_ada': _jnp.float32, 'b_ada': _jnp.float32, 'norm1_g': _jnp.float32, 'w_in': _jnp.float32, 'q_norm_a': _jnp.float32, 'k_norm_a': _jnp.float32, 'sink_b': _jnp.float32, 'w_branch': _jnp.float32, 'w_out': _jnp.float32, 'norm2_g': _jnp.float32, 'w_mlp_in': _jnp.float32, 'w_mlp_out': _jnp.float32, 'final_g': _jnp.float32}
MOMENT_SCALE = {'w_ada': 1.414729e-01, 'b_ada': 2.750959e-01, 'norm1_g': 2.340608e-02, 'w_in': 2.066660e-02, 'q_norm_a': 2.616040e-02, 'k_norm_a': 2.544528e-02, 'sink_b': 1.007738e-03, 'w_branch': 2.315872e-02, 'w_out': 3.339329e-02, 'norm2_g': 1.502583e-01, 'w_mlp_in': 7.718684e-02, 'w_mlp_out': 1.530555e-01, 'final_g': 6.491471e+01}


def _to_microbatches(a, axis):
    t = _jnp.moveaxis(a, axis, 0)
    t = t.reshape((N_MICROBATCH, t.shape[0] // N_MICROBATCH) + t.shape[1:])
    return _jnp.moveaxis(t, 1, axis + 1)


def setup_inputs(seed: int = 0) -> dict:
    inp = _fwd_setup_inputs(seed)
    key = _jax.random.fold_in(_jax.random.key(seed), 7919)
    shape, _ = _output_shape()
    out = dict(inp)
    out["loss_target"] = _jax.random.normal(_jax.random.fold_in(key, 0), shape, _jnp.float32)
    for i, name in enumerate(TWIN_WEIGHTS):
        w = inp[name].astype(_jnp.float32)
        if MOMENT_SCALE is None:
            s = _jnp.sqrt(_jnp.mean(_jnp.square(w)) + 1e-30)
        else:
            s = MOMENT_SCALE[name]
        km, kv = _jax.random.split(_jax.random.fold_in(key, i + 1))
        out[name] = w
        out["m_" + name] = s * _jax.random.normal(km, w.shape, _jnp.float32)
        out["v_" + name] = (s * s) * _jax.random.uniform(kv, w.shape, _jnp.float32, 0.5, 1.5)
    if N_MICROBATCH > 1:
        for name, axis in PER_EXAMPLE_BATCH_AXIS.items():
            out[name] = _to_microbatches(out[name], axis)
    return {'x': out['x'], 'c': out['c'], 'w_ada': out['w_ada'], 'b_ada': out['b_ada'], 'norm1_g': out['norm1_g'], 'w_in': out['w_in'], 'q_norm_a': out['q_norm_a'], 'k_norm_a': out['k_norm_a'], 'sink_b': out['sink_b'], 'w_branch': out['w_branch'], 'w_out': out['w_out'], 'norm2_g': out['norm2_g'], 'w_mlp_in': out['w_mlp_in'], 'w_mlp_out': out['w_mlp_out'], 'final_g': out['final_g'], 'loss_target': out['loss_target'], 'm_w_ada': out['m_w_ada'], 'm_b_ada': out['m_b_ada'], 'm_norm1_g': out['m_norm1_g'], 'm_w_in': out['m_w_in'], 'm_q_norm_a': out['m_q_norm_a'], 'm_k_norm_a': out['m_k_norm_a'], 'm_sink_b': out['m_sink_b'], 'm_w_branch': out['m_w_branch'], 'm_w_out': out['m_w_out'], 'm_norm2_g': out['m_norm2_g'], 'm_w_mlp_in': out['m_w_mlp_in'], 'm_w_mlp_out': out['m_w_mlp_out'], 'm_final_g': out['m_final_g'], 'v_w_ada': out['v_w_ada'], 'v_b_ada': out['v_b_ada'], 'v_norm1_g': out['v_norm1_g'], 'v_w_in': out['v_w_in'], 'v_q_norm_a': out['v_q_norm_a'], 'v_k_norm_a': out['v_k_norm_a'], 'v_sink_b': out['v_sink_b'], 'v_w_branch': out['v_w_branch'], 'v_w_out': out['v_w_out'], 'v_norm2_g': out['v_norm2_g'], 'v_w_mlp_in': out['v_w_mlp_in'], 'v_w_mlp_out': out['v_w_mlp_out'], 'v_final_g': out['v_final_g']}


def _loss(weights, diff, rest, loss_target):
    with _jax.named_scope("forward"):
        args = {**rest, TWIN_DIFF_INPUT: diff, **{k: w.astype(_WEIGHT_DTYPES[k]) for k, w in weights.items()}}
        y = _forward(args)
    with _jax.named_scope("loss_head"):
        err = _jnp.square(y.astype(_jnp.float32) - loss_target)
        return 0.5 * _jnp.sum(_jnp.mean(err, axis=-1)) if err.ndim else 0.5 * err


def _adamw(w, g, m, v):
    m = ADAM_B1 * m + (1.0 - ADAM_B1) * g
    v = ADAM_B2 * v + (1.0 - ADAM_B2) * _jnp.square(g)
    m_hat = m / (1.0 - ADAM_B1 ** ADAM_STEP)
    v_hat = v / (1.0 - ADAM_B2 ** ADAM_STEP)
    delta = -ADAM_LR * (m_hat / (_jnp.sqrt(v_hat) + ADAM_EPS) + ADAM_WD * w)
    return delta, m, v


def reference(x, c, w_ada, b_ada, norm1_g, w_in, q_norm_a, k_norm_a, sink_b, w_branch, w_out, norm2_g, w_mlp_in, w_mlp_out, final_g, loss_target, m_w_ada, m_b_ada, m_norm1_g, m_w_in, m_q_norm_a, m_k_norm_a, m_sink_b, m_w_branch, m_w_out, m_norm2_g, m_w_mlp_in, m_w_mlp_out, m_final_g, v_w_ada, v_b_ada, v_norm1_g, v_w_in, v_q_norm_a, v_k_norm_a, v_sink_b, v_w_branch, v_w_out, v_norm2_g, v_w_mlp_in, v_w_mlp_out, v_final_g):
    given = dict(x=x, c=c, w_ada=w_ada, b_ada=b_ada, norm1_g=norm1_g, w_in=w_in, q_norm_a=q_norm_a, k_norm_a=k_norm_a, sink_b=sink_b, w_branch=w_branch, w_out=w_out, norm2_g=norm2_g, w_mlp_in=w_mlp_in, w_mlp_out=w_mlp_out, final_g=final_g, loss_target=loss_target, m_w_ada=m_w_ada, m_b_ada=m_b_ada, m_norm1_g=m_norm1_g, m_w_in=m_w_in, m_q_norm_a=m_q_norm_a, m_k_norm_a=m_k_norm_a, m_sink_b=m_sink_b, m_w_branch=m_w_branch, m_w_out=m_w_out, m_norm2_g=m_norm2_g, m_w_mlp_in=m_w_mlp_in, m_w_mlp_out=m_w_mlp_out, m_final_g=m_final_g, v_w_ada=v_w_ada, v_b_ada=v_b_ada, v_norm1_g=v_norm1_g, v_w_in=v_w_in, v_q_norm_a=v_q_norm_a, v_k_norm_a=v_k_norm_a, v_sink_b=v_sink_b, v_w_branch=v_w_branch, v_w_out=v_w_out, v_norm2_g=v_norm2_g, v_w_mlp_in=v_w_mlp_in, v_w_mlp_out=v_w_mlp_out, v_final_g=v_final_g)
    weights = {n: given[n] for n in TWIN_WEIGHTS}
    shared = {n: given[n] for n in SHARED_INPUTS}
    per_example = {n: given[n] for n in ['x', 'c']}
    grad_fn = _jax.value_and_grad(_loss, argnums=(0, 1))

    def one_microbatch(ex, loss_target):
        ex = dict(ex)
        diff = ex.pop(TWIN_DIFF_INPUT)
        return grad_fn(weights, diff, {**shared, **ex}, loss_target)

    if N_MICROBATCH == 1:
        loss, (grad_w, grad_x) = one_microbatch(per_example, given["loss_target"])
    else:
        def body(carry, xs):
            loss_sum, grad_sum = carry
            l_k, (gw_k, gx_k) = one_microbatch(xs[0], xs[1])
            with _jax.named_scope("update"):
                return (loss_sum + l_k, _jax.tree.map(_jnp.add, grad_sum, gw_k)), gx_k

        init = (_jnp.zeros((), _jnp.float32), _jax.tree.map(_jnp.zeros_like, weights))
        (loss, grad_w), grad_x = _jax.lax.scan(body, init, (per_example, given["loss_target"]))
    with _jax.named_scope("update"):
        delta_w, new_m, new_v = {}, {}, {}
        for n in TWIN_WEIGHTS:
            delta_w[n], new_m[n], new_v[n] = _adamw(weights[n], grad_w[n], given["m_" + n], given["v_" + n])
    return (loss, grad_x, *[grad_w[n] for n in TWIN_WEIGHTS], *[delta_w[n] for n in TWIN_WEIGHTS],
            *[new_m[n] for n in TWIN_WEIGHTS], *[new_v[n] for n in TWIN_WEIGHTS])
```

```python
import jax
import jax.numpy as jnp
from jax import lax
from jax.experimental import pallas as pl
from jax.experimental.pallas import tpu as pltpu

F32, BF16 = jnp.float32, jnp.bfloat16
MESH = pl.DeviceIdType.MESH

D = 1024
HD = 64
NH = 8
GRP = 4
BW = 512
FF = 4096
INW = 3584
GRID_W = 64
WIN = 128
THETA = 10000.0
EPS = 1e-6
NEG = -1e30
N_DEV = 8
ROWS_PACK = 1728
SMALL_ROWS = 16
V7X_VMEM_LIMIT = 56 * 1024 * 1024

ADAM_LR, ADAM_B1, ADAM_B2, ADAM_EPS, ADAM_WD, ADAM_STEP = 0.001, 0.9, 0.999, 1e-08, 0.01, 10

NT = (((1,), (1,)), ((), ()))
TN = (((0,), (0,)), ((), ()))


def _params(n_axes, vmem=V7X_VMEM_LIMIT):
    return pltpu.CompilerParams(dimension_semantics=("arbitrary",) * n_axes, vmem_limit_bytes=vmem)


def _const(shape):
    return pl.BlockSpec(shape, lambda *_: (0,) * len(shape))


def _rows(tm, width):
    return pl.BlockSpec((tm, width), lambda i, *_: (i, 0))


def _seg_matrix(n, seg):
    r = lax.broadcasted_iota(jnp.int32, (n, n), 0) // seg
    c = lax.broadcasted_iota(jnp.int32, (n, n), 1) // seg
    return (r == c).astype(F32)


def _seg_sum(z, seg_mat):
    return jnp.dot(z, seg_mat, preferred_element_type=F32, precision=lax.Precision.HIGHEST)


def _rope(z, t_ref, sh):
    return z * t_ref[0] + pltpu.roll(z, sh, 1) * t_ref[1] + pltpu.roll(z, 128 - sh, 1) * t_ref[2]


def _rope_t(dz, t_ref, sh):
    return dz * t_ref[0] + pltpu.roll(dz * t_ref[1], 128 - sh, 1) + pltpu.roll(dz * t_ref[2], sh, 1)


def _rope_tables(S):
    t = jnp.arange(S, dtype=jnp.int32)
    lane = jnp.arange(HD)

    def build(cos, sin, first):
        t0 = cos
        t1 = jnp.where(first[None, :], 0.0, sin)
        t2 = jnp.where(first[None, :], -sin, 0.0)
        return jnp.stack([jnp.tile(a, (1, 2)) for a in (t0, t1, t2)]).astype(F32)

    inv_a = THETA ** (-jnp.arange(0, HD // 2, 2, dtype=F32) / (HD // 2))
    ar = (t // GRID_W).astype(F32)[:, None] * inv_a[None, :]
    ac = (t % GRID_W).astype(F32)[:, None] * inv_a[None, :]
    cos_a = jnp.concatenate([jnp.cos(ar), jnp.cos(ar), jnp.cos(ac), jnp.cos(ac)], axis=1)
    sin_a = jnp.concatenate([jnp.sin(ar), jnp.sin(ar), jnp.sin(ac), jnp.sin(ac)], axis=1)
    tab_a = build(cos_a, sin_a, (lane % 32) < 16)
    inv_b = THETA ** (-jnp.arange(0, HD, 2, dtype=F32) / HD)
    ab = t.astype(F32)[:, None] * inv_b[None, :]
    cos_b = jnp.concatenate([jnp.cos(ab), jnp.cos(ab)], axis=1)
    sin_b = jnp.concatenate([jnp.sin(ab), jnp.sin(ab)], axis=1)
    tab_b = build(cos_b, sin_b, lane < 32)
    return tab_a, tab_b


def _in_proj(x, modv, n1g, win, qg2, kg2, tab_a, tab_b):
    S = x.shape[0]
    tm = min(256, S)

    def body(x_ref, mod_ref, g_ref, w_ref, qg_ref, kg_ref, ta_ref, tb_ref,
             h_ref, qar_ref, kar_ref, qa_ref, ka_ref, va_ref, qb_ref, kb_ref, vb_ref, ga_ref, gb_ref):
        xt = x_ref[...]
        r = lax.rsqrt(jnp.mean(xt * xt, axis=-1, keepdims=True) + EPS)
        h = ((xt * r) * g_ref[...]) * (1.0 + mod_ref[1:2, :]) + mod_ref[0:1, :]
        hb = h.astype(BF16)
        h_ref[...] = hb
        proj = jnp.dot(hb, w_ref[...], preferred_element_type=F32)
        seg = _seg_matrix(128, HD)

        def head_norm(z, g):
            ms = _seg_sum(z * z, seg) * (1.0 / HD)
            return (z * lax.rsqrt(ms + EPS)) * g

        for p in range(4):
            z = proj[:, 128 * p:128 * p + 128]
            qar_ref[:, 128 * p:128 * p + 128] = z.astype(BF16)
            qa_ref[:, 128 * p:128 * p + 128] = (_rope(head_norm(z, qg_ref[...]), ta_ref, 16) * 0.125).astype(BF16)
            zb = proj[:, 768 + 128 * p:768 + 128 * p + 128]
            qb_ref[:, 128 * p:128 * p + 128] = (_rope(zb, tb_ref, 32) * 0.125).astype(BF16)
        z = proj[:, 512:640]
        kar_ref[...] = z.astype(BF16)
        ka_ref[...] = _rope(head_norm(z, kg_ref[...]), ta_ref, 16).astype(BF16)
        va_ref[...] = proj[:, 640:768].astype(BF16)
        kb_ref[...] = _rope(proj[:, 1280:1408], tb_ref, 32).astype(BF16)
        vb_ref[...] = proj[:, 1408:1536].astype(BF16)
        ga_ref[...] = proj[:, 1536:2560].astype(BF16)
        gb_ref[...] = proj[:, 2560:3584].astype(BF16)

    tab = pl.BlockSpec((3, tm, 128), lambda i: (0, i, 0))
    shapes = [(D, BF16), (BW, BF16), (128, BF16), (BW, BF16), (128, BF16), (128, BF16),
              (BW, BF16), (128, BF16), (128, BF16), (D, BF16), (D, BF16)]
    return pl.pallas_call(
        body, grid=(S // tm,),
        in_specs=[_rows(tm, D), _const((6, D)), _const((1, D)), _const((D, INW)), _const((1, 128)), _const((1, 128)), tab, tab],
        out_specs=[_rows(tm, w) for w, _ in shapes],
        out_shape=[jax.ShapeDtypeStruct((S, w), dt) for w, dt in shapes],
        compiler_params=_params(1), name="in_proj",
    )(x, modv, n1g, win, qg2, kg2, tab_a, tab_b)


def _attn_a_fwd(q, k, v):
    S = q.shape[0]
    tq = min(512, S)
    tk = min(512, S)
    nk = S // tk

    def body(q_ref, k_ref, v_ref, o_ref, lse_ref):
        lane8 = lax.broadcasted_iota(jnp.int32, (tq, NH), 1)
        lse_all = jnp.zeros((tq, NH), F32)
        for p in range(4):
            qp = q_ref[:, 128 * p:128 * p + 128]
            kv = p // 2
            outs = []
            for hh in range(2):
                qh = qp[:, HD * hh:HD * hh + HD]

                def step(j, carry, qh=qh, kv=kv):
                    m, l, acc = carry
                    off = pl.multiple_of(j * tk, tk)
                    kj = k_ref[pl.ds(off, tk), :][:, HD * kv:HD * kv + HD]
                    vj = v_ref[pl.ds(off, tk), :][:, HD * kv:HD * kv + HD]
                    s = lax.dot_general(qh, kj, NT, preferred_element_type=F32)
                    mn = jnp.maximum(m, jnp.max(s, axis=1, keepdims=True))
                    a = jnp.exp(m - mn)
                    pj = jnp.exp(s - mn)
                    l = a * l + jnp.sum(pj, axis=1, keepdims=True)
                    acc = a * acc + jnp.dot(pj.astype(BF16), vj, preferred_element_type=F32)
                    return mn, l, acc

                init = (jnp.full((tq, 1), NEG, F32), jnp.zeros((tq, 1), F32), jnp.zeros((tq, HD), F32))
                m, l, acc = lax.fori_loop(0, nk, step, init)
                outs.append(acc / l)
                lse_all = jnp.where(lane8 == 2 * p + hh, m + jnp.log(l), lse_all)
            o_ref[:, 128 * p:128 * p + 128] = jnp.concatenate(outs, axis=1).astype(BF16)
        lse_ref[...] = lse_all

    return pl.pallas_call(
        body, grid=(S // tq,),
        in_specs=[_rows(tq, BW), _const((S, 128)), _const((S, 128))],
        out_specs=[_rows(tq, BW), _rows(tq, NH)],
        out_shape=[jax.ShapeDtypeStruct((S, BW), BF16), jax.ShapeDtypeStruct((S, NH), F32)],
        compiler_params=_params(1), name="attn_a_fwd",
    )(q, k, v)


def _window_mask(i, tq, S):
    W = tq + 2 * WIN
    r = lax.broadcasted_iota(jnp.int32, (tq, W), 0)
    c = lax.broadcasted_iota(jnp.int32, (tq, W), 1)
    kpos = i * tq - WIN + c
    return (jnp.abs(c - WIN - r) <= WIN) & (kpos >= 0) & (kpos < S)


def _attn_b_fwd(q, kp, vp, sink):
    S = q.shape[0]
    tq = min(512, S)
    W = tq + 2 * WIN

    def body(q_ref, k_ref, v_ref, sink_ref, o_ref, lse_ref):
        i = pl.program_id(0)
        off = pl.multiple_of(i * tq, tq)
        valid = _window_mask(i, tq, S)
        kw = k_ref[pl.ds(off, W), :]
        vw = v_ref[pl.ds(off, W), :]
        lane8 = lax.broadcasted_iota(jnp.int32, (tq, NH), 1)
        lse_all = jnp.zeros((tq, NH), F32)
        for p in range(4):
            qp = q_ref[:, 128 * p:128 * p + 128]
            kv = p // 2
            kh = kw[:, HD * kv:HD * kv + HD]
            vh = vw[:, HD * kv:HD * kv + HD]
            outs = []
            for hh in range(2):
                h = 2 * p + hh
                s = lax.dot_general(qp[:, HD * hh:HD * hh + HD], kh, NT, preferred_element_type=F32)
                s = jnp.where(valid, s, NEG)
                sk = sink_ref[:, h:h + 1]
                m = jnp.maximum(jnp.max(s, axis=1, keepdims=True), sk)
                pm = jnp.exp(s - m)
                l = jnp.sum(pm, axis=1, keepdims=True) + jnp.exp(sk - m)
                outs.append(jnp.dot(pm.astype(BF16), vh, preferred_element_type=F32) / l)
                lse_all = jnp.where(lane8 == h, m + jnp.log(l), lse_all)
            o_ref[:, 128 * p:128 * p + 128] = jnp.concatenate(outs, axis=1).astype(BF16)
        lse_ref[...] = lse_all

    return pl.pallas_call(
        body, grid=(S // tq,),
        in_specs=[_rows(tq, BW), _const((S + 2 * WIN, 128)), _const((S + 2 * WIN, 128)), _const((1, NH))],
        out_specs=[_rows(tq, BW), _rows(tq, NH)],
        out_shape=[jax.ShapeDtypeStruct((S, BW), BF16), jax.ShapeDtypeStruct((S, NH), F32)],
        compiler_params=_params(1), name="attn_b_fwd",
    )(q, kp, vp, sink)


def _merge_out(ya, yb, ga, gb, x, modv, wb, wout):
    S = x.shape[0]
    tm = min(256, S)

    def body(ya_ref, yb_ref, ga_ref, gb_ref, x_ref, mod_ref, wb_ref, wo_ref, x1_ref, mg_ref, ua_ref, ub_ref):
        ua = jnp.dot(ya_ref[...], wb_ref[0], preferred_element_type=F32)
        ub = jnp.dot(yb_ref[...], wb_ref[1], preferred_element_type=F32)
        merged = jax.nn.sigmoid(ga_ref[...].astype(F32)) * ua + jax.nn.sigmoid(gb_ref[...].astype(F32)) * ub
        mb = merged.astype(BF16)
        ua_ref[...] = ua.astype(BF16)
        ub_ref[...] = ub.astype(BF16)
        mg_ref[...] = mb
        x1_ref[...] = x_ref[...] + mod_ref[2:3, :] * jnp.dot(mb, wo_ref[...], preferred_element_type=F32)

    return pl.pallas_call(
        body, grid=(S // tm,),
        in_specs=[_rows(tm, BW), _rows(tm, BW), _rows(tm, D), _rows(tm, D), _rows(tm, D), _const((6, D)),
                  _const((2, BW, D)), _const((D, D))],
        out_specs=[_rows(tm, D)] * 4,
        out_shape=[jax.ShapeDtypeStruct((S, D), F32)] + [jax.ShapeDtypeStruct((S, D), BF16)] * 3,
        compiler_params=_params(1), name="merge_out",
    )(ya, yb, ga, gb, x, modv, wb, wout)


def _mlp_fwd(x1, modv, n2g, wmi, wmo, fg, target):
    S = x1.shape[0]
    tm = min(512, S)
    tf = wmi.shape[2]
    nj = wmi.shape[0]

    def body(x1_ref, mod_ref, g_ref, wi_ref, wo_ref, fg_ref, t_ref, h2_ref, hp_ref, dx2_ref, st_ref, acc_ref):
        i, j = pl.program_id(0), pl.program_id(1)

        @pl.when(j == 0)
        def _():
            xt = x1_ref[...]
            r = lax.rsqrt(jnp.mean(xt * xt, axis=-1, keepdims=True) + EPS)
            h2 = ((xt * r) * g_ref[...]) * (1.0 + mod_ref[4:5, :]) + mod_ref[3:4, :]
            h2_ref[...] = h2.astype(BF16)
            acc_ref[...] = jnp.zeros_like(acc_ref)

        @pl.when((i == 0) & (j == 0))
        def _():
            st_ref[...] = jnp.zeros_like(st_ref)

        hp = jnp.dot(h2_ref[...], wi_ref[...], preferred_element_type=F32)
        hp_ref[...] = hp.astype(BF16)
        hid = jnp.square(jnp.maximum(hp, 0.0))
        acc_ref[...] += jnp.dot(hid.astype(BF16), wo_ref[...], preferred_element_type=F32)

        @pl.when(j == nj - 1)
        def _():
            x2 = x1_ref[...] + mod_ref[5:6, :] * acc_ref[...]
            r3 = lax.rsqrt(jnp.mean(x2 * x2, axis=-1, keepdims=True) + EPS)
            xn = x2 * r3
            err = xn * fg_ref[...] - t_ref[...]
            dy = err * (1.0 / D)
            gy = dy * fg_ref[...]
            dx2_ref[...] = r3 * (gy - xn * jnp.mean(gy * xn, axis=-1, keepdims=True))
            st_ref[0:1, :] += jnp.sum(dy * xn, axis=0, keepdims=True)
            st_ref[1:2, :] += jnp.sum(err * err, axis=0, keepdims=True) * (0.5 / D)

    return pl.pallas_call(
        body, grid=(S // tm, nj),
        in_specs=[pl.BlockSpec((tm, D), lambda i, j: (i, 0)), _const((6, D)), _const((1, D)),
                  pl.BlockSpec((None, D, tf), lambda i, j: (j, 0, 0)), pl.BlockSpec((None, tf, D), lambda i, j: (j, 0, 0)),
                  _const((1, D)), pl.BlockSpec((tm, D), lambda i, j: (i, 0))],
        out_specs=[pl.BlockSpec((tm, D), lambda i, j: (i, 0)), pl.BlockSpec((tm, tf), lambda i, j: (i, j)),
                   pl.BlockSpec((tm, D), lambda i, j: (i, 0)), _const((8, D))],
        out_shape=[jax.ShapeDtypeStruct((S, D), BF16), jax.ShapeDtypeStruct((S, nj * tf), BF16),
                   jax.ShapeDtypeStruct((S, D), F32), jax.ShapeDtypeStruct((8, D), F32)],
        scratch_shapes=[pltpu.VMEM((tm, D), F32)],
        compiler_params=_params(2), name="mlp_fwd",
    )(x1, modv, n2g, wmi, wmo, fg, target)


def _mlp_bwd(dx2, x1, hp, modv, n2g, wmi, wmo):
    S = x1.shape[0]
    tm = min(512, S)
    tf = wmi.shape[2]
    nj = wmi.shape[0]

    def body(dx2_ref, x1_ref, hp_ref, mod_ref, g_ref, wi_ref, wo_ref, dhp_ref, dx1_ref, st_ref, dmo_ref, acc_ref):
        i, j = pl.program_id(0), pl.program_id(1)

        @pl.when(j == 0)
        def _():
            dmo_ref[...] = (mod_ref[5:6, :] * dx2_ref[...]).astype(BF16)
            acc_ref[...] = jnp.zeros_like(acc_ref)

        @pl.when((i == 0) & (j == 0))
        def _():
            st_ref[...] = jnp.zeros_like(st_ref)

        dhid = lax.dot_general(dmo_ref[...], wo_ref[...], NT, preferred_element_type=F32)
        dhp = (dhid * (2.0 * jnp.maximum(hp_ref[...].astype(F32), 0.0))).astype(BF16)
        dhp_ref[...] = dhp
        acc_ref[...] += lax.dot_general(dhp, wi_ref[...], NT, preferred_element_type=F32)

        @pl.when(j == nj - 1)
        def _():
            dh2 = acc_ref[...]
            xt = x1_ref[...]
            r = lax.rsqrt(jnp.mean(xt * xt, axis=-1, keepdims=True) + EPS)
            xn = xt * r
            st_ref[0:1, :] += jnp.sum(dh2, axis=0, keepdims=True)
            st_ref[1:2, :] += jnp.sum(dh2 * xn, axis=0, keepdims=True)
            dxn = dh2 * (g_ref[...] * (1.0 + mod_ref[4:5, :]))
            dx1_ref[...] = dx2_ref[...] + r * (dxn - xn * jnp.mean(dxn * xn, axis=-1, keepdims=True))

    return pl.pallas_call(
        body, grid=(S // tm, nj),
        in_specs=[pl.BlockSpec((tm, D), lambda i, j: (i, 0)), pl.BlockSpec((tm, D), lambda i, j: (i, 0)),
                  pl.BlockSpec((tm, tf), lambda i, j: (i, j)), _const((6, D)), _const((1, D)),
                  pl.BlockSpec((None, D, tf), lambda i, j: (j, 0, 0)), pl.BlockSpec((None, tf, D), lambda i, j: (j, 0, 0))],
        out_specs=[pl.BlockSpec((tm, tf), lambda i, j: (i, j)), pl.BlockSpec((tm, D), lambda i, j: (i, 0)), _const((8, D))],
        out_shape=[jax.ShapeDtypeStruct((S, nj * tf), BF16), jax.ShapeDtypeStruct((S, D), F32), jax.ShapeDtypeStruct((8, D), F32)],
        scratch_shapes=[pltpu.VMEM((tm, D), BF16), pltpu.VMEM((tm, D), F32)],
        compiler_params=_params(2), name="mlp_bwd",
    )(dx2, x1, hp, modv, n2g, wmi, wmo)


def _tn_matmul(a, b, tk, tn, name, relu_sq=False, dev_major=False):
    S, K = a.shape
    N = b.shape[1]
    ts = min(512, S)
    ns = S // ts

    def body(a_ref, b_ref, o_ref):
        @pl.when(pl.program_id(2) == 0)
        def _():
            o_ref[...] = jnp.zeros_like(o_ref)

        at = a_ref[...]
        if relu_sq:
            at = jnp.square(jnp.maximum(at.astype(F32), 0.0)).astype(BF16)
        o_ref[...] += lax.dot_general(at, b_ref[...].astype(BF16), TN, preferred_element_type=F32)

    if dev_major:
        out_spec = pl.BlockSpec((None, tk, tn), lambda k, n, s: (n, k, 0))
        out_shape = jax.ShapeDtypeStruct((N // tn, K, tn), F32)
    else:
        out_spec = pl.BlockSpec((tk, tn), lambda k, n, s: (k, n))
        out_shape = jax.ShapeDtypeStruct((K, N), F32)
    return pl.pallas_call(
        body, grid=(K // tk, N // tn, ns),
        in_specs=[pl.BlockSpec((ts, tk), lambda k, n, s: (s, k)), pl.BlockSpec((ts, tn), lambda k, n, s: (s, n))],
        out_specs=out_spec, out_shape=out_shape,
        compiler_params=_params(3), name=name,
    )(a, b)


def _scale_gate(m, w, g, row, name):
    K = m.shape[0]
    tk = min(512, K)

    def body(m_ref, w_ref, mod_ref, dw_ref, dg_ref):
        @pl.when(pl.program_id(0) == 0)
        def _():
            dg_ref[...] = jnp.zeros_like(dg_ref)

        mt = m_ref[...]
        dw_ref[...] = mt * mod_ref[row:row + 1, :]
        dg_ref[0:1, :] += jnp.sum(mt * w_ref[...].astype(F32), axis=0, keepdims=True)

    return pl.pallas_call(
        body, grid=(K // tk,),
        in_specs=[_rows(tk, D), _rows(tk, D), _const((6, D))],
        out_specs=[_rows(tk, D), _const((8, D))],
        out_shape=[jax.ShapeDtypeStruct((K, D), F32), jax.ShapeDtypeStruct((8, D), F32)],
        compiler_params=_params(1), name=name,
    )(m, w, g)


def _merge_bwd(dx1, modv, ga, gb, ua, ub, ya, yb, wb, wout):
    S = dx1.shape[0]
    tm = min(256, S)

    def body(dx1_ref, mod_ref, ga_ref, gb_ref, ua_ref, ub_ref, ya_ref, yb_ref, wb_ref, wo_ref,
             dua_ref, dub_ref, dga_ref, dgb_ref, dya_ref, dyb_ref, dla_ref, dlb_ref):
        dao = (mod_ref[2:3, :] * dx1_ref[...]).astype(BF16)
        dm = lax.dot_general(dao, wo_ref[...], NT, preferred_element_type=F32)
        r = lax.broadcasted_iota(jnp.int32, (BW, NH), 0) // HD
        c = lax.broadcasted_iota(jnp.int32, (BW, NH), 1)
        head_of = (r == c).astype(F32)
        for br, (g_ref, u_ref, y_ref, du_ref, dg_ref, dy_ref, dl_ref) in enumerate((
                (ga_ref, ua_ref, ya_ref, dua_ref, dga_ref, dya_ref, dla_ref),
                (gb_ref, ub_ref, yb_ref, dub_ref, dgb_ref, dyb_ref, dlb_ref))):
            sg = jax.nn.sigmoid(g_ref[...].astype(F32))
            du = (dm * sg).astype(BF16)
            du_ref[...] = du
            dg_ref[...] = (dm * u_ref[...].astype(F32) * sg * (1.0 - sg)).astype(BF16)
            dy = lax.dot_general(du, wb_ref[br], NT, preferred_element_type=F32)
            dyb16 = dy.astype(BF16)
            dy_ref[...] = dyb16
            dl_ref[...] = jnp.dot(dyb16.astype(F32) * y_ref[...].astype(F32), head_of,
                                  preferred_element_type=F32, precision=lax.Precision.HIGHEST)

    return pl.pallas_call(
        body, grid=(S // tm,),
        in_specs=[_rows(tm, D), _const((6, D)), _rows(tm, D), _rows(tm, D), _rows(tm, D), _rows(tm, D),
                  _rows(tm, BW), _rows(tm, BW), _const((2, BW, D)), _const((D, D))],
        out_specs=[_rows(tm, D)] * 4 + [_rows(tm, BW)] * 2 + [_rows(tm, NH)] * 2,
        out_shape=[jax.ShapeDtypeStruct((S, D), BF16)] * 4 + [jax.ShapeDtypeStruct((S, BW), BF16)] * 2
        + [jax.ShapeDtypeStruct((S, NH), F32)] * 2,
        compiler_params=_params(1), name="merge_bwd",
    )(dx1, modv, ga, gb, ua, ub, ya, yb, wb, wout)


def _attn_a_bwd(q, k, v, do, lse, delta):
    S = q.shape[0]
    tq = min(512, S)
    tk = min(512, S)
    nq, nk = S // tq, S // tk

    def body(q_hbm, do_hbm, lse_hbm, dl_hbm, k_ref, v_ref, dq_hbm, dk_ref, dv_ref, q_sc, do_sc, lse_sc, dl_sc, dq_sc, sem):
        j = pl.program_id(0)

        @pl.when(j == 0)
        def _():
            copies = [pltpu.make_async_copy(src, dst, sem.at[n]) for n, (src, dst) in enumerate(
                ((q_hbm, q_sc), (do_hbm, do_sc), (lse_hbm, lse_sc), (dl_hbm, dl_sc)))]
            for cp in copies:
                cp.start()
            dq_sc[...] = jnp.zeros_like(dq_sc)
            for cp in copies:
                cp.wait()

        dks, dvs = [], []
        for kv in range(2):
            kj = k_ref[:, HD * kv:HD * kv + HD]
            vj = v_ref[:, HD * kv:HD * kv + HD]
            carry = (jnp.zeros((tk, HD), F32), jnp.zeros((tk, HD), F32))
            for pp in range(2):
                p = 2 * kv + pp

                def step(i, carry, p=p, kj=kj, vj=vj):
                    dk, dv = carry
                    off = pl.multiple_of(i * tq, tq)
                    qp = q_sc[pl.ds(off, tq), 128 * p:128 * p + 128]
                    dop = do_sc[pl.ds(off, tq), 128 * p:128 * p + 128]
                    lse_i = lse_sc[pl.ds(off, tq), :]
                    dl_i = dl_sc[pl.ds(off, tq), :]
                    dqs = []
                    for hh in range(2):
                        h = 2 * p + hh
                        qh = qp[:, HD * hh:HD * hh + HD]
                        doh = dop[:, HD * hh:HD * hh + HD]
                        s = lax.dot_general(qh, kj, NT, preferred_element_type=F32)
                        pm = jnp.exp(s - lse_i[:, h:h + 1])
                        dp = lax.dot_general(doh, vj, NT, preferred_element_type=F32)
                        ds = (pm * (dp - dl_i[:, h:h + 1])).astype(BF16)
                        dv = dv + lax.dot_general(pm.astype(BF16), doh, TN, preferred_element_type=F32)
                        dk = dk + lax.dot_general(ds, qh, TN, preferred_element_type=F32)
                        dqs.append(jnp.dot(ds, kj, preferred_element_type=F32))
                    dq_sc[pl.ds(off, tq), 128 * p:128 * p + 128] += jnp.concatenate(dqs, axis=1)
                    return dk, dv

                carry = lax.fori_loop(0, nq, step, carry)
            dks.append(carry[0])
            dvs.append(carry[1])
        dk_ref[...] = jnp.concatenate(dks, axis=1)
        dv_ref[...] = jnp.concatenate(dvs, axis=1)

        @pl.when(j == nk - 1)
        def _():
            out = pltpu.make_async_copy(dq_sc, dq_hbm, sem.at[0])
            out.start()
            out.wait()

    any_spec = pl.BlockSpec(memory_space=pl.ANY)
    return pl.pallas_call(
        body, grid=(nk,),
        in_specs=[any_spec, any_spec, any_spec, any_spec, _rows(tk, 128), _rows(tk, 128)],
        out_specs=[any_spec, _rows(tk, 128), _rows(tk, 128)],
        out_shape=[jax.ShapeDtypeStruct((S, BW), F32), jax.ShapeDtypeStruct((S, 128), F32), jax.ShapeDtypeStruct((S, 128), F32)],
        scratch_shapes=[pltpu.VMEM((S, BW), BF16), pltpu.VMEM((S, BW), BF16), pltpu.VMEM((S, NH), F32), pltpu.VMEM((S, NH), F32),
                        pltpu.VMEM((S, BW), F32), pltpu.SemaphoreType.DMA((4,))],
        compiler_params=_params(1), name="attn_a_bwd",
    )(q, do, lse, delta, k, v)


def _attn_b_bwd(q, kp, vp, sink, do, lse, delta):
    S = q.shape[0]
    tq = min(512, S)
    W = tq + 2 * WIN
    nq = S // tq

    def body(q_ref, k_ref, v_ref, sink_ref, do_ref, lse_ref, dl_ref, dq_ref, dk_hbm, dv_hbm, ds_ref, dk_sc, dv_sc, sem):
        i = pl.program_id(0)

        @pl.when(i == 0)
        def _():
            dk_sc[...] = jnp.zeros_like(dk_sc)
            dv_sc[...] = jnp.zeros_like(dv_sc)
            ds_ref[...] = jnp.zeros_like(ds_ref)

        off = pl.multiple_of(i * tq, tq)
        valid = _window_mask(i, tq, S)
        kw = k_ref[pl.ds(off, W), :]
        vw = v_ref[pl.ds(off, W), :]
        lse_i = lse_ref[...]
        dl_i = dl_ref[...]
        dks, dvs = [], []
        for kv in range(2):
            kh = kw[:, HD * kv:HD * kv + HD]
            vh = vw[:, HD * kv:HD * kv + HD]
            dk = jnp.zeros((W, HD), F32)
            dv = jnp.zeros((W, HD), F32)
            for pp in range(2):
                p = 2 * kv + pp
                qp = q_ref[:, 128 * p:128 * p + 128]
                dop = do_ref[:, 128 * p:128 * p + 128]
                dqs = []
                for hh in range(2):
                    h = 2 * p + hh
                    qh = qp[:, HD * hh:HD * hh + HD]
                    doh = dop[:, HD * hh:HD * hh + HD]
                    s = jnp.where(valid, lax.dot_general(qh, kh, NT, preferred_element_type=F32), NEG)
                    pm = jnp.exp(s - lse_i[:, h:h + 1])
                    dp = lax.dot_general(doh, vh, NT, preferred_element_type=F32)
                    ds = (pm * (dp - dl_i[:, h:h + 1])).astype(BF16)
                    dv = dv + lax.dot_general(pm.astype(BF16), doh, TN, preferred_element_type=F32)
                    dk = dk + lax.dot_general(ds, qh, TN, preferred_element_type=F32)
                    dqs.append(jnp.dot(ds, kh, preferred_element_type=F32))
                dq_ref[:, 128 * p:128 * p + 128] = jnp.concatenate(dqs, axis=1)
            dks.append(dk)
            dvs.append(dv)
        dk_sc[pl.ds(off, W), :] += jnp.concatenate(dks, axis=1)
        dv_sc[pl.ds(off, W), :] += jnp.concatenate(dvs, axis=1)
        psd = jnp.exp(sink_ref[...] - lse_i) * dl_i
        r = lax.broadcasted_iota(jnp.int32, (NH, 128), 0)
        c = lax.broadcasted_iota(jnp.int32, (NH, 128), 1)
        row = jnp.dot(jnp.sum(psd, axis=0, keepdims=True), (r == c).astype(F32),
                      preferred_element_type=F32, precision=lax.Precision.HIGHEST)
        ds_ref[...] -= jnp.broadcast_to(row, (8, 128))

        @pl.when(i == nq - 1)
        def _():
            c1 = pltpu.make_async_copy(dk_sc, dk_hbm, sem.at[0])
            c2 = pltpu.make_async_copy(dv_sc, dv_hbm, sem.at[1])
            c1.start()
            c2.start()
            c1.wait()
            c2.wait()

    any_spec = pl.BlockSpec(memory_space=pl.ANY)
    return pl.pallas_call(
        body, grid=(nq,),
        in_specs=[_rows(tq, BW), _const((S + 2 * WIN, 128)), _const((S + 2 * WIN, 128)), _const((1, NH)),
                  _rows(tq, BW), _rows(tq, NH), _rows(tq, NH)],
        out_specs=[_rows(tq, BW), any_spec, any_spec, _const((8, 128))],
        out_shape=[jax.ShapeDtypeStruct((S, BW), F32), jax.ShapeDtypeStruct((S + 2 * WIN, 128), F32),
                   jax.ShapeDtypeStruct((S + 2 * WIN, 128), F32), jax.ShapeDtypeStruct((8, 128), F32)],
        scratch_shapes=[pltpu.VMEM((S + 2 * WIN, 128), F32), pltpu.VMEM((S + 2 * WIN, 128), F32), pltpu.SemaphoreType.DMA((2,))],
        compiler_params=_params(1), name="attn_b_bwd",
    )(q, kp, vp, sink, do, lse, delta)


def _qk_bwd(dqa, dka, dva, dqb, dkb, dvb, qar, kar, qg2, kg2, tab_a, tab_b, dga, dgb):
    S = dqa.shape[0]
    tm = min(256, S)

    def body(dqa_ref, dka_ref, dva_ref, dqb_ref, dkb_ref, dvb_ref, qar_ref, kar_ref, qg_ref, kg_ref, ta_ref, tb_ref,
             dga_ref, dgb_ref, dp_ref, st_ref):
        @pl.when(pl.program_id(0) == 0)
        def _():
            st_ref[...] = jnp.zeros_like(st_ref)

        seg = _seg_matrix(128, HD)

        def norm_bwd(dz_rot, raw, g):
            dzn = _rope_t(dz_rot, ta_ref, 16)
            raw = raw.astype(F32)
            rr = lax.rsqrt(_seg_sum(raw * raw, seg) * (1.0 / HD) + EPS)
            zhat = raw * rr
            dzh = dzn * g
            draw = rr * (dzh - zhat * (_seg_sum(dzh * zhat, seg) * (1.0 / HD)))
            return draw, jnp.sum(dzn * zhat, axis=0, keepdims=True)

        gq = jnp.zeros((1, 128), F32)
        for p in range(4):
            sl = slice(128 * p, 128 * p + 128)
            draw, gsum = norm_bwd(dqa_ref[:, sl] * 0.125, qar_ref[:, sl], qg_ref[...])
            gq = gq + gsum
            dp_ref[:, sl] = draw.astype(BF16)
            dp_ref[:, 768 + 128 * p:768 + 128 * p + 128] = _rope_t(dqb_ref[:, sl] * 0.125, tb_ref, 32).astype(BF16)
        draw, gk = norm_bwd(dka_ref[...], kar_ref[...], kg_ref[...])
        dp_ref[:, 512:640] = draw.astype(BF16)
        dp_ref[:, 640:768] = dva_ref[...].astype(BF16)
        dp_ref[:, 1280:1408] = _rope_t(dkb_ref[...], tb_ref, 32).astype(BF16)
        dp_ref[:, 1408:1536] = dvb_ref[...].astype(BF16)
        dp_ref[:, 1536:2560] = dga_ref[...]
        dp_ref[:, 2560:3584] = dgb_ref[...]
        st_ref[0:1, :] += gq
        st_ref[1:2, :] += gk

    tab = pl.BlockSpec((3, tm, 128), lambda i: (0, i, 0))
    return pl.pallas_call(
        body, grid=(S // tm,),
        in_specs=[_rows(tm, BW), _rows(tm, 128), _rows(tm, 128), _rows(tm, BW), _rows(tm, 128), _rows(tm, 128),
                  _rows(tm, BW), _rows(tm, 128), _const((1, 128)), _const((1, 128)), tab, tab, _rows(tm, D), _rows(tm, D)],
        out_specs=[_rows(tm, INW), _const((8, 128))],
        out_shape=[jax.ShapeDtypeStruct((S, INW), BF16), jax.ShapeDtypeStruct((8, 128), F32)],
        compiler_params=_params(1), name="qk_bwd",
    )(dqa, dka, dva, dqb, dkb, dvb, qar, kar, qg2, kg2, tab_a, tab_b, dga, dgb)


def _in_bwd(dproj, win, x, dx1, modv, n1g):
    S = x.shape[0]
    tm = min(256, S)

    def body(dp_ref, w_ref, x_ref, dx1_ref, mod_ref, g_ref, gx_ref, st_ref):
        @pl.when(pl.program_id(0) == 0)
        def _():
            st_ref[...] = jnp.zeros_like(st_ref)

        dh = lax.dot_general(dp_ref[...], w_ref[...], NT, preferred_element_type=F32)
        xt = x_ref[...]
        r = lax.rsqrt(jnp.mean(xt * xt, axis=-1, keepdims=True) + EPS)
        xn = xt * r
        st_ref[0:1, :] += jnp.sum(dh, axis=0, keepdims=True)
        st_ref[1:2, :] += jnp.sum(dh * xn, axis=0, keepdims=True)
        dxn = dh * (g_ref[...] * (1.0 + mod_ref[1:2, :]))
        gx_ref[...] = dx1_ref[...] + r * (dxn - xn * jnp.mean(dxn * xn, axis=-1, keepdims=True))

    return pl.pallas_call(
        body, grid=(S // tm,),
        in_specs=[_rows(tm, INW), _const((D, INW)), _rows(tm, D), _rows(tm, D), _const((6, D)), _const((1, D))],
        out_specs=[_rows(tm, D), _const((8, D))],
        out_shape=[jax.ShapeDtypeStruct((S, D), F32), jax.ShapeDtypeStruct((8, D), F32)],
        compiler_params=_params(1), name="in_bwd",
    )(dproj, win, x, dx1, modv, n1g)


def _pack_small(st1, st2, stf, dg1, dg2, stqk, dsink, modv, n1g, n2g):
    def body(st1_ref, st2_ref, stf_ref, dg1_ref, dg2_ref, qk_ref, ds_ref, mod_ref, g1_ref, g2_ref, o_ref):
        a1, b1 = st1_ref[0:1, :], st1_ref[1:2, :]
        a2, b2 = st2_ref[0:1, :], st2_ref[1:2, :]
        r = lax.broadcasted_iota(jnp.int32, (128, D), 0)
        c = lax.broadcasted_iota(jnp.int32, (128, D), 1)
        fold_q = (c == r % HD).astype(F32)
        fold_k = (c == HD + r % HD).astype(F32)
        keep = (c == r).astype(F32)

        def place(v, sel):
            return jnp.dot(v, sel, preferred_element_type=F32, precision=lax.Precision.HIGHEST)

        loss = jnp.sum(stf_ref[1:2, :], axis=1, keepdims=True)
        lane = lax.broadcasted_iota(jnp.int32, (1, D), 1)
        rows = [a1, g1_ref[...] * b1, dg1_ref[0:1, :], a2, g2_ref[...] * b2, dg2_ref[0:1, :],
                (1.0 + mod_ref[1:2, :]) * b1, (1.0 + mod_ref[4:5, :]) * b2, stf_ref[0:1, :],
                place(qk_ref[0:1, :], fold_q) + place(qk_ref[1:2, :], fold_k),
                place(ds_ref[0:1, :], keep),
                jnp.where(lane == 0, loss, 0.0)]
        rows += [jnp.zeros((1, D), F32)] * (SMALL_ROWS - len(rows))
        for n, v in enumerate(rows):
            o_ref[n:n + 1, :] = v

    return pl.pallas_call(
        body, out_shape=jax.ShapeDtypeStruct((SMALL_ROWS, D), F32),
        compiler_params=pltpu.CompilerParams(vmem_limit_bytes=V7X_VMEM_LIMIT), name="pack_small",
    )(st1, st2, stf, dg1, dg2, stqk, dsink, modv, n1g, n2g)


def _wada_grad(silu_all, dmod_cols):
    def body(a_ref, b_ref, o_ref):
        o_ref[...] = lax.dot_general(a_ref[...], b_ref[...], TN, preferred_element_type=F32, precision=lax.Precision.HIGHEST)

    return pl.pallas_call(
        body, out_shape=jax.ShapeDtypeStruct((D, dmod_cols.shape[1]), F32),
        compiler_params=pltpu.CompilerParams(vmem_limit_bytes=V7X_VMEM_LIMIT), name="wada_grad",
    )(silu_all, dmod_cols)


def _adamw_sum(parts, w, m, v, name):
    R, C = w.shape
    tr = R if R <= 512 else next(t for t in (512, 432, 256, 216, 128, 64, 8) if R % t == 0)
    n = len(parts)
    dyn = [idx for _, idx in parts if idx is not None and not isinstance(idx, int)]
    b1c = 1.0 - ADAM_B1 ** ADAM_STEP
    b2c = 1.0 - ADAM_B2 ** ADAM_STEP

    def body(*refs):
        refs = refs[len(dyn):]
        g = refs[0][...]
        for k in range(1, n):
            g = g + refs[k][...]
        w_ref, m_ref, v_ref, g_out, d_out, m_out, v_out = refs[n:]
        mn = ADAM_B1 * m_ref[...] + (1.0 - ADAM_B1) * g
        vn = ADAM_B2 * v_ref[...] + (1.0 - ADAM_B2) * jnp.square(g)
        g_out[...] = g
        m_out[...] = mn
        v_out[...] = vn
        d_out[...] = -ADAM_LR * ((mn / b1c) / (jnp.sqrt(vn / b2c) + ADAM_EPS) + ADAM_WD * w_ref[...])

    in_specs = []
    nd = 0
    for a, idx in parts:
        if idx is None:
            in_specs.append(pl.BlockSpec((tr, C), lambda i, *s: (i, 0)))
        elif isinstance(idx, int):
            in_specs.append(pl.BlockSpec((None, tr, C), lambda i, *s, idx=idx: (idx, i, 0)))
        else:
            in_specs.append(pl.BlockSpec((None, tr, C), lambda i, *s, nd=nd: (s[nd][0], i, 0)))
            nd += 1
    blk = pl.BlockSpec((tr, C), lambda i, *s: (i, 0))
    grid_spec = pltpu.PrefetchScalarGridSpec(
        num_scalar_prefetch=len(dyn), grid=(R // tr,), in_specs=in_specs + [blk] * 3, out_specs=[blk] * 4)
    return pl.pallas_call(
        body, grid_spec=grid_spec, out_shape=[jax.ShapeDtypeStruct((R, C), F32)] * 4,
        compiler_params=_params(1), name=name,
    )(*dyn, *[a for a, _ in parts], w, m, v)


def _pair_sum(g4, recv, core):
    tr = 216

    def body(c_ref, a_ref, b_ref, o_ref):
        o_ref[...] = a_ref[...] + b_ref[...]

    grid_spec = pltpu.PrefetchScalarGridSpec(
        num_scalar_prefetch=1, grid=(4, ROWS_PACK // tr),
        in_specs=[pl.BlockSpec((None, None, tr, D), lambda ch, i, c: (ch, c[0], i, 0)),
                  pl.BlockSpec((None, tr, D), lambda ch, i, c: (ch, i, 0))],
        out_specs=pl.BlockSpec((None, tr, D), lambda ch, i, c: (ch, i, 0)))
    return pl.pallas_call(
        body, grid_spec=grid_spec, out_shape=jax.ShapeDtypeStruct((4, ROWS_PACK, D), F32),
        compiler_params=_params(2), name="pair_sum",
    )(core, g4, recv)


def _me():
    return lax.axis_index("x"), lax.axis_index("y"), lax.axis_index("c")


def _peer(k):
    x, y, c = _me()
    return (x ^ ((k >> 2) & 1), y ^ ((k >> 1) & 1), c ^ (k & 1))


def _ada_exchange(c_row, w_ada, b_rows):
    NW = w_ada.shape[1]

    def body(c_ref, w_ref, b_ref, sall_ref, mod_ref, src_ref, mp_ref, send1, recv1, send2, recv2):
        x, y, c = _me()
        me = 4 * x + 2 * y + c
        cv = c_ref[...]
        src_ref[...] = jnp.broadcast_to(cv * jax.nn.sigmoid(cv), (8, D))
        mine = pl.ds(pl.multiple_of(me * 8, 8), 8)
        sall_ref[mine, :] = src_ref[...]
        sends = [pltpu.make_async_remote_copy(src_ref, sall_ref.at[mine, :], send1.at[k - 1], recv1.at[k - 1],
                                              device_id=_peer(k), device_id_type=MESH) for k in range(1, N_DEV)]
        for cp in sends:
            cp.start()
        for k in range(1, N_DEV):
            theirs = pl.ds(pl.multiple_of((me ^ k) * 8, 8), 8)
            pltpu.make_async_remote_copy(src_ref, sall_ref.at[theirs, :], send1.at[k - 1], recv1.at[k - 1],
                                         device_id=_peer(k), device_id_type=MESH).wait_recv()
        for cp in sends:
            cp.wait_send()
        mp_ref[...] = jnp.dot(sall_ref[...], w_ref[...], preferred_element_type=F32, precision=lax.Precision.HIGHEST)
        mod_ref[mine, :] = mp_ref[mine, :] + b_ref[mine, :]
        sends = []
        for k in range(1, N_DEV):
            theirs = pl.ds(pl.multiple_of((me ^ k) * 8, 8), 8)
            sends.append(pltpu.make_async_remote_copy(mp_ref.at[theirs, :], mod_ref.at[mine, :], send2.at[k - 1], recv2.at[k - 1],
                                                      device_id=_peer(k), device_id_type=MESH))
        for cp in sends:
            cp.start()
        for k in range(1, N_DEV):
            theirs = pl.ds(pl.multiple_of((me ^ k) * 8, 8), 8)
            pltpu.make_async_remote_copy(mp_ref.at[mine, :], mod_ref.at[theirs, :], send2.at[k - 1], recv2.at[k - 1],
                                         device_id=_peer(k), device_id_type=MESH).wait_recv()
            mod_ref[theirs, :] = mod_ref[theirs, :] + b_ref[theirs, :]
        for cp in sends:
            cp.wait_send()

    vm = pl.BlockSpec(memory_space=pltpu.VMEM)
    return pl.pallas_call(
        body, in_specs=[vm, vm, vm], out_specs=[vm, vm],
        out_shape=[jax.ShapeDtypeStruct((8 * N_DEV, D), F32), jax.ShapeDtypeStruct((8 * N_DEV, NW), F32)],
        scratch_shapes=[pltpu.VMEM((8, D), F32), pltpu.VMEM((8 * N_DEV, NW), F32)]
        + [pltpu.SemaphoreType.DMA((N_DEV - 1,))] * 4,
        compiler_params=pltpu.CompilerParams(vmem_limit_bytes=V7X_VMEM_LIMIT), name="ada_exchange",
    )(c_row, w_ada, b_rows)


def _weight_gather(shard):
    def body(x_ref, out_ref, send_sems, recv_sems, local_sem):
        x, y, c = _me()
        me, sibling = (x, y, c), (x, y, 1 - c)
        chips = [(1 - x, y), (x, 1 - y), (1 - x, 1 - y)]

        def slot(px, py, pc):
            return out_ref.at[4 * px + 2 * py + pc]

        def copy(k, block, to, src=None):
            return pltpu.make_async_remote_copy(
                src_ref=slot(*block) if src is None else src, dst_ref=slot(*block),
                send_sem=send_sems.at[k], recv_sem=recv_sems.at[k], device_id=to, device_id_type=MESH)

        mine = pltpu.make_async_copy(x_ref, slot(*me), local_sem)
        mine.start()
        first = [copy(0, me, sibling, src=x_ref)]
        first += [copy(1 + j, me, (*chip, c), src=x_ref) for j, chip in enumerate(chips)]
        for cp in first:
            cp.start()
        passed = [copy(4 + j, (*chip, c), sibling) for j, chip in enumerate(chips)]
        for j, chip in enumerate(chips):
            copy(1 + j, (*chip, c), me).wait_recv()
            passed[j].start()
        copy(0, sibling, me).wait_recv()
        for j, chip in enumerate(chips):
            copy(4 + j, (*chip, 1 - c), me).wait_recv()
        for cp in first + passed:
            cp.wait_send()
        mine.wait()

    any_spec = pl.BlockSpec(memory_space=pl.ANY)
    return pl.pallas_call(
        body, in_specs=[any_spec], out_specs=any_spec,
        out_shape=jax.ShapeDtypeStruct((N_DEV,) + shard.shape, shard.dtype),
        scratch_shapes=[pltpu.SemaphoreType.DMA((7,)), pltpu.SemaphoreType.DMA((7,)), pltpu.SemaphoreType.DMA],
        name="weight_gather",
    )(shard)


def _grad_rs_sibling(g4):
    def body(g_ref, recv_ref, send_sems, recv_sems):
        x, y, c = _me()
        sibling = (x, y, 1 - c)
        copies = [pltpu.make_async_remote_copy(g_ref.at[ch, 1 - c], recv_ref.at[ch], send_sems.at[ch], recv_sems.at[ch],
                                               device_id=sibling, device_id_type=MESH) for ch in range(4)]
        for cp in copies:
            cp.start()
        for cp in copies:
            cp.wait_recv()
        for cp in copies:
            cp.wait_send()

    any_spec = pl.BlockSpec(memory_space=pl.ANY)
    return pl.pallas_call(
        body, in_specs=[any_spec], out_specs=any_spec,
        out_shape=jax.ShapeDtypeStruct((4,) + g4.shape[2:], g4.dtype),
        scratch_shapes=[pltpu.SemaphoreType.DMA((4,)), pltpu.SemaphoreType.DMA((4,))],
        name="grad_rs_sibling",
    )(g4)


def _grad_rs_chips(t4):
    def body(t_ref, recv_ref, send_sems, recv_sems):
        x, y, c = _me()
        chip = 2 * x + y
        copies = []
        for k in range(1, 4):
            to = (x ^ (k >> 1), y ^ (k & 1), c)
            copies.append(pltpu.make_async_remote_copy(t_ref.at[chip ^ k], recv_ref.at[k - 1], send_sems.at[k - 1], recv_sems.at[k - 1],
                                                       device_id=to, device_id_type=MESH))
        for cp in copies:
            cp.start()
        for cp in copies:
            cp.wait_recv()
        for cp in copies:
            cp.wait_send()

    any_spec = pl.BlockSpec(memory_space=pl.ANY)
    return pl.pallas_call(
        body, in_specs=[any_spec], out_specs=any_spec,
        out_shape=jax.ShapeDtypeStruct((3,) + t4.shape[1:], t4.dtype),
        scratch_shapes=[pltpu.SemaphoreType.DMA((3,)), pltpu.SemaphoreType.DMA((3,))],
        name="grad_rs_chips",
    )(t4)


def _small_gather(block):
    def body(b_ref, out_ref, send_sems, recv_sems):
        x, y, c = _me()
        me = 4 * x + 2 * y + c
        out_ref[me] = b_ref[...]
        sends = [pltpu.make_async_remote_copy(b_ref, out_ref.at[me], send_sems.at[k - 1], recv_sems.at[k - 1],
                                              device_id=_peer(k), device_id_type=MESH) for k in range(1, N_DEV)]
        for cp in sends:
            cp.start()
        for k in range(1, N_DEV):
            pltpu.make_async_remote_copy(b_ref, out_ref.at[me ^ k], send_sems.at[k - 1], recv_sems.at[k - 1],
                                         device_id=_peer(k), device_id_type=MESH).wait_recv()
        for cp in sends:
            cp.wait_send()

    vm = pl.BlockSpec(memory_space=pltpu.VMEM)
    return pl.pallas_call(
        body, in_specs=[vm], out_specs=vm,
        out_shape=jax.ShapeDtypeStruct((N_DEV,) + block.shape, block.dtype),
        scratch_shapes=[pltpu.SemaphoreType.DMA((N_DEV - 1,)), pltpu.SemaphoreType.DMA((N_DEV - 1,))],
        name="small_gather",
    )(block)


def _pack_shards(w_in, w_branch, w_out, w_mlp_in, w_mlp_out):
    return jnp.concatenate([w_in.reshape(-1, D), w_branch.reshape(-1, D), w_out.reshape(-1, D),
                            w_mlp_in.reshape(-1, D), w_mlp_out.reshape(-1, D)], axis=0)


def _unpack_shards(p):
    return (p[0:448].reshape(1, D, 448), p[448:576].reshape(1, 2, BW, 128), p[576:704].reshape(1, 128, D),
            p[704:1216].reshape(1, D, 512), p[1216:1728].reshape(1, 512, D))


def _pack_small_params(b_ada, n1, n2, fg, qn, kn, sink):
    z = jnp.zeros((SMALL_ROWS, D), F32)
    z = z.at[0:6].set(b_ada.reshape(6, D)).at[6].set(n1.reshape(D)).at[7].set(n2.reshape(D)).at[8].set(fg.reshape(D))
    z = z.at[9, 0:HD].set(qn.reshape(HD)).at[9, HD:2 * HD].set(kn.reshape(HD)).at[10, 0:NH].set(sink.reshape(NH))
    return z


def _unpack_small(p):
    return (p[0:6].reshape(1, 6 * D), p[6].reshape(1, D), p[9, 0:HD].reshape(1, HD), p[9, HD:2 * HD].reshape(1, HD),
            p[10, 0:NH].reshape(1, NH), p[7].reshape(1, D), p[8].reshape(D))


def kernel(x, c, w_ada, b_ada, norm1_g, w_in, q_norm_a, k_norm_a, sink_b, w_branch, w_out, norm2_g, w_mlp_in, w_mlp_out, final_g, loss_target, m_w_ada, m_b_ada, m_norm1_g, m_w_in, m_q_norm_a, m_k_norm_a, m_sink_b, m_w_branch, m_w_out, m_norm2_g, m_w_mlp_in, m_w_mlp_out, m_final_g, v_w_ada, v_b_ada, v_norm1_g, v_w_in, v_q_norm_a, v_k_norm_a, v_sink_b, v_w_branch, v_w_out, v_norm2_g, v_w_mlp_in, v_w_mlp_out, v_final_g):
    S = x.shape[1]
    xs = x.reshape(S, D)
    tgt = loss_target.reshape(S, D)
    ax, ay, ac = lax.axis_index("x"), lax.axis_index("y"), lax.axis_index("c")
    me = 4 * ax + 2 * ay + ac
    NW = w_ada.shape[2]

    silu64, mod64 = _ada_exchange(c.reshape(1, D), w_ada.reshape(D, NW),
                                  jnp.repeat(b_ada.reshape(N_DEV, NW), 8, axis=0))
    silu_all = silu64[0::8]
    modv = mod64[0::8].reshape(6, D)

    gathered = _weight_gather(_pack_shards(w_in[0], w_branch[0], w_out[0], w_mlp_in[0], w_mlp_out[0]).astype(BF16))
    win = gathered[:, 0:448].reshape(N_DEV, D, 448).transpose(1, 0, 2).reshape(D, INW)
    wb = gathered[:, 448:576].reshape(N_DEV, 2, BW, 128).transpose(1, 2, 0, 3).reshape(2, BW, D)
    wout = gathered[:, 576:704].reshape(D, D)
    wmi = gathered[:, 704:1216].reshape(N_DEV, D, 512)
    wmo = gathered[:, 1216:1728]

    tab_a, tab_b = _rope_tables(S)
    qg2 = jnp.tile(q_norm_a.reshape(1, HD), (1, 2))
    kg2 = jnp.tile(k_norm_a.reshape(1, HD), (1, 2))
    n1g = norm1_g.reshape(1, D)
    n2g = norm2_g.reshape(1, D)
    fg = final_g.reshape(1, D)
    sink = sink_b.reshape(1, NH)

    h, qar, kar, qa, ka, va, qb, kb, vb, ga, gb = _in_proj(xs, modv, n1g, win, qg2, kg2, tab_a, tab_b)
    ya, lse_a = _attn_a_fwd(qa, ka, va)
    pad = ((WIN, WIN), (0, 0))
    kbp, vbp = jnp.pad(kb, pad), jnp.pad(vb, pad)
    yb, lse_b = _attn_b_fwd(qb, kbp, vbp, sink)
    x1, merged, ua, ub = _merge_out(ya, yb, ga, gb, xs, modv, wb, wout)
    h2, hp, dx2, stf = _mlp_fwd(x1, modv, n2g, wmi, wmo, fg, tgt)

    dhp, dx1, st2 = _mlp_bwd(dx2, x1, hp, modv, n2g, wmi, wmo)
    m2 = _tn_matmul(hp, dx2, 2048, D, "dw_mlp_out", relu_sq=True)
    g_wmo, dg2 = _scale_gate(m2, wmo.reshape(FF, D), modv, 5, "gate2_grad")
    g_wmi = _tn_matmul(h2, dhp, D, 512, "dw_mlp_in", dev_major=True)
    dua, dub, dga, dgb, dya, dyb, dl_a, dl_b = _merge_bwd(dx1, modv, ga, gb, ua, ub, ya, yb, wb, wout)
    m1 = _tn_matmul(merged, dx1, D, D, "dw_out")
    g_wout, dg1 = _scale_gate(m1, wout, modv, 2, "gate1_grad")
    g_wb0 = _tn_matmul(ya, dua, BW, D, "dw_branch_a")
    g_wb1 = _tn_matmul(yb, dub, BW, D, "dw_branch_b")
    dqa, dka, dva = _attn_a_bwd(qa, ka, va, dya, lse_a, dl_a)
    dqb, dkbp, dvbp, dsink = _attn_b_bwd(qb, kbp, vbp, sink, dyb, lse_b, dl_b)
    dproj, stqk = _qk_bwd(dqa, dka, dva, dqb, dkbp[WIN:WIN + S], dvbp[WIN:WIN + S], qar, kar, qg2, kg2, tab_a, tab_b, dga, dgb)
    grad_x, st1 = _in_bwd(dproj, win, xs, dx1, modv, n1g)
    g_win = _tn_matmul(h, dproj, D, 896, "dw_in")

    g_pack = jnp.concatenate([
        g_win.reshape(D, N_DEV, 448).transpose(1, 0, 2).reshape(N_DEV, 448, D),
        jnp.stack([g_wb0, g_wb1]).reshape(2, BW, N_DEV, 128).transpose(2, 0, 1, 3).reshape(N_DEV, 128, D),
        g_wout.reshape(N_DEV, 128, D),
        g_wmi.reshape(N_DEV, 512, D),
        g_wmo.reshape(N_DEV, 512, D)], axis=1)
    g4 = g_pack.reshape(4, 2, ROWS_PACK, D)
    from_sibling = _grad_rs_sibling(g4)
    t4 = _pair_sum(g4, from_sibling, ac.reshape(1).astype(jnp.int32))
    from_chips = _grad_rs_chips(t4)
    chip = (2 * ax + ay).reshape(1).astype(jnp.int32)
    w_p = _pack_shards(w_in[0], w_branch[0], w_out[0], w_mlp_in[0], w_mlp_out[0])
    m_p = _pack_shards(m_w_in[0], m_w_branch[0], m_w_out[0], m_w_mlp_in[0], m_w_mlp_out[0])
    v_p = _pack_shards(v_w_in[0], v_w_branch[0], v_w_out[0], v_w_mlp_in[0], v_w_mlp_out[0])
    big = _adamw_sum([(t4, chip), (from_chips, 0), (from_chips, 1), (from_chips, 2)], w_p, m_p, v_p, "adamw_big")
    big = [_unpack_shards(a) for a in big]

    small = _pack_small(st1, st2, stf, dg1, dg2, stqk, dsink, modv, n1g, n2g)
    small_all = _small_gather(small)
    sw = _pack_small_params(b_ada, norm1_g, norm2_g, final_g, q_norm_a, k_norm_a, sink_b)
    sm = _pack_small_params(m_b_ada, m_norm1_g, m_norm2_g, m_final_g, m_q_norm_a, m_k_norm_a, m_sink_b)
    sv = _pack_small_params(v_b_ada, v_norm1_g, v_norm2_g, v_final_g, v_q_norm_a, v_k_norm_a, v_sink_b)
    sm_out = _adamw_sum([(small_all, k) for k in range(N_DEV)], sw, sm, sv, "adamw_small")
    loss = sm_out[0][11, 0]
    sm_out = [_unpack_small(a) for a in sm_out]

    dmod_all = small_all[:, 0:6, :].reshape(N_DEV, 6 * D)
    dmod_cols = lax.dynamic_slice_in_dim(dmod_all, me * NW, NW, axis=1)
    g_wada = _wada_grad(silu_all, dmod_cols)
    ada = _adamw_sum([(g_wada, None)], w_ada.reshape(D, NW), m_w_ada.reshape(D, NW), v_w_ada.reshape(D, NW), "adamw_ada")
    ada = [a.reshape(1, D, NW) for a in ada]

    def leaves(k):
        b_, n1_, qn_, kn_, sk_, n2_, fg_ = sm_out[k]
        wi_, wbr_, wo_, wmi_, wmo_ = big[k]
        return [ada[k], b_, n1_, wi_, qn_, kn_, sk_, wbr_, wo_, n2_, wmi_, wmo_, fg_]

    return (loss, grad_x.reshape(1, S, D), *leaves(0), *leaves(1), *leaves(2), *leaves(3))
```

```python
import jax
import jax.numpy as jnp
from jax import lax
from jax.experimental import pallas as pl
from jax.experimental.pallas import tpu as pltpu

F32, BF16 = jnp.float32, jnp.bfloat16
MESH = pl.DeviceIdType.MESH

D = 1024
HD = 64
NH = 8
GRP = 4
BW = 512
FF = 4096
INW = 3584
GRID_W = 64
WIN = 128
THETA = 10000.0
EPS = 1e-6
NEG = -1e30
N_DEV = 8
LOG2E = 1.4426950408889634
LN2 = 0.6931471805599453
QA_SCALE = 0.125 * LOG2E
ROWS_PACK = 1728
SMALL_ROWS = 16
V7X_VMEM_LIMIT = 56 * 1024 * 1024

ADAM_LR, ADAM_B1, ADAM_B2, ADAM_EPS, ADAM_WD, ADAM_STEP = 0.001, 0.9, 0.999, 1e-08, 0.01, 10

NT = (((1,), (1,)), ((), ()))
TN = (((0,), (0,)), ((), ()))


def _params(n_axes, vmem=V7X_VMEM_LIMIT):
    return pltpu.CompilerParams(dimension_semantics=("arbitrary",) * n_axes, vmem_limit_bytes=vmem)


def _const(shape):
    return pl.BlockSpec(shape, lambda *_: (0,) * len(shape))


def _rows(tm, width):
    return pl.BlockSpec((tm, width), lambda i, *_: (i, 0))


def _seg_matrix(n, seg):
    r = lax.broadcasted_iota(jnp.int32, (n, n), 0) // seg
    c = lax.broadcasted_iota(jnp.int32, (n, n), 1) // seg
    return (r == c).astype(F32)


def _seg_sum(z, seg_mat):
    return jnp.dot(z, seg_mat, preferred_element_type=F32, precision=lax.Precision.HIGHEST)


def _rope(z, t_ref, sh):
    return z * t_ref[0] + pltpu.roll(z, sh, 1) * t_ref[1] + pltpu.roll(z, 128 - sh, 1) * t_ref[2]


def _rope_t(dz, t_ref, sh):
    return dz * t_ref[0] + pltpu.roll(dz * t_ref[1], 128 - sh, 1) + pltpu.roll(dz * t_ref[2], sh, 1)


def _rope_tables(S):
    t = jnp.arange(S, dtype=jnp.int32)
    lane = jnp.arange(HD)

    def build(cos, sin, first):
        t0 = cos
        t1 = jnp.where(first[None, :], 0.0, sin)
        t2 = jnp.where(first[None, :], -sin, 0.0)
        return jnp.stack([jnp.tile(a, (1, 2)) for a in (t0, t1, t2)]).astype(F32)

    inv_a = THETA ** (-jnp.arange(0, HD // 2, 2, dtype=F32) / (HD // 2))
    ar = (t // GRID_W).astype(F32)[:, None] * inv_a[None, :]
    ac = (t % GRID_W).astype(F32)[:, None] * inv_a[None, :]
    cos_a = jnp.concatenate([jnp.cos(ar), jnp.cos(ar), jnp.cos(ac), jnp.cos(ac)], axis=1)
    sin_a = jnp.concatenate([jnp.sin(ar), jnp.sin(ar), jnp.sin(ac), jnp.sin(ac)], axis=1)
    tab_a = build(cos_a, sin_a, (lane % 32) < 16)
    inv_b = THETA ** (-jnp.arange(0, HD, 2, dtype=F32) / HD)
    ab = t.astype(F32)[:, None] * inv_b[None, :]
    cos_b = jnp.concatenate([jnp.cos(ab), jnp.cos(ab)], axis=1)
    sin_b = jnp.concatenate([jnp.sin(ab), jnp.sin(ab)], axis=1)
    tab_b = build(cos_b, sin_b, lane < 32)
    return tab_a, tab_b


def _in_proj(x, modv, n1g, win, qg2, kg2, tab_a, tab_b):
    S = x.shape[0]
    tm = min(256, S)

    def body(x_ref, mod_ref, g_ref, w_ref, qg_ref, kg_ref, ta_ref, tb_ref,
             h_ref, qar_ref, kar_ref, qa_ref, ka_ref, va_ref, qb_ref, kb_ref, vb_ref, ga_ref, gb_ref):
        xt = x_ref[...]
        r = lax.rsqrt(jnp.mean(xt * xt, axis=-1, keepdims=True) + EPS)
        h = ((xt * r) * g_ref[...]) * (1.0 + mod_ref[1:2, :]) + mod_ref[0:1, :]
        hb = h.astype(BF16)
        h_ref[...] = hb
        proj = jnp.dot(hb, w_ref[...], preferred_element_type=F32)
        seg = _seg_matrix(128, HD)

        def head_norm(z, g):
            ms = _seg_sum(z * z, seg) * (1.0 / HD)
            return (z * lax.rsqrt(ms + EPS)) * g

        for p in range(4):
            z = proj[:, 128 * p:128 * p + 128]
            qar_ref[:, 128 * p:128 * p + 128] = z.astype(BF16)
            qa_ref[:, 128 * p:128 * p + 128] = (_rope(head_norm(z, qg_ref[...]), ta_ref, 16) * QA_SCALE).astype(BF16)
            zb = proj[:, 768 + 128 * p:768 + 128 * p + 128]
            qb_ref[:, 128 * p:128 * p + 128] = (_rope(zb, tb_ref, 32) * 0.125).astype(BF16)
        z = proj[:, 512:640]
        kar_ref[...] = z.astype(BF16)
        ka_ref[...] = _rope(head_norm(z, kg_ref[...]), ta_ref, 16).astype(BF16)
        va_ref[...] = proj[:, 640:768].astype(BF16)
        kb_ref[...] = _rope(proj[:, 1280:1408], tb_ref, 32).astype(BF16)
        vb_ref[...] = proj[:, 1408:1536].astype(BF16)
        ga_ref[...] = proj[:, 1536:2560].astype(BF16)
        gb_ref[...] = proj[:, 2560:3584].astype(BF16)

    tab = pl.BlockSpec((3, tm, 128), lambda i: (0, i, 0))
    shapes = [(D, BF16), (BW, BF16), (128, BF16), (BW, BF16), (128, BF16), (128, BF16),
              (BW, BF16), (128, BF16), (128, BF16), (D, BF16), (D, BF16)]
    return pl.pallas_call(
        body, grid=(S // tm,),
        in_specs=[_rows(tm, D), _const((6, D)), _const((1, D)), _const((D, INW)), _const((1, 128)), _const((1, 128)), tab, tab],
        out_specs=[_rows(tm, w) for w, _ in shapes],
        out_shape=[jax.ShapeDtypeStruct((S, w), dt) for w, dt in shapes],
        compiler_params=_params(1), name="in_proj",
    )(x, modv, n1g, win, qg2, kg2, tab_a, tab_b)


def _attn_a_fwd(q, k, v):
    S = q.shape[0]
    tq = min(512, S)
    tk = min(512, S)
    nk = S // tk

    def body(q_ref, k_ref, v_ref, o_ref, lse_ref):
        lane8 = lax.broadcasted_iota(jnp.int32, (tq, NH), 1)
        lse_all = jnp.zeros((tq, NH), F32)
        ones = jnp.ones((tk, HD), BF16)
        for kv in range(2):
            qs = []
            for pp in range(2):
                qp = q_ref[:, 128 * (2 * kv + pp):128 * (2 * kv + pp) + 128]
                qs += [qp[:, :HD], qp[:, HD:]]

            def step(j, carry, qs=qs, kv=kv):
                off = pl.multiple_of(j * tk, tk)
                kj = k_ref[pl.ds(off, tk), :][:, HD * kv:HD * kv + HD]
                vj = v_ref[pl.ds(off, tk), :][:, HD * kv:HD * kv + HD]
                v1 = jnp.concatenate([vj, ones], axis=1)
                new = []
                for t in range(GRP):
                    m, acc = carry[2 * t], carry[2 * t + 1]
                    s = lax.dot_general(qs[t], kj, NT, preferred_element_type=F32)
                    mn = jnp.maximum(m, jnp.max(s, axis=1, keepdims=True))
                    pj = jnp.exp2(s - mn)
                    acc = jnp.exp2(m - mn) * acc + jnp.dot(pj.astype(BF16), v1, preferred_element_type=F32)
                    new += [mn, acc]
                return tuple(new)

            init = (jnp.full((tq, 1), NEG, F32), jnp.zeros((tq, 128), F32)) * GRP
            res = lax.fori_loop(0, nk, step, init)
            outs = []
            for t in range(GRP):
                m, acc = res[2 * t], res[2 * t + 1]
                l = acc[:, HD:HD + 1]
                outs.append(acc[:, :HD] / l)
                lse_all = jnp.where(lane8 == GRP * kv + t, m + jnp.log2(l), lse_all)
            o_ref[:, 256 * kv:256 * kv + 256] = jnp.concatenate(outs, axis=1).astype(BF16)
        lse_ref[...] = lse_all

    return pl.pallas_call(
        body, grid=(S // tq,),
        in_specs=[_rows(tq, BW), _const((S, 128)), _const((S, 128))],
        out_specs=[_rows(tq, BW), _rows(tq, NH)],
        out_shape=[jax.ShapeDtypeStruct((S, BW), BF16), jax.ShapeDtypeStruct((S, NH), F32)],
        compiler_params=_params(1), name="attn_a_fwd",
    )(q, k, v)


def _window_mask(i, tq, S):
    W = tq + 2 * WIN
    r = lax.broadcasted_iota(jnp.int32, (tq, W), 0)
    c = lax.broadcasted_iota(jnp.int32, (tq, W), 1)
    kpos = i * tq - WIN + c
    return (jnp.abs(c - WIN - r) <= WIN) & (kpos >= 0) & (kpos < S)


def _attn_b_fwd(q, kp, vp, sink):
    S = q.shape[0]
    tq = min(512, S)
    W = tq + 2 * WIN

    def body(q_ref, k_ref, v_ref, sink_ref, o_ref, lse_ref):
        i = pl.program_id(0)
        off = pl.multiple_of(i * tq, tq)
        valid = _window_mask(i, tq, S)
        kw = k_ref[pl.ds(off, W), :]
        vw = v_ref[pl.ds(off, W), :]
        lane8 = lax.broadcasted_iota(jnp.int32, (tq, NH), 1)
        lse_all = jnp.zeros((tq, NH), F32)
        for p in range(4):
            qp = q_ref[:, 128 * p:128 * p + 128]
            kv = p // 2
            kh = kw[:, HD * kv:HD * kv + HD]
            vh = vw[:, HD * kv:HD * kv + HD]
            outs = []
            for hh in range(2):
                h = 2 * p + hh
                s = lax.dot_general(qp[:, HD * hh:HD * hh + HD], kh, NT, preferred_element_type=F32)
                s = jnp.where(valid, s, NEG)
                sk = sink_ref[:, h:h + 1]
                m = jnp.maximum(jnp.max(s, axis=1, keepdims=True), sk)
                pm = jnp.exp(s - m)
                l = jnp.sum(pm, axis=1, keepdims=True) + jnp.exp(sk - m)
                outs.append(jnp.dot(pm.astype(BF16), vh, preferred_element_type=F32) / l)
                lse_all = jnp.where(lane8 == h, m + jnp.log(l), lse_all)
            o_ref[:, 128 * p:128 * p + 128] = jnp.concatenate(outs, axis=1).astype(BF16)
        lse_ref[...] = lse_all

    return pl.pallas_call(
        body, grid=(S // tq,),
        in_specs=[_rows(tq, BW), _const((S + 2 * WIN, 128)), _const((S + 2 * WIN, 128)), _const((1, NH))],
        out_specs=[_rows(tq, BW), _rows(tq, NH)],
        out_shape=[jax.ShapeDtypeStruct((S, BW), BF16), jax.ShapeDtypeStruct((S, NH), F32)],
        compiler_params=_params(1), name="attn_b_fwd",
    )(q, kp, vp, sink)


def _merge_out(ya, yb, ga, gb, x, modv, wb, wout):
    S = x.shape[0]
    tm = min(256, S)

    def body(ya_ref, yb_ref, ga_ref, gb_ref, x_ref, mod_ref, wb_ref, wo_ref, x1_ref, mg_ref, ua_ref, ub_ref):
        ua = jnp.dot(ya_ref[...], wb_ref[0], preferred_element_type=F32)
        ub = jnp.dot(yb_ref[...], wb_ref[1], preferred_element_type=F32)
        merged = jax.nn.sigmoid(ga_ref[...].astype(F32)) * ua + jax.nn.sigmoid(gb_ref[...].astype(F32)) * ub
        mb = merged.astype(BF16)
        ua_ref[...] = ua.astype(BF16)
        ub_ref[...] = ub.astype(BF16)
        mg_ref[...] = mb
        x1_ref[...] = x_ref[...] + mod_ref[2:3, :] * jnp.dot(mb, wo_ref[...], preferred_element_type=F32)

    return pl.pallas_call(
        body, grid=(S // tm,),
        in_specs=[_rows(tm, BW), _rows(tm, BW), _rows(tm, D), _rows(tm, D), _rows(tm, D), _const((6, D)),
                  _const((2, BW, D)), _const((D, D))],
        out_specs=[_rows(tm, D)] * 4,
        out_shape=[jax.ShapeDtypeStruct((S, D), F32)] + [jax.ShapeDtypeStruct((S, D), BF16)] * 3,
        compiler_params=_params(1), name="merge_out",
    )(ya, yb, ga, gb, x, modv, wb, wout)


def _mlp_fwd(x1, modv, n2g, wmi, wmo, fg, target):
    S = x1.shape[0]
    tm = min(512, S)
    tf = wmi.shape[2]
    nj = wmi.shape[0]

    def body(x1_ref, mod_ref, g_ref, wi_ref, wo_ref, fg_ref, t_ref, h2_ref, hp_ref, dx2_ref, st_ref, acc_ref):
        i, j = pl.program_id(0), pl.program_id(1)

        @pl.when(j == 0)
        def _():
            xt = x1_ref[...]
            r = lax.rsqrt(jnp.mean(xt * xt, axis=-1, keepdims=True) + EPS)
            h2 = ((xt * r) * g_ref[...]) * (1.0 + mod_ref[4:5, :]) + mod_ref[3:4, :]
            h2_ref[...] = h2.astype(BF16)
            acc_ref[...] = jnp.zeros_like(acc_ref)

        @pl.when((i == 0) & (j == 0))
        def _():
            st_ref[...] = jnp.zeros_like(st_ref)

        hp = jnp.dot(h2_ref[...], wi_ref[...], preferred_element_type=F32)
        hp_ref[...] = hp.astype(BF16)
        hid = jnp.square(jnp.maximum(hp, 0.0))
        acc_ref[...] += jnp.dot(hid.astype(BF16), wo_ref[...], preferred_element_type=F32)

        @pl.when(j == nj - 1)
        def _():
            x2 = x1_ref[...] + mod_ref[5:6, :] * acc_ref[...]
            r3 = lax.rsqrt(jnp.mean(x2 * x2, axis=-1, keepdims=True) + EPS)
            xn = x2 * r3
            err = xn * fg_ref[...] - t_ref[...]
            dy = err * (1.0 / D)
            gy = dy * fg_ref[...]
            dx2_ref[...] = r3 * (gy - xn * jnp.mean(gy * xn, axis=-1, keepdims=True))
            st_ref[0:1, :] += jnp.sum(dy * xn, axis=0, keepdims=True)
            st_ref[1:2, :] += jnp.sum(err * err, axis=0, keepdims=True) * (0.5 / D)

    return pl.pallas_call(
        body, grid=(S // tm, nj),
        in_specs=[pl.BlockSpec((tm, D), lambda i, j: (i, 0)), _const((6, D)), _const((1, D)),
                  pl.BlockSpec((None, D, tf), lambda i, j: (j, 0, 0)), pl.BlockSpec((None, tf, D), lambda i, j: (j, 0, 0)),
                  _const((1, D)), pl.BlockSpec((tm, D), lambda i, j: (i, 0))],
        out_specs=[pl.BlockSpec((tm, D), lambda i, j: (i, 0)), pl.BlockSpec((tm, tf), lambda i, j: (i, j)),
                   pl.BlockSpec((tm, D), lambda i, j: (i, 0)), _const((8, D))],
        out_shape=[jax.ShapeDtypeStruct((S, D), BF16), jax.ShapeDtypeStruct((S, nj * tf), BF16),
                   jax.ShapeDtypeStruct((S, D), F32), jax.ShapeDtypeStruct((8, D), F32)],
        scratch_shapes=[pltpu.VMEM((tm, D), F32)],
        compiler_params=_params(2), name="mlp_fwd",
    )(x1, modv, n2g, wmi, wmo, fg, target)


def _mlp_bwd(dx2, x1, hp, modv, n2g, wmi, wmo):
    S = x1.shape[0]
    tm = min(512, S)
    tf = wmi.shape[2]
    nj = wmi.shape[0]

    def body(dx2_ref, x1_ref, hp_ref, mod_ref, g_ref, wi_ref, wo_ref, dhp_ref, dx1_ref, st_ref, dmo_ref, acc_ref):
        i, j = pl.program_id(0), pl.program_id(1)

        @pl.when(j == 0)
        def _():
            dmo_ref[...] = (mod_ref[5:6, :] * dx2_ref[...]).astype(BF16)
            acc_ref[...] = jnp.zeros_like(acc_ref)

        @pl.when((i == 0) & (j == 0))
        def _():
            st_ref[...] = jnp.zeros_like(st_ref)

        dhid = lax.dot_general(dmo_ref[...], wo_ref[...], NT, preferred_element_type=F32)
        dhp = (dhid * (2.0 * jnp.maximum(hp_ref[...].astype(F32), 0.0))).astype(BF16)
        dhp_ref[...] = dhp
        acc_ref[...] += lax.dot_general(dhp, wi_ref[...], NT, preferred_element_type=F32)

        @pl.when(j == nj - 1)
        def _():
            dh2 = acc_ref[...]
            xt = x1_ref[...]
            r = lax.rsqrt(jnp.mean(xt * xt, axis=-1, keepdims=True) + EPS)
            xn = xt * r
            st_ref[0:1, :] += jnp.sum(dh2, axis=0, keepdims=True)
            st_ref[1:2, :] += jnp.sum(dh2 * xn, axis=0, keepdims=True)
            dxn = dh2 * (g_ref[...] * (1.0 + mod_ref[4:5, :]))
            dx1_ref[...] = dx2_ref[...] + r * (dxn - xn * jnp.mean(dxn * xn, axis=-1, keepdims=True))

    return pl.pallas_call(
        body, grid=(S // tm, nj),
        in_specs=[pl.BlockSpec((tm, D), lambda i, j: (i, 0)), pl.BlockSpec((tm, D), lambda i, j: (i, 0)),
                  pl.BlockSpec((tm, tf), lambda i, j: (i, j)), _const((6, D)), _const((1, D)),
                  pl.BlockSpec((None, D, tf), lambda i, j: (j, 0, 0)), pl.BlockSpec((None, tf, D), lambda i, j: (j, 0, 0))],
        out_specs=[pl.BlockSpec((tm, tf), lambda i, j: (i, j)), pl.BlockSpec((tm, D), lambda i, j: (i, 0)), _const((8, D))],
        out_shape=[jax.ShapeDtypeStruct((S, nj * tf), BF16), jax.ShapeDtypeStruct((S, D), F32), jax.ShapeDtypeStruct((8, D), F32)],
        scratch_shapes=[pltpu.VMEM((tm, D), BF16), pltpu.VMEM((tm, D), F32)],
        compiler_params=_params(2), name="mlp_bwd",
    )(dx2, x1, hp, modv, n2g, wmi, wmo)


def _tn_matmul(a, b, tk, tn, name, relu_sq=False, dev_major=False):
    S, K = a.shape
    N = b.shape[1]
    ts = min(512, S)
    ns = S // ts

    def body(a_ref, b_ref, o_ref):
        @pl.when(pl.program_id(2) == 0)
        def _():
            o_ref[...] = jnp.zeros_like(o_ref)

        at = a_ref[...]
        if relu_sq:
            at = jnp.square(jnp.maximum(at.astype(F32), 0.0)).astype(BF16)
        o_ref[...] += lax.dot_general(at, b_ref[...].astype(BF16), TN, preferred_element_type=F32)

    if dev_major:
        out_spec = pl.BlockSpec((None, tk, tn), lambda k, n, s: (n, k, 0))
        out_shape = jax.ShapeDtypeStruct((N // tn, K, tn), F32)
    else:
        out_spec = pl.BlockSpec((tk, tn), lambda k, n, s: (k, n))
        out_shape = jax.ShapeDtypeStruct((K, N), F32)
    return pl.pallas_call(
        body, grid=(K // tk, N // tn, ns),
        in_specs=[pl.BlockSpec((ts, tk), lambda k, n, s: (s, k)), pl.BlockSpec((ts, tn), lambda k, n, s: (s, n))],
        out_specs=out_spec, out_shape=out_shape,
        compiler_params=_params(3), name=name,
    )(a, b)


def _scale_gate(m, w, g, row, name):
    K = m.shape[0]
    tk = min(512, K)

    def body(m_ref, w_ref, mod_ref, dw_ref, dg_ref):
        @pl.when(pl.program_id(0) == 0)
        def _():
            dg_ref[...] = jnp.zeros_like(dg_ref)

        mt = m_ref[...]
        dw_ref[...] = mt * mod_ref[row:row + 1, :]
        dg_ref[0:1, :] += jnp.sum(mt * w_ref[...].astype(F32), axis=0, keepdims=True)

    return pl.pallas_call(
        body, grid=(K // tk,),
        in_specs=[_rows(tk, D), _rows(tk, D), _const((6, D))],
        out_specs=[_rows(tk, D), _const((8, D))],
        out_shape=[jax.ShapeDtypeStruct((K, D), F32), jax.ShapeDtypeStruct((8, D), F32)],
        compiler_params=_params(1), name=name,
    )(m, w, g)


def _merge_bwd(dx1, modv, ga, gb, ua, ub, ya, yb, wb, wout):
    S = dx1.shape[0]
    tm = min(256, S)

    def body(dx1_ref, mod_ref, ga_ref, gb_ref, ua_ref, ub_ref, ya_ref, yb_ref, wb_ref, wo_ref,
             dua_ref, dub_ref, dga_ref, dgb_ref, dya_ref, dyb_ref, dla_ref, dlb_ref):
        dao = (mod_ref[2:3, :] * dx1_ref[...]).astype(BF16)
        dm = lax.dot_general(dao, wo_ref[...], NT, preferred_element_type=F32)
        r = lax.broadcasted_iota(jnp.int32, (BW, NH), 0) // HD
        c = lax.broadcasted_iota(jnp.int32, (BW, NH), 1)
        head_of = (r == c).astype(F32)
        for br, (g_ref, u_ref, y_ref, du_ref, dg_ref, dy_ref, dl_ref) in enumerate((
                (ga_ref, ua_ref, ya_ref, dua_ref, dga_ref, dya_ref, dla_ref),
                (gb_ref, ub_ref, yb_ref, dub_ref, dgb_ref, dyb_ref, dlb_ref))):
            sg = jax.nn.sigmoid(g_ref[...].astype(F32))
            du = (dm * sg).astype(BF16)
            du_ref[...] = du
            dg_ref[...] = (dm * u_ref[...].astype(F32) * sg * (1.0 - sg)).astype(BF16)
            dy = lax.dot_general(du, wb_ref[br], NT, preferred_element_type=F32)
            dyb16 = dy.astype(BF16)
            dy_ref[...] = dyb16
            dl_ref[...] = jnp.dot(dyb16.astype(F32) * y_ref[...].astype(F32), head_of,
                                  preferred_element_type=F32, precision=lax.Precision.HIGHEST)

    return pl.pallas_call(
        body, grid=(S // tm,),
        in_specs=[_rows(tm, D), _const((6, D)), _rows(tm, D), _rows(tm, D), _rows(tm, D), _rows(tm, D),
                  _rows(tm, BW), _rows(tm, BW), _const((2, BW, D)), _const((D, D))],
        out_specs=[_rows(tm, D)] * 4 + [_rows(tm, BW)] * 2 + [_rows(tm, NH)] * 2,
        out_shape=[jax.ShapeDtypeStruct((S, D), BF16)] * 4 + [jax.ShapeDtypeStruct((S, BW), BF16)] * 2
        + [jax.ShapeDtypeStruct((S, NH), F32)] * 2,
        compiler_params=_params(1), name="merge_bwd",
    )(dx1, modv, ga, gb, ua, ub, ya, yb, wb, wout)


def _attn_a_bwd(q, k, v, do, lse, delta):
    S = q.shape[0]
    tq = min(512, S)
    tk = min(512, S)
    nq, nk = S // tq, S // tk

    def body(q_hbm, do_hbm, lse_hbm, dl_hbm, k_ref, v_ref, dq_hbm, dk_ref, dv_ref, q_sc, do_sc, lse_sc, dl_sc, dq_sc, sem):
        j = pl.program_id(0)

        @pl.when(j == 0)
        def _():
            copies = [pltpu.make_async_copy(src, dst, sem.at[n]) for n, (src, dst) in enumerate(
                ((q_hbm, q_sc), (do_hbm, do_sc), (lse_hbm, lse_sc), (dl_hbm, dl_sc)))]
            for cp in copies:
                cp.start()
            dq_sc[...] = jnp.zeros_like(dq_sc)
            for cp in copies:
                cp.wait()

        dks, dvs = [], []
        for kv in range(2):
            kj = k_ref[:, HD * kv:HD * kv + HD]
            vj = v_ref[:, HD * kv:HD * kv + HD]
            carry = (jnp.zeros((tk, HD), F32), jnp.zeros((tk, HD), F32))

            def step(i, carry, kv=kv, kj=kj, vj=vj):
                dk, dv = carry
                off = pl.multiple_of(i * tq, tq)
                qg = q_sc[pl.ds(off, tq), 256 * kv:256 * kv + 256]
                dog = do_sc[pl.ds(off, tq), 256 * kv:256 * kv + 256]
                lse_i = lse_sc[pl.ds(off, tq), :]
                dl_i = dl_sc[pl.ds(off, tq), :]
                dqs = []
                for t in range(GRP):
                    h = GRP * kv + t
                    qh = qg[:, HD * t:HD * t + HD]
                    doh = dog[:, HD * t:HD * t + HD]
                    s = lax.dot_general(qh, kj, NT, preferred_element_type=F32)
                    pm = jnp.exp2(s - lse_i[:, h:h + 1])
                    dp = lax.dot_general(doh, vj, NT, preferred_element_type=F32)
                    ds = (pm * (dp - dl_i[:, h:h + 1])).astype(BF16)
                    dv = dv + lax.dot_general(pm.astype(BF16), doh, TN, preferred_element_type=F32)
                    dk = dk + lax.dot_general(ds, qh, TN, preferred_element_type=F32)
                    dqs.append(jnp.dot(ds, kj, preferred_element_type=F32))
                dq_sc[pl.ds(off, tq), 256 * kv:256 * kv + 256] += jnp.concatenate(dqs, axis=1)
                return dk, dv

            carry = lax.fori_loop(0, nq, step, carry)
            dks.append(carry[0])
            dvs.append(carry[1])
        dk_ref[...] = jnp.concatenate(dks, axis=1)
        dv_ref[...] = jnp.concatenate(dvs, axis=1)

        @pl.when(j == nk - 1)
        def _():
            out = pltpu.make_async_copy(dq_sc, dq_hbm, sem.at[0])
            out.start()
            out.wait()

    any_spec = pl.BlockSpec(memory_space=pl.ANY)
    return pl.pallas_call(
        body, grid=(nk,),
        in_specs=[any_spec, any_spec, any_spec, any_spec, _rows(tk, 128), _rows(tk, 128)],
        out_specs=[any_spec, _rows(tk, 128), _rows(tk, 128)],
        out_shape=[jax.ShapeDtypeStruct((S, BW), F32), jax.ShapeDtypeStruct((S, 128), F32), jax.ShapeDtypeStruct((S, 128), F32)],
        scratch_shapes=[pltpu.VMEM((S, BW), BF16), pltpu.VMEM((S, BW), BF16), pltpu.VMEM((S, NH), F32), pltpu.VMEM((S, NH), F32),
                        pltpu.VMEM((S, BW), F32), pltpu.SemaphoreType.DMA((4,))],
        compiler_params=_params(1), name="attn_a_bwd",
    )(q, do, lse, delta, k, v)


def _attn_b_bwd(q, kp, vp, sink, do, lse, delta):
    S = q.shape[0]
    tq = min(512, S)
    W = tq + 2 * WIN
    nq = S // tq

    def body(q_ref, k_ref, v_ref, sink_ref, do_ref, lse_ref, dl_ref, dq_ref, dk_hbm, dv_hbm, ds_ref, dk_sc, dv_sc, sem):
        i = pl.program_id(0)

        @pl.when(i == 0)
        def _():
            dk_sc[...] = jnp.zeros_like(dk_sc)
            dv_sc[...] = jnp.zeros_like(dv_sc)
            ds_ref[...] = jnp.zeros_like(ds_ref)

        off = pl.multiple_of(i * tq, tq)
        valid = _window_mask(i, tq, S)
        kw = k_ref[pl.ds(off, W), :]
        vw = v_ref[pl.ds(off, W), :]
        lse_i = lse_ref[...]
        dl_i = dl_ref[...]
        dks, dvs = [], []
        for kv in range(2):
            kh = kw[:, HD * kv:HD * kv + HD]
            vh = vw[:, HD * kv:HD * kv + HD]
            dk = jnp.zeros((W, HD), F32)
            dv = jnp.zeros((W, HD), F32)
            for pp in range(2):
                p = 2 * kv + pp
                qp = q_ref[:, 128 * p:128 * p + 128]
                dop = do_ref[:, 128 * p:128 * p + 128]
                dqs = []
                for hh in range(2):
                    h = 2 * p + hh
                    qh = qp[:, HD * hh:HD * hh + HD]
                    doh = dop[:, HD * hh:HD * hh + HD]
                    s = jnp.where(valid, lax.dot_general(qh, kh, NT, preferred_element_type=F32), NEG)
                    pm = jnp.exp(s - lse_i[:, h:h + 1])
                    dp = lax.dot_general(doh, vh, NT, preferred_element_type=F32)
                    ds = (pm * (dp - dl_i[:, h:h + 1])).astype(BF16)
                    dv = dv + lax.dot_general(pm.astype(BF16), doh, TN, preferred_element_type=F32)
                    dk = dk + lax.dot_general(ds, qh, TN, preferred_element_type=F32)
                    dqs.append(jnp.dot(ds, kh, preferred_element_type=F32))
                dq_ref[:, 128 * p:128 * p + 128] = jnp.concatenate(dqs, axis=1)
            dks.append(dk)
            dvs.append(dv)
        dk_sc[pl.ds(off, W), :] += jnp.concatenate(dks, axis=1)
        dv_sc[pl.ds(off, W), :] += jnp.concatenate(dvs, axis=1)
        psd = jnp.exp(sink_ref[...] - lse_i) * dl_i
        r = lax.broadcasted_iota(jnp.int32, (NH, 128), 0)
        c = lax.broadcasted_iota(jnp.int32, (NH, 128), 1)
        row = jnp.dot(jnp.sum(psd, axis=0, keepdims=True), (r == c).astype(F32),
                      preferred_element_type=F32, precision=lax.Precision.HIGHEST)
        ds_ref[...] -= jnp.broadcast_to(row, (8, 128))

        @pl.when(i == nq - 1)
        def _():
            c1 = pltpu.make_async_copy(dk_sc, dk_hbm, sem.at[0])
            c2 = pltpu.make_async_copy(dv_sc, dv_hbm, sem.at[1])
            c1.start()
            c2.start()
            c1.wait()
            c2.wait()

    any_spec = pl.BlockSpec(memory_space=pl.ANY)
    return pl.pallas_call(
        body, grid=(nq,),
        in_specs=[_rows(tq, BW), _const((S + 2 * WIN, 128)), _const((S + 2 * WIN, 128)), _const((1, NH)),
                  _rows(tq, BW), _rows(tq, NH), _rows(tq, NH)],
        out_specs=[_rows(tq, BW), any_spec, any_spec, _const((8, 128))],
        out_shape=[jax.ShapeDtypeStruct((S, BW), F32), jax.ShapeDtypeStruct((S + 2 * WIN, 128), F32),
                   jax.ShapeDtypeStruct((S + 2 * WIN, 128), F32), jax.ShapeDtypeStruct((8, 128), F32)],
        scratch_shapes=[pltpu.VMEM((S + 2 * WIN, 128), F32), pltpu.VMEM((S + 2 * WIN, 128), F32), pltpu.SemaphoreType.DMA((2,))],
        compiler_params=_params(1), name="attn_b_bwd",
    )(q, kp, vp, sink, do, lse, delta)


def _qk_bwd(dqa, dka, dva, dqb, dkb, dvb, qar, kar, qg2, kg2, tab_a, tab_b, dga, dgb):
    S = dqa.shape[0]
    tm = min(256, S)

    def body(dqa_ref, dka_ref, dva_ref, dqb_ref, dkb_ref, dvb_ref, qar_ref, kar_ref, qg_ref, kg_ref, ta_ref, tb_ref,
             dga_ref, dgb_ref, dp_ref, st_ref):
        @pl.when(pl.program_id(0) == 0)
        def _():
            st_ref[...] = jnp.zeros_like(st_ref)

        seg = _seg_matrix(128, HD)

        def norm_bwd(dz_rot, raw, g):
            dzn = _rope_t(dz_rot, ta_ref, 16)
            raw = raw.astype(F32)
            rr = lax.rsqrt(_seg_sum(raw * raw, seg) * (1.0 / HD) + EPS)
            zhat = raw * rr
            dzh = dzn * g
            draw = rr * (dzh - zhat * (_seg_sum(dzh * zhat, seg) * (1.0 / HD)))
            return draw, jnp.sum(dzn * zhat, axis=0, keepdims=True)

        gq = jnp.zeros((1, 128), F32)
        for p in range(4):
            sl = slice(128 * p, 128 * p + 128)
            draw, gsum = norm_bwd(dqa_ref[:, sl] * 0.125, qar_ref[:, sl], qg_ref[...])
            gq = gq + gsum
            dp_ref[:, sl] = draw.astype(BF16)
            dp_ref[:, 768 + 128 * p:768 + 128 * p + 128] = _rope_t(dqb_ref[:, sl] * 0.125, tb_ref, 32).astype(BF16)
        draw, gk = norm_bwd(dka_ref[...] * LN2, kar_ref[...], kg_ref[...])
        dp_ref[:, 512:640] = draw.astype(BF16)
        dp_ref[:, 640:768] = dva_ref[...].astype(BF16)
        dp_ref[:, 1280:1408] = _rope_t(dkb_ref[...], tb_ref, 32).astype(BF16)
        dp_ref[:, 1408:1536] = dvb_ref[...].astype(BF16)
        dp_ref[:, 1536:2560] = dga_ref[...]
        dp_ref[:, 2560:3584] = dgb_ref[...]
        st_ref[0:1, :] += gq
        st_ref[1:2, :] += gk

    tab = pl.BlockSpec((3, tm, 128), lambda i: (0, i, 0))
    return pl.pallas_call(
        body, grid=(S // tm,),
        in_specs=[_rows(tm, BW), _rows(tm, 128), _rows(tm, 128), _rows(tm, BW), _rows(tm, 128), _rows(tm, 128),
                  _rows(tm, BW), _rows(tm, 128), _const((1, 128)), _const((1, 128)), tab, tab, _rows(tm, D), _rows(tm, D)],
        out_specs=[_rows(tm, INW), _const((8, 128))],
        out_shape=[jax.ShapeDtypeStruct((S, INW), BF16), jax.ShapeDtypeStruct((8, 128), F32)],
        compiler_params=_params(1), name="qk_bwd",
    )(dqa, dka, dva, dqb, dkb, dvb, qar, kar, qg2, kg2, tab_a, tab_b, dga, dgb)


def _in_bwd(dproj, win, x, dx1, modv, n1g):
    S = x.shape[0]
    tm = min(256, S)

    def body(dp_ref, w_ref, x_ref, dx1_ref, mod_ref, g_ref, gx_ref, st_ref):
        @pl.when(pl.program_id(0) == 0)
        def _():
            st_ref[...] = jnp.zeros_like(st_ref)

        dh = lax.dot_general(dp_ref[...], w_ref[...], NT, preferred_element_type=F32)
        xt = x_ref[...]
        r = lax.rsqrt(jnp.mean(xt * xt, axis=-1, keepdims=True) + EPS)
        xn = xt * r
        st_ref[0:1, :] += jnp.sum(dh, axis=0, keepdims=True)
        st_ref[1:2, :] += jnp.sum(dh * xn, axis=0, keepdims=True)
        dxn = dh * (g_ref[...] * (1.0 + mod_ref[1:2, :]))
        gx_ref[...] = dx1_ref[...] + r * (dxn - xn * jnp.mean(dxn * xn, axis=-1, keepdims=True))

    return pl.pallas_call(
        body, grid=(S // tm,),
        in_specs=[_rows(tm, INW), _const((D, INW)), _rows(tm, D), _rows(tm, D), _const((6, D)), _const((1, D))],
        out_specs=[_rows(tm, D), _const((8, D))],
        out_shape=[jax.ShapeDtypeStruct((S, D), F32), jax.ShapeDtypeStruct((8, D), F32)],
        compiler_params=_params(1), name="in_bwd",
    )(dproj, win, x, dx1, modv, n1g)


def _pack_small(st1, st2, stf, dg1, dg2, stqk, dsink, modv, n1g, n2g):
    def body(st1_ref, st2_ref, stf_ref, dg1_ref, dg2_ref, qk_ref, ds_ref, mod_ref, g1_ref, g2_ref, o_ref):
        a1, b1 = st1_ref[0:1, :], st1_ref[1:2, :]
        a2, b2 = st2_ref[0:1, :], st2_ref[1:2, :]
        r = lax.broadcasted_iota(jnp.int32, (128, D), 0)
        c = lax.broadcasted_iota(jnp.int32, (128, D), 1)
        fold_q = (c == r % HD).astype(F32)
        fold_k = (c == HD + r % HD).astype(F32)
        keep = (c == r).astype(F32)

        def place(v, sel):
            return jnp.dot(v, sel, preferred_element_type=F32, precision=lax.Precision.HIGHEST)

        loss = jnp.sum(stf_ref[1:2, :], axis=1, keepdims=True)
        lane = lax.broadcasted_iota(jnp.int32, (1, D), 1)
        rows = [a1, g1_ref[...] * b1, dg1_ref[0:1, :], a2, g2_ref[...] * b2, dg2_ref[0:1, :],
                (1.0 + mod_ref[1:2, :]) * b1, (1.0 + mod_ref[4:5, :]) * b2, stf_ref[0:1, :],
                place(qk_ref[0:1, :], fold_q) + place(qk_ref[1:2, :], fold_k),
                place(ds_ref[0:1, :], keep),
                jnp.where(lane == 0, loss, 0.0)]
        rows += [jnp.zeros((1, D), F32)] * (SMALL_ROWS - len(rows))
        for n, v in enumerate(rows):
            o_ref[n:n + 1, :] = v

    return pl.pallas_call(
        body, out_shape=jax.ShapeDtypeStruct((SMALL_ROWS, D), F32),
        compiler_params=pltpu.CompilerParams(vmem_limit_bytes=V7X_VMEM_LIMIT), name="pack_small",
    )(st1, st2, stf, dg1, dg2, stqk, dsink, modv, n1g, n2g)


def _wada_grad(silu_all, dmod_cols):
    def body(a_ref, b_ref, o_ref):
        o_ref[...] = lax.dot_general(a_ref[...], b_ref[...], TN, preferred_element_type=F32, precision=lax.Precision.HIGHEST)

    return pl.pallas_call(
        body, out_shape=jax.ShapeDtypeStruct((D, dmod_cols.shape[1]), F32),
        compiler_params=pltpu.CompilerParams(vmem_limit_bytes=V7X_VMEM_LIMIT), name="wada_grad",
    )(silu_all, dmod_cols)


def _adamw_sum(parts, w, m, v, name):
    R, C = w.shape
    tr = R if R <= 512 else next(t for t in (512, 432, 256, 216, 128, 64, 8) if R % t == 0)
    n = len(parts)
    dyn = [idx for _, idx in parts if idx is not None and not isinstance(idx, int)]
    b1c = 1.0 - ADAM_B1 ** ADAM_STEP
    b2c = 1.0 - ADAM_B2 ** ADAM_STEP

    def body(*refs):
        refs = refs[len(dyn):]
        g = refs[0][...]
        for k in range(1, n):
            g = g + refs[k][...]
        w_ref, m_ref, v_ref, g_out, d_out, m_out, v_out = refs[n:]
        mn = ADAM_B1 * m_ref[...] + (1.0 - ADAM_B1) * g
        vn = ADAM_B2 * v_ref[...] + (1.0 - ADAM_B2) * jnp.square(g)
        g_out[...] = g
        m_out[...] = mn
        v_out[...] = vn
        d_out[...] = -ADAM_LR * ((mn / b1c) / (jnp.sqrt(vn / b2c) + ADAM_EPS) + ADAM_WD * w_ref[...])

    in_specs = []
    nd = 0
    for a, idx in parts:
        if idx is None:
            in_specs.append(pl.BlockSpec((tr, C), lambda i, *s: (i, 0)))
        elif isinstance(idx, int):
            in_specs.append(pl.BlockSpec((None, tr, C), lambda i, *s, idx=idx: (idx, i, 0)))
        else:
            in_specs.append(pl.BlockSpec((None, tr, C), lambda i, *s, nd=nd: (s[nd][0], i, 0)))
            nd += 1
    blk = pl.BlockSpec((tr, C), lambda i, *s: (i, 0))
    grid_spec = pltpu.PrefetchScalarGridSpec(
        num_scalar_prefetch=len(dyn), grid=(R // tr,), in_specs=in_specs + [blk] * 3, out_specs=[blk] * 4)
    return pl.pallas_call(
        body, grid_spec=grid_spec, out_shape=[jax.ShapeDtypeStruct((R, C), F32)] * 4,
        compiler_params=_params(1), name=name,
    )(*dyn, *[a for a, _ in parts], w, m, v)


def _pair_sum(g4, recv, core):
    tr = 216

    def body(c_ref, a_ref, b_ref, o_ref):
        o_ref[...] = a_ref[...] + b_ref[...]

    grid_spec = pltpu.PrefetchScalarGridSpec(
        num_scalar_prefetch=1, grid=(4, ROWS_PACK // tr),
        in_specs=[pl.BlockSpec((None, None, tr, D), lambda ch, i, c: (ch, c[0], i, 0)),
                  pl.BlockSpec((None, tr, D), lambda ch, i, c: (ch, i, 0))],
        out_specs=pl.BlockSpec((None, tr, D), lambda ch, i, c: (ch, i, 0)))
    return pl.pallas_call(
        body, grid_spec=grid_spec, out_shape=jax.ShapeDtypeStruct((4, ROWS_PACK, D), F32),
        compiler_params=_params(2), name="pair_sum",
    )(core, g4, recv)


def _me():
    return lax.axis_index("x"), lax.axis_index("y"), lax.axis_index("c")


def _peer(k):
    x, y, c = _me()
    return (x ^ ((k >> 2) & 1), y ^ ((k >> 1) & 1), c ^ (k & 1))


def _ada_exchange(c_row, w_ada, b_rows):
    NW = w_ada.shape[1]

    def body(c_ref, w_ref, b_ref, sall_ref, mod_ref, src_ref, mp_ref, send1, recv1, send2, recv2):
        x, y, c = _me()
        me = 4 * x + 2 * y + c
        cv = c_ref[...]
        src_ref[...] = jnp.broadcast_to(cv * jax.nn.sigmoid(cv), (8, D))
        mine = pl.ds(pl.multiple_of(me * 8, 8), 8)
        sall_ref[mine, :] = src_ref[...]
        sends = [pltpu.make_async_remote_copy(src_ref, sall_ref.at[mine, :], send1.at[k - 1], recv1.at[k - 1],
                                              device_id=_peer(k), device_id_type=MESH) for k in range(1, N_DEV)]
        for cp in sends:
            cp.start()
        for k in range(1, N_DEV):
            theirs = pl.ds(pl.multiple_of((me ^ k) * 8, 8), 8)
            pltpu.make_async_remote_copy(src_ref, sall_ref.at[theirs, :], send1.at[k - 1], recv1.at[k - 1],
                                         device_id=_peer(k), device_id_type=MESH).wait_recv()
        for cp in sends:
            cp.wait_send()
        mp_ref[...] = jnp.dot(sall_ref[...], w_ref[...], preferred_element_type=F32, precision=lax.Precision.HIGHEST)
        mod_ref[mine, :] = mp_ref[mine, :] + b_ref[mine, :]
        sends = []
        for k in range(1, N_DEV):
            theirs = pl.ds(pl.multiple_of((me ^ k) * 8, 8), 8)
            sends.append(pltpu.make_async_remote_copy(mp_ref.at[theirs, :], mod_ref.at[mine, :], send2.at[k - 1], recv2.at[k - 1],
                                                      device_id=_peer(k), device_id_type=MESH))
        for cp in sends:
            cp.start()
        for k in range(1, N_DEV):
            theirs = pl.ds(pl.multiple_of((me ^ k) * 8, 8), 8)
            pltpu.make_async_remote_copy(mp_ref.at[mine, :], mod_ref.at[theirs, :], send2.at[k - 1], recv2.at[k - 1],
                                         device_id=_peer(k), device_id_type=MESH).wait_recv()
            mod_ref[theirs, :] = mod_ref[theirs, :] + b_ref[theirs, :]
        for cp in sends:
            cp.wait_send()

    vm = pl.BlockSpec(memory_space=pltpu.VMEM)
    return pl.pallas_call(
        body, in_specs=[vm, vm, vm], out_specs=[vm, vm],
        out_shape=[jax.ShapeDtypeStruct((8 * N_DEV, D), F32), jax.ShapeDtypeStruct((8 * N_DEV, NW), F32)],
        scratch_shapes=[pltpu.VMEM((8, D), F32), pltpu.VMEM((8 * N_DEV, NW), F32)]
        + [pltpu.SemaphoreType.DMA((N_DEV - 1,))] * 4,
        compiler_params=pltpu.CompilerParams(vmem_limit_bytes=V7X_VMEM_LIMIT), name="ada_exchange",
    )(c_row, w_ada, b_rows)


def _weight_gather(shard):
    def body(x_ref, out_ref, send_sems, recv_sems, local_sem):
        x, y, c = _me()
        me, sibling = (x, y, c), (x, y, 1 - c)
        chips = [(1 - x, y), (x, 1 - y), (1 - x, 1 - y)]

        def slot(px, py, pc):
            return out_ref.at[4 * px + 2 * py + pc]

        def copy(k, block, to, src=None):
            return pltpu.make_async_remote_copy(
                src_ref=slot(*block) if src is None else src, dst_ref=slot(*block),
                send_sem=send_sems.at[k], recv_sem=recv_sems.at[k], device_id=to, device_id_type=MESH)

        mine = pltpu.make_async_copy(x_ref, slot(*me), local_sem)
        mine.start()
        first = [copy(0, me, sibling, src=x_ref)]
        first += [copy(1 + j, me, (*chip, c), src=x_ref) for j, chip in enumerate(chips)]
        for cp in first:
            cp.start()
        passed = [copy(4 + j, (*chip, c), sibling) for j, chip in enumerate(chips)]
        for j, chip in enumerate(chips):
            copy(1 + j, (*chip, c), me).wait_recv()
            passed[j].start()
        copy(0, sibling, me).wait_recv()
        for j, chip in enumerate(chips):
            copy(4 + j, (*chip, 1 - c), me).wait_recv()
        for cp in first + passed:
            cp.wait_send()
        mine.wait()

    any_spec = pl.BlockSpec(memory_space=pl.ANY)
    return pl.pallas_call(
        body, in_specs=[any_spec], out_specs=any_spec,
        out_shape=jax.ShapeDtypeStruct((N_DEV,) + shard.shape, shard.dtype),
        scratch_shapes=[pltpu.SemaphoreType.DMA((7,)), pltpu.SemaphoreType.DMA((7,)), pltpu.SemaphoreType.DMA],
        name="weight_gather",
    )(shard)


def _grad_rs_sibling(g4):
    def body(g_ref, recv_ref, send_sems, recv_sems):
        x, y, c = _me()
        sibling = (x, y, 1 - c)
        copies = [pltpu.make_async_remote_copy(g_ref.at[ch, 1 - c], recv_ref.at[ch], send_sems.at[ch], recv_sems.at[ch],
                                               device_id=sibling, device_id_type=MESH) for ch in range(4)]
        for cp in copies:
            cp.start()
        for cp in copies:
            cp.wait_recv()
        for cp in copies:
            cp.wait_send()

    any_spec = pl.BlockSpec(memory_space=pl.ANY)
    return pl.pallas_call(
        body, in_specs=[any_spec], out_specs=any_spec,
        out_shape=jax.ShapeDtypeStruct((4,) + g4.shape[2:], g4.dtype),
        scratch_shapes=[pltpu.SemaphoreType.DMA((4,)), pltpu.SemaphoreType.DMA((4,))],
        name="grad_rs_sibling",
    )(g4)


def _grad_rs_chips(t4):
    def body(t_ref, recv_ref, send_sems, recv_sems):
        x, y, c = _me()
        chip = 2 * x + y
        copies = []
        for k in range(1, 4):
            to = (x ^ (k >> 1), y ^ (k & 1), c)
            copies.append(pltpu.make_async_remote_copy(t_ref.at[chip ^ k], recv_ref.at[k - 1], send_sems.at[k - 1], recv_sems.at[k - 1],
                                                       device_id=to, device_id_type=MESH))
        for cp in copies:
            cp.start()
        for cp in copies:
            cp.wait_recv()
        for cp in copies:
            cp.wait_send()

    any_spec = pl.BlockSpec(memory_space=pl.ANY)
    return pl.pallas_call(
        body, in_specs=[any_spec], out_specs=any_spec,
        out_shape=jax.ShapeDtypeStruct((3,) + t4.shape[1:], t4.dtype),
        scratch_shapes=[pltpu.SemaphoreType.DMA((3,)), pltpu.SemaphoreType.DMA((3,))],
        name="grad_rs_chips",
    )(t4)


def _small_gather(block):
    def body(b_ref, out_ref, send_sems, recv_sems):
        x, y, c = _me()
        me = 4 * x + 2 * y + c
        out_ref[me] = b_ref[...]
        sends = [pltpu.make_async_remote_copy(b_ref, out_ref.at[me], send_sems.at[k - 1], recv_sems.at[k - 1],
                                              device_id=_peer(k), device_id_type=MESH) for k in range(1, N_DEV)]
        for cp in sends:
            cp.start()
        for k in range(1, N_DEV):
            pltpu.make_async_remote_copy(b_ref, out_ref.at[me ^ k], send_sems.at[k - 1], recv_sems.at[k - 1],
                                         device_id=_peer(k), device_id_type=MESH).wait_recv()
        for cp in sends:
            cp.wait_send()

    vm = pl.BlockSpec(memory_space=pltpu.VMEM)
    return pl.pallas_call(
        body, in_specs=[vm], out_specs=vm,
        out_shape=jax.ShapeDtypeStruct((N_DEV,) + block.shape, block.dtype),
        scratch_shapes=[pltpu.SemaphoreType.DMA((N_DEV - 1,)), pltpu.SemaphoreType.DMA((N_DEV - 1,))],
        name="small_gather",
    )(block)


def _pack_shards(w_in, w_branch, w_out, w_mlp_in, w_mlp_out):
    return jnp.concatenate([w_in.reshape(-1, D), w_branch.reshape(-1, D), w_out.reshape(-1, D),
                            w_mlp_in.reshape(-1, D), w_mlp_out.reshape(-1, D)], axis=0)


def _unpack_shards(p):
    return (p[0:448].reshape(1, D, 448), p[448:576].reshape(1, 2, BW, 128), p[576:704].reshape(1, 128, D),
            p[704:1216].reshape(1, D, 512), p[1216:1728].reshape(1, 512, D))


def _pack_small_params(b_ada, n1, n2, fg, qn, kn, sink):
    z = jnp.zeros((SMALL_ROWS, D), F32)
    z = z.at[0:6].set(b_ada.reshape(6, D)).at[6].set(n1.reshape(D)).at[7].set(n2.reshape(D)).at[8].set(fg.reshape(D))
    z = z.at[9, 0:HD].set(qn.reshape(HD)).at[9, HD:2 * HD].set(kn.reshape(HD)).at[10, 0:NH].set(sink.reshape(NH))
    return z


def _unpack_small(p):
    return (p[0:6].reshape(1, 6 * D), p[6].reshape(1, D), p[9, 0:HD].reshape(1, HD), p[9, HD:2 * HD].reshape(1, HD),
            p[10, 0:NH].reshape(1, NH), p[7].reshape(1, D), p[8].reshape(D))


def kernel(x, c, w_ada, b_ada, norm1_g, w_in, q_norm_a, k_norm_a, sink_b, w_branch, w_out, norm2_g, w_mlp_in, w_mlp_out, final_g, loss_target, m_w_ada, m_b_ada, m_norm1_g, m_w_in, m_q_norm_a, m_k_norm_a, m_sink_b, m_w_branch, m_w_out, m_norm2_g, m_w_mlp_in, m_w_mlp_out, m_final_g, v_w_ada, v_b_ada, v_norm1_g, v_w_in, v_q_norm_a, v_k_norm_a, v_sink_b, v_w_branch, v_w_out, v_norm2_g, v_w_mlp_in, v_w_mlp_out, v_final_g):
    S = x.shape[1]
    xs = x.reshape(S, D)
    tgt = loss_target.reshape(S, D)
    ax, ay, ac = lax.axis_index("x"), lax.axis_index("y"), lax.axis_index("c")
    me = 4 * ax + 2 * ay + ac
    NW = w_ada.shape[2]

    silu64, mod64 = _ada_exchange(c.reshape(1, D), w_ada.reshape(D, NW),
                                  jnp.repeat(b_ada.reshape(N_DEV, NW), 8, axis=0))
    silu_all = silu64[0::8]
    modv = mod64[0::8].reshape(6, D)

    gathered = _weight_gather(_pack_shards(w_in[0], w_branch[0], w_out[0], w_mlp_in[0], w_mlp_out[0]).astype(BF16))
    win = gathered[:, 0:448].reshape(N_DEV, D, 448).transpose(1, 0, 2).reshape(D, INW)
    wb = gathered[:, 448:576].reshape(N_DEV, 2, BW, 128).transpose(1, 2, 0, 3).reshape(2, BW, D)
    wout = gathered[:, 576:704].reshape(D, D)
    wmi = gathered[:, 704:1216].reshape(N_DEV, D, 512)
    wmo = gathered[:, 1216:1728]

    tab_a, tab_b = _rope_tables(S)
    qg2 = jnp.tile(q_norm_a.reshape(1, HD), (1, 2))
    kg2 = jnp.tile(k_norm_a.reshape(1, HD), (1, 2))
    n1g = norm1_g.reshape(1, D)
    n2g = norm2_g.reshape(1, D)
    fg = final_g.reshape(1, D)
    sink = sink_b.reshape(1, NH)

    h, qar, kar, qa, ka, va, qb, kb, vb, ga, gb = _in_proj(xs, modv, n1g, win, qg2, kg2, tab_a, tab_b)
    ya, lse_a = _attn_a_fwd(qa, ka, va)
    pad = ((WIN, WIN), (0, 0))
    kbp, vbp = jnp.pad(kb, pad), jnp.pad(vb, pad)
    yb, lse_b = _attn_b_fwd(qb, kbp, vbp, sink)
    x1, merged, ua, ub = _merge_out(ya, yb, ga, gb, xs, modv, wb, wout)
    h2, hp, dx2, stf = _mlp_fwd(x1, modv, n2g, wmi, wmo, fg, tgt)

    dhp, dx1, st2 = _mlp_bwd(dx2, x1, hp, modv, n2g, wmi, wmo)
    m2 = _tn_matmul(hp, dx2, 2048, D, "dw_mlp_out", relu_sq=True)
    g_wmo, dg2 = _scale_gate(m2, wmo.reshape(FF, D), modv, 5, "gate2_grad")
    g_wmi = _tn_matmul(h2, dhp, D, 512, "dw_mlp_in", dev_major=True)
    dua, dub, dga, dgb, dya, dyb, dl_a, dl_b = _merge_bwd(dx1, modv, ga, gb, ua, ub, ya, yb, wb, wout)
    m1 = _tn_matmul(merged, dx1, D, D, "dw_out")
    g_wout, dg1 = _scale_gate(m1, wout, modv, 2, "gate1_grad")
    g_wb0 = _tn_matmul(ya, dua, BW, D, "dw_branch_a")
    g_wb1 = _tn_matmul(yb, dub, BW, D, "dw_branch_b")
    dqa, dka, dva = _attn_a_bwd(qa, ka, va, dya, lse_a, dl_a)
    dqb, dkbp, dvbp, dsink = _attn_b_bwd(qb, kbp, vbp, sink, dyb, lse_b, dl_b)
    dproj, stqk = _qk_bwd(dqa, dka, dva, dqb, dkbp[WIN:WIN + S], dvbp[WIN:WIN + S], qar, kar, qg2, kg2, tab_a, tab_b, dga, dgb)
    grad_x, st1 = _in_bwd(dproj, win, xs, dx1, modv, n1g)
    g_win = _tn_matmul(h, dproj, D, 896, "dw_in")

    g_pack = jnp.concatenate([
        g_win.reshape(D, N_DEV, 448).transpose(1, 0, 2).reshape(N_DEV, 448, D),
        jnp.stack([g_wb0, g_wb1]).reshape(2, BW, N_DEV, 128).transpose(2, 0, 1, 3).reshape(N_DEV, 128, D),
        g_wout.reshape(N_DEV, 128, D),
        g_wmi.reshape(N_DEV, 512, D),
        g_wmo.reshape(N_DEV, 512, D)], axis=1)
    g4 = g_pack.reshape(4, 2, ROWS_PACK, D)
    from_sibling = _grad_rs_sibling(g4)
    t4 = _pair_sum(g4, from_sibling, ac.reshape(1).astype(jnp.int32))
    from_chips = _grad_rs_chips(t4)
    chip = (2 * ax + ay).reshape(1).astype(jnp.int32)
    w_p = _pack_shards(w_in[0], w_branch[0], w_out[0], w_mlp_in[0], w_mlp_out[0])
    m_p = _pack_shards(m_w_in[0], m_w_branch[0], m_w_out[0], m_w_mlp_in[0], m_w_mlp_out[0])
    v_p = _pack_shards(v_w_in[0], v_w_branch[0], v_w_out[0], v_w_mlp_in[0], v_w_mlp_out[0])
    big = _adamw_sum([(t4, chip), (from_chips, 0), (from_chips, 1), (from_chips, 2)], w_p, m_p, v_p, "adamw_big")
    big = [_unpack_shards(a) for a in big]

    small = _pack_small(st1, st2, stf, dg1, dg2, stqk, dsink, modv, n1g, n2g)
    small_all = _small_gather(small)
    sw = _pack_small_params(b_ada, norm1_g, norm2_g, final_g, q_norm_a, k_norm_a, sink_b)
    sm = _pack_small_params(m_b_ada, m_norm1_g, m_norm2_g, m_final_g, m_q_norm_a, m_k_norm_a, m_sink_b)
    sv = _pack_small_params(v_b_ada, v_norm1_g, v_norm2_g, v_final_g, v_q_norm_a, v_k_norm_a, v_sink_b)
    sm_out = _adamw_sum([(small_all, k) for k in range(N_DEV)], sw, sm, sv, "adamw_small")
    loss = sm_out[0][11, 0]
    sm_out = [_unpack_small(a) for a in sm_out]

    dmod_all = small_all[:, 0:6, :].reshape(N_DEV, 6 * D)
    dmod_cols = lax.dynamic_slice_in_dim(dmod_all, me * NW, NW, axis=1)
    g_wada = _wada_grad(silu_all, dmod_cols)
    ada = _adamw_sum([(g_wada, None)], w_ada.reshape(D, NW), m_w_ada.reshape(D, NW), v_w_ada.reshape(D, NW), "adamw_ada")
    ada = [a.reshape(1, D, NW) for a in ada]

    def leaves(k):
        b_, n1_, qn_, kn_, sk_, n2_, fg_ = sm_out[k]
        wi_, wbr_, wo_, wmi_, wmo_ = big[k]
        return [ada[k], b_, n1_, wi_, qn_, kn_, sk_, wbr_, wo_, n2_, wmi_, wmo_, fg_]

    return (loss, grad_x.reshape(1, S, D), *leaves(0), *leaves(1), *leaves(2), *leaves(3))
```

```python
import jax
import jax.numpy as jnp
from jax import lax
from jax.experimental import pallas as pl
from jax.experimental.pallas import tpu as pltpu

F32, BF16 = jnp.float32, jnp.bfloat16
MESH = pl.DeviceIdType.MESH

D = 1024
HD = 64
NH = 8
GRP = 4
BW = 512
FF = 4096
INW = 3584
GRID_W = 64
WIN = 128
THETA = 10000.0
EPS = 1e-6
NEG = -1e30
N_DEV = 8
LOG2E = 1.4426950408889634
LN2 = 0.6931471805599453
QA_SCALE = 0.125 * LOG2E
ROWS_PACK = 1728
SMALL_ROWS = 16
V7X_VMEM_LIMIT = 56 * 1024 * 1024

ADAM_LR, ADAM_B1, ADAM_B2, ADAM_EPS, ADAM_WD, ADAM_STEP = 0.001, 0.9, 0.999, 1e-08, 0.01, 10

NT = (((1,), (1,)), ((), ()))
TN = (((0,), (0,)), ((), ()))


def _params(n_axes, vmem=V7X_VMEM_LIMIT):
    return pltpu.CompilerParams(dimension_semantics=("arbitrary",) * n_axes, vmem_limit_bytes=vmem)


def _const(shape):
    return pl.BlockSpec(shape, lambda *_: (0,) * len(shape))


def _rows(tm, width):
    return pl.BlockSpec((tm, width), lambda i, *_: (i, 0))


def _seg_matrix(n, seg):
    r = lax.broadcasted_iota(jnp.int32, (n, n), 0) // seg
    c = lax.broadcasted_iota(jnp.int32, (n, n), 1) // seg
    return (r == c).astype(F32)


def _seg_sum(z, seg_mat):
    return jnp.dot(z, seg_mat, preferred_element_type=F32, precision=lax.Precision.HIGHEST)


def _rope(z, t_ref, sh):
    return z * t_ref[0] + pltpu.roll(z, sh, 1) * t_ref[1] + pltpu.roll(z, 128 - sh, 1) * t_ref[2]


def _rope_t(dz, t_ref, sh):
    return dz * t_ref[0] + pltpu.roll(dz * t_ref[1], 128 - sh, 1) + pltpu.roll(dz * t_ref[2], sh, 1)


def _rope_tables(S):
    t = jnp.arange(S, dtype=jnp.int32)
    lane = jnp.arange(HD)

    def build(cos, sin, first):
        t0 = cos
        t1 = jnp.where(first[None, :], 0.0, sin)
        t2 = jnp.where(first[None, :], -sin, 0.0)
        return jnp.stack([jnp.tile(a, (1, 2)) for a in (t0, t1, t2)]).astype(F32)

    inv_a = THETA ** (-jnp.arange(0, HD // 2, 2, dtype=F32) / (HD // 2))
    ar = (t // GRID_W).astype(F32)[:, None] * inv_a[None, :]
    ac = (t % GRID_W).astype(F32)[:, None] * inv_a[None, :]
    cos_a = jnp.concatenate([jnp.cos(ar), jnp.cos(ar), jnp.cos(ac), jnp.cos(ac)], axis=1)
    sin_a = jnp.concatenate([jnp.sin(ar), jnp.sin(ar), jnp.sin(ac), jnp.sin(ac)], axis=1)
    tab_a = build(cos_a, sin_a, (lane % 32) < 16)
    inv_b = THETA ** (-jnp.arange(0, HD, 2, dtype=F32) / HD)
    ab = t.astype(F32)[:, None] * inv_b[None, :]
    cos_b = jnp.concatenate([jnp.cos(ab), jnp.cos(ab)], axis=1)
    sin_b = jnp.concatenate([jnp.sin(ab), jnp.sin(ab)], axis=1)
    tab_b = build(cos_b, sin_b, lane < 32)
    return tab_a, tab_b


def _in_proj(x, modv, n1g, win, qg2, kg2, tab_a, tab_b):
    S = x.shape[0]
    tm = min(256, S)

    def body(x_ref, mod_ref, g_ref, w_ref, qg_ref, kg_ref, ta_ref, tb_ref,
             h_ref, qar_ref, kar_ref, qa_ref, ka_ref, va_ref, qb_ref, kb_ref, vb_ref, ga_ref, gb_ref):
        xt = x_ref[...]
        r = lax.rsqrt(jnp.mean(xt * xt, axis=-1, keepdims=True) + EPS)
        h = ((xt * r) * g_ref[...]) * (1.0 + mod_ref[1:2, :]) + mod_ref[0:1, :]
        hb = h.astype(BF16)
        h_ref[...] = hb
        proj = jnp.dot(hb, w_ref[...], preferred_element_type=F32)
        seg = _seg_matrix(128, HD)

        def head_norm(z, g):
            ms = _seg_sum(z * z, seg) * (1.0 / HD)
            return (z * lax.rsqrt(ms + EPS)) * g

        for p in range(4):
            z = proj[:, 128 * p:128 * p + 128]
            qar_ref[:, 128 * p:128 * p + 128] = z.astype(BF16)
            qa_ref[:, 128 * p:128 * p + 128] = (_rope(head_norm(z, qg_ref[...]), ta_ref, 16) * QA_SCALE).astype(BF16)
            zb = proj[:, 768 + 128 * p:768 + 128 * p + 128]
            qb_ref[:, 128 * p:128 * p + 128] = (_rope(zb, tb_ref, 32) * 0.125).astype(BF16)
        z = proj[:, 512:640]
        kar_ref[...] = z.astype(BF16)
        ka_ref[...] = _rope(head_norm(z, kg_ref[...]), ta_ref, 16).astype(BF16)
        va_ref[...] = proj[:, 640:768].astype(BF16)
        kb_ref[...] = _rope(proj[:, 1280:1408], tb_ref, 32).astype(BF16)
        vb_ref[...] = proj[:, 1408:1536].astype(BF16)
        ga_ref[...] = proj[:, 1536:2560].astype(BF16)
        gb_ref[...] = proj[:, 2560:3584].astype(BF16)

    tab = pl.BlockSpec((3, tm, 128), lambda i: (0, i, 0))
    shapes = [(D, BF16), (BW, BF16), (128, BF16), (BW, BF16), (128, BF16), (128, BF16),
              (BW, BF16), (128, BF16), (128, BF16), (D, BF16), (D, BF16)]
    return pl.pallas_call(
        body, grid=(S // tm,),
        in_specs=[_rows(tm, D), _const((6, D)), _const((1, D)), _const((D, INW)), _const((1, 128)), _const((1, 128)), tab, tab],
        out_specs=[_rows(tm, w) for w, _ in shapes],
        out_shape=[jax.ShapeDtypeStruct((S, w), dt) for w, dt in shapes],
        compiler_params=_params(1), name="in_proj",
    )(x, modv, n1g, win, qg2, kg2, tab_a, tab_b)


def _attn_a_fwd(q, k, vt3):
    S = q.shape[0]
    tq = min(512, S)
    nk, _, tk = vt3.shape
    ONES = 16

    def body(q_ref, k_ref, vt_ref, o_ref, lse_ref, st_sc):
        row8 = lax.broadcasted_iota(jnp.int32, (NH, tq), 0)
        lse_all = jnp.zeros((NH, tq), F32)
        ones = jnp.ones((ONES, tk), BF16)
        for kv in range(2):
            qs = []
            for pp in range(2):
                qp = q_ref[:, 128 * (2 * kv + pp):128 * (2 * kv + pp) + 128]
                qs += [qp[:, :HD], qp[:, HD:]]

            def keys(j, kv=kv):
                return k_ref[pl.ds(pl.multiple_of(j * tk, tk), tk), :][:, HD * kv:HD * kv + HD]

            def scores(kj, t, qs=qs):
                return lax.dot_general(kj, qs[t], NT, preferred_element_type=F32)

            def step(j, carry, kv=kv):
                kj = keys(j)
                kn = keys(jnp.minimum(j + 1, nk - 1))
                v1 = jnp.concatenate([vt_ref[j, HD * kv:HD * kv + HD, :], ones], axis=0)
                sts = [st_sc[0], st_sc[1]]
                new = []
                for t in range(GRP):
                    m, acc = carry[2 * t], carry[2 * t + 1]
                    if t + 2 < GRP:
                        sts.append(scores(kj, t + 2))
                    st = sts[t]
                    mn = jnp.maximum(m, jnp.max(st, axis=0, keepdims=True))
                    pt = jnp.exp2(st - mn)
                    if t + 2 >= GRP:
                        st_sc[t + 2 - GRP] = scores(kn, t + 2 - GRP)
                    acc = jnp.exp2(m - mn) * acc + jnp.dot(v1, pt.astype(BF16), preferred_element_type=F32)
                    new += [mn, acc]
                return tuple(new)

            k0 = keys(0)
            st_sc[0] = scores(k0, 0)
            st_sc[1] = scores(k0, 1)
            init = (jnp.full((1, tq), NEG, F32), jnp.zeros((HD + ONES, tq), F32)) * GRP
            res = lax.fori_loop(0, nk, step, init)
            for t in range(GRP):
                h = GRP * kv + t
                m, acc = res[2 * t], res[2 * t + 1]
                l = acc[HD:HD + 1, :]
                o_ref[HD * h:HD * h + HD, :] = (acc[:HD, :] / l).astype(BF16)
                lse_all = jnp.where(row8 == h, m + jnp.log2(l), lse_all)
        lse_ref[...] = lse_all

    return pl.pallas_call(
        body, grid=(S // tq,),
        in_specs=[_rows(tq, BW), _const((S, 128)), _const((nk, 128, tk))],
        out_specs=[pl.BlockSpec((BW, tq), lambda i: (0, i)), pl.BlockSpec((NH, tq), lambda i: (0, i))],
        out_shape=[jax.ShapeDtypeStruct((BW, S), BF16), jax.ShapeDtypeStruct((NH, S), F32)],
        scratch_shapes=[pltpu.VMEM((2, tk, tq), F32)],
        compiler_params=_params(1), name="attn_a_fwd",
    )(q, k, vt3)


def _window_mask(i, tq, S):
    W = tq + 2 * WIN
    r = lax.broadcasted_iota(jnp.int32, (tq, W), 0)
    c = lax.broadcasted_iota(jnp.int32, (tq, W), 1)
    kpos = i * tq - WIN + c
    return (jnp.abs(c - WIN - r) <= WIN) & (kpos >= 0) & (kpos < S)


def _attn_b_fwd(q, kp, vp, sink):
    S = q.shape[0]
    tq = min(512, S)
    W = tq + 2 * WIN

    def body(q_ref, k_ref, v_ref, sink_ref, o_ref, lse_ref):
        i = pl.program_id(0)
        off = pl.multiple_of(i * tq, tq)
        valid = _window_mask(i, tq, S)
        kw = k_ref[pl.ds(off, W), :]
        vw = v_ref[pl.ds(off, W), :]
        lane8 = lax.broadcasted_iota(jnp.int32, (tq, NH), 1)
        lse_all = jnp.zeros((tq, NH), F32)
        for p in range(4):
            qp = q_ref[:, 128 * p:128 * p + 128]
            kv = p // 2
            kh = kw[:, HD * kv:HD * kv + HD]
            vh = vw[:, HD * kv:HD * kv + HD]
            outs = []
            for hh in range(2):
                h = 2 * p + hh
                s = lax.dot_general(qp[:, HD * hh:HD * hh + HD], kh, NT, preferred_element_type=F32)
                s = jnp.where(valid, s, NEG)
                sk = sink_ref[:, h:h + 1]
                m = jnp.maximum(jnp.max(s, axis=1, keepdims=True), sk)
                pm = jnp.exp(s - m)
                l = jnp.sum(pm, axis=1, keepdims=True) + jnp.exp(sk - m)
                outs.append(jnp.dot(pm.astype(BF16), vh, preferred_element_type=F32) / l)
                lse_all = jnp.where(lane8 == h, m + jnp.log(l), lse_all)
            o_ref[:, 128 * p:128 * p + 128] = jnp.concatenate(outs, axis=1).astype(BF16)
        lse_ref[...] = lse_all

    return pl.pallas_call(
        body, grid=(S // tq,),
        in_specs=[_rows(tq, BW), _const((S + 2 * WIN, 128)), _const((S + 2 * WIN, 128)), _const((1, NH))],
        out_specs=[_rows(tq, BW), _rows(tq, NH)],
        out_shape=[jax.ShapeDtypeStruct((S, BW), BF16), jax.ShapeDtypeStruct((S, NH), F32)],
        compiler_params=_params(1), name="attn_b_fwd",
    )(q, kp, vp, sink)


def _merge_out(ya, yb, ga, gb, x, modv, wb, wout):
    S = x.shape[0]
    tm = min(256, S)

    def body(ya_ref, yb_ref, ga_ref, gb_ref, x_ref, mod_ref, wb_ref, wo_ref, x1_ref, mg_ref, ua_ref, ub_ref):
        ua = jnp.dot(ya_ref[...], wb_ref[0], preferred_element_type=F32)
        ub = jnp.dot(yb_ref[...], wb_ref[1], preferred_element_type=F32)
        merged = jax.nn.sigmoid(ga_ref[...].astype(F32)) * ua + jax.nn.sigmoid(gb_ref[...].astype(F32)) * ub
        mb = merged.astype(BF16)
        ua_ref[...] = ua.astype(BF16)
        ub_ref[...] = ub.astype(BF16)
        mg_ref[...] = mb
        x1_ref[...] = x_ref[...] + mod_ref[2:3, :] * jnp.dot(mb, wo_ref[...], preferred_element_type=F32)

    return pl.pallas_call(
        body, grid=(S // tm,),
        in_specs=[_rows(tm, BW), _rows(tm, BW), _rows(tm, D), _rows(tm, D), _rows(tm, D), _const((6, D)),
                  _const((2, BW, D)), _const((D, D))],
        out_specs=[_rows(tm, D)] * 4,
        out_shape=[jax.ShapeDtypeStruct((S, D), F32)] + [jax.ShapeDtypeStruct((S, D), BF16)] * 3,
        compiler_params=_params(1), name="merge_out",
    )(ya, yb, ga, gb, x, modv, wb, wout)


def _mlp_fwd(x1, modv, n2g, wmi, wmo, fg, target):
    S = x1.shape[0]
    tm = min(512, S)
    tf = wmi.shape[2]
    nj = wmi.shape[0]

    def body(x1_ref, mod_ref, g_ref, wi_ref, wo_ref, fg_ref, t_ref, h2_ref, hp_ref, dx2_ref, st_ref, acc_ref):
        i, j = pl.program_id(0), pl.program_id(1)

        @pl.when(j == 0)
        def _():
            xt = x1_ref[...]
            r = lax.rsqrt(jnp.mean(xt * xt, axis=-1, keepdims=True) + EPS)
            h2 = ((xt * r) * g_ref[...]) * (1.0 + mod_ref[4:5, :]) + mod_ref[3:4, :]
            h2_ref[...] = h2.astype(BF16)
            acc_ref[...] = jnp.zeros_like(acc_ref)

        @pl.when((i == 0) & (j == 0))
        def _():
            st_ref[...] = jnp.zeros_like(st_ref)

        hp = jnp.dot(h2_ref[...], wi_ref[...], preferred_element_type=F32)
        hp_ref[...] = hp.astype(BF16)
        hid = jnp.square(jnp.maximum(hp, 0.0))
        acc_ref[...] += jnp.dot(hid.astype(BF16), wo_ref[...], preferred_element_type=F32)

        @pl.when(j == nj - 1)
        def _():
            x2 = x1_ref[...] + mod_ref[5:6, :] * acc_ref[...]
            r3 = lax.rsqrt(jnp.mean(x2 * x2, axis=-1, keepdims=True) + EPS)
            xn = x2 * r3
            err = xn * fg_ref[...] - t_ref[...]
            dy = err * (1.0 / D)
            gy = dy * fg_ref[...]
            dx2_ref[...] = r3 * (gy - xn * jnp.mean(gy * xn, axis=-1, keepdims=True))
            st_ref[0:1, :] += jnp.sum(dy * xn, axis=0, keepdims=True)
            st_ref[1:2, :] += jnp.sum(err * err, axis=0, keepdims=True) * (0.5 / D)

    return pl.pallas_call(
        body, grid=(S // tm, nj),
        in_specs=[pl.BlockSpec((tm, D), lambda i, j: (i, 0)), _const((6, D)), _const((1, D)),
                  pl.BlockSpec((None, D, tf), lambda i, j: (j, 0, 0)), pl.BlockSpec((None, tf, D), lambda i, j: (j, 0, 0)),
                  _const((1, D)), pl.BlockSpec((tm, D), lambda i, j: (i, 0))],
        out_specs=[pl.BlockSpec((tm, D), lambda i, j: (i, 0)), pl.BlockSpec((tm, tf), lambda i, j: (i, j)),
                   pl.BlockSpec((tm, D), lambda i, j: (i, 0)), _const((8, D))],
        out_shape=[jax.ShapeDtypeStruct((S, D), BF16), jax.ShapeDtypeStruct((S, nj * tf), BF16),
                   jax.ShapeDtypeStruct((S, D), F32), jax.ShapeDtypeStruct((8, D), F32)],
        scratch_shapes=[pltpu.VMEM((tm, D), F32)],
        compiler_params=_params(2), name="mlp_fwd",
    )(x1, modv, n2g, wmi, wmo, fg, target)


def _mlp_bwd(dx2, x1, hp, modv, n2g, wmi, wmo):
    S = x1.shape[0]
    tm = min(512, S)
    tf = wmi.shape[2]
    nj = wmi.shape[0]

    def body(dx2_ref, x1_ref, hp_ref, mod_ref, g_ref, wi_ref, wo_ref, dhp_ref, dx1_ref, st_ref, dmo_ref, acc_ref):
        i, j = pl.program_id(0), pl.program_id(1)

        @pl.when(j == 0)
        def _():
            dmo_ref[...] = (mod_ref[5:6, :] * dx2_ref[...]).astype(BF16)
            acc_ref[...] = jnp.zeros_like(acc_ref)

        @pl.when((i == 0) & (j == 0))
        def _():
            st_ref[...] = jnp.zeros_like(st_ref)

        dhid = lax.dot_general(dmo_ref[...], wo_ref[...], NT, preferred_element_type=F32)
        dhp = (dhid * (2.0 * jnp.maximum(hp_ref[...].astype(F32), 0.0))).astype(BF16)
        dhp_ref[...] = dhp
        acc_ref[...] += lax.dot_general(dhp, wi_ref[...], NT, preferred_element_type=F32)

        @pl.when(j == nj - 1)
        def _():
            dh2 = acc_ref[...]
            xt = x1_ref[...]
            r = lax.rsqrt(jnp.mean(xt * xt, axis=-1, keepdims=True) + EPS)
            xn = xt * r
            st_ref[0:1, :] += jnp.sum(dh2, axis=0, keepdims=True)
            st_ref[1:2, :] += jnp.sum(dh2 * xn, axis=0, keepdims=True)
            dxn = dh2 * (g_ref[...] * (1.0 + mod_ref[4:5, :]))
            dx1_ref[...] = dx2_ref[...] + r * (dxn - xn * jnp.mean(dxn * xn, axis=-1, keepdims=True))

    return pl.pallas_call(
        body, grid=(S // tm, nj),
        in_specs=[pl.BlockSpec((tm, D), lambda i, j: (i, 0)), pl.BlockSpec((tm, D), lambda i, j: (i, 0)),
                  pl.BlockSpec((tm, tf), lambda i, j: (i, j)), _const((6, D)), _const((1, D)),
                  pl.BlockSpec((None, D, tf), lambda i, j: (j, 0, 0)), pl.BlockSpec((None, tf, D), lambda i, j: (j, 0, 0))],
        out_specs=[pl.BlockSpec((tm, tf), lambda i, j: (i, j)), pl.BlockSpec((tm, D), lambda i, j: (i, 0)), _const((8, D))],
        out_shape=[jax.ShapeDtypeStruct((S, nj * tf), BF16), jax.ShapeDtypeStruct((S, D), F32), jax.ShapeDtypeStruct((8, D), F32)],
        scratch_shapes=[pltpu.VMEM((tm, D), BF16), pltpu.VMEM((tm, D), F32)],
        compiler_params=_params(2), name="mlp_bwd",
    )(dx2, x1, hp, modv, n2g, wmi, wmo)


def _tn_matmul(a, b, tk, tn, name, relu_sq=False, dev_major=False):
    S, K = a.shape
    N = b.shape[1]
    ts = min(512, S)
    ns = S // ts

    def body(a_ref, b_ref, o_ref):
        @pl.when(pl.program_id(2) == 0)
        def _():
            o_ref[...] = jnp.zeros_like(o_ref)

        at = a_ref[...]
        if relu_sq:
            at = jnp.square(jnp.maximum(at.astype(F32), 0.0)).astype(BF16)
        o_ref[...] += lax.dot_general(at, b_ref[...].astype(BF16), TN, preferred_element_type=F32)

    if dev_major:
        out_spec = pl.BlockSpec((None, tk, tn), lambda k, n, s: (n, k, 0))
        out_shape = jax.ShapeDtypeStruct((N // tn, K, tn), F32)
    else:
        out_spec = pl.BlockSpec((tk, tn), lambda k, n, s: (k, n))
        out_shape = jax.ShapeDtypeStruct((K, N), F32)
    return pl.pallas_call(
        body, grid=(K // tk, N // tn, ns),
        in_specs=[pl.BlockSpec((ts, tk), lambda k, n, s: (s, k)), pl.BlockSpec((ts, tn), lambda k, n, s: (s, n))],
        out_specs=out_spec, out_shape=out_shape,
        compiler_params=_params(3), name=name,
    )(a, b)


def _scale_gate(m, w, g, row, name):
    K = m.shape[0]
    tk = min(512, K)

    def body(m_ref, w_ref, mod_ref, dw_ref, dg_ref):
        @pl.when(pl.program_id(0) == 0)
        def _():
            dg_ref[...] = jnp.zeros_like(dg_ref)

        mt = m_ref[...]
        dw_ref[...] = mt * mod_ref[row:row + 1, :]
        dg_ref[0:1, :] += jnp.sum(mt * w_ref[...].astype(F32), axis=0, keepdims=True)

    return pl.pallas_call(
        body, grid=(K // tk,),
        in_specs=[_rows(tk, D), _rows(tk, D), _const((6, D))],
        out_specs=[_rows(tk, D), _const((8, D))],
        out_shape=[jax.ShapeDtypeStruct((K, D), F32), jax.ShapeDtypeStruct((8, D), F32)],
        compiler_params=_params(1), name=name,
    )(m, w, g)


def _merge_bwd(dx1, modv, ga, gb, ua, ub, ya, yb, wb, wout):
    S = dx1.shape[0]
    tm = min(256, S)

    def body(dx1_ref, mod_ref, ga_ref, gb_ref, ua_ref, ub_ref, ya_ref, yb_ref, wb_ref, wo_ref,
             dua_ref, dub_ref, dga_ref, dgb_ref, dya_ref, dyb_ref, dla_ref, dlb_ref):
        dao = (mod_ref[2:3, :] * dx1_ref[...]).astype(BF16)
        dm = lax.dot_general(dao, wo_ref[...], NT, preferred_element_type=F32)
        r = lax.broadcasted_iota(jnp.int32, (BW, NH), 0) // HD
        c = lax.broadcasted_iota(jnp.int32, (BW, NH), 1)
        head_of = (r == c).astype(F32)
        for br, (g_ref, u_ref, y_ref, du_ref, dg_ref, dy_ref, dl_ref) in enumerate((
                (ga_ref, ua_ref, ya_ref, dua_ref, dga_ref, dya_ref, dla_ref),
                (gb_ref, ub_ref, yb_ref, dub_ref, dgb_ref, dyb_ref, dlb_ref))):
            sg = jax.nn.sigmoid(g_ref[...].astype(F32))
            du = (dm * sg).astype(BF16)
            du_ref[...] = du
            dg_ref[...] = (dm * u_ref[...].astype(F32) * sg * (1.0 - sg)).astype(BF16)
            dy = lax.dot_general(du, wb_ref[br], NT, preferred_element_type=F32)
            dyb16 = dy.astype(BF16)
            dy_ref[...] = dyb16
            dl_ref[...] = jnp.dot(dyb16.astype(F32) * y_ref[...].astype(F32), head_of,
                                  preferred_element_type=F32, precision=lax.Precision.HIGHEST)

    return pl.pallas_call(
        body, grid=(S // tm,),
        in_specs=[_rows(tm, D), _const((6, D)), _rows(tm, D), _rows(tm, D), _rows(tm, D), _rows(tm, D),
                  _rows(tm, BW), _rows(tm, BW), _const((2, BW, D)), _const((D, D))],
        out_specs=[_rows(tm, D)] * 4 + [_rows(tm, BW)] * 2 + [_rows(tm, NH)] * 2,
        out_shape=[jax.ShapeDtypeStruct((S, D), BF16)] * 4 + [jax.ShapeDtypeStruct((S, BW), BF16)] * 2
        + [jax.ShapeDtypeStruct((S, NH), F32)] * 2,
        compiler_params=_params(1), name="merge_bwd",
    )(dx1, modv, ga, gb, ua, ub, ya, yb, wb, wout)


def _attn_a_bwd(q, qt, k, kt3, v, do, dot_, lse, delta):
    S = q.shape[0]
    tq = min(512, S)
    tk = min(512, S)
    nq, nk = S // tq, S // tk

    def body(q_ref, qt_ref, do_ref, dot_ref, lse_ref, dl_ref, k_ref, v_ref, kt_ref, dq_ref, dk_hbm, dv_hbm, dk_sc, dv_sc, sem):
        i = pl.program_id(0)

        @pl.when(i == 0)
        def _():
            dk_sc[...] = jnp.zeros_like(dk_sc)
            dv_sc[...] = jnp.zeros_like(dv_sc)

        for kv in range(2):
            qg = q_ref[:, 256 * kv:256 * kv + 256]
            dog = do_ref[:, 256 * kv:256 * kv + 256]
            heads = []
            for t in range(GRP):
                h = GRP * kv + t
                heads.append((qg[:, HD * t:HD * t + HD], dog[:, HD * t:HD * t + HD],
                              qt_ref[HD * h:HD * h + HD, :], dot_ref[HD * h:HD * h + HD, :],
                              lse_ref[:, h:h + 1], dl_ref[:, h:h + 1]))

            def step(j, carry, kv=kv, heads=heads):
                off = pl.multiple_of(j * tk, tk)
                kj = k_ref[pl.ds(off, tk), :][:, HD * kv:HD * kv + HD]
                vj = v_ref[pl.ds(off, tk), :][:, HD * kv:HD * kv + HD]
                kjt = kt_ref[j, HD * kv:HD * kv + HD, :]
                dkt = jnp.zeros((HD, tk), F32)
                dvt = jnp.zeros((HD, tk), F32)
                new = []

                def logits(t):
                    return (lax.dot_general(heads[t][0], kj, NT, preferred_element_type=F32),
                            lax.dot_general(heads[t][1], vj, NT, preferred_element_type=F32))

                sd = [logits(0)]
                for t, (qh, doh, qth, doth, lse_h, dl_h) in enumerate(heads):
                    if t + 1 < GRP:
                        sd.append(logits(t + 1))
                    s, dp = sd[t]
                    pm = jnp.exp2(s - lse_h)
                    ds = (pm * (dp - dl_h)).astype(BF16)
                    dvt = dvt + jnp.dot(doth, pm.astype(BF16), preferred_element_type=F32)
                    dkt = dkt + jnp.dot(qth, ds, preferred_element_type=F32)
                    new.append(carry[t] + lax.dot_general(kjt, ds, NT, preferred_element_type=F32))
                dk_sc[j, HD * kv:HD * kv + HD, :] += dkt
                dv_sc[j, HD * kv:HD * kv + HD, :] += dvt
                return tuple(new)

            res = lax.fori_loop(0, nk, step, (jnp.zeros((HD, tq), F32),) * GRP)
            for t in range(GRP):
                dq_ref[HD * (GRP * kv + t):HD * (GRP * kv + t) + HD, :] = res[t]

        @pl.when(i == nq - 1)
        def _():
            c1 = pltpu.make_async_copy(dk_sc, dk_hbm, sem.at[0])
            c2 = pltpu.make_async_copy(dv_sc, dv_hbm, sem.at[1])
            c1.start()
            c2.start()
            c1.wait()
            c2.wait()

    any_spec = pl.BlockSpec(memory_space=pl.ANY)
    cols = pl.BlockSpec((BW, tq), lambda i: (0, i))
    return pl.pallas_call(
        body, grid=(nq,),
        in_specs=[_rows(tq, BW), cols, _rows(tq, BW), cols, _rows(tq, NH), _rows(tq, NH), _const((S, 128)), _const((S, 128)),
                  _const((nk, 128, tk))],
        out_specs=[cols, any_spec, any_spec],
        out_shape=[jax.ShapeDtypeStruct((BW, S), F32), jax.ShapeDtypeStruct((nk, 128, tk), F32),
                   jax.ShapeDtypeStruct((nk, 128, tk), F32)],
        scratch_shapes=[pltpu.VMEM((nk, 128, tk), F32), pltpu.VMEM((nk, 128, tk), F32), pltpu.SemaphoreType.DMA((2,))],
        compiler_params=_params(1), name="attn_a_bwd",
    )(q, qt, do, dot_, lse, delta, k, v, kt3)


def _attn_b_bwd(q, kp, vp, sink, do, lse, delta):
    S = q.shape[0]
    tq = min(512, S)
    W = tq + 2 * WIN
    nq = S // tq

    def body(q_ref, k_ref, v_ref, sink_ref, do_ref, lse_ref, dl_ref, dq_ref, dk_hbm, dv_hbm, ds_ref, dk_sc, dv_sc, sem):
        i = pl.program_id(0)

        @pl.when(i == 0)
        def _():
            dk_sc[...] = jnp.zeros_like(dk_sc)
            dv_sc[...] = jnp.zeros_like(dv_sc)
            ds_ref[...] = jnp.zeros_like(ds_ref)

        off = pl.multiple_of(i * tq, tq)
        valid = _window_mask(i, tq, S)
        kw = k_ref[pl.ds(off, W), :]
        vw = v_ref[pl.ds(off, W), :]
        lse_i = lse_ref[...]
        dl_i = dl_ref[...]
        dks, dvs = [], []
        for kv in range(2):
            kh = kw[:, HD * kv:HD * kv + HD]
            vh = vw[:, HD * kv:HD * kv + HD]
            dk = jnp.zeros((W, HD), F32)
            dv = jnp.zeros((W, HD), F32)
            for pp in range(2):
                p = 2 * kv + pp
                qp = q_ref[:, 128 * p:128 * p + 128]
                dop = do_ref[:, 128 * p:128 * p + 128]
                dqs = []
                for hh in range(2):
                    h = 2 * p + hh
                    qh = qp[:, HD * hh:HD * hh + HD]
                    doh = dop[:, HD * hh:HD * hh + HD]
                    s = jnp.where(valid, lax.dot_general(qh, kh, NT, preferred_element_type=F32), NEG)
                    pm = jnp.exp(s - lse_i[:, h:h + 1])
                    dp = lax.dot_general(doh, vh, NT, preferred_element_type=F32)
                    ds = (pm * (dp - dl_i[:, h:h + 1])).astype(BF16)
                    dv = dv + lax.dot_general(pm.astype(BF16), doh, TN, preferred_element_type=F32)
                    dk = dk + lax.dot_general(ds, qh, TN, preferred_element_type=F32)
                    dqs.append(jnp.dot(ds, kh, preferred_element_type=F32))
                dq_ref[:, 128 * p:128 * p + 128] = jnp.concatenate(dqs, axis=1)
            dks.append(dk)
            dvs.append(dv)
        dk_sc[pl.ds(off, W), :] += jnp.concatenate(dks, axis=1)
        dv_sc[pl.ds(off, W), :] += jnp.concatenate(dvs, axis=1)
        psd = jnp.exp(sink_ref[...] - lse_i) * dl_i
        r = lax.broadcasted_iota(jnp.int32, (NH, 128), 0)
        c = lax.broadcasted_iota(jnp.int32, (NH, 128), 1)
        row = jnp.dot(jnp.sum(psd, axis=0, keepdims=True), (r == c).astype(F32),
                      preferred_element_type=F32, precision=lax.Precision.HIGHEST)
        ds_ref[...] -= jnp.broadcast_to(row, (8, 128))

        @pl.when(i == nq - 1)
        def _():
            c1 = pltpu.make_async_copy(dk_sc, dk_hbm, sem.at[0])
            c2 = pltpu.make_async_copy(dv_sc, dv_hbm, sem.at[1])
            c1.start()
            c2.start()
            c1.wait()
            c2.wait()

    any_spec = pl.BlockSpec(memory_space=pl.ANY)
    return pl.pallas_call(
        body, grid=(nq,),
        in_specs=[_rows(tq, BW), _const((S + 2 * WIN, 128)), _const((S + 2 * WIN, 128)), _const((1, NH)),
                  _rows(tq, BW), _rows(tq, NH), _rows(tq, NH)],
        out_specs=[_rows(tq, BW), any_spec, any_spec, _const((8, 128))],
        out_shape=[jax.ShapeDtypeStruct((S, BW), F32), jax.ShapeDtypeStruct((S + 2 * WIN, 128), F32),
                   jax.ShapeDtypeStruct((S + 2 * WIN, 128), F32), jax.ShapeDtypeStruct((8, 128), F32)],
        scratch_shapes=[pltpu.VMEM((S + 2 * WIN, 128), F32), pltpu.VMEM((S + 2 * WIN, 128), F32), pltpu.SemaphoreType.DMA((2,))],
        compiler_params=_params(1), name="attn_b_bwd",
    )(q, kp, vp, sink, do, lse, delta)


def _qk_bwd(dqa, dka, dva, dqb, dkb, dvb, qar, kar, qg2, kg2, tab_a, tab_b, dga, dgb):
    S = dqa.shape[0]
    tm = min(256, S)

    def body(dqa_ref, dka_ref, dva_ref, dqb_ref, dkb_ref, dvb_ref, qar_ref, kar_ref, qg_ref, kg_ref, ta_ref, tb_ref,
             dga_ref, dgb_ref, dp_ref, st_ref):
        @pl.when(pl.program_id(0) == 0)
        def _():
            st_ref[...] = jnp.zeros_like(st_ref)

        seg = _seg_matrix(128, HD)

        def norm_bwd(dz_rot, raw, g):
            dzn = _rope_t(dz_rot, ta_ref, 16)
            raw = raw.astype(F32)
            rr = lax.rsqrt(_seg_sum(raw * raw, seg) * (1.0 / HD) + EPS)
            zhat = raw * rr
            dzh = dzn * g
            draw = rr * (dzh - zhat * (_seg_sum(dzh * zhat, seg) * (1.0 / HD)))
            return draw, jnp.sum(dzn * zhat, axis=0, keepdims=True)

        gq = jnp.zeros((1, 128), F32)
        for p in range(4):
            sl = slice(128 * p, 128 * p + 128)
            draw, gsum = norm_bwd(dqa_ref[:, sl] * 0.125, qar_ref[:, sl], qg_ref[...])
            gq = gq + gsum
            dp_ref[:, sl] = draw.astype(BF16)
            dp_ref[:, 768 + 128 * p:768 + 128 * p + 128] = _rope_t(dqb_ref[:, sl] * 0.125, tb_ref, 32).astype(BF16)
        draw, gk = norm_bwd(dka_ref[...] * LN2, kar_ref[...], kg_ref[...])
        dp_ref[:, 512:640] = draw.astype(BF16)
        dp_ref[:, 640:768] = dva_ref[...].astype(BF16)
        dp_ref[:, 1280:1408] = _rope_t(dkb_ref[...], tb_ref, 32).astype(BF16)
        dp_ref[:, 1408:1536] = dvb_ref[...].astype(BF16)
        dp_ref[:, 1536:2560] = dga_ref[...]
        dp_ref[:, 2560:3584] = dgb_ref[...]
        st_ref[0:1, :] += gq
        st_ref[1:2, :] += gk

    tab = pl.BlockSpec((3, tm, 128), lambda i: (0, i, 0))
    return pl.pallas_call(
        body, grid=(S // tm,),
        in_specs=[_rows(tm, BW), _rows(tm, 128), _rows(tm, 128), _rows(tm, BW), _rows(tm, 128), _rows(tm, 128),
                  _rows(tm, BW), _rows(tm, 128), _const((1, 128)), _const((1, 128)), tab, tab, _rows(tm, D), _rows(tm, D)],
        out_specs=[_rows(tm, INW), _const((8, 128))],
        out_shape=[jax.ShapeDtypeStruct((S, INW), BF16), jax.ShapeDtypeStruct((8, 128), F32)],
        compiler_params=_params(1), name="qk_bwd",
    )(dqa, dka, dva, dqb, dkb, dvb, qar, kar, qg2, kg2, tab_a, tab_b, dga, dgb)


def _in_bwd(dproj, win, x, dx1, modv, n1g):
    S = x.shape[0]
    tm = min(256, S)

    def body(dp_ref, w_ref, x_ref, dx1_ref, mod_ref, g_ref, gx_ref, st_ref):
        @pl.when(pl.program_id(0) == 0)
        def _():
            st_ref[...] = jnp.zeros_like(st_ref)

        dh = lax.dot_general(dp_ref[...], w_ref[...], NT, preferred_element_type=F32)
        xt = x_ref[...]
        r = lax.rsqrt(jnp.mean(xt * xt, axis=-1, keepdims=True) + EPS)
        xn = xt * r
        st_ref[0:1, :] += jnp.sum(dh, axis=0, keepdims=True)
        st_ref[1:2, :] += jnp.sum(dh * xn, axis=0, keepdims=True)
        dxn = dh * (g_ref[...] * (1.0 + mod_ref[1:2, :]))
        gx_ref[...] = dx1_ref[...] + r * (dxn - xn * jnp.mean(dxn * xn, axis=-1, keepdims=True))

    return pl.pallas_call(
        body, grid=(S // tm,),
        in_specs=[_rows(tm, INW), _const((D, INW)), _rows(tm, D), _rows(tm, D), _const((6, D)), _const((1, D))],
        out_specs=[_rows(tm, D), _const((8, D))],
        out_shape=[jax.ShapeDtypeStruct((S, D), F32), jax.ShapeDtypeStruct((8, D), F32)],
        compiler_params=_params(1), name="in_bwd",
    )(dproj, win, x, dx1, modv, n1g)


def _pack_small(st1, st2, stf, dg1, dg2, stqk, dsink, modv, n1g, n2g):
    def body(st1_ref, st2_ref, stf_ref, dg1_ref, dg2_ref, qk_ref, ds_ref, mod_ref, g1_ref, g2_ref, o_ref):
        a1, b1 = st1_ref[0:1, :], st1_ref[1:2, :]
        a2, b2 = st2_ref[0:1, :], st2_ref[1:2, :]
        r = lax.broadcasted_iota(jnp.int32, (128, D), 0)
        c = lax.broadcasted_iota(jnp.int32, (128, D), 1)
        fold_q = (c == r % HD).astype(F32)
        fold_k = (c == HD + r % HD).astype(F32)
        keep = (c == r).astype(F32)

        def place(v, sel):
            return jnp.dot(v, sel, preferred_element_type=F32, precision=lax.Precision.HIGHEST)

        loss = jnp.sum(stf_ref[1:2, :], axis=1, keepdims=True)
        lane = lax.broadcasted_iota(jnp.int32, (1, D), 1)
        rows = [a1, g1_ref[...] * b1, dg1_ref[0:1, :], a2, g2_ref[...] * b2, dg2_ref[0:1, :],
                (1.0 + mod_ref[1:2, :]) * b1, (1.0 + mod_ref[4:5, :]) * b2, stf_ref[0:1, :],
                place(qk_ref[0:1, :], fold_q) + place(qk_ref[1:2, :], fold_k),
                place(ds_ref[0:1, :], keep),
                jnp.where(lane == 0, loss, 0.0)]
        rows += [jnp.zeros((1, D), F32)] * (SMALL_ROWS - len(rows))
        for n, v in enumerate(rows):
            o_ref[n:n + 1, :] = v

    return pl.pallas_call(
        body, out_shape=jax.ShapeDtypeStruct((SMALL_ROWS, D), F32),
        compiler_params=pltpu.CompilerParams(vmem_limit_bytes=V7X_VMEM_LIMIT), name="pack_small",
    )(st1, st2, stf, dg1, dg2, stqk, dsink, modv, n1g, n2g)


def _wada_grad(silu_all, dmod_cols):
    def body(a_ref, b_ref, o_ref):
        o_ref[...] = lax.dot_general(a_ref[...], b_ref[...], TN, preferred_element_type=F32, precision=lax.Precision.HIGHEST)

    return pl.pallas_call(
        body, out_shape=jax.ShapeDtypeStruct((D, dmod_cols.shape[1]), F32),
        compiler_params=pltpu.CompilerParams(vmem_limit_bytes=V7X_VMEM_LIMIT), name="wada_grad",
    )(silu_all, dmod_cols)


def _adamw_sum(parts, w, m, v, name):
    R, C = w.shape
    tr = R if R <= 512 else next(t for t in (512, 432, 256, 216, 128, 64, 8) if R % t == 0)
    n = len(parts)
    dyn = [idx for _, idx in parts if idx is not None and not isinstance(idx, int)]
    b1c = 1.0 - ADAM_B1 ** ADAM_STEP
    b2c = 1.0 - ADAM_B2 ** ADAM_STEP

    def body(*refs):
        refs = refs[len(dyn):]
        g = refs[0][...]
        for k in range(1, n):
            g = g + refs[k][...]
        w_ref, m_ref, v_ref, g_out, d_out, m_out, v_out = refs[n:]
        mn = ADAM_B1 * m_ref[...] + (1.0 - ADAM_B1) * g
        vn = ADAM_B2 * v_ref[...] + (1.0 - ADAM_B2) * jnp.square(g)
        g_out[...] = g
        m_out[...] = mn
        v_out[...] = vn
        d_out[...] = -ADAM_LR * ((mn / b1c) / (jnp.sqrt(vn / b2c) + ADAM_EPS) + ADAM_WD * w_ref[...])

    in_specs = []
    nd = 0
    for a, idx in parts:
        if idx is None:
            in_specs.append(pl.BlockSpec((tr, C), lambda i, *s: (i, 0)))
        elif isinstance(idx, int):
            in_specs.append(pl.BlockSpec((None, tr, C), lambda i, *s, idx=idx: (idx, i, 0)))
        else:
            in_specs.append(pl.BlockSpec((None, tr, C), lambda i, *s, nd=nd: (s[nd][0], i, 0)))
            nd += 1
    blk = pl.BlockSpec((tr, C), lambda i, *s: (i, 0))
    grid_spec = pltpu.PrefetchScalarGridSpec(
        num_scalar_prefetch=len(dyn), grid=(R // tr,), in_specs=in_specs + [blk] * 3, out_specs=[blk] * 4)
    return pl.pallas_call(
        body, grid_spec=grid_spec, out_shape=[jax.ShapeDtypeStruct((R, C), F32)] * 4,
        compiler_params=_params(1), name=name,
    )(*dyn, *[a for a, _ in parts], w, m, v)


def _pair_sum(g4, recv, core):
    tr = 216

    def body(c_ref, a_ref, b_ref, o_ref):
        o_ref[...] = a_ref[...] + b_ref[...]

    grid_spec = pltpu.PrefetchScalarGridSpec(
        num_scalar_prefetch=1, grid=(4, ROWS_PACK // tr),
        in_specs=[pl.BlockSpec((None, None, tr, D), lambda ch, i, c: (ch, c[0], i, 0)),
                  pl.BlockSpec((None, tr, D), lambda ch, i, c: (ch, i, 0))],
        out_specs=pl.BlockSpec((None, tr, D), lambda ch, i, c: (ch, i, 0)))
    return pl.pallas_call(
        body, grid_spec=grid_spec, out_shape=jax.ShapeDtypeStruct((4, ROWS_PACK, D), F32),
        compiler_params=_params(2), name="pair_sum",
    )(core, g4, recv)


def _me():
    return lax.axis_index("x"), lax.axis_index("y"), lax.axis_index("c")


def _peer(k):
    x, y, c = _me()
    return (x ^ ((k >> 2) & 1), y ^ ((k >> 1) & 1), c ^ (k & 1))


def _ada_exchange(c_row, w_ada, b_rows):
    NW = w_ada.shape[1]

    def body(c_ref, w_ref, b_ref, sall_ref, mod_ref, src_ref, mp_ref, send1, recv1, send2, recv2):
        x, y, c = _me()
        me = 4 * x + 2 * y + c
        cv = c_ref[...]
        src_ref[...] = jnp.broadcast_to(cv * jax.nn.sigmoid(cv), (8, D))
        mine = pl.ds(pl.multiple_of(me * 8, 8), 8)
        sall_ref[mine, :] = src_ref[...]
        sends = [pltpu.make_async_remote_copy(src_ref, sall_ref.at[mine, :], send1.at[k - 1], recv1.at[k - 1],
                                              device_id=_peer(k), device_id_type=MESH) for k in range(1, N_DEV)]
        for cp in sends:
            cp.start()
        for k in range(1, N_DEV):
            theirs = pl.ds(pl.multiple_of((me ^ k) * 8, 8), 8)
            pltpu.make_async_remote_copy(src_ref, sall_ref.at[theirs, :], send1.at[k - 1], recv1.at[k - 1],
                                         device_id=_peer(k), device_id_type=MESH).wait_recv()
        for cp in sends:
            cp.wait_send()
        mp_ref[...] = jnp.dot(sall_ref[...], w_ref[...], preferred_element_type=F32, precision=lax.Precision.HIGHEST)
        mod_ref[mine, :] = mp_ref[mine, :] + b_ref[mine, :]
        sends = []
        for k in range(1, N_DEV):
            theirs = pl.ds(pl.multiple_of((me ^ k) * 8, 8), 8)
            sends.append(pltpu.make_async_remote_copy(mp_ref.at[theirs, :], mod_ref.at[mine, :], send2.at[k - 1], recv2.at[k - 1],
                                                      device_id=_peer(k), device_id_type=MESH))
        for cp in sends:
            cp.start()
        for k in range(1, N_DEV):
            theirs = pl.ds(pl.multiple_of((me ^ k) * 8, 8), 8)
            pltpu.make_async_remote_copy(mp_ref.at[mine, :], mod_ref.at[theirs, :], send2.at[k - 1], recv2.at[k - 1],
                                         device_id=_peer(k), device_id_type=MESH).wait_recv()
            mod_ref[theirs, :] = mod_ref[theirs, :] + b_ref[theirs, :]
        for cp in sends:
            cp.wait_send()

    vm = pl.BlockSpec(memory_space=pltpu.VMEM)
    return pl.pallas_call(
        body, in_specs=[vm, vm, vm], out_specs=[vm, vm],
        out_shape=[jax.ShapeDtypeStruct((8 * N_DEV, D), F32), jax.ShapeDtypeStruct((8 * N_DEV, NW), F32)],
        scratch_shapes=[pltpu.VMEM((8, D), F32), pltpu.VMEM((8 * N_DEV, NW), F32)]
        + [pltpu.SemaphoreType.DMA((N_DEV - 1,))] * 4,
        compiler_params=pltpu.CompilerParams(vmem_limit_bytes=V7X_VMEM_LIMIT), name="ada_exchange",
    )(c_row, w_ada, b_rows)


def _weight_gather(shard):
    def body(x_ref, out_ref, send_sems, recv_sems, local_sem):
        x, y, c = _me()
        me, sibling = (x, y, c), (x, y, 1 - c)
        chips = [(1 - x, y), (x, 1 - y), (1 - x, 1 - y)]

        def slot(px, py, pc):
            return out_ref.at[4 * px + 2 * py + pc]

        def copy(k, block, to, src=None):
            return pltpu.make_async_remote_copy(
                src_ref=slot(*block) if src is None else src, dst_ref=slot(*block),
                send_sem=send_sems.at[k], recv_sem=recv_sems.at[k], device_id=to, device_id_type=MESH)

        mine = pltpu.make_async_copy(x_ref, slot(*me), local_sem)
        mine.start()
        first = [copy(0, me, sibling, src=x_ref)]
        first += [copy(1 + j, me, (*chip, c), src=x_ref) for j, chip in enumerate(chips)]
        for cp in first:
            cp.start()
        passed = [copy(4 + j, (*chip, c), sibling) for j, chip in enumerate(chips)]
        for j, chip in enumerate(chips):
            copy(1 + j, (*chip, c), me).wait_recv()
            passed[j].start()
        copy(0, sibling, me).wait_recv()
        for j, chip in enumerate(chips):
            copy(4 + j, (*chip, 1 - c), me).wait_recv()
        for cp in first + passed:
            cp.wait_send()
        mine.wait()

    any_spec = pl.BlockSpec(memory_space=pl.ANY)
    return pl.pallas_call(
        body, in_specs=[any_spec], out_specs=any_spec,
        out_shape=jax.ShapeDtypeStruct((N_DEV,) + shard.shape, shard.dtype),
        scratch_shapes=[pltpu.SemaphoreType.DMA((7,)), pltpu.SemaphoreType.DMA((7,)), pltpu.SemaphoreType.DMA],
        name="weight_gather",
    )(shard)


def _grad_rs_sibling(g4):
    def body(g_ref, recv_ref, send_sems, recv_sems):
        x, y, c = _me()
        sibling = (x, y, 1 - c)
        copies = [pltpu.make_async_remote_copy(g_ref.at[ch, 1 - c], recv_ref.at[ch], send_sems.at[ch], recv_sems.at[ch],
                                               device_id=sibling, device_id_type=MESH) for ch in range(4)]
        for cp in copies:
            cp.start()
        for cp in copies:
            cp.wait_recv()
        for cp in copies:
            cp.wait_send()

    any_spec = pl.BlockSpec(memory_space=pl.ANY)
    return pl.pallas_call(
        body, in_specs=[any_spec], out_specs=any_spec,
        out_shape=jax.ShapeDtypeStruct((4,) + g4.shape[2:], g4.dtype),
        scratch_shapes=[pltpu.SemaphoreType.DMA((4,)), pltpu.SemaphoreType.DMA((4,))],
        name="grad_rs_sibling",
    )(g4)


def _grad_rs_chips(t4):
    def body(t_ref, recv_ref, send_sems, recv_sems):
        x, y, c = _me()
        chip = 2 * x + y
        copies = []
        for k in range(1, 4):
            to = (x ^ (k >> 1), y ^ (k & 1), c)
            copies.append(pltpu.make_async_remote_copy(t_ref.at[chip ^ k], recv_ref.at[k - 1], send_sems.at[k - 1], recv_sems.at[k - 1],
                                                       device_id=to, device_id_type=MESH))
        for cp in copies:
            cp.start()
        for cp in copies:
            cp.wait_recv()
        for cp in copies:
            cp.wait_send()

    any_spec = pl.BlockSpec(memory_space=pl.ANY)
    return pl.pallas_call(
        body, in_specs=[any_spec], out_specs=any_spec,
        out_shape=jax.ShapeDtypeStruct((3,) + t4.shape[1:], t4.dtype),
        scratch_shapes=[pltpu.SemaphoreType.DMA((3,)), pltpu.SemaphoreType.DMA((3,))],
        name="grad_rs_chips",
    )(t4)


def _small_gather(block):
    def body(b_ref, out_ref, send_sems, recv_sems):
        x, y, c = _me()
        me = 4 * x + 2 * y + c
        out_ref[me] = b_ref[...]
        sends = [pltpu.make_async_remote_copy(b_ref, out_ref.at[me], send_sems.at[k - 1], recv_sems.at[k - 1],
                                              device_id=_peer(k), device_id_type=MESH) for k in range(1, N_DEV)]
        for cp in sends:
            cp.start()
        for k in range(1, N_DEV):
            pltpu.make_async_remote_copy(b_ref, out_ref.at[me ^ k], send_sems.at[k - 1], recv_sems.at[k - 1],
                                         device_id=_peer(k), device_id_type=MESH).wait_recv()
        for cp in sends:
            cp.wait_send()

    vm = pl.BlockSpec(memory_space=pltpu.VMEM)
    return pl.pallas_call(
        body, in_specs=[vm], out_specs=vm,
        out_shape=jax.ShapeDtypeStruct((N_DEV,) + block.shape, block.dtype),
        scratch_shapes=[pltpu.SemaphoreType.DMA((N_DEV - 1,)), pltpu.SemaphoreType.DMA((N_DEV - 1,))],
        name="small_gather",
    )(block)


def _pack_shards(w_in, w_branch, w_out, w_mlp_in, w_mlp_out):
    return jnp.concatenate([w_in.reshape(-1, D), w_branch.reshape(-1, D), w_out.reshape(-1, D),
                            w_mlp_in.reshape(-1, D), w_mlp_out.reshape(-1, D)], axis=0)


def _unpack_shards(p):
    return (p[0:448].reshape(1, D, 448), p[448:576].reshape(1, 2, BW, 128), p[576:704].reshape(1, 128, D),
            p[704:1216].reshape(1, D, 512), p[1216:1728].reshape(1, 512, D))


def _pack_small_params(b_ada, n1, n2, fg, qn, kn, sink):
    z = jnp.zeros((SMALL_ROWS, D), F32)
    z = z.at[0:6].set(b_ada.reshape(6, D)).at[6].set(n1.reshape(D)).at[7].set(n2.reshape(D)).at[8].set(fg.reshape(D))
    z = z.at[9, 0:HD].set(qn.reshape(HD)).at[9, HD:2 * HD].set(kn.reshape(HD)).at[10, 0:NH].set(sink.reshape(NH))
    return z


def _unpack_small(p):
    return (p[0:6].reshape(1, 6 * D), p[6].reshape(1, D), p[9, 0:HD].reshape(1, HD), p[9, HD:2 * HD].reshape(1, HD),
            p[10, 0:NH].reshape(1, NH), p[7].reshape(1, D), p[8].reshape(D))


def kernel(x, c, w_ada, b_ada, norm1_g, w_in, q_norm_a, k_norm_a, sink_b, w_branch, w_out, norm2_g, w_mlp_in, w_mlp_out, final_g, loss_target, m_w_ada, m_b_ada, m_norm1_g, m_w_in, m_q_norm_a, m_k_norm_a, m_sink_b, m_w_branch, m_w_out, m_norm2_g, m_w_mlp_in, m_w_mlp_out, m_final_g, v_w_ada, v_b_ada, v_norm1_g, v_w_in, v_q_norm_a, v_k_norm_a, v_sink_b, v_w_branch, v_w_out, v_norm2_g, v_w_mlp_in, v_w_mlp_out, v_final_g):
    S = x.shape[1]
    xs = x.reshape(S, D)
    tgt = loss_target.reshape(S, D)
    ax, ay, ac = lax.axis_index("x"), lax.axis_index("y"), lax.axis_index("c")
    me = 4 * ax + 2 * ay + ac
    NW = w_ada.shape[2]

    silu64, mod64 = _ada_exchange(c.reshape(1, D), w_ada.reshape(D, NW),
                                  jnp.repeat(b_ada.reshape(N_DEV, NW), 8, axis=0))
    silu_all = silu64[0::8]
    modv = mod64[0::8].reshape(6, D)

    gathered = _weight_gather(_pack_shards(w_in[0], w_branch[0], w_out[0], w_mlp_in[0], w_mlp_out[0]).astype(BF16))
    win = gathered[:, 0:448].reshape(N_DEV, D, 448).transpose(1, 0, 2).reshape(D, INW)
    wb = gathered[:, 448:576].reshape(N_DEV, 2, BW, 128).transpose(1, 2, 0, 3).reshape(2, BW, D)
    wout = gathered[:, 576:704].reshape(D, D)
    wmi = gathered[:, 704:1216].reshape(N_DEV, D, 512)
    wmo = gathered[:, 1216:1728]

    tab_a, tab_b = _rope_tables(S)
    qg2 = jnp.tile(q_norm_a.reshape(1, HD), (1, 2))
    kg2 = jnp.tile(k_norm_a.reshape(1, HD), (1, 2))
    n1g = norm1_g.reshape(1, D)
    n2g = norm2_g.reshape(1, D)
    fg = final_g.reshape(1, D)
    sink = sink_b.reshape(1, NH)

    h, qar, kar, qa, ka, va, qb, kb, vb, ga, gb = _in_proj(xs, modv, n1g, win, qg2, kg2, tab_a, tab_b)
    tk = min(512, S)
    ya_t, lse_at = _attn_a_fwd(qa, ka, va.reshape(S // tk, tk, 128).transpose(0, 2, 1))
    ya, lse_a = ya_t.T, lse_at.T
    pad = ((WIN, WIN), (0, 0))
    kbp, vbp = jnp.pad(kb, pad), jnp.pad(vb, pad)
    yb, lse_b = _attn_b_fwd(qb, kbp, vbp, sink)
    x1, merged, ua, ub = _merge_out(ya, yb, ga, gb, xs, modv, wb, wout)
    h2, hp, dx2, stf = _mlp_fwd(x1, modv, n2g, wmi, wmo, fg, tgt)

    dhp, dx1, st2 = _mlp_bwd(dx2, x1, hp, modv, n2g, wmi, wmo)
    m2 = _tn_matmul(hp, dx2, 2048, D, "dw_mlp_out", relu_sq=True)
    g_wmo, dg2 = _scale_gate(m2, wmo.reshape(FF, D), modv, 5, "gate2_grad")
    g_wmi = _tn_matmul(h2, dhp, D, 512, "dw_mlp_in", dev_major=True)
    dua, dub, dga, dgb, dya, dyb, dl_a, dl_b = _merge_bwd(dx1, modv, ga, gb, ua, ub, ya, yb, wb, wout)
    m1 = _tn_matmul(merged, dx1, D, D, "dw_out")
    g_wout, dg1 = _scale_gate(m1, wout, modv, 2, "gate1_grad")
    g_wb0 = _tn_matmul(ya, dua, BW, D, "dw_branch_a")
    g_wb1 = _tn_matmul(yb, dub, BW, D, "dw_branch_b")
    ka_t3 = ka.reshape(S // tk, tk, 128).transpose(0, 2, 1)
    dqa_t, dka_t, dva_t = _attn_a_bwd(qa, qa.T, ka, ka_t3, va, dya, dya.T, lse_a, dl_a)
    dqa = dqa_t.T
    dka = dka_t.transpose(0, 2, 1).reshape(S, 128)
    dva = dva_t.transpose(0, 2, 1).reshape(S, 128)
    dqb, dkbp, dvbp, dsink = _attn_b_bwd(qb, kbp, vbp, sink, dyb, lse_b, dl_b)
    dproj, stqk = _qk_bwd(dqa, dka, dva, dqb, dkbp[WIN:WIN + S], dvbp[WIN:WIN + S], qar, kar, qg2, kg2, tab_a, tab_b, dga, dgb)
    grad_x, st1 = _in_bwd(dproj, win, xs, dx1, modv, n1g)
    g_win = _tn_matmul(h, dproj, D, 896, "dw_in")

    g_pack = jnp.concatenate([
        g_win.reshape(D, N_DEV, 448).transpose(1, 0, 2).reshape(N_DEV, 448, D),
        jnp.stack([g_wb0, g_wb1]).reshape(2, BW, N_DEV, 128).transpose(2, 0, 1, 3).reshape(N_DEV, 128, D),
        g_wout.reshape(N_DEV, 128, D),
        g_wmi.reshape(N_DEV, 512, D),
        g_wmo.reshape(N_DEV, 512, D)], axis=1)
    g4 = g_pack.reshape(4, 2, ROWS_PACK, D)
    from_sibling = _grad_rs_sibling(g4)
    t4 = _pair_sum(g4, from_sibling, ac.reshape(1).astype(jnp.int32))
    from_chips = _grad_rs_chips(t4)
    chip = (2 * ax + ay).reshape(1).astype(jnp.int32)
    w_p = _pack_shards(w_in[0], w_branch[0], w_out[0], w_mlp_in[0], w_mlp_out[0])
    m_p = _pack_shards(m_w_in[0], m_w_branch[0], m_w_out[0], m_w_mlp_in[0], m_w_mlp_out[0])
    v_p = _pack_shards(v_w_in[0], v_w_branch[0], v_w_out[0], v_w_mlp_in[0], v_w_mlp_out[0])
    big = _adamw_sum([(t4, chip), (from_chips, 0), (from_chips, 1), (from_chips, 2)], w_p, m_p, v_p, "adamw_big")
    big = [_unpack_shards(a) for a in big]

    small = _pack_small(st1, st2, stf, dg1, dg2, stqk, dsink, modv, n1g, n2g)
    small_all = _small_gather(small)
    sw = _pack_small_params(b_ada, norm1_g, norm2_g, final_g, q_norm_a, k_norm_a, sink_b)
    sm = _pack_small_params(m_b_ada, m_norm1_g, m_norm2_g, m_final_g, m_q_norm_a, m_k_norm_a, m_sink_b)
    sv = _pack_small_params(v_b_ada, v_norm1_g, v_norm2_g, v_final_g, v_q_norm_a, v_k_norm_a, v_sink_b)
    sm_out = _adamw_sum([(small_all, k) for k in range(N_DEV)], sw, sm, sv, "adamw_small")
    loss = sm_out[0][11, 0]
    sm_out = [_unpack_small(a) for a in sm_out]

    dmod_all = small_all[:, 0:6, :].reshape(N_DEV, 6 * D)
    dmod_cols = lax.dynamic_slice_in_dim(dmod_all, me * NW, NW, axis=1)
    g_wada = _wada_grad(silu_all, dmod_cols)
    ada = _adamw_sum([(g_wada, None)], w_ada.reshape(D, NW), m_w_ada.reshape(D, NW), v_w_ada.reshape(D, NW), "adamw_ada")
    ada = [a.reshape(1, D, NW) for a in ada]

    def leaves(k):
        b_, n1_, qn_, kn_, sk_, n2_, fg_ = sm_out[k]
        wi_, wbr_, wo_, wmi_, wmo_ = big[k]
        return [ada[k], b_, n1_, wi_, qn_, kn_, sk_, wbr_, wo_, n2_, wmi_, wmo_, fg_]

    return (loss, grad_x.reshape(1, S, D), *leaves(0), *leaves(1), *leaves(2), *leaves(3))
```

```python
import jax
import jax.numpy as jnp
from jax import lax
from jax.experimental import pallas as pl
from jax.experimental.pallas import tpu as pltpu

F32, BF16 = jnp.float32, jnp.bfloat16
MESH = pl.DeviceIdType.MESH

D = 1024
HD = 64
NH = 8
GRP = 4
BW = 512
FF = 4096
INW = 3584
GRID_W = 64
WIN = 128
THETA = 10000.0
EPS = 1e-6
NEG = -1e30
N_DEV = 8
LOG2E = 1.4426950408889634
LN2 = 0.6931471805599453
QA_SCALE = 0.125 * LOG2E
SMALL_ROWS = 16
V7X_VMEM_LIMIT = 56 * 1024 * 1024

ADAM_LR, ADAM_B1, ADAM_B2, ADAM_EPS, ADAM_WD, ADAM_STEP = 0.001, 0.9, 0.999, 1e-08, 0.01, 10

NT = (((1,), (1,)), ((), ()))
TN = (((0,), (0,)), ((), ()))


def _params(n_axes, vmem=V7X_VMEM_LIMIT):
    return pltpu.CompilerParams(dimension_semantics=("arbitrary",) * n_axes, vmem_limit_bytes=vmem)


def _const(shape):
    return pl.BlockSpec(shape, lambda *_: (0,) * len(shape))


def _rows(tm, width):
    return pl.BlockSpec((tm, width), lambda i, *_: (i, 0))


def _seg_matrix(n, seg):
    r = lax.broadcasted_iota(jnp.int32, (n, n), 0) // seg
    c = lax.broadcasted_iota(jnp.int32, (n, n), 1) // seg
    return (r == c).astype(F32)


def _seg_sum(z, seg_mat):
    return jnp.dot(z, seg_mat, preferred_element_type=F32, precision=lax.Precision.HIGHEST)


def _rope(z, t_ref, sh):
    return z * t_ref[0] + pltpu.roll(z, sh, 1) * t_ref[1] + pltpu.roll(z, 128 - sh, 1) * t_ref[2]


def _rope_t(dz, t_ref, sh):
    return dz * t_ref[0] + pltpu.roll(dz * t_ref[1], 128 - sh, 1) + pltpu.roll(dz * t_ref[2], sh, 1)


def _rope_tables(S):
    t = jnp.arange(S, dtype=jnp.int32)
    lane = jnp.arange(HD)

    def build(cos, sin, first):
        t0 = cos
        t1 = jnp.where(first[None, :], 0.0, sin)
        t2 = jnp.where(first[None, :], -sin, 0.0)
        return jnp.stack([jnp.tile(a, (1, 2)) for a in (t0, t1, t2)]).astype(F32)

    inv_a = THETA ** (-jnp.arange(0, HD // 2, 2, dtype=F32) / (HD // 2))
    ar = (t // GRID_W).astype(F32)[:, None] * inv_a[None, :]
    ac = (t % GRID_W).astype(F32)[:, None] * inv_a[None, :]
    cos_a = jnp.concatenate([jnp.cos(ar), jnp.cos(ar), jnp.cos(ac), jnp.cos(ac)], axis=1)
    sin_a = jnp.concatenate([jnp.sin(ar), jnp.sin(ar), jnp.sin(ac), jnp.sin(ac)], axis=1)
    tab_a = build(cos_a, sin_a, (lane % 32) < 16)
    inv_b = THETA ** (-jnp.arange(0, HD, 2, dtype=F32) / HD)
    ab = t.astype(F32)[:, None] * inv_b[None, :]
    cos_b = jnp.concatenate([jnp.cos(ab), jnp.cos(ab)], axis=1)
    sin_b = jnp.concatenate([jnp.sin(ab), jnp.sin(ab)], axis=1)
    tab_b = build(cos_b, sin_b, lane < 32)
    return tab_a, tab_b


def _in_proj(x, modv, n1g, win, qg2, kg2, tab_a, tab_b):
    S = x.shape[0]
    tm = min(256, S)

    def body(x_ref, mod_ref, g_ref, w_ref, qg_ref, kg_ref, ta_ref, tb_ref,
             h_ref, qar_ref, kar_ref, qa_ref, ka_ref, va_ref, qb_ref, kb_ref, vb_ref, ga_ref, gb_ref):
        xt = x_ref[...]
        r = lax.rsqrt(jnp.mean(xt * xt, axis=-1, keepdims=True) + EPS)
        h = ((xt * r) * g_ref[...]) * (1.0 + mod_ref[1:2, :]) + mod_ref[0:1, :]
        hb = h.astype(BF16)
        h_ref[...] = hb
        proj = jnp.dot(hb, w_ref[...], preferred_element_type=F32)
        seg = _seg_matrix(128, HD)

        def head_norm(z, g):
            ms = _seg_sum(z * z, seg) * (1.0 / HD)
            return (z * lax.rsqrt(ms + EPS)) * g

        for p in range(4):
            z = proj[:, 128 * p:128 * p + 128]
            qar_ref[:, 128 * p:128 * p + 128] = z.astype(BF16)
            qa_ref[:, 128 * p:128 * p + 128] = (_rope(head_norm(z, qg_ref[...]), ta_ref, 16) * QA_SCALE).astype(BF16)
            zb = proj[:, 768 + 128 * p:768 + 128 * p + 128]
            qb_ref[:, 128 * p:128 * p + 128] = (_rope(zb, tb_ref, 32) * 0.125).astype(BF16)
        z = proj[:, 512:640]
        kar_ref[...] = z.astype(BF16)
        ka_ref[...] = _rope(head_norm(z, kg_ref[...]), ta_ref, 16).astype(BF16)
        va_ref[...] = proj[:, 640:768].astype(BF16)
        kb_ref[...] = _rope(proj[:, 1280:1408], tb_ref, 32).astype(BF16)
        vb_ref[...] = proj[:, 1408:1536].astype(BF16)
        ga_ref[...] = proj[:, 1536:2560].astype(BF16)
        gb_ref[...] = proj[:, 2560:3584].astype(BF16)

    tab = pl.BlockSpec((3, tm, 128), lambda i: (0, i, 0))
    shapes = [(D, BF16), (BW, BF16), (128, BF16), (BW, BF16), (128, BF16), (128, BF16),
              (BW, BF16), (128, BF16), (128, BF16), (D, BF16), (D, BF16)]
    return pl.pallas_call(
        body, grid=(S // tm,),
        in_specs=[_rows(tm, D), _const((6, D)), _const((1, D)), _const((D, INW)), _const((1, 128)), _const((1, 128)), tab, tab],
        out_specs=[_rows(tm, w) for w, _ in shapes],
        out_shape=[jax.ShapeDtypeStruct((S, w), dt) for w, dt in shapes],
        compiler_params=_params(1), name="in_proj",
    )(x, modv, n1g, win, qg2, kg2, tab_a, tab_b)


def _exchange_gather(block_ref, out_ref, send_sems, recv_sems, local_sem):
    x, y, c = _me()
    me = 4 * x + 2 * y + c

    def copies():
        own = pltpu.make_async_copy(block_ref, out_ref.at[me], local_sem)
        out = [pltpu.make_async_remote_copy(block_ref, out_ref.at[me], send_sems.at[k - 1], recv_sems.at[k - 1],
                                            device_id=_peer(k), device_id_type=MESH) for k in range(1, N_DEV)]
        arrive = [pltpu.make_async_remote_copy(block_ref, out_ref.at[me ^ k], send_sems.at[k - 1], recv_sems.at[k - 1],
                                               device_id=_peer(k), device_id_type=MESH) for k in range(1, N_DEV)]
        return own, out, arrive

    def start():
        own, out, _ = copies()
        own.start()
        for cp in out:
            cp.start()

    def finish():
        own, out, arrive = copies()
        for cp in arrive:
            cp.wait_recv()
        for cp in out:
            cp.wait_send()
        own.wait()

    return start, finish


def _exchange_scatter(chunks_ref, recv_ref, send_sems, recv_sems):
    x, y, c = _me()
    me = 4 * x + 2 * y + c

    def copies():
        return [pltpu.make_async_remote_copy(chunks_ref.at[me ^ k], recv_ref.at[k - 1], send_sems.at[k - 1], recv_sems.at[k - 1],
                                             device_id=_peer(k), device_id_type=MESH) for k in range(1, N_DEV)]

    def start():
        for cp in copies():
            cp.start()

    def finish():
        cps = copies()
        for cp in cps:
            cp.wait_recv()
        for cp in cps:
            cp.wait_send()

    return start, finish


_EXCHANGE_SEMS = [pltpu.SemaphoreType.DMA((N_DEV - 1,)), pltpu.SemaphoreType.DMA((N_DEV - 1,))]


def _attn_a_fwd(q, k, vt3, shard):
    S = q.shape[0]
    tq = min(512, S)
    nq = S // tq
    nk, _, tk = vt3.shape
    ONES = 16

    def body(q_ref, k_ref, vt_ref, w_hbm, o_ref, lse_ref, wall_hbm, st_sc, send_sems, recv_sems, local_sem):
        start, finish = _exchange_gather(w_hbm, wall_hbm, send_sems, recv_sems, local_sem)
        pl.when(pl.program_id(0) == 0)(start)
        row8 = lax.broadcasted_iota(jnp.int32, (NH, tq), 0)
        lse_all = jnp.zeros((NH, tq), F32)
        ones = jnp.ones((ONES, tk), BF16)
        for kv in range(2):
            qs = []
            for pp in range(2):
                qp = q_ref[:, 128 * (2 * kv + pp):128 * (2 * kv + pp) + 128]
                qs += [qp[:, :HD], qp[:, HD:]]

            def keys(j, kv=kv):
                return k_ref[pl.ds(pl.multiple_of(j * tk, tk), tk), :][:, HD * kv:HD * kv + HD]

            def scores(kj, t, qs=qs):
                return lax.dot_general(kj, qs[t], NT, preferred_element_type=F32)

            def step(j, carry, kv=kv):
                kj = keys(j)
                kn = keys(jnp.minimum(j + 1, nk - 1))
                v1 = jnp.concatenate([vt_ref[j, HD * kv:HD * kv + HD, :], ones], axis=0)
                sts = [st_sc[0], st_sc[1]]
                new = []
                for t in range(GRP):
                    m, acc = carry[2 * t], carry[2 * t + 1]
                    if t + 2 < GRP:
                        sts.append(scores(kj, t + 2))
                    st = sts[t]
                    mn = jnp.maximum(m, jnp.max(st, axis=0, keepdims=True))
                    pt = jnp.exp2(st - mn)
                    if t + 2 >= GRP:
                        st_sc[t + 2 - GRP] = scores(kn, t + 2 - GRP)
                    acc = jnp.exp2(m - mn) * acc + jnp.dot(v1, pt.astype(BF16), preferred_element_type=F32)
                    new += [mn, acc]
                return tuple(new)

            k0 = keys(0)
            st_sc[0] = scores(k0, 0)
            st_sc[1] = scores(k0, 1)
            init = (jnp.full((1, tq), NEG, F32), jnp.zeros((HD + ONES, tq), F32)) * GRP
            res = lax.fori_loop(0, nk, step, init)
            for t in range(GRP):
                h = GRP * kv + t
                m, acc = res[2 * t], res[2 * t + 1]
                l = acc[HD:HD + 1, :]
                o_ref[HD * h:HD * h + HD, :] = (acc[:HD, :] / l).astype(BF16)
                lse_all = jnp.where(row8 == h, m + jnp.log2(l), lse_all)
        lse_ref[...] = lse_all
        pl.when(pl.program_id(0) == nq - 1)(finish)

    any_spec = pl.BlockSpec(memory_space=pl.ANY)
    return pl.pallas_call(
        body, grid=(nq,),
        in_specs=[_rows(tq, BW), _const((S, 128)), _const((nk, 128, tk)), any_spec],
        out_specs=[pl.BlockSpec((BW, tq), lambda i: (0, i)), pl.BlockSpec((NH, tq), lambda i: (0, i)), any_spec],
        out_shape=[jax.ShapeDtypeStruct((BW, S), BF16), jax.ShapeDtypeStruct((NH, S), F32),
                   jax.ShapeDtypeStruct((N_DEV,) + shard.shape, shard.dtype)],
        scratch_shapes=[pltpu.VMEM((2, tk, tq), F32)] + _EXCHANGE_SEMS + [pltpu.SemaphoreType.DMA],
        compiler_params=_params(1), name="attn_a_fwd",
    )(q, k, vt3, shard)


def _window_mask(i, tq, S):
    W = tq + 2 * WIN
    r = lax.broadcasted_iota(jnp.int32, (tq, W), 0)
    c = lax.broadcasted_iota(jnp.int32, (tq, W), 1)
    kpos = i * tq - WIN + c
    return (jnp.abs(c - WIN - r) <= WIN) & (kpos >= 0) & (kpos < S)


def _attn_b_fwd(q, kp, vp, sink):
    S = q.shape[0]
    tq = min(512, S)
    W = tq + 2 * WIN

    def body(q_ref, k_ref, v_ref, sink_ref, o_ref, lse_ref):
        i = pl.program_id(0)
        off = pl.multiple_of(i * tq, tq)
        valid = _window_mask(i, tq, S)
        kw = k_ref[pl.ds(off, W), :]
        vw = v_ref[pl.ds(off, W), :]
        lane8 = lax.broadcasted_iota(jnp.int32, (tq, NH), 1)
        lse_all = jnp.zeros((tq, NH), F32)
        for p in range(4):
            qp = q_ref[:, 128 * p:128 * p + 128]
            kv = p // 2
            kh = kw[:, HD * kv:HD * kv + HD]
            vh = vw[:, HD * kv:HD * kv + HD]
            outs = []
            for hh in range(2):
                h = 2 * p + hh
                s = lax.dot_general(qp[:, HD * hh:HD * hh + HD], kh, NT, preferred_element_type=F32)
                s = jnp.where(valid, s, NEG)
                sk = sink_ref[:, h:h + 1]
                m = jnp.maximum(jnp.max(s, axis=1, keepdims=True), sk)
                pm = jnp.exp(s - m)
                l = jnp.sum(pm, axis=1, keepdims=True) + jnp.exp(sk - m)
                outs.append(jnp.dot(pm.astype(BF16), vh, preferred_element_type=F32) / l)
                lse_all = jnp.where(lane8 == h, m + jnp.log(l), lse_all)
            o_ref[:, 128 * p:128 * p + 128] = jnp.concatenate(outs, axis=1).astype(BF16)
        lse_ref[...] = lse_all

    return pl.pallas_call(
        body, grid=(S // tq,),
        in_specs=[_rows(tq, BW), _const((S + 2 * WIN, 128)), _const((S + 2 * WIN, 128)), _const((1, NH))],
        out_specs=[_rows(tq, BW), _rows(tq, NH)],
        out_shape=[jax.ShapeDtypeStruct((S, BW), BF16), jax.ShapeDtypeStruct((S, NH), F32)],
        compiler_params=_params(1), name="attn_b_fwd",
    )(q, kp, vp, sink)


def _merge_out(ya, yb, ga, gb, x, modv, wb, wout):
    S = x.shape[0]
    tm = min(256, S)

    def body(ya_ref, yb_ref, ga_ref, gb_ref, x_ref, mod_ref, wb_ref, wo_ref, x1_ref, mg_ref, ua_ref, ub_ref):
        ua = jnp.dot(ya_ref[...], wb_ref[0], preferred_element_type=F32)
        ub = jnp.dot(yb_ref[...], wb_ref[1], preferred_element_type=F32)
        merged = jax.nn.sigmoid(ga_ref[...].astype(F32)) * ua + jax.nn.sigmoid(gb_ref[...].astype(F32)) * ub
        mb = merged.astype(BF16)
        ua_ref[...] = ua.astype(BF16)
        ub_ref[...] = ub.astype(BF16)
        mg_ref[...] = mb
        x1_ref[...] = x_ref[...] + mod_ref[2:3, :] * jnp.dot(mb, wo_ref[...], preferred_element_type=F32)

    return pl.pallas_call(
        body, grid=(S // tm,),
        in_specs=[_rows(tm, BW), _rows(tm, BW), _rows(tm, D), _rows(tm, D), _rows(tm, D), _const((6, D)),
                  _const((2, BW, D)), _const((D, D))],
        out_specs=[_rows(tm, D)] * 4,
        out_shape=[jax.ShapeDtypeStruct((S, D), F32)] + [jax.ShapeDtypeStruct((S, D), BF16)] * 3,
        compiler_params=_params(1), name="merge_out",
    )(ya, yb, ga, gb, x, modv, wb, wout)


def _mlp_fwd(x1, modv, n2g, wmi, wmo, fg, target):
    S = x1.shape[0]
    tm = min(512, S)
    tf = wmi.shape[2]
    nj = wmi.shape[0]

    def body(x1_ref, mod_ref, g_ref, wi_ref, wo_ref, fg_ref, t_ref, h2_ref, hp_ref, dx2_ref, st_ref, acc_ref):
        i, j = pl.program_id(0), pl.program_id(1)

        @pl.when(j == 0)
        def _():
            xt = x1_ref[...]
            r = lax.rsqrt(jnp.mean(xt * xt, axis=-1, keepdims=True) + EPS)
            h2 = ((xt * r) * g_ref[...]) * (1.0 + mod_ref[4:5, :]) + mod_ref[3:4, :]
            h2_ref[...] = h2.astype(BF16)
            acc_ref[...] = jnp.zeros_like(acc_ref)

        @pl.when((i == 0) & (j == 0))
        def _():
            st_ref[...] = jnp.zeros_like(st_ref)

        hp = jnp.dot(h2_ref[...], wi_ref[...], preferred_element_type=F32)
        hp_ref[...] = hp.astype(BF16)
        hid = jnp.square(jnp.maximum(hp, 0.0))
        acc_ref[...] += jnp.dot(hid.astype(BF16), wo_ref[...], preferred_element_type=F32)

        @pl.when(j == nj - 1)
        def _():
            x2 = x1_ref[...] + mod_ref[5:6, :] * acc_ref[...]
            r3 = lax.rsqrt(jnp.mean(x2 * x2, axis=-1, keepdims=True) + EPS)
            xn = x2 * r3
            err = xn * fg_ref[...] - t_ref[...]
            dy = err * (1.0 / D)
            gy = dy * fg_ref[...]
            dx2_ref[...] = r3 * (gy - xn * jnp.mean(gy * xn, axis=-1, keepdims=True))
            st_ref[0:1, :] += jnp.sum(dy * xn, axis=0, keepdims=True)
            st_ref[1:2, :] += jnp.sum(err * err, axis=0, keepdims=True) * (0.5 / D)

    return pl.pallas_call(
        body, grid=(S // tm, nj),
        in_specs=[pl.BlockSpec((tm, D), lambda i, j: (i, 0)), _const((6, D)), _const((1, D)),
                  pl.BlockSpec((None, D, tf), lambda i, j: (j, 0, 0)), pl.BlockSpec((None, tf, D), lambda i, j: (j, 0, 0)),
                  _const((1, D)), pl.BlockSpec((tm, D), lambda i, j: (i, 0))],
        out_specs=[pl.BlockSpec((tm, D), lambda i, j: (i, 0)), pl.BlockSpec((tm, tf), lambda i, j: (i, j)),
                   pl.BlockSpec((tm, D), lambda i, j: (i, 0)), _const((8, D))],
        out_shape=[jax.ShapeDtypeStruct((S, D), BF16), jax.ShapeDtypeStruct((S, nj * tf), BF16),
                   jax.ShapeDtypeStruct((S, D), F32), jax.ShapeDtypeStruct((8, D), F32)],
        scratch_shapes=[pltpu.VMEM((tm, D), F32)],
        compiler_params=_params(2), name="mlp_fwd",
    )(x1, modv, n2g, wmi, wmo, fg, target)


def _mlp_bwd(dx2, x1, hp, modv, n2g, wmi, wmo):
    S = x1.shape[0]
    tm = min(512, S)
    tf = wmi.shape[2]
    nj = wmi.shape[0]

    def body(dx2_ref, x1_ref, hp_ref, mod_ref, g_ref, wi_ref, wo_ref, dhp_ref, dx1_ref, st_ref, dmo_ref, acc_ref):
        i, j = pl.program_id(0), pl.program_id(1)

        @pl.when(j == 0)
        def _():
            dmo_ref[...] = (mod_ref[5:6, :] * dx2_ref[...]).astype(BF16)
            acc_ref[...] = jnp.zeros_like(acc_ref)

        @pl.when((i == 0) & (j == 0))
        def _():
            st_ref[...] = jnp.zeros_like(st_ref)

        dhid = lax.dot_general(dmo_ref[...], wo_ref[...], NT, preferred_element_type=F32)
        dhp = (dhid * (2.0 * jnp.maximum(hp_ref[...].astype(F32), 0.0))).astype(BF16)
        dhp_ref[...] = dhp
        acc_ref[...] += lax.dot_general(dhp, wi_ref[...], NT, preferred_element_type=F32)

        @pl.when(j == nj - 1)
        def _():
            dh2 = acc_ref[...]
            xt = x1_ref[...]
            r = lax.rsqrt(jnp.mean(xt * xt, axis=-1, keepdims=True) + EPS)
            xn = xt * r
            st_ref[0:1, :] += jnp.sum(dh2, axis=0, keepdims=True)
            st_ref[1:2, :] += jnp.sum(dh2 * xn, axis=0, keepdims=True)
            dxn = dh2 * (g_ref[...] * (1.0 + mod_ref[4:5, :]))
            dx1_ref[...] = dx2_ref[...] + r * (dxn - xn * jnp.mean(dxn * xn, axis=-1, keepdims=True))

    return pl.pallas_call(
        body, grid=(S // tm, nj),
        in_specs=[pl.BlockSpec((tm, D), lambda i, j: (i, 0)), pl.BlockSpec((tm, D), lambda i, j: (i, 0)),
                  pl.BlockSpec((tm, tf), lambda i, j: (i, j)), _const((6, D)), _const((1, D)),
                  pl.BlockSpec((None, D, tf), lambda i, j: (j, 0, 0)), pl.BlockSpec((None, tf, D), lambda i, j: (j, 0, 0))],
        out_specs=[pl.BlockSpec((tm, tf), lambda i, j: (i, j)), pl.BlockSpec((tm, D), lambda i, j: (i, 0)), _const((8, D))],
        out_shape=[jax.ShapeDtypeStruct((S, nj * tf), BF16), jax.ShapeDtypeStruct((S, D), F32), jax.ShapeDtypeStruct((8, D), F32)],
        scratch_shapes=[pltpu.VMEM((tm, D), BF16), pltpu.VMEM((tm, D), F32)],
        compiler_params=_params(2), name="mlp_bwd",
    )(dx2, x1, hp, modv, n2g, wmi, wmo)


def _tn_matmul(a, b, tk, tn, name, relu_sq=False, dev_major=False):
    S, K = a.shape
    N = b.shape[1]
    ts = min(512, S)
    ns = S // ts

    def body(a_ref, b_ref, o_ref):
        @pl.when(pl.program_id(2) == 0)
        def _():
            o_ref[...] = jnp.zeros_like(o_ref)

        at = a_ref[...]
        if relu_sq:
            at = jnp.square(jnp.maximum(at.astype(F32), 0.0)).astype(BF16)
        o_ref[...] += lax.dot_general(at, b_ref[...].astype(BF16), TN, preferred_element_type=F32)

    if dev_major:
        out_spec = pl.BlockSpec((None, tk, tn), lambda k, n, s: (n, k, 0))
        out_shape = jax.ShapeDtypeStruct((N // tn, K, tn), F32)
    else:
        out_spec = pl.BlockSpec((tk, tn), lambda k, n, s: (k, n))
        out_shape = jax.ShapeDtypeStruct((K, N), F32)
    return pl.pallas_call(
        body, grid=(K // tk, N // tn, ns),
        in_specs=[pl.BlockSpec((ts, tk), lambda k, n, s: (s, k)), pl.BlockSpec((ts, tn), lambda k, n, s: (s, n))],
        out_specs=out_spec, out_shape=out_shape,
        compiler_params=_params(3), name=name,
    )(a, b)


def _scale_gate(m, w, g, row, name):
    K = m.shape[0]
    tk = min(512, K)

    def body(m_ref, w_ref, mod_ref, dw_ref, dg_ref):
        @pl.when(pl.program_id(0) == 0)
        def _():
            dg_ref[...] = jnp.zeros_like(dg_ref)

        mt = m_ref[...]
        dw_ref[...] = mt * mod_ref[row:row + 1, :]
        dg_ref[0:1, :] += jnp.sum(mt * w_ref[...].astype(F32), axis=0, keepdims=True)

    return pl.pallas_call(
        body, grid=(K // tk,),
        in_specs=[_rows(tk, D), _rows(tk, D), _const((6, D))],
        out_specs=[_rows(tk, D), _const((8, D))],
        out_shape=[jax.ShapeDtypeStruct((K, D), F32), jax.ShapeDtypeStruct((8, D), F32)],
        compiler_params=_params(1), name=name,
    )(m, w, g)


def _merge_bwd(dx1, modv, ga, gb, ua, ub, ya, yb, wb, wout):
    S = dx1.shape[0]
    tm = min(256, S)

    def body(dx1_ref, mod_ref, ga_ref, gb_ref, ua_ref, ub_ref, ya_ref, yb_ref, wb_ref, wo_ref,
             dua_ref, dub_ref, dga_ref, dgb_ref, dya_ref, dyb_ref, dla_ref, dlb_ref):
        dao = (mod_ref[2:3, :] * dx1_ref[...]).astype(BF16)
        dm = lax.dot_general(dao, wo_ref[...], NT, preferred_element_type=F32)
        r = lax.broadcasted_iota(jnp.int32, (BW, NH), 0) // HD
        c = lax.broadcasted_iota(jnp.int32, (BW, NH), 1)
        head_of = (r == c).astype(F32)
        for br, (g_ref, u_ref, y_ref, du_ref, dg_ref, dy_ref, dl_ref) in enumerate((
                (ga_ref, ua_ref, ya_ref, dua_ref, dga_ref, dya_ref, dla_ref),
                (gb_ref, ub_ref, yb_ref, dub_ref, dgb_ref, dyb_ref, dlb_ref))):
            sg = jax.nn.sigmoid(g_ref[...].astype(F32))
            du = (dm * sg).astype(BF16)
            du_ref[...] = du
            dg_ref[...] = (dm * u_ref[...].astype(F32) * sg * (1.0 - sg)).astype(BF16)
            dy = lax.dot_general(du, wb_ref[br], NT, preferred_element_type=F32)
            dyb16 = dy.astype(BF16)
            dy_ref[...] = dyb16
            dl_ref[...] = jnp.dot(dyb16.astype(F32) * y_ref[...].astype(F32), head_of,
                                  preferred_element_type=F32, precision=lax.Precision.HIGHEST)

    return pl.pallas_call(
        body, grid=(S // tm,),
        in_specs=[_rows(tm, D), _const((6, D)), _rows(tm, D), _rows(tm, D), _rows(tm, D), _rows(tm, D),
                  _rows(tm, BW), _rows(tm, BW), _const((2, BW, D)), _const((D, D))],
        out_specs=[_rows(tm, D)] * 4 + [_rows(tm, BW)] * 2 + [_rows(tm, NH)] * 2,
        out_shape=[jax.ShapeDtypeStruct((S, D), BF16)] * 4 + [jax.ShapeDtypeStruct((S, BW), BF16)] * 2
        + [jax.ShapeDtypeStruct((S, NH), F32)] * 2,
        compiler_params=_params(1), name="merge_bwd",
    )(dx1, modv, ga, gb, ua, ub, ya, yb, wb, wout)


def _attn_a_bwd(q, qt, k, kt3, v, do, dot_, lse, delta, chunks):
    S = q.shape[0]
    tq = min(512, S)
    tk = min(512, S)
    nq, nk = S // tq, S // tk

    def body(q_ref, qt_ref, do_ref, dot_ref, lse_ref, dl_ref, k_ref, v_ref, kt_ref, g_hbm, dq_ref, dk_hbm, dv_hbm, recv_hbm,
             dk_sc, dv_sc, sem, send_sems, recv_sems):
        i = pl.program_id(0)
        start, finish = _exchange_scatter(g_hbm, recv_hbm, send_sems, recv_sems)

        @pl.when(i == 0)
        def _():
            start()
            dk_sc[...] = jnp.zeros_like(dk_sc)
            dv_sc[...] = jnp.zeros_like(dv_sc)

        for kv in range(2):
            qg = q_ref[:, 256 * kv:256 * kv + 256]
            dog = do_ref[:, 256 * kv:256 * kv + 256]
            heads = []
            for t in range(GRP):
                h = GRP * kv + t
                heads.append((qg[:, HD * t:HD * t + HD], dog[:, HD * t:HD * t + HD],
                              qt_ref[HD * h:HD * h + HD, :], dot_ref[HD * h:HD * h + HD, :],
                              lse_ref[:, h:h + 1], dl_ref[:, h:h + 1]))

            def step(j, carry, kv=kv, heads=heads):
                off = pl.multiple_of(j * tk, tk)
                kj = k_ref[pl.ds(off, tk), :][:, HD * kv:HD * kv + HD]
                vj = v_ref[pl.ds(off, tk), :][:, HD * kv:HD * kv + HD]
                kjt = kt_ref[j, HD * kv:HD * kv + HD, :]
                dkt = jnp.zeros((HD, tk), F32)
                dvt = jnp.zeros((HD, tk), F32)
                new = []

                def logits(t):
                    return (lax.dot_general(heads[t][0], kj, NT, preferred_element_type=F32),
                            lax.dot_general(heads[t][1], vj, NT, preferred_element_type=F32))

                sd = [logits(0)]
                for t, (qh, doh, qth, doth, lse_h, dl_h) in enumerate(heads):
                    if t + 1 < GRP:
                        sd.append(logits(t + 1))
                    s, dp = sd[t]
                    pm = jnp.exp2(s - lse_h)
                    ds = (pm * (dp - dl_h)).astype(BF16)
                    dvt = dvt + jnp.dot(doth, pm.astype(BF16), preferred_element_type=F32)
                    dkt = dkt + jnp.dot(qth, ds, preferred_element_type=F32)
                    new.append(carry[t] + lax.dot_general(kjt, ds, NT, preferred_element_type=F32))
                dk_sc[j, HD * kv:HD * kv + HD, :] += dkt
                dv_sc[j, HD * kv:HD * kv + HD, :] += dvt
                return tuple(new)

            res = lax.fori_loop(0, nk, step, (jnp.zeros((HD, tq), F32),) * GRP)
            for t in range(GRP):
                dq_ref[HD * (GRP * kv + t):HD * (GRP * kv + t) + HD, :] = res[t]

        @pl.when(i == nq - 1)
        def _():
            c1 = pltpu.make_async_copy(dk_sc, dk_hbm, sem.at[0])
            c2 = pltpu.make_async_copy(dv_sc, dv_hbm, sem.at[1])
            c1.start()
            c2.start()
            c1.wait()
            c2.wait()
            finish()

    any_spec = pl.BlockSpec(memory_space=pl.ANY)
    cols = pl.BlockSpec((BW, tq), lambda i: (0, i))
    return pl.pallas_call(
        body, grid=(nq,),
        in_specs=[_rows(tq, BW), cols, _rows(tq, BW), cols, _rows(tq, NH), _rows(tq, NH), _const((S, 128)), _const((S, 128)),
                  _const((nk, 128, tk)), any_spec],
        out_specs=[cols, any_spec, any_spec, any_spec],
        out_shape=[jax.ShapeDtypeStruct((BW, S), F32), jax.ShapeDtypeStruct((nk, 128, tk), F32),
                   jax.ShapeDtypeStruct((nk, 128, tk), F32), jax.ShapeDtypeStruct((N_DEV - 1,) + chunks.shape[1:], chunks.dtype)],
        scratch_shapes=[pltpu.VMEM((nk, 128, tk), F32), pltpu.VMEM((nk, 128, tk), F32), pltpu.SemaphoreType.DMA((2,))]
        + _EXCHANGE_SEMS,
        compiler_params=_params(1), name="attn_a_bwd",
    )(q, qt, do, dot_, lse, delta, k, v, kt3, chunks)


def _attn_b_bwd(q, kp, vp, sink, do, lse, delta):
    S = q.shape[0]
    tq = min(512, S)
    W = tq + 2 * WIN
    nq = S // tq

    def body(q_ref, k_ref, v_ref, sink_ref, do_ref, lse_ref, dl_ref, dq_ref, dk_hbm, dv_hbm, ds_ref, dk_sc, dv_sc, sem):
        i = pl.program_id(0)

        @pl.when(i == 0)
        def _():
            dk_sc[...] = jnp.zeros_like(dk_sc)
            dv_sc[...] = jnp.zeros_like(dv_sc)
            ds_ref[...] = jnp.zeros_like(ds_ref)

        off = pl.multiple_of(i * tq, tq)
        valid = _window_mask(i, tq, S)
        kw = k_ref[pl.ds(off, W), :]
        vw = v_ref[pl.ds(off, W), :]
        lse_i = lse_ref[...]
        dl_i = dl_ref[...]
        dks, dvs = [], []
        for kv in range(2):
            kh = kw[:, HD * kv:HD * kv + HD]
            vh = vw[:, HD * kv:HD * kv + HD]
            dk = jnp.zeros((W, HD), F32)
            dv = jnp.zeros((W, HD), F32)
            for pp in range(2):
                p = 2 * kv + pp
                qp = q_ref[:, 128 * p:128 * p + 128]
                dop = do_ref[:, 128 * p:128 * p + 128]
                dqs = []
                for hh in range(2):
                    h = 2 * p + hh
                    qh = qp[:, HD * hh:HD * hh + HD]
                    doh = dop[:, HD * hh:HD * hh + HD]
                    s = jnp.where(valid, lax.dot_general(qh, kh, NT, preferred_element_type=F32), NEG)
                    pm = jnp.exp(s - lse_i[:, h:h + 1])
                    dp = lax.dot_general(doh, vh, NT, preferred_element_type=F32)
                    ds = (pm * (dp - dl_i[:, h:h + 1])).astype(BF16)
                    dv = dv + lax.dot_general(pm.astype(BF16), doh, TN, preferred_element_type=F32)
                    dk = dk + lax.dot_general(ds, qh, TN, preferred_element_type=F32)
                    dqs.append(jnp.dot(ds, kh, preferred_element_type=F32))
                dq_ref[:, 128 * p:128 * p + 128] = jnp.concatenate(dqs, axis=1)
            dks.append(dk)
            dvs.append(dv)
        dk_sc[pl.ds(off, W), :] += jnp.concatenate(dks, axis=1)
        dv_sc[pl.ds(off, W), :] += jnp.concatenate(dvs, axis=1)
        psd = jnp.exp(sink_ref[...] - lse_i) * dl_i
        r = lax.broadcasted_iota(jnp.int32, (NH, 128), 0)
        c = lax.broadcasted_iota(jnp.int32, (NH, 128), 1)
        row = jnp.dot(jnp.sum(psd, axis=0, keepdims=True), (r == c).astype(F32),
                      preferred_element_type=F32, precision=lax.Precision.HIGHEST)
        ds_ref[...] -= jnp.broadcast_to(row, (8, 128))

        @pl.when(i == nq - 1)
        def _():
            c1 = pltpu.make_async_copy(dk_sc, dk_hbm, sem.at[0])
            c2 = pltpu.make_async_copy(dv_sc, dv_hbm, sem.at[1])
            c1.start()
            c2.start()
            c1.wait()
            c2.wait()

    any_spec = pl.BlockSpec(memory_space=pl.ANY)
    return pl.pallas_call(
        body, grid=(nq,),
        in_specs=[_rows(tq, BW), _const((S + 2 * WIN, 128)), _const((S + 2 * WIN, 128)), _const((1, NH)),
                  _rows(tq, BW), _rows(tq, NH), _rows(tq, NH)],
        out_specs=[_rows(tq, BW), any_spec, any_spec, _const((8, 128))],
        out_shape=[jax.ShapeDtypeStruct((S, BW), F32), jax.ShapeDtypeStruct((S + 2 * WIN, 128), F32),
                   jax.ShapeDtypeStruct((S + 2 * WIN, 128), F32), jax.ShapeDtypeStruct((8, 128), F32)],
        scratch_shapes=[pltpu.VMEM((S + 2 * WIN, 128), F32), pltpu.VMEM((S + 2 * WIN, 128), F32), pltpu.SemaphoreType.DMA((2,))],
        compiler_params=_params(1), name="attn_b_bwd",
    )(q, kp, vp, sink, do, lse, delta)


def _qk_bwd(dqa, dka, dva, dqb, dkb, dvb, qar, kar, qg2, kg2, tab_a, tab_b, dga, dgb):
    S = dqa.shape[0]
    tm = min(256, S)

    def body(dqa_ref, dka_ref, dva_ref, dqb_ref, dkb_ref, dvb_ref, qar_ref, kar_ref, qg_ref, kg_ref, ta_ref, tb_ref,
             dga_ref, dgb_ref, dp_ref, st_ref):
        @pl.when(pl.program_id(0) == 0)
        def _():
            st_ref[...] = jnp.zeros_like(st_ref)

        seg = _seg_matrix(128, HD)

        def norm_bwd(dz_rot, raw, g):
            dzn = _rope_t(dz_rot, ta_ref, 16)
            raw = raw.astype(F32)
            rr = lax.rsqrt(_seg_sum(raw * raw, seg) * (1.0 / HD) + EPS)
            zhat = raw * rr
            dzh = dzn * g
            draw = rr * (dzh - zhat * (_seg_sum(dzh * zhat, seg) * (1.0 / HD)))
            return draw, jnp.sum(dzn * zhat, axis=0, keepdims=True)

        gq = jnp.zeros((1, 128), F32)
        for p in range(4):
            sl = slice(128 * p, 128 * p + 128)
            draw, gsum = norm_bwd(dqa_ref[:, sl] * 0.125, qar_ref[:, sl], qg_ref[...])
            gq = gq + gsum
            dp_ref[:, sl] = draw.astype(BF16)
            dp_ref[:, 768 + 128 * p:768 + 128 * p + 128] = _rope_t(dqb_ref[:, sl] * 0.125, tb_ref, 32).astype(BF16)
        draw, gk = norm_bwd(dka_ref[...] * LN2, kar_ref[...], kg_ref[...])
        dp_ref[:, 512:640] = draw.astype(BF16)
        dp_ref[:, 640:768] = dva_ref[...].astype(BF16)
        dp_ref[:, 1280:1408] = _rope_t(dkb_ref[...], tb_ref, 32).astype(BF16)
        dp_ref[:, 1408:1536] = dvb_ref[...].astype(BF16)
        dp_ref[:, 1536:2560] = dga_ref[...]
        dp_ref[:, 2560:3584] = dgb_ref[...]
        st_ref[0:1, :] += gq
        st_ref[1:2, :] += gk

    tab = pl.BlockSpec((3, tm, 128), lambda i: (0, i, 0))
    return pl.pallas_call(
        body, grid=(S // tm,),
        in_specs=[_rows(tm, BW), _rows(tm, 128), _rows(tm, 128), _rows(tm, BW), _rows(tm, 128), _rows(tm, 128),
                  _rows(tm, BW), _rows(tm, 128), _const((1, 128)), _const((1, 128)), tab, tab, _rows(tm, D), _rows(tm, D)],
        out_specs=[_rows(tm, INW), _const((8, 128))],
        out_shape=[jax.ShapeDtypeStruct((S, INW), BF16), jax.ShapeDtypeStruct((8, 128), F32)],
        compiler_params=_params(1), name="qk_bwd",
    )(dqa, dka, dva, dqb, dkb, dvb, qar, kar, qg2, kg2, tab_a, tab_b, dga, dgb)


def _in_bwd(dproj, win, x, dx1, modv, n1g, chunks):
    S = x.shape[0]
    tm = min(256, S)
    n = S // tm

    def body(dp_ref, w_ref, x_ref, dx1_ref, mod_ref, g_ref, c_hbm, gx_ref, st_ref, recv_hbm, send_sems, recv_sems):
        start, finish = _exchange_scatter(c_hbm, recv_hbm, send_sems, recv_sems)

        @pl.when(pl.program_id(0) == 0)
        def _():
            start()
            st_ref[...] = jnp.zeros_like(st_ref)

        dh = lax.dot_general(dp_ref[...], w_ref[...], NT, preferred_element_type=F32)
        xt = x_ref[...]
        r = lax.rsqrt(jnp.mean(xt * xt, axis=-1, keepdims=True) + EPS)
        xn = xt * r
        st_ref[0:1, :] += jnp.sum(dh, axis=0, keepdims=True)
        st_ref[1:2, :] += jnp.sum(dh * xn, axis=0, keepdims=True)
        dxn = dh * (g_ref[...] * (1.0 + mod_ref[1:2, :]))
        gx_ref[...] = dx1_ref[...] + r * (dxn - xn * jnp.mean(dxn * xn, axis=-1, keepdims=True))
        pl.when(pl.program_id(0) == n - 1)(finish)

    any_spec = pl.BlockSpec(memory_space=pl.ANY)
    return pl.pallas_call(
        body, grid=(n,),
        in_specs=[_rows(tm, INW), _const((D, INW)), _rows(tm, D), _rows(tm, D), _const((6, D)), _const((1, D)), any_spec],
        out_specs=[_rows(tm, D), _const((8, D)), any_spec],
        out_shape=[jax.ShapeDtypeStruct((S, D), F32), jax.ShapeDtypeStruct((8, D), F32),
                   jax.ShapeDtypeStruct((N_DEV - 1,) + chunks.shape[1:], chunks.dtype)],
        scratch_shapes=list(_EXCHANGE_SEMS),
        compiler_params=_params(1), name="in_bwd",
    )(dproj, win, x, dx1, modv, n1g, chunks)


def _pack_small(st1, st2, stf, dg1, dg2, stqk, dsink, modv, n1g, n2g):
    def body(st1_ref, st2_ref, stf_ref, dg1_ref, dg2_ref, qk_ref, ds_ref, mod_ref, g1_ref, g2_ref, o_ref):
        a1, b1 = st1_ref[0:1, :], st1_ref[1:2, :]
        a2, b2 = st2_ref[0:1, :], st2_ref[1:2, :]
        r = lax.broadcasted_iota(jnp.int32, (128, D), 0)
        c = lax.broadcasted_iota(jnp.int32, (128, D), 1)
        fold_q = (c == r % HD).astype(F32)
        fold_k = (c == HD + r % HD).astype(F32)
        keep = (c == r).astype(F32)

        def place(v, sel):
            return jnp.dot(v, sel, preferred_element_type=F32, precision=lax.Precision.HIGHEST)

        loss = jnp.sum(stf_ref[1:2, :], axis=1, keepdims=True)
        lane = lax.broadcasted_iota(jnp.int32, (1, D), 1)
        rows = [a1, g1_ref[...] * b1, dg1_ref[0:1, :], a2, g2_ref[...] * b2, dg2_ref[0:1, :],
                (1.0 + mod_ref[1:2, :]) * b1, (1.0 + mod_ref[4:5, :]) * b2, stf_ref[0:1, :],
                place(qk_ref[0:1, :], fold_q) + place(qk_ref[1:2, :], fold_k),
                place(ds_ref[0:1, :], keep),
                jnp.where(lane == 0, loss, 0.0)]
        rows += [jnp.zeros((1, D), F32)] * (SMALL_ROWS - len(rows))
        for n, v in enumerate(rows):
            o_ref[n:n + 1, :] = v

    return pl.pallas_call(
        body, out_shape=jax.ShapeDtypeStruct((SMALL_ROWS, D), F32),
        compiler_params=pltpu.CompilerParams(vmem_limit_bytes=V7X_VMEM_LIMIT), name="pack_small",
    )(st1, st2, stf, dg1, dg2, stqk, dsink, modv, n1g, n2g)


def _wada_grad(silu_all, dmod_cols):
    def body(a_ref, b_ref, o_ref):
        o_ref[...] = lax.dot_general(a_ref[...], b_ref[...], TN, preferred_element_type=F32, precision=lax.Precision.HIGHEST)

    return pl.pallas_call(
        body, out_shape=jax.ShapeDtypeStruct((D, dmod_cols.shape[1]), F32),
        compiler_params=pltpu.CompilerParams(vmem_limit_bytes=V7X_VMEM_LIMIT), name="wada_grad",
    )(silu_all, dmod_cols)


def _adamw_sum(parts, w, m, v, name):
    R, C = w.shape
    tr = R if R <= 64 else next(t for t in (256, 128, 64, 32, 16, 8) if R % t == 0)
    n = len(parts)
    dyn = [idx for _, idx in parts if idx is not None and not isinstance(idx, int)]
    b1c = 1.0 - ADAM_B1 ** ADAM_STEP
    b2c = 1.0 - ADAM_B2 ** ADAM_STEP

    def body(*refs):
        refs = refs[len(dyn):]
        g = refs[0][...].astype(F32)
        for k in range(1, n):
            g = g + refs[k][...].astype(F32)
        w_ref, m_ref, v_ref, g_out, d_out, m_out, v_out = refs[n:]
        mn = ADAM_B1 * m_ref[...] + (1.0 - ADAM_B1) * g
        vn = ADAM_B2 * v_ref[...] + (1.0 - ADAM_B2) * jnp.square(g)
        g_out[...] = g
        m_out[...] = mn
        v_out[...] = vn
        d_out[...] = -ADAM_LR * ((mn / b1c) / (jnp.sqrt(vn / b2c) + ADAM_EPS) + ADAM_WD * w_ref[...])

    in_specs = []
    nd = 0
    for a, idx in parts:
        if idx is None:
            in_specs.append(pl.BlockSpec((tr, C), lambda i, *s: (i, 0)))
        elif isinstance(idx, int):
            in_specs.append(pl.BlockSpec((None, tr, C), lambda i, *s, idx=idx: (idx, i, 0)))
        else:
            in_specs.append(pl.BlockSpec((None, tr, C), lambda i, *s, nd=nd: (s[nd][0], i, 0)))
            nd += 1
    blk = pl.BlockSpec((tr, C), lambda i, *s: (i, 0))
    grid_spec = pltpu.PrefetchScalarGridSpec(
        num_scalar_prefetch=len(dyn), grid=(R // tr,), in_specs=in_specs + [blk] * 3, out_specs=[blk] * 4)
    return pl.pallas_call(
        body, grid_spec=grid_spec, out_shape=[jax.ShapeDtypeStruct((R, C), F32)] * 4,
        compiler_params=_params(1), name=name,
    )(*dyn, *[a for a, _ in parts], w, m, v)


def _me():
    return lax.axis_index("x"), lax.axis_index("y"), lax.axis_index("c")


def _peer(k):
    x, y, c = _me()
    return (x ^ ((k >> 2) & 1), y ^ ((k >> 1) & 1), c ^ (k & 1))


def _ada_exchange(c_row, w_ada, b_rows):
    NW = w_ada.shape[1]

    def body(c_ref, w_ref, b_ref, sall_ref, mod_ref, src_ref, mp_ref, send1, recv1, send2, recv2):
        x, y, c = _me()
        me = 4 * x + 2 * y + c
        cv = c_ref[...]
        src_ref[...] = jnp.broadcast_to(cv * jax.nn.sigmoid(cv), (8, D))
        mine = pl.ds(pl.multiple_of(me * 8, 8), 8)
        sall_ref[mine, :] = src_ref[...]
        sends = [pltpu.make_async_remote_copy(src_ref, sall_ref.at[mine, :], send1.at[k - 1], recv1.at[k - 1],
                                              device_id=_peer(k), device_id_type=MESH) for k in range(1, N_DEV)]
        for cp in sends:
            cp.start()
        for k in range(1, N_DEV):
            theirs = pl.ds(pl.multiple_of((me ^ k) * 8, 8), 8)
            pltpu.make_async_remote_copy(src_ref, sall_ref.at[theirs, :], send1.at[k - 1], recv1.at[k - 1],
                                         device_id=_peer(k), device_id_type=MESH).wait_recv()
        for cp in sends:
            cp.wait_send()
        mp_ref[...] = jnp.dot(sall_ref[...], w_ref[...], preferred_element_type=F32, precision=lax.Precision.HIGHEST)
        mod_ref[mine, :] = mp_ref[mine, :] + b_ref[mine, :]
        sends = []
        for k in range(1, N_DEV):
            theirs = pl.ds(pl.multiple_of((me ^ k) * 8, 8), 8)
            sends.append(pltpu.make_async_remote_copy(mp_ref.at[theirs, :], mod_ref.at[mine, :], send2.at[k - 1], recv2.at[k - 1],
                                                      device_id=_peer(k), device_id_type=MESH))
        for cp in sends:
            cp.start()
        for k in range(1, N_DEV):
            theirs = pl.ds(pl.multiple_of((me ^ k) * 8, 8), 8)
            pltpu.make_async_remote_copy(mp_ref.at[mine, :], mod_ref.at[theirs, :], send2.at[k - 1], recv2.at[k - 1],
                                         device_id=_peer(k), device_id_type=MESH).wait_recv()
            mod_ref[theirs, :] = mod_ref[theirs, :] + b_ref[theirs, :]
        for cp in sends:
            cp.wait_send()

    vm = pl.BlockSpec(memory_space=pltpu.VMEM)
    return pl.pallas_call(
        body, in_specs=[vm, vm, vm], out_specs=[vm, vm],
        out_shape=[jax.ShapeDtypeStruct((8 * N_DEV, D), F32), jax.ShapeDtypeStruct((8 * N_DEV, NW), F32)],
        scratch_shapes=[pltpu.VMEM((8, D), F32), pltpu.VMEM((8 * N_DEV, NW), F32)]
        + [pltpu.SemaphoreType.DMA((N_DEV - 1,))] * 4,
        compiler_params=pltpu.CompilerParams(vmem_limit_bytes=V7X_VMEM_LIMIT), name="ada_exchange",
    )(c_row, w_ada, b_rows)


def _weight_gather(shard):
    def body(x_ref, out_ref, send_sems, recv_sems, local_sem):
        x, y, c = _me()
        me, sibling = (x, y, c), (x, y, 1 - c)
        chips = [(1 - x, y), (x, 1 - y), (1 - x, 1 - y)]

        def slot(px, py, pc):
            return out_ref.at[4 * px + 2 * py + pc]

        def copy(k, block, to, src=None):
            return pltpu.make_async_remote_copy(
                src_ref=slot(*block) if src is None else src, dst_ref=slot(*block),
                send_sem=send_sems.at[k], recv_sem=recv_sems.at[k], device_id=to, device_id_type=MESH)

        mine = pltpu.make_async_copy(x_ref, slot(*me), local_sem)
        mine.start()
        first = [copy(0, me, sibling, src=x_ref)]
        first += [copy(1 + j, me, (*chip, c), src=x_ref) for j, chip in enumerate(chips)]
        for cp in first:
            cp.start()
        passed = [copy(4 + j, (*chip, c), sibling) for j, chip in enumerate(chips)]
        for j, chip in enumerate(chips):
            copy(1 + j, (*chip, c), me).wait_recv()
            passed[j].start()
        copy(0, sibling, me).wait_recv()
        for j, chip in enumerate(chips):
            copy(4 + j, (*chip, 1 - c), me).wait_recv()
        for cp in first + passed:
            cp.wait_send()
        mine.wait()

    any_spec = pl.BlockSpec(memory_space=pl.ANY)
    return pl.pallas_call(
        body, in_specs=[any_spec], out_specs=any_spec,
        out_shape=jax.ShapeDtypeStruct((N_DEV,) + shard.shape, shard.dtype),
        scratch_shapes=[pltpu.SemaphoreType.DMA((7,)), pltpu.SemaphoreType.DMA((7,)), pltpu.SemaphoreType.DMA],
        name="weight_gather",
    )(shard)


def _small_gather(block):
    def body(b_ref, out_ref, send_sems, recv_sems):
        x, y, c = _me()
        me = 4 * x + 2 * y + c
        out_ref[me] = b_ref[...]
        sends = [pltpu.make_async_remote_copy(b_ref, out_ref.at[me], send_sems.at[k - 1], recv_sems.at[k - 1],
                                              device_id=_peer(k), device_id_type=MESH) for k in range(1, N_DEV)]
        for cp in sends:
            cp.start()
        for k in range(1, N_DEV):
            pltpu.make_async_remote_copy(b_ref, out_ref.at[me ^ k], send_sems.at[k - 1], recv_sems.at[k - 1],
                                         device_id=_peer(k), device_id_type=MESH).wait_recv()
        for cp in sends:
            cp.wait_send()

    vm = pl.BlockSpec(memory_space=pltpu.VMEM)
    return pl.pallas_call(
        body, in_specs=[vm], out_specs=vm,
        out_shape=jax.ShapeDtypeStruct((N_DEV,) + block.shape, block.dtype),
        scratch_shapes=[pltpu.SemaphoreType.DMA((N_DEV - 1,)), pltpu.SemaphoreType.DMA((N_DEV - 1,))],
        name="small_gather",
    )(block)


def _pack_rest(w_branch, w_out, w_mlp_in, w_mlp_out):
    return jnp.concatenate([w_branch.reshape(-1, D), w_out.reshape(-1, D), w_mlp_in.reshape(-1, D), w_mlp_out.reshape(-1, D)], axis=0)


def _unpack_rest(p):
    return p[0:128].reshape(1, 2, BW, 128), p[128:256].reshape(1, 128, D), p[256:768].reshape(1, D, 512), p[768:1280].reshape(1, 512, D)


def _pack_small_params(b_ada, n1, n2, fg, qn, kn, sink):
    z = jnp.zeros((SMALL_ROWS, D), F32)
    z = z.at[0:6].set(b_ada.reshape(6, D)).at[6].set(n1.reshape(D)).at[7].set(n2.reshape(D)).at[8].set(fg.reshape(D))
    z = z.at[9, 0:HD].set(qn.reshape(HD)).at[9, HD:2 * HD].set(kn.reshape(HD)).at[10, 0:NH].set(sink.reshape(NH))
    return z


def _unpack_small(p):
    return (p[0:6].reshape(1, 6 * D), p[6].reshape(1, D), p[9, 0:HD].reshape(1, HD), p[9, HD:2 * HD].reshape(1, HD),
            p[10, 0:NH].reshape(1, NH), p[7].reshape(1, D), p[8].reshape(D))


def kernel(x, c, w_ada, b_ada, norm1_g, w_in, q_norm_a, k_norm_a, sink_b, w_branch, w_out, norm2_g, w_mlp_in, w_mlp_out, final_g, loss_target, m_w_ada, m_b_ada, m_norm1_g, m_w_in, m_q_norm_a, m_k_norm_a, m_sink_b, m_w_branch, m_w_out, m_norm2_g, m_w_mlp_in, m_w_mlp_out, m_final_g, v_w_ada, v_b_ada, v_norm1_g, v_w_in, v_q_norm_a, v_k_norm_a, v_sink_b, v_w_branch, v_w_out, v_norm2_g, v_w_mlp_in, v_w_mlp_out, v_final_g):
    S = x.shape[1]
    xs = x.reshape(S, D)
    tgt = loss_target.reshape(S, D)
    ax, ay, ac = lax.axis_index("x"), lax.axis_index("y"), lax.axis_index("c")
    me = 4 * ax + 2 * ay + ac
    me1 = me.reshape(1).astype(jnp.int32)
    NW = w_ada.shape[2]
    NI = w_in.shape[2]

    silu64, mod64 = _ada_exchange(c.reshape(1, D), w_ada.reshape(D, NW),
                                  jnp.repeat(b_ada.reshape(N_DEV, NW), 8, axis=0))
    silu_all = silu64[0::8]
    modv = mod64[0::8].reshape(6, D)

    win = _weight_gather(w_in[0].reshape(NI, D).astype(BF16))
    win = win.reshape(N_DEV, D, NI).transpose(1, 0, 2).reshape(D, INW)
    rest_shard = _pack_rest(w_branch[0], w_out[0], w_mlp_in[0], w_mlp_out[0]).astype(BF16)

    tab_a, tab_b = _rope_tables(S)
    qg2 = jnp.tile(q_norm_a.reshape(1, HD), (1, 2))
    kg2 = jnp.tile(k_norm_a.reshape(1, HD), (1, 2))
    n1g = norm1_g.reshape(1, D)
    n2g = norm2_g.reshape(1, D)
    fg = final_g.reshape(1, D)
    sink = sink_b.reshape(1, NH)

    h, qar, kar, qa, ka, va, qb, kb, vb, ga, gb = _in_proj(xs, modv, n1g, win, qg2, kg2, tab_a, tab_b)
    tk = min(512, S)
    ya_t, lse_at, rest = _attn_a_fwd(qa, ka, va.reshape(S // tk, tk, 128).transpose(0, 2, 1), rest_shard)
    ya, lse_a = ya_t.T, lse_at.T
    wb = rest[:, 0:128].reshape(N_DEV, 2, BW, 128).transpose(1, 2, 0, 3).reshape(2, BW, D)
    wout = rest[:, 128:256].reshape(D, D)
    wmi = rest[:, 256:768].reshape(N_DEV, D, 512)
    wmo = rest[:, 768:1280]
    pad = ((WIN, WIN), (0, 0))
    kbp, vbp = jnp.pad(kb, pad), jnp.pad(vb, pad)
    yb, lse_b = _attn_b_fwd(qb, kbp, vbp, sink)
    x1, merged, ua, ub = _merge_out(ya, yb, ga, gb, xs, modv, wb, wout)
    h2, hp, dx2, stf = _mlp_fwd(x1, modv, n2g, wmi, wmo, fg, tgt)

    dhp, dx1, st2 = _mlp_bwd(dx2, x1, hp, modv, n2g, wmi, wmo)
    m2 = _tn_matmul(hp, dx2, 2048, D, "dw_mlp_out", relu_sq=True)
    g_wmo, dg2 = _scale_gate(m2, wmo.reshape(FF, D), modv, 5, "gate2_grad")
    g_wmi = _tn_matmul(h2, dhp, D, 512, "dw_mlp_in", dev_major=True)
    dua, dub, dga, dgb, dya, dyb, dl_a, dl_b = _merge_bwd(dx1, modv, ga, gb, ua, ub, ya, yb, wb, wout)
    m1 = _tn_matmul(merged, dx1, D, D, "dw_out")
    g_wout, dg1 = _scale_gate(m1, wout, modv, 2, "gate1_grad")
    g_wb0 = _tn_matmul(ya, dua, BW, D, "dw_branch_a")
    g_wb1 = _tn_matmul(yb, dub, BW, D, "dw_branch_b")
    g_rest = jnp.concatenate([
        jnp.stack([g_wb0, g_wb1]).reshape(2, BW, N_DEV, 128).transpose(2, 0, 1, 3).reshape(N_DEV, 128, D),
        g_wout.reshape(N_DEV, 128, D),
        g_wmi.reshape(N_DEV, 512, D),
        g_wmo.reshape(N_DEV, 512, D)], axis=1)
    ka_t3 = ka.reshape(S // tk, tk, 128).transpose(0, 2, 1)
    dqa_t, dka_t, dva_t, rest_recv = _attn_a_bwd(qa, qa.T, ka, ka_t3, va, dya, dya.T, lse_a, dl_a, g_rest)
    dqa = dqa_t.T
    dka = dka_t.transpose(0, 2, 1).reshape(S, 128)
    dva = dva_t.transpose(0, 2, 1).reshape(S, 128)
    dqb, dkbp, dvbp, dsink = _attn_b_bwd(qb, kbp, vbp, sink, dyb, lse_b, dl_b)
    dproj, stqk = _qk_bwd(dqa, dka, dva, dqb, dkbp[WIN:WIN + S], dvbp[WIN:WIN + S], qar, kar, qg2, kg2, tab_a, tab_b, dga, dgb)
    g_win = _tn_matmul(h, dproj, D, 896, "dw_in")
    g_win = g_win.reshape(D, N_DEV, NI).transpose(1, 0, 2).reshape(N_DEV, NI, D)
    grad_x, st1, win_recv = _in_bwd(dproj, win, xs, dx1, modv, n1g, g_win.astype(BF16))

    rest_out = _adamw_sum([(g_rest, me1)] + [(rest_recv, k) for k in range(N_DEV - 1)],
                          _pack_rest(w_branch[0], w_out[0], w_mlp_in[0], w_mlp_out[0]),
                          _pack_rest(m_w_branch[0], m_w_out[0], m_w_mlp_in[0], m_w_mlp_out[0]),
                          _pack_rest(v_w_branch[0], v_w_out[0], v_w_mlp_in[0], v_w_mlp_out[0]), "adamw_rest")
    rest_out = [_unpack_rest(a) for a in rest_out]
    win_out = _adamw_sum([(g_win, me1)] + [(win_recv, k) for k in range(N_DEV - 1)],
                         w_in[0].reshape(NI, D), m_w_in[0].reshape(NI, D), v_w_in[0].reshape(NI, D), "adamw_w_in")
    win_out = [a.reshape(1, D, NI) for a in win_out]

    small = _pack_small(st1, st2, stf, dg1, dg2, stqk, dsink, modv, n1g, n2g)
    small_all = _small_gather(small)
    sw = _pack_small_params(b_ada, norm1_g, norm2_g, final_g, q_norm_a, k_norm_a, sink_b)
    sm = _pack_small_params(m_b_ada, m_norm1_g, m_norm2_g, m_final_g, m_q_norm_a, m_k_norm_a, m_sink_b)
    sv = _pack_small_params(v_b_ada, v_norm1_g, v_norm2_g, v_final_g, v_q_norm_a, v_k_norm_a, v_sink_b)
    sm_out = _adamw_sum([(small_all, k) for k in range(N_DEV)], sw, sm, sv, "adamw_small")
    loss = sm_out[0][11, 0]
    sm_out = [_unpack_small(a) for a in sm_out]

    dmod_all = small_all[:, 0:6, :].reshape(N_DEV, 6 * D)
    dmod_cols = lax.dynamic_slice_in_dim(dmod_all, me * NW, NW, axis=1)
    g_wada = _wada_grad(silu_all, dmod_cols)
    ada = _adamw_sum([(g_wada, None)], w_ada.reshape(D, NW), m_w_ada.reshape(D, NW), v_w_ada.reshape(D, NW), "adamw_ada")
    ada = [a.reshape(1, D, NW) for a in ada]

    def leaves(k):
        b_, n1_, qn_, kn_, sk_, n2_, fg_ = sm_out[k]
        wbr_, wo_, wmi_, wmo_ = rest_out[k]
        return [ada[k], b_, n1_, win_out[k], qn_, kn_, sk_, wbr_, wo_, n2_, wmi_, wmo_, fg_]

    return (loss, grad_x.reshape(1, S, D), *leaves(0), *leaves(1), *leaves(2), *leaves(3))
```

```python
import jax
import jax.numpy as jnp
from jax import lax
from jax.experimental import pallas as pl
from jax.experimental.pallas import tpu as pltpu

F32, BF16 = jnp.float32, jnp.bfloat16
MESH = pl.DeviceIdType.MESH

D = 1024
HD = 64
NH = 8
GRP = 4
BW = 512
FF = 4096
INW = 3584
GRID_W = 64
WIN = 128
THETA = 10000.0
EPS = 1e-6
NEG = -1e30
N_DEV = 8
LOG2E = 1.4426950408889634
LN2 = 0.6931471805599453
QA_SCALE = 0.125 * LOG2E
SMALL_ROWS = 16
V7X_VMEM_LIMIT = 56 * 1024 * 1024

ADAM_LR, ADAM_B1, ADAM_B2, ADAM_EPS, ADAM_WD, ADAM_STEP = 0.001, 0.9, 0.999, 1e-08, 0.01, 10

NT = (((1,), (1,)), ((), ()))
TN = (((0,), (0,)), ((), ()))


def _params(n_axes, vmem=V7X_VMEM_LIMIT):
    return pltpu.CompilerParams(dimension_semantics=("arbitrary",) * n_axes, vmem_limit_bytes=vmem)


def _const(shape):
    return pl.BlockSpec(shape, lambda *_: (0,) * len(shape))


def _rows(tm, width):
    return pl.BlockSpec((tm, width), lambda i, *_: (i, 0))


def _seg_matrix(n, seg):
    r = lax.broadcasted_iota(jnp.int32, (n, n), 0) // seg
    c = lax.broadcasted_iota(jnp.int32, (n, n), 1) // seg
    return (r == c).astype(F32)


def _seg_sum(z, seg_mat):
    return jnp.dot(z, seg_mat, preferred_element_type=F32, precision=lax.Precision.HIGHEST)


def _rope(z, t_ref, sh):
    return z * t_ref[0] + pltpu.roll(z, sh, 1) * t_ref[1] + pltpu.roll(z, 128 - sh, 1) * t_ref[2]


def _rope_t(dz, t_ref, sh):
    return dz * t_ref[0] + pltpu.roll(dz * t_ref[1], 128 - sh, 1) + pltpu.roll(dz * t_ref[2], sh, 1)


def _rope_tables(S):
    t = jnp.arange(S, dtype=jnp.int32)
    lane = jnp.arange(HD)

    def build(cos, sin, first):
        t0 = cos
        t1 = jnp.where(first[None, :], 0.0, sin)
        t2 = jnp.where(first[None, :], -sin, 0.0)
        return jnp.stack([jnp.tile(a, (1, 2)) for a in (t0, t1, t2)]).astype(F32)

    inv_a = THETA ** (-jnp.arange(0, HD // 2, 2, dtype=F32) / (HD // 2))
    ar = (t // GRID_W).astype(F32)[:, None] * inv_a[None, :]
    ac = (t % GRID_W).astype(F32)[:, None] * inv_a[None, :]
    cos_a = jnp.concatenate([jnp.cos(ar), jnp.cos(ar), jnp.cos(ac), jnp.cos(ac)], axis=1)
    sin_a = jnp.concatenate([jnp.sin(ar), jnp.sin(ar), jnp.sin(ac), jnp.sin(ac)], axis=1)
    tab_a = build(cos_a, sin_a, (lane % 32) < 16)
    inv_b = THETA ** (-jnp.arange(0, HD, 2, dtype=F32) / HD)
    ab = t.astype(F32)[:, None] * inv_b[None, :]
    cos_b = jnp.concatenate([jnp.cos(ab), jnp.cos(ab)], axis=1)
    sin_b = jnp.concatenate([jnp.sin(ab), jnp.sin(ab)], axis=1)
    tab_b = build(cos_b, sin_b, lane < 32)
    return tab_a, tab_b


def _in_proj(x, modv, n1g, win, qg2, kg2, tab_a, tab_b):
    S = x.shape[0]
    tm = min(256, S)

    def body(x_ref, mod_ref, g_ref, w_ref, qg_ref, kg_ref, ta_ref, tb_ref,
             h_ref, qar_ref, kar_ref, qa_ref, ka_ref, va_ref, qb_ref, kb_ref, vb_ref, ga_ref, gb_ref):
        xt = x_ref[...]
        r = lax.rsqrt(jnp.mean(xt * xt, axis=-1, keepdims=True) + EPS)
        h = ((xt * r) * g_ref[...]) * (1.0 + mod_ref[1:2, :]) + mod_ref[0:1, :]
        hb = h.astype(BF16)
        h_ref[...] = hb
        proj = jnp.dot(hb, w_ref[...], preferred_element_type=F32)
        seg = _seg_matrix(128, HD)

        def head_norm(z, g):
            ms = _seg_sum(z * z, seg) * (1.0 / HD)
            return (z * lax.rsqrt(ms + EPS)) * g

        for p in range(4):
            z = proj[:, 128 * p:128 * p + 128]
            qar_ref[:, 128 * p:128 * p + 128] = z.astype(BF16)
            qa_ref[:, 128 * p:128 * p + 128] = (_rope(head_norm(z, qg_ref[...]), ta_ref, 16) * QA_SCALE).astype(BF16)
            zb = proj[:, 768 + 128 * p:768 + 128 * p + 128]
            qb_ref[:, 128 * p:128 * p + 128] = (_rope(zb, tb_ref, 32) * 0.125).astype(BF16)
        z = proj[:, 512:640]
        kar_ref[...] = z.astype(BF16)
        ka_ref[...] = _rope(head_norm(z, kg_ref[...]), ta_ref, 16).astype(BF16)
        va_ref[...] = proj[:, 640:768].astype(BF16)
        kb_ref[...] = _rope(proj[:, 1280:1408], tb_ref, 32).astype(BF16)
        vb_ref[...] = proj[:, 1408:1536].astype(BF16)
        ga_ref[...] = proj[:, 1536:2560].astype(BF16)
        gb_ref[...] = proj[:, 2560:3584].astype(BF16)

    tab = pl.BlockSpec((3, tm, 128), lambda i: (0, i, 0))
    shapes = [(D, BF16), (BW, BF16), (128, BF16), (BW, BF16), (128, BF16), (128, BF16),
              (BW, BF16), (128, BF16), (128, BF16), (D, BF16), (D, BF16)]
    return pl.pallas_call(
        body, grid=(S // tm,),
        in_specs=[_rows(tm, D), _const((6, D)), _const((1, D)), _const((D, INW)), _const((1, 128)), _const((1, 128)), tab, tab],
        out_specs=[_rows(tm, w) for w, _ in shapes],
        out_shape=[jax.ShapeDtypeStruct((S, w), dt) for w, dt in shapes],
        compiler_params=_params(1), name="in_proj",
    )(x, modv, n1g, win, qg2, kg2, tab_a, tab_b)


def _exchange_gather(block_refs, out_refs, send_sems, recv_sems, local_sems):
    x, y, c = _me()
    me = 4 * x + 2 * y + c

    def copies():
        own, out, arrive = [], [], []
        for a, (blk, dst) in enumerate(zip(block_refs, out_refs)):
            own.append(pltpu.make_async_copy(blk, dst.at[me], local_sems.at[a]))
            for k in range(1, N_DEV):
                sems = dict(send_sem=send_sems.at[a, k - 1], recv_sem=recv_sems.at[a, k - 1], device_id=_peer(k), device_id_type=MESH)
                out.append(pltpu.make_async_remote_copy(blk, dst.at[me], **sems))
                arrive.append(pltpu.make_async_remote_copy(blk, dst.at[me ^ k], **sems))
        return own, out, arrive

    def start():
        own, out, _ = copies()
        for cp in own + out:
            cp.start()

    def finish():
        own, out, arrive = copies()
        for cp in arrive:
            cp.wait_recv()
        for cp in out:
            cp.wait_send()
        for cp in own:
            cp.wait()

    return start, finish


def _exchange_scatter(chunk_refs, recv_refs, send_sems, recv_sems):
    x, y, c = _me()
    me = 4 * x + 2 * y + c

    def copies():
        return [pltpu.make_async_remote_copy(src.at[me ^ k], dst.at[k - 1], send_sems.at[a, k - 1], recv_sems.at[a, k - 1],
                                             device_id=_peer(k), device_id_type=MESH)
                for a, (src, dst) in enumerate(zip(chunk_refs, recv_refs)) for k in range(1, N_DEV)]

    def start():
        for cp in copies():
            cp.start()

    def finish():
        cps = copies()
        for cp in cps:
            cp.wait_recv()
        for cp in cps:
            cp.wait_send()

    return start, finish


def _exchange_sems(n):
    return [pltpu.SemaphoreType.DMA((n, N_DEV - 1)), pltpu.SemaphoreType.DMA((n, N_DEV - 1))]


def _attn_a_fwd(q, k, vt3, shards):
    S = q.shape[0]
    tq = min(512, S)
    nq = S // tq
    nk, _, tk = vt3.shape
    ONES = 16
    ns = len(shards)

    def body(q_ref, k_ref, vt_ref, *rest):
        w_hbm, (o_ref, lse_ref), wall_hbm = rest[:ns], rest[ns:ns + 2], rest[ns + 2:2 * ns + 2]
        st_sc, send_sems, recv_sems, local_sems = rest[2 * ns + 2:]
        start, finish = _exchange_gather(w_hbm, wall_hbm, send_sems, recv_sems, local_sems)
        pl.when(pl.program_id(0) == 0)(start)
        row8 = lax.broadcasted_iota(jnp.int32, (NH, tq), 0)
        lse_all = jnp.zeros((NH, tq), F32)
        ones = jnp.ones((ONES, tk), BF16)
        for kv in range(2):
            qs = []
            for pp in range(2):
                qp = q_ref[:, 128 * (2 * kv + pp):128 * (2 * kv + pp) + 128]
                qs += [qp[:, :HD], qp[:, HD:]]

            def keys(j, kv=kv):
                return k_ref[pl.ds(pl.multiple_of(j * tk, tk), tk), :][:, HD * kv:HD * kv + HD]

            def scores(kj, t, qs=qs):
                return lax.dot_general(kj, qs[t], NT, preferred_element_type=F32)

            def step(j, carry, kv=kv):
                kj = keys(j)
                kn = keys(jnp.minimum(j + 1, nk - 1))
                v1 = jnp.concatenate([vt_ref[j, HD * kv:HD * kv + HD, :], ones], axis=0)
                sts = [st_sc[0], st_sc[1]]
                new = []
                for t in range(GRP):
                    m, acc = carry[2 * t], carry[2 * t + 1]
                    if t + 2 < GRP:
                        sts.append(scores(kj, t + 2))
                    st = sts[t]
                    mn = jnp.maximum(m, jnp.max(st, axis=0, keepdims=True))
                    pt = jnp.exp2(st - mn)
                    if t + 2 >= GRP:
                        st_sc[t + 2 - GRP] = scores(kn, t + 2 - GRP)
                    acc = jnp.exp2(m - mn) * acc + jnp.dot(v1, pt.astype(BF16), preferred_element_type=F32)
                    new += [mn, acc]
                return tuple(new)

            k0 = keys(0)
            st_sc[0] = scores(k0, 0)
            st_sc[1] = scores(k0, 1)
            init = (jnp.full((1, tq), NEG, F32), jnp.zeros((HD + ONES, tq), F32)) * GRP
            res = lax.fori_loop(0, nk, step, init)
            for t in range(GRP):
                h = GRP * kv + t
                m, acc = res[2 * t], res[2 * t + 1]
                l = acc[HD:HD + 1, :]
                o_ref[HD * h:HD * h + HD, :] = (acc[:HD, :] / l).astype(BF16)
                lse_all = jnp.where(row8 == h, m + jnp.log2(l), lse_all)
        lse_ref[...] = lse_all
        pl.when(pl.program_id(0) == nq - 1)(finish)

    any_spec = pl.BlockSpec(memory_space=pl.ANY)
    return pl.pallas_call(
        body, grid=(nq,),
        in_specs=[_rows(tq, BW), _const((S, 128)), _const((nk, 128, tk))] + [any_spec] * ns,
        out_specs=[pl.BlockSpec((BW, tq), lambda i: (0, i)), pl.BlockSpec((NH, tq), lambda i: (0, i))] + [any_spec] * ns,
        out_shape=[jax.ShapeDtypeStruct((BW, S), BF16), jax.ShapeDtypeStruct((NH, S), F32)]
        + [jax.ShapeDtypeStruct((N_DEV,) + s.shape, s.dtype) for s in shards],
        scratch_shapes=[pltpu.VMEM((2, tk, tq), F32)] + _exchange_sems(ns) + [pltpu.SemaphoreType.DMA((ns,))],
        compiler_params=_params(1), name="attn_a_fwd",
    )(q, k, vt3, *shards)


def _window_mask(i, tq, S):
    W = tq + 2 * WIN
    r = lax.broadcasted_iota(jnp.int32, (tq, W), 0)
    c = lax.broadcasted_iota(jnp.int32, (tq, W), 1)
    kpos = i * tq - WIN + c
    return (jnp.abs(c - WIN - r) <= WIN) & (kpos >= 0) & (kpos < S)


def _attn_b_fwd(q, kp, vp, sink):
    S = q.shape[0]
    tq = min(512, S)
    W = tq + 2 * WIN

    def body(q_ref, k_ref, v_ref, sink_ref, o_ref, lse_ref):
        i = pl.program_id(0)
        off = pl.multiple_of(i * tq, tq)
        valid = _window_mask(i, tq, S)
        kw = k_ref[pl.ds(off, W), :]
        vw = v_ref[pl.ds(off, W), :]
        lane8 = lax.broadcasted_iota(jnp.int32, (tq, NH), 1)
        lse_all = jnp.zeros((tq, NH), F32)
        for p in range(4):
            qp = q_ref[:, 128 * p:128 * p + 128]
            kv = p // 2
            kh = kw[:, HD * kv:HD * kv + HD]
            vh = vw[:, HD * kv:HD * kv + HD]
            outs = []
            for hh in range(2):
                h = 2 * p + hh
                s = lax.dot_general(qp[:, HD * hh:HD * hh + HD], kh, NT, preferred_element_type=F32)
                s = jnp.where(valid, s, NEG)
                sk = sink_ref[:, h:h + 1]
                m = jnp.maximum(jnp.max(s, axis=1, keepdims=True), sk)
                pm = jnp.exp(s - m)
                l = jnp.sum(pm, axis=1, keepdims=True) + jnp.exp(sk - m)
                outs.append(jnp.dot(pm.astype(BF16), vh, preferred_element_type=F32) / l)
                lse_all = jnp.where(lane8 == h, m + jnp.log(l), lse_all)
            o_ref[:, 128 * p:128 * p + 128] = jnp.concatenate(outs, axis=1).astype(BF16)
        lse_ref[...] = lse_all

    return pl.pallas_call(
        body, grid=(S // tq,),
        in_specs=[_rows(tq, BW), _const((S + 2 * WIN, 128)), _const((S + 2 * WIN, 128)), _const((1, NH))],
        out_specs=[_rows(tq, BW), _rows(tq, NH)],
        out_shape=[jax.ShapeDtypeStruct((S, BW), BF16), jax.ShapeDtypeStruct((S, NH), F32)],
        compiler_params=_params(1), name="attn_b_fwd",
    )(q, kp, vp, sink)


def _merge_out(ya, yb, ga, gb, x, modv, wb, wout):
    S = x.shape[0]
    tm = min(256, S)

    def body(ya_ref, yb_ref, ga_ref, gb_ref, x_ref, mod_ref, wb_ref, wo_ref, x1_ref, mg_ref, ua_ref, ub_ref):
        ua = jnp.dot(ya_ref[...], wb_ref[0], preferred_element_type=F32)
        ub = jnp.dot(yb_ref[...], wb_ref[1], preferred_element_type=F32)
        merged = jax.nn.sigmoid(ga_ref[...].astype(F32)) * ua + jax.nn.sigmoid(gb_ref[...].astype(F32)) * ub
        mb = merged.astype(BF16)
        ua_ref[...] = ua.astype(BF16)
        ub_ref[...] = ub.astype(BF16)
        mg_ref[...] = mb
        x1_ref[...] = x_ref[...] + mod_ref[2:3, :] * jnp.dot(mb, wo_ref[...], preferred_element_type=F32)

    return pl.pallas_call(
        body, grid=(S // tm,),
        in_specs=[_rows(tm, BW), _rows(tm, BW), _rows(tm, D), _rows(tm, D), _rows(tm, D), _const((6, D)),
                  _const((2, BW, D)), _const((D, D))],
        out_specs=[_rows(tm, D)] * 4,
        out_shape=[jax.ShapeDtypeStruct((S, D), F32)] + [jax.ShapeDtypeStruct((S, D), BF16)] * 3,
        compiler_params=_params(1), name="merge_out",
    )(ya, yb, ga, gb, x, modv, wb, wout)


def _mlp_fwd(x1, modv, n2g, wmi, wmo, fg, target):
    S = x1.shape[0]
    tm = min(512, S)
    tf = wmi.shape[2]
    nj = wmi.shape[0]

    def body(x1_ref, mod_ref, g_ref, wi_ref, wo_ref, fg_ref, t_ref, h2_ref, hp_ref, dx2_ref, st_ref, acc_ref):
        i, j = pl.program_id(0), pl.program_id(1)

        @pl.when(j == 0)
        def _():
            xt = x1_ref[...]
            r = lax.rsqrt(jnp.mean(xt * xt, axis=-1, keepdims=True) + EPS)
            h2 = ((xt * r) * g_ref[...]) * (1.0 + mod_ref[4:5, :]) + mod_ref[3:4, :]
            h2_ref[...] = h2.astype(BF16)
            acc_ref[...] = jnp.zeros_like(acc_ref)

        @pl.when((i == 0) & (j == 0))
        def _():
            st_ref[...] = jnp.zeros_like(st_ref)

        hp = jnp.dot(h2_ref[...], wi_ref[...], preferred_element_type=F32)
        hp_ref[...] = hp.astype(BF16)
        hid = jnp.square(jnp.maximum(hp, 0.0))
        acc_ref[...] += jnp.dot(hid.astype(BF16), wo_ref[...], preferred_element_type=F32)

        @pl.when(j == nj - 1)
        def _():
            x2 = x1_ref[...] + mod_ref[5:6, :] * acc_ref[...]
            r3 = lax.rsqrt(jnp.mean(x2 * x2, axis=-1, keepdims=True) + EPS)
            xn = x2 * r3
            err = xn * fg_ref[...] - t_ref[...]
            dy = err * (1.0 / D)
            gy = dy * fg_ref[...]
            dx2_ref[...] = r3 * (gy - xn * jnp.mean(gy * xn, axis=-1, keepdims=True))
            st_ref[0:1, :] += jnp.sum(dy * xn, axis=0, keepdims=True)
            st_ref[1:2, :] += jnp.sum(err * err, axis=0, keepdims=True) * (0.5 / D)

    return pl.pallas_call(
        body, grid=(S // tm, nj),
        in_specs=[pl.BlockSpec((tm, D), lambda i, j: (i, 0)), _const((6, D)), _const((1, D)),
                  pl.BlockSpec((None, D, tf), lambda i, j: (j, 0, 0)), pl.BlockSpec((None, tf, D), lambda i, j: (j, 0, 0)),
                  _const((1, D)), pl.BlockSpec((tm, D), lambda i, j: (i, 0))],
        out_specs=[pl.BlockSpec((tm, D), lambda i, j: (i, 0)), pl.BlockSpec((tm, tf), lambda i, j: (i, j)),
                   pl.BlockSpec((tm, D), lambda i, j: (i, 0)), _const((8, D))],
        out_shape=[jax.ShapeDtypeStruct((S, D), BF16), jax.ShapeDtypeStruct((S, nj * tf), BF16),
                   jax.ShapeDtypeStruct((S, D), F32), jax.ShapeDtypeStruct((8, D), F32)],
        scratch_shapes=[pltpu.VMEM((tm, D), F32)],
        compiler_params=_params(2), name="mlp_fwd",
    )(x1, modv, n2g, wmi, wmo, fg, target)


def _mlp_bwd(dx2, x1, hp, modv, n2g, wmi, wmo):
    S = x1.shape[0]
    tm = min(512, S)
    tf = wmi.shape[2]
    nj = wmi.shape[0]

    def body(dx2_ref, x1_ref, hp_ref, mod_ref, g_ref, wi_ref, wo_ref, dhp_ref, dx1_ref, st_ref, dmo_ref, acc_ref):
        i, j = pl.program_id(0), pl.program_id(1)

        @pl.when(j == 0)
        def _():
            dmo_ref[...] = (mod_ref[5:6, :] * dx2_ref[...]).astype(BF16)
            acc_ref[...] = jnp.zeros_like(acc_ref)

        @pl.when((i == 0) & (j == 0))
        def _():
            st_ref[...] = jnp.zeros_like(st_ref)

        dhid = lax.dot_general(dmo_ref[...], wo_ref[...], NT, preferred_element_type=F32)
        dhp = (dhid * (2.0 * jnp.maximum(hp_ref[...].astype(F32), 0.0))).astype(BF16)
        dhp_ref[...] = dhp
        acc_ref[...] += lax.dot_general(dhp, wi_ref[...], NT, preferred_element_type=F32)

        @pl.when(j == nj - 1)
        def _():
            dh2 = acc_ref[...]
            xt = x1_ref[...]
            r = lax.rsqrt(jnp.mean(xt * xt, axis=-1, keepdims=True) + EPS)
            xn = xt * r
            st_ref[0:1, :] += jnp.sum(dh2, axis=0, keepdims=True)
            st_ref[1:2, :] += jnp.sum(dh2 * xn, axis=0, keepdims=True)
            dxn = dh2 * (g_ref[...] * (1.0 + mod_ref[4:5, :]))
            dx1_ref[...] = dx2_ref[...] + r * (dxn - xn * jnp.mean(dxn * xn, axis=-1, keepdims=True))

    return pl.pallas_call(
        body, grid=(S // tm, nj),
        in_specs=[pl.BlockSpec((tm, D), lambda i, j: (i, 0)), pl.BlockSpec((tm, D), lambda i, j: (i, 0)),
                  pl.BlockSpec((tm, tf), lambda i, j: (i, j)), _const((6, D)), _const((1, D)),
                  pl.BlockSpec((None, D, tf), lambda i, j: (j, 0, 0)), pl.BlockSpec((None, tf, D), lambda i, j: (j, 0, 0))],
        out_specs=[pl.BlockSpec((tm, tf), lambda i, j: (i, j)), pl.BlockSpec((tm, D), lambda i, j: (i, 0)), _const((8, D))],
        out_shape=[jax.ShapeDtypeStruct((S, nj * tf), BF16), jax.ShapeDtypeStruct((S, D), F32), jax.ShapeDtypeStruct((8, D), F32)],
        scratch_shapes=[pltpu.VMEM((tm, D), BF16), pltpu.VMEM((tm, D), F32)],
        compiler_params=_params(2), name="mlp_bwd",
    )(dx2, x1, hp, modv, n2g, wmi, wmo)


def _tn_matmul(a, b, tk, tn, name, relu_sq=False, dev_major=False):
    S, K = a.shape
    N = b.shape[1]
    ts = min(1024, S)
    ns = S // ts

    def body(a_ref, b_ref, o_ref):
        @pl.when(pl.program_id(2) == 0)
        def _():
            o_ref[...] = jnp.zeros_like(o_ref)

        at = a_ref[...]
        if relu_sq:
            at = jnp.square(jnp.maximum(at.astype(F32), 0.0)).astype(BF16)
        o_ref[...] += lax.dot_general(at, b_ref[...].astype(BF16), TN, preferred_element_type=F32)

    if dev_major:
        out_spec = pl.BlockSpec((None, tk, tn), lambda k, n, s: (n, k, 0))
        out_shape = jax.ShapeDtypeStruct((N // tn, K, tn), F32)
    else:
        out_spec = pl.BlockSpec((tk, tn), lambda k, n, s: (k, n))
        out_shape = jax.ShapeDtypeStruct((K, N), F32)
    return pl.pallas_call(
        body, grid=(K // tk, N // tn, ns),
        in_specs=[pl.BlockSpec((ts, tk), lambda k, n, s: (s, k)), pl.BlockSpec((ts, tn), lambda k, n, s: (s, n))],
        out_specs=out_spec, out_shape=out_shape,
        compiler_params=_params(3), name=name,
    )(a, b)


def _scale_gate(m, w, g, row, name):
    K = m.shape[0]
    tk = min(512, K)

    def body(m_ref, w_ref, mod_ref, dw_ref, dg_ref):
        @pl.when(pl.program_id(0) == 0)
        def _():
            dg_ref[...] = jnp.zeros_like(dg_ref)

        mt = m_ref[...]
        dw_ref[...] = mt * mod_ref[row:row + 1, :]
        dg_ref[0:1, :] += jnp.sum(mt * w_ref[...].astype(F32), axis=0, keepdims=True)

    return pl.pallas_call(
        body, grid=(K // tk,),
        in_specs=[_rows(tk, D), _rows(tk, D), _const((6, D))],
        out_specs=[_rows(tk, D), _const((8, D))],
        out_shape=[jax.ShapeDtypeStruct((K, D), F32), jax.ShapeDtypeStruct((8, D), F32)],
        compiler_params=_params(1), name=name,
    )(m, w, g)


def _merge_bwd(dx1, modv, ga, gb, ua, ub, ya, yb, wb, wout):
    S = dx1.shape[0]
    tm = min(256, S)

    def body(dx1_ref, mod_ref, ga_ref, gb_ref, ua_ref, ub_ref, ya_ref, yb_ref, wb_ref, wo_ref,
             dua_ref, dub_ref, dga_ref, dgb_ref, dya_ref, dyb_ref, dla_ref, dlb_ref):
        dao = (mod_ref[2:3, :] * dx1_ref[...]).astype(BF16)
        dm = lax.dot_general(dao, wo_ref[...], NT, preferred_element_type=F32)
        r = lax.broadcasted_iota(jnp.int32, (BW, NH), 0) // HD
        c = lax.broadcasted_iota(jnp.int32, (BW, NH), 1)
        head_of = (r == c).astype(F32)
        for br, (g_ref, u_ref, y_ref, du_ref, dg_ref, dy_ref, dl_ref) in enumerate((
                (ga_ref, ua_ref, ya_ref, dua_ref, dga_ref, dya_ref, dla_ref),
                (gb_ref, ub_ref, yb_ref, dub_ref, dgb_ref, dyb_ref, dlb_ref))):
            sg = jax.nn.sigmoid(g_ref[...].astype(F32))
            du = (dm * sg).astype(BF16)
            du_ref[...] = du
            dg_ref[...] = (dm * u_ref[...].astype(F32) * sg * (1.0 - sg)).astype(BF16)
            dy = lax.dot_general(du, wb_ref[br], NT, preferred_element_type=F32)
            dyb16 = dy.astype(BF16)
            dy_ref[...] = dyb16
            dl_ref[...] = jnp.dot(dyb16.astype(F32) * y_ref[...].astype(F32), head_of,
                                  preferred_element_type=F32, precision=lax.Precision.HIGHEST)

    return pl.pallas_call(
        body, grid=(S // tm,),
        in_specs=[_rows(tm, D), _const((6, D)), _rows(tm, D), _rows(tm, D), _rows(tm, D), _rows(tm, D),
                  _rows(tm, BW), _rows(tm, BW), _const((2, BW, D)), _const((D, D))],
        out_specs=[_rows(tm, D)] * 4 + [_rows(tm, BW)] * 2 + [_rows(tm, NH)] * 2,
        out_shape=[jax.ShapeDtypeStruct((S, D), BF16)] * 4 + [jax.ShapeDtypeStruct((S, BW), BF16)] * 2
        + [jax.ShapeDtypeStruct((S, NH), F32)] * 2,
        compiler_params=_params(1), name="merge_bwd",
    )(dx1, modv, ga, gb, ua, ub, ya, yb, wb, wout)


def _attn_a_bwd(q, qt, k, kt3, v, do, dot_, lse, delta, chunks):
    S = q.shape[0]
    tq = min(512, S)
    tk = min(512, S)
    nq, nk = S // tq, S // tk

    nc = len(chunks)

    def body(q_ref, qt_ref, do_ref, dot_ref, lse_ref, dl_ref, k_ref, v_ref, kt_ref, *rest):
        g_hbm, (dq_ref, dk_hbm, dv_hbm), recv_hbm = rest[:nc], rest[nc:nc + 3], rest[nc + 3:2 * nc + 3]
        dk_sc, dv_sc, sem, send_sems, recv_sems = rest[2 * nc + 3:]
        i = pl.program_id(0)
        start, finish = _exchange_scatter(g_hbm, recv_hbm, send_sems, recv_sems)

        @pl.when(i == 0)
        def _():
            start()
            dk_sc[...] = jnp.zeros_like(dk_sc)
            dv_sc[...] = jnp.zeros_like(dv_sc)

        for kv in range(2):
            qg = q_ref[:, 256 * kv:256 * kv + 256]
            dog = do_ref[:, 256 * kv:256 * kv + 256]
            heads = []
            for t in range(GRP):
                h = GRP * kv + t
                heads.append((qg[:, HD * t:HD * t + HD], dog[:, HD * t:HD * t + HD],
                              qt_ref[HD * h:HD * h + HD, :], dot_ref[HD * h:HD * h + HD, :],
                              lse_ref[:, h:h + 1], dl_ref[:, h:h + 1]))

            def step(j, carry, kv=kv, heads=heads):
                off = pl.multiple_of(j * tk, tk)
                kj = k_ref[pl.ds(off, tk), :][:, HD * kv:HD * kv + HD]
                vj = v_ref[pl.ds(off, tk), :][:, HD * kv:HD * kv + HD]
                kjt = kt_ref[j, HD * kv:HD * kv + HD, :]
                dkt = jnp.zeros((HD, tk), F32)
                dvt = jnp.zeros((HD, tk), F32)
                new = []

                def logits(t):
                    return (lax.dot_general(heads[t][0], kj, NT, preferred_element_type=F32),
                            lax.dot_general(heads[t][1], vj, NT, preferred_element_type=F32))

                sd = [logits(0)]
                for t, (qh, doh, qth, doth, lse_h, dl_h) in enumerate(heads):
                    if t + 1 < GRP:
                        sd.append(logits(t + 1))
                    s, dp = sd[t]
                    pm = jnp.exp2(s - lse_h)
                    ds = (pm * (dp - dl_h)).astype(BF16)
                    dvt = dvt + jnp.dot(doth, pm.astype(BF16), preferred_element_type=F32)
                    dkt = dkt + jnp.dot(qth, ds, preferred_element_type=F32)
                    new.append(carry[t] + lax.dot_general(kjt, ds, NT, preferred_element_type=F32))
                dk_sc[j, HD * kv:HD * kv + HD, :] += dkt
                dv_sc[j, HD * kv:HD * kv + HD, :] += dvt
                return tuple(new)

            res = lax.fori_loop(0, nk, step, (jnp.zeros((HD, tq), F32),) * GRP)
            for t in range(GRP):
                dq_ref[HD * (GRP * kv + t):HD * (GRP * kv + t) + HD, :] = res[t]

        @pl.when(i == nq - 1)
        def _():
            c1 = pltpu.make_async_copy(dk_sc, dk_hbm, sem.at[0])
            c2 = pltpu.make_async_copy(dv_sc, dv_hbm, sem.at[1])
            c1.start()
            c2.start()
            c1.wait()
            c2.wait()
            finish()

    any_spec = pl.BlockSpec(memory_space=pl.ANY)
    cols = pl.BlockSpec((BW, tq), lambda i: (0, i))
    return pl.pallas_call(
        body, grid=(nq,),
        in_specs=[_rows(tq, BW), cols, _rows(tq, BW), cols, _rows(tq, NH), _rows(tq, NH), _const((S, 128)), _const((S, 128)),
                  _const((nk, 128, tk))] + [any_spec] * nc,
        out_specs=[cols, any_spec, any_spec] + [any_spec] * nc,
        out_shape=[jax.ShapeDtypeStruct((BW, S), F32), jax.ShapeDtypeStruct((nk, 128, tk), F32),
                   jax.ShapeDtypeStruct((nk, 128, tk), F32)]
        + [jax.ShapeDtypeStruct((N_DEV - 1,) + c.shape[1:], c.dtype) for c in chunks],
        scratch_shapes=[pltpu.VMEM((nk, 128, tk), F32), pltpu.VMEM((nk, 128, tk), F32), pltpu.SemaphoreType.DMA((2,))]
        + _exchange_sems(nc),
        compiler_params=_params(1), name="attn_a_bwd",
    )(q, qt, do, dot_, lse, delta, k, v, kt3, *chunks)


def _attn_b_bwd(q, kp, vp, sink, do, lse, delta):
    S = q.shape[0]
    tq = min(512, S)
    W = tq + 2 * WIN
    nq = S // tq

    def body(q_ref, k_ref, v_ref, sink_ref, do_ref, lse_ref, dl_ref, dq_ref, dk_hbm, dv_hbm, ds_ref, dk_sc, dv_sc, sem):
        i = pl.program_id(0)

        @pl.when(i == 0)
        def _():
            dk_sc[...] = jnp.zeros_like(dk_sc)
            dv_sc[...] = jnp.zeros_like(dv_sc)
            ds_ref[...] = jnp.zeros_like(ds_ref)

        off = pl.multiple_of(i * tq, tq)
        valid = _window_mask(i, tq, S)
        kw = k_ref[pl.ds(off, W), :]
        vw = v_ref[pl.ds(off, W), :]
        lse_i = lse_ref[...]
        dl_i = dl_ref[...]
        dks, dvs = [], []
        for kv in range(2):
            kh = kw[:, HD * kv:HD * kv + HD]
            vh = vw[:, HD * kv:HD * kv + HD]
            dk = jnp.zeros((W, HD), F32)
            dv = jnp.zeros((W, HD), F32)
            for pp in range(2):
                p = 2 * kv + pp
                qp = q_ref[:, 128 * p:128 * p + 128]
                dop = do_ref[:, 128 * p:128 * p + 128]
                dqs = []
                for hh in range(2):
                    h = 2 * p + hh
                    qh = qp[:, HD * hh:HD * hh + HD]
                    doh = dop[:, HD * hh:HD * hh + HD]
                    s = jnp.where(valid, lax.dot_general(qh, kh, NT, preferred_element_type=F32), NEG)
                    pm = jnp.exp(s - lse_i[:, h:h + 1])
                    dp = lax.dot_general(doh, vh, NT, preferred_element_type=F32)
                    ds = (pm * (dp - dl_i[:, h:h + 1])).astype(BF16)
                    dv = dv + lax.dot_general(pm.astype(BF16), doh, TN, preferred_element_type=F32)
                    dk = dk + lax.dot_general(ds, qh, TN, preferred_element_type=F32)
                    dqs.append(jnp.dot(ds, kh, preferred_element_type=F32))
                dq_ref[:, 128 * p:128 * p + 128] = jnp.concatenate(dqs, axis=1)
            dks.append(dk)
            dvs.append(dv)
        dk_sc[pl.ds(off, W), :] += jnp.concatenate(dks, axis=1)
        dv_sc[pl.ds(off, W), :] += jnp.concatenate(dvs, axis=1)
        psd = jnp.exp(sink_ref[...] - lse_i) * dl_i
        r = lax.broadcasted_iota(jnp.int32, (NH, 128), 0)
        c = lax.broadcasted_iota(jnp.int32, (NH, 128), 1)
        row = jnp.dot(jnp.sum(psd, axis=0, keepdims=True), (r == c).astype(F32),
                      preferred_element_type=F32, precision=lax.Precision.HIGHEST)
        ds_ref[...] -= jnp.broadcast_to(row, (8, 128))

        @pl.when(i == nq - 1)
        def _():
            c1 = pltpu.make_async_copy(dk_sc, dk_hbm, sem.at[0])
            c2 = pltpu.make_async_copy(dv_sc, dv_hbm, sem.at[1])
            c1.start()
            c2.start()
            c1.wait()
            c2.wait()

    any_spec = pl.BlockSpec(memory_space=pl.ANY)
    return pl.pallas_call(
        body, grid=(nq,),
        in_specs=[_rows(tq, BW), _const((S + 2 * WIN, 128)), _const((S + 2 * WIN, 128)), _const((1, NH)),
                  _rows(tq, BW), _rows(tq, NH), _rows(tq, NH)],
        out_specs=[_rows(tq, BW), any_spec, any_spec, _const((8, 128))],
        out_shape=[jax.ShapeDtypeStruct((S, BW), F32), jax.ShapeDtypeStruct((S + 2 * WIN, 128), F32),
                   jax.ShapeDtypeStruct((S + 2 * WIN, 128), F32), jax.ShapeDtypeStruct((8, 128), F32)],
        scratch_shapes=[pltpu.VMEM((S + 2 * WIN, 128), F32), pltpu.VMEM((S + 2 * WIN, 128), F32), pltpu.SemaphoreType.DMA((2,))],
        compiler_params=_params(1), name="attn_b_bwd",
    )(q, kp, vp, sink, do, lse, delta)


def _qk_bwd(dqa, dka, dva, dqb, dkb, dvb, qar, kar, qg2, kg2, tab_a, tab_b, dga, dgb):
    S = dqa.shape[0]
    tm = min(256, S)

    def body(dqa_ref, dka_ref, dva_ref, dqb_ref, dkb_ref, dvb_ref, qar_ref, kar_ref, qg_ref, kg_ref, ta_ref, tb_ref,
             dga_ref, dgb_ref, dp_ref, st_ref):
        @pl.when(pl.program_id(0) == 0)
        def _():
            st_ref[...] = jnp.zeros_like(st_ref)

        seg = _seg_matrix(128, HD)

        def norm_bwd(dz_rot, raw, g):
            dzn = _rope_t(dz_rot, ta_ref, 16)
            raw = raw.astype(F32)
            rr = lax.rsqrt(_seg_sum(raw * raw, seg) * (1.0 / HD) + EPS)
            zhat = raw * rr
            dzh = dzn * g
            draw = rr * (dzh - zhat * (_seg_sum(dzh * zhat, seg) * (1.0 / HD)))
            return draw, jnp.sum(dzn * zhat, axis=0, keepdims=True)

        gq = jnp.zeros((1, 128), F32)
        for p in range(4):
            sl = slice(128 * p, 128 * p + 128)
            draw, gsum = norm_bwd(dqa_ref[:, sl] * 0.125, qar_ref[:, sl], qg_ref[...])
            gq = gq + gsum
            dp_ref[:, sl] = draw.astype(BF16)
            dp_ref[:, 768 + 128 * p:768 + 128 * p + 128] = _rope_t(dqb_ref[:, sl] * 0.125, tb_ref, 32).astype(BF16)
        draw, gk = norm_bwd(dka_ref[...] * LN2, kar_ref[...], kg_ref[...])
        dp_ref[:, 512:640] = draw.astype(BF16)
        dp_ref[:, 640:768] = dva_ref[...].astype(BF16)
        dp_ref[:, 1280:1408] = _rope_t(dkb_ref[...], tb_ref, 32).astype(BF16)
        dp_ref[:, 1408:1536] = dvb_ref[...].astype(BF16)
        dp_ref[:, 1536:2560] = dga_ref[...]
        dp_ref[:, 2560:3584] = dgb_ref[...]
        st_ref[0:1, :] += gq
        st_ref[1:2, :] += gk

    tab = pl.BlockSpec((3, tm, 128), lambda i: (0, i, 0))
    return pl.pallas_call(
        body, grid=(S // tm,),
        in_specs=[_rows(tm, BW), _rows(tm, 128), _rows(tm, 128), _rows(tm, BW), _rows(tm, 128), _rows(tm, 128),
                  _rows(tm, BW), _rows(tm, 128), _const((1, 128)), _const((1, 128)), tab, tab, _rows(tm, D), _rows(tm, D)],
        out_specs=[_rows(tm, INW), _const((8, 128))],
        out_shape=[jax.ShapeDtypeStruct((S, INW), BF16), jax.ShapeDtypeStruct((8, 128), F32)],
        compiler_params=_params(1), name="qk_bwd",
    )(dqa, dka, dva, dqb, dkb, dvb, qar, kar, qg2, kg2, tab_a, tab_b, dga, dgb)


def _in_bwd(dproj, win, x, dx1, modv, n1g, chunks):
    S = x.shape[0]
    tm = min(256, S)
    n = S // tm

    def body(dp_ref, w_ref, x_ref, dx1_ref, mod_ref, g_ref, c_hbm, gx_ref, st_ref, recv_hbm, send_sems, recv_sems):
        start, finish = _exchange_scatter([c_hbm], [recv_hbm], send_sems, recv_sems)

        @pl.when(pl.program_id(0) == 0)
        def _():
            start()
            st_ref[...] = jnp.zeros_like(st_ref)

        dh = lax.dot_general(dp_ref[...], w_ref[...], NT, preferred_element_type=F32)
        xt = x_ref[...]
        r = lax.rsqrt(jnp.mean(xt * xt, axis=-1, keepdims=True) + EPS)
        xn = xt * r
        st_ref[0:1, :] += jnp.sum(dh, axis=0, keepdims=True)
        st_ref[1:2, :] += jnp.sum(dh * xn, axis=0, keepdims=True)
        dxn = dh * (g_ref[...] * (1.0 + mod_ref[1:2, :]))
        gx_ref[...] = dx1_ref[...] + r * (dxn - xn * jnp.mean(dxn * xn, axis=-1, keepdims=True))
        pl.when(pl.program_id(0) == n - 1)(finish)

    any_spec = pl.BlockSpec(memory_space=pl.ANY)
    return pl.pallas_call(
        body, grid=(n,),
        in_specs=[_rows(tm, INW), _const((D, INW)), _rows(tm, D), _rows(tm, D), _const((6, D)), _const((1, D)), any_spec],
        out_specs=[_rows(tm, D), _const((8, D)), any_spec],
        out_shape=[jax.ShapeDtypeStruct((S, D), F32), jax.ShapeDtypeStruct((8, D), F32),
                   jax.ShapeDtypeStruct((N_DEV - 1,) + chunks.shape[1:], chunks.dtype)],
        scratch_shapes=_exchange_sems(1),
        compiler_params=_params(1), name="in_bwd",
    )(dproj, win, x, dx1, modv, n1g, chunks)


def _pack_small(st1, st2, stf, dg1, dg2, stqk, dsink, modv, n1g, n2g):
    def body(st1_ref, st2_ref, stf_ref, dg1_ref, dg2_ref, qk_ref, ds_ref, mod_ref, g1_ref, g2_ref, o_ref):
        a1, b1 = st1_ref[0:1, :], st1_ref[1:2, :]
        a2, b2 = st2_ref[0:1, :], st2_ref[1:2, :]
        r = lax.broadcasted_iota(jnp.int32, (128, D), 0)
        c = lax.broadcasted_iota(jnp.int32, (128, D), 1)
        fold_q = (c == r % HD).astype(F32)
        fold_k = (c == HD + r % HD).astype(F32)
        keep = (c == r).astype(F32)

        def place(v, sel):
            return jnp.dot(v, sel, preferred_element_type=F32, precision=lax.Precision.HIGHEST)

        loss = jnp.sum(stf_ref[1:2, :], axis=1, keepdims=True)
        lane = lax.broadcasted_iota(jnp.int32, (1, D), 1)
        rows = [a1, g1_ref[...] * b1, dg1_ref[0:1, :], a2, g2_ref[...] * b2, dg2_ref[0:1, :],
                (1.0 + mod_ref[1:2, :]) * b1, (1.0 + mod_ref[4:5, :]) * b2, stf_ref[0:1, :],
                place(qk_ref[0:1, :], fold_q) + place(qk_ref[1:2, :], fold_k),
                place(ds_ref[0:1, :], keep),
                jnp.where(lane == 0, loss, 0.0)]
        rows += [jnp.zeros((1, D), F32)] * (SMALL_ROWS - len(rows))
        for n, v in enumerate(rows):
            o_ref[n:n + 1, :] = v

    return pl.pallas_call(
        body, out_shape=jax.ShapeDtypeStruct((SMALL_ROWS, D), F32),
        compiler_params=pltpu.CompilerParams(vmem_limit_bytes=V7X_VMEM_LIMIT), name="pack_small",
    )(st1, st2, stf, dg1, dg2, stqk, dsink, modv, n1g, n2g)


def _wada_grad(silu_all, dmod_cols):
    def body(a_ref, b_ref, o_ref):
        o_ref[...] = lax.dot_general(a_ref[...], b_ref[...], TN, preferred_element_type=F32, precision=lax.Precision.HIGHEST)

    return pl.pallas_call(
        body, out_shape=jax.ShapeDtypeStruct((D, dmod_cols.shape[1]), F32),
        compiler_params=pltpu.CompilerParams(vmem_limit_bytes=V7X_VMEM_LIMIT), name="wada_grad",
    )(silu_all, dmod_cols)


def _adamw_sum(parts, w, m, v, name):
    R, C = w.shape
    tr = R if R <= 64 else next(t for t in (256, 128, 64, 32, 16, 8) if R % t == 0)
    n = len(parts)
    dyn = [idx for _, idx in parts if idx is not None and not isinstance(idx, int)]
    b1c = 1.0 - ADAM_B1 ** ADAM_STEP
    b2c = 1.0 - ADAM_B2 ** ADAM_STEP

    def body(*refs):
        refs = refs[len(dyn):]
        g = refs[0][...].astype(F32)
        for k in range(1, n):
            g = g + refs[k][...].astype(F32)
        w_ref, m_ref, v_ref, g_out, d_out, m_out, v_out = refs[n:]
        mn = ADAM_B1 * m_ref[...] + (1.0 - ADAM_B1) * g
        vn = ADAM_B2 * v_ref[...] + (1.0 - ADAM_B2) * jnp.square(g)
        g_out[...] = g
        m_out[...] = mn
        v_out[...] = vn
        d_out[...] = -ADAM_LR * ((mn / b1c) / (jnp.sqrt(vn / b2c) + ADAM_EPS) + ADAM_WD * w_ref[...])

    in_specs = []
    nd = 0
    for a, idx in parts:
        if idx is None:
            in_specs.append(pl.BlockSpec((tr, C), lambda i, *s: (i, 0)))
        elif isinstance(idx, int):
            in_specs.append(pl.BlockSpec((None, tr, C), lambda i, *s, idx=idx: (idx, i, 0)))
        else:
            in_specs.append(pl.BlockSpec((None, tr, C), lambda i, *s, nd=nd: (s[nd][0], i, 0)))
            nd += 1
    blk = pl.BlockSpec((tr, C), lambda i, *s: (i, 0))
    grid_spec = pltpu.PrefetchScalarGridSpec(
        num_scalar_prefetch=len(dyn), grid=(R // tr,), in_specs=in_specs + [blk] * 3, out_specs=[blk] * 4)
    return pl.pallas_call(
        body, grid_spec=grid_spec, out_shape=[jax.ShapeDtypeStruct((R, C), F32)] * 4,
        compiler_params=_params(1), name=name,
    )(*dyn, *[a for a, _ in parts], w, m, v)


def _me():
    return lax.axis_index("x"), lax.axis_index("y"), lax.axis_index("c")


def _peer(k):
    x, y, c = _me()
    return (x ^ ((k >> 2) & 1), y ^ ((k >> 1) & 1), c ^ (k & 1))


def _ada_exchange(c_row, w_ada, b_rows):
    NW = w_ada.shape[1]

    def body(c_ref, w_ref, b_ref, sall_ref, mod_ref, src_ref, mp_ref, send1, recv1, send2, recv2):
        x, y, c = _me()
        me = 4 * x + 2 * y + c
        cv = c_ref[...]
        src_ref[...] = jnp.broadcast_to(cv * jax.nn.sigmoid(cv), (8, D))
        mine = pl.ds(pl.multiple_of(me * 8, 8), 8)
        sall_ref[mine, :] = src_ref[...]
        sends = [pltpu.make_async_remote_copy(src_ref, sall_ref.at[mine, :], send1.at[k - 1], recv1.at[k - 1],
                                              device_id=_peer(k), device_id_type=MESH) for k in range(1, N_DEV)]
        for cp in sends:
            cp.start()
        for k in range(1, N_DEV):
            theirs = pl.ds(pl.multiple_of((me ^ k) * 8, 8), 8)
            pltpu.make_async_remote_copy(src_ref, sall_ref.at[theirs, :], send1.at[k - 1], recv1.at[k - 1],
                                         device_id=_peer(k), device_id_type=MESH).wait_recv()
        for cp in sends:
            cp.wait_send()
        mp_ref[...] = jnp.dot(sall_ref[...], w_ref[...], preferred_element_type=F32, precision=lax.Precision.HIGHEST)
        mod_ref[mine, :] = mp_ref[mine, :] + b_ref[mine, :]
        sends = []
        for k in range(1, N_DEV):
            theirs = pl.ds(pl.multiple_of((me ^ k) * 8, 8), 8)
            sends.append(pltpu.make_async_remote_copy(mp_ref.at[theirs, :], mod_ref.at[mine, :], send2.at[k - 1], recv2.at[k - 1],
                                                      device_id=_peer(k), device_id_type=MESH))
        for cp in sends:
            cp.start()
        for k in range(1, N_DEV):
            theirs = pl.ds(pl.multiple_of((me ^ k) * 8, 8), 8)
            pltpu.make_async_remote_copy(mp_ref.at[mine, :], mod_ref.at[theirs, :], send2.at[k - 1], recv2.at[k - 1],
                                         device_id=_peer(k), device_id_type=MESH).wait_recv()
            mod_ref[theirs, :] = mod_ref[theirs, :] + b_ref[theirs, :]
        for cp in sends:
            cp.wait_send()

    vm = pl.BlockSpec(memory_space=pltpu.VMEM)
    return pl.pallas_call(
        body, in_specs=[vm, vm, vm], out_specs=[vm, vm],
        out_shape=[jax.ShapeDtypeStruct((8 * N_DEV, D), F32), jax.ShapeDtypeStruct((8 * N_DEV, NW), F32)],
        scratch_shapes=[pltpu.VMEM((8, D), F32), pltpu.VMEM((8 * N_DEV, NW), F32)]
        + [pltpu.SemaphoreType.DMA((N_DEV - 1,))] * 4,
        compiler_params=pltpu.CompilerParams(vmem_limit_bytes=V7X_VMEM_LIMIT), name="ada_exchange",
    )(c_row, w_ada, b_rows)


def _weight_gather(shard):
    def body(x_ref, out_ref, send_sems, recv_sems, local_sem):
        x, y, c = _me()
        me, sibling = (x, y, c), (x, y, 1 - c)
        chips = [(1 - x, y), (x, 1 - y), (1 - x, 1 - y)]

        def slot(px, py, pc):
            return out_ref.at[4 * px + 2 * py + pc]

        def copy(k, block, to, src=None):
            return pltpu.make_async_remote_copy(
                src_ref=slot(*block) if src is None else src, dst_ref=slot(*block),
                send_sem=send_sems.at[k], recv_sem=recv_sems.at[k], device_id=to, device_id_type=MESH)

        mine = pltpu.make_async_copy(x_ref, slot(*me), local_sem)
        mine.start()
        first = [copy(0, me, sibling, src=x_ref)]
        first += [copy(1 + j, me, (*chip, c), src=x_ref) for j, chip in enumerate(chips)]
        for cp in first:
            cp.start()
        passed = [copy(4 + j, (*chip, c), sibling) for j, chip in enumerate(chips)]
        for j, chip in enumerate(chips):
            copy(1 + j, (*chip, c), me).wait_recv()
            passed[j].start()
        copy(0, sibling, me).wait_recv()
        for j, chip in enumerate(chips):
            copy(4 + j, (*chip, 1 - c), me).wait_recv()
        for cp in first + passed:
            cp.wait_send()
        mine.wait()

    any_spec = pl.BlockSpec(memory_space=pl.ANY)
    return pl.pallas_call(
        body, in_specs=[any_spec], out_specs=any_spec,
        out_shape=jax.ShapeDtypeStruct((N_DEV,) + shard.shape, shard.dtype),
        scratch_shapes=[pltpu.SemaphoreType.DMA((7,)), pltpu.SemaphoreType.DMA((7,)), pltpu.SemaphoreType.DMA],
        name="weight_gather",
    )(shard)


def _small_gather(block):
    def body(b_ref, out_ref, send_sems, recv_sems):
        x, y, c = _me()
        me = 4 * x + 2 * y + c
        out_ref[me] = b_ref[...]
        sends = [pltpu.make_async_remote_copy(b_ref, out_ref.at[me], send_sems.at[k - 1], recv_sems.at[k - 1],
                                              device_id=_peer(k), device_id_type=MESH) for k in range(1, N_DEV)]
        for cp in sends:
            cp.start()
        for k in range(1, N_DEV):
            pltpu.make_async_remote_copy(b_ref, out_ref.at[me ^ k], send_sems.at[k - 1], recv_sems.at[k - 1],
                                         device_id=_peer(k), device_id_type=MESH).wait_recv()
        for cp in sends:
            cp.wait_send()

    vm = pl.BlockSpec(memory_space=pltpu.VMEM)
    return pl.pallas_call(
        body, in_specs=[vm], out_specs=vm,
        out_shape=jax.ShapeDtypeStruct((N_DEV,) + block.shape, block.dtype),
        scratch_shapes=[pltpu.SemaphoreType.DMA((N_DEV - 1,)), pltpu.SemaphoreType.DMA((N_DEV - 1,))],
        name="small_gather",
    )(block)


def _pack_small_params(b_ada, n1, n2, fg, qn, kn, sink):
    z = jnp.zeros((SMALL_ROWS, D), F32)
    z = z.at[0:6].set(b_ada.reshape(6, D)).at[6].set(n1.reshape(D)).at[7].set(n2.reshape(D)).at[8].set(fg.reshape(D))
    z = z.at[9, 0:HD].set(qn.reshape(HD)).at[9, HD:2 * HD].set(kn.reshape(HD)).at[10, 0:NH].set(sink.reshape(NH))
    return z


def _unpack_small(p):
    return (p[0:6].reshape(1, 6 * D), p[6].reshape(1, D), p[9, 0:HD].reshape(1, HD), p[9, HD:2 * HD].reshape(1, HD),
            p[10, 0:NH].reshape(1, NH), p[7].reshape(1, D), p[8].reshape(D))


def kernel(x, c, w_ada, b_ada, norm1_g, w_in, q_norm_a, k_norm_a, sink_b, w_branch, w_out, norm2_g, w_mlp_in, w_mlp_out, final_g, loss_target, m_w_ada, m_b_ada, m_norm1_g, m_w_in, m_q_norm_a, m_k_norm_a, m_sink_b, m_w_branch, m_w_out, m_norm2_g, m_w_mlp_in, m_w_mlp_out, m_final_g, v_w_ada, v_b_ada, v_norm1_g, v_w_in, v_q_norm_a, v_k_norm_a, v_sink_b, v_w_branch, v_w_out, v_norm2_g, v_w_mlp_in, v_w_mlp_out, v_final_g):
    S = x.shape[1]
    xs = x.reshape(S, D)
    tgt = loss_target.reshape(S, D)
    ax, ay, ac = lax.axis_index("x"), lax.axis_index("y"), lax.axis_index("c")
    me = 4 * ax + 2 * ay + ac
    me1 = me.reshape(1).astype(jnp.int32)
    NW = w_ada.shape[2]
    NI = w_in.shape[2]

    silu64, mod64 = _ada_exchange(c.reshape(1, D), w_ada.reshape(D, NW),
                                  jnp.repeat(b_ada.reshape(N_DEV, NW), 8, axis=0))
    silu_all = silu64[0::8]
    modv = mod64[0::8].reshape(6, D)

    win = _weight_gather(w_in[0].astype(BF16)).transpose(1, 0, 2).reshape(D, INW)
    rest_shards = tuple(w[0].astype(BF16) for w in (w_branch, w_out, w_mlp_in, w_mlp_out))

    tab_a, tab_b = _rope_tables(S)
    qg2 = jnp.tile(q_norm_a.reshape(1, HD), (1, 2))
    kg2 = jnp.tile(k_norm_a.reshape(1, HD), (1, 2))
    n1g = norm1_g.reshape(1, D)
    n2g = norm2_g.reshape(1, D)
    fg = final_g.reshape(1, D)
    sink = sink_b.reshape(1, NH)

    h, qar, kar, qa, ka, va, qb, kb, vb, ga, gb = _in_proj(xs, modv, n1g, win, qg2, kg2, tab_a, tab_b)
    tk = min(512, S)
    ya_t, lse_at, wb, wout, wmi, wmo = _attn_a_fwd(qa, ka, va.reshape(S // tk, tk, 128).transpose(0, 2, 1), rest_shards)
    ya, lse_a = ya_t.T, lse_at.T
    wb = wb.transpose(1, 2, 0, 3).reshape(2, BW, D)
    wout = wout.reshape(D, D)
    pad = ((WIN, WIN), (0, 0))
    kbp, vbp = jnp.pad(kb, pad), jnp.pad(vb, pad)
    yb, lse_b = _attn_b_fwd(qb, kbp, vbp, sink)
    x1, merged, ua, ub = _merge_out(ya, yb, ga, gb, xs, modv, wb, wout)
    h2, hp, dx2, stf = _mlp_fwd(x1, modv, n2g, wmi, wmo, fg, tgt)

    dhp, dx1, st2 = _mlp_bwd(dx2, x1, hp, modv, n2g, wmi, wmo)
    m2 = _tn_matmul(hp, dx2, 2048, D, "dw_mlp_out", relu_sq=True)
    g_wmo, dg2 = _scale_gate(m2, wmo.reshape(FF, D), modv, 5, "gate2_grad")
    g_wmi = _tn_matmul(h2, dhp, D, 512, "dw_mlp_in", dev_major=True)
    dua, dub, dga, dgb, dya, dyb, dl_a, dl_b = _merge_bwd(dx1, modv, ga, gb, ua, ub, ya, yb, wb, wout)
    m1 = _tn_matmul(merged, dx1, D, D, "dw_out")
    g_wout, dg1 = _scale_gate(m1, wout, modv, 2, "gate1_grad")
    g_wb0 = _tn_matmul(ya, dua, BW, D, "dw_branch_a")
    g_wb1 = _tn_matmul(yb, dub, BW, D, "dw_branch_b")
    g_wb = jnp.stack([g_wb0, g_wb1]).reshape(2, BW, N_DEV, 128).transpose(2, 0, 1, 3).reshape(N_DEV, 2 * BW, 128)
    g_wout = g_wout.reshape(N_DEV, 128, D)
    g_wmo = g_wmo.reshape(N_DEV, 512, D)
    ka_t3 = ka.reshape(S // tk, tk, 128).transpose(0, 2, 1)
    dqa_t, dka_t, dva_t, r_wb, r_wout, r_wmi, r_wmo = _attn_a_bwd(qa, qa.T, ka, ka_t3, va, dya, dya.T, lse_a, dl_a,
                                                                  (g_wb, g_wout, g_wmi, g_wmo))
    dqa = dqa_t.T
    dka = dka_t.transpose(0, 2, 1).reshape(S, 128)
    dva = dva_t.transpose(0, 2, 1).reshape(S, 128)
    dqb, dkbp, dvbp, dsink = _attn_b_bwd(qb, kbp, vbp, sink, dyb, lse_b, dl_b)
    dproj, stqk = _qk_bwd(dqa, dka, dva, dqb, dkbp[WIN:WIN + S], dvbp[WIN:WIN + S], qar, kar, qg2, kg2, tab_a, tab_b, dga, dgb)
    g_win = _tn_matmul(h, dproj, D, 896, "dw_in")
    g_win = g_win.reshape(D, N_DEV, NI).transpose(1, 0, 2)
    grad_x, st1, r_win = _in_bwd(dproj, win, xs, dx1, modv, n1g, g_win.astype(BF16))

    def adam(name, own, recv, w, m, v):
        shape = w.shape
        w2, m2_, v2 = (a.reshape(own.shape[1:]) for a in (w, m, v))
        outs = _adamw_sum([(own, me1)] + [(recv, k) for k in range(N_DEV - 1)], w2, m2_, v2, name)
        return [a.reshape(shape) for a in outs]

    o_win = adam("adamw_w_in", g_win, r_win, w_in, m_w_in, v_w_in)
    o_wb = adam("adamw_w_branch", g_wb, r_wb, w_branch, m_w_branch, v_w_branch)
    o_wout = adam("adamw_w_out", g_wout, r_wout, w_out, m_w_out, v_w_out)
    o_wmi = adam("adamw_w_mlp_in", g_wmi, r_wmi, w_mlp_in, m_w_mlp_in, v_w_mlp_in)
    o_wmo = adam("adamw_w_mlp_out", g_wmo, r_wmo, w_mlp_out, m_w_mlp_out, v_w_mlp_out)

    small = _pack_small(st1, st2, stf, dg1, dg2, stqk, dsink, modv, n1g, n2g)
    small_all = _small_gather(small)
    sw = _pack_small_params(b_ada, norm1_g, norm2_g, final_g, q_norm_a, k_norm_a, sink_b)
    sm = _pack_small_params(m_b_ada, m_norm1_g, m_norm2_g, m_final_g, m_q_norm_a, m_k_norm_a, m_sink_b)
    sv = _pack_small_params(v_b_ada, v_norm1_g, v_norm2_g, v_final_g, v_q_norm_a, v_k_norm_a, v_sink_b)
    sm_out = _adamw_sum([(small_all, k) for k in range(N_DEV)], sw, sm, sv, "adamw_small")
    loss = sm_out[0][11, 0]
    sm_out = [_unpack_small(a) for a in sm_out]

    dmod_all = small_all[:, 0:6, :].reshape(N_DEV, 6 * D)
    dmod_cols = lax.dynamic_slice_in_dim(dmod_all, me * NW, NW, axis=1)
    g_wada = _wada_grad(silu_all, dmod_cols)
    ada = _adamw_sum([(g_wada, None)], w_ada.reshape(D, NW), m_w_ada.reshape(D, NW), v_w_ada.reshape(D, NW), "adamw_ada")
    ada = [a.reshape(1, D, NW) for a in ada]

    def leaves(k):
        b_, n1_, qn_, kn_, sk_, n2_, fg_ = sm_out[k]
        return [ada[k], b_, n1_, o_win[k], qn_, kn_, sk_, o_wb[k], o_wout[k], n2_, o_wmi[k], o_wmo[k], fg_]

    return (loss, grad_x.reshape(1, S, D), *leaves(0), *leaves(1), *leaves(2), *leaves(3))
```

```python
import jax
import jax.numpy as jnp
from jax import lax
from jax.experimental import pallas as pl
from jax.experimental.pallas import tpu as pltpu

F32, BF16 = jnp.float32, jnp.bfloat16
MESH = pl.DeviceIdType.MESH

D = 1024
HD = 64
NH = 8
GRP = 4
BW = 512
FF = 4096
INW = 3584
GRID_W = 64
WIN = 128
THETA = 10000.0
EPS = 1e-6
NEG = -1e30
N_DEV = 8
LOG2E = 1.4426950408889634
LN2 = 0.6931471805599453
QA_SCALE = 0.125 * LOG2E
SMALL_ROWS = 16
V7X_VMEM_LIMIT = 56 * 1024 * 1024

ADAM_LR, ADAM_B1, ADAM_B2, ADAM_EPS, ADAM_WD, ADAM_STEP = 0.001, 0.9, 0.999, 1e-08, 0.01, 10

NT = (((1,), (1,)), ((), ()))
TN = (((0,), (0,)), ((), ()))


def _params(n_axes, vmem=V7X_VMEM_LIMIT):
    return pltpu.CompilerParams(dimension_semantics=("arbitrary",) * n_axes, vmem_limit_bytes=vmem)


def _const(shape):
    return pl.BlockSpec(shape, lambda *_: (0,) * len(shape))


def _rows(tm, width):
    return pl.BlockSpec((tm, width), lambda i, *_: (i, 0))


def _seg_matrix(n, seg):
    r = lax.broadcasted_iota(jnp.int32, (n, n), 0) // seg
    c = lax.broadcasted_iota(jnp.int32, (n, n), 1) // seg
    return (r == c).astype(F32)


def _seg_sum(z, seg_mat):
    return jnp.dot(z, seg_mat, preferred_element_type=F32, precision=lax.Precision.HIGHEST)


def _rope(z, t_ref, sh):
    return z * t_ref[0] + pltpu.roll(z, sh, 1) * t_ref[1] + pltpu.roll(z, 128 - sh, 1) * t_ref[2]


def _rope_t(dz, t_ref, sh):
    return dz * t_ref[0] + pltpu.roll(dz * t_ref[1], 128 - sh, 1) + pltpu.roll(dz * t_ref[2], sh, 1)


def _rope_tables(S):
    t = jnp.arange(S, dtype=jnp.int32)
    lane = jnp.arange(HD)

    def build(cos, sin, first):
        t0 = cos
        t1 = jnp.where(first[None, :], 0.0, sin)
        t2 = jnp.where(first[None, :], -sin, 0.0)
        return jnp.stack([jnp.tile(a, (1, 2)) for a in (t0, t1, t2)]).astype(F32)

    inv_a = THETA ** (-jnp.arange(0, HD // 2, 2, dtype=F32) / (HD // 2))
    ar = (t // GRID_W).astype(F32)[:, None] * inv_a[None, :]
    ac = (t % GRID_W).astype(F32)[:, None] * inv_a[None, :]
    cos_a = jnp.concatenate([jnp.cos(ar), jnp.cos(ar), jnp.cos(ac), jnp.cos(ac)], axis=1)
    sin_a = jnp.concatenate([jnp.sin(ar), jnp.sin(ar), jnp.sin(ac), jnp.sin(ac)], axis=1)
    tab_a = build(cos_a, sin_a, (lane % 32) < 16)
    inv_b = THETA ** (-jnp.arange(0, HD, 2, dtype=F32) / HD)
    ab = t.astype(F32)[:, None] * inv_b[None, :]
    cos_b = jnp.concatenate([jnp.cos(ab), jnp.cos(ab)], axis=1)
    sin_b = jnp.concatenate([jnp.sin(ab), jnp.sin(ab)], axis=1)
    tab_b = build(cos_b, sin_b, lane < 32)
    return tab_a, tab_b


def _in_proj(x, modv, n1g, win, qg2, kg2, tab_a, tab_b):
    S = x.shape[0]
    tm = min(256, S)

    def body(x_ref, mod_ref, g_ref, w_ref, qg_ref, kg_ref, ta_ref, tb_ref,
             h_ref, qar_ref, kar_ref, qa_ref, ka_ref, va_ref, qb_ref, kb_ref, vb_ref, ga_ref, gb_ref):
        xt = x_ref[...]
        r = lax.rsqrt(jnp.mean(xt * xt, axis=-1, keepdims=True) + EPS)
        h = ((xt * r) * g_ref[...]) * (1.0 + mod_ref[1:2, :]) + mod_ref[0:1, :]
        hb = h.astype(BF16)
        h_ref[...] = hb
        proj = jnp.dot(hb, w_ref[...], preferred_element_type=F32)
        seg = _seg_matrix(128, HD)

        def head_norm(z, g):
            ms = _seg_sum(z * z, seg) * (1.0 / HD)
            return (z * lax.rsqrt(ms + EPS)) * g

        for p in range(4):
            z = proj[:, 128 * p:128 * p + 128]
            qar_ref[:, 128 * p:128 * p + 128] = z.astype(BF16)
            qa_ref[:, 128 * p:128 * p + 128] = (_rope(head_norm(z, qg_ref[...]), ta_ref, 16) * QA_SCALE).astype(BF16)
            zb = proj[:, 768 + 128 * p:768 + 128 * p + 128]
            qb_ref[:, 128 * p:128 * p + 128] = (_rope(zb, tb_ref, 32) * QA_SCALE).astype(BF16)
        z = proj[:, 512:640]
        kar_ref[...] = z.astype(BF16)
        ka_ref[...] = _rope(head_norm(z, kg_ref[...]), ta_ref, 16).astype(BF16)
        va_ref[...] = proj[:, 640:768].astype(BF16)
        kb_ref[...] = _rope(proj[:, 1280:1408], tb_ref, 32).astype(BF16)
        vb_ref[...] = proj[:, 1408:1536].astype(BF16)
        ga_ref[...] = proj[:, 1536:2560].astype(BF16)
        gb_ref[...] = proj[:, 2560:3584].astype(BF16)

    tab = pl.BlockSpec((3, tm, 128), lambda i: (0, i, 0))
    shapes = [(D, BF16), (BW, BF16), (128, BF16), (BW, BF16), (128, BF16), (128, BF16),
              (BW, BF16), (128, BF16), (128, BF16), (D, BF16), (D, BF16)]
    return pl.pallas_call(
        body, grid=(S // tm,),
        in_specs=[_rows(tm, D), _const((6, D)), _const((1, D)), _const((D, INW)), _const((1, 128)), _const((1, 128)), tab, tab],
        out_specs=[_rows(tm, w) for w, _ in shapes],
        out_shape=[jax.ShapeDtypeStruct((S, w), dt) for w, dt in shapes],
        compiler_params=_params(1), name="in_proj",
    )(x, modv, n1g, win, qg2, kg2, tab_a, tab_b)


def _exchange_gather(block_refs, out_refs, send_sems, recv_sems, local_sems):
    x, y, c = _me()
    me = 4 * x + 2 * y + c

    def copies():
        own, out, arrive = [], [], []
        for a, (blk, dst) in enumerate(zip(block_refs, out_refs)):
            own.append(pltpu.make_async_copy(blk, dst.at[me], local_sems.at[a]))
            for k in range(1, N_DEV):
                sems = dict(send_sem=send_sems.at[a, k - 1], recv_sem=recv_sems.at[a, k - 1], device_id=_peer(k), device_id_type=MESH)
                out.append(pltpu.make_async_remote_copy(blk, dst.at[me], **sems))
                arrive.append(pltpu.make_async_remote_copy(blk, dst.at[me ^ k], **sems))
        return own, out, arrive

    def start():
        own, out, _ = copies()
        for cp in own + out:
            cp.start()

    def finish():
        own, out, arrive = copies()
        for cp in arrive:
            cp.wait_recv()
        for cp in out:
            cp.wait_send()
        for cp in own:
            cp.wait()

    return start, finish


def _exchange_scatter(chunk_refs, recv_refs, send_sems, recv_sems):
    x, y, c = _me()
    me = 4 * x + 2 * y + c

    def copies():
        return [pltpu.make_async_remote_copy(src.at[me ^ k], dst.at[k - 1], send_sems.at[a, k - 1], recv_sems.at[a, k - 1],
                                             device_id=_peer(k), device_id_type=MESH)
                for a, (src, dst) in enumerate(zip(chunk_refs, recv_refs)) for k in range(1, N_DEV)]

    def start():
        for cp in copies():
            cp.start()

    def finish():
        cps = copies()
        for cp in cps:
            cp.wait_recv()
        for cp in cps:
            cp.wait_send()

    return start, finish


def _exchange_sems(n):
    return [pltpu.SemaphoreType.DMA((n, N_DEV - 1)), pltpu.SemaphoreType.DMA((n, N_DEV - 1))]


def _attn_a_fwd(q, k, vt3, shards):
    S = q.shape[0]
    tq = min(512, S)
    nq = S // tq
    nk, _, tk = vt3.shape
    ONES = 16
    AHEAD = 3
    ns = len(shards)

    def body(q_ref, k_ref, vt_ref, *rest):
        w_hbm, (o_ref, lse_ref), wall_hbm = rest[:ns], rest[ns:ns + 2], rest[ns + 2:2 * ns + 2]
        st_sc, send_sems, recv_sems, local_sems = rest[2 * ns + 2:]
        start, finish = _exchange_gather(w_hbm, wall_hbm, send_sems, recv_sems, local_sems)
        pl.when(pl.program_id(0) == 0)(start)
        row8 = lax.broadcasted_iota(jnp.int32, (NH, tq), 0)
        lse_all = jnp.zeros((NH, tq), F32)
        ones = jnp.ones((ONES, tk), BF16)
        for kv in range(2):
            qs = []
            for pp in range(2):
                qp = q_ref[:, 128 * (2 * kv + pp):128 * (2 * kv + pp) + 128]
                qs += [qp[:, :HD], qp[:, HD:]]

            def keys(j, kv=kv):
                return k_ref[pl.ds(pl.multiple_of(j * tk, tk), tk), :][:, HD * kv:HD * kv + HD]

            def scores(kj, t, qs=qs):
                return lax.dot_general(kj, qs[t], NT, preferred_element_type=F32)

            def step(j, carry, kv=kv):
                kj = keys(j)
                kn = keys(jnp.minimum(j + 1, nk - 1))
                v1 = jnp.concatenate([vt_ref[j, HD * kv:HD * kv + HD, :], ones], axis=0)
                sts = [st_sc[t] for t in range(AHEAD)]
                new = []
                for t in range(GRP):
                    m, acc = carry[2 * t], carry[2 * t + 1]
                    if t + AHEAD < GRP:
                        sts.append(scores(kj, t + AHEAD))
                    st = sts[t]
                    mn = jnp.maximum(m, jnp.max(st, axis=0, keepdims=True))
                    pt = jnp.exp2(st - mn)
                    if t + AHEAD >= GRP:
                        st_sc[t + AHEAD - GRP] = scores(kn, t + AHEAD - GRP)
                    acc = jnp.exp2(m - mn) * acc + jnp.dot(v1, pt.astype(BF16), preferred_element_type=F32)
                    new += [mn, acc]
                return tuple(new)

            k0 = keys(0)
            for t in range(AHEAD):
                st_sc[t] = scores(k0, t)
            init = (jnp.full((1, tq), NEG, F32), jnp.zeros((HD + ONES, tq), F32)) * GRP
            res = lax.fori_loop(0, nk, step, init)
            for t in range(GRP):
                h = GRP * kv + t
                m, acc = res[2 * t], res[2 * t + 1]
                l = acc[HD:HD + 1, :]
                o_ref[HD * h:HD * h + HD, :] = (acc[:HD, :] / l).astype(BF16)
                lse_all = jnp.where(row8 == h, m + jnp.log2(l), lse_all)
        lse_ref[...] = lse_all
        pl.when(pl.program_id(0) == nq - 1)(finish)

    any_spec = pl.BlockSpec(memory_space=pl.ANY)
    return pl.pallas_call(
        body, grid=(nq,),
        in_specs=[_rows(tq, BW), _const((S, 128)), _const((nk, 128, tk))] + [any_spec] * ns,
        out_specs=[pl.BlockSpec((BW, tq), lambda i: (0, i)), pl.BlockSpec((NH, tq), lambda i: (0, i))] + [any_spec] * ns,
        out_shape=[jax.ShapeDtypeStruct((BW, S), BF16), jax.ShapeDtypeStruct((NH, S), F32)]
        + [jax.ShapeDtypeStruct((N_DEV,) + s.shape, s.dtype) for s in shards],
        scratch_shapes=[pltpu.VMEM((AHEAD, tk, tq), F32)] + _exchange_sems(ns) + [pltpu.SemaphoreType.DMA((ns,))],
        compiler_params=_params(1), name="attn_a_fwd",
    )(q, k, vt3, *shards)


def _window_mask(i, tq, S):
    W = tq + 2 * WIN
    r = lax.broadcasted_iota(jnp.int32, (tq, W), 0)
    c = lax.broadcasted_iota(jnp.int32, (tq, W), 1)
    kpos = i * tq - WIN + c
    return (jnp.abs(c - WIN - r) <= WIN) & (kpos >= 0) & (kpos < S)


TQ_B = 256


def _attn_b_fwd(q, kp, vp, sink2):
    S = q.shape[0]
    tq = min(TQ_B, S)
    W = tq + 2 * WIN

    def body(q_ref, k_ref, v_ref, sink_ref, o_ref, lse_ref):
        i = pl.program_id(0)
        off = pl.multiple_of(i * tq, tq)
        valid = _window_mask(i, tq, S)
        kw = k_ref[pl.ds(off, W), :]
        vw = v_ref[pl.ds(off, W), :]
        ones = jnp.ones((W, HD), BF16)
        lane8 = lax.broadcasted_iota(jnp.int32, (tq, NH), 1)
        lse_all = jnp.zeros((tq, NH), F32)
        qs = []
        for p in range(4):
            qp = q_ref[:, 128 * p:128 * p + 128]
            qs += [qp[:, :HD], qp[:, HD:]]
        khs = [kw[:, HD * kv:HD * kv + HD] for kv in range(2)]
        v1s = [jnp.concatenate([vw[:, HD * kv:HD * kv + HD], ones], axis=1) for kv in range(2)]

        def scores(h):
            return lax.dot_general(qs[h], khs[h // GRP], NT, preferred_element_type=F32)

        ss = [scores(0), scores(1)]
        outs = []
        for h in range(NH):
            if h + 2 < NH:
                ss.append(scores(h + 2))
            s = jnp.where(valid, ss[h], NEG)
            sk = sink_ref[:, h:h + 1]
            m = jnp.maximum(jnp.max(s, axis=1, keepdims=True), sk)
            acc = jnp.dot(jnp.exp2(s - m).astype(BF16), v1s[h // GRP], preferred_element_type=F32)
            l = acc[:, HD:HD + 1] + jnp.exp2(sk - m)
            outs.append(acc[:, :HD] / l)
            lse_all = jnp.where(lane8 == h, m + jnp.log2(l), lse_all)
        for p in range(4):
            o_ref[:, 128 * p:128 * p + 128] = jnp.concatenate(outs[2 * p:2 * p + 2], axis=1).astype(BF16)
        lse_ref[...] = lse_all

    return pl.pallas_call(
        body, grid=(S // tq,),
        in_specs=[_rows(tq, BW), _const((S + 2 * WIN, 128)), _const((S + 2 * WIN, 128)), _const((1, NH))],
        out_specs=[_rows(tq, BW), _rows(tq, NH)],
        out_shape=[jax.ShapeDtypeStruct((S, BW), BF16), jax.ShapeDtypeStruct((S, NH), F32)],
        compiler_params=_params(1), name="attn_b_fwd",
    )(q, kp, vp, sink2)


def _merge_out(ya, yb, ga, gb, x, modv, wb, wout):
    S = x.shape[0]
    tm = min(256, S)

    def body(ya_ref, yb_ref, ga_ref, gb_ref, x_ref, mod_ref, wb_ref, wo_ref, x1_ref, mg_ref, ua_ref, ub_ref):
        ua = jnp.dot(ya_ref[...], wb_ref[0], preferred_element_type=F32)
        ub = jnp.dot(yb_ref[...], wb_ref[1], preferred_element_type=F32)
        merged = jax.nn.sigmoid(ga_ref[...].astype(F32)) * ua + jax.nn.sigmoid(gb_ref[...].astype(F32)) * ub
        mb = merged.astype(BF16)
        ua_ref[...] = ua.astype(BF16)
        ub_ref[...] = ub.astype(BF16)
        mg_ref[...] = mb
        x1_ref[...] = x_ref[...] + mod_ref[2:3, :] * jnp.dot(mb, wo_ref[...], preferred_element_type=F32)

    return pl.pallas_call(
        body, grid=(S // tm,),
        in_specs=[_rows(tm, BW), _rows(tm, BW), _rows(tm, D), _rows(tm, D), _rows(tm, D), _const((6, D)),
                  _const((2, BW, D)), _const((D, D))],
        out_specs=[_rows(tm, D)] * 4,
        out_shape=[jax.ShapeDtypeStruct((S, D), F32)] + [jax.ShapeDtypeStruct((S, D), BF16)] * 3,
        compiler_params=_params(1), name="merge_out",
    )(ya, yb, ga, gb, x, modv, wb, wout)


def _mlp_fwd(x1, modv, n2g, wmi, wmo, fg, target):
    S = x1.shape[0]
    tm = min(512, S)
    tf = wmi.shape[2]
    nj = wmi.shape[0]

    def body(x1_ref, mod_ref, g_ref, wi_ref, wo_ref, fg_ref, t_ref, h2_ref, hp_ref, dx2_ref, st_ref, acc_ref):
        i, j = pl.program_id(0), pl.program_id(1)

        @pl.when(j == 0)
        def _():
            xt = x1_ref[...]
            r = lax.rsqrt(jnp.mean(xt * xt, axis=-1, keepdims=True) + EPS)
            h2 = ((xt * r) * g_ref[...]) * (1.0 + mod_ref[4:5, :]) + mod_ref[3:4, :]
            h2_ref[...] = h2.astype(BF16)
            acc_ref[...] = jnp.zeros_like(acc_ref)

        @pl.when((i == 0) & (j == 0))
        def _():
            st_ref[...] = jnp.zeros_like(st_ref)

        hp = jnp.dot(h2_ref[...], wi_ref[...], preferred_element_type=F32)
        hp_ref[...] = hp.astype(BF16)
        hid = jnp.square(jnp.maximum(hp, 0.0))
        acc_ref[...] += jnp.dot(hid.astype(BF16), wo_ref[...], preferred_element_type=F32)

        @pl.when(j == nj - 1)
        def _():
            x2 = x1_ref[...] + mod_ref[5:6, :] * acc_ref[...]
            r3 = lax.rsqrt(jnp.mean(x2 * x2, axis=-1, keepdims=True) + EPS)
            xn = x2 * r3
            err = xn * fg_ref[...] - t_ref[...]
            dy = err * (1.0 / D)
            gy = dy * fg_ref[...]
            dx2_ref[...] = r3 * (gy - xn * jnp.mean(gy * xn, axis=-1, keepdims=True))
            st_ref[0:1, :] += jnp.sum(dy * xn, axis=0, keepdims=True)
            st_ref[1:2, :] += jnp.sum(err * err, axis=0, keepdims=True) * (0.5 / D)

    return pl.pallas_call(
        body, grid=(S // tm, nj),
        in_specs=[pl.BlockSpec((tm, D), lambda i, j: (i, 0)), _const((6, D)), _const((1, D)),
                  pl.BlockSpec((None, D, tf), lambda i, j: (j, 0, 0)), pl.BlockSpec((None, tf, D), lambda i, j: (j, 0, 0)),
                  _const((1, D)), pl.BlockSpec((tm, D), lambda i, j: (i, 0))],
        out_specs=[pl.BlockSpec((tm, D), lambda i, j: (i, 0)), pl.BlockSpec((tm, tf), lambda i, j: (i, j)),
                   pl.BlockSpec((tm, D), lambda i, j: (i, 0)), _const((8, D))],
        out_shape=[jax.ShapeDtypeStruct((S, D), BF16), jax.ShapeDtypeStruct((S, nj * tf), BF16),
                   jax.ShapeDtypeStruct((S, D), F32), jax.ShapeDtypeStruct((8, D), F32)],
        scratch_shapes=[pltpu.VMEM((tm, D), F32)],
        compiler_params=_params(2), name="mlp_fwd",
    )(x1, modv, n2g, wmi, wmo, fg, target)


def _mlp_bwd(dx2, x1, hp, modv, n2g, wmi, wmo):
    S = x1.shape[0]
    tm = min(512, S)
    tf = wmi.shape[2]
    nj = wmi.shape[0]

    def body(dx2_ref, x1_ref, hp_ref, mod_ref, g_ref, wi_ref, wo_ref, dhp_ref, dx1_ref, st_ref, dmo_ref, acc_ref):
        i, j = pl.program_id(0), pl.program_id(1)

        @pl.when(j == 0)
        def _():
            dmo_ref[...] = (mod_ref[5:6, :] * dx2_ref[...]).astype(BF16)
            acc_ref[...] = jnp.zeros_like(acc_ref)

        @pl.when((i == 0) & (j == 0))
        def _():
            st_ref[...] = jnp.zeros_like(st_ref)

        dhid = lax.dot_general(dmo_ref[...], wo_ref[...], NT, preferred_element_type=F32)
        dhp = (dhid * (2.0 * jnp.maximum(hp_ref[...].astype(F32), 0.0))).astype(BF16)
        dhp_ref[...] = dhp
        acc_ref[...] += lax.dot_general(dhp, wi_ref[...], NT, preferred_element_type=F32)

        @pl.when(j == nj - 1)
        def _():
            dh2 = acc_ref[...]
            xt = x1_ref[...]
            r = lax.rsqrt(jnp.mean(xt * xt, axis=-1, keepdims=True) + EPS)
            xn = xt * r
            st_ref[0:1, :] += jnp.sum(dh2, axis=0, keepdims=True)
            st_ref[1:2, :] += jnp.sum(dh2 * xn, axis=0, keepdims=True)
            dxn = dh2 * (g_ref[...] * (1.0 + mod_ref[4:5, :]))
            dx1_ref[...] = dx2_ref[...] + r * (dxn - xn * jnp.mean(dxn * xn, axis=-1, keepdims=True))

    return pl.pallas_call(
        body, grid=(S // tm, nj),
        in_specs=[pl.BlockSpec((tm, D), lambda i, j: (i, 0)), pl.BlockSpec((tm, D), lambda i, j: (i, 0)),
                  pl.BlockSpec((tm, tf), lambda i, j: (i, j)), _const((6, D)), _const((1, D)),
                  pl.BlockSpec((None, D, tf), lambda i, j: (j, 0, 0)), pl.BlockSpec((None, tf, D), lambda i, j: (j, 0, 0))],
        out_specs=[pl.BlockSpec((tm, tf), lambda i, j: (i, j)), pl.BlockSpec((tm, D), lambda i, j: (i, 0)), _const((8, D))],
        out_shape=[jax.ShapeDtypeStruct((S, nj * tf), BF16), jax.ShapeDtypeStruct((S, D), F32), jax.ShapeDtypeStruct((8, D), F32)],
        scratch_shapes=[pltpu.VMEM((tm, D), BF16), pltpu.VMEM((tm, D), F32)],
        compiler_params=_params(2), name="mlp_bwd",
    )(dx2, x1, hp, modv, n2g, wmi, wmo)


def _tn_matmul(a, b, tk, tn, name, relu_sq=False, dev_major=False):
    S, K = a.shape
    N = b.shape[1]
    ts = min(1024, S)
    ns = S // ts

    def body(a_ref, b_ref, o_ref):
        @pl.when(pl.program_id(2) == 0)
        def _():
            o_ref[...] = jnp.zeros_like(o_ref)

        at = a_ref[...]
        if relu_sq:
            at = jnp.square(jnp.maximum(at.astype(F32), 0.0)).astype(BF16)
        o_ref[...] += lax.dot_general(at, b_ref[...].astype(BF16), TN, preferred_element_type=F32)

    if dev_major:
        out_spec = pl.BlockSpec((None, tk, tn), lambda k, n, s: (n, k, 0))
        out_shape = jax.ShapeDtypeStruct((N // tn, K, tn), F32)
    else:
        out_spec = pl.BlockSpec((tk, tn), lambda k, n, s: (k, n))
        out_shape = jax.ShapeDtypeStruct((K, N), F32)
    return pl.pallas_call(
        body, grid=(K // tk, N // tn, ns),
        in_specs=[pl.BlockSpec((ts, tk), lambda k, n, s: (s, k)), pl.BlockSpec((ts, tn), lambda k, n, s: (s, n))],
        out_specs=out_spec, out_shape=out_shape,
        compiler_params=_params(3), name=name,
    )(a, b)


def _scale_gate(m, w, g, row, name):
    K = m.shape[0]
    tk = min(512, K)

    def body(m_ref, w_ref, mod_ref, dw_ref, dg_ref):
        @pl.when(pl.program_id(0) == 0)
        def _():
            dg_ref[...] = jnp.zeros_like(dg_ref)

        mt = m_ref[...]
        dw_ref[...] = mt * mod_ref[row:row + 1, :]
        dg_ref[0:1, :] += jnp.sum(mt * w_ref[...].astype(F32), axis=0, keepdims=True)

    return pl.pallas_call(
        body, grid=(K // tk,),
        in_specs=[_rows(tk, D), _rows(tk, D), _const((6, D))],
        out_specs=[_rows(tk, D), _const((8, D))],
        out_shape=[jax.ShapeDtypeStruct((K, D), F32), jax.ShapeDtypeStruct((8, D), F32)],
        compiler_params=_params(1), name=name,
    )(m, w, g)


def _merge_bwd(dx1, modv, ga, gb, ua, ub, ya, yb, wb, wout):
    S = dx1.shape[0]
    tm = min(256, S)

    def body(dx1_ref, mod_ref, ga_ref, gb_ref, ua_ref, ub_ref, ya_ref, yb_ref, wb_ref, wo_ref,
             dua_ref, dub_ref, dga_ref, dgb_ref, dya_ref, dyb_ref, dla_ref, dlb_ref):
        dao = (mod_ref[2:3, :] * dx1_ref[...]).astype(BF16)
        dm = lax.dot_general(dao, wo_ref[...], NT, preferred_element_type=F32)
        r = lax.broadcasted_iota(jnp.int32, (BW, NH), 0) // HD
        c = lax.broadcasted_iota(jnp.int32, (BW, NH), 1)
        head_of = (r == c).astype(F32)
        for br, (g_ref, u_ref, y_ref, du_ref, dg_ref, dy_ref, dl_ref) in enumerate((
                (ga_ref, ua_ref, ya_ref, dua_ref, dga_ref, dya_ref, dla_ref),
                (gb_ref, ub_ref, yb_ref, dub_ref, dgb_ref, dyb_ref, dlb_ref))):
            sg = jax.nn.sigmoid(g_ref[...].astype(F32))
            du = (dm * sg).astype(BF16)
            du_ref[...] = du
            dg_ref[...] = (dm * u_ref[...].astype(F32) * sg * (1.0 - sg)).astype(BF16)
            dy = lax.dot_general(du, wb_ref[br], NT, preferred_element_type=F32)
            dyb16 = dy.astype(BF16)
            dy_ref[...] = dyb16
            dl_ref[...] = jnp.dot(dyb16.astype(F32) * y_ref[...].astype(F32), head_of,
                                  preferred_element_type=F32, precision=lax.Precision.HIGHEST)

    return pl.pallas_call(
        body, grid=(S // tm,),
        in_specs=[_rows(tm, D), _const((6, D)), _rows(tm, D), _rows(tm, D), _rows(tm, D), _rows(tm, D),
                  _rows(tm, BW), _rows(tm, BW), _const((2, BW, D)), _const((D, D))],
        out_specs=[_rows(tm, D)] * 4 + [_rows(tm, BW)] * 2 + [_rows(tm, NH)] * 2,
        out_shape=[jax.ShapeDtypeStruct((S, D), BF16)] * 4 + [jax.ShapeDtypeStruct((S, BW), BF16)] * 2
        + [jax.ShapeDtypeStruct((S, NH), F32)] * 2,
        compiler_params=_params(1), name="merge_bwd",
    )(dx1, modv, ga, gb, ua, ub, ya, yb, wb, wout)


def _attn_a_bwd(q, qt, k, kt3, v, do, dot_, lse, delta, chunks):
    S = q.shape[0]
    tq = min(512, S)
    tk = min(512, S)
    nq, nk = S // tq, S // tk

    nc = len(chunks)

    def body(q_ref, qt_ref, do_ref, dot_ref, lse_ref, dl_ref, k_ref, v_ref, kt_ref, *rest):
        g_hbm, (dq_ref, dk_hbm, dv_hbm), recv_hbm = rest[:nc], rest[nc:nc + 3], rest[nc + 3:2 * nc + 3]
        dk_sc, dv_sc, sem, send_sems, recv_sems = rest[2 * nc + 3:]
        i = pl.program_id(0)
        start, finish = _exchange_scatter(g_hbm, recv_hbm, send_sems, recv_sems)

        @pl.when(i == 0)
        def _():
            start()
            dk_sc[...] = jnp.zeros_like(dk_sc)
            dv_sc[...] = jnp.zeros_like(dv_sc)

        for kv in range(2):
            qg = q_ref[:, 256 * kv:256 * kv + 256]
            dog = do_ref[:, 256 * kv:256 * kv + 256]
            heads = []
            for t in range(GRP):
                h = GRP * kv + t
                heads.append((qg[:, HD * t:HD * t + HD], dog[:, HD * t:HD * t + HD],
                              qt_ref[HD * h:HD * h + HD, :], dot_ref[HD * h:HD * h + HD, :],
                              lse_ref[:, h:h + 1], dl_ref[:, h:h + 1]))

            def step(j, carry, kv=kv, heads=heads):
                off = pl.multiple_of(j * tk, tk)
                kj = k_ref[pl.ds(off, tk), :][:, HD * kv:HD * kv + HD]
                vj = v_ref[pl.ds(off, tk), :][:, HD * kv:HD * kv + HD]
                kjt = kt_ref[j, HD * kv:HD * kv + HD, :]
                dkt = jnp.zeros((HD, tk), F32)
                dvt = jnp.zeros((HD, tk), F32)
                new = []

                def logits(t):
                    return (lax.dot_general(heads[t][0], kj, NT, preferred_element_type=F32),
                            lax.dot_general(heads[t][1], vj, NT, preferred_element_type=F32))

                sd = [logits(0)]
                for t, (qh, doh, qth, doth, lse_h, dl_h) in enumerate(heads):
                    if t + 1 < GRP:
                        sd.append(logits(t + 1))
                    s, dp = sd[t]
                    pm = jnp.exp2(s - lse_h)
                    ds = (pm * (dp - dl_h)).astype(BF16)
                    dvt = dvt + jnp.dot(doth, pm.astype(BF16), preferred_element_type=F32)
                    dkt = dkt + jnp.dot(qth, ds, preferred_element_type=F32)
                    new.append(carry[t] + lax.dot_general(kjt, ds, NT, preferred_element_type=F32))
                dk_sc[j, HD * kv:HD * kv + HD, :] += dkt
                dv_sc[j, HD * kv:HD * kv + HD, :] += dvt
                return tuple(new)

            res = lax.fori_loop(0, nk, step, (jnp.zeros((HD, tq), F32),) * GRP)
            for t in range(GRP):
                dq_ref[HD * (GRP * kv + t):HD * (GRP * kv + t) + HD, :] = res[t]

        @pl.when(i == nq - 1)
        def _():
            c1 = pltpu.make_async_copy(dk_sc, dk_hbm, sem.at[0])
            c2 = pltpu.make_async_copy(dv_sc, dv_hbm, sem.at[1])
            c1.start()
            c2.start()
            c1.wait()
            c2.wait()
            finish()

    any_spec = pl.BlockSpec(memory_space=pl.ANY)
    cols = pl.BlockSpec((BW, tq), lambda i: (0, i))
    return pl.pallas_call(
        body, grid=(nq,),
        in_specs=[_rows(tq, BW), cols, _rows(tq, BW), cols, _rows(tq, NH), _rows(tq, NH), _const((S, 128)), _const((S, 128)),
                  _const((nk, 128, tk))] + [any_spec] * nc,
        out_specs=[cols, any_spec, any_spec] + [any_spec] * nc,
        out_shape=[jax.ShapeDtypeStruct((BW, S), F32), jax.ShapeDtypeStruct((nk, 128, tk), F32),
                   jax.ShapeDtypeStruct((nk, 128, tk), F32)]
        + [jax.ShapeDtypeStruct((N_DEV - 1,) + c.shape[1:], c.dtype) for c in chunks],
        scratch_shapes=[pltpu.VMEM((nk, 128, tk), F32), pltpu.VMEM((nk, 128, tk), F32), pltpu.SemaphoreType.DMA((2,))]
        + _exchange_sems(nc),
        compiler_params=_params(1), name="attn_a_bwd",
    )(q, qt, do, dot_, lse, delta, k, v, kt3, *chunks)


def _attn_b_bwd(q, kp, vp, sink2, do, lse, delta):
    S = q.shape[0]
    tq = min(TQ_B, S)
    W = tq + 2 * WIN
    nq = S // tq
    nc = (S + 2 * WIN) // tq

    def body(q_ref, k_ref, v_ref, sink_ref, do_ref, lse_ref, dl_ref, dq_ref, dk_hbm, dv_hbm, ds_ref, dk_sc, dv_sc, sem):
        i = pl.program_id(0)

        @pl.when(i == 0)
        def _():
            dk_sc[...] = jnp.zeros_like(dk_sc)
            dv_sc[...] = jnp.zeros_like(dv_sc)
            ds_ref[...] = jnp.zeros_like(ds_ref)

        off = pl.multiple_of(i * tq, tq)
        valid = _window_mask(i, tq, S)
        kw = k_ref[pl.ds(off, W), :]
        vw = v_ref[pl.ds(off, W), :]
        lse_i = lse_ref[...]
        dl_i = dl_ref[...]
        qa = q_ref[...]
        doa = do_ref[...]
        qt = qa.astype(F32).T.astype(BF16)
        dot_ = doa.astype(F32).T.astype(BF16)
        khs = [kw[:, HD * kv:HD * kv + HD] for kv in range(2)]
        vhs = [vw[:, HD * kv:HD * kv + HD] for kv in range(2)]

        def logits(h):
            return (lax.dot_general(qa[:, HD * h:HD * h + HD], khs[h // GRP], NT, preferred_element_type=F32),
                    lax.dot_general(doa[:, HD * h:HD * h + HD], vhs[h // GRP], NT, preferred_element_type=F32))

        sd = [logits(0)]
        dqs = []
        dkt = [jnp.zeros((HD, W), F32), jnp.zeros((HD, W), F32)]
        dvt = [jnp.zeros((HD, W), F32), jnp.zeros((HD, W), F32)]
        for h in range(NH):
            kv = h // GRP
            if h + 1 < NH:
                sd.append(logits(h + 1))
            s, dp = sd[h]
            pm = jnp.exp2(jnp.where(valid, s, NEG) - lse_i[:, h:h + 1])
            ds = (pm * (dp - dl_i[:, h:h + 1])).astype(BF16)
            dvt[kv] = dvt[kv] + jnp.dot(dot_[HD * h:HD * h + HD, :], pm.astype(BF16), preferred_element_type=F32)
            dkt[kv] = dkt[kv] + jnp.dot(qt[HD * h:HD * h + HD, :], ds, preferred_element_type=F32)
            dqs.append(jnp.dot(ds, khs[kv], preferred_element_type=F32))
        for p in range(4):
            dq_ref[:, 128 * p:128 * p + 128] = jnp.concatenate(dqs[2 * p:2 * p + 2], axis=1)
        for half in range(W // tq):
            dk_sc[i + half] += jnp.concatenate([d[:, tq * half:tq * half + tq] for d in dkt], axis=0)
            dv_sc[i + half] += jnp.concatenate([d[:, tq * half:tq * half + tq] for d in dvt], axis=0)
        psd = jnp.exp2(sink_ref[...] - lse_i) * dl_i
        r = lax.broadcasted_iota(jnp.int32, (NH, 128), 0)
        c = lax.broadcasted_iota(jnp.int32, (NH, 128), 1)
        row = jnp.dot(jnp.sum(psd, axis=0, keepdims=True), (r == c).astype(F32),
                      preferred_element_type=F32, precision=lax.Precision.HIGHEST)
        ds_ref[...] -= jnp.broadcast_to(row, (8, 128))

        @pl.when(i == nq - 1)
        def _():
            c1 = pltpu.make_async_copy(dk_sc, dk_hbm, sem.at[0])
            c2 = pltpu.make_async_copy(dv_sc, dv_hbm, sem.at[1])
            c1.start()
            c2.start()
            c1.wait()
            c2.wait()

    any_spec = pl.BlockSpec(memory_space=pl.ANY)
    return pl.pallas_call(
        body, grid=(nq,),
        in_specs=[_rows(tq, BW), _const((S + 2 * WIN, 128)), _const((S + 2 * WIN, 128)), _const((1, NH)),
                  _rows(tq, BW), _rows(tq, NH), _rows(tq, NH)],
        out_specs=[_rows(tq, BW), any_spec, any_spec, _const((8, 128))],
        out_shape=[jax.ShapeDtypeStruct((S, BW), F32), jax.ShapeDtypeStruct((nc, 128, tq), F32),
                   jax.ShapeDtypeStruct((nc, 128, tq), F32), jax.ShapeDtypeStruct((8, 128), F32)],
        scratch_shapes=[pltpu.VMEM((nc, 128, tq), F32), pltpu.VMEM((nc, 128, tq), F32), pltpu.SemaphoreType.DMA((2,))],
        compiler_params=_params(1), name="attn_b_bwd",
    )(q, kp, vp, sink2, do, lse, delta)


def _qk_bwd(dqa, dka, dva, dqb, dkb, dvb, qar, kar, qg2, kg2, tab_a, tab_b, dga, dgb):
    S = dqa.shape[0]
    tm = min(256, S)

    def body(dqa_ref, dka_ref, dva_ref, dqb_ref, dkb_ref, dvb_ref, qar_ref, kar_ref, qg_ref, kg_ref, ta_ref, tb_ref,
             dga_ref, dgb_ref, dp_ref, st_ref):
        @pl.when(pl.program_id(0) == 0)
        def _():
            st_ref[...] = jnp.zeros_like(st_ref)

        seg = _seg_matrix(128, HD)

        def norm_bwd(dz_rot, raw, g):
            dzn = _rope_t(dz_rot, ta_ref, 16)
            raw = raw.astype(F32)
            rr = lax.rsqrt(_seg_sum(raw * raw, seg) * (1.0 / HD) + EPS)
            zhat = raw * rr
            dzh = dzn * g
            draw = rr * (dzh - zhat * (_seg_sum(dzh * zhat, seg) * (1.0 / HD)))
            return draw, jnp.sum(dzn * zhat, axis=0, keepdims=True)

        gq = jnp.zeros((1, 128), F32)
        for p in range(4):
            sl = slice(128 * p, 128 * p + 128)
            draw, gsum = norm_bwd(dqa_ref[:, sl] * 0.125, qar_ref[:, sl], qg_ref[...])
            gq = gq + gsum
            dp_ref[:, sl] = draw.astype(BF16)
            dp_ref[:, 768 + 128 * p:768 + 128 * p + 128] = _rope_t(dqb_ref[:, sl] * 0.125, tb_ref, 32).astype(BF16)
        draw, gk = norm_bwd(dka_ref[...] * LN2, kar_ref[...], kg_ref[...])
        dp_ref[:, 512:640] = draw.astype(BF16)
        dp_ref[:, 640:768] = dva_ref[...].astype(BF16)
        dp_ref[:, 1280:1408] = _rope_t(dkb_ref[...] * LN2, tb_ref, 32).astype(BF16)
        dp_ref[:, 1408:1536] = dvb_ref[...].astype(BF16)
        dp_ref[:, 1536:2560] = dga_ref[...]
        dp_ref[:, 2560:3584] = dgb_ref[...]
        st_ref[0:1, :] += gq
        st_ref[1:2, :] += gk

    tab = pl.BlockSpec((3, tm, 128), lambda i: (0, i, 0))
    return pl.pallas_call(
        body, grid=(S // tm,),
        in_specs=[_rows(tm, BW), _rows(tm, 128), _rows(tm, 128), _rows(tm, BW), _rows(tm, 128), _rows(tm, 128),
                  _rows(tm, BW), _rows(tm, 128), _const((1, 128)), _const((1, 128)), tab, tab, _rows(tm, D), _rows(tm, D)],
        out_specs=[_rows(tm, INW), _const((8, 128))],
        out_shape=[jax.ShapeDtypeStruct((S, INW), BF16), jax.ShapeDtypeStruct((8, 128), F32)],
        compiler_params=_params(1), name="qk_bwd",
    )(dqa, dka, dva, dqb, dkb, dvb, qar, kar, qg2, kg2, tab_a, tab_b, dga, dgb)


def _in_bwd(dproj, win, x, dx1, modv, n1g, chunks):
    S = x.shape[0]
    tm = min(256, S)
    n = S // tm

    def body(dp_ref, w_ref, x_ref, dx1_ref, mod_ref, g_ref, c_hbm, gx_ref, st_ref, recv_hbm, send_sems, recv_sems):
        start, finish = _exchange_scatter([c_hbm], [recv_hbm], send_sems, recv_sems)

        @pl.when(pl.program_id(0) == 0)
        def _():
            start()
            st_ref[...] = jnp.zeros_like(st_ref)

        dh = lax.dot_general(dp_ref[...], w_ref[...], NT, preferred_element_type=F32)
        xt = x_ref[...]
        r = lax.rsqrt(jnp.mean(xt * xt, axis=-1, keepdims=True) + EPS)
        xn = xt * r
        st_ref[0:1, :] += jnp.sum(dh, axis=0, keepdims=True)
        st_ref[1:2, :] += jnp.sum(dh * xn, axis=0, keepdims=True)
        dxn = dh * (g_ref[...] * (1.0 + mod_ref[1:2, :]))
        gx_ref[...] = dx1_ref[...] + r * (dxn - xn * jnp.mean(dxn * xn, axis=-1, keepdims=True))
        pl.when(pl.program_id(0) == n - 1)(finish)

    any_spec = pl.BlockSpec(memory_space=pl.ANY)
    return pl.pallas_call(
        body, grid=(n,),
        in_specs=[_rows(tm, INW), _const((D, INW)), _rows(tm, D), _rows(tm, D), _const((6, D)), _const((1, D)), any_spec],
        out_specs=[_rows(tm, D), _const((8, D)), any_spec],
        out_shape=[jax.ShapeDtypeStruct((S, D), F32), jax.ShapeDtypeStruct((8, D), F32),
                   jax.ShapeDtypeStruct((N_DEV - 1,) + chunks.shape[1:], chunks.dtype)],
        scratch_shapes=_exchange_sems(1),
        compiler_params=_params(1), name="in_bwd",
    )(dproj, win, x, dx1, modv, n1g, chunks)


def _pack_small(st1, st2, stf, dg1, dg2, stqk, dsink, modv, n1g, n2g):
    def body(st1_ref, st2_ref, stf_ref, dg1_ref, dg2_ref, qk_ref, ds_ref, mod_ref, g1_ref, g2_ref, o_ref):
        a1, b1 = st1_ref[0:1, :], st1_ref[1:2, :]
        a2, b2 = st2_ref[0:1, :], st2_ref[1:2, :]
        r = lax.broadcasted_iota(jnp.int32, (128, D), 0)
        c = lax.broadcasted_iota(jnp.int32, (128, D), 1)
        fold_q = (c == r % HD).astype(F32)
        fold_k = (c == HD + r % HD).astype(F32)
        keep = (c == r).astype(F32)

        def place(v, sel):
            return jnp.dot(v, sel, preferred_element_type=F32, precision=lax.Precision.HIGHEST)

        loss = jnp.sum(stf_ref[1:2, :], axis=1, keepdims=True)
        lane = lax.broadcasted_iota(jnp.int32, (1, D), 1)
        rows = [a1, g1_ref[...] * b1, dg1_ref[0:1, :], a2, g2_ref[...] * b2, dg2_ref[0:1, :],
                (1.0 + mod_ref[1:2, :]) * b1, (1.0 + mod_ref[4:5, :]) * b2, stf_ref[0:1, :],
                place(qk_ref[0:1, :], fold_q) + place(qk_ref[1:2, :], fold_k),
                place(ds_ref[0:1, :], keep),
                jnp.where(lane == 0, loss, 0.0)]
        rows += [jnp.zeros((1, D), F32)] * (SMALL_ROWS - len(rows))
        for n, v in enumerate(rows):
            o_ref[n:n + 1, :] = v

    return pl.pallas_call(
        body, out_shape=jax.ShapeDtypeStruct((SMALL_ROWS, D), F32),
        compiler_params=pltpu.CompilerParams(vmem_limit_bytes=V7X_VMEM_LIMIT), name="pack_small",
    )(st1, st2, stf, dg1, dg2, stqk, dsink, modv, n1g, n2g)


def _wada_grad(silu_all, dmod_cols):
    def body(a_ref, b_ref, o_ref):
        o_ref[...] = lax.dot_general(a_ref[...], b_ref[...], TN, preferred_element_type=F32, precision=lax.Precision.HIGHEST)

    return pl.pallas_call(
        body, out_shape=jax.ShapeDtypeStruct((D, dmod_cols.shape[1]), F32),
        compiler_params=pltpu.CompilerParams(vmem_limit_bytes=V7X_VMEM_LIMIT), name="wada_grad",
    )(silu_all, dmod_cols)


def _adamw_sum(parts, w, m, v, name):
    R, C = w.shape
    tr = R if R <= 64 else next(t for t in (256, 128, 64, 32, 16, 8) if R % t == 0)
    n = len(parts)
    dyn = [idx for _, idx in parts if idx is not None and not isinstance(idx, int)]
    b1c = 1.0 - ADAM_B1 ** ADAM_STEP
    b2c = 1.0 - ADAM_B2 ** ADAM_STEP

    def body(*refs):
        refs = refs[len(dyn):]
        g = refs[0][...].astype(F32)
        for k in range(1, n):
            g = g + refs[k][...].astype(F32)
        w_ref, m_ref, v_ref, g_out, d_out, m_out, v_out = refs[n:]
        mn = ADAM_B1 * m_ref[...] + (1.0 - ADAM_B1) * g
        vn = ADAM_B2 * v_ref[...] + (1.0 - ADAM_B2) * jnp.square(g)
        g_out[...] = g
        m_out[...] = mn
        v_out[...] = vn
        d_out[...] = -ADAM_LR * ((mn / b1c) / (jnp.sqrt(vn / b2c) + ADAM_EPS) + ADAM_WD * w_ref[...])

    in_specs = []
    nd = 0
    for a, idx in parts:
        if idx is None:
            in_specs.append(pl.BlockSpec((tr, C), lambda i, *s: (i, 0)))
        elif isinstance(idx, int):
            in_specs.append(pl.BlockSpec((None, tr, C), lambda i, *s, idx=idx: (idx, i, 0)))
        else:
            in_specs.append(pl.BlockSpec((None, tr, C), lambda i, *s, nd=nd: (s[nd][0], i, 0)))
            nd += 1
    blk = pl.BlockSpec((tr, C), lambda i, *s: (i, 0))
    grid_spec = pltpu.PrefetchScalarGridSpec(
        num_scalar_prefetch=len(dyn), grid=(R // tr,), in_specs=in_specs + [blk] * 3, out_specs=[blk] * 4)
    return pl.pallas_call(
        body, grid_spec=grid_spec, out_shape=[jax.ShapeDtypeStruct((R, C), F32)] * 4,
        compiler_params=_params(1), name=name,
    )(*dyn, *[a for a, _ in parts], w, m, v)


def _me():
    return lax.axis_index("x"), lax.axis_index("y"), lax.axis_index("c")


def _peer(k):
    x, y, c = _me()
    return (x ^ ((k >> 2) & 1), y ^ ((k >> 1) & 1), c ^ (k & 1))


def _ada_exchange(c_row, w_ada, b_rows):
    NW = w_ada.shape[1]

    def body(c_ref, w_ref, b_ref, sall_ref, mod_ref, src_ref, mp_ref, send1, recv1, send2, recv2):
        x, y, c = _me()
        me = 4 * x + 2 * y + c
        cv = c_ref[...]
        src_ref[...] = jnp.broadcast_to(cv * jax.nn.sigmoid(cv), (8, D))
        mine = pl.ds(pl.multiple_of(me * 8, 8), 8)
        sall_ref[mine, :] = src_ref[...]
        sends = [pltpu.make_async_remote_copy(src_ref, sall_ref.at[mine, :], send1.at[k - 1], recv1.at[k - 1],
                                              device_id=_peer(k), device_id_type=MESH) for k in range(1, N_DEV)]
        for cp in sends:
            cp.start()
        for k in range(1, N_DEV):
            theirs = pl.ds(pl.multiple_of((me ^ k) * 8, 8), 8)
            pltpu.make_async_remote_copy(src_ref, sall_ref.at[theirs, :], send1.at[k - 1], recv1.at[k - 1],
                                         device_id=_peer(k), device_id_type=MESH).wait_recv()
        for cp in sends:
            cp.wait_send()
        mp_ref[...] = jnp.dot(sall_ref[...], w_ref[...], preferred_element_type=F32, precision=lax.Precision.HIGHEST)
        mod_ref[mine, :] = mp_ref[mine, :] + b_ref[mine, :]
        sends = []
        for k in range(1, N_DEV):
            theirs = pl.ds(pl.multiple_of((me ^ k) * 8, 8), 8)
            sends.append(pltpu.make_async_remote_copy(mp_ref.at[theirs, :], mod_ref.at[mine, :], send2.at[k - 1], recv2.at[k - 1],
                                                      device_id=_peer(k), device_id_type=MESH))
        for cp in sends:
            cp.start()
        for k in range(1, N_DEV):
            theirs = pl.ds(pl.multiple_of((me ^ k) * 8, 8), 8)
            pltpu.make_async_remote_copy(mp_ref.at[mine, :], mod_ref.at[theirs, :], send2.at[k - 1], recv2.at[k - 1],
                                         device_id=_peer(k), device_id_type=MESH).wait_recv()
            mod_ref[theirs, :] = mod_ref[theirs, :] + b_ref[theirs, :]
        for cp in sends:
            cp.wait_send()

    vm = pl.BlockSpec(memory_space=pltpu.VMEM)
    return pl.pallas_call(
        body, in_specs=[vm, vm, vm], out_specs=[vm, vm],
        out_shape=[jax.ShapeDtypeStruct((8 * N_DEV, D), F32), jax.ShapeDtypeStruct((8 * N_DEV, NW), F32)],
        scratch_shapes=[pltpu.VMEM((8, D), F32), pltpu.VMEM((8 * N_DEV, NW), F32)]
        + [pltpu.SemaphoreType.DMA((N_DEV - 1,))] * 4,
        compiler_params=pltpu.CompilerParams(vmem_limit_bytes=V7X_VMEM_LIMIT), name="ada_exchange",
    )(c_row, w_ada, b_rows)


def _weight_gather(shard):
    def body(x_ref, out_ref, send_sems, recv_sems, local_sem):
        x, y, c = _me()
        me, sibling = (x, y, c), (x, y, 1 - c)
        chips = [(1 - x, y), (x, 1 - y), (1 - x, 1 - y)]

        def slot(px, py, pc):
            return out_ref.at[4 * px + 2 * py + pc]

        def copy(k, block, to, src=None):
            return pltpu.make_async_remote_copy(
                src_ref=slot(*block) if src is None else src, dst_ref=slot(*block),
                send_sem=send_sems.at[k], recv_sem=recv_sems.at[k], device_id=to, device_id_type=MESH)

        mine = pltpu.make_async_copy(x_ref, slot(*me), local_sem)
        mine.start()
        first = [copy(0, me, sibling, src=x_ref)]
        first += [copy(1 + j, me, (*chip, c), src=x_ref) for j, chip in enumerate(chips)]
        for cp in first:
            cp.start()
        passed = [copy(4 + j, (*chip, c), sibling) for j, chip in enumerate(chips)]
        for j, chip in enumerate(chips):
            copy(1 + j, (*chip, c), me).wait_recv()
            passed[j].start()
        copy(0, sibling, me).wait_recv()
        for j, chip in enumerate(chips):
            copy(4 + j, (*chip, 1 - c), me).wait_recv()
        for cp in first + passed:
            cp.wait_send()
        mine.wait()

    any_spec = pl.BlockSpec(memory_space=pl.ANY)
    return pl.pallas_call(
        body, in_specs=[any_spec], out_specs=any_spec,
        out_shape=jax.ShapeDtypeStruct((N_DEV,) + shard.shape, shard.dtype),
        scratch_shapes=[pltpu.SemaphoreType.DMA((7,)), pltpu.SemaphoreType.DMA((7,)), pltpu.SemaphoreType.DMA],
        name="weight_gather",
    )(shard)


def _small_gather(block):
    def body(b_ref, out_ref, send_sems, recv_sems):
        x, y, c = _me()
        me = 4 * x + 2 * y + c
        out_ref[me] = b_ref[...]
        sends = [pltpu.make_async_remote_copy(b_ref, out_ref.at[me], send_sems.at[k - 1], recv_sems.at[k - 1],
                                              device_id=_peer(k), device_id_type=MESH) for k in range(1, N_DEV)]
        for cp in sends:
            cp.start()
        for k in range(1, N_DEV):
            pltpu.make_async_remote_copy(b_ref, out_ref.at[me ^ k], send_sems.at[k - 1], recv_sems.at[k - 1],
                                         device_id=_peer(k), device_id_type=MESH).wait_recv()
        for cp in sends:
            cp.wait_send()

    vm = pl.BlockSpec(memory_space=pltpu.VMEM)
    return pl.pallas_call(
        body, in_specs=[vm], out_specs=vm,
        out_shape=jax.ShapeDtypeStruct((N_DEV,) + block.shape, block.dtype),
        scratch_shapes=[pltpu.SemaphoreType.DMA((N_DEV - 1,)), pltpu.SemaphoreType.DMA((N_DEV - 1,))],
        name="small_gather",
    )(block)


def _pack_small_params(b_ada, n1, n2, fg, qn, kn, sink):
    z = jnp.zeros((SMALL_ROWS, D), F32)
    z = z.at[0:6].set(b_ada.reshape(6, D)).at[6].set(n1.reshape(D)).at[7].set(n2.reshape(D)).at[8].set(fg.reshape(D))
    z = z.at[9, 0:HD].set(qn.reshape(HD)).at[9, HD:2 * HD].set(kn.reshape(HD)).at[10, 0:NH].set(sink.reshape(NH))
    return z


def _unpack_small(p):
    return (p[0:6].reshape(1, 6 * D), p[6].reshape(1, D), p[9, 0:HD].reshape(1, HD), p[9, HD:2 * HD].reshape(1, HD),
            p[10, 0:NH].reshape(1, NH), p[7].reshape(1, D), p[8].reshape(D))


def kernel(x, c, w_ada, b_ada, norm1_g, w_in, q_norm_a, k_norm_a, sink_b, w_branch, w_out, norm2_g, w_mlp_in, w_mlp_out, final_g, loss_target, m_w_ada, m_b_ada, m_norm1_g, m_w_in, m_q_norm_a, m_k_norm_a, m_sink_b, m_w_branch, m_w_out, m_norm2_g, m_w_mlp_in, m_w_mlp_out, m_final_g, v_w_ada, v_b_ada, v_norm1_g, v_w_in, v_q_norm_a, v_k_norm_a, v_sink_b, v_w_branch, v_w_out, v_norm2_g, v_w_mlp_in, v_w_mlp_out, v_final_g):
    S = x.shape[1]
    xs = x.reshape(S, D)
    tgt = loss_target.reshape(S, D)
    ax, ay, ac = lax.axis_index("x"), lax.axis_index("y"), lax.axis_index("c")
    me = 4 * ax + 2 * ay + ac
    me1 = me.reshape(1).astype(jnp.int32)
    NW = w_ada.shape[2]
    NI = w_in.shape[2]

    silu64, mod64 = _ada_exchange(c.reshape(1, D), w_ada.reshape(D, NW),
                                  jnp.repeat(b_ada.reshape(N_DEV, NW), 8, axis=0))
    silu_all = silu64[0::8]
    modv = mod64[0::8].reshape(6, D)

    win = _weight_gather(w_in[0].astype(BF16)).transpose(1, 0, 2).reshape(D, INW)
    rest_shards = tuple(w[0].astype(BF16) for w in (w_branch, w_out, w_mlp_in, w_mlp_out))

    tab_a, tab_b = _rope_tables(S)
    qg2 = jnp.tile(q_norm_a.reshape(1, HD), (1, 2))
    kg2 = jnp.tile(k_norm_a.reshape(1, HD), (1, 2))
    n1g = norm1_g.reshape(1, D)
    n2g = norm2_g.reshape(1, D)
    fg = final_g.reshape(1, D)
    sink2 = sink_b.reshape(1, NH) * LOG2E

    h, qar, kar, qa, ka, va, qb, kb, vb, ga, gb = _in_proj(xs, modv, n1g, win, qg2, kg2, tab_a, tab_b)
    tk = min(512, S)
    ya_t, lse_at, wb, wout, wmi, wmo = _attn_a_fwd(qa, ka, va.reshape(S // tk, tk, 128).transpose(0, 2, 1), rest_shards)
    ya, lse_a = ya_t.T, lse_at.T
    wb = wb.transpose(1, 2, 0, 3).reshape(2, BW, D)
    wout = wout.reshape(D, D)
    pad = ((WIN, WIN), (0, 0))
    kbp, vbp = jnp.pad(kb, pad), jnp.pad(vb, pad)
    yb, lse_b = _attn_b_fwd(qb, kbp, vbp, sink2)
    x1, merged, ua, ub = _merge_out(ya, yb, ga, gb, xs, modv, wb, wout)
    h2, hp, dx2, stf = _mlp_fwd(x1, modv, n2g, wmi, wmo, fg, tgt)

    dhp, dx1, st2 = _mlp_bwd(dx2, x1, hp, modv, n2g, wmi, wmo)
    m2 = _tn_matmul(hp, dx2, 2048, D, "dw_mlp_out", relu_sq=True)
    g_wmo, dg2 = _scale_gate(m2, wmo.reshape(FF, D), modv, 5, "gate2_grad")
    g_wmi = _tn_matmul(h2, dhp, D, 512, "dw_mlp_in", dev_major=True)
    dua, dub, dga, dgb, dya, dyb, dl_a, dl_b = _merge_bwd(dx1, modv, ga, gb, ua, ub, ya, yb, wb, wout)
    m1 = _tn_matmul(merged, dx1, D, D, "dw_out")
    g_wout, dg1 = _scale_gate(m1, wout, modv, 2, "gate1_grad")
    g_wb0 = _tn_matmul(ya, dua, BW, D, "dw_branch_a")
    g_wb1 = _tn_matmul(yb, dub, BW, D, "dw_branch_b")
    g_wb = jnp.stack([g_wb0, g_wb1]).reshape(2, BW, N_DEV, 128).transpose(2, 0, 1, 3).reshape(N_DEV, 2 * BW, 128)
    g_wout = g_wout.reshape(N_DEV, 128, D)
    g_wmo = g_wmo.reshape(N_DEV, 512, D)
    ka_t3 = ka.reshape(S // tk, tk, 128).transpose(0, 2, 1)
    dqa_t, dka_t, dva_t, r_wb, r_wout, r_wmi, r_wmo = _attn_a_bwd(qa, qa.T, ka, ka_t3, va, dya, dya.T, lse_a, dl_a,
                                                                  (g_wb, g_wout, g_wmi, g_wmo))
    dqa = dqa_t.T
    dka = dka_t.transpose(0, 2, 1).reshape(S, 128)
    dva = dva_t.transpose(0, 2, 1).reshape(S, 128)
    dqb, dkb_t, dvb_t, dsink = _attn_b_bwd(qb, kbp, vbp, sink2, dyb, lse_b, dl_b)
    dkb = dkb_t.transpose(0, 2, 1).reshape(S + 2 * WIN, 128)[WIN:WIN + S]
    dvb = dvb_t.transpose(0, 2, 1).reshape(S + 2 * WIN, 128)[WIN:WIN + S]
    dproj, stqk = _qk_bwd(dqa, dka, dva, dqb, dkb, dvb, qar, kar, qg2, kg2, tab_a, tab_b, dga, dgb)
    g_win = _tn_matmul(h, dproj, D, 896, "dw_in")
    g_win = g_win.reshape(D, N_DEV, NI).transpose(1, 0, 2)
    grad_x, st1, r_win = _in_bwd(dproj, win, xs, dx1, modv, n1g, g_win.astype(BF16))

    def adam(name, own, recv, w, m, v):
        shape = w.shape
        w2, m2_, v2 = (a.reshape(own.shape[1:]) for a in (w, m, v))
        outs = _adamw_sum([(own, me1)] + [(recv, k) for k in range(N_DEV - 1)], w2, m2_, v2, name)
        return [a.reshape(shape) for a in outs]

    o_win = adam("adamw_w_in", g_win, r_win, w_in, m_w_in, v_w_in)
    o_wb = adam("adamw_w_branch", g_wb, r_wb, w_branch, m_w_branch, v_w_branch)
    o_wout = adam("adamw_w_out", g_wout, r_wout, w_out, m_w_out, v_w_out)
    o_wmi = adam("adamw_w_mlp_in", g_wmi, r_wmi, w_mlp_in, m_w_mlp_in, v_w_mlp_in)
    o_wmo = adam("adamw_w_mlp_out", g_wmo, r_wmo, w_mlp_out, m_w_mlp_out, v_w_mlp_out)

    small = _pack_small(st1, st2, stf, dg1, dg2, stqk, dsink, modv, n1g, n2g)
    small_all = _small_gather(small)
    sw = _pack_small_params(b_ada, norm1_g, norm2_g, final_g, q_norm_a, k_norm_a, sink_b)
    sm = _pack_small_params(m_b_ada, m_norm1_g, m_norm2_g, m_final_g, m_q_norm_a, m_k_norm_a, m_sink_b)
    sv = _pack_small_params(v_b_ada, v_norm1_g, v_norm2_g, v_final_g, v_q_norm_a, v_k_norm_a, v_sink_b)
    sm_out = _adamw_sum([(small_all, k) for k in range(N_DEV)], sw, sm, sv, "adamw_small")
    loss = sm_out[0][11, 0]
    sm_out = [_unpack_small(a) for a in sm_out]

    dmod_all = small_all[:, 0:6, :].reshape(N_DEV, 6 * D)
    dmod_cols = lax.dynamic_slice_in_dim(dmod_all, me * NW, NW, axis=1)
    g_wada = _wada_grad(silu_all, dmod_cols)
    ada = _adamw_sum([(g_wada, None)], w_ada.reshape(D, NW), m_w_ada.reshape(D, NW), v_w_ada.reshape(D, NW), "adamw_ada")
    ada = [a.reshape(1, D, NW) for a in ada]

    def leaves(k):
        b_, n1_, qn_, kn_, sk_, n2_, fg_ = sm_out[k]
        return [ada[k], b_, n1_, o_win[k], qn_, kn_, sk_, o_wb[k], o_wout[k], n2_, o_wmi[k], o_wmo[k], fg_]

    return (loss, grad_x.reshape(1, S, D), *leaves(0), *leaves(1), *leaves(2), *leaves(3))
```

```python
import jax
import jax.numpy as jnp
from jax import lax
from jax.experimental import pallas as pl
from jax.experimental.pallas import tpu as pltpu

F32, BF16 = jnp.float32, jnp.bfloat16
MESH = pl.DeviceIdType.MESH

D = 1024
HD = 64
NH = 8
GRP = 4
BW = 512
FF = 4096
INW = 3584
GRID_W = 64
WIN = 128
THETA = 10000.0
EPS = 1e-6
NEG = -1e30
N_DEV = 8
LOG2E = 1.4426950408889634
LN2 = 0.6931471805599453
QA_SCALE = 0.125 * LOG2E
SMALL_ROWS = 16
V7X_VMEM_LIMIT = 56 * 1024 * 1024

ADAM_LR, ADAM_B1, ADAM_B2, ADAM_EPS, ADAM_WD, ADAM_STEP = 0.001, 0.9, 0.999, 1e-08, 0.01, 10

NT = (((1,), (1,)), ((), ()))
TN = (((0,), (0,)), ((), ()))


def _params(n_axes, vmem=V7X_VMEM_LIMIT):
    return pltpu.CompilerParams(dimension_semantics=("arbitrary",) * n_axes, vmem_limit_bytes=vmem)


def _const(shape):
    return pl.BlockSpec(shape, lambda *_: (0,) * len(shape))


def _rows(tm, width):
    return pl.BlockSpec((tm, width), lambda i, *_: (i, 0))


def _seg_matrix(n, seg):
    r = lax.broadcasted_iota(jnp.int32, (n, n), 0) // seg
    c = lax.broadcasted_iota(jnp.int32, (n, n), 1) // seg
    return (r == c).astype(F32)


def _seg_sum(z, seg_mat):
    return jnp.dot(z, seg_mat, preferred_element_type=F32, precision=lax.Precision.HIGHEST)


def _rope(z, t_ref, sh):
    return z * t_ref[0] + pltpu.roll(z, sh, 1) * t_ref[1] + pltpu.roll(z, 128 - sh, 1) * t_ref[2]


def _rope_t(dz, t_ref, sh):
    return dz * t_ref[0] + pltpu.roll(dz * t_ref[1], 128 - sh, 1) + pltpu.roll(dz * t_ref[2], sh, 1)


def _rope_tables(S):
    t = jnp.arange(S, dtype=jnp.int32)
    lane = jnp.arange(HD)

    def build(cos, sin, first):
        t0 = cos
        t1 = jnp.where(first[None, :], 0.0, sin)
        t2 = jnp.where(first[None, :], -sin, 0.0)
        return jnp.stack([jnp.tile(a, (1, 2)) for a in (t0, t1, t2)]).astype(F32)

    inv_a = THETA ** (-jnp.arange(0, HD // 2, 2, dtype=F32) / (HD // 2))
    ar = (t // GRID_W).astype(F32)[:, None] * inv_a[None, :]
    ac = (t % GRID_W).astype(F32)[:, None] * inv_a[None, :]
    cos_a = jnp.concatenate([jnp.cos(ar), jnp.cos(ar), jnp.cos(ac), jnp.cos(ac)], axis=1)
    sin_a = jnp.concatenate([jnp.sin(ar), jnp.sin(ar), jnp.sin(ac), jnp.sin(ac)], axis=1)
    tab_a = build(cos_a, sin_a, (lane % 32) < 16)
    inv_b = THETA ** (-jnp.arange(0, HD, 2, dtype=F32) / HD)
    ab = t.astype(F32)[:, None] * inv_b[None, :]
    cos_b = jnp.concatenate([jnp.cos(ab), jnp.cos(ab)], axis=1)
    sin_b = jnp.concatenate([jnp.sin(ab), jnp.sin(ab)], axis=1)
    tab_b = build(cos_b, sin_b, lane < 32)
    return tab_a, tab_b


def _in_proj(x, modv, n1g, win, qg2, kg2, tab_a, tab_b):
    S = x.shape[0]
    tm = min(256, S)
    tk = min(512, S)
    per = tk // tm

    def body(x_ref, mod_ref, g_ref, w_ref, qg_ref, kg_ref, ta_ref, tb_ref,
             h_ref, qar_ref, kar_ref, qa_ref, ka_ref, va_ref, qb_ref, kb_ref, vb_ref, ga_ref, gb_ref, qat_ref, kat_ref, vat_ref):
        xt = x_ref[...]
        r = lax.rsqrt(jnp.mean(xt * xt, axis=-1, keepdims=True) + EPS)
        h = ((xt * r) * g_ref[...]) * (1.0 + mod_ref[1:2, :]) + mod_ref[0:1, :]
        hb = h.astype(BF16)
        h_ref[...] = hb
        proj = jnp.dot(hb, w_ref[...], preferred_element_type=F32)
        seg = _seg_matrix(128, HD)

        def head_norm(z, g):
            ms = _seg_sum(z * z, seg) * (1.0 / HD)
            return (z * lax.rsqrt(ms + EPS)) * g

        for p in range(4):
            z = proj[:, 128 * p:128 * p + 128]
            qar_ref[:, 128 * p:128 * p + 128] = z.astype(BF16)
            qv = _rope(head_norm(z, qg_ref[...]), ta_ref, 16) * QA_SCALE
            qa_ref[:, 128 * p:128 * p + 128] = qv.astype(BF16)
            qat_ref[128 * p:128 * p + 128, :] = qv.T.astype(BF16)
            zb = proj[:, 768 + 128 * p:768 + 128 * p + 128]
            qb_ref[:, 128 * p:128 * p + 128] = (_rope(zb, tb_ref, 32) * QA_SCALE).astype(BF16)
        z = proj[:, 512:640]
        kar_ref[...] = z.astype(BF16)
        kv_ = _rope(head_norm(z, kg_ref[...]), ta_ref, 16)
        ka_ref[...] = kv_.astype(BF16)
        kat_ref[...] = kv_.T.astype(BF16)
        va_ref[...] = proj[:, 640:768].astype(BF16)
        vat_ref[...] = proj[:, 640:768].T.astype(BF16)
        kb_ref[...] = _rope(proj[:, 1280:1408], tb_ref, 32).astype(BF16)
        vb_ref[...] = proj[:, 1408:1536].astype(BF16)
        ga_ref[...] = proj[:, 1536:2560].astype(BF16)
        gb_ref[...] = proj[:, 2560:3584].astype(BF16)

    tab = pl.BlockSpec((3, tm, 128), lambda i: (0, i, 0))
    shapes = [(D, BF16), (BW, BF16), (128, BF16), (BW, BF16), (128, BF16), (128, BF16),
              (BW, BF16), (128, BF16), (128, BF16), (D, BF16), (D, BF16)]
    return pl.pallas_call(
        body, grid=(S // tm,),
        in_specs=[_rows(tm, D), _const((6, D)), _const((1, D)), _const((D, INW)), _const((1, 128)), _const((1, 128)), tab, tab],
        out_specs=[_rows(tm, w) for w, _ in shapes] + [pl.BlockSpec((BW, tm), lambda i: (0, i))]
        + [pl.BlockSpec((None, 128, tm), lambda i: (i // per, 0, i % per))] * 2,
        out_shape=[jax.ShapeDtypeStruct((S, w), dt) for w, dt in shapes] + [jax.ShapeDtypeStruct((BW, S), BF16)]
        + [jax.ShapeDtypeStruct((S // tk, 128, tk), BF16)] * 2,
        compiler_params=_params(1), name="in_proj",
    )(x, modv, n1g, win, qg2, kg2, tab_a, tab_b)


def _exchange_gather(block_refs, out_refs, send_sems, recv_sems, local_sems):
    x, y, c = _me()
    me = 4 * x + 2 * y + c

    def copies():
        own, out, arrive = [], [], []
        for a, (blk, dst) in enumerate(zip(block_refs, out_refs)):
            own.append(pltpu.make_async_copy(blk, dst.at[me], local_sems.at[a]))
            for k in range(1, N_DEV):
                sems = dict(send_sem=send_sems.at[a, k - 1], recv_sem=recv_sems.at[a, k - 1], device_id=_peer(k), device_id_type=MESH)
                out.append(pltpu.make_async_remote_copy(blk, dst.at[me], **sems))
                arrive.append(pltpu.make_async_remote_copy(blk, dst.at[me ^ k], **sems))
        return own, out, arrive

    def start():
        own, out, _ = copies()
        for cp in own + out:
            cp.start()

    def finish():
        own, out, arrive = copies()
        for cp in arrive:
            cp.wait_recv()
        for cp in out:
            cp.wait_send()
        for cp in own:
            cp.wait()

    return start, finish


def _exchange_scatter(chunk_refs, recv_refs, send_sems, recv_sems):
    x, y, c = _me()
    me = 4 * x + 2 * y + c

    def copies():
        return [pltpu.make_async_remote_copy(src.at[me ^ k], dst.at[k - 1], send_sems.at[a, k - 1], recv_sems.at[a, k - 1],
                                             device_id=_peer(k), device_id_type=MESH)
                for a, (src, dst) in enumerate(zip(chunk_refs, recv_refs)) for k in range(1, N_DEV)]

    def start():
        for cp in copies():
            cp.start()

    def finish():
        cps = copies()
        for cp in cps:
            cp.wait_recv()
        for cp in cps:
            cp.wait_send()

    return start, finish


def _exchange_sems(n):
    return [pltpu.SemaphoreType.DMA((n, N_DEV - 1)), pltpu.SemaphoreType.DMA((n, N_DEV - 1))]


def _attn_a_fwd(q, k, vt3, shards):
    S = q.shape[0]
    tq = min(512, S)
    nq = S // tq
    nk, _, tk = vt3.shape
    ONES = 16
    AHEAD = 2
    ns = len(shards)

    def body(q_ref, k_ref, vt_ref, *rest):
        w_hbm, (o_ref, lse_ref), wall_hbm = rest[:ns], rest[ns:ns + 2], rest[ns + 2:2 * ns + 2]
        st_sc, send_sems, recv_sems, local_sems = rest[2 * ns + 2:]
        start, finish = _exchange_gather(w_hbm, wall_hbm, send_sems, recv_sems, local_sems)
        pl.when(pl.program_id(0) == 0)(start)
        row8 = lax.broadcasted_iota(jnp.int32, (NH, tq), 0)
        lse_all = jnp.zeros((NH, tq), F32)
        ones = jnp.ones((ONES, tk), BF16)
        for kv in range(2):
            qs = []
            for pp in range(2):
                qp = q_ref[:, 128 * (2 * kv + pp):128 * (2 * kv + pp) + 128]
                qs += [qp[:, :HD], qp[:, HD:]]

            def keys(j, kv=kv):
                return k_ref[pl.ds(pl.multiple_of(j * tk, tk), tk), :][:, HD * kv:HD * kv + HD]

            def scores(kj, t, qs=qs):
                return lax.dot_general(kj, qs[t], NT, preferred_element_type=F32)

            def step(j, carry, kv=kv):
                kj = keys(j)
                kn = keys(jnp.minimum(j + 1, nk - 1))
                v1 = jnp.concatenate([vt_ref[j, HD * kv:HD * kv + HD, :], ones], axis=0)
                sts = [st_sc[t] for t in range(AHEAD)]
                new = []
                for t in range(GRP):
                    m, acc = carry[2 * t], carry[2 * t + 1]
                    if t + AHEAD < GRP:
                        sts.append(scores(kj, t + AHEAD))
                    st = sts[t]
                    mn = jnp.maximum(m, jnp.max(st, axis=0, keepdims=True))
                    pt = jnp.exp2(st - mn)
                    if t + AHEAD >= GRP:
                        st_sc[t + AHEAD - GRP] = scores(kn, t + AHEAD - GRP)
                    acc = jnp.exp2(m - mn) * acc + jnp.dot(v1, pt.astype(BF16), preferred_element_type=F32)
                    new += [mn, acc]
                return tuple(new)

            k0 = keys(0)
            for t in range(AHEAD):
                st_sc[t] = scores(k0, t)
            init = (jnp.full((1, tq), NEG, F32), jnp.zeros((HD + ONES, tq), F32)) * GRP
            res = lax.fori_loop(0, nk, step, init)
            outs = []
            for t in range(GRP):
                m, acc = res[2 * t], res[2 * t + 1]
                l = acc[HD:HD + 1, :]
                outs.append((acc[:HD, :] / l).T)
                lse_all = jnp.where(row8 == GRP * kv + t, m + jnp.log2(l), lse_all)
            o_ref[:, 256 * kv:256 * kv + 256] = jnp.concatenate(outs, axis=1).astype(BF16)
        lse_ref[...] = lse_all
        pl.when(pl.program_id(0) == nq - 1)(finish)

    any_spec = pl.BlockSpec(memory_space=pl.ANY)
    return pl.pallas_call(
        body, grid=(nq,),
        in_specs=[_rows(tq, BW), _const((S, 128)), _const((nk, 128, tk))] + [any_spec] * ns,
        out_specs=[_rows(tq, BW), pl.BlockSpec((NH, tq), lambda i: (0, i))] + [any_spec] * ns,
        out_shape=[jax.ShapeDtypeStruct((S, BW), BF16), jax.ShapeDtypeStruct((NH, S), F32)]
        + [jax.ShapeDtypeStruct((N_DEV,) + s.shape, s.dtype) for s in shards],
        scratch_shapes=[pltpu.VMEM((AHEAD, tk, tq), F32)] + _exchange_sems(ns) + [pltpu.SemaphoreType.DMA((ns,))],
        compiler_params=_params(1), name="attn_a_fwd",
    )(q, k, vt3, *shards)


def _window_mask(i, tq, S):
    W = tq + 2 * WIN
    r = lax.broadcasted_iota(jnp.int32, (tq, W), 0)
    c = lax.broadcasted_iota(jnp.int32, (tq, W), 1)
    kpos = i * tq - WIN + c
    return (jnp.abs(c - WIN - r) <= WIN) & (kpos >= 0) & (kpos < S)


TQ_B = 256


def _attn_b_fwd(q, kp, vpt3, sink2):
    S = q.shape[0]
    tq = min(TQ_B, S)
    W = tq + 2 * WIN
    nc = vpt3.shape[0]
    ONES = 16

    def body(q_ref, k_ref, vt_ref, sink_ref, o_ref, lse_ref):
        i = pl.program_id(0)
        off = pl.multiple_of(i * tq, tq)
        r = lax.broadcasted_iota(jnp.int32, (W, tq), 1)
        c = lax.broadcasted_iota(jnp.int32, (W, tq), 0)
        kpos = i * tq - WIN + c
        valid = (jnp.abs(c - WIN - r) <= WIN) & (kpos >= 0) & (kpos < S)
        kw = k_ref[pl.ds(off, W), :]
        vt = jnp.concatenate([vt_ref[i + half] for half in range(W // tq)], axis=1)
        ones = jnp.ones((ONES, W), BF16)
        row8 = lax.broadcasted_iota(jnp.int32, (NH, tq), 0)
        lse_all = jnp.zeros((NH, tq), F32)
        qs = []
        for p in range(4):
            qp = q_ref[:, 128 * p:128 * p + 128]
            qs += [qp[:, :HD], qp[:, HD:]]
        khs = [kw[:, HD * kv:HD * kv + HD] for kv in range(2)]
        v1s = [jnp.concatenate([vt[HD * kv:HD * kv + HD, :], ones], axis=0) for kv in range(2)]

        def scores(h):
            return lax.dot_general(khs[h // GRP], qs[h], NT, preferred_element_type=F32)

        ss = [scores(0), scores(1)]
        outs = []
        for h in range(NH):
            if h + 2 < NH:
                ss.append(scores(h + 2))
            st = jnp.where(valid, ss[h], NEG)
            sk = sink_ref[:, h:h + 1]
            m = jnp.maximum(jnp.max(st, axis=0, keepdims=True), sk)
            acc = jnp.dot(v1s[h // GRP], jnp.exp2(st - m).astype(BF16), preferred_element_type=F32)
            l = acc[HD:HD + 1, :] + jnp.exp2(sk - m)
            outs.append((acc[:HD, :] / l).T)
            lse_all = jnp.where(row8 == h, m + jnp.log2(l), lse_all)
        for p in range(4):
            o_ref[:, 128 * p:128 * p + 128] = jnp.concatenate(outs[2 * p:2 * p + 2], axis=1).astype(BF16)
        lse_ref[...] = lse_all

    return pl.pallas_call(
        body, grid=(S // tq,),
        in_specs=[_rows(tq, BW), _const((S + 2 * WIN, 128)), _const((nc, 128, tq)), _const((1, NH))],
        out_specs=[_rows(tq, BW), pl.BlockSpec((NH, tq), lambda i: (0, i))],
        out_shape=[jax.ShapeDtypeStruct((S, BW), BF16), jax.ShapeDtypeStruct((NH, S), F32)],
        compiler_params=_params(1), name="attn_b_fwd",
    )(q, kp, vpt3, sink2)


def _merge_out(ya, yb, ga, gb, x, modv, wb, wout):
    S = x.shape[0]
    tm = min(256, S)

    def body(ya_ref, yb_ref, ga_ref, gb_ref, x_ref, mod_ref, wb_ref, wo_ref, x1_ref, mg_ref, ua_ref, ub_ref):
        ua = jnp.dot(ya_ref[...], wb_ref[0], preferred_element_type=F32)
        ub = jnp.dot(yb_ref[...], wb_ref[1], preferred_element_type=F32)
        merged = jax.nn.sigmoid(ga_ref[...].astype(F32)) * ua + jax.nn.sigmoid(gb_ref[...].astype(F32)) * ub
        mb = merged.astype(BF16)
        ua_ref[...] = ua.astype(BF16)
        ub_ref[...] = ub.astype(BF16)
        mg_ref[...] = mb
        x1_ref[...] = x_ref[...] + mod_ref[2:3, :] * jnp.dot(mb, wo_ref[...], preferred_element_type=F32)

    return pl.pallas_call(
        body, grid=(S // tm,),
        in_specs=[_rows(tm, BW), _rows(tm, BW), _rows(tm, D), _rows(tm, D), _rows(tm, D), _const((6, D)),
                  _const((2, BW, D)), _const((D, D))],
        out_specs=[_rows(tm, D)] * 4,
        out_shape=[jax.ShapeDtypeStruct((S, D), F32)] + [jax.ShapeDtypeStruct((S, D), BF16)] * 3,
        compiler_params=_params(1), name="merge_out",
    )(ya, yb, ga, gb, x, modv, wb, wout)


def _mlp_fwd(x1, modv, n2g, wmi, wmo, fg, target):
    S = x1.shape[0]
    tm = min(512, S)
    tf = wmi.shape[2]
    nj = wmi.shape[0]

    def body(x1_ref, mod_ref, g_ref, wi_ref, wo_ref, fg_ref, t_ref, h2_ref, hp_ref, dx2_ref, st_ref, acc_ref):
        i, j = pl.program_id(0), pl.program_id(1)

        @pl.when(j == 0)
        def _():
            xt = x1_ref[...]
            r = lax.rsqrt(jnp.mean(xt * xt, axis=-1, keepdims=True) + EPS)
            h2 = ((xt * r) * g_ref[...]) * (1.0 + mod_ref[4:5, :]) + mod_ref[3:4, :]
            h2_ref[...] = h2.astype(BF16)
            acc_ref[...] = jnp.zeros_like(acc_ref)

        @pl.when((i == 0) & (j == 0))
        def _():
            st_ref[...] = jnp.zeros_like(st_ref)

        hp = jnp.dot(h2_ref[...], wi_ref[...], preferred_element_type=F32)
        hp_ref[...] = hp.astype(BF16)
        hid = jnp.square(jnp.maximum(hp, 0.0))
        acc_ref[...] += jnp.dot(hid.astype(BF16), wo_ref[...], preferred_element_type=F32)

        @pl.when(j == nj - 1)
        def _():
            x2 = x1_ref[...] + mod_ref[5:6, :] * acc_ref[...]
            r3 = lax.rsqrt(jnp.mean(x2 * x2, axis=-1, keepdims=True) + EPS)
            xn = x2 * r3
            err = xn * fg_ref[...] - t_ref[...]
            dy = err * (1.0 / D)
            gy = dy * fg_ref[...]
            dx2_ref[...] = r3 * (gy - xn * jnp.mean(gy * xn, axis=-1, keepdims=True))
            st_ref[0:1, :] += jnp.sum(dy * xn, axis=0, keepdims=True)
            st_ref[1:2, :] += jnp.sum(err * err, axis=0, keepdims=True) * (0.5 / D)

    return pl.pallas_call(
        body, grid=(S // tm, nj),
        in_specs=[pl.BlockSpec((tm, D), lambda i, j: (i, 0)), _const((6, D)), _const((1, D)),
                  pl.BlockSpec((None, D, tf), lambda i, j: (j, 0, 0)), pl.BlockSpec((None, tf, D), lambda i, j: (j, 0, 0)),
                  _const((1, D)), pl.BlockSpec((tm, D), lambda i, j: (i, 0))],
        out_specs=[pl.BlockSpec((tm, D), lambda i, j: (i, 0)), pl.BlockSpec((tm, tf), lambda i, j: (i, j)),
                   pl.BlockSpec((tm, D), lambda i, j: (i, 0)), _const((8, D))],
        out_shape=[jax.ShapeDtypeStruct((S, D), BF16), jax.ShapeDtypeStruct((S, nj * tf), BF16),
                   jax.ShapeDtypeStruct((S, D), F32), jax.ShapeDtypeStruct((8, D), F32)],
        scratch_shapes=[pltpu.VMEM((tm, D), F32)],
        compiler_params=_params(2), name="mlp_fwd",
    )(x1, modv, n2g, wmi, wmo, fg, target)


def _mlp_bwd(dx2, x1, hp, modv, n2g, wmi, wmo):
    S = x1.shape[0]
    tm = min(512, S)
    tf = wmi.shape[2]
    nj = wmi.shape[0]

    def body(dx2_ref, x1_ref, hp_ref, mod_ref, g_ref, wi_ref, wo_ref, dhp_ref, dx1_ref, st_ref, dmo_ref, acc_ref):
        i, j = pl.program_id(0), pl.program_id(1)

        @pl.when(j == 0)
        def _():
            dmo_ref[...] = (mod_ref[5:6, :] * dx2_ref[...]).astype(BF16)
            acc_ref[...] = jnp.zeros_like(acc_ref)

        @pl.when((i == 0) & (j == 0))
        def _():
            st_ref[...] = jnp.zeros_like(st_ref)

        dhid = lax.dot_general(dmo_ref[...], wo_ref[...], NT, preferred_element_type=F32)
        dhp = (dhid * (2.0 * jnp.maximum(hp_ref[...].astype(F32), 0.0))).astype(BF16)
        dhp_ref[...] = dhp
        acc_ref[...] += lax.dot_general(dhp, wi_ref[...], NT, preferred_element_type=F32)

        @pl.when(j == nj - 1)
        def _():
            dh2 = acc_ref[...]
            xt = x1_ref[...]
            r = lax.rsqrt(jnp.mean(xt * xt, axis=-1, keepdims=True) + EPS)
            xn = xt * r
            st_ref[0:1, :] += jnp.sum(dh2, axis=0, keepdims=True)
            st_ref[1:2, :] += jnp.sum(dh2 * xn, axis=0, keepdims=True)
            dxn = dh2 * (g_ref[...] * (1.0 + mod_ref[4:5, :]))
            dx1_ref[...] = dx2_ref[...] + r * (dxn - xn * jnp.mean(dxn * xn, axis=-1, keepdims=True))

    return pl.pallas_call(
        body, grid=(S // tm, nj),
        in_specs=[pl.BlockSpec((tm, D), lambda i, j: (i, 0)), pl.BlockSpec((tm, D), lambda i, j: (i, 0)),
                  pl.BlockSpec((tm, tf), lambda i, j: (i, j)), _const((6, D)), _const((1, D)),
                  pl.BlockSpec((None, D, tf), lambda i, j: (j, 0, 0)), pl.BlockSpec((None, tf, D), lambda i, j: (j, 0, 0))],
        out_specs=[pl.BlockSpec((tm, tf), lambda i, j: (i, j)), pl.BlockSpec((tm, D), lambda i, j: (i, 0)), _const((8, D))],
        out_shape=[jax.ShapeDtypeStruct((S, nj * tf), BF16), jax.ShapeDtypeStruct((S, D), F32), jax.ShapeDtypeStruct((8, D), F32)],
        scratch_shapes=[pltpu.VMEM((tm, D), BF16), pltpu.VMEM((tm, D), F32)],
        compiler_params=_params(2), name="mlp_bwd",
    )(dx2, x1, hp, modv, n2g, wmi, wmo)


def _tn_matmul(a, b, tk, tn, name, relu_sq=False, dev_major=False):
    S, K = a.shape
    N = b.shape[1]
    ts = min(1024, S)
    ns = S // ts

    def body(a_ref, b_ref, o_ref):
        @pl.when(pl.program_id(2) == 0)
        def _():
            o_ref[...] = jnp.zeros_like(o_ref)

        at = a_ref[...]
        if relu_sq:
            at = jnp.square(jnp.maximum(at.astype(F32), 0.0)).astype(BF16)
        o_ref[...] += lax.dot_general(at, b_ref[...].astype(BF16), TN, preferred_element_type=F32)

    if dev_major:
        out_spec = pl.BlockSpec((None, tk, tn), lambda k, n, s: (n, k, 0))
        out_shape = jax.ShapeDtypeStruct((N // tn, K, tn), F32)
    else:
        out_spec = pl.BlockSpec((tk, tn), lambda k, n, s: (k, n))
        out_shape = jax.ShapeDtypeStruct((K, N), F32)
    return pl.pallas_call(
        body, grid=(K // tk, N // tn, ns),
        in_specs=[pl.BlockSpec((ts, tk), lambda k, n, s: (s, k)), pl.BlockSpec((ts, tn), lambda k, n, s: (s, n))],
        out_specs=out_spec, out_shape=out_shape,
        compiler_params=_params(3), name=name,
    )(a, b)


def _scale_gate(m, w, g, row, name):
    K = m.shape[0]
    tk = min(512, K)

    def body(m_ref, w_ref, mod_ref, dw_ref, dg_ref):
        @pl.when(pl.program_id(0) == 0)
        def _():
            dg_ref[...] = jnp.zeros_like(dg_ref)

        mt = m_ref[...]
        dw_ref[...] = mt * mod_ref[row:row + 1, :]
        dg_ref[0:1, :] += jnp.sum(mt * w_ref[...].astype(F32), axis=0, keepdims=True)

    return pl.pallas_call(
        body, grid=(K // tk,),
        in_specs=[_rows(tk, D), _rows(tk, D), _const((6, D))],
        out_specs=[_rows(tk, D), _const((8, D))],
        out_shape=[jax.ShapeDtypeStruct((K, D), F32), jax.ShapeDtypeStruct((8, D), F32)],
        compiler_params=_params(1), name=name,
    )(m, w, g)


def _merge_bwd(dx1, modv, ga, gb, ua, ub, ya, yb, wb, wout):
    S = dx1.shape[0]
    tm = min(256, S)

    def body(dx1_ref, mod_ref, ga_ref, gb_ref, ua_ref, ub_ref, ya_ref, yb_ref, wb_ref, wo_ref,
             dua_ref, dub_ref, dga_ref, dgb_ref, dya_ref, dyb_ref, dla_ref, dlb_ref, dyat_ref):
        dao = (mod_ref[2:3, :] * dx1_ref[...]).astype(BF16)
        dm = lax.dot_general(dao, wo_ref[...], NT, preferred_element_type=F32)
        r = lax.broadcasted_iota(jnp.int32, (BW, NH), 0) // HD
        c = lax.broadcasted_iota(jnp.int32, (BW, NH), 1)
        head_of = (r == c).astype(F32)
        for br, (g_ref, u_ref, y_ref, du_ref, dg_ref, dy_ref, dl_ref) in enumerate((
                (ga_ref, ua_ref, ya_ref, dua_ref, dga_ref, dya_ref, dla_ref),
                (gb_ref, ub_ref, yb_ref, dub_ref, dgb_ref, dyb_ref, dlb_ref))):
            sg = jax.nn.sigmoid(g_ref[...].astype(F32))
            du = (dm * sg).astype(BF16)
            du_ref[...] = du
            dg_ref[...] = (dm * u_ref[...].astype(F32) * sg * (1.0 - sg)).astype(BF16)
            dy = lax.dot_general(du, wb_ref[br], NT, preferred_element_type=F32)
            dyb16 = dy.astype(BF16)
            dy_ref[...] = dyb16
            if br == 0:
                dyat_ref[...] = dy.T.astype(BF16)
            dl_ref[...] = jnp.dot(dyb16.astype(F32) * y_ref[...].astype(F32), head_of,
                                  preferred_element_type=F32, precision=lax.Precision.HIGHEST)

    return pl.pallas_call(
        body, grid=(S // tm,),
        in_specs=[_rows(tm, D), _const((6, D)), _rows(tm, D), _rows(tm, D), _rows(tm, D), _rows(tm, D),
                  _rows(tm, BW), _rows(tm, BW), _const((2, BW, D)), _const((D, D))],
        out_specs=[_rows(tm, D)] * 4 + [_rows(tm, BW)] * 2 + [_rows(tm, NH)] * 2 + [pl.BlockSpec((BW, tm), lambda i: (0, i))],
        out_shape=[jax.ShapeDtypeStruct((S, D), BF16)] * 4 + [jax.ShapeDtypeStruct((S, BW), BF16)] * 2
        + [jax.ShapeDtypeStruct((S, NH), F32)] * 2 + [jax.ShapeDtypeStruct((BW, S), BF16)],
        compiler_params=_params(1), name="merge_bwd",
    )(dx1, modv, ga, gb, ua, ub, ya, yb, wb, wout)


def _attn_a_bwd(q, qt, k, kt3, v, do, dot_, lse, delta, chunks):
    S = q.shape[0]
    tq = min(512, S)
    tk = min(512, S)
    nq, nk = S // tq, S // tk

    nc = len(chunks)

    def body(q_ref, qt_ref, do_ref, dot_ref, lse_ref, dl_ref, k_ref, v_ref, kt_ref, *rest):
        g_hbm, (dq_ref, dk_hbm, dv_hbm), recv_hbm = rest[:nc], rest[nc:nc + 3], rest[nc + 3:2 * nc + 3]
        dk_sc, dv_sc, sem, send_sems, recv_sems = rest[2 * nc + 3:]
        i = pl.program_id(0)
        start, finish = _exchange_scatter(g_hbm, recv_hbm, send_sems, recv_sems)

        @pl.when(i == 0)
        def _():
            start()
            dk_sc[...] = jnp.zeros_like(dk_sc)
            dv_sc[...] = jnp.zeros_like(dv_sc)

        for kv in range(2):
            qg = q_ref[:, 256 * kv:256 * kv + 256]
            dog = do_ref[:, 256 * kv:256 * kv + 256]
            heads = []
            for t in range(GRP):
                h = GRP * kv + t
                heads.append((qg[:, HD * t:HD * t + HD], dog[:, HD * t:HD * t + HD],
                              qt_ref[HD * h:HD * h + HD, :], dot_ref[HD * h:HD * h + HD, :],
                              lse_ref[:, h:h + 1], dl_ref[:, h:h + 1]))

            def step(j, carry, kv=kv, heads=heads):
                off = pl.multiple_of(j * tk, tk)
                kj = k_ref[pl.ds(off, tk), :][:, HD * kv:HD * kv + HD]
                vj = v_ref[pl.ds(off, tk), :][:, HD * kv:HD * kv + HD]
                kjt = kt_ref[j, HD * kv:HD * kv + HD, :]
                dkt = jnp.zeros((HD, tk), F32)
                dvt = jnp.zeros((HD, tk), F32)
                new = []

                def logits(t):
                    return (lax.dot_general(heads[t][0], kj, NT, preferred_element_type=F32),
                            lax.dot_general(heads[t][1], vj, NT, preferred_element_type=F32))

                sd = [logits(0)]
                for t, (qh, doh, qth, doth, lse_h, dl_h) in enumerate(heads):
                    if t + 1 < GRP:
                        sd.append(logits(t + 1))
                    s, dp = sd[t]
                    pm = jnp.exp2(s - lse_h)
                    ds = (pm * (dp - dl_h)).astype(BF16)
                    dvt = dvt + jnp.dot(doth, pm.astype(BF16), preferred_element_type=F32)
                    dkt = dkt + jnp.dot(qth, ds, preferred_element_type=F32)
                    new.append(carry[t] + lax.dot_general(kjt, ds, NT, preferred_element_type=F32))
                dk_sc[j, HD * kv:HD * kv + HD, :] += dkt
                dv_sc[j, HD * kv:HD * kv + HD, :] += dvt
                return tuple(new)

            res = lax.fori_loop(0, nk, step, (jnp.zeros((HD, tq), F32),) * GRP)
            for t in range(GRP):
                dq_ref[HD * (GRP * kv + t):HD * (GRP * kv + t) + HD, :] = res[t]

        @pl.when(i == nq - 1)
        def _():
            c1 = pltpu.make_async_copy(dk_sc, dk_hbm, sem.at[0])
            c2 = pltpu.make_async_copy(dv_sc, dv_hbm, sem.at[1])
            c1.start()
            c2.start()
            c1.wait()
            c2.wait()
            finish()

    any_spec = pl.BlockSpec(memory_space=pl.ANY)
    cols = pl.BlockSpec((BW, tq), lambda i: (0, i))
    return pl.pallas_call(
        body, grid=(nq,),
        in_specs=[_rows(tq, BW), cols, _rows(tq, BW), cols, _rows(tq, NH), _rows(tq, NH), _const((S, 128)), _const((S, 128)),
                  _const((nk, 128, tk))] + [any_spec] * nc,
        out_specs=[cols, any_spec, any_spec] + [any_spec] * nc,
        out_shape=[jax.ShapeDtypeStruct((BW, S), F32), jax.ShapeDtypeStruct((nk, 128, tk), F32),
                   jax.ShapeDtypeStruct((nk, 128, tk), F32)]
        + [jax.ShapeDtypeStruct((N_DEV - 1,) + c.shape[1:], c.dtype) for c in chunks],
        scratch_shapes=[pltpu.VMEM((nk, 128, tk), F32), pltpu.VMEM((nk, 128, tk), F32), pltpu.SemaphoreType.DMA((2,))]
        + _exchange_sems(nc),
        compiler_params=_params(1), name="attn_a_bwd",
    )(q, qt, do, dot_, lse, delta, k, v, kt3, *chunks)


def _attn_b_bwd(q, kp, vp, sink2, do, lse, delta):
    S = q.shape[0]
    tq = min(TQ_B, S)
    W = tq + 2 * WIN
    nq = S // tq
    nc = (S + 2 * WIN) // tq

    def body(q_ref, k_ref, v_ref, sink_ref, do_ref, lse_ref, dl_ref, dq_ref, dk_hbm, dv_hbm, ds_ref, dk_sc, dv_sc, sem):
        i = pl.program_id(0)

        @pl.when(i == 0)
        def _():
            dk_sc[...] = jnp.zeros_like(dk_sc)
            dv_sc[...] = jnp.zeros_like(dv_sc)
            ds_ref[...] = jnp.zeros_like(ds_ref)

        off = pl.multiple_of(i * tq, tq)
        valid = _window_mask(i, tq, S)
        kw = k_ref[pl.ds(off, W), :]
        vw = v_ref[pl.ds(off, W), :]
        lse_i = lse_ref[...]
        dl_i = dl_ref[...]
        qa = q_ref[...]
        doa = do_ref[...]
        qt = qa.astype(F32).T.astype(BF16)
        dot_ = doa.astype(F32).T.astype(BF16)
        khs = [kw[:, HD * kv:HD * kv + HD] for kv in range(2)]
        vhs = [vw[:, HD * kv:HD * kv + HD] for kv in range(2)]

        def logits(h):
            return (lax.dot_general(qa[:, HD * h:HD * h + HD], khs[h // GRP], NT, preferred_element_type=F32),
                    lax.dot_general(doa[:, HD * h:HD * h + HD], vhs[h // GRP], NT, preferred_element_type=F32))

        sd = [logits(0)]
        dqs = []
        dkt = [jnp.zeros((HD, W), F32), jnp.zeros((HD, W), F32)]
        dvt = [jnp.zeros((HD, W), F32), jnp.zeros((HD, W), F32)]
        for h in range(NH):
            kv = h // GRP
            if h + 1 < NH:
                sd.append(logits(h + 1))
            s, dp = sd[h]
            pm = jnp.exp2(jnp.where(valid, s, NEG) - lse_i[:, h:h + 1])
            ds = (pm * (dp - dl_i[:, h:h + 1])).astype(BF16)
            dvt[kv] = dvt[kv] + jnp.dot(dot_[HD * h:HD * h + HD, :], pm.astype(BF16), preferred_element_type=F32)
            dkt[kv] = dkt[kv] + jnp.dot(qt[HD * h:HD * h + HD, :], ds, preferred_element_type=F32)
            dqs.append(jnp.dot(ds, khs[kv], preferred_element_type=F32))
        for p in range(4):
            dq_ref[:, 128 * p:128 * p + 128] = jnp.concatenate(dqs[2 * p:2 * p + 2], axis=1)
        for half in range(W // tq):
            dk_sc[i + half] += jnp.concatenate([d[:, tq * half:tq * half + tq] for d in dkt], axis=0)
            dv_sc[i + half] += jnp.concatenate([d[:, tq * half:tq * half + tq] for d in dvt], axis=0)
        psd = jnp.exp2(sink_ref[...] - lse_i) * dl_i
        r = lax.broadcasted_iota(jnp.int32, (NH, 128), 0)
        c = lax.broadcasted_iota(jnp.int32, (NH, 128), 1)
        row = jnp.dot(jnp.sum(psd, axis=0, keepdims=True), (r == c).astype(F32),
                      preferred_element_type=F32, precision=lax.Precision.HIGHEST)
        ds_ref[...] -= jnp.broadcast_to(row, (8, 128))

        @pl.when(i == nq - 1)
        def _():
            c1 = pltpu.make_async_copy(dk_sc, dk_hbm, sem.at[0])
            c2 = pltpu.make_async_copy(dv_sc, dv_hbm, sem.at[1])
            c1.start()
            c2.start()
            c1.wait()
            c2.wait()

    any_spec = pl.BlockSpec(memory_space=pl.ANY)
    return pl.pallas_call(
        body, grid=(nq,),
        in_specs=[_rows(tq, BW), _const((S + 2 * WIN, 128)), _const((S + 2 * WIN, 128)), _const((1, NH)),
                  _rows(tq, BW), _rows(tq, NH), _rows(tq, NH)],
        out_specs=[_rows(tq, BW), any_spec, any_spec, _const((8, 128))],
        out_shape=[jax.ShapeDtypeStruct((S, BW), F32), jax.ShapeDtypeStruct((nc, 128, tq), F32),
                   jax.ShapeDtypeStruct((nc, 128, tq), F32), jax.ShapeDtypeStruct((8, 128), F32)],
        scratch_shapes=[pltpu.VMEM((nc, 128, tq), F32), pltpu.VMEM((nc, 128, tq), F32), pltpu.SemaphoreType.DMA((2,))],
        compiler_params=_params(1), name="attn_b_bwd",
    )(q, kp, vp, sink2, do, lse, delta)


def _qk_bwd(dqa_t, dka_t3, dva_t3, dqb, dkb, dvb, qar, kar, qg2, kg2, tab_a, tab_b, dga, dgb):
    S = dqb.shape[0]
    tm = min(256, S)
    per = dka_t3.shape[2] // tm

    def body(dqa_ref, dka_ref, dva_ref, dqb_ref, dkb_ref, dvb_ref, qar_ref, kar_ref, qg_ref, kg_ref, ta_ref, tb_ref,
             dga_ref, dgb_ref, dp_ref, st_ref):
        @pl.when(pl.program_id(0) == 0)
        def _():
            st_ref[...] = jnp.zeros_like(st_ref)

        seg = _seg_matrix(128, HD)

        def norm_bwd(dz_rot, raw, g):
            dzn = _rope_t(dz_rot, ta_ref, 16)
            raw = raw.astype(F32)
            rr = lax.rsqrt(_seg_sum(raw * raw, seg) * (1.0 / HD) + EPS)
            zhat = raw * rr
            dzh = dzn * g
            draw = rr * (dzh - zhat * (_seg_sum(dzh * zhat, seg) * (1.0 / HD)))
            return draw, jnp.sum(dzn * zhat, axis=0, keepdims=True)

        gq = jnp.zeros((1, 128), F32)
        for p in range(4):
            sl = slice(128 * p, 128 * p + 128)
            draw, gsum = norm_bwd(dqa_ref[sl, :].T * 0.125, qar_ref[:, sl], qg_ref[...])
            gq = gq + gsum
            dp_ref[:, sl] = draw.astype(BF16)
            dp_ref[:, 768 + 128 * p:768 + 128 * p + 128] = _rope_t(dqb_ref[:, sl] * 0.125, tb_ref, 32).astype(BF16)
        draw, gk = norm_bwd(dka_ref[...].T * LN2, kar_ref[...], kg_ref[...])
        dp_ref[:, 512:640] = draw.astype(BF16)
        dp_ref[:, 640:768] = dva_ref[...].T.astype(BF16)
        dp_ref[:, 1280:1408] = _rope_t(dkb_ref[...] * LN2, tb_ref, 32).astype(BF16)
        dp_ref[:, 1408:1536] = dvb_ref[...].astype(BF16)
        dp_ref[:, 1536:2560] = dga_ref[...]
        dp_ref[:, 2560:3584] = dgb_ref[...]
        st_ref[0:1, :] += gq
        st_ref[1:2, :] += gk

    tab = pl.BlockSpec((3, tm, 128), lambda i: (0, i, 0))
    chunk_t = pl.BlockSpec((None, 128, tm), lambda i: (i // per, 0, i % per))
    return pl.pallas_call(
        body, grid=(S // tm,),
        in_specs=[pl.BlockSpec((BW, tm), lambda i: (0, i)), chunk_t, chunk_t, _rows(tm, BW), _rows(tm, 128), _rows(tm, 128),
                  _rows(tm, BW), _rows(tm, 128), _const((1, 128)), _const((1, 128)), tab, tab, _rows(tm, D), _rows(tm, D)],
        out_specs=[_rows(tm, INW), _const((8, 128))],
        out_shape=[jax.ShapeDtypeStruct((S, INW), BF16), jax.ShapeDtypeStruct((8, 128), F32)],
        compiler_params=_params(1), name="qk_bwd",
    )(dqa_t, dka_t3, dva_t3, dqb, dkb, dvb, qar, kar, qg2, kg2, tab_a, tab_b, dga, dgb)


def _in_bwd(dproj, win, x, dx1, modv, n1g, chunks):
    S = x.shape[0]
    tm = min(256, S)
    n = S // tm

    def body(dp_ref, w_ref, x_ref, dx1_ref, mod_ref, g_ref, c_hbm, gx_ref, st_ref, recv_hbm, send_sems, recv_sems):
        start, finish = _exchange_scatter([c_hbm], [recv_hbm], send_sems, recv_sems)

        @pl.when(pl.program_id(0) == 0)
        def _():
            start()
            st_ref[...] = jnp.zeros_like(st_ref)

        dh = lax.dot_general(dp_ref[...], w_ref[...], NT, preferred_element_type=F32)
        xt = x_ref[...]
        r = lax.rsqrt(jnp.mean(xt * xt, axis=-1, keepdims=True) + EPS)
        xn = xt * r
        st_ref[0:1, :] += jnp.sum(dh, axis=0, keepdims=True)
        st_ref[1:2, :] += jnp.sum(dh * xn, axis=0, keepdims=True)
        dxn = dh * (g_ref[...] * (1.0 + mod_ref[1:2, :]))
        gx_ref[...] = dx1_ref[...] + r * (dxn - xn * jnp.mean(dxn * xn, axis=-1, keepdims=True))
        pl.when(pl.program_id(0) == n - 1)(finish)

    any_spec = pl.BlockSpec(memory_space=pl.ANY)
    return pl.pallas_call(
        body, grid=(n,),
        in_specs=[_rows(tm, INW), _const((D, INW)), _rows(tm, D), _rows(tm, D), _const((6, D)), _const((1, D)), any_spec],
        out_specs=[_rows(tm, D), _const((8, D)), any_spec],
        out_shape=[jax.ShapeDtypeStruct((S, D), F32), jax.ShapeDtypeStruct((8, D), F32),
                   jax.ShapeDtypeStruct((N_DEV - 1,) + chunks.shape[1:], chunks.dtype)],
        scratch_shapes=_exchange_sems(1),
        compiler_params=_params(1), name="in_bwd",
    )(dproj, win, x, dx1, modv, n1g, chunks)


def _pack_small(st1, st2, stf, dg1, dg2, stqk, dsink, modv, n1g, n2g):
    def body(st1_ref, st2_ref, stf_ref, dg1_ref, dg2_ref, qk_ref, ds_ref, mod_ref, g1_ref, g2_ref, o_ref):
        a1, b1 = st1_ref[0:1, :], st1_ref[1:2, :]
        a2, b2 = st2_ref[0:1, :], st2_ref[1:2, :]
        r = lax.broadcasted_iota(jnp.int32, (128, D), 0)
        c = lax.broadcasted_iota(jnp.int32, (128, D), 1)
        fold_q = (c == r % HD).astype(F32)
        fold_k = (c == HD + r % HD).astype(F32)
        keep = (c == r).astype(F32)

        def place(v, sel):
            return jnp.dot(v, sel, preferred_element_type=F32, precision=lax.Precision.HIGHEST)

        loss = jnp.sum(stf_ref[1:2, :], axis=1, keepdims=True)
        lane = lax.broadcasted_iota(jnp.int32, (1, D), 1)
        rows = [a1, g1_ref[...] * b1, dg1_ref[0:1, :], a2, g2_ref[...] * b2, dg2_ref[0:1, :],
                (1.0 + mod_ref[1:2, :]) * b1, (1.0 + mod_ref[4:5, :]) * b2, stf_ref[0:1, :],
                place(qk_ref[0:1, :], fold_q) + place(qk_ref[1:2, :], fold_k),
                place(ds_ref[0:1, :], keep),
                jnp.where(lane == 0, loss, 0.0)]
        rows += [jnp.zeros((1, D), F32)] * (SMALL_ROWS - len(rows))
        for n, v in enumerate(rows):
            o_ref[n:n + 1, :] = v

    return pl.pallas_call(
        body, out_shape=jax.ShapeDtypeStruct((SMALL_ROWS, D), F32),
        compiler_params=pltpu.CompilerParams(vmem_limit_bytes=V7X_VMEM_LIMIT), name="pack_small",
    )(st1, st2, stf, dg1, dg2, stqk, dsink, modv, n1g, n2g)


def _wada_grad(silu_all, dmod_cols):
    def body(a_ref, b_ref, o_ref):
        o_ref[...] = lax.dot_general(a_ref[...], b_ref[...], TN, preferred_element_type=F32, precision=lax.Precision.HIGHEST)

    return pl.pallas_call(
        body, out_shape=jax.ShapeDtypeStruct((D, dmod_cols.shape[1]), F32),
        compiler_params=pltpu.CompilerParams(vmem_limit_bytes=V7X_VMEM_LIMIT), name="wada_grad",
    )(silu_all, dmod_cols)


def _adamw_sum(parts, w, m, v, name):
    R, C = w.shape
    tr = R if R <= 64 else next(t for t in (256, 128, 64, 32, 16, 8) if R % t == 0)
    n = len(parts)
    dyn = [idx for _, idx in parts if idx is not None and not isinstance(idx, int)]
    b1c = 1.0 - ADAM_B1 ** ADAM_STEP
    b2c = 1.0 - ADAM_B2 ** ADAM_STEP

    def body(*refs):
        refs = refs[len(dyn):]
        g = refs[0][...].astype(F32)
        for k in range(1, n):
            g = g + refs[k][...].astype(F32)
        w_ref, m_ref, v_ref, g_out, d_out, m_out, v_out = refs[n:]
        mn = ADAM_B1 * m_ref[...] + (1.0 - ADAM_B1) * g
        vn = ADAM_B2 * v_ref[...] + (1.0 - ADAM_B2) * jnp.square(g)
        g_out[...] = g
        m_out[...] = mn
        v_out[...] = vn
        d_out[...] = -ADAM_LR * ((mn / b1c) / (jnp.sqrt(vn / b2c) + ADAM_EPS) + ADAM_WD * w_ref[...])

    in_specs = []
    nd = 0
    for a, idx in parts:
        if idx is None:
            in_specs.append(pl.BlockSpec((tr, C), lambda i, *s: (i, 0)))
        elif isinstance(idx, int):
            in_specs.append(pl.BlockSpec((None, tr, C), lambda i, *s, idx=idx: (idx, i, 0)))
        else:
            in_specs.append(pl.BlockSpec((None, tr, C), lambda i, *s, nd=nd: (s[nd][0], i, 0)))
            nd += 1
    blk = pl.BlockSpec((tr, C), lambda i, *s: (i, 0))
    grid_spec = pltpu.PrefetchScalarGridSpec(
        num_scalar_prefetch=len(dyn), grid=(R // tr,), in_specs=in_specs + [blk] * 3, out_specs=[blk] * 4)
    return pl.pallas_call(
        body, grid_spec=grid_spec, out_shape=[jax.ShapeDtypeStruct((R, C), F32)] * 4,
        compiler_params=_params(1), name=name,
    )(*dyn, *[a for a, _ in parts], w, m, v)


def _me():
    return lax.axis_index("x"), lax.axis_index("y"), lax.axis_index("c")


def _peer(k):
    x, y, c = _me()
    return (x ^ ((k >> 2) & 1), y ^ ((k >> 1) & 1), c ^ (k & 1))


def _ada_exchange(c_row, w_ada, b_rows):
    NW = w_ada.shape[1]

    def body(c_ref, w_ref, b_ref, sall_ref, mod_ref, src_ref, mp_ref, send1, recv1, send2, recv2):
        x, y, c = _me()
        me = 4 * x + 2 * y + c
        cv = c_ref[...]
        src_ref[...] = jnp.broadcast_to(cv * jax.nn.sigmoid(cv), (8, D))
        mine = pl.ds(pl.multiple_of(me * 8, 8), 8)
        sall_ref[mine, :] = src_ref[...]
        sends = [pltpu.make_async_remote_copy(src_ref, sall_ref.at[mine, :], send1.at[k - 1], recv1.at[k - 1],
                                              device_id=_peer(k), device_id_type=MESH) for k in range(1, N_DEV)]
        for cp in sends:
            cp.start()
        for k in range(1, N_DEV):
            theirs = pl.ds(pl.multiple_of((me ^ k) * 8, 8), 8)
            pltpu.make_async_remote_copy(src_ref, sall_ref.at[theirs, :], send1.at[k - 1], recv1.at[k - 1],
                                         device_id=_peer(k), device_id_type=MESH).wait_recv()
        for cp in sends:
            cp.wait_send()
        mp_ref[...] = jnp.dot(sall_ref[...], w_ref[...], preferred_element_type=F32, precision=lax.Precision.HIGHEST)
        mod_ref[mine, :] = mp_ref[mine, :] + b_ref[mine, :]
        sends = []
        for k in range(1, N_DEV):
            theirs = pl.ds(pl.multiple_of((me ^ k) * 8, 8), 8)
            sends.append(pltpu.make_async_remote_copy(mp_ref.at[theirs, :], mod_ref.at[mine, :], send2.at[k - 1], recv2.at[k - 1],
                                                      device_id=_peer(k), device_id_type=MESH))
        for cp in sends:
            cp.start()
        for k in range(1, N_DEV):
            theirs = pl.ds(pl.multiple_of((me ^ k) * 8, 8), 8)
            pltpu.make_async_remote_copy(mp_ref.at[mine, :], mod_ref.at[theirs, :], send2.at[k - 1], recv2.at[k - 1],
                                         device_id=_peer(k), device_id_type=MESH).wait_recv()
            mod_ref[theirs, :] = mod_ref[theirs, :] + b_ref[theirs, :]
        for cp in sends:
            cp.wait_send()

    vm = pl.BlockSpec(memory_space=pltpu.VMEM)
    return pl.pallas_call(
        body, in_specs=[vm, vm, vm], out_specs=[vm, vm],
        out_shape=[jax.ShapeDtypeStruct((8 * N_DEV, D), F32), jax.ShapeDtypeStruct((8 * N_DEV, NW), F32)],
        scratch_shapes=[pltpu.VMEM((8, D), F32), pltpu.VMEM((8 * N_DEV, NW), F32)]
        + [pltpu.SemaphoreType.DMA((N_DEV - 1,))] * 4,
        compiler_params=pltpu.CompilerParams(vmem_limit_bytes=V7X_VMEM_LIMIT), name="ada_exchange",
    )(c_row, w_ada, b_rows)


def _weight_gather(shard):
    def body(x_ref, out_ref, send_sems, recv_sems, local_sem):
        x, y, c = _me()
        me, sibling = (x, y, c), (x, y, 1 - c)
        chips = [(1 - x, y), (x, 1 - y), (1 - x, 1 - y)]

        def slot(px, py, pc):
            return out_ref.at[4 * px + 2 * py + pc]

        def copy(k, block, to, src=None):
            return pltpu.make_async_remote_copy(
                src_ref=slot(*block) if src is None else src, dst_ref=slot(*block),
                send_sem=send_sems.at[k], recv_sem=recv_sems.at[k], device_id=to, device_id_type=MESH)

        mine = pltpu.make_async_copy(x_ref, slot(*me), local_sem)
        mine.start()
        first = [copy(0, me, sibling, src=x_ref)]
        first += [copy(1 + j, me, (*chip, c), src=x_ref) for j, chip in enumerate(chips)]
        for cp in first:
            cp.start()
        passed = [copy(4 + j, (*chip, c), sibling) for j, chip in enumerate(chips)]
        for j, chip in enumerate(chips):
            copy(1 + j, (*chip, c), me).wait_recv()
            passed[j].start()
        copy(0, sibling, me).wait_recv()
        for j, chip in enumerate(chips):
            copy(4 + j, (*chip, 1 - c), me).wait_recv()
        for cp in first + passed:
            cp.wait_send()
        mine.wait()

    any_spec = pl.BlockSpec(memory_space=pl.ANY)
    return pl.pallas_call(
        body, in_specs=[any_spec], out_specs=any_spec,
        out_shape=jax.ShapeDtypeStruct((N_DEV,) + shard.shape, shard.dtype),
        scratch_shapes=[pltpu.SemaphoreType.DMA((7,)), pltpu.SemaphoreType.DMA((7,)), pltpu.SemaphoreType.DMA],
        name="weight_gather",
    )(shard)


def _small_gather(block):
    def body(b_ref, out_ref, send_sems, recv_sems):
        x, y, c = _me()
        me = 4 * x + 2 * y + c
        out_ref[me] = b_ref[...]
        sends = [pltpu.make_async_remote_copy(b_ref, out_ref.at[me], send_sems.at[k - 1], recv_sems.at[k - 1],
                                              device_id=_peer(k), device_id_type=MESH) for k in range(1, N_DEV)]
        for cp in sends:
            cp.start()
        for k in range(1, N_DEV):
            pltpu.make_async_remote_copy(b_ref, out_ref.at[me ^ k], send_sems.at[k - 1], recv_sems.at[k - 1],
                                         device_id=_peer(k), device_id_type=MESH).wait_recv()
        for cp in sends:
            cp.wait_send()

    vm = pl.BlockSpec(memory_space=pltpu.VMEM)
    return pl.pallas_call(
        body, in_specs=[vm], out_specs=vm,
        out_shape=jax.ShapeDtypeStruct((N_DEV,) + block.shape, block.dtype),
        scratch_shapes=[pltpu.SemaphoreType.DMA((N_DEV - 1,)), pltpu.SemaphoreType.DMA((N_DEV - 1,))],
        name="small_gather",
    )(block)


def _pack_small_params(b_ada, n1, n2, fg, qn, kn, sink):
    z = jnp.zeros((SMALL_ROWS, D), F32)
    z = z.at[0:6].set(b_ada.reshape(6, D)).at[6].set(n1.reshape(D)).at[7].set(n2.reshape(D)).at[8].set(fg.reshape(D))
    z = z.at[9, 0:HD].set(qn.reshape(HD)).at[9, HD:2 * HD].set(kn.reshape(HD)).at[10, 0:NH].set(sink.reshape(NH))
    return z


def _unpack_small(p):
    return (p[0:6].reshape(1, 6 * D), p[6].reshape(1, D), p[9, 0:HD].reshape(1, HD), p[9, HD:2 * HD].reshape(1, HD),
            p[10, 0:NH].reshape(1, NH), p[7].reshape(1, D), p[8].reshape(D))


def kernel(x, c, w_ada, b_ada, norm1_g, w_in, q_norm_a, k_norm_a, sink_b, w_branch, w_out, norm2_g, w_mlp_in, w_mlp_out, final_g, loss_target, m_w_ada, m_b_ada, m_norm1_g, m_w_in, m_q_norm_a, m_k_norm_a, m_sink_b, m_w_branch, m_w_out, m_norm2_g, m_w_mlp_in, m_w_mlp_out, m_final_g, v_w_ada, v_b_ada, v_norm1_g, v_w_in, v_q_norm_a, v_k_norm_a, v_sink_b, v_w_branch, v_w_out, v_norm2_g, v_w_mlp_in, v_w_mlp_out, v_final_g):
    S = x.shape[1]
    xs = x.reshape(S, D)
    tgt = loss_target.reshape(S, D)
    ax, ay, ac = lax.axis_index("x"), lax.axis_index("y"), lax.axis_index("c")
    me = 4 * ax + 2 * ay + ac
    me1 = me.reshape(1).astype(jnp.int32)
    NW = w_ada.shape[2]
    NI = w_in.shape[2]

    silu64, mod64 = _ada_exchange(c.reshape(1, D), w_ada.reshape(D, NW),
                                  jnp.repeat(b_ada.reshape(N_DEV, NW), 8, axis=0))
    silu_all = silu64[0::8]
    modv = mod64[0::8].reshape(6, D)

    win = _weight_gather(w_in[0].astype(BF16)).transpose(1, 0, 2).reshape(D, INW)
    rest_shards = tuple(w[0].astype(BF16) for w in (w_branch, w_out, w_mlp_in, w_mlp_out))

    tab_a, tab_b = _rope_tables(S)
    qg2 = jnp.tile(q_norm_a.reshape(1, HD), (1, 2))
    kg2 = jnp.tile(k_norm_a.reshape(1, HD), (1, 2))
    n1g = norm1_g.reshape(1, D)
    n2g = norm2_g.reshape(1, D)
    fg = final_g.reshape(1, D)
    sink2 = sink_b.reshape(1, NH) * LOG2E

    h, qar, kar, qa, ka, va, qb, kb, vb, ga, gb, qa_t, ka_t3, va_t3 = _in_proj(xs, modv, n1g, win, qg2, kg2, tab_a, tab_b)
    ya, lse_at, wb, wout, wmi, wmo = _attn_a_fwd(qa, ka, va_t3, rest_shards)
    lse_a = lse_at.T
    wb = wb.transpose(1, 2, 0, 3).reshape(2, BW, D)
    wout = wout.reshape(D, D)
    pad = ((WIN, WIN), (0, 0))
    kbp, vbp = jnp.pad(kb, pad), jnp.pad(vb, pad)
    tb = min(TQ_B, S)
    yb, lse_bt = _attn_b_fwd(qb, kbp, vbp.reshape((S + 2 * WIN) // tb, tb, 128).transpose(0, 2, 1), sink2)
    lse_b = lse_bt.T
    x1, merged, ua, ub = _merge_out(ya, yb, ga, gb, xs, modv, wb, wout)
    h2, hp, dx2, stf = _mlp_fwd(x1, modv, n2g, wmi, wmo, fg, tgt)

    dhp, dx1, st2 = _mlp_bwd(dx2, x1, hp, modv, n2g, wmi, wmo)
    m2 = _tn_matmul(hp, dx2, 2048, D, "dw_mlp_out", relu_sq=True)
    g_wmo, dg2 = _scale_gate(m2, wmo.reshape(FF, D), modv, 5, "gate2_grad")
    g_wmi = _tn_matmul(h2, dhp, D, 512, "dw_mlp_in", dev_major=True)
    dua, dub, dga, dgb, dya, dyb, dl_a, dl_b, dya_t = _merge_bwd(dx1, modv, ga, gb, ua, ub, ya, yb, wb, wout)
    m1 = _tn_matmul(merged, dx1, D, D, "dw_out")
    g_wout, dg1 = _scale_gate(m1, wout, modv, 2, "gate1_grad")
    g_wb0 = _tn_matmul(ya, dua, BW, D, "dw_branch_a")
    g_wb1 = _tn_matmul(yb, dub, BW, D, "dw_branch_b")
    g_wb = jnp.stack([g_wb0, g_wb1]).reshape(2, BW, N_DEV, 128).transpose(2, 0, 1, 3).reshape(N_DEV, 2 * BW, 128)
    g_wout = g_wout.reshape(N_DEV, 128, D)
    g_wmo = g_wmo.reshape(N_DEV, 512, D)
    dqa_t, dka_t3, dva_t3, r_wb, r_wout, r_wmi, r_wmo = _attn_a_bwd(qa, qa_t, ka, ka_t3, va, dya, dya_t, lse_a, dl_a,
                                                                    (g_wb, g_wout, g_wmi, g_wmo))
    dqb, dkb_t, dvb_t, dsink = _attn_b_bwd(qb, kbp, vbp, sink2, dyb, lse_b, dl_b)
    dkb = dkb_t.transpose(0, 2, 1).reshape(S + 2 * WIN, 128)[WIN:WIN + S]
    dvb = dvb_t.transpose(0, 2, 1).reshape(S + 2 * WIN, 128)[WIN:WIN + S]
    dproj, stqk = _qk_bwd(dqa_t, dka_t3, dva_t3, dqb, dkb, dvb, qar, kar, qg2, kg2, tab_a, tab_b, dga, dgb)
    g_win = _tn_matmul(h, dproj, D, 896, "dw_in")
    g_win = g_win.reshape(D, N_DEV, NI).transpose(1, 0, 2)
    grad_x, st1, r_win = _in_bwd(dproj, win, xs, dx1, modv, n1g, g_win.astype(BF16))

    def adam(name, own, recv, w, m, v):
        shape = w.shape
        w2, m2_, v2 = (a.reshape(own.shape[1:]) for a in (w, m, v))
        outs = _adamw_sum([(own, me1)] + [(recv, k) for k in range(N_DEV - 1)], w2, m2_, v2, name)
        return [a.reshape(shape) for a in outs]

    o_win = adam("adamw_w_in", g_win, r_win, w_in, m_w_in, v_w_in)
    o_wb = adam("adamw_w_branch", g_wb, r_wb, w_branch, m_w_branch, v_w_branch)
    o_wout = adam("adamw_w_out", g_wout, r_wout, w_out, m_w_out, v_w_out)
    o_wmi = adam("adamw_w_mlp_in", g_wmi, r_wmi, w_mlp_in, m_w_mlp_in, v_w_mlp_in)
    o_wmo = adam("adamw_w_mlp_out", g_wmo, r_wmo, w_mlp_out, m_w_mlp_out, v_w_mlp_out)

    small = _pack_small(st1, st2, stf, dg1, dg2, stqk, dsink, modv, n1g, n2g)
    small_all = _small_gather(small)
    sw = _pack_small_params(b_ada, norm1_g, norm2_g, final_g, q_norm_a, k_norm_a, sink_b)
    sm = _pack_small_params(m_b_ada, m_norm1_g, m_norm2_g, m_final_g, m_q_norm_a, m_k_norm_a, m_sink_b)
    sv = _pack_small_params(v_b_ada, v_norm1_g, v_norm2_g, v_final_g, v_q_norm_a, v_k_norm_a, v_sink_b)
    sm_out = _adamw_sum([(small_all, k) for k in range(N_DEV)], sw, sm, sv, "adamw_small")
    loss = sm_out[0][11, 0]
    sm_out = [_unpack_small(a) for a in sm_out]

    dmod_all = small_all[:, 0:6, :].reshape(N_DEV, 6 * D)
    dmod_cols = lax.dynamic_slice_in_dim(dmod_all, me * NW, NW, axis=1)
    g_wada = _wada_grad(silu_all, dmod_cols)
    ada = _adamw_sum([(g_wada, None)], w_ada.reshape(D, NW), m_w_ada.reshape(D, NW), v_w_ada.reshape(D, NW), "adamw_ada")
    ada = [a.reshape(1, D, NW) for a in ada]

    def leaves(k):
        b_, n1_, qn_, kn_, sk_, n2_, fg_ = sm_out[k]
        return [ada[k], b_, n1_, o_win[k], qn_, kn_, sk_, o_wb[k], o_wout[k], n2_, o_wmi[k], o_wmo[k], fg_]

    return (loss, grad_x.reshape(1, S, D), *leaves(0), *leaves(1), *leaves(2), *leaves(3))
```

```python
import jax
import jax.numpy as jnp
from jax import lax
from jax.experimental import pallas as pl
from jax.experimental.pallas import tpu as pltpu

F32, BF16 = jnp.float32, jnp.bfloat16
MESH = pl.DeviceIdType.MESH

D = 1024
HD = 64
NH = 8
GRP = 4
BW = 512
FF = 4096
INW = 3584
GRID_W = 64
WIN = 128
THETA = 10000.0
EPS = 1e-6
NEG = -1e30
N_DEV = 8
LOG2E = 1.4426950408889634
LN2 = 0.6931471805599453
QA_SCALE = 0.125 * LOG2E
SMALL_ROWS = 16
MLP_SHARDS = 2
V7X_VMEM_LIMIT = 56 * 1024 * 1024

ADAM_LR, ADAM_B1, ADAM_B2, ADAM_EPS, ADAM_WD, ADAM_STEP = 0.001, 0.9, 0.999, 1e-08, 0.01, 10

NT = (((1,), (1,)), ((), ()))
TN = (((0,), (0,)), ((), ()))


def _params(n_axes, vmem=V7X_VMEM_LIMIT):
    return pltpu.CompilerParams(dimension_semantics=("arbitrary",) * n_axes, vmem_limit_bytes=vmem)


def _const(shape):
    return pl.BlockSpec(shape, lambda *_: (0,) * len(shape))


def _rows(tm, width):
    return pl.BlockSpec((tm, width), lambda i, *_: (i, 0))


def _seg_matrix(n, seg):
    r = lax.broadcasted_iota(jnp.int32, (n, n), 0) // seg
    c = lax.broadcasted_iota(jnp.int32, (n, n), 1) // seg
    return (r == c).astype(F32)


def _seg_sum(z, seg_mat):
    return jnp.dot(z, seg_mat, preferred_element_type=F32, precision=lax.Precision.HIGHEST)


def _rope(z, t_ref, sh):
    return z * t_ref[0] + pltpu.roll(z, sh, 1) * t_ref[1] + pltpu.roll(z, 128 - sh, 1) * t_ref[2]


def _rope_t(dz, t_ref, sh):
    return dz * t_ref[0] + pltpu.roll(dz * t_ref[1], 128 - sh, 1) + pltpu.roll(dz * t_ref[2], sh, 1)


def _rope_tables(S):
    t = jnp.arange(S, dtype=jnp.int32)
    lane = jnp.arange(HD)

    def build(cos, sin, first):
        t0 = cos
        t1 = jnp.where(first[None, :], 0.0, sin)
        t2 = jnp.where(first[None, :], -sin, 0.0)
        return jnp.stack([jnp.tile(a, (1, 2)) for a in (t0, t1, t2)]).astype(F32)

    inv_a = THETA ** (-jnp.arange(0, HD // 2, 2, dtype=F32) / (HD // 2))
    ar = (t // GRID_W).astype(F32)[:, None] * inv_a[None, :]
    ac = (t % GRID_W).astype(F32)[:, None] * inv_a[None, :]
    cos_a = jnp.concatenate([jnp.cos(ar), jnp.cos(ar), jnp.cos(ac), jnp.cos(ac)], axis=1)
    sin_a = jnp.concatenate([jnp.sin(ar), jnp.sin(ar), jnp.sin(ac), jnp.sin(ac)], axis=1)
    tab_a = build(cos_a, sin_a, (lane % 32) < 16)
    inv_b = THETA ** (-jnp.arange(0, HD, 2, dtype=F32) / HD)
    ab = t.astype(F32)[:, None] * inv_b[None, :]
    cos_b = jnp.concatenate([jnp.cos(ab), jnp.cos(ab)], axis=1)
    sin_b = jnp.concatenate([jnp.sin(ab), jnp.sin(ab)], axis=1)
    tab_b = build(cos_b, sin_b, lane < 32)
    return tab_a, tab_b


def _in_proj(x, modv, n1g, win, qg2, kg2, tab_a, tab_b):
    S = x.shape[0]
    tm = min(256, S)
    tk = min(512, S)
    per = tk // tm

    def body(x_ref, mod_ref, g_ref, w_ref, qg_ref, kg_ref, ta_ref, tb_ref,
             h_ref, qar_ref, kar_ref, qa_ref, ka_ref, va_ref, qb_ref, kb_ref, vb_ref, ga_ref, gb_ref, qat_ref, kat_ref, vat_ref):
        xt = x_ref[...]
        r = lax.rsqrt(jnp.mean(xt * xt, axis=-1, keepdims=True) + EPS)
        h = ((xt * r) * g_ref[...]) * (1.0 + mod_ref[1:2, :]) + mod_ref[0:1, :]
        hb = h.astype(BF16)
        h_ref[...] = hb
        proj = jnp.dot(hb, w_ref[...], preferred_element_type=F32)
        seg = _seg_matrix(128, HD)

        def head_norm(z, g):
            ms = _seg_sum(z * z, seg) * (1.0 / HD)
            return (z * lax.rsqrt(ms + EPS)) * g

        for p in range(4):
            z = proj[:, 128 * p:128 * p + 128]
            qar_ref[:, 128 * p:128 * p + 128] = z.astype(BF16)
            qv = _rope(head_norm(z, qg_ref[...]), ta_ref, 16) * QA_SCALE
            qa_ref[:, 128 * p:128 * p + 128] = qv.astype(BF16)
            qat_ref[128 * p:128 * p + 128, :] = qv.T.astype(BF16)
            zb = proj[:, 768 + 128 * p:768 + 128 * p + 128]
            qb_ref[:, 128 * p:128 * p + 128] = (_rope(zb, tb_ref, 32) * QA_SCALE).astype(BF16)
        z = proj[:, 512:640]
        kar_ref[...] = z.astype(BF16)
        kv_ = _rope(head_norm(z, kg_ref[...]), ta_ref, 16)
        ka_ref[...] = kv_.astype(BF16)
        kat_ref[...] = kv_.T.astype(BF16)
        va_ref[...] = proj[:, 640:768].astype(BF16)
        vat_ref[...] = proj[:, 640:768].T.astype(BF16)
        kb_ref[...] = _rope(proj[:, 1280:1408], tb_ref, 32).astype(BF16)
        vb_ref[...] = proj[:, 1408:1536].astype(BF16)
        ga_ref[...] = proj[:, 1536:2560].astype(BF16)
        gb_ref[...] = proj[:, 2560:3584].astype(BF16)

    tab = pl.BlockSpec((3, tm, 128), lambda i: (0, i, 0))
    shapes = [(D, BF16), (BW, BF16), (128, BF16), (BW, BF16), (128, BF16), (128, BF16),
              (BW, BF16), (128, BF16), (128, BF16), (D, BF16), (D, BF16)]
    return pl.pallas_call(
        body, grid=(S // tm,),
        in_specs=[_rows(tm, D), _const((6, D)), _const((1, D)), _const((D, INW)), _const((1, 128)), _const((1, 128)), tab, tab],
        out_specs=[_rows(tm, w) for w, _ in shapes] + [pl.BlockSpec((BW, tm), lambda i: (0, i))]
        + [pl.BlockSpec((None, 128, tm), lambda i: (i // per, 0, i % per))] * 2,
        out_shape=[jax.ShapeDtypeStruct((S, w), dt) for w, dt in shapes] + [jax.ShapeDtypeStruct((BW, S), BF16)]
        + [jax.ShapeDtypeStruct((S // tk, 128, tk), BF16)] * 2,
        compiler_params=_params(1), name="in_proj",
    )(x, modv, n1g, win, qg2, kg2, tab_a, tab_b)


def _exchange_gather(block_refs, out_refs, send_sems, recv_sems, local_sems):
    x, y, c = _me()
    me = 4 * x + 2 * y + c

    def copies():
        own, out, arrive = [], [], []
        for a, (blk, dst) in enumerate(zip(block_refs, out_refs)):
            own.append(pltpu.make_async_copy(blk, dst.at[me], local_sems.at[a]))
            for k in range(1, N_DEV):
                sems = dict(send_sem=send_sems.at[a, k - 1], recv_sem=recv_sems.at[a, k - 1], device_id=_peer(k), device_id_type=MESH)
                out.append(pltpu.make_async_remote_copy(blk, dst.at[me], **sems))
                arrive.append(pltpu.make_async_remote_copy(blk, dst.at[me ^ k], **sems))
        return own, out, arrive

    def start():
        own, out, _ = copies()
        for cp in own + out:
            cp.start()

    def finish():
        own, out, arrive = copies()
        for cp in arrive:
            cp.wait_recv()
        for cp in out:
            cp.wait_send()
        for cp in own:
            cp.wait()

    return start, finish


def _exchange_scatter(chunk_refs, recv_refs, send_sems, recv_sems):
    x, y, c = _me()
    me = 4 * x + 2 * y + c

    def copies():
        return [pltpu.make_async_remote_copy(src.at[me ^ k], dst.at[k - 1], send_sems.at[a, k - 1], recv_sems.at[a, k - 1],
                                             device_id=_peer(k), device_id_type=MESH)
                for a, (src, dst) in enumerate(zip(chunk_refs, recv_refs)) for k in range(1, N_DEV)]

    def start():
        for cp in copies():
            cp.start()

    def finish():
        cps = copies()
        for cp in cps:
            cp.wait_recv()
        for cp in cps:
            cp.wait_send()

    return start, finish


def _exchange_sems(n):
    return [pltpu.SemaphoreType.DMA((n, N_DEV - 1)), pltpu.SemaphoreType.DMA((n, N_DEV - 1))]


def _attn_a_fwd(q, k, vt3, shards):
    S = q.shape[0]
    tq = min(512, S)
    nq = S // tq
    nk, _, tk = vt3.shape
    ONES = 16
    AHEAD = 2
    ns = len(shards)

    def body(q_ref, k_ref, vt_ref, *rest):
        w_hbm, (o_ref, lse_ref), wall_hbm = rest[:ns], rest[ns:ns + 2], rest[ns + 2:2 * ns + 2]
        st_sc, send_sems, recv_sems, local_sems = rest[2 * ns + 2:]
        start, finish = _exchange_gather(w_hbm, wall_hbm, send_sems, recv_sems, local_sems)
        pl.when(pl.program_id(0) == 0)(start)
        row8 = lax.broadcasted_iota(jnp.int32, (NH, tq), 0)
        lse_all = jnp.zeros((NH, tq), F32)
        ones = jnp.ones((ONES, tk), BF16)
        for kv in range(2):
            qs = []
            for pp in range(2):
                qp = q_ref[:, 128 * (2 * kv + pp):128 * (2 * kv + pp) + 128]
                qs += [qp[:, :HD], qp[:, HD:]]

            def keys(j, kv=kv):
                return k_ref[pl.ds(pl.multiple_of(j * tk, tk), tk), :][:, HD * kv:HD * kv + HD]

            def scores(kj, t, qs=qs):
                return lax.dot_general(kj, qs[t], NT, preferred_element_type=F32)

            def step(j, carry, kv=kv):
                kj = keys(j)
                kn = keys(jnp.minimum(j + 1, nk - 1))
                v1 = jnp.concatenate([vt_ref[j, HD * kv:HD * kv + HD, :], ones], axis=0)
                sts = [st_sc[t] for t in range(AHEAD)]
                new = []
                for t in range(GRP):
                    m, acc = carry[2 * t], carry[2 * t + 1]
                    if t + AHEAD < GRP:
                        sts.append(scores(kj, t + AHEAD))
                    st = sts[t]
                    mn = jnp.maximum(m, jnp.max(st, axis=0, keepdims=True))
                    pt = jnp.exp2(st - mn)
                    if t + AHEAD >= GRP:
                        st_sc[t + AHEAD - GRP] = scores(kn, t + AHEAD - GRP)
                    acc = jnp.exp2(m - mn) * acc + jnp.dot(v1, pt.astype(BF16), preferred_element_type=F32)
                    new += [mn, acc]
                return tuple(new)

            k0 = keys(0)
            for t in range(AHEAD):
                st_sc[t] = scores(k0, t)
            init = (jnp.full((1, tq), NEG, F32), jnp.zeros((HD + ONES, tq), F32)) * GRP
            res = lax.fori_loop(0, nk, step, init)
            outs = []
            for t in range(GRP):
                m, acc = res[2 * t], res[2 * t + 1]
                l = acc[HD:HD + 1, :]
                outs.append((acc[:HD, :] / l).T)
                lse_all = jnp.where(row8 == GRP * kv + t, m + jnp.log2(l), lse_all)
            o_ref[:, 256 * kv:256 * kv + 256] = jnp.concatenate(outs, axis=1).astype(BF16)
        lse_ref[...] = lse_all
        pl.when(pl.program_id(0) == nq - 1)(finish)

    any_spec = pl.BlockSpec(memory_space=pl.ANY)
    return pl.pallas_call(
        body, grid=(nq,),
        in_specs=[_rows(tq, BW), _const((S, 128)), _const((nk, 128, tk))] + [any_spec] * ns,
        out_specs=[_rows(tq, BW), pl.BlockSpec((NH, tq), lambda i: (0, i))] + [any_spec] * ns,
        out_shape=[jax.ShapeDtypeStruct((S, BW), BF16), jax.ShapeDtypeStruct((NH, S), F32)]
        + [jax.ShapeDtypeStruct((N_DEV,) + s.shape, s.dtype) for s in shards],
        scratch_shapes=[pltpu.VMEM((AHEAD, tk, tq), F32)] + _exchange_sems(ns) + [pltpu.SemaphoreType.DMA((ns,))],
        compiler_params=_params(1), name="attn_a_fwd",
    )(q, k, vt3, *shards)


def _window_mask(i, tq, S):
    W = tq + 2 * WIN
    r = lax.broadcasted_iota(jnp.int32, (tq, W), 0)
    c = lax.broadcasted_iota(jnp.int32, (tq, W), 1)
    kpos = i * tq - WIN + c
    return (jnp.abs(c - WIN - r) <= WIN) & (kpos >= 0) & (kpos < S)


TQ_B = 256


def _attn_b_fwd(q, kp, vpt3, sink2):
    S = q.shape[0]
    tq = min(TQ_B, S)
    W = tq + 2 * WIN
    nc = vpt3.shape[0]
    ONES = 16

    def body(q_ref, k_ref, vt_ref, sink_ref, o_ref, lse_ref):
        i = pl.program_id(0)
        off = pl.multiple_of(i * tq, tq)
        r = lax.broadcasted_iota(jnp.int32, (W, tq), 1)
        c = lax.broadcasted_iota(jnp.int32, (W, tq), 0)
        kpos = i * tq - WIN + c
        valid = (jnp.abs(c - WIN - r) <= WIN) & (kpos >= 0) & (kpos < S)
        kw = k_ref[pl.ds(off, W), :]
        vt = jnp.concatenate([vt_ref[i + half] for half in range(W // tq)], axis=1)
        ones = jnp.ones((ONES, W), BF16)
        row8 = lax.broadcasted_iota(jnp.int32, (NH, tq), 0)
        lse_all = jnp.zeros((NH, tq), F32)
        qs = []
        for p in range(4):
            qp = q_ref[:, 128 * p:128 * p + 128]
            qs += [qp[:, :HD], qp[:, HD:]]
        khs = [kw[:, HD * kv:HD * kv + HD] for kv in range(2)]
        v1s = [jnp.concatenate([vt[HD * kv:HD * kv + HD, :], ones], axis=0) for kv in range(2)]

        def scores(h):
            return lax.dot_general(khs[h // GRP], qs[h], NT, preferred_element_type=F32)

        ss = [scores(0), scores(1)]
        outs = []
        for h in range(NH):
            if h + 2 < NH:
                ss.append(scores(h + 2))
            st = jnp.where(valid, ss[h], NEG)
            sk = sink_ref[:, h:h + 1]
            m = jnp.maximum(jnp.max(st, axis=0, keepdims=True), sk)
            acc = jnp.dot(v1s[h // GRP], jnp.exp2(st - m).astype(BF16), preferred_element_type=F32)
            l = acc[HD:HD + 1, :] + jnp.exp2(sk - m)
            outs.append((acc[:HD, :] / l).T)
            lse_all = jnp.where(row8 == h, m + jnp.log2(l), lse_all)
        for p in range(4):
            o_ref[:, 128 * p:128 * p + 128] = jnp.concatenate(outs[2 * p:2 * p + 2], axis=1).astype(BF16)
        lse_ref[...] = lse_all

    return pl.pallas_call(
        body, grid=(S // tq,),
        in_specs=[_rows(tq, BW), _const((S + 2 * WIN, 128)), _const((nc, 128, tq)), _const((1, NH))],
        out_specs=[_rows(tq, BW), pl.BlockSpec((NH, tq), lambda i: (0, i))],
        out_shape=[jax.ShapeDtypeStruct((S, BW), BF16), jax.ShapeDtypeStruct((NH, S), F32)],
        compiler_params=_params(1), name="attn_b_fwd",
    )(q, kp, vpt3, sink2)


def _merge_out(ya, yb, ga, gb, x, modv, wb, wout):
    S = x.shape[0]
    tm = min(256, S)

    def body(ya_ref, yb_ref, ga_ref, gb_ref, x_ref, mod_ref, wb_ref, wo_ref, x1_ref, mg_ref, ua_ref, ub_ref):
        ua = jnp.dot(ya_ref[...], wb_ref[0], preferred_element_type=F32)
        ub = jnp.dot(yb_ref[...], wb_ref[1], preferred_element_type=F32)
        merged = jax.nn.sigmoid(ga_ref[...].astype(F32)) * ua + jax.nn.sigmoid(gb_ref[...].astype(F32)) * ub
        mb = merged.astype(BF16)
        ua_ref[...] = ua.astype(BF16)
        ub_ref[...] = ub.astype(BF16)
        mg_ref[...] = mb
        x1_ref[...] = x_ref[...] + mod_ref[2:3, :] * jnp.dot(mb, wo_ref[...], preferred_element_type=F32)

    return pl.pallas_call(
        body, grid=(S // tm,),
        in_specs=[_rows(tm, BW), _rows(tm, BW), _rows(tm, D), _rows(tm, D), _rows(tm, D), _const((6, D)),
                  _const((2, BW, D)), _const((D, D))],
        out_specs=[_rows(tm, D)] * 4,
        out_shape=[jax.ShapeDtypeStruct((S, D), F32)] + [jax.ShapeDtypeStruct((S, D), BF16)] * 3,
        compiler_params=_params(1), name="merge_out",
    )(ya, yb, ga, gb, x, modv, wb, wout)


def _mlp_fwd(x1, modv, n2g, wmi, wmo, fg, target):
    S = x1.shape[0]
    tm = min(1024, S)
    tf = wmi.shape[2]
    nj = wmi.shape[0] // MLP_SHARDS

    def body(x1_ref, mod_ref, g_ref, wi_ref, wo_ref, fg_ref, t_ref, h2_ref, hp_ref, dx2_ref, st_ref, acc_ref):
        i, j = pl.program_id(0), pl.program_id(1)

        @pl.when(j == 0)
        def _():
            xt = x1_ref[...]
            r = lax.rsqrt(jnp.mean(xt * xt, axis=-1, keepdims=True) + EPS)
            h2 = ((xt * r) * g_ref[...]) * (1.0 + mod_ref[4:5, :]) + mod_ref[3:4, :]
            h2_ref[...] = h2.astype(BF16)
            acc_ref[...] = jnp.zeros_like(acc_ref)

        @pl.when((i == 0) & (j == 0))
        def _():
            st_ref[...] = jnp.zeros_like(st_ref)

        out = None
        for u in range(MLP_SHARDS):
            hp = jnp.dot(h2_ref[...], wi_ref[u], preferred_element_type=F32)
            hp_ref[:, tf * u:tf * u + tf] = hp.astype(BF16)
            hid = jnp.square(jnp.maximum(hp, 0.0))
            part = jnp.dot(hid.astype(BF16), wo_ref[u], preferred_element_type=F32)
            out = part if out is None else out + part
        acc_ref[...] += out

        @pl.when(j == nj - 1)
        def _():
            x2 = x1_ref[...] + mod_ref[5:6, :] * acc_ref[...]
            r3 = lax.rsqrt(jnp.mean(x2 * x2, axis=-1, keepdims=True) + EPS)
            xn = x2 * r3
            err = xn * fg_ref[...] - t_ref[...]
            dy = err * (1.0 / D)
            gy = dy * fg_ref[...]
            dx2_ref[...] = r3 * (gy - xn * jnp.mean(gy * xn, axis=-1, keepdims=True))
            st_ref[0:1, :] += jnp.sum(dy * xn, axis=0, keepdims=True)
            st_ref[1:2, :] += jnp.sum(err * err, axis=0, keepdims=True) * (0.5 / D)

    return pl.pallas_call(
        body, grid=(S // tm, nj),
        in_specs=[pl.BlockSpec((tm, D), lambda i, j: (i, 0)), _const((6, D)), _const((1, D)),
                  pl.BlockSpec((MLP_SHARDS, D, tf), lambda i, j: (j, 0, 0)), pl.BlockSpec((MLP_SHARDS, tf, D), lambda i, j: (j, 0, 0)),
                  _const((1, D)), pl.BlockSpec((tm, D), lambda i, j: (i, 0))],
        out_specs=[pl.BlockSpec((tm, D), lambda i, j: (i, 0)), pl.BlockSpec((tm, MLP_SHARDS * tf), lambda i, j: (i, j)),
                   pl.BlockSpec((tm, D), lambda i, j: (i, 0)), _const((8, D))],
        out_shape=[jax.ShapeDtypeStruct((S, D), BF16), jax.ShapeDtypeStruct((S, wmi.shape[0] * tf), BF16),
                   jax.ShapeDtypeStruct((S, D), F32), jax.ShapeDtypeStruct((8, D), F32)],
        scratch_shapes=[pltpu.VMEM((tm, D), F32)],
        compiler_params=_params(2), name="mlp_fwd",
    )(x1, modv, n2g, wmi, wmo, fg, target)


def _mlp_bwd(dx2, x1, hp, modv, n2g, wmi, wmo):
    S = x1.shape[0]
    tm = min(512, S)
    tf = wmi.shape[2]
    nj = wmi.shape[0] // MLP_SHARDS

    def body(dx2_ref, x1_ref, hp_ref, mod_ref, g_ref, wi_ref, wo_ref, dhp_ref, dx1_ref, st_ref, dmo_ref, acc_ref):
        i, j = pl.program_id(0), pl.program_id(1)

        @pl.when(j == 0)
        def _():
            dmo_ref[...] = (mod_ref[5:6, :] * dx2_ref[...]).astype(BF16)
            acc_ref[...] = jnp.zeros_like(acc_ref)

        @pl.when((i == 0) & (j == 0))
        def _():
            st_ref[...] = jnp.zeros_like(st_ref)

        out = None
        for u in range(MLP_SHARDS):
            sl = slice(tf * u, tf * u + tf)
            dhid = lax.dot_general(dmo_ref[...], wo_ref[u], NT, preferred_element_type=F32)
            dhp = (dhid * (2.0 * jnp.maximum(hp_ref[:, sl].astype(F32), 0.0))).astype(BF16)
            dhp_ref[:, sl] = dhp
            part = lax.dot_general(dhp, wi_ref[u], NT, preferred_element_type=F32)
            out = part if out is None else out + part
        acc_ref[...] += out

        @pl.when(j == nj - 1)
        def _():
            dh2 = acc_ref[...]
            xt = x1_ref[...]
            r = lax.rsqrt(jnp.mean(xt * xt, axis=-1, keepdims=True) + EPS)
            xn = xt * r
            st_ref[0:1, :] += jnp.sum(dh2, axis=0, keepdims=True)
            st_ref[1:2, :] += jnp.sum(dh2 * xn, axis=0, keepdims=True)
            dxn = dh2 * (g_ref[...] * (1.0 + mod_ref[4:5, :]))
            dx1_ref[...] = dx2_ref[...] + r * (dxn - xn * jnp.mean(dxn * xn, axis=-1, keepdims=True))

    return pl.pallas_call(
        body, grid=(S // tm, nj),
        in_specs=[pl.BlockSpec((tm, D), lambda i, j: (i, 0)), pl.BlockSpec((tm, D), lambda i, j: (i, 0)),
                  pl.BlockSpec((tm, MLP_SHARDS * tf), lambda i, j: (i, j)), _const((6, D)), _const((1, D)),
                  pl.BlockSpec((MLP_SHARDS, D, tf), lambda i, j: (j, 0, 0)), pl.BlockSpec((MLP_SHARDS, tf, D), lambda i, j: (j, 0, 0))],
        out_specs=[pl.BlockSpec((tm, MLP_SHARDS * tf), lambda i, j: (i, j)), pl.BlockSpec((tm, D), lambda i, j: (i, 0)), _const((8, D))],
        out_shape=[jax.ShapeDtypeStruct((S, wmi.shape[0] * tf), BF16), jax.ShapeDtypeStruct((S, D), F32),
                   jax.ShapeDtypeStruct((8, D), F32)],
        scratch_shapes=[pltpu.VMEM((tm, D), BF16), pltpu.VMEM((tm, D), F32)],
        compiler_params=_params(2), name="mlp_bwd",
    )(dx2, x1, hp, modv, n2g, wmi, wmo)


def _tn_matmul(a, b, tk, tn, name, relu_sq=False, dev_major=False):
    S, K = a.shape
    N = b.shape[1]
    ts = min(1024, S)
    ns = S // ts

    def body(a_ref, b_ref, o_ref):
        @pl.when(pl.program_id(2) == 0)
        def _():
            o_ref[...] = jnp.zeros_like(o_ref)

        at = a_ref[...]
        if relu_sq:
            at = jnp.square(jnp.maximum(at.astype(F32), 0.0)).astype(BF16)
        o_ref[...] += lax.dot_general(at, b_ref[...].astype(BF16), TN, preferred_element_type=F32)

    if dev_major:
        out_spec = pl.BlockSpec((None, tk, tn), lambda k, n, s: (n, k, 0))
        out_shape = jax.ShapeDtypeStruct((N // tn, K, tn), F32)
    else:
        out_spec = pl.BlockSpec((tk, tn), lambda k, n, s: (k, n))
        out_shape = jax.ShapeDtypeStruct((K, N), F32)
    return pl.pallas_call(
        body, grid=(K // tk, N // tn, ns),
        in_specs=[pl.BlockSpec((ts, tk), lambda k, n, s: (s, k)), pl.BlockSpec((ts, tn), lambda k, n, s: (s, n))],
        out_specs=out_spec, out_shape=out_shape,
        compiler_params=_params(3), name=name,
    )(a, b)


def _scale_gate(m, w, g, row, name):
    K = m.shape[0]
    tk = min(512, K)

    def body(m_ref, w_ref, mod_ref, dw_ref, dg_ref):
        @pl.when(pl.program_id(0) == 0)
        def _():
            dg_ref[...] = jnp.zeros_like(dg_ref)

        mt = m_ref[...]
        dw_ref[...] = mt * mod_ref[row:row + 1, :]
        dg_ref[0:1, :] += jnp.sum(mt * w_ref[...].astype(F32), axis=0, keepdims=True)

    return pl.pallas_call(
        body, grid=(K // tk,),
        in_specs=[_rows(tk, D), _rows(tk, D), _const((6, D))],
        out_specs=[_rows(tk, D), _const((8, D))],
        out_shape=[jax.ShapeDtypeStruct((K, D), F32), jax.ShapeDtypeStruct((8, D), F32)],
        compiler_params=_params(1), name=name,
    )(m, w, g)


def _merge_bwd(dx1, modv, ga, gb, ua, ub, ya, yb, wb, wout):
    S = dx1.shape[0]
    tm = min(512, S)

    def body(dx1_ref, mod_ref, ga_ref, gb_ref, ua_ref, ub_ref, ya_ref, yb_ref, wb_ref, wo_ref,
             dua_ref, dub_ref, dga_ref, dgb_ref, dya_ref, dyb_ref, dla_ref, dlb_ref, dyat_ref):
        dao = (mod_ref[2:3, :] * dx1_ref[...]).astype(BF16)
        dm = lax.dot_general(dao, wo_ref[...], NT, preferred_element_type=F32)
        r = lax.broadcasted_iota(jnp.int32, (BW, NH), 0) // HD
        c = lax.broadcasted_iota(jnp.int32, (BW, NH), 1)
        head_of = (r == c).astype(F32)
        for br, (g_ref, u_ref, y_ref, du_ref, dg_ref, dy_ref, dl_ref) in enumerate((
                (ga_ref, ua_ref, ya_ref, dua_ref, dga_ref, dya_ref, dla_ref),
                (gb_ref, ub_ref, yb_ref, dub_ref, dgb_ref, dyb_ref, dlb_ref))):
            sg = jax.nn.sigmoid(g_ref[...].astype(F32))
            du = (dm * sg).astype(BF16)
            du_ref[...] = du
            dg_ref[...] = (dm * u_ref[...].astype(F32) * sg * (1.0 - sg)).astype(BF16)
            dy = lax.dot_general(du, wb_ref[br], NT, preferred_element_type=F32)
            dyb16 = dy.astype(BF16)
            dy_ref[...] = dyb16
            if br == 0:
                dyat_ref[...] = dy.T.astype(BF16)
            dl_ref[...] = jnp.dot(dyb16.astype(F32) * y_ref[...].astype(F32), head_of,
                                  preferred_element_type=F32, precision=lax.Precision.HIGHEST)

    return pl.pallas_call(
        body, grid=(S // tm,),
        in_specs=[_rows(tm, D), _const((6, D)), _rows(tm, D), _rows(tm, D), _rows(tm, D), _rows(tm, D),
                  _rows(tm, BW), _rows(tm, BW), _const((2, BW, D)), _const((D, D))],
        out_specs=[_rows(tm, D)] * 4 + [_rows(tm, BW)] * 2 + [_rows(tm, NH)] * 2 + [pl.BlockSpec((BW, tm), lambda i: (0, i))],
        out_shape=[jax.ShapeDtypeStruct((S, D), BF16)] * 4 + [jax.ShapeDtypeStruct((S, BW), BF16)] * 2
        + [jax.ShapeDtypeStruct((S, NH), F32)] * 2 + [jax.ShapeDtypeStruct((BW, S), BF16)],
        compiler_params=_params(1), name="merge_bwd",
    )(dx1, modv, ga, gb, ua, ub, ya, yb, wb, wout)


def _attn_a_bwd(q, qt, k, kt3, v, do, dot_, lse, delta, chunks):
    S = q.shape[0]
    tq = min(512, S)
    tk = min(512, S)
    nq, nk = S // tq, S // tk

    nc = len(chunks)

    def body(q_ref, qt_ref, do_ref, dot_ref, lse_ref, dl_ref, k_ref, v_ref, kt_ref, *rest):
        g_hbm, (dq_ref, dk_hbm, dv_hbm), recv_hbm = rest[:nc], rest[nc:nc + 3], rest[nc + 3:2 * nc + 3]
        dk_sc, dv_sc, sem, send_sems, recv_sems = rest[2 * nc + 3:]
        i = pl.program_id(0)
        start, finish = _exchange_scatter(g_hbm, recv_hbm, send_sems, recv_sems)

        @pl.when(i == 0)
        def _():
            start()
            dk_sc[...] = jnp.zeros_like(dk_sc)
            dv_sc[...] = jnp.zeros_like(dv_sc)

        for kv in range(2):
            qg = q_ref[:, 256 * kv:256 * kv + 256]
            dog = do_ref[:, 256 * kv:256 * kv + 256]
            heads = []
            for t in range(GRP):
                h = GRP * kv + t
                heads.append((qg[:, HD * t:HD * t + HD], dog[:, HD * t:HD * t + HD],
                              qt_ref[HD * h:HD * h + HD, :], dot_ref[HD * h:HD * h + HD, :],
                              lse_ref[:, h:h + 1], dl_ref[:, h:h + 1]))

            def step(j, carry, kv=kv, heads=heads):
                off = pl.multiple_of(j * tk, tk)
                kj = k_ref[pl.ds(off, tk), :][:, HD * kv:HD * kv + HD]
                vj = v_ref[pl.ds(off, tk), :][:, HD * kv:HD * kv + HD]
                kjt = kt_ref[j, HD * kv:HD * kv + HD, :]
                dkt = jnp.zeros((HD, tk), F32)
                dvt = jnp.zeros((HD, tk), F32)
                new = []

                def logits(t):
                    return (lax.dot_general(heads[t][0], kj, NT, preferred_element_type=F32),
                            lax.dot_general(heads[t][1], vj, NT, preferred_element_type=F32))

                sd = [logits(0)]
                for t, (qh, doh, qth, doth, lse_h, dl_h) in enumerate(heads):
                    if t + 1 < GRP:
                        sd.append(logits(t + 1))
                    s, dp = sd[t]
                    pm = jnp.exp2(s - lse_h)
                    ds = (pm * (dp - dl_h)).astype(BF16)
                    dvt = dvt + jnp.dot(doth, pm.astype(BF16), preferred_element_type=F32)
                    dkt = dkt + jnp.dot(qth, ds, preferred_element_type=F32)
                    new.append(carry[t] + lax.dot_general(kjt, ds, NT, preferred_element_type=F32))
                dk_sc[j, HD * kv:HD * kv + HD, :] += dkt
                dv_sc[j, HD * kv:HD * kv + HD, :] += dvt
                return tuple(new)

            res = lax.fori_loop(0, nk, step, (jnp.zeros((HD, tq), F32),) * GRP)
            for t in range(GRP):
                dq_ref[HD * (GRP * kv + t):HD * (GRP * kv + t) + HD, :] = res[t]

        @pl.when(i == nq - 1)
        def _():
            c1 = pltpu.make_async_copy(dk_sc, dk_hbm, sem.at[0])
            c2 = pltpu.make_async_copy(dv_sc, dv_hbm, sem.at[1])
            c1.start()
            c2.start()
            c1.wait()
            c2.wait()
            finish()

    any_spec = pl.BlockSpec(memory_space=pl.ANY)
    cols = pl.BlockSpec((BW, tq), lambda i: (0, i))
    return pl.pallas_call(
        body, grid=(nq,),
        in_specs=[_rows(tq, BW), cols, _rows(tq, BW), cols, _rows(tq, NH), _rows(tq, NH), _const((S, 128)), _const((S, 128)),
                  _const((nk, 128, tk))] + [any_spec] * nc,
        out_specs=[cols, any_spec, any_spec] + [any_spec] * nc,
        out_shape=[jax.ShapeDtypeStruct((BW, S), F32), jax.ShapeDtypeStruct((nk, 128, tk), F32),
                   jax.ShapeDtypeStruct((nk, 128, tk), F32)]
        + [jax.ShapeDtypeStruct((N_DEV - 1,) + c.shape[1:], c.dtype) for c in chunks],
        scratch_shapes=[pltpu.VMEM((nk, 128, tk), F32), pltpu.VMEM((nk, 128, tk), F32), pltpu.SemaphoreType.DMA((2,))]
        + _exchange_sems(nc),
        compiler_params=_params(1), name="attn_a_bwd",
    )(q, qt, do, dot_, lse, delta, k, v, kt3, *chunks)


def _attn_b_bwd(q, kp, vp, sink2, do, lse, delta):
    S = q.shape[0]
    tq = min(TQ_B, S)
    W = tq + 2 * WIN
    nq = S // tq
    nc = (S + 2 * WIN) // tq

    def body(q_ref, k_ref, v_ref, sink_ref, do_ref, lse_ref, dl_ref, dq_ref, dk_hbm, dv_hbm, ds_ref, dk_sc, dv_sc, sem):
        i = pl.program_id(0)

        @pl.when(i == 0)
        def _():
            dk_sc[...] = jnp.zeros_like(dk_sc)
            dv_sc[...] = jnp.zeros_like(dv_sc)
            ds_ref[...] = jnp.zeros_like(ds_ref)

        off = pl.multiple_of(i * tq, tq)
        valid = _window_mask(i, tq, S)
        kw = k_ref[pl.ds(off, W), :]
        vw = v_ref[pl.ds(off, W), :]
        lse_i = lse_ref[...]
        dl_i = dl_ref[...]
        qa = q_ref[...]
        doa = do_ref[...]
        qt = qa.astype(F32).T.astype(BF16)
        dot_ = doa.astype(F32).T.astype(BF16)
        khs = [kw[:, HD * kv:HD * kv + HD] for kv in range(2)]
        vhs = [vw[:, HD * kv:HD * kv + HD] for kv in range(2)]

        def logits(h):
            return (lax.dot_general(qa[:, HD * h:HD * h + HD], khs[h // GRP], NT, preferred_element_type=F32),
                    lax.dot_general(doa[:, HD * h:HD * h + HD], vhs[h // GRP], NT, preferred_element_type=F32))

        sd = [logits(0)]
        dqs = []
        dkt = [jnp.zeros((HD, W), F32), jnp.zeros((HD, W), F32)]
        dvt = [jnp.zeros((HD, W), F32), jnp.zeros((HD, W), F32)]
        for h in range(NH):
            kv = h // GRP
            if h + 1 < NH:
                sd.append(logits(h + 1))
            s, dp = sd[h]
            pm = jnp.exp2(jnp.where(valid, s, NEG) - lse_i[:, h:h + 1])
            ds = (pm * (dp - dl_i[:, h:h + 1])).astype(BF16)
            dvt[kv] = dvt[kv] + jnp.dot(dot_[HD * h:HD * h + HD, :], pm.astype(BF16), preferred_element_type=F32)
            dkt[kv] = dkt[kv] + jnp.dot(qt[HD * h:HD * h + HD, :], ds, preferred_element_type=F32)
            dqs.append(jnp.dot(ds, khs[kv], preferred_element_type=F32))
        for p in range(4):
            dq_ref[:, 128 * p:128 * p + 128] = jnp.concatenate(dqs[2 * p:2 * p + 2], axis=1)
        for half in range(W // tq):
            dk_sc[i + half] += jnp.concatenate([d[:, tq * half:tq * half + tq] for d in dkt], axis=0)
            dv_sc[i + half] += jnp.concatenate([d[:, tq * half:tq * half + tq] for d in dvt], axis=0)
        psd = jnp.exp2(sink_ref[...] - lse_i) * dl_i
        r = lax.broadcasted_iota(jnp.int32, (NH, 128), 0)
        c = lax.broadcasted_iota(jnp.int32, (NH, 128), 1)
        row = jnp.dot(jnp.sum(psd, axis=0, keepdims=True), (r == c).astype(F32),
                      preferred_element_type=F32, precision=lax.Precision.HIGHEST)
        ds_ref[...] -= jnp.broadcast_to(row, (8, 128))

        @pl.when(i == nq - 1)
        def _():
            c1 = pltpu.make_async_copy(dk_sc, dk_hbm, sem.at[0])
            c2 = pltpu.make_async_copy(dv_sc, dv_hbm, sem.at[1])
            c1.start()
            c2.start()
            c1.wait()
            c2.wait()

    any_spec = pl.BlockSpec(memory_space=pl.ANY)
    return pl.pallas_call(
        body, grid=(nq,),
        in_specs=[_rows(tq, BW), _const((S + 2 * WIN, 128)), _const((S + 2 * WIN, 128)), _const((1, NH)),
                  _rows(tq, BW), _rows(tq, NH), _rows(tq, NH)],
        out_specs=[_rows(tq, BW), any_spec, any_spec, _const((8, 128))],
        out_shape=[jax.ShapeDtypeStruct((S, BW), F32), jax.ShapeDtypeStruct((nc, 128, tq), F32),
                   jax.ShapeDtypeStruct((nc, 128, tq), F32), jax.ShapeDtypeStruct((8, 128), F32)],
        scratch_shapes=[pltpu.VMEM((nc, 128, tq), F32), pltpu.VMEM((nc, 128, tq), F32), pltpu.SemaphoreType.DMA((2,))],
        compiler_params=_params(1), name="attn_b_bwd",
    )(q, kp, vp, sink2, do, lse, delta)


def _qk_bwd(dqa_t, dka_t3, dva_t3, dqb, dkb, dvb, qar, kar, qg2, kg2, tab_a, tab_b, dga, dgb):
    S = dqb.shape[0]
    tm = min(256, S)
    per = dka_t3.shape[2] // tm

    def body(dqa_ref, dka_ref, dva_ref, dqb_ref, dkb_ref, dvb_ref, qar_ref, kar_ref, qg_ref, kg_ref, ta_ref, tb_ref,
             dga_ref, dgb_ref, dp_ref, st_ref):
        @pl.when(pl.program_id(0) == 0)
        def _():
            st_ref[...] = jnp.zeros_like(st_ref)

        seg = _seg_matrix(128, HD)

        def norm_bwd(dz_rot, raw, g):
            dzn = _rope_t(dz_rot, ta_ref, 16)
            raw = raw.astype(F32)
            rr = lax.rsqrt(_seg_sum(raw * raw, seg) * (1.0 / HD) + EPS)
            zhat = raw * rr
            dzh = dzn * g
            draw = rr * (dzh - zhat * (_seg_sum(dzh * zhat, seg) * (1.0 / HD)))
            return draw, jnp.sum(dzn * zhat, axis=0, keepdims=True)

        gq = jnp.zeros((1, 128), F32)
        for p in range(4):
            sl = slice(128 * p, 128 * p + 128)
            draw, gsum = norm_bwd(dqa_ref[sl, :].T * 0.125, qar_ref[:, sl], qg_ref[...])
            gq = gq + gsum
            dp_ref[:, sl] = draw.astype(BF16)
            dp_ref[:, 768 + 128 * p:768 + 128 * p + 128] = _rope_t(dqb_ref[:, sl] * 0.125, tb_ref, 32).astype(BF16)
        draw, gk = norm_bwd(dka_ref[...].T * LN2, kar_ref[...], kg_ref[...])
        dp_ref[:, 512:640] = draw.astype(BF16)
        dp_ref[:, 640:768] = dva_ref[...].T.astype(BF16)
        dp_ref[:, 1280:1408] = _rope_t(dkb_ref[...] * LN2, tb_ref, 32).astype(BF16)
        dp_ref[:, 1408:1536] = dvb_ref[...].astype(BF16)
        dp_ref[:, 1536:2560] = dga_ref[...]
        dp_ref[:, 2560:3584] = dgb_ref[...]
        st_ref[0:1, :] += gq
        st_ref[1:2, :] += gk

    tab = pl.BlockSpec((3, tm, 128), lambda i: (0, i, 0))
    chunk_t = pl.BlockSpec((None, 128, tm), lambda i: (i // per, 0, i % per))
    return pl.pallas_call(
        body, grid=(S // tm,),
        in_specs=[pl.BlockSpec((BW, tm), lambda i: (0, i)), chunk_t, chunk_t, _rows(tm, BW), _rows(tm, 128), _rows(tm, 128),
                  _rows(tm, BW), _rows(tm, 128), _const((1, 128)), _const((1, 128)), tab, tab, _rows(tm, D), _rows(tm, D)],
        out_specs=[_rows(tm, INW), _const((8, 128))],
        out_shape=[jax.ShapeDtypeStruct((S, INW), BF16), jax.ShapeDtypeStruct((8, 128), F32)],
        compiler_params=_params(1), name="qk_bwd",
    )(dqa_t, dka_t3, dva_t3, dqb, dkb, dvb, qar, kar, qg2, kg2, tab_a, tab_b, dga, dgb)


def _in_bwd(dproj, win, x, dx1, modv, n1g, chunks):
    S = x.shape[0]
    tm = min(512, S)
    n = S // tm

    def body(dp_ref, w_ref, x_ref, dx1_ref, mod_ref, g_ref, c_hbm, gx_ref, st_ref, recv_hbm, send_sems, recv_sems):
        start, finish = _exchange_scatter([c_hbm], [recv_hbm], send_sems, recv_sems)

        @pl.when(pl.program_id(0) == 0)
        def _():
            start()
            st_ref[...] = jnp.zeros_like(st_ref)

        dh = lax.dot_general(dp_ref[...], w_ref[...], NT, preferred_element_type=F32)
        xt = x_ref[...]
        r = lax.rsqrt(jnp.mean(xt * xt, axis=-1, keepdims=True) + EPS)
        xn = xt * r
        st_ref[0:1, :] += jnp.sum(dh, axis=0, keepdims=True)
        st_ref[1:2, :] += jnp.sum(dh * xn, axis=0, keepdims=True)
        dxn = dh * (g_ref[...] * (1.0 + mod_ref[1:2, :]))
        gx_ref[...] = dx1_ref[...] + r * (dxn - xn * jnp.mean(dxn * xn, axis=-1, keepdims=True))
        pl.when(pl.program_id(0) == n - 1)(finish)

    any_spec = pl.BlockSpec(memory_space=pl.ANY)
    return pl.pallas_call(
        body, grid=(n,),
        in_specs=[_rows(tm, INW), _const((D, INW)), _rows(tm, D), _rows(tm, D), _const((6, D)), _const((1, D)), any_spec],
        out_specs=[_rows(tm, D), _const((8, D)), any_spec],
        out_shape=[jax.ShapeDtypeStruct((S, D), F32), jax.ShapeDtypeStruct((8, D), F32),
                   jax.ShapeDtypeStruct((N_DEV - 1,) + chunks.shape[1:], chunks.dtype)],
        scratch_shapes=_exchange_sems(1),
        compiler_params=_params(1), name="in_bwd",
    )(dproj, win, x, dx1, modv, n1g, chunks)


def _pack_small(st1, st2, stf, dg1, dg2, stqk, dsink, modv, n1g, n2g):
    def body(st1_ref, st2_ref, stf_ref, dg1_ref, dg2_ref, qk_ref, ds_ref, mod_ref, g1_ref, g2_ref, o_ref):
        a1, b1 = st1_ref[0:1, :], st1_ref[1:2, :]
        a2, b2 = st2_ref[0:1, :], st2_ref[1:2, :]
        r = lax.broadcasted_iota(jnp.int32, (128, D), 0)
        c = lax.broadcasted_iota(jnp.int32, (128, D), 1)
        fold_q = (c == r % HD).astype(F32)
        fold_k = (c == HD + r % HD).astype(F32)
        keep = (c == r).astype(F32)

        def place(v, sel):
            return jnp.dot(v, sel, preferred_element_type=F32, precision=lax.Precision.HIGHEST)

        loss = jnp.sum(stf_ref[1:2, :], axis=1, keepdims=True)
        lane = lax.broadcasted_iota(jnp.int32, (1, D), 1)
        rows = [a1, g1_ref[...] * b1, dg1_ref[0:1, :], a2, g2_ref[...] * b2, dg2_ref[0:1, :],
                (1.0 + mod_ref[1:2, :]) * b1, (1.0 + mod_ref[4:5, :]) * b2, stf_ref[0:1, :],
                place(qk_ref[0:1, :], fold_q) + place(qk_ref[1:2, :], fold_k),
                place(ds_ref[0:1, :], keep),
                jnp.where(lane == 0, loss, 0.0)]
        rows += [jnp.zeros((1, D), F32)] * (SMALL_ROWS - len(rows))
        for n, v in enumerate(rows):
            o_ref[n:n + 1, :] = v

    return pl.pallas_call(
        body, out_shape=jax.ShapeDtypeStruct((SMALL_ROWS, D), F32),
        compiler_params=pltpu.CompilerParams(vmem_limit_bytes=V7X_VMEM_LIMIT), name="pack_small",
    )(st1, st2, stf, dg1, dg2, stqk, dsink, modv, n1g, n2g)


def _wada_grad(silu_all, dmod_cols):
    def body(a_ref, b_ref, o_ref):
        o_ref[...] = lax.dot_general(a_ref[...], b_ref[...], TN, preferred_element_type=F32, precision=lax.Precision.HIGHEST)

    return pl.pallas_call(
        body, out_shape=jax.ShapeDtypeStruct((D, dmod_cols.shape[1]), F32),
        compiler_params=pltpu.CompilerParams(vmem_limit_bytes=V7X_VMEM_LIMIT), name="wada_grad",
    )(silu_all, dmod_cols)


def _adamw_sum(parts, w, m, v, name):
    R, C = w.shape
    tr = R if R <= 64 else next(t for t in (256, 128, 64, 32, 16, 8) if R % t == 0)
    n = len(parts)
    dyn = [idx for _, idx in parts if idx is not None and not isinstance(idx, int)]
    b1c = 1.0 - ADAM_B1 ** ADAM_STEP
    b2c = 1.0 - ADAM_B2 ** ADAM_STEP

    def body(*refs):
        refs = refs[len(dyn):]
        g = refs[0][...].astype(F32)
        for k in range(1, n):
            g = g + refs[k][...].astype(F32)
        w_ref, m_ref, v_ref, g_out, d_out, m_out, v_out = refs[n:]
        mn = ADAM_B1 * m_ref[...] + (1.0 - ADAM_B1) * g
        vn = ADAM_B2 * v_ref[...] + (1.0 - ADAM_B2) * jnp.square(g)
        g_out[...] = g
        m_out[...] = mn
        v_out[...] = vn
        d_out[...] = -ADAM_LR * ((mn / b1c) / (jnp.sqrt(vn / b2c) + ADAM_EPS) + ADAM_WD * w_ref[...])

    in_specs = []
    nd = 0
    for a, idx in parts:
        if idx is None:
            in_specs.append(pl.BlockSpec((tr, C), lambda i, *s: (i, 0)))
        elif isinstance(idx, int):
            in_specs.append(pl.BlockSpec((None, tr, C), lambda i, *s, idx=idx: (idx, i, 0)))
        else:
            in_specs.append(pl.BlockSpec((None, tr, C), lambda i, *s, nd=nd: (s[nd][0], i, 0)))
            nd += 1
    blk = pl.BlockSpec((tr, C), lambda i, *s: (i, 0))
    grid_spec = pltpu.PrefetchScalarGridSpec(
        num_scalar_prefetch=len(dyn), grid=(R // tr,), in_specs=in_specs + [blk] * 3, out_specs=[blk] * 4)
    return pl.pallas_call(
        body, grid_spec=grid_spec, out_shape=[jax.ShapeDtypeStruct((R, C), F32)] * 4,
        compiler_params=_params(1), name=name,
    )(*dyn, *[a for a, _ in parts], w, m, v)


def _me():
    return lax.axis_index("x"), lax.axis_index("y"), lax.axis_index("c")


def _peer(k):
    x, y, c = _me()
    return (x ^ ((k >> 2) & 1), y ^ ((k >> 1) & 1), c ^ (k & 1))


def _ada_exchange(c_row, w_ada, b_rows):
    NW = w_ada.shape[1]

    def body(c_ref, w_ref, b_ref, sall_ref, mod_ref, src_ref, mp_ref, send1, recv1, send2, recv2):
        x, y, c = _me()
        me = 4 * x + 2 * y + c
        cv = c_ref[...]
        src_ref[...] = jnp.broadcast_to(cv * jax.nn.sigmoid(cv), (8, D))
        mine = pl.ds(pl.multiple_of(me * 8, 8), 8)
        sall_ref[mine, :] = src_ref[...]
        sends = [pltpu.make_async_remote_copy(src_ref, sall_ref.at[mine, :], send1.at[k - 1], recv1.at[k - 1],
                                              device_id=_peer(k), device_id_type=MESH) for k in range(1, N_DEV)]
        for cp in sends:
            cp.start()
        for k in range(1, N_DEV):
            theirs = pl.ds(pl.multiple_of((me ^ k) * 8, 8), 8)
            pltpu.make_async_remote_copy(src_ref, sall_ref.at[theirs, :], send1.at[k - 1], recv1.at[k - 1],
                                         device_id=_peer(k), device_id_type=MESH).wait_recv()
        for cp in sends:
            cp.wait_send()
        mp_ref[...] = jnp.dot(sall_ref[...], w_ref[...], preferred_element_type=F32, precision=lax.Precision.HIGHEST)
        mod_ref[mine, :] = mp_ref[mine, :] + b_ref[mine, :]
        sends = []
        for k in range(1, N_DEV):
            theirs = pl.ds(pl.multiple_of((me ^ k) * 8, 8), 8)
            sends.append(pltpu.make_async_remote_copy(mp_ref.at[theirs, :], mod_ref.at[mine, :], send2.at[k - 1], recv2.at[k - 1],
                                                      device_id=_peer(k), device_id_type=MESH))
        for cp in sends:
            cp.start()
        for k in range(1, N_DEV):
            theirs = pl.ds(pl.multiple_of((me ^ k) * 8, 8), 8)
            pltpu.make_async_remote_copy(mp_ref.at[mine, :], mod_ref.at[theirs, :], send2.at[k - 1], recv2.at[k - 1],
                                         device_id=_peer(k), device_id_type=MESH).wait_recv()
            mod_ref[theirs, :] = mod_ref[theirs, :] + b_ref[theirs, :]
        for cp in sends:
            cp.wait_send()

    vm = pl.BlockSpec(memory_space=pltpu.VMEM)
    return pl.pallas_call(
        body, in_specs=[vm, vm, vm], out_specs=[vm, vm],
        out_shape=[jax.ShapeDtypeStruct((8 * N_DEV, D), F32), jax.ShapeDtypeStruct((8 * N_DEV, NW), F32)],
        scratch_shapes=[pltpu.VMEM((8, D), F32), pltpu.VMEM((8 * N_DEV, NW), F32)]
        + [pltpu.SemaphoreType.DMA((N_DEV - 1,))] * 4,
        compiler_params=pltpu.CompilerParams(vmem_limit_bytes=V7X_VMEM_LIMIT), name="ada_exchange",
    )(c_row, w_ada, b_rows)


def _weight_gather(shard):
    def body(x_ref, out_ref, send_sems, recv_sems, local_sem):
        x, y, c = _me()
        me, sibling = (x, y, c), (x, y, 1 - c)
        chips = [(1 - x, y), (x, 1 - y), (1 - x, 1 - y)]

        def slot(px, py, pc):
            return out_ref.at[4 * px + 2 * py + pc]

        def copy(k, block, to, src=None):
            return pltpu.make_async_remote_copy(
                src_ref=slot(*block) if src is None else src, dst_ref=slot(*block),
                send_sem=send_sems.at[k], recv_sem=recv_sems.at[k], device_id=to, device_id_type=MESH)

        mine = pltpu.make_async_copy(x_ref, slot(*me), local_sem)
        mine.start()
        first = [copy(0, me, sibling, src=x_ref)]
        first += [copy(1 + j, me, (*chip, c), src=x_ref) for j, chip in enumerate(chips)]
        for cp in first:
            cp.start()
        passed = [copy(4 + j, (*chip, c), sibling) for j, chip in enumerate(chips)]
        for j, chip in enumerate(chips):
            copy(1 + j, (*chip, c), me).wait_recv()
            passed[j].start()
        copy(0, sibling, me).wait_recv()
        for j, chip in enumerate(chips):
            copy(4 + j, (*chip, 1 - c), me).wait_recv()
        for cp in first + passed:
            cp.wait_send()
        mine.wait()

    any_spec = pl.BlockSpec(memory_space=pl.ANY)
    return pl.pallas_call(
        body, in_specs=[any_spec], out_specs=any_spec,
        out_shape=jax.ShapeDtypeStruct((N_DEV,) + shard.shape, shard.dtype),
        scratch_shapes=[pltpu.SemaphoreType.DMA((7,)), pltpu.SemaphoreType.DMA((7,)), pltpu.SemaphoreType.DMA],
        name="weight_gather",
    )(shard)


def _small_gather(block):
    def body(b_ref, out_ref, send_sems, recv_sems):
        x, y, c = _me()
        me = 4 * x + 2 * y + c
        out_ref[me] = b_ref[...]
        sends = [pltpu.make_async_remote_copy(b_ref, out_ref.at[me], send_sems.at[k - 1], recv_sems.at[k - 1],
                                              device_id=_peer(k), device_id_type=MESH) for k in range(1, N_DEV)]
        for cp in sends:
            cp.start()
        for k in range(1, N_DEV):
            pltpu.make_async_remote_copy(b_ref, out_ref.at[me ^ k], send_sems.at[k - 1], recv_sems.at[k - 1],
                                         device_id=_peer(k), device_id_type=MESH).wait_recv()
        for cp in sends:
            cp.wait_send()

    vm = pl.BlockSpec(memory_space=pltpu.VMEM)
    return pl.pallas_call(
        body, in_specs=[vm], out_specs=vm,
        out_shape=jax.ShapeDtypeStruct((N_DEV,) + block.shape, block.dtype),
        scratch_shapes=[pltpu.SemaphoreType.DMA((N_DEV - 1,)), pltpu.SemaphoreType.DMA((N_DEV - 1,))],
        name="small_gather",
    )(block)


def _pack_small_params(b_ada, n1, n2, fg, qn, kn, sink):
    z = jnp.zeros((SMALL_ROWS, D), F32)
    z = z.at[0:6].set(b_ada.reshape(6, D)).at[6].set(n1.reshape(D)).at[7].set(n2.reshape(D)).at[8].set(fg.reshape(D))
    z = z.at[9, 0:HD].set(qn.reshape(HD)).at[9, HD:2 * HD].set(kn.reshape(HD)).at[10, 0:NH].set(sink.reshape(NH))
    return z


def _unpack_small(p):
    return (p[0:6].reshape(1, 6 * D), p[6].reshape(1, D), p[9, 0:HD].reshape(1, HD), p[9, HD:2 * HD].reshape(1, HD),
            p[10, 0:NH].reshape(1, NH), p[7].reshape(1, D), p[8].reshape(D))


def kernel(x, c, w_ada, b_ada, norm1_g, w_in, q_norm_a, k_norm_a, sink_b, w_branch, w_out, norm2_g, w_mlp_in, w_mlp_out, final_g, loss_target, m_w_ada, m_b_ada, m_norm1_g, m_w_in, m_q_norm_a, m_k_norm_a, m_sink_b, m_w_branch, m_w_out, m_norm2_g, m_w_mlp_in, m_w_mlp_out, m_final_g, v_w_ada, v_b_ada, v_norm1_g, v_w_in, v_q_norm_a, v_k_norm_a, v_sink_b, v_w_branch, v_w_out, v_norm2_g, v_w_mlp_in, v_w_mlp_out, v_final_g):
    S = x.shape[1]
    xs = x.reshape(S, D)
    tgt = loss_target.reshape(S, D)
    ax, ay, ac = lax.axis_index("x"), lax.axis_index("y"), lax.axis_index("c")
    me = 4 * ax + 2 * ay + ac
    me1 = me.reshape(1).astype(jnp.int32)
    NW = w_ada.shape[2]
    NI = w_in.shape[2]

    silu64, mod64 = _ada_exchange(c.reshape(1, D), w_ada.reshape(D, NW),
                                  jnp.repeat(b_ada.reshape(N_DEV, NW), 8, axis=0))
    silu_all = silu64[0::8]
    modv = mod64[0::8].reshape(6, D)

    win = _weight_gather(w_in[0].astype(BF16)).transpose(1, 0, 2).reshape(D, INW)
    rest_shards = tuple(w[0].astype(BF16) for w in (w_branch, w_out, w_mlp_in, w_mlp_out))

    tab_a, tab_b = _rope_tables(S)
    qg2 = jnp.tile(q_norm_a.reshape(1, HD), (1, 2))
    kg2 = jnp.tile(k_norm_a.reshape(1, HD), (1, 2))
    n1g = norm1_g.reshape(1, D)
    n2g = norm2_g.reshape(1, D)
    fg = final_g.reshape(1, D)
    sink2 = sink_b.reshape(1, NH) * LOG2E

    h, qar, kar, qa, ka, va, qb, kb, vb, ga, gb, qa_t, ka_t3, va_t3 = _in_proj(xs, modv, n1g, win, qg2, kg2, tab_a, tab_b)
    ya, lse_at, wb, wout, wmi, wmo = _attn_a_fwd(qa, ka, va_t3, rest_shards)
    lse_a = lse_at.T
    wb = wb.transpose(1, 2, 0, 3).reshape(2, BW, D)
    wout = wout.reshape(D, D)
    pad = ((WIN, WIN), (0, 0))
    kbp, vbp = jnp.pad(kb, pad), jnp.pad(vb, pad)
    tb = min(TQ_B, S)
    yb, lse_bt = _attn_b_fwd(qb, kbp, vbp.reshape((S + 2 * WIN) // tb, tb, 128).transpose(0, 2, 1), sink2)
    lse_b = lse_bt.T
    x1, merged, ua, ub = _merge_out(ya, yb, ga, gb, xs, modv, wb, wout)
    h2, hp, dx2, stf = _mlp_fwd(x1, modv, n2g, wmi, wmo, fg, tgt)

    dhp, dx1, st2 = _mlp_bwd(dx2, x1, hp, modv, n2g, wmi, wmo)
    m2 = _tn_matmul(hp, dx2, 2048, D, "dw_mlp_out", relu_sq=True)
    g_wmo, dg2 = _scale_gate(m2, wmo.reshape(FF, D), modv, 5, "gate2_grad")
    g_wmi = _tn_matmul(h2, dhp, D, 512, "dw_mlp_in", dev_major=True)
    dua, dub, dga, dgb, dya, dyb, dl_a, dl_b, dya_t = _merge_bwd(dx1, modv, ga, gb, ua, ub, ya, yb, wb, wout)
    m1 = _tn_matmul(merged, dx1, D, D, "dw_out")
    g_wout, dg1 = _scale_gate(m1, wout, modv, 2, "gate1_grad")
    g_wb0 = _tn_matmul(ya, dua, BW, D, "dw_branch_a")
    g_wb1 = _tn_matmul(yb, dub, BW, D, "dw_branch_b")
    g_wb = jnp.stack([g_wb0, g_wb1]).reshape(2, BW, N_DEV, 128).transpose(2, 0, 1, 3).reshape(N_DEV, 2 * BW, 128)
    g_wout = g_wout.reshape(N_DEV, 128, D)
    g_wmo = g_wmo.reshape(N_DEV, 512, D)
    dqa_t, dka_t3, dva_t3, r_wb, r_wout, r_wmi, r_wmo = _attn_a_bwd(qa, qa_t, ka, ka_t3, va, dya, dya_t, lse_a, dl_a,
                                                                    (g_wb, g_wout, g_wmi, g_wmo))
    dqb, dkb_t, dvb_t, dsink = _attn_b_bwd(qb, kbp, vbp, sink2, dyb, lse_b, dl_b)
    dkb = dkb_t.transpose(0, 2, 1).reshape(S + 2 * WIN, 128)[WIN:WIN + S]
    dvb = dvb_t.transpose(0, 2, 1).reshape(S + 2 * WIN, 128)[WIN:WIN + S]
    dproj, stqk = _qk_bwd(dqa_t, dka_t3, dva_t3, dqb, dkb, dvb, qar, kar, qg2, kg2, tab_a, tab_b, dga, dgb)
    g_win = _tn_matmul(h, dproj, D, 896, "dw_in")
    g_win = g_win.reshape(D, N_DEV, NI).transpose(1, 0, 2)
    grad_x, st1, r_win = _in_bwd(dproj, win, xs, dx1, modv, n1g, g_win.astype(BF16))

    def adam(name, own, recv, w, m, v):
        shape = w.shape
        w2, m2_, v2 = (a.reshape(own.shape[1:]) for a in (w, m, v))
        outs = _adamw_sum([(own, me1)] + [(recv, k) for k in range(N_DEV - 1)], w2, m2_, v2, name)
        return [a.reshape(shape) for a in outs]

    o_win = adam("adamw_w_in", g_win, r_win, w_in, m_w_in, v_w_in)
    o_wb = adam("adamw_w_branch", g_wb, r_wb, w_branch, m_w_branch, v_w_branch)
    o_wout = adam("adamw_w_out", g_wout, r_wout, w_out, m_w_out, v_w_out)
    o_wmi = adam("adamw_w_mlp_in", g_wmi, r_wmi, w_mlp_in, m_w_mlp_in, v_w_mlp_in)
    o_wmo = adam("adamw_w_mlp_out", g_wmo, r_wmo, w_mlp_out, m_w_mlp_out, v_w_mlp_out)

    small = _pack_small(st1, st2, stf, dg1, dg2, stqk, dsink, modv, n1g, n2g)
    small_all = _small_gather(small)
    sw = _pack_small_params(b_ada, norm1_g, norm2_g, final_g, q_norm_a, k_norm_a, sink_b)
    sm = _pack_small_params(m_b_ada, m_norm1_g, m_norm2_g, m_final_g, m_q_norm_a, m_k_norm_a, m_sink_b)
    sv = _pack_small_params(v_b_ada, v_norm1_g, v_norm2_g, v_final_g, v_q_norm_a, v_k_norm_a, v_sink_b)
    sm_out = _adamw_sum([(small_all, k) for k in range(N_DEV)], sw, sm, sv, "adamw_small")
    loss = sm_out[0][11, 0]
    sm_out = [_unpack_small(a) for a in sm_out]

    dmod_all = small_all[:, 0:6, :].reshape(N_DEV, 6 * D)
    dmod_cols = lax.dynamic_slice_in_dim(dmod_all, me * NW, NW, axis=1)
    g_wada = _wada_grad(silu_all, dmod_cols)
    ada = _adamw_sum([(g_wada, None)], w_ada.reshape(D, NW), m_w_ada.reshape(D, NW), v_w_ada.reshape(D, NW), "adamw_ada")
    ada = [a.reshape(1, D, NW) for a in ada]

    def leaves(k):
        b_, n1_, qn_, kn_, sk_, n2_, fg_ = sm_out[k]
        return [ada[k], b_, n1_, o_win[k], qn_, kn_, sk_, o_wb[k], o_wout[k], n2_, o_wmi[k], o_wmo[k], fg_]

    return (loss, grad_x.reshape(1, S, D), *leaves(0), *leaves(1), *leaves(2), *leaves(3))
```

```python
import jax
import jax.numpy as jnp
from jax import lax
from jax.experimental import pallas as pl
from jax.experimental.pallas import tpu as pltpu

F32, BF16 = jnp.float32, jnp.bfloat16
MESH = pl.DeviceIdType.MESH

D = 1024
HD = 64
NH = 8
GRP = 4
BW = 512
FF = 4096
INW = 3584
GRID_W = 64
WIN = 128
THETA = 10000.0
EPS = 1e-6
NEG = -1e30
N_DEV = 8
LOG2E = 1.4426950408889634
LN2 = 0.6931471805599453
QA_SCALE = 0.125 * LOG2E
SMALL_ROWS = 16
MLP_SHARDS = 4
V7X_VMEM_LIMIT = 56 * 1024 * 1024

ADAM_LR, ADAM_B1, ADAM_B2, ADAM_EPS, ADAM_WD, ADAM_STEP = 0.001, 0.9, 0.999, 1e-08, 0.01, 10

NT = (((1,), (1,)), ((), ()))
TN = (((0,), (0,)), ((), ()))


def _params(n_axes, vmem=V7X_VMEM_LIMIT):
    return pltpu.CompilerParams(dimension_semantics=("arbitrary",) * n_axes, vmem_limit_bytes=vmem)


def _const(shape):
    return pl.BlockSpec(shape, lambda *_: (0,) * len(shape))


def _rows(tm, width):
    return pl.BlockSpec((tm, width), lambda i, *_: (i, 0))


def _seg_matrix(n, seg):
    r = lax.broadcasted_iota(jnp.int32, (n, n), 0) // seg
    c = lax.broadcasted_iota(jnp.int32, (n, n), 1) // seg
    return (r == c).astype(F32)


def _seg_sum(z, seg_mat):
    return jnp.dot(z, seg_mat, preferred_element_type=F32, precision=lax.Precision.HIGHEST)


def _rope(z, t_ref, sh):
    return z * t_ref[0] + pltpu.roll(z, sh, 1) * t_ref[1] + pltpu.roll(z, 128 - sh, 1) * t_ref[2]


def _rope_t(dz, t_ref, sh):
    return dz * t_ref[0] + pltpu.roll(dz * t_ref[1], 128 - sh, 1) + pltpu.roll(dz * t_ref[2], sh, 1)


def _rope_tables(S):
    t = jnp.arange(S, dtype=jnp.int32)
    lane = jnp.arange(HD)

    def build(cos, sin, first):
        t0 = cos
        t1 = jnp.where(first[None, :], 0.0, sin)
        t2 = jnp.where(first[None, :], -sin, 0.0)
        return jnp.stack([jnp.tile(a, (1, 2)) for a in (t0, t1, t2)]).astype(F32)

    inv_a = THETA ** (-jnp.arange(0, HD // 2, 2, dtype=F32) / (HD // 2))
    ar = (t // GRID_W).astype(F32)[:, None] * inv_a[None, :]
    ac = (t % GRID_W).astype(F32)[:, None] * inv_a[None, :]
    cos_a = jnp.concatenate([jnp.cos(ar), jnp.cos(ar), jnp.cos(ac), jnp.cos(ac)], axis=1)
    sin_a = jnp.concatenate([jnp.sin(ar), jnp.sin(ar), jnp.sin(ac), jnp.sin(ac)], axis=1)
    tab_a = build(cos_a, sin_a, (lane % 32) < 16)
    inv_b = THETA ** (-jnp.arange(0, HD, 2, dtype=F32) / HD)
    ab = t.astype(F32)[:, None] * inv_b[None, :]
    cos_b = jnp.concatenate([jnp.cos(ab), jnp.cos(ab)], axis=1)
    sin_b = jnp.concatenate([jnp.sin(ab), jnp.sin(ab)], axis=1)
    tab_b = build(cos_b, sin_b, lane < 32)
    return tab_a, tab_b


def _in_proj(x, modv, n1g, win, qg2, kg2, tab_a, tab_b):
    S = x.shape[0]
    tm = min(512, S)
    tk = min(512, S)
    per = tk // tm
    ts = min(256, tm)

    def body(x_ref, mod_ref, g_ref, w_ref, qg_ref, kg_ref, ta_ref, tb_ref,
             h_ref, qar_ref, kar_ref, qa_ref, ka_ref, va_ref, qb_ref, kb_ref, vb_ref, ga_ref, gb_ref, qat_ref, kat_ref, vat_ref):
        seg = _seg_matrix(128, HD)

        def head_norm(z, g):
            ms = _seg_sum(z * z, seg) * (1.0 / HD)
            return (z * lax.rsqrt(ms + EPS)) * g

        subs = [slice(ts * u, ts * u + ts) for u in range(tm // ts)]
        hbs = []
        for rows in subs:
            xt = x_ref[rows, :]
            r = lax.rsqrt(jnp.mean(xt * xt, axis=-1, keepdims=True) + EPS)
            h = ((xt * r) * g_ref[...]) * (1.0 + mod_ref[1:2, :]) + mod_ref[0:1, :]
            hbs.append(h.astype(BF16))
            h_ref[rows, :] = hbs[-1]
        projs = [jnp.dot(hb, w_ref[...], preferred_element_type=F32) for hb in hbs]
        for rows, proj in zip(subs, projs):
            ta, tb = ta_ref[:, rows, :], tb_ref[:, rows, :]
            for p in range(4):
                z = proj[:, 128 * p:128 * p + 128]
                qar_ref[rows, 128 * p:128 * p + 128] = z.astype(BF16)
                qv = _rope(head_norm(z, qg_ref[...]), ta, 16) * QA_SCALE
                qa_ref[rows, 128 * p:128 * p + 128] = qv.astype(BF16)
                qat_ref[128 * p:128 * p + 128, rows] = qv.T.astype(BF16)
                zb = proj[:, 768 + 128 * p:768 + 128 * p + 128]
                qb_ref[rows, 128 * p:128 * p + 128] = (_rope(zb, tb, 32) * QA_SCALE).astype(BF16)
            z = proj[:, 512:640]
            kar_ref[rows, :] = z.astype(BF16)
            kv_ = _rope(head_norm(z, kg_ref[...]), ta, 16)
            ka_ref[rows, :] = kv_.astype(BF16)
            kat_ref[:, rows] = kv_.T.astype(BF16)
            va_ref[rows, :] = proj[:, 640:768].astype(BF16)
            vat_ref[:, rows] = proj[:, 640:768].T.astype(BF16)
            kb_ref[rows, :] = _rope(proj[:, 1280:1408], tb, 32).astype(BF16)
            vb_ref[rows, :] = proj[:, 1408:1536].astype(BF16)
            ga_ref[rows, :] = proj[:, 1536:2560].astype(BF16)
            gb_ref[rows, :] = proj[:, 2560:3584].astype(BF16)

    tab = pl.BlockSpec((3, tm, 128), lambda i: (0, i, 0))
    shapes = [(D, BF16), (BW, BF16), (128, BF16), (BW, BF16), (128, BF16), (128, BF16),
              (BW, BF16), (128, BF16), (128, BF16), (D, BF16), (D, BF16)]
    return pl.pallas_call(
        body, grid=(S // tm,),
        in_specs=[_rows(tm, D), _const((6, D)), _const((1, D)), _const((D, INW)), _const((1, 128)), _const((1, 128)), tab, tab],
        out_specs=[_rows(tm, w) for w, _ in shapes] + [pl.BlockSpec((BW, tm), lambda i: (0, i))]
        + [pl.BlockSpec((None, 128, tm), lambda i: (i // per, 0, i % per))] * 2,
        out_shape=[jax.ShapeDtypeStruct((S, w), dt) for w, dt in shapes] + [jax.ShapeDtypeStruct((BW, S), BF16)]
        + [jax.ShapeDtypeStruct((S // tk, 128, tk), BF16)] * 2,
        compiler_params=_params(1), name="in_proj",
    )(x, modv, n1g, win, qg2, kg2, tab_a, tab_b)


def _exchange_gather(block_refs, out_refs, send_sems, recv_sems, local_sems):
    x, y, c = _me()
    me = 4 * x + 2 * y + c

    def copies():
        own, out, arrive = [], [], []
        for a, (blk, dst) in enumerate(zip(block_refs, out_refs)):
            own.append(pltpu.make_async_copy(blk, dst.at[me], local_sems.at[a]))
            for k in range(1, N_DEV):
                sems = dict(send_sem=send_sems.at[a, k - 1], recv_sem=recv_sems.at[a, k - 1], device_id=_peer(k), device_id_type=MESH)
                out.append(pltpu.make_async_remote_copy(blk, dst.at[me], **sems))
                arrive.append(pltpu.make_async_remote_copy(blk, dst.at[me ^ k], **sems))
        return own, out, arrive

    def start():
        own, out, _ = copies()
        for cp in own + out:
            cp.start()

    def finish():
        own, out, arrive = copies()
        for cp in arrive:
            cp.wait_recv()
        for cp in out:
            cp.wait_send()
        for cp in own:
            cp.wait()

    return start, finish


def _exchange_scatter(chunk_refs, recv_refs, send_sems, recv_sems):
    x, y, c = _me()
    me = 4 * x + 2 * y + c

    def copies():
        return [pltpu.make_async_remote_copy(src.at[me ^ k], dst.at[k - 1], send_sems.at[a, k - 1], recv_sems.at[a, k - 1],
                                             device_id=_peer(k), device_id_type=MESH)
                for a, (src, dst) in enumerate(zip(chunk_refs, recv_refs)) for k in range(1, N_DEV)]

    def start():
        for cp in copies():
            cp.start()

    def finish():
        cps = copies()
        for cp in cps:
            cp.wait_recv()
        for cp in cps:
            cp.wait_send()

    return start, finish


def _exchange_sems(n):
    return [pltpu.SemaphoreType.DMA((n, N_DEV - 1)), pltpu.SemaphoreType.DMA((n, N_DEV - 1))]


def _attn_a_fwd(q, k, vt3, shards):
    S = q.shape[0]
    tq = min(512, S)
    nq = S // tq
    nk, _, tk = vt3.shape
    ONES = 16
    AHEAD = 2
    ns = len(shards)

    def body(q_ref, k_ref, vt_ref, *rest):
        w_hbm, (o_ref, lse_ref), wall_hbm = rest[:ns], rest[ns:ns + 2], rest[ns + 2:2 * ns + 2]
        st_sc, send_sems, recv_sems, local_sems = rest[2 * ns + 2:]
        start, finish = _exchange_gather(w_hbm, wall_hbm, send_sems, recv_sems, local_sems)
        pl.when(pl.program_id(0) == 0)(start)
        row8 = lax.broadcasted_iota(jnp.int32, (NH, tq), 0)
        lse_all = jnp.zeros((NH, tq), F32)
        ones = jnp.ones((ONES, tk), BF16)
        for kv in range(2):
            qs = []
            for pp in range(2):
                qp = q_ref[:, 128 * (2 * kv + pp):128 * (2 * kv + pp) + 128]
                qs += [qp[:, :HD], qp[:, HD:]]

            def keys(j, kv=kv):
                return k_ref[pl.ds(pl.multiple_of(j * tk, tk), tk), :][:, HD * kv:HD * kv + HD]

            def scores(kj, t, qs=qs):
                return lax.dot_general(kj, qs[t], NT, preferred_element_type=F32)

            def step(j, carry, kv=kv):
                kj = keys(j)
                kn = keys(jnp.minimum(j + 1, nk - 1))
                v1 = jnp.concatenate([vt_ref[j, HD * kv:HD * kv + HD, :], ones], axis=0)
                sts = [st_sc[t] for t in range(AHEAD)]
                new = []
                for t in range(GRP):
                    m, acc = carry[2 * t], carry[2 * t + 1]
                    if t + AHEAD < GRP:
                        sts.append(scores(kj, t + AHEAD))
                    st = sts[t]
                    mn = jnp.maximum(m, jnp.max(st, axis=0, keepdims=True))
                    pt = jnp.exp2(st - mn)
                    if t + AHEAD >= GRP:
                        st_sc[t + AHEAD - GRP] = scores(kn, t + AHEAD - GRP)
                    acc = jnp.exp2(m - mn) * acc + jnp.dot(v1, pt.astype(BF16), preferred_element_type=F32)
                    new += [mn, acc]
                return tuple(new)

            k0 = keys(0)
            for t in range(AHEAD):
                st_sc[t] = scores(k0, t)
            init = (jnp.full((1, tq), NEG, F32), jnp.zeros((HD + ONES, tq), F32)) * GRP
            res = lax.fori_loop(0, nk, step, init)
            outs = []
            for t in range(GRP):
                m, acc = res[2 * t], res[2 * t + 1]
                l = acc[HD:HD + 1, :]
                outs.append((acc[:HD, :] / l).T)
                lse_all = jnp.where(row8 == GRP * kv + t, m + jnp.log2(l), lse_all)
            o_ref[:, 256 * kv:256 * kv + 256] = jnp.concatenate(outs, axis=1).astype(BF16)
        lse_ref[...] = lse_all
        pl.when(pl.program_id(0) == nq - 1)(finish)

    any_spec = pl.BlockSpec(memory_space=pl.ANY)
    return pl.pallas_call(
        body, grid=(nq,),
        in_specs=[_rows(tq, BW), _const((S, 128)), _const((nk, 128, tk))] + [any_spec] * ns,
        out_specs=[_rows(tq, BW), pl.BlockSpec((NH, tq), lambda i: (0, i))] + [any_spec] * ns,
        out_shape=[jax.ShapeDtypeStruct((S, BW), BF16), jax.ShapeDtypeStruct((NH, S), F32)]
        + [jax.ShapeDtypeStruct((N_DEV,) + s.shape, s.dtype) for s in shards],
        scratch_shapes=[pltpu.VMEM((AHEAD, tk, tq), F32)] + _exchange_sems(ns) + [pltpu.SemaphoreType.DMA((ns,))],
        compiler_params=_params(1), name="attn_a_fwd",
    )(q, k, vt3, *shards)


def _window_mask(i, tq, S):
    W = tq + 2 * WIN
    r = lax.broadcasted_iota(jnp.int32, (tq, W), 0)
    c = lax.broadcasted_iota(jnp.int32, (tq, W), 1)
    kpos = i * tq - WIN + c
    return (jnp.abs(c - WIN - r) <= WIN) & (kpos >= 0) & (kpos < S)


TQ_B = 256


def _attn_b_fwd(q, kp, vpt3, sink2):
    S = q.shape[0]
    tq = min(TQ_B, S)
    W = tq + 2 * WIN
    nc = vpt3.shape[0]
    ONES = 16

    def body(q_ref, k_ref, vt_ref, sink_ref, o_ref, lse_ref):
        i = pl.program_id(0)
        off = pl.multiple_of(i * tq, tq)
        r = lax.broadcasted_iota(jnp.int32, (W, tq), 1)
        c = lax.broadcasted_iota(jnp.int32, (W, tq), 0)
        kpos = i * tq - WIN + c
        valid = (jnp.abs(c - WIN - r) <= WIN) & (kpos >= 0) & (kpos < S)
        kw = k_ref[pl.ds(off, W), :]
        vt = jnp.concatenate([vt_ref[i + half] for half in range(W // tq)], axis=1)
        ones = jnp.ones((ONES, W), BF16)
        row8 = lax.broadcasted_iota(jnp.int32, (NH, tq), 0)
        lse_all = jnp.zeros((NH, tq), F32)
        qs = []
        for p in range(4):
            qp = q_ref[:, 128 * p:128 * p + 128]
            qs += [qp[:, :HD], qp[:, HD:]]
        khs = [kw[:, HD * kv:HD * kv + HD] for kv in range(2)]
        v1s = [jnp.concatenate([vt[HD * kv:HD * kv + HD, :], ones], axis=0) for kv in range(2)]

        def scores(h):
            return lax.dot_general(khs[h // GRP], qs[h], NT, preferred_element_type=F32)

        ss = [scores(0), scores(1)]
        outs = []
        for h in range(NH):
            if h + 2 < NH:
                ss.append(scores(h + 2))
            st = jnp.where(valid, ss[h], NEG)
            sk = sink_ref[:, h:h + 1]
            m = jnp.maximum(jnp.max(st, axis=0, keepdims=True), sk)
            acc = jnp.dot(v1s[h // GRP], jnp.exp2(st - m).astype(BF16), preferred_element_type=F32)
            l = acc[HD:HD + 1, :] + jnp.exp2(sk - m)
            outs.append((acc[:HD, :] / l).T)
            lse_all = jnp.where(row8 == h, m + jnp.log2(l), lse_all)
        for p in range(4):
            o_ref[:, 128 * p:128 * p + 128] = jnp.concatenate(outs[2 * p:2 * p + 2], axis=1).astype(BF16)
        lse_ref[...] = lse_all

    return pl.pallas_call(
        body, grid=(S // tq,),
        in_specs=[_rows(tq, BW), _const((S + 2 * WIN, 128)), _const((nc, 128, tq)), _const((1, NH))],
        out_specs=[_rows(tq, BW), pl.BlockSpec((NH, tq), lambda i: (0, i))],
        out_shape=[jax.ShapeDtypeStruct((S, BW), BF16), jax.ShapeDtypeStruct((NH, S), F32)],
        compiler_params=_params(1), name="attn_b_fwd",
    )(q, kp, vpt3, sink2)


def _merge_out(ya, yb, ga, gb, x, modv, wb, wout):
    S = x.shape[0]
    tm = min(256, S)

    def body(ya_ref, yb_ref, ga_ref, gb_ref, x_ref, mod_ref, wb_ref, wo_ref, x1_ref, mg_ref, ua_ref, ub_ref):
        ua = jnp.dot(ya_ref[...], wb_ref[0], preferred_element_type=F32)
        ub = jnp.dot(yb_ref[...], wb_ref[1], preferred_element_type=F32)
        merged = jax.nn.sigmoid(ga_ref[...].astype(F32)) * ua + jax.nn.sigmoid(gb_ref[...].astype(F32)) * ub
        mb = merged.astype(BF16)
        ua_ref[...] = ua.astype(BF16)
        ub_ref[...] = ub.astype(BF16)
        mg_ref[...] = mb
        x1_ref[...] = x_ref[...] + mod_ref[2:3, :] * jnp.dot(mb, wo_ref[...], preferred_element_type=F32)

    return pl.pallas_call(
        body, grid=(S // tm,),
        in_specs=[_rows(tm, BW), _rows(tm, BW), _rows(tm, D), _rows(tm, D), _rows(tm, D), _const((6, D)),
                  _const((2, BW, D)), _const((D, D))],
        out_specs=[_rows(tm, D)] * 4,
        out_shape=[jax.ShapeDtypeStruct((S, D), F32)] + [jax.ShapeDtypeStruct((S, D), BF16)] * 3,
        compiler_params=_params(1), name="merge_out",
    )(ya, yb, ga, gb, x, modv, wb, wout)


def _mlp_fwd(x1, modv, n2g, wmi, wmo, fg, target):
    S = x1.shape[0]
    tm = min(512, S)
    tf = wmi.shape[2]
    nj = wmi.shape[0] // MLP_SHARDS

    def body(x1_ref, mod_ref, g_ref, wi_ref, wo_ref, fg_ref, t_ref, h2_ref, hp_ref, dx2_ref, st_ref, acc_ref):
        i, j = pl.program_id(0), pl.program_id(1)

        @pl.when(j == 0)
        def _():
            xt = x1_ref[...]
            r = lax.rsqrt(jnp.mean(xt * xt, axis=-1, keepdims=True) + EPS)
            h2 = ((xt * r) * g_ref[...]) * (1.0 + mod_ref[4:5, :]) + mod_ref[3:4, :]
            h2_ref[...] = h2.astype(BF16)
            acc_ref[...] = jnp.zeros_like(acc_ref)

        @pl.when((i == 0) & (j == 0))
        def _():
            st_ref[...] = jnp.zeros_like(st_ref)

        out = None
        for u in range(MLP_SHARDS):
            hp = jnp.dot(h2_ref[...], wi_ref[u], preferred_element_type=F32)
            hp_ref[:, tf * u:tf * u + tf] = hp.astype(BF16)
            hid = jnp.square(jnp.maximum(hp, 0.0))
            part = jnp.dot(hid.astype(BF16), wo_ref[u], preferred_element_type=F32)
            out = part if out is None else out + part
        acc_ref[...] += out

        @pl.when(j == nj - 1)
        def _():
            x2 = x1_ref[...] + mod_ref[5:6, :] * acc_ref[...]
            r3 = lax.rsqrt(jnp.mean(x2 * x2, axis=-1, keepdims=True) + EPS)
            xn = x2 * r3
            err = xn * fg_ref[...] - t_ref[...]
            dy = err * (1.0 / D)
            gy = dy * fg_ref[...]
            dx2_ref[...] = r3 * (gy - xn * jnp.mean(gy * xn, axis=-1, keepdims=True))
            st_ref[0:1, :] += jnp.sum(dy * xn, axis=0, keepdims=True)
            st_ref[1:2, :] += jnp.sum(err * err, axis=0, keepdims=True) * (0.5 / D)

    return pl.pallas_call(
        body, grid=(S // tm, nj),
        in_specs=[pl.BlockSpec((tm, D), lambda i, j: (i, 0)), _const((6, D)), _const((1, D)),
                  pl.BlockSpec((MLP_SHARDS, D, tf), lambda i, j: (j, 0, 0)), pl.BlockSpec((MLP_SHARDS, tf, D), lambda i, j: (j, 0, 0)),
                  _const((1, D)), pl.BlockSpec((tm, D), lambda i, j: (i, 0))],
        out_specs=[pl.BlockSpec((tm, D), lambda i, j: (i, 0)), pl.BlockSpec((tm, MLP_SHARDS * tf), lambda i, j: (i, j)),
                   pl.BlockSpec((tm, D), lambda i, j: (i, 0)), _const((8, D))],
        out_shape=[jax.ShapeDtypeStruct((S, D), BF16), jax.ShapeDtypeStruct((S, wmi.shape[0] * tf), BF16),
                   jax.ShapeDtypeStruct((S, D), F32), jax.ShapeDtypeStruct((8, D), F32)],
        scratch_shapes=[pltpu.VMEM((tm, D), F32)],
        compiler_params=_params(2), name="mlp_fwd",
    )(x1, modv, n2g, wmi, wmo, fg, target)


def _mlp_bwd(dx2, x1, hp, modv, n2g, wmi, wmo):
    S = x1.shape[0]
    tm = min(512, S)
    tf = wmi.shape[2]
    nj = wmi.shape[0] // MLP_SHARDS

    def body(dx2_ref, x1_ref, hp_ref, mod_ref, g_ref, wi_ref, wo_ref, dhp_ref, dx1_ref, st_ref, dmo_ref, acc_ref):
        i, j = pl.program_id(0), pl.program_id(1)

        @pl.when(j == 0)
        def _():
            dmo_ref[...] = (mod_ref[5:6, :] * dx2_ref[...]).astype(BF16)
            acc_ref[...] = jnp.zeros_like(acc_ref)

        @pl.when((i == 0) & (j == 0))
        def _():
            st_ref[...] = jnp.zeros_like(st_ref)

        out = None
        for u in range(MLP_SHARDS):
            sl = slice(tf * u, tf * u + tf)
            dhid = lax.dot_general(dmo_ref[...], wo_ref[u], NT, preferred_element_type=F32)
            dhp = (dhid * (2.0 * jnp.maximum(hp_ref[:, sl].astype(F32), 0.0))).astype(BF16)
            dhp_ref[:, sl] = dhp
            part = lax.dot_general(dhp, wi_ref[u], NT, preferred_element_type=F32)
            out = part if out is None else out + part
        acc_ref[...] += out

        @pl.when(j == nj - 1)
        def _():
            dh2 = acc_ref[...]
            xt = x1_ref[...]
            r = lax.rsqrt(jnp.mean(xt * xt, axis=-1, keepdims=True) + EPS)
            xn = xt * r
            st_ref[0:1, :] += jnp.sum(dh2, axis=0, keepdims=True)
            st_ref[1:2, :] += jnp.sum(dh2 * xn, axis=0, keepdims=True)
            dxn = dh2 * (g_ref[...] * (1.0 + mod_ref[4:5, :]))
            dx1_ref[...] = dx2_ref[...] + r * (dxn - xn * jnp.mean(dxn * xn, axis=-1, keepdims=True))

    return pl.pallas_call(
        body, grid=(S // tm, nj),
        in_specs=[pl.BlockSpec((tm, D), lambda i, j: (i, 0)), pl.BlockSpec((tm, D), lambda i, j: (i, 0)),
                  pl.BlockSpec((tm, MLP_SHARDS * tf), lambda i, j: (i, j)), _const((6, D)), _const((1, D)),
                  pl.BlockSpec((MLP_SHARDS, D, tf), lambda i, j: (j, 0, 0)), pl.BlockSpec((MLP_SHARDS, tf, D), lambda i, j: (j, 0, 0))],
        out_specs=[pl.BlockSpec((tm, MLP_SHARDS * tf), lambda i, j: (i, j)), pl.BlockSpec((tm, D), lambda i, j: (i, 0)), _const((8, D))],
        out_shape=[jax.ShapeDtypeStruct((S, wmi.shape[0] * tf), BF16), jax.ShapeDtypeStruct((S, D), F32),
                   jax.ShapeDtypeStruct((8, D), F32)],
        scratch_shapes=[pltpu.VMEM((tm, D), BF16), pltpu.VMEM((tm, D), F32)],
        compiler_params=_params(2), name="mlp_bwd",
    )(dx2, x1, hp, modv, n2g, wmi, wmo)


def _tn_matmul(a, b, tk, tn, name, relu_sq=False, dev_major=False):
    S, K = a.shape
    N = b.shape[1]
    ts = min(1024, S)
    ns = S // ts

    def body(a_ref, b_ref, o_ref):
        @pl.when(pl.program_id(2) == 0)
        def _():
            o_ref[...] = jnp.zeros_like(o_ref)

        at = a_ref[...]
        if relu_sq:
            at = jnp.square(jnp.maximum(at.astype(F32), 0.0)).astype(BF16)
        o_ref[...] += lax.dot_general(at, b_ref[...].astype(BF16), TN, preferred_element_type=F32)

    if dev_major:
        out_spec = pl.BlockSpec((None, tk, tn), lambda k, n, s: (n, k, 0))
        out_shape = jax.ShapeDtypeStruct((N // tn, K, tn), F32)
    else:
        out_spec = pl.BlockSpec((tk, tn), lambda k, n, s: (k, n))
        out_shape = jax.ShapeDtypeStruct((K, N), F32)
    return pl.pallas_call(
        body, grid=(K // tk, N // tn, ns),
        in_specs=[pl.BlockSpec((ts, tk), lambda k, n, s: (s, k)), pl.BlockSpec((ts, tn), lambda k, n, s: (s, n))],
        out_specs=out_spec, out_shape=out_shape,
        compiler_params=_params(3), name=name,
    )(a, b)


def _scale_gate(m, w, g, row, name):
    K = m.shape[0]
    tk = min(512, K)

    def body(m_ref, w_ref, mod_ref, dw_ref, dg_ref):
        @pl.when(pl.program_id(0) == 0)
        def _():
            dg_ref[...] = jnp.zeros_like(dg_ref)

        mt = m_ref[...]
        dw_ref[...] = mt * mod_ref[row:row + 1, :]
        dg_ref[0:1, :] += jnp.sum(mt * w_ref[...].astype(F32), axis=0, keepdims=True)

    return pl.pallas_call(
        body, grid=(K // tk,),
        in_specs=[_rows(tk, D), _rows(tk, D), _const((6, D))],
        out_specs=[_rows(tk, D), _const((8, D))],
        out_shape=[jax.ShapeDtypeStruct((K, D), F32), jax.ShapeDtypeStruct((8, D), F32)],
        compiler_params=_params(1), name=name,
    )(m, w, g)


def _merge_bwd(dx1, modv, ga, gb, ua, ub, ya, yb, wb, wout):
    S = dx1.shape[0]
    tm = min(512, S)

    def body(dx1_ref, mod_ref, ga_ref, gb_ref, ua_ref, ub_ref, ya_ref, yb_ref, wb_ref, wo_ref,
             dua_ref, dub_ref, dga_ref, dgb_ref, dya_ref, dyb_ref, dla_ref, dlb_ref, dyat_ref):
        dao = (mod_ref[2:3, :] * dx1_ref[...]).astype(BF16)
        dm = lax.dot_general(dao, wo_ref[...], NT, preferred_element_type=F32)
        r = lax.broadcasted_iota(jnp.int32, (BW, NH), 0) // HD
        c = lax.broadcasted_iota(jnp.int32, (BW, NH), 1)
        head_of = (r == c).astype(F32)
        for br, (g_ref, u_ref, y_ref, du_ref, dg_ref, dy_ref, dl_ref) in enumerate((
                (ga_ref, ua_ref, ya_ref, dua_ref, dga_ref, dya_ref, dla_ref),
                (gb_ref, ub_ref, yb_ref, dub_ref, dgb_ref, dyb_ref, dlb_ref))):
            sg = jax.nn.sigmoid(g_ref[...].astype(F32))
            du = (dm * sg).astype(BF16)
            du_ref[...] = du
            dg_ref[...] = (dm * u_ref[...].astype(F32) * sg * (1.0 - sg)).astype(BF16)
            dy = lax.dot_general(du, wb_ref[br], NT, preferred_element_type=F32)
            dyb16 = dy.astype(BF16)
            dy_ref[...] = dyb16
            if br == 0:
                dyat_ref[...] = dy.T.astype(BF16)
            dl_ref[...] = jnp.dot(dyb16.astype(F32) * y_ref[...].astype(F32), head_of,
                                  preferred_element_type=F32, precision=lax.Precision.HIGHEST)

    return pl.pallas_call(
        body, grid=(S // tm,),
        in_specs=[_rows(tm, D), _const((6, D)), _rows(tm, D), _rows(tm, D), _rows(tm, D), _rows(tm, D),
                  _rows(tm, BW), _rows(tm, BW), _const((2, BW, D)), _const((D, D))],
        out_specs=[_rows(tm, D)] * 4 + [_rows(tm, BW)] * 2 + [_rows(tm, NH)] * 2 + [pl.BlockSpec((BW, tm), lambda i: (0, i))],
        out_shape=[jax.ShapeDtypeStruct((S, D), BF16)] * 4 + [jax.ShapeDtypeStruct((S, BW), BF16)] * 2
        + [jax.ShapeDtypeStruct((S, NH), F32)] * 2 + [jax.ShapeDtypeStruct((BW, S), BF16)],
        compiler_params=_params(1), name="merge_bwd",
    )(dx1, modv, ga, gb, ua, ub, ya, yb, wb, wout)


def _attn_a_bwd(q, qt, k, kt3, v, do, dot_, lse, delta, chunks):
    S = q.shape[0]
    tq = min(512, S)
    tk = min(512, S)
    nq, nk = S // tq, S // tk

    nc = len(chunks)

    def body(q_ref, qt_ref, do_ref, dot_ref, lse_ref, dl_ref, k_ref, v_ref, kt_ref, *rest):
        g_hbm, (dq_ref, dk_hbm, dv_hbm), recv_hbm = rest[:nc], rest[nc:nc + 3], rest[nc + 3:2 * nc + 3]
        dk_sc, dv_sc, sem, send_sems, recv_sems = rest[2 * nc + 3:]
        i = pl.program_id(0)
        start, finish = _exchange_scatter(g_hbm, recv_hbm, send_sems, recv_sems)

        @pl.when(i == 0)
        def _():
            start()
            dk_sc[...] = jnp.zeros_like(dk_sc)
            dv_sc[...] = jnp.zeros_like(dv_sc)

        for kv in range(2):
            qg = q_ref[:, 256 * kv:256 * kv + 256]
            dog = do_ref[:, 256 * kv:256 * kv + 256]
            heads = []
            for t in range(GRP):
                h = GRP * kv + t
                heads.append((qg[:, HD * t:HD * t + HD], dog[:, HD * t:HD * t + HD],
                              qt_ref[HD * h:HD * h + HD, :], dot_ref[HD * h:HD * h + HD, :],
                              lse_ref[:, h:h + 1], dl_ref[:, h:h + 1]))

            def step(j, carry, kv=kv, heads=heads):
                off = pl.multiple_of(j * tk, tk)
                kj = k_ref[pl.ds(off, tk), :][:, HD * kv:HD * kv + HD]
                vj = v_ref[pl.ds(off, tk), :][:, HD * kv:HD * kv + HD]
                kjt = kt_ref[j, HD * kv:HD * kv + HD, :]
                dkt = jnp.zeros((HD, tk), F32)
                dvt = jnp.zeros((HD, tk), F32)
                new = []

                def logits(t):
                    return (lax.dot_general(heads[t][0], kj, NT, preferred_element_type=F32),
                            lax.dot_general(heads[t][1], vj, NT, preferred_element_type=F32))

                sd = [logits(0)]
                for t, (qh, doh, qth, doth, lse_h, dl_h) in enumerate(heads):
                    if t + 1 < GRP:
                        sd.append(logits(t + 1))
                    s, dp = sd[t]
                    pm = jnp.exp2(s - lse_h)
                    ds = (pm * (dp - dl_h)).astype(BF16)
                    dvt = dvt + jnp.dot(doth, pm.astype(BF16), preferred_element_type=F32)
                    dkt = dkt + jnp.dot(qth, ds, preferred_element_type=F32)
                    new.append(carry[t] + lax.dot_general(kjt, ds, NT, preferred_element_type=F32))
                dk_sc[j, HD * kv:HD * kv + HD, :] += dkt
                dv_sc[j, HD * kv:HD * kv + HD, :] += dvt
                return tuple(new)

            res = lax.fori_loop(0, nk, step, (jnp.zeros((HD, tq), F32),) * GRP)
            for t in range(GRP):
                dq_ref[HD * (GRP * kv + t):HD * (GRP * kv + t) + HD, :] = res[t]

        @pl.when(i == nq - 1)
        def _():
            c1 = pltpu.make_async_copy(dk_sc, dk_hbm, sem.at[0])
            c2 = pltpu.make_async_copy(dv_sc, dv_hbm, sem.at[1])
            c1.start()
            c2.start()
            c1.wait()
            c2.wait()
            finish()

    any_spec = pl.BlockSpec(memory_space=pl.ANY)
    cols = pl.BlockSpec((BW, tq), lambda i: (0, i))
    return pl.pallas_call(
        body, grid=(nq,),
        in_specs=[_rows(tq, BW), cols, _rows(tq, BW), cols, _rows(tq, NH), _rows(tq, NH), _const((S, 128)), _const((S, 128)),
                  _const((nk, 128, tk))] + [any_spec] * nc,
        out_specs=[cols, any_spec, any_spec] + [any_spec] * nc,
        out_shape=[jax.ShapeDtypeStruct((BW, S), F32), jax.ShapeDtypeStruct((nk, 128, tk), F32),
                   jax.ShapeDtypeStruct((nk, 128, tk), F32)]
        + [jax.ShapeDtypeStruct((N_DEV - 1,) + c.shape[1:], c.dtype) for c in chunks],
        scratch_shapes=[pltpu.VMEM((nk, 128, tk), F32), pltpu.VMEM((nk, 128, tk), F32), pltpu.SemaphoreType.DMA((2,))]
        + _exchange_sems(nc),
        compiler_params=_params(1), name="attn_a_bwd",
    )(q, qt, do, dot_, lse, delta, k, v, kt3, *chunks)


def _attn_b_bwd(q, kp, vp, sink2, do, lse, delta):
    S = q.shape[0]
    tq = min(TQ_B, S)
    W = tq + 2 * WIN
    nq = S // tq
    nc = (S + 2 * WIN) // tq

    def body(q_ref, k_ref, v_ref, sink_ref, do_ref, lse_ref, dl_ref, dq_ref, dk_hbm, dv_hbm, ds_ref, dk_sc, dv_sc, sem):
        i = pl.program_id(0)

        @pl.when(i == 0)
        def _():
            dk_sc[...] = jnp.zeros_like(dk_sc)
            dv_sc[...] = jnp.zeros_like(dv_sc)
            ds_ref[...] = jnp.zeros_like(ds_ref)

        off = pl.multiple_of(i * tq, tq)
        valid = _window_mask(i, tq, S)
        kw = k_ref[pl.ds(off, W), :]
        vw = v_ref[pl.ds(off, W), :]
        lse_i = lse_ref[...]
        dl_i = dl_ref[...]
        qa = q_ref[...]
        doa = do_ref[...]
        qt = qa.astype(F32).T.astype(BF16)
        dot_ = doa.astype(F32).T.astype(BF16)
        khs = [kw[:, HD * kv:HD * kv + HD] for kv in range(2)]
        vhs = [vw[:, HD * kv:HD * kv + HD] for kv in range(2)]

        def logits(h):
            return (lax.dot_general(qa[:, HD * h:HD * h + HD], khs[h // GRP], NT, preferred_element_type=F32),
                    lax.dot_general(doa[:, HD * h:HD * h + HD], vhs[h // GRP], NT, preferred_element_type=F32))

        sd = [logits(0)]
        dqs = []
        dkt = [jnp.zeros((HD, W), F32), jnp.zeros((HD, W), F32)]
        dvt = [jnp.zeros((HD, W), F32), jnp.zeros((HD, W), F32)]
        for h in range(NH):
            kv = h // GRP
            if h + 1 < NH:
                sd.append(logits(h + 1))
            s, dp = sd[h]
            pm = jnp.exp2(jnp.where(valid, s, NEG) - lse_i[:, h:h + 1])
            ds = (pm * (dp - dl_i[:, h:h + 1])).astype(BF16)
            dvt[kv] = dvt[kv] + jnp.dot(dot_[HD * h:HD * h + HD, :], pm.astype(BF16), preferred_element_type=F32)
            dkt[kv] = dkt[kv] + jnp.dot(qt[HD * h:HD * h + HD, :], ds, preferred_element_type=F32)
            dqs.append(jnp.dot(ds, khs[kv], preferred_element_type=F32))
        for p in range(4):
            dq_ref[:, 128 * p:128 * p + 128] = jnp.concatenate(dqs[2 * p:2 * p + 2], axis=1)
        for half in range(W // tq):
            dk_sc[i + half] += jnp.concatenate([d[:, tq * half:tq * half + tq] for d in dkt], axis=0)
            dv_sc[i + half] += jnp.concatenate([d[:, tq * half:tq * half + tq] for d in dvt], axis=0)
        psd = jnp.exp2(sink_ref[...] - lse_i) * dl_i
        r = lax.broadcasted_iota(jnp.int32, (NH, 128), 0)
        c = lax.broadcasted_iota(jnp.int32, (NH, 128), 1)
        row = jnp.dot(jnp.sum(psd, axis=0, keepdims=True), (r == c).astype(F32),
                      preferred_element_type=F32, precision=lax.Precision.HIGHEST)
        ds_ref[...] -= jnp.broadcast_to(row, (8, 128))

        @pl.when(i == nq - 1)
        def _():
            c1 = pltpu.make_async_copy(dk_sc, dk_hbm, sem.at[0])
            c2 = pltpu.make_async_copy(dv_sc, dv_hbm, sem.at[1])
            c1.start()
            c2.start()
            c1.wait()
            c2.wait()

    any_spec = pl.BlockSpec(memory_space=pl.ANY)
    return pl.pallas_call(
        body, grid=(nq,),
        in_specs=[_rows(tq, BW), _const((S + 2 * WIN, 128)), _const((S + 2 * WIN, 128)), _const((1, NH)),
                  _rows(tq, BW), _rows(tq, NH), _rows(tq, NH)],
        out_specs=[_rows(tq, BW), any_spec, any_spec, _const((8, 128))],
        out_shape=[jax.ShapeDtypeStruct((S, BW), F32), jax.ShapeDtypeStruct((nc, 128, tq), F32),
                   jax.ShapeDtypeStruct((nc, 128, tq), F32), jax.ShapeDtypeStruct((8, 128), F32)],
        scratch_shapes=[pltpu.VMEM((nc, 128, tq), F32), pltpu.VMEM((nc, 128, tq), F32), pltpu.SemaphoreType.DMA((2,))],
        compiler_params=_params(1), name="attn_b_bwd",
    )(q, kp, vp, sink2, do, lse, delta)


def _qk_bwd(dqa_t, dka_t3, dva_t3, dqb, dkb, dvb, qar, kar, qg2, kg2, tab_a, tab_b, dga, dgb):
    S = dqb.shape[0]
    tm = min(256, S)
    per = dka_t3.shape[2] // tm

    def body(dqa_ref, dka_ref, dva_ref, dqb_ref, dkb_ref, dvb_ref, qar_ref, kar_ref, qg_ref, kg_ref, ta_ref, tb_ref,
             dga_ref, dgb_ref, dp_ref, st_ref):
        @pl.when(pl.program_id(0) == 0)
        def _():
            st_ref[...] = jnp.zeros_like(st_ref)

        seg = _seg_matrix(128, HD)

        def norm_bwd(dz_rot, raw, g):
            dzn = _rope_t(dz_rot, ta_ref, 16)
            raw = raw.astype(F32)
            rr = lax.rsqrt(_seg_sum(raw * raw, seg) * (1.0 / HD) + EPS)
            zhat = raw * rr
            dzh = dzn * g
            draw = rr * (dzh - zhat * (_seg_sum(dzh * zhat, seg) * (1.0 / HD)))
            return draw, jnp.sum(dzn * zhat, axis=0, keepdims=True)

        gq = jnp.zeros((1, 128), F32)
        for p in range(4):
            sl = slice(128 * p, 128 * p + 128)
            draw, gsum = norm_bwd(dqa_ref[sl, :].T * 0.125, qar_ref[:, sl], qg_ref[...])
            gq = gq + gsum
            dp_ref[:, sl] = draw.astype(BF16)
            dp_ref[:, 768 + 128 * p:768 + 128 * p + 128] = _rope_t(dqb_ref[:, sl] * 0.125, tb_ref, 32).astype(BF16)
        draw, gk = norm_bwd(dka_ref[...].T * LN2, kar_ref[...], kg_ref[...])
        dp_ref[:, 512:640] = draw.astype(BF16)
        dp_ref[:, 640:768] = dva_ref[...].T.astype(BF16)
        dp_ref[:, 1280:1408] = _rope_t(dkb_ref[...] * LN2, tb_ref, 32).astype(BF16)
        dp_ref[:, 1408:1536] = dvb_ref[...].astype(BF16)
        dp_ref[:, 1536:2560] = dga_ref[...]
        dp_ref[:, 2560:3584] = dgb_ref[...]
        st_ref[0:1, :] += gq
        st_ref[1:2, :] += gk

    tab = pl.BlockSpec((3, tm, 128), lambda i: (0, i, 0))
    chunk_t = pl.BlockSpec((None, 128, tm), lambda i: (i // per, 0, i % per))
    return pl.pallas_call(
        body, grid=(S // tm,),
        in_specs=[pl.BlockSpec((BW, tm), lambda i: (0, i)), chunk_t, chunk_t, _rows(tm, BW), _rows(tm, 128), _rows(tm, 128),
                  _rows(tm, BW), _rows(tm, 128), _const((1, 128)), _const((1, 128)), tab, tab, _rows(tm, D), _rows(tm, D)],
        out_specs=[_rows(tm, INW), _const((8, 128))],
        out_shape=[jax.ShapeDtypeStruct((S, INW), BF16), jax.ShapeDtypeStruct((8, 128), F32)],
        compiler_params=_params(1), name="qk_bwd",
    )(dqa_t, dka_t3, dva_t3, dqb, dkb, dvb, qar, kar, qg2, kg2, tab_a, tab_b, dga, dgb)


def _in_bwd(dproj, win, x, dx1, modv, n1g, chunks):
    S = x.shape[0]
    tm = min(512, S)
    n = S // tm
    ts = min(256, tm)

    def body(dp_ref, w_ref, x_ref, dx1_ref, mod_ref, g_ref, c_hbm, gx_ref, st_ref, recv_hbm, send_sems, recv_sems):
        start, finish = _exchange_scatter([c_hbm], [recv_hbm], send_sems, recv_sems)

        @pl.when(pl.program_id(0) == 0)
        def _():
            start()
            st_ref[...] = jnp.zeros_like(st_ref)

        subs = [slice(ts * u, ts * u + ts) for u in range(tm // ts)]
        dhs = [lax.dot_general(dp_ref[rows, :], w_ref[...], NT, preferred_element_type=F32) for rows in subs]
        for rows, dh in zip(subs, dhs):
            xt = x_ref[rows, :]
            r = lax.rsqrt(jnp.mean(xt * xt, axis=-1, keepdims=True) + EPS)
            xn = xt * r
            st_ref[0:1, :] += jnp.sum(dh, axis=0, keepdims=True)
            st_ref[1:2, :] += jnp.sum(dh * xn, axis=0, keepdims=True)
            dxn = dh * (g_ref[...] * (1.0 + mod_ref[1:2, :]))
            gx_ref[rows, :] = dx1_ref[rows, :] + r * (dxn - xn * jnp.mean(dxn * xn, axis=-1, keepdims=True))
        pl.when(pl.program_id(0) == n - 1)(finish)

    any_spec = pl.BlockSpec(memory_space=pl.ANY)
    return pl.pallas_call(
        body, grid=(n,),
        in_specs=[_rows(tm, INW), _const((D, INW)), _rows(tm, D), _rows(tm, D), _const((6, D)), _const((1, D)), any_spec],
        out_specs=[_rows(tm, D), _const((8, D)), any_spec],
        out_shape=[jax.ShapeDtypeStruct((S, D), F32), jax.ShapeDtypeStruct((8, D), F32),
                   jax.ShapeDtypeStruct((N_DEV - 1,) + chunks.shape[1:], chunks.dtype)],
        scratch_shapes=_exchange_sems(1),
        compiler_params=_params(1), name="in_bwd",
    )(dproj, win, x, dx1, modv, n1g, chunks)


def _pack_small(st1, st2, stf, dg1, dg2, stqk, dsink, modv, n1g, n2g):
    def body(st1_ref, st2_ref, stf_ref, dg1_ref, dg2_ref, qk_ref, ds_ref, mod_ref, g1_ref, g2_ref, o_ref):
        a1, b1 = st1_ref[0:1, :], st1_ref[1:2, :]
        a2, b2 = st2_ref[0:1, :], st2_ref[1:2, :]
        r = lax.broadcasted_iota(jnp.int32, (128, D), 0)
        c = lax.broadcasted_iota(jnp.int32, (128, D), 1)
        fold_q = (c == r % HD).astype(F32)
        fold_k = (c == HD + r % HD).astype(F32)
        keep = (c == r).astype(F32)

        def place(v, sel):
            return jnp.dot(v, sel, preferred_element_type=F32, precision=lax.Precision.HIGHEST)

        loss = jnp.sum(stf_ref[1:2, :], axis=1, keepdims=True)
        lane = lax.broadcasted_iota(jnp.int32, (1, D), 1)
        rows = [a1, g1_ref[...] * b1, dg1_ref[0:1, :], a2, g2_ref[...] * b2, dg2_ref[0:1, :],
                (1.0 + mod_ref[1:2, :]) * b1, (1.0 + mod_ref[4:5, :]) * b2, stf_ref[0:1, :],
                place(qk_ref[0:1, :], fold_q) + place(qk_ref[1:2, :], fold_k),
                place(ds_ref[0:1, :], keep),
                jnp.where(lane == 0, loss, 0.0)]
        rows += [jnp.zeros((1, D), F32)] * (SMALL_ROWS - len(rows))
        for n, v in enumerate(rows):
            o_ref[n:n + 1, :] = v

    return pl.pallas_call(
        body, out_shape=jax.ShapeDtypeStruct((SMALL_ROWS, D), F32),
        compiler_params=pltpu.CompilerParams(vmem_limit_bytes=V7X_VMEM_LIMIT), name="pack_small",
    )(st1, st2, stf, dg1, dg2, stqk, dsink, modv, n1g, n2g)


def _wada_grad(silu_all, dmod_cols):
    def body(a_ref, b_ref, o_ref):
        o_ref[...] = lax.dot_general(a_ref[...], b_ref[...], TN, preferred_element_type=F32, precision=lax.Precision.HIGHEST)

    return pl.pallas_call(
        body, out_shape=jax.ShapeDtypeStruct((D, dmod_cols.shape[1]), F32),
        compiler_params=pltpu.CompilerParams(vmem_limit_bytes=V7X_VMEM_LIMIT), name="wada_grad",
    )(silu_all, dmod_cols)


def _adamw_sum(parts, w, m, v, name):
    R, C = w.shape
    tr = R if R <= 64 else next(t for t in (256, 128, 64, 32, 16, 8) if R % t == 0)
    n = len(parts)
    dyn = [idx for _, idx in parts if idx is not None and not isinstance(idx, int)]
    b1c = 1.0 - ADAM_B1 ** ADAM_STEP
    b2c = 1.0 - ADAM_B2 ** ADAM_STEP

    def body(*refs):
        refs = refs[len(dyn):]
        g = refs[0][...].astype(F32)
        for k in range(1, n):
            g = g + refs[k][...].astype(F32)
        w_ref, m_ref, v_ref, g_out, d_out, m_out, v_out = refs[n:]
        mn = ADAM_B1 * m_ref[...] + (1.0 - ADAM_B1) * g
        vn = ADAM_B2 * v_ref[...] + (1.0 - ADAM_B2) * jnp.square(g)
        g_out[...] = g
        m_out[...] = mn
        v_out[...] = vn
        d_out[...] = -ADAM_LR * ((mn / b1c) / (jnp.sqrt(vn / b2c) + ADAM_EPS) + ADAM_WD * w_ref[...])

    in_specs = []
    nd = 0
    for a, idx in parts:
        if idx is None:
            in_specs.append(pl.BlockSpec((tr, C), lambda i, *s: (i, 0)))
        elif isinstance(idx, int):
            in_specs.append(pl.BlockSpec((None, tr, C), lambda i, *s, idx=idx: (idx, i, 0)))
        else:
            in_specs.append(pl.BlockSpec((None, tr, C), lambda i, *s, nd=nd: (s[nd][0], i, 0)))
            nd += 1
    blk = pl.BlockSpec((tr, C), lambda i, *s: (i, 0))
    grid_spec = pltpu.PrefetchScalarGridSpec(
        num_scalar_prefetch=len(dyn), grid=(R // tr,), in_specs=in_specs + [blk] * 3, out_specs=[blk] * 4)
    return pl.pallas_call(
        body, grid_spec=grid_spec, out_shape=[jax.ShapeDtypeStruct((R, C), F32)] * 4,
        compiler_params=_params(1), name=name,
    )(*dyn, *[a for a, _ in parts], w, m, v)


def _me():
    return lax.axis_index("x"), lax.axis_index("y"), lax.axis_index("c")


def _peer(k):
    x, y, c = _me()
    return (x ^ ((k >> 2) & 1), y ^ ((k >> 1) & 1), c ^ (k & 1))


def _ada_exchange(c_row, w_ada, b_rows):
    NW = w_ada.shape[1]

    def body(c_ref, w_ref, b_ref, sall_ref, mod_ref, src_ref, mp_ref, send1, recv1, send2, recv2):
        x, y, c = _me()
        me = 4 * x + 2 * y + c
        cv = c_ref[...]
        src_ref[...] = jnp.broadcast_to(cv * jax.nn.sigmoid(cv), (8, D))
        mine = pl.ds(pl.multiple_of(me * 8, 8), 8)
        sall_ref[mine, :] = src_ref[...]
        sends = [pltpu.make_async_remote_copy(src_ref, sall_ref.at[mine, :], send1.at[k - 1], recv1.at[k - 1],
                                              device_id=_peer(k), device_id_type=MESH) for k in range(1, N_DEV)]
        for cp in sends:
            cp.start()
        for k in range(1, N_DEV):
            theirs = pl.ds(pl.multiple_of((me ^ k) * 8, 8), 8)
            pltpu.make_async_remote_copy(src_ref, sall_ref.at[theirs, :], send1.at[k - 1], recv1.at[k - 1],
                                         device_id=_peer(k), device_id_type=MESH).wait_recv()
        for cp in sends:
            cp.wait_send()
        mp_ref[...] = jnp.dot(sall_ref[...], w_ref[...], preferred_element_type=F32, precision=lax.Precision.HIGHEST)
        mod_ref[mine, :] = mp_ref[mine, :] + b_ref[mine, :]
        sends = []
        for k in range(1, N_DEV):
            theirs = pl.ds(pl.multiple_of((me ^ k) * 8, 8), 8)
            sends.append(pltpu.make_async_remote_copy(mp_ref.at[theirs, :], mod_ref.at[mine, :], send2.at[k - 1], recv2.at[k - 1],
                                                      device_id=_peer(k), device_id_type=MESH))
        for cp in sends:
            cp.start()
        for k in range(1, N_DEV):
            theirs = pl.ds(pl.multiple_of((me ^ k) * 8, 8), 8)
            pltpu.make_async_remote_copy(mp_ref.at[mine, :], mod_ref.at[theirs, :], send2.at[k - 1], recv2.at[k - 1],
                                         device_id=_peer(k), device_id_type=MESH).wait_recv()
            mod_ref[theirs, :] = mod_ref[theirs, :] + b_ref[theirs, :]
        for cp in sends:
            cp.wait_send()

    vm = pl.BlockSpec(memory_space=pltpu.VMEM)
    return pl.pallas_call(
        body, in_specs=[vm, vm, vm], out_specs=[vm, vm],
        out_shape=[jax.ShapeDtypeStruct((8 * N_DEV, D), F32), jax.ShapeDtypeStruct((8 * N_DEV, NW), F32)],
        scratch_shapes=[pltpu.VMEM((8, D), F32), pltpu.VMEM((8 * N_DEV, NW), F32)]
        + [pltpu.SemaphoreType.DMA((N_DEV - 1,))] * 4,
        compiler_params=pltpu.CompilerParams(vmem_limit_bytes=V7X_VMEM_LIMIT), name="ada_exchange",
    )(c_row, w_ada, b_rows)


def _weight_gather(shard):
    def body(x_ref, out_ref, send_sems, recv_sems, local_sem):
        x, y, c = _me()
        me, sibling = (x, y, c), (x, y, 1 - c)
        chips = [(1 - x, y), (x, 1 - y), (1 - x, 1 - y)]

        def slot(px, py, pc):
            return out_ref.at[4 * px + 2 * py + pc]

        def copy(k, block, to, src=None):
            return pltpu.make_async_remote_copy(
                src_ref=slot(*block) if src is None else src, dst_ref=slot(*block),
                send_sem=send_sems.at[k], recv_sem=recv_sems.at[k], device_id=to, device_id_type=MESH)

        mine = pltpu.make_async_copy(x_ref, slot(*me), local_sem)
        mine.start()
        first = [copy(0, me, sibling, src=x_ref)]
        first += [copy(1 + j, me, (*chip, c), src=x_ref) for j, chip in enumerate(chips)]
        for cp in first:
            cp.start()
        passed = [copy(4 + j, (*chip, c), sibling) for j, chip in enumerate(chips)]
        for j, chip in enumerate(chips):
            copy(1 + j, (*chip, c), me).wait_recv()
            passed[j].start()
        copy(0, sibling, me).wait_recv()
        for j, chip in enumerate(chips):
            copy(4 + j, (*chip, 1 - c), me).wait_recv()
        for cp in first + passed:
            cp.wait_send()
        mine.wait()

    any_spec = pl.BlockSpec(memory_space=pl.ANY)
    return pl.pallas_call(
        body, in_specs=[any_spec], out_specs=any_spec,
        out_shape=jax.ShapeDtypeStruct((N_DEV,) + shard.shape, shard.dtype),
        scratch_shapes=[pltpu.SemaphoreType.DMA((7,)), pltpu.SemaphoreType.DMA((7,)), pltpu.SemaphoreType.DMA],
        name="weight_gather",
    )(shard)


def _small_gather(block):
    def body(b_ref, out_ref, send_sems, recv_sems):
        x, y, c = _me()
        me = 4 * x + 2 * y + c
        out_ref[me] = b_ref[...]
        sends = [pltpu.make_async_remote_copy(b_ref, out_ref.at[me], send_sems.at[k - 1], recv_sems.at[k - 1],
                                              device_id=_peer(k), device_id_type=MESH) for k in range(1, N_DEV)]
        for cp in sends:
            cp.start()
        for k in range(1, N_DEV):
            pltpu.make_async_remote_copy(b_ref, out_ref.at[me ^ k], send_sems.at[k - 1], recv_sems.at[k - 1],
                                         device_id=_peer(k), device_id_type=MESH).wait_recv()
        for cp in sends:
            cp.wait_send()

    vm = pl.BlockSpec(memory_space=pltpu.VMEM)
    return pl.pallas_call(
        body, in_specs=[vm], out_specs=vm,
        out_shape=jax.ShapeDtypeStruct((N_DEV,) + block.shape, block.dtype),
        scratch_shapes=[pltpu.SemaphoreType.DMA((N_DEV - 1,)), pltpu.SemaphoreType.DMA((N_DEV - 1,))],
        name="small_gather",
    )(block)


def _pack_small_params(b_ada, n1, n2, fg, qn, kn, sink):
    z = jnp.zeros((SMALL_ROWS, D), F32)
    z = z.at[0:6].set(b_ada.reshape(6, D)).at[6].set(n1.reshape(D)).at[7].set(n2.reshape(D)).at[8].set(fg.reshape(D))
    z = z.at[9, 0:HD].set(qn.reshape(HD)).at[9, HD:2 * HD].set(kn.reshape(HD)).at[10, 0:NH].set(sink.reshape(NH))
    return z


def _unpack_small(p):
    return (p[0:6].reshape(1, 6 * D), p[6].reshape(1, D), p[9, 0:HD].reshape(1, HD), p[9, HD:2 * HD].reshape(1, HD),
            p[10, 0:NH].reshape(1, NH), p[7].reshape(1, D), p[8].reshape(D))


def kernel(x, c, w_ada, b_ada, norm1_g, w_in, q_norm_a, k_norm_a, sink_b, w_branch, w_out, norm2_g, w_mlp_in, w_mlp_out, final_g, loss_target, m_w_ada, m_b_ada, m_norm1_g, m_w_in, m_q_norm_a, m_k_norm_a, m_sink_b, m_w_branch, m_w_out, m_norm2_g, m_w_mlp_in, m_w_mlp_out, m_final_g, v_w_ada, v_b_ada, v_norm1_g, v_w_in, v_q_norm_a, v_k_norm_a, v_sink_b, v_w_branch, v_w_out, v_norm2_g, v_w_mlp_in, v_w_mlp_out, v_final_g):
    S = x.shape[1]
    xs = x.reshape(S, D)
    tgt = loss_target.reshape(S, D)
    ax, ay, ac = lax.axis_index("x"), lax.axis_index("y"), lax.axis_index("c")
    me = 4 * ax + 2 * ay + ac
    me1 = me.reshape(1).astype(jnp.int32)
    NW = w_ada.shape[2]
    NI = w_in.shape[2]

    silu64, mod64 = _ada_exchange(c.reshape(1, D), w_ada.reshape(D, NW),
                                  jnp.repeat(b_ada.reshape(N_DEV, NW), 8, axis=0))
    silu_all = silu64[0::8]
    modv = mod64[0::8].reshape(6, D)

    win = _weight_gather(w_in[0].astype(BF16)).transpose(1, 0, 2).reshape(D, INW)
    rest_shards = tuple(w[0].astype(BF16) for w in (w_branch, w_out, w_mlp_in, w_mlp_out))

    tab_a, tab_b = _rope_tables(S)
    qg2 = jnp.tile(q_norm_a.reshape(1, HD), (1, 2))
    kg2 = jnp.tile(k_norm_a.reshape(1, HD), (1, 2))
    n1g = norm1_g.reshape(1, D)
    n2g = norm2_g.reshape(1, D)
    fg = final_g.reshape(1, D)
    sink2 = sink_b.reshape(1, NH) * LOG2E

    h, qar, kar, qa, ka, va, qb, kb, vb, ga, gb, qa_t, ka_t3, va_t3 = _in_proj(xs, modv, n1g, win, qg2, kg2, tab_a, tab_b)
    ya, lse_at, wb, wout, wmi, wmo = _attn_a_fwd(qa, ka, va_t3, rest_shards)
    lse_a = lse_at.T
    wb = wb.transpose(1, 2, 0, 3).reshape(2, BW, D)
    wout = wout.reshape(D, D)
    pad = ((WIN, WIN), (0, 0))
    kbp, vbp = jnp.pad(kb, pad), jnp.pad(vb, pad)
    tb = min(TQ_B, S)
    yb, lse_bt = _attn_b_fwd(qb, kbp, vbp.reshape((S + 2 * WIN) // tb, tb, 128).transpose(0, 2, 1), sink2)
    lse_b = lse_bt.T
    x1, merged, ua, ub = _merge_out(ya, yb, ga, gb, xs, modv, wb, wout)
    h2, hp, dx2, stf = _mlp_fwd(x1, modv, n2g, wmi, wmo, fg, tgt)

    dhp, dx1, st2 = _mlp_bwd(dx2, x1, hp, modv, n2g, wmi, wmo)
    m2 = _tn_matmul(hp, dx2, 2048, D, "dw_mlp_out", relu_sq=True)
    g_wmo, dg2 = _scale_gate(m2, wmo.reshape(FF, D), modv, 5, "gate2_grad")
    g_wmi = _tn_matmul(h2, dhp, D, 512, "dw_mlp_in", dev_major=True)
    dua, dub, dga, dgb, dya, dyb, dl_a, dl_b, dya_t = _merge_bwd(dx1, modv, ga, gb, ua, ub, ya, yb, wb, wout)
    m1 = _tn_matmul(merged, dx1, D, D, "dw_out")
    g_wout, dg1 = _scale_gate(m1, wout, modv, 2, "gate1_grad")
    g_wb0 = _tn_matmul(ya, dua, BW, D, "dw_branch_a")
    g_wb1 = _tn_matmul(yb, dub, BW, D, "dw_branch_b")
    g_wb = jnp.stack([g_wb0, g_wb1]).reshape(2, BW, N_DEV, 128).transpose(2, 0, 1, 3).reshape(N_DEV, 2 * BW, 128)
    g_wout = g_wout.reshape(N_DEV, 128, D)
    g_wmo = g_wmo.reshape(N_DEV, 512, D)
    dqa_t, dka_t3, dva_t3, r_wb, r_wout, r_wmi, r_wmo = _attn_a_bwd(qa, qa_t, ka, ka_t3, va, dya, dya_t, lse_a, dl_a,
                                                                    (g_wb, g_wout, g_wmi, g_wmo))
    dqb, dkb_t, dvb_t, dsink = _attn_b_bwd(qb, kbp, vbp, sink2, dyb, lse_b, dl_b)
    dkb = dkb_t.transpose(0, 2, 1).reshape(S + 2 * WIN, 128)[WIN:WIN + S]
    dvb = dvb_t.transpose(0, 2, 1).reshape(S + 2 * WIN, 128)[WIN:WIN + S]
    dproj, stqk = _qk_bwd(dqa_t, dka_t3, dva_t3, dqb, dkb, dvb, qar, kar, qg2, kg2, tab_a, tab_b, dga, dgb)
    g_win = _tn_matmul(h, dproj, D, 896, "dw_in")
    g_win = g_win.reshape(D, N_DEV, NI).transpose(1, 0, 2)
    grad_x, st1, r_win = _in_bwd(dproj, win, xs, dx1, modv, n1g, g_win.astype(BF16))

    def adam(name, own, recv, w, m, v):
        shape = w.shape
        w2, m2_, v2 = (a.reshape(own.shape[1:]) for a in (w, m, v))
        outs = _adamw_sum([(own, me1)] + [(recv, k) for k in range(N_DEV - 1)], w2, m2_, v2, name)
        return [a.reshape(shape) for a in outs]

    o_win = adam("adamw_w_in", g_win, r_win, w_in, m_w_in, v_w_in)
    o_wb = adam("adamw_w_branch", g_wb, r_wb, w_branch, m_w_branch, v_w_branch)
    o_wout = adam("adamw_w_out", g_wout, r_wout, w_out, m_w_out, v_w_out)
    o_wmi = adam("adamw_w_mlp_in", g_wmi, r_wmi, w_mlp_in, m_w_mlp_in, v_w_mlp_in)
    o_wmo = adam("adamw_w_mlp_out", g_wmo, r_wmo, w_mlp_out, m_w_mlp_out, v_w_mlp_out)

    small = _pack_small(st1, st2, stf, dg1, dg2, stqk, dsink, modv, n1g, n2g)
    small_all = _small_gather(small)
    sw = _pack_small_params(b_ada, norm1_g, norm2_g, final_g, q_norm_a, k_norm_a, sink_b)
    sm = _pack_small_params(m_b_ada, m_norm1_g, m_norm2_g, m_final_g, m_q_norm_a, m_k_norm_a, m_sink_b)
    sv = _pack_small_params(v_b_ada, v_norm1_g, v_norm2_g, v_final_g, v_q_norm_a, v_k_norm_a, v_sink_b)
    sm_out = _adamw_sum([(small_all, k) for k in range(N_DEV)], sw, sm, sv, "adamw_small")
    loss = sm_out[0][11, 0]
    sm_out = [_unpack_small(a) for a in sm_out]

    dmod_all = small_all[:, 0:6, :].reshape(N_DEV, 6 * D)
    dmod_cols = lax.dynamic_slice_in_dim(dmod_all, me * NW, NW, axis=1)
    g_wada = _wada_grad(silu_all, dmod_cols)
    ada = _adamw_sum([(g_wada, None)], w_ada.reshape(D, NW), m_w_ada.reshape(D, NW), v_w_ada.reshape(D, NW), "adamw_ada")
    ada = [a.reshape(1, D, NW) for a in ada]

    def leaves(k):
        b_, n1_, qn_, kn_, sk_, n2_, fg_ = sm_out[k]
        return [ada[k], b_, n1_, o_win[k], qn_, kn_, sk_, o_wb[k], o_wout[k], n2_, o_wmi[k], o_wmo[k], fg_]

    return (loss, grad_x.reshape(1, S, D), *leaves(0), *leaves(1), *leaves(2), *leaves(3))
```

```python
import jax
import jax.numpy as jnp
from jax import lax
from jax.experimental import pallas as pl
from jax.experimental.pallas import tpu as pltpu

F32, BF16 = jnp.float32, jnp.bfloat16
MESH = pl.DeviceIdType.MESH

D = 1024
HD = 64
NH = 8
GRP = 4
BW = 512
FF = 4096
INW = 3584
GRID_W = 64
WIN = 128
THETA = 10000.0
EPS = 1e-6
NEG = -1e30
N_DEV = 8
LOG2E = 1.4426950408889634
LN2 = 0.6931471805599453
QA_SCALE = 0.125 * LOG2E
SMALL_ROWS = 16
MLP_SHARDS = 4
V7X_VMEM_LIMIT = 56 * 1024 * 1024

ADAM_LR, ADAM_B1, ADAM_B2, ADAM_EPS, ADAM_WD, ADAM_STEP = 0.001, 0.9, 0.999, 1e-08, 0.01, 10

NT = (((1,), (1,)), ((), ()))
TN = (((0,), (0,)), ((), ()))


def _params(n_axes, vmem=V7X_VMEM_LIMIT):
    return pltpu.CompilerParams(dimension_semantics=("arbitrary",) * n_axes, vmem_limit_bytes=vmem)


def _const(shape):
    return pl.BlockSpec(shape, lambda *_: (0,) * len(shape))


def _rows(tm, width):
    return pl.BlockSpec((tm, width), lambda i, *_: (i, 0))


def _seg_matrix(n, seg):
    r = lax.broadcasted_iota(jnp.int32, (n, n), 0) // seg
    c = lax.broadcasted_iota(jnp.int32, (n, n), 1) // seg
    return (r == c).astype(F32)


def _seg_sum(z, seg_mat):
    return jnp.dot(z, seg_mat, preferred_element_type=F32, precision=lax.Precision.HIGHEST)


def _rope(z, t_ref, sh):
    return z * t_ref[0] + pltpu.roll(z, sh, 1) * t_ref[1] + pltpu.roll(z, 128 - sh, 1) * t_ref[2]


def _rope_t(dz, t_ref, sh):
    return dz * t_ref[0] + pltpu.roll(dz * t_ref[1], 128 - sh, 1) + pltpu.roll(dz * t_ref[2], sh, 1)


def _rope_tables(S):
    t = jnp.arange(S, dtype=jnp.int32)
    lane = jnp.arange(HD)

    def build(cos, sin, first):
        t0 = cos
        t1 = jnp.where(first[None, :], 0.0, sin)
        t2 = jnp.where(first[None, :], -sin, 0.0)
        return jnp.stack([jnp.tile(a, (1, 2)) for a in (t0, t1, t2)]).astype(F32)

    inv_a = THETA ** (-jnp.arange(0, HD // 2, 2, dtype=F32) / (HD // 2))
    ar = (t // GRID_W).astype(F32)[:, None] * inv_a[None, :]
    ac = (t % GRID_W).astype(F32)[:, None] * inv_a[None, :]
    cos_a = jnp.concatenate([jnp.cos(ar), jnp.cos(ar), jnp.cos(ac), jnp.cos(ac)], axis=1)
    sin_a = jnp.concatenate([jnp.sin(ar), jnp.sin(ar), jnp.sin(ac), jnp.sin(ac)], axis=1)
    tab_a = build(cos_a, sin_a, (lane % 32) < 16)
    inv_b = THETA ** (-jnp.arange(0, HD, 2, dtype=F32) / HD)
    ab = t.astype(F32)[:, None] * inv_b[None, :]
    cos_b = jnp.concatenate([jnp.cos(ab), jnp.cos(ab)], axis=1)
    sin_b = jnp.concatenate([jnp.sin(ab), jnp.sin(ab)], axis=1)
    tab_b = build(cos_b, sin_b, lane < 32)
    return tab_a, tab_b


def _in_proj(x, modv, n1g, win, qg2, kg2, tab_a, tab_b):
    S = x.shape[0]
    tm = min(512, S)
    tk = min(512, S)
    per = tk // tm
    ts = min(256, tm)

    def body(x_ref, mod_ref, g_ref, w_ref, qg_ref, kg_ref, ta_ref, tb_ref,
             h_ref, qar_ref, kar_ref, qa_ref, ka_ref, va_ref, qb_ref, kb_ref, vb_ref, ga_ref, gb_ref, qat_ref, kat_ref, vat_ref):
        seg = _seg_matrix(128, HD)

        def head_norm(z, g):
            ms = _seg_sum(z * z, seg) * (1.0 / HD)
            return (z * lax.rsqrt(ms + EPS)) * g

        subs = [slice(ts * u, ts * u + ts) for u in range(tm // ts)]
        hbs = []
        for rows in subs:
            xt = x_ref[rows, :]
            r = lax.rsqrt(jnp.mean(xt * xt, axis=-1, keepdims=True) + EPS)
            h = ((xt * r) * g_ref[...]) * (1.0 + mod_ref[1:2, :]) + mod_ref[0:1, :]
            hbs.append(h.astype(BF16))
            h_ref[rows, :] = hbs[-1]
        projs = [jnp.dot(hb, w_ref[...], preferred_element_type=F32) for hb in hbs]
        for rows, proj in zip(subs, projs):
            ta, tb = ta_ref[:, rows, :], tb_ref[:, rows, :]
            for p in range(4):
                z = proj[:, 128 * p:128 * p + 128]
                qar_ref[rows, 128 * p:128 * p + 128] = z.astype(BF16)
                qv = _rope(head_norm(z, qg_ref[...]), ta, 16) * QA_SCALE
                qa_ref[rows, 128 * p:128 * p + 128] = qv.astype(BF16)
                qat_ref[128 * p:128 * p + 128, rows] = qv.T.astype(BF16)
                zb = proj[:, 768 + 128 * p:768 + 128 * p + 128]
                qb_ref[rows, 128 * p:128 * p + 128] = (_rope(zb, tb, 32) * QA_SCALE).astype(BF16)
            z = proj[:, 512:640]
            kar_ref[rows, :] = z.astype(BF16)
            kv_ = _rope(head_norm(z, kg_ref[...]), ta, 16)
            ka_ref[rows, :] = kv_.astype(BF16)
            kat_ref[:, rows] = kv_.T.astype(BF16)
            va_ref[rows, :] = proj[:, 640:768].astype(BF16)
            vat_ref[:, rows] = proj[:, 640:768].T.astype(BF16)
            kb_ref[rows, :] = _rope(proj[:, 1280:1408], tb, 32).astype(BF16)
            vb_ref[rows, :] = proj[:, 1408:1536].astype(BF16)
            ga_ref[rows, :] = proj[:, 1536:2560].astype(BF16)
            gb_ref[rows, :] = proj[:, 2560:3584].astype(BF16)

    tab = pl.BlockSpec((3, tm, 128), lambda i: (0, i, 0))
    shapes = [(D, BF16), (BW, BF16), (128, BF16), (BW, BF16), (128, BF16), (128, BF16),
              (BW, BF16), (128, BF16), (128, BF16), (D, BF16), (D, BF16)]
    return pl.pallas_call(
        body, grid=(S // tm,),
        in_specs=[_rows(tm, D), _const((6, D)), _const((1, D)), _const((D, INW)), _const((1, 128)), _const((1, 128)), tab, tab],
        out_specs=[_rows(tm, w) for w, _ in shapes] + [pl.BlockSpec((BW, tm), lambda i: (0, i))]
        + [pl.BlockSpec((None, 128, tm), lambda i: (i // per, 0, i % per))] * 2,
        out_shape=[jax.ShapeDtypeStruct((S, w), dt) for w, dt in shapes] + [jax.ShapeDtypeStruct((BW, S), BF16)]
        + [jax.ShapeDtypeStruct((S // tk, 128, tk), BF16)] * 2,
        compiler_params=_params(1), name="in_proj",
    )(x, modv, n1g, win, qg2, kg2, tab_a, tab_b)


def _exchange_gather(block_refs, out_refs, send_sems, recv_sems, local_sems):
    x, y, c = _me()
    me = 4 * x + 2 * y + c

    def copies():
        own, out, arrive = [], [], []
        for a, (blk, dst) in enumerate(zip(block_refs, out_refs)):
            own.append(pltpu.make_async_copy(blk, dst.at[me], local_sems.at[a]))
            for k in range(1, N_DEV):
                sems = dict(send_sem=send_sems.at[a, k - 1], recv_sem=recv_sems.at[a, k - 1], device_id=_peer(k), device_id_type=MESH)
                out.append(pltpu.make_async_remote_copy(blk, dst.at[me], **sems))
                arrive.append(pltpu.make_async_remote_copy(blk, dst.at[me ^ k], **sems))
        return own, out, arrive

    def start():
        own, out, _ = copies()
        for cp in own + out:
            cp.start()

    def finish():
        own, out, arrive = copies()
        for cp in arrive:
            cp.wait_recv()
        for cp in out:
            cp.wait_send()
        for cp in own:
            cp.wait()

    return start, finish


def _exchange_scatter(chunk_refs, recv_refs, send_sems, recv_sems):
    x, y, c = _me()
    me = 4 * x + 2 * y + c

    def copies():
        return [pltpu.make_async_remote_copy(src.at[me ^ k], dst.at[k - 1], send_sems.at[a, k - 1], recv_sems.at[a, k - 1],
                                             device_id=_peer(k), device_id_type=MESH)
                for a, (src, dst) in enumerate(zip(chunk_refs, recv_refs)) for k in range(1, N_DEV)]

    def start():
        for cp in copies():
            cp.start()

    def finish():
        cps = copies()
        for cp in cps:
            cp.wait_recv()
        for cp in cps:
            cp.wait_send()

    return start, finish


def _exchange_sems(n):
    return [pltpu.SemaphoreType.DMA((n, N_DEV - 1)), pltpu.SemaphoreType.DMA((n, N_DEV - 1))]


def _attn_a_fwd(qt, k, vt3, shards):
    S = qt.shape[1]
    tq = min(512, S)
    nq = S // tq
    nk, _, tk = vt3.shape
    ONES = 16
    AHEAD = 2
    ns = len(shards)

    def body(q_ref, k_ref, vt_ref, *rest):
        w_hbm, (o_ref, lse_ref), wall_hbm = rest[:ns], rest[ns:ns + 2], rest[ns + 2:2 * ns + 2]
        st_sc, send_sems, recv_sems, local_sems = rest[2 * ns + 2:]
        start, finish = _exchange_gather(w_hbm, wall_hbm, send_sems, recv_sems, local_sems)
        pl.when(pl.program_id(0) == 0)(start)
        row8 = lax.broadcasted_iota(jnp.int32, (NH, tq), 0)
        lse_all = jnp.zeros((NH, tq), F32)
        ones = jnp.ones((ONES, tk), BF16)
        for kv in range(2):
            qts = [q_ref[HD * (GRP * kv + t):HD * (GRP * kv + t) + HD, :] for t in range(GRP)]

            def keys(j, kv=kv):
                return k_ref[pl.ds(pl.multiple_of(j * tk, tk), tk), :][:, HD * kv:HD * kv + HD]

            def scores(kj, t, qts=qts):
                return jnp.dot(kj, qts[t], preferred_element_type=F32)

            def step(j, carry, kv=kv):
                kj = keys(j)
                kn = keys(jnp.minimum(j + 1, nk - 1))
                v1 = jnp.concatenate([vt_ref[j, HD * kv:HD * kv + HD, :], ones], axis=0)
                sts = [st_sc[t] for t in range(AHEAD)]
                new = []
                for t in range(GRP):
                    m, acc = carry[2 * t], carry[2 * t + 1]
                    if t + AHEAD < GRP:
                        sts.append(scores(kj, t + AHEAD))
                    st = sts[t]
                    mn = jnp.maximum(m, jnp.max(st, axis=0, keepdims=True))
                    pt = jnp.exp2(st - mn)
                    if t + AHEAD >= GRP:
                        st_sc[t + AHEAD - GRP] = scores(kn, t + AHEAD - GRP)
                    acc = jnp.exp2(m - mn) * acc + jnp.dot(v1, pt.astype(BF16), preferred_element_type=F32)
                    new += [mn, acc]
                return tuple(new)

            k0 = keys(0)
            for t in range(AHEAD):
                st_sc[t] = scores(k0, t)
            init = (jnp.full((1, tq), NEG, F32), jnp.zeros((HD + ONES, tq), F32)) * GRP
            res = lax.fori_loop(0, nk, step, init)
            outs = []
            for t in range(GRP):
                m, acc = res[2 * t], res[2 * t + 1]
                l = acc[HD:HD + 1, :]
                outs.append((acc[:HD, :] / l).T)
                lse_all = jnp.where(row8 == GRP * kv + t, m + jnp.log2(l), lse_all)
            o_ref[:, 256 * kv:256 * kv + 256] = jnp.concatenate(outs, axis=1).astype(BF16)
        lse_ref[...] = lse_all
        pl.when(pl.program_id(0) == nq - 1)(finish)

    any_spec = pl.BlockSpec(memory_space=pl.ANY)
    return pl.pallas_call(
        body, grid=(nq,),
        in_specs=[pl.BlockSpec((BW, tq), lambda i: (0, i)), _const((S, 128)), _const((nk, 128, tk))] + [any_spec] * ns,
        out_specs=[_rows(tq, BW), pl.BlockSpec((NH, tq), lambda i: (0, i))] + [any_spec] * ns,
        out_shape=[jax.ShapeDtypeStruct((S, BW), BF16), jax.ShapeDtypeStruct((NH, S), F32)]
        + [jax.ShapeDtypeStruct((N_DEV,) + s.shape, s.dtype) for s in shards],
        scratch_shapes=[pltpu.VMEM((AHEAD, tk, tq), F32)] + _exchange_sems(ns) + [pltpu.SemaphoreType.DMA((ns,))],
        compiler_params=_params(1), name="attn_a_fwd",
    )(qt, k, vt3, *shards)


def _window_mask(i, tq, S):
    W = tq + 2 * WIN
    r = lax.broadcasted_iota(jnp.int32, (tq, W), 0)
    c = lax.broadcasted_iota(jnp.int32, (tq, W), 1)
    kpos = i * tq - WIN + c
    return (jnp.abs(c - WIN - r) <= WIN) & (kpos >= 0) & (kpos < S)


TQ_B = 256


def _attn_b_fwd(q, kp, vpt3, sink2):
    S = q.shape[0]
    tq = min(TQ_B, S)
    W = tq + 2 * WIN
    nc = vpt3.shape[0]
    ONES = 16

    def body(q_ref, k_ref, vt_ref, sink_ref, o_ref, lse_ref):
        i = pl.program_id(0)
        off = pl.multiple_of(i * tq, tq)
        r = lax.broadcasted_iota(jnp.int32, (W, tq), 1)
        c = lax.broadcasted_iota(jnp.int32, (W, tq), 0)
        kpos = i * tq - WIN + c
        valid = (jnp.abs(c - WIN - r) <= WIN) & (kpos >= 0) & (kpos < S)
        kw = k_ref[pl.ds(off, W), :]
        vt = jnp.concatenate([vt_ref[i + half] for half in range(W // tq)], axis=1)
        ones = jnp.ones((ONES, W), BF16)
        row8 = lax.broadcasted_iota(jnp.int32, (NH, tq), 0)
        lse_all = jnp.zeros((NH, tq), F32)
        qs = []
        for p in range(4):
            qp = q_ref[:, 128 * p:128 * p + 128]
            qs += [qp[:, :HD], qp[:, HD:]]
        khs = [kw[:, HD * kv:HD * kv + HD] for kv in range(2)]
        v1s = [jnp.concatenate([vt[HD * kv:HD * kv + HD, :], ones], axis=0) for kv in range(2)]

        def scores(h):
            return lax.dot_general(khs[h // GRP], qs[h], NT, preferred_element_type=F32)

        ss = [scores(0), scores(1)]
        outs = []
        for h in range(NH):
            if h + 2 < NH:
                ss.append(scores(h + 2))
            st = jnp.where(valid, ss[h], NEG)
            sk = sink_ref[:, h:h + 1]
            m = jnp.maximum(jnp.max(st, axis=0, keepdims=True), sk)
            acc = jnp.dot(v1s[h // GRP], jnp.exp2(st - m).astype(BF16), preferred_element_type=F32)
            l = acc[HD:HD + 1, :] + jnp.exp2(sk - m)
            outs.append((acc[:HD, :] / l).T)
            lse_all = jnp.where(row8 == h, m + jnp.log2(l), lse_all)
        for p in range(4):
            o_ref[:, 128 * p:128 * p + 128] = jnp.concatenate(outs[2 * p:2 * p + 2], axis=1).astype(BF16)
        lse_ref[...] = lse_all

    return pl.pallas_call(
        body, grid=(S // tq,),
        in_specs=[_rows(tq, BW), _const((S + 2 * WIN, 128)), _const((nc, 128, tq)), _const((1, NH))],
        out_specs=[_rows(tq, BW), pl.BlockSpec((NH, tq), lambda i: (0, i))],
        out_shape=[jax.ShapeDtypeStruct((S, BW), BF16), jax.ShapeDtypeStruct((NH, S), F32)],
        compiler_params=_params(1), name="attn_b_fwd",
    )(q, kp, vpt3, sink2)


def _merge_out(ya, yb, ga, gb, x, modv, wb, wout):
    S = x.shape[0]
    tm = min(256, S)

    def body(ya_ref, yb_ref, ga_ref, gb_ref, x_ref, mod_ref, wb_ref, wo_ref, x1_ref, mg_ref, ua_ref, ub_ref):
        ua = jnp.dot(ya_ref[...], wb_ref[0], preferred_element_type=F32)
        ub = jnp.dot(yb_ref[...], wb_ref[1], preferred_element_type=F32)
        merged = jax.nn.sigmoid(ga_ref[...].astype(F32)) * ua + jax.nn.sigmoid(gb_ref[...].astype(F32)) * ub
        mb = merged.astype(BF16)
        ua_ref[...] = ua.astype(BF16)
        ub_ref[...] = ub.astype(BF16)
        mg_ref[...] = mb
        x1_ref[...] = x_ref[...] + mod_ref[2:3, :] * jnp.dot(mb, wo_ref[...], preferred_element_type=F32)

    return pl.pallas_call(
        body, grid=(S // tm,),
        in_specs=[_rows(tm, BW), _rows(tm, BW), _rows(tm, D), _rows(tm, D), _rows(tm, D), _const((6, D)),
                  _const((2, BW, D)), _const((D, D))],
        out_specs=[_rows(tm, D)] * 4,
        out_shape=[jax.ShapeDtypeStruct((S, D), F32)] + [jax.ShapeDtypeStruct((S, D), BF16)] * 3,
        compiler_params=_params(1), name="merge_out",
    )(ya, yb, ga, gb, x, modv, wb, wout)


def _mlp_fwd(x1, modv, n2g, wmi, wmo, fg, target):
    S = x1.shape[0]
    tm = min(512, S)
    tf = wmi.shape[2]
    nj = wmi.shape[0] // MLP_SHARDS

    def body(x1_ref, mod_ref, g_ref, wi_ref, wo_ref, fg_ref, t_ref, h2_ref, hp_ref, dx2_ref, st_ref, acc_ref):
        i, j = pl.program_id(0), pl.program_id(1)

        @pl.when(j == 0)
        def _():
            xt = x1_ref[...]
            r = lax.rsqrt(jnp.mean(xt * xt, axis=-1, keepdims=True) + EPS)
            h2 = ((xt * r) * g_ref[...]) * (1.0 + mod_ref[4:5, :]) + mod_ref[3:4, :]
            h2_ref[...] = h2.astype(BF16)
            acc_ref[...] = jnp.zeros_like(acc_ref)

        @pl.when((i == 0) & (j == 0))
        def _():
            st_ref[...] = jnp.zeros_like(st_ref)

        out = None
        for u in range(MLP_SHARDS):
            hp = jnp.dot(h2_ref[...], wi_ref[u], preferred_element_type=F32)
            hp_ref[:, tf * u:tf * u + tf] = hp.astype(BF16)
            hid = jnp.square(jnp.maximum(hp, 0.0))
            part = jnp.dot(hid.astype(BF16), wo_ref[u], preferred_element_type=F32)
            out = part if out is None else out + part
        acc_ref[...] += out

        @pl.when(j == nj - 1)
        def _():
            x2 = x1_ref[...] + mod_ref[5:6, :] * acc_ref[...]
            r3 = lax.rsqrt(jnp.mean(x2 * x2, axis=-1, keepdims=True) + EPS)
            xn = x2 * r3
            err = xn * fg_ref[...] - t_ref[...]
            dy = err * (1.0 / D)
            gy = dy * fg_ref[...]
            dx2_ref[...] = r3 * (gy - xn * jnp.mean(gy * xn, axis=-1, keepdims=True))
            st_ref[0:1, :] += jnp.sum(dy * xn, axis=0, keepdims=True)
            st_ref[1:2, :] += jnp.sum(err * err, axis=0, keepdims=True) * (0.5 / D)

    return pl.pallas_call(
        body, grid=(S // tm, nj),
        in_specs=[pl.BlockSpec((tm, D), lambda i, j: (i, 0)), _const((6, D)), _const((1, D)),
                  pl.BlockSpec((MLP_SHARDS, D, tf), lambda i, j: (j, 0, 0)), pl.BlockSpec((MLP_SHARDS, tf, D), lambda i, j: (j, 0, 0)),
                  _const((1, D)), pl.BlockSpec((tm, D), lambda i, j: (i, 0))],
        out_specs=[pl.BlockSpec((tm, D), lambda i, j: (i, 0)), pl.BlockSpec((tm, MLP_SHARDS * tf), lambda i, j: (i, j)),
                   pl.BlockSpec((tm, D), lambda i, j: (i, 0)), _const((8, D))],
        out_shape=[jax.ShapeDtypeStruct((S, D), BF16), jax.ShapeDtypeStruct((S, wmi.shape[0] * tf), BF16),
                   jax.ShapeDtypeStruct((S, D), F32), jax.ShapeDtypeStruct((8, D), F32)],
        scratch_shapes=[pltpu.VMEM((tm, D), F32)],
        compiler_params=_params(2), name="mlp_fwd",
    )(x1, modv, n2g, wmi, wmo, fg, target)


def _mlp_bwd(dx2, x1, hp, modv, n2g, wmi, wmo):
    S = x1.shape[0]
    tm = min(512, S)
    tf = wmi.shape[2]
    nj = wmi.shape[0] // MLP_SHARDS

    def body(dx2_ref, x1_ref, hp_ref, mod_ref, g_ref, wi_ref, wo_ref, dhp_ref, dx1_ref, st_ref, dmo_ref, acc_ref):
        i, j = pl.program_id(0), pl.program_id(1)

        @pl.when(j == 0)
        def _():
            dmo_ref[...] = (mod_ref[5:6, :] * dx2_ref[...]).astype(BF16)
            acc_ref[...] = jnp.zeros_like(acc_ref)

        @pl.when((i == 0) & (j == 0))
        def _():
            st_ref[...] = jnp.zeros_like(st_ref)

        out = None
        for u in range(MLP_SHARDS):
            sl = slice(tf * u, tf * u + tf)
            dhid = lax.dot_general(dmo_ref[...], wo_ref[u], NT, preferred_element_type=F32)
            dhp = (dhid * (2.0 * jnp.maximum(hp_ref[:, sl].astype(F32), 0.0))).astype(BF16)
            dhp_ref[:, sl] = dhp
            part = lax.dot_general(dhp, wi_ref[u], NT, preferred_element_type=F32)
            out = part if out is None else out + part
        acc_ref[...] += out

        @pl.when(j == nj - 1)
        def _():
            dh2 = acc_ref[...]
            xt = x1_ref[...]
            r = lax.rsqrt(jnp.mean(xt * xt, axis=-1, keepdims=True) + EPS)
            xn = xt * r
            st_ref[0:1, :] += jnp.sum(dh2, axis=0, keepdims=True)
            st_ref[1:2, :] += jnp.sum(dh2 * xn, axis=0, keepdims=True)
            dxn = dh2 * (g_ref[...] * (1.0 + mod_ref[4:5, :]))
            dx1_ref[...] = dx2_ref[...] + r * (dxn - xn * jnp.mean(dxn * xn, axis=-1, keepdims=True))

    return pl.pallas_call(
        body, grid=(S // tm, nj),
        in_specs=[pl.BlockSpec((tm, D), lambda i, j: (i, 0)), pl.BlockSpec((tm, D), lambda i, j: (i, 0)),
                  pl.BlockSpec((tm, MLP_SHARDS * tf), lambda i, j: (i, j)), _const((6, D)), _const((1, D)),
                  pl.BlockSpec((MLP_SHARDS, D, tf), lambda i, j: (j, 0, 0)), pl.BlockSpec((MLP_SHARDS, tf, D), lambda i, j: (j, 0, 0))],
        out_specs=[pl.BlockSpec((tm, MLP_SHARDS * tf), lambda i, j: (i, j)), pl.BlockSpec((tm, D), lambda i, j: (i, 0)), _const((8, D))],
        out_shape=[jax.ShapeDtypeStruct((S, wmi.shape[0] * tf), BF16), jax.ShapeDtypeStruct((S, D), F32),
                   jax.ShapeDtypeStruct((8, D), F32)],
        scratch_shapes=[pltpu.VMEM((tm, D), BF16), pltpu.VMEM((tm, D), F32)],
        compiler_params=_params(2), name="mlp_bwd",
    )(dx2, x1, hp, modv, n2g, wmi, wmo)


def _tn_matmul(a, b, tk, tn, name, relu_sq=False, dev_major=False):
    S, K = a.shape
    N = b.shape[1]
    ts = min(1024, S)
    ns = S // ts

    def body(a_ref, b_ref, o_ref):
        @pl.when(pl.program_id(2) == 0)
        def _():
            o_ref[...] = jnp.zeros_like(o_ref)

        at = a_ref[...]
        if relu_sq:
            at = jnp.square(jnp.maximum(at.astype(F32), 0.0)).astype(BF16)
        o_ref[...] += lax.dot_general(at, b_ref[...].astype(BF16), TN, preferred_element_type=F32)

    if dev_major:
        out_spec = pl.BlockSpec((None, tk, tn), lambda k, n, s: (n, k, 0))
        out_shape = jax.ShapeDtypeStruct((N // tn, K, tn), F32)
    else:
        out_spec = pl.BlockSpec((tk, tn), lambda k, n, s: (k, n))
        out_shape = jax.ShapeDtypeStruct((K, N), F32)
    return pl.pallas_call(
        body, grid=(K // tk, N // tn, ns),
        in_specs=[pl.BlockSpec((ts, tk), lambda k, n, s: (s, k)), pl.BlockSpec((ts, tn), lambda k, n, s: (s, n))],
        out_specs=out_spec, out_shape=out_shape,
        compiler_params=_params(3), name=name,
    )(a, b)


def _scale_gate(m, w, g, row, name):
    K = m.shape[0]
    tk = min(512, K)

    def body(m_ref, w_ref, mod_ref, dw_ref, dg_ref):
        @pl.when(pl.program_id(0) == 0)
        def _():
            dg_ref[...] = jnp.zeros_like(dg_ref)

        mt = m_ref[...]
        dw_ref[...] = mt * mod_ref[row:row + 1, :]
        dg_ref[0:1, :] += jnp.sum(mt * w_ref[...].astype(F32), axis=0, keepdims=True)

    return pl.pallas_call(
        body, grid=(K // tk,),
        in_specs=[_rows(tk, D), _rows(tk, D), _const((6, D))],
        out_specs=[_rows(tk, D), _const((8, D))],
        out_shape=[jax.ShapeDtypeStruct((K, D), F32), jax.ShapeDtypeStruct((8, D), F32)],
        compiler_params=_params(1), name=name,
    )(m, w, g)


def _merge_bwd(dx1, modv, ga, gb, ua, ub, ya, yb, wb, wout):
    S = dx1.shape[0]
    tm = min(512, S)

    def body(dx1_ref, mod_ref, ga_ref, gb_ref, ua_ref, ub_ref, ya_ref, yb_ref, wb_ref, wo_ref,
             dua_ref, dub_ref, dga_ref, dgb_ref, dya_ref, dyb_ref, dla_ref, dlb_ref, dyat_ref):
        dao = (mod_ref[2:3, :] * dx1_ref[...]).astype(BF16)
        dm = lax.dot_general(dao, wo_ref[...], NT, preferred_element_type=F32)
        r = lax.broadcasted_iota(jnp.int32, (BW, NH), 0) // HD
        c = lax.broadcasted_iota(jnp.int32, (BW, NH), 1)
        head_of = (r == c).astype(F32)
        for br, (g_ref, u_ref, y_ref, du_ref, dg_ref, dy_ref, dl_ref) in enumerate((
                (ga_ref, ua_ref, ya_ref, dua_ref, dga_ref, dya_ref, dla_ref),
                (gb_ref, ub_ref, yb_ref, dub_ref, dgb_ref, dyb_ref, dlb_ref))):
            sg = jax.nn.sigmoid(g_ref[...].astype(F32))
            du = (dm * sg).astype(BF16)
            du_ref[...] = du
            dg_ref[...] = (dm * u_ref[...].astype(F32) * sg * (1.0 - sg)).astype(BF16)
            dy = lax.dot_general(du, wb_ref[br], NT, preferred_element_type=F32)
            dyb16 = dy.astype(BF16)
            dy_ref[...] = dyb16
            if br == 0:
                dyat_ref[...] = dy.T.astype(BF16)
            dl_ref[...] = jnp.dot(dyb16.astype(F32) * y_ref[...].astype(F32), head_of,
                                  preferred_element_type=F32, precision=lax.Precision.HIGHEST)

    return pl.pallas_call(
        body, grid=(S // tm,),
        in_specs=[_rows(tm, D), _const((6, D)), _rows(tm, D), _rows(tm, D), _rows(tm, D), _rows(tm, D),
                  _rows(tm, BW), _rows(tm, BW), _const((2, BW, D)), _const((D, D))],
        out_specs=[_rows(tm, D)] * 4 + [_rows(tm, BW)] * 2 + [_rows(tm, NH)] * 2 + [pl.BlockSpec((BW, tm), lambda i: (0, i))],
        out_shape=[jax.ShapeDtypeStruct((S, D), BF16)] * 4 + [jax.ShapeDtypeStruct((S, BW), BF16)] * 2
        + [jax.ShapeDtypeStruct((S, NH), F32)] * 2 + [jax.ShapeDtypeStruct((BW, S), BF16)],
        compiler_params=_params(1), name="merge_bwd",
    )(dx1, modv, ga, gb, ua, ub, ya, yb, wb, wout)


def _attn_a_bwd(q, qt, kt3, vt3, do, dot_, lse, delta, chunks):
    S = q.shape[0]
    tq = min(512, S)
    tk = min(512, S)
    nq, nk = S // tq, S // tk

    nc = len(chunks)

    def body(q_ref, qt_ref, do_ref, dot_ref, lse_ref, dl_ref, kt_ref, vt_ref, *rest):
        g_hbm, (dq_ref, dk_hbm, dv_hbm), recv_hbm = rest[:nc], rest[nc:nc + 3], rest[nc + 3:2 * nc + 3]
        dk_sc, dv_sc, sem, send_sems, recv_sems = rest[2 * nc + 3:]
        i = pl.program_id(0)
        start, finish = _exchange_scatter(g_hbm, recv_hbm, send_sems, recv_sems)

        @pl.when(i == 0)
        def _():
            start()
            dk_sc[...] = jnp.zeros_like(dk_sc)
            dv_sc[...] = jnp.zeros_like(dv_sc)

        for kv in range(2):
            qg = q_ref[:, 256 * kv:256 * kv + 256]
            dog = do_ref[:, 256 * kv:256 * kv + 256]
            heads = []
            for t in range(GRP):
                h = GRP * kv + t
                heads.append((qg[:, HD * t:HD * t + HD], dog[:, HD * t:HD * t + HD],
                              qt_ref[HD * h:HD * h + HD, :], dot_ref[HD * h:HD * h + HD, :],
                              lse_ref[:, h:h + 1], dl_ref[:, h:h + 1]))

            def step(j, carry, kv=kv, heads=heads):
                kjt = kt_ref[j, HD * kv:HD * kv + HD, :]
                vjt = vt_ref[j, HD * kv:HD * kv + HD, :]
                dkt = jnp.zeros((HD, tk), F32)
                dvt = jnp.zeros((HD, tk), F32)
                new = []

                def logits(t):
                    return (jnp.dot(heads[t][0], kjt, preferred_element_type=F32),
                            jnp.dot(heads[t][1], vjt, preferred_element_type=F32))

                sd = [logits(0)]
                for t, (qh, doh, qth, doth, lse_h, dl_h) in enumerate(heads):
                    if t + 1 < GRP:
                        sd.append(logits(t + 1))
                    s, dp = sd[t]
                    pm = jnp.exp2(s - lse_h)
                    ds = (pm * (dp - dl_h)).astype(BF16)
                    dvt = dvt + jnp.dot(doth, pm.astype(BF16), preferred_element_type=F32)
                    dkt = dkt + jnp.dot(qth, ds, preferred_element_type=F32)
                    new.append(carry[t] + lax.dot_general(kjt, ds, NT, preferred_element_type=F32))
                dk_sc[j, HD * kv:HD * kv + HD, :] += dkt
                dv_sc[j, HD * kv:HD * kv + HD, :] += dvt
                return tuple(new)

            res = lax.fori_loop(0, nk, step, (jnp.zeros((HD, tq), F32),) * GRP)
            for t in range(GRP):
                dq_ref[HD * (GRP * kv + t):HD * (GRP * kv + t) + HD, :] = res[t]

        @pl.when(i == nq - 1)
        def _():
            c1 = pltpu.make_async_copy(dk_sc, dk_hbm, sem.at[0])
            c2 = pltpu.make_async_copy(dv_sc, dv_hbm, sem.at[1])
            c1.start()
            c2.start()
            c1.wait()
            c2.wait()
            finish()

    any_spec = pl.BlockSpec(memory_space=pl.ANY)
    cols = pl.BlockSpec((BW, tq), lambda i: (0, i))
    return pl.pallas_call(
        body, grid=(nq,),
        in_specs=[_rows(tq, BW), cols, _rows(tq, BW), cols, _rows(tq, NH), _rows(tq, NH), _const((nk, 128, tk)),
                  _const((nk, 128, tk))] + [any_spec] * nc,
        out_specs=[cols, any_spec, any_spec] + [any_spec] * nc,
        out_shape=[jax.ShapeDtypeStruct((BW, S), F32), jax.ShapeDtypeStruct((nk, 128, tk), F32),
                   jax.ShapeDtypeStruct((nk, 128, tk), F32)]
        + [jax.ShapeDtypeStruct((N_DEV - 1,) + c.shape[1:], c.dtype) for c in chunks],
        scratch_shapes=[pltpu.VMEM((nk, 128, tk), F32), pltpu.VMEM((nk, 128, tk), F32), pltpu.SemaphoreType.DMA((2,))]
        + _exchange_sems(nc),
        compiler_params=_params(1), name="attn_a_bwd",
    )(q, qt, do, dot_, lse, delta, kt3, vt3, *chunks)


def _attn_b_bwd(q, kp, vp, sink2, do, lse, delta):
    S = q.shape[0]
    tq = min(TQ_B, S)
    W = tq + 2 * WIN
    nq = S // tq
    nc = (S + 2 * WIN) // tq

    def body(q_ref, k_ref, v_ref, sink_ref, do_ref, lse_ref, dl_ref, dq_ref, dk_hbm, dv_hbm, ds_ref, dk_sc, dv_sc, sem):
        i = pl.program_id(0)

        @pl.when(i == 0)
        def _():
            dk_sc[...] = jnp.zeros_like(dk_sc)
            dv_sc[...] = jnp.zeros_like(dv_sc)
            ds_ref[...] = jnp.zeros_like(ds_ref)

        off = pl.multiple_of(i * tq, tq)
        valid = _window_mask(i, tq, S)
        kw = k_ref[pl.ds(off, W), :]
        vw = v_ref[pl.ds(off, W), :]
        lse_i = lse_ref[...]
        dl_i = dl_ref[...]
        qa = q_ref[...]
        doa = do_ref[...]
        qt = qa.astype(F32).T.astype(BF16)
        dot_ = doa.astype(F32).T.astype(BF16)
        khs = [kw[:, HD * kv:HD * kv + HD] for kv in range(2)]
        vhs = [vw[:, HD * kv:HD * kv + HD] for kv in range(2)]

        def logits(h):
            return (lax.dot_general(qa[:, HD * h:HD * h + HD], khs[h // GRP], NT, preferred_element_type=F32),
                    lax.dot_general(doa[:, HD * h:HD * h + HD], vhs[h // GRP], NT, preferred_element_type=F32))

        sd = [logits(0)]
        dqs = []
        dkt = [jnp.zeros((HD, W), F32), jnp.zeros((HD, W), F32)]
        dvt = [jnp.zeros((HD, W), F32), jnp.zeros((HD, W), F32)]
        for h in range(NH):
            kv = h // GRP
            if h + 1 < NH:
                sd.append(logits(h + 1))
            s, dp = sd[h]
            pm = jnp.exp2(jnp.where(valid, s, NEG) - lse_i[:, h:h + 1])
            ds = (pm * (dp - dl_i[:, h:h + 1])).astype(BF16)
            dvt[kv] = dvt[kv] + jnp.dot(dot_[HD * h:HD * h + HD, :], pm.astype(BF16), preferred_element_type=F32)
            dkt[kv] = dkt[kv] + jnp.dot(qt[HD * h:HD * h + HD, :], ds, preferred_element_type=F32)
            dqs.append(jnp.dot(ds, khs[kv], preferred_element_type=F32))
        for p in range(4):
            dq_ref[:, 128 * p:128 * p + 128] = jnp.concatenate(dqs[2 * p:2 * p + 2], axis=1)
        for half in range(W // tq):
            dk_sc[i + half] += jnp.concatenate([d[:, tq * half:tq * half + tq] for d in dkt], axis=0)
            dv_sc[i + half] += jnp.concatenate([d[:, tq * half:tq * half + tq] for d in dvt], axis=0)
        psd = jnp.exp2(sink_ref[...] - lse_i) * dl_i
        r = lax.broadcasted_iota(jnp.int32, (NH, 128), 0)
        c = lax.broadcasted_iota(jnp.int32, (NH, 128), 1)
        row = jnp.dot(jnp.sum(psd, axis=0, keepdims=True), (r == c).astype(F32),
                      preferred_element_type=F32, precision=lax.Precision.HIGHEST)
        ds_ref[...] -= jnp.broadcast_to(row, (8, 128))

        @pl.when(i == nq - 1)
        def _():
            c1 = pltpu.make_async_copy(dk_sc, dk_hbm, sem.at[0])
            c2 = pltpu.make_async_copy(dv_sc, dv_hbm, sem.at[1])
            c1.start()
            c2.start()
            c1.wait()
            c2.wait()

    any_spec = pl.BlockSpec(memory_space=pl.ANY)
    return pl.pallas_call(
        body, grid=(nq,),
        in_specs=[_rows(tq, BW), _const((S + 2 * WIN, 128)), _const((S + 2 * WIN, 128)), _const((1, NH)),
                  _rows(tq, BW), _rows(tq, NH), _rows(tq, NH)],
        out_specs=[_rows(tq, BW), any_spec, any_spec, _const((8, 128))],
        out_shape=[jax.ShapeDtypeStruct((S, BW), F32), jax.ShapeDtypeStruct((nc, 128, tq), F32),
                   jax.ShapeDtypeStruct((nc, 128, tq), F32), jax.ShapeDtypeStruct((8, 128), F32)],
        scratch_shapes=[pltpu.VMEM((nc, 128, tq), F32), pltpu.VMEM((nc, 128, tq), F32), pltpu.SemaphoreType.DMA((2,))],
        compiler_params=_params(1), name="attn_b_bwd",
    )(q, kp, vp, sink2, do, lse, delta)


def _qk_bwd(dqa_t, dka_t3, dva_t3, dqb, dkb, dvb, qar, kar, qg2, kg2, tab_a, tab_b, dga, dgb):
    S = dqb.shape[0]
    tm = min(256, S)
    per = dka_t3.shape[2] // tm

    def body(dqa_ref, dka_ref, dva_ref, dqb_ref, dkb_ref, dvb_ref, qar_ref, kar_ref, qg_ref, kg_ref, ta_ref, tb_ref,
             dga_ref, dgb_ref, dp_ref, st_ref):
        @pl.when(pl.program_id(0) == 0)
        def _():
            st_ref[...] = jnp.zeros_like(st_ref)

        seg = _seg_matrix(128, HD)

        def norm_bwd(dz_rot, raw, g):
            dzn = _rope_t(dz_rot, ta_ref, 16)
            raw = raw.astype(F32)
            rr = lax.rsqrt(_seg_sum(raw * raw, seg) * (1.0 / HD) + EPS)
            zhat = raw * rr
            dzh = dzn * g
            draw = rr * (dzh - zhat * (_seg_sum(dzh * zhat, seg) * (1.0 / HD)))
            return draw, jnp.sum(dzn * zhat, axis=0, keepdims=True)

        gq = jnp.zeros((1, 128), F32)
        for p in range(4):
            sl = slice(128 * p, 128 * p + 128)
            draw, gsum = norm_bwd(dqa_ref[sl, :].T * 0.125, qar_ref[:, sl], qg_ref[...])
            gq = gq + gsum
            dp_ref[:, sl] = draw.astype(BF16)
            dp_ref[:, 768 + 128 * p:768 + 128 * p + 128] = _rope_t(dqb_ref[:, sl] * 0.125, tb_ref, 32).astype(BF16)
        draw, gk = norm_bwd(dka_ref[...].T * LN2, kar_ref[...], kg_ref[...])
        dp_ref[:, 512:640] = draw.astype(BF16)
        dp_ref[:, 640:768] = dva_ref[...].T.astype(BF16)
        dp_ref[:, 1280:1408] = _rope_t(dkb_ref[...] * LN2, tb_ref, 32).astype(BF16)
        dp_ref[:, 1408:1536] = dvb_ref[...].astype(BF16)
        dp_ref[:, 1536:2560] = dga_ref[...]
        dp_ref[:, 2560:3584] = dgb_ref[...]
        st_ref[0:1, :] += gq
        st_ref[1:2, :] += gk

    tab = pl.BlockSpec((3, tm, 128), lambda i: (0, i, 0))
    chunk_t = pl.BlockSpec((None, 128, tm), lambda i: (i // per, 0, i % per))
    return pl.pallas_call(
        body, grid=(S // tm,),
        in_specs=[pl.BlockSpec((BW, tm), lambda i: (0, i)), chunk_t, chunk_t, _rows(tm, BW), _rows(tm, 128), _rows(tm, 128),
                  _rows(tm, BW), _rows(tm, 128), _const((1, 128)), _const((1, 128)), tab, tab, _rows(tm, D), _rows(tm, D)],
        out_specs=[_rows(tm, INW), _const((8, 128))],
        out_shape=[jax.ShapeDtypeStruct((S, INW), BF16), jax.ShapeDtypeStruct((8, 128), F32)],
        compiler_params=_params(1), name="qk_bwd",
    )(dqa_t, dka_t3, dva_t3, dqb, dkb, dvb, qar, kar, qg2, kg2, tab_a, tab_b, dga, dgb)


def _in_bwd(dproj, win, x, dx1, modv, n1g, chunks):
    S = x.shape[0]
    tm = min(512, S)
    n = S // tm
    ts = min(256, tm)

    def body(dp_ref, w_ref, x_ref, dx1_ref, mod_ref, g_ref, c_hbm, gx_ref, st_ref, recv_hbm, send_sems, recv_sems):
        start, finish = _exchange_scatter([c_hbm], [recv_hbm], send_sems, recv_sems)

        @pl.when(pl.program_id(0) == 0)
        def _():
            start()
            st_ref[...] = jnp.zeros_like(st_ref)

        subs = [slice(ts * u, ts * u + ts) for u in range(tm // ts)]
        dhs = [lax.dot_general(dp_ref[rows, :], w_ref[...], NT, preferred_element_type=F32) for rows in subs]
        for rows, dh in zip(subs, dhs):
            xt = x_ref[rows, :]
            r = lax.rsqrt(jnp.mean(xt * xt, axis=-1, keepdims=True) + EPS)
            xn = xt * r
            st_ref[0:1, :] += jnp.sum(dh, axis=0, keepdims=True)
            st_ref[1:2, :] += jnp.sum(dh * xn, axis=0, keepdims=True)
            dxn = dh * (g_ref[...] * (1.0 + mod_ref[1:2, :]))
            gx_ref[rows, :] = dx1_ref[rows, :] + r * (dxn - xn * jnp.mean(dxn * xn, axis=-1, keepdims=True))
        pl.when(pl.program_id(0) == n - 1)(finish)

    any_spec = pl.BlockSpec(memory_space=pl.ANY)
    return pl.pallas_call(
        body, grid=(n,),
        in_specs=[_rows(tm, INW), _const((D, INW)), _rows(tm, D), _rows(tm, D), _const((6, D)), _const((1, D)), any_spec],
        out_specs=[_rows(tm, D), _const((8, D)), any_spec],
        out_shape=[jax.ShapeDtypeStruct((S, D), F32), jax.ShapeDtypeStruct((8, D), F32),
                   jax.ShapeDtypeStruct((N_DEV - 1,) + chunks.shape[1:], chunks.dtype)],
        scratch_shapes=_exchange_sems(1),
        compiler_params=_params(1), name="in_bwd",
    )(dproj, win, x, dx1, modv, n1g, chunks)


def _pack_small(st1, st2, stf, dg1, dg2, stqk, dsink, modv, n1g, n2g):
    def body(st1_ref, st2_ref, stf_ref, dg1_ref, dg2_ref, qk_ref, ds_ref, mod_ref, g1_ref, g2_ref, o_ref):
        a1, b1 = st1_ref[0:1, :], st1_ref[1:2, :]
        a2, b2 = st2_ref[0:1, :], st2_ref[1:2, :]
        r = lax.broadcasted_iota(jnp.int32, (128, D), 0)
        c = lax.broadcasted_iota(jnp.int32, (128, D), 1)
        fold_q = (c == r % HD).astype(F32)
        fold_k = (c == HD + r % HD).astype(F32)
        keep = (c == r).astype(F32)

        def place(v, sel):
            return jnp.dot(v, sel, preferred_element_type=F32, precision=lax.Precision.HIGHEST)

        loss = jnp.sum(stf_ref[1:2, :], axis=1, keepdims=True)
        lane = lax.broadcasted_iota(jnp.int32, (1, D), 1)
        rows = [a1, g1_ref[...] * b1, dg1_ref[0:1, :], a2, g2_ref[...] * b2, dg2_ref[0:1, :],
                (1.0 + mod_ref[1:2, :]) * b1, (1.0 + mod_ref[4:5, :]) * b2, stf_ref[0:1, :],
                place(qk_ref[0:1, :], fold_q) + place(qk_ref[1:2, :], fold_k),
                place(ds_ref[0:1, :], keep),
                jnp.where(lane == 0, loss, 0.0)]
        rows += [jnp.zeros((1, D), F32)] * (SMALL_ROWS - len(rows))
        for n, v in enumerate(rows):
            o_ref[n:n + 1, :] = v

    return pl.pallas_call(
        body, out_shape=jax.ShapeDtypeStruct((SMALL_ROWS, D), F32),
        compiler_params=pltpu.CompilerParams(vmem_limit_bytes=V7X_VMEM_LIMIT), name="pack_small",
    )(st1, st2, stf, dg1, dg2, stqk, dsink, modv, n1g, n2g)


def _wada_grad(silu_all, dmod_cols):
    def body(a_ref, b_ref, o_ref):
        o_ref[...] = lax.dot_general(a_ref[...], b_ref[...], TN, preferred_element_type=F32, precision=lax.Precision.HIGHEST)

    return pl.pallas_call(
        body, out_shape=jax.ShapeDtypeStruct((D, dmod_cols.shape[1]), F32),
        compiler_params=pltpu.CompilerParams(vmem_limit_bytes=V7X_VMEM_LIMIT), name="wada_grad",
    )(silu_all, dmod_cols)


def _adamw_sum(parts, w, m, v, name):
    R, C = w.shape
    tr = R if R <= 64 else next(t for t in (256, 128, 64, 32, 16, 8) if R % t == 0)
    n = len(parts)
    dyn = [idx for _, idx in parts if idx is not None and not isinstance(idx, int)]
    b1c = 1.0 - ADAM_B1 ** ADAM_STEP
    b2c = 1.0 - ADAM_B2 ** ADAM_STEP

    def body(*refs):
        refs = refs[len(dyn):]
        g = refs[0][...].astype(F32)
        for k in range(1, n):
            g = g + refs[k][...].astype(F32)
        w_ref, m_ref, v_ref, g_out, d_out, m_out, v_out = refs[n:]
        mn = ADAM_B1 * m_ref[...] + (1.0 - ADAM_B1) * g
        vn = ADAM_B2 * v_ref[...] + (1.0 - ADAM_B2) * jnp.square(g)
        g_out[...] = g
        m_out[...] = mn
        v_out[...] = vn
        d_out[...] = -ADAM_LR * ((mn / b1c) / (jnp.sqrt(vn / b2c) + ADAM_EPS) + ADAM_WD * w_ref[...])

    in_specs = []
    nd = 0
    for a, idx in parts:
        if idx is None:
            in_specs.append(pl.BlockSpec((tr, C), lambda i, *s: (i, 0)))
        elif isinstance(idx, int):
            in_specs.append(pl.BlockSpec((None, tr, C), lambda i, *s, idx=idx: (idx, i, 0)))
        else:
            in_specs.append(pl.BlockSpec((None, tr, C), lambda i, *s, nd=nd: (s[nd][0], i, 0)))
            nd += 1
    blk = pl.BlockSpec((tr, C), lambda i, *s: (i, 0))
    grid_spec = pltpu.PrefetchScalarGridSpec(
        num_scalar_prefetch=len(dyn), grid=(R // tr,), in_specs=in_specs + [blk] * 3, out_specs=[blk] * 4)
    return pl.pallas_call(
        body, grid_spec=grid_spec, out_shape=[jax.ShapeDtypeStruct((R, C), F32)] * 4,
        compiler_params=_params(1), name=name,
    )(*dyn, *[a for a, _ in parts], w, m, v)


def _me():
    return lax.axis_index("x"), lax.axis_index("y"), lax.axis_index("c")


def _peer(k):
    x, y, c = _me()
    return (x ^ ((k >> 2) & 1), y ^ ((k >> 1) & 1), c ^ (k & 1))


def _ada_exchange(c_row, w_ada, b_rows):
    NW = w_ada.shape[1]

    def body(c_ref, w_ref, b_ref, sall_ref, mod_ref, src_ref, mp_ref, send1, recv1, send2, recv2):
        x, y, c = _me()
        me = 4 * x + 2 * y + c
        cv = c_ref[...]
        src_ref[...] = jnp.broadcast_to(cv * jax.nn.sigmoid(cv), (8, D))
        mine = pl.ds(pl.multiple_of(me * 8, 8), 8)
        sall_ref[mine, :] = src_ref[...]
        sends = [pltpu.make_async_remote_copy(src_ref, sall_ref.at[mine, :], send1.at[k - 1], recv1.at[k - 1],
                                              device_id=_peer(k), device_id_type=MESH) for k in range(1, N_DEV)]
        for cp in sends:
            cp.start()
        for k in range(1, N_DEV):
            theirs = pl.ds(pl.multiple_of((me ^ k) * 8, 8), 8)
            pltpu.make_async_remote_copy(src_ref, sall_ref.at[theirs, :], send1.at[k - 1], recv1.at[k - 1],
                                         device_id=_peer(k), device_id_type=MESH).wait_recv()
        for cp in sends:
            cp.wait_send()
        mp_ref[...] = jnp.dot(sall_ref[...], w_ref[...], preferred_element_type=F32, precision=lax.Precision.HIGHEST)
        mod_ref[mine, :] = mp_ref[mine, :] + b_ref[mine, :]
        sends = []
        for k in range(1, N_DEV):
            theirs = pl.ds(pl.multiple_of((me ^ k) * 8, 8), 8)
            sends.append(pltpu.make_async_remote_copy(mp_ref.at[theirs, :], mod_ref.at[mine, :], send2.at[k - 1], recv2.at[k - 1],
                                                      device_id=_peer(k), device_id_type=MESH))
        for cp in sends:
            cp.start()
        for k in range(1, N_DEV):
            theirs = pl.ds(pl.multiple_of((me ^ k) * 8, 8), 8)
            pltpu.make_async_remote_copy(mp_ref.at[mine, :], mod_ref.at[theirs, :], send2.at[k - 1], recv2.at[k - 1],
                                         device_id=_peer(k), device_id_type=MESH).wait_recv()
            mod_ref[theirs, :] = mod_ref[theirs, :] + b_ref[theirs, :]
        for cp in sends:
            cp.wait_send()

    vm = pl.BlockSpec(memory_space=pltpu.VMEM)
    return pl.pallas_call(
        body, in_specs=[vm, vm, vm], out_specs=[vm, vm],
        out_shape=[jax.ShapeDtypeStruct((8 * N_DEV, D), F32), jax.ShapeDtypeStruct((8 * N_DEV, NW), F32)],
        scratch_shapes=[pltpu.VMEM((8, D), F32), pltpu.VMEM((8 * N_DEV, NW), F32)]
        + [pltpu.SemaphoreType.DMA((N_DEV - 1,))] * 4,
        compiler_params=pltpu.CompilerParams(vmem_limit_bytes=V7X_VMEM_LIMIT), name="ada_exchange",
    )(c_row, w_ada, b_rows)


def _weight_gather(shard):
    def body(x_ref, out_ref, send_sems, recv_sems, local_sem):
        x, y, c = _me()
        me, sibling = (x, y, c), (x, y, 1 - c)
        chips = [(1 - x, y), (x, 1 - y), (1 - x, 1 - y)]

        def slot(px, py, pc):
            return out_ref.at[4 * px + 2 * py + pc]

        def copy(k, block, to, src=None):
            return pltpu.make_async_remote_copy(
                src_ref=slot(*block) if src is None else src, dst_ref=slot(*block),
                send_sem=send_sems.at[k], recv_sem=recv_sems.at[k], device_id=to, device_id_type=MESH)

        mine = pltpu.make_async_copy(x_ref, slot(*me), local_sem)
        mine.start()
        first = [copy(0, me, sibling, src=x_ref)]
        first += [copy(1 + j, me, (*chip, c), src=x_ref) for j, chip in enumerate(chips)]
        for cp in first:
            cp.start()
        passed = [copy(4 + j, (*chip, c), sibling) for j, chip in enumerate(chips)]
        for j, chip in enumerate(chips):
            copy(1 + j, (*chip, c), me).wait_recv()
            passed[j].start()
        copy(0, sibling, me).wait_recv()
        for j, chip in enumerate(chips):
            copy(4 + j, (*chip, 1 - c), me).wait_recv()
        for cp in first + passed:
            cp.wait_send()
        mine.wait()

    any_spec = pl.BlockSpec(memory_space=pl.ANY)
    return pl.pallas_call(
        body, in_specs=[any_spec], out_specs=any_spec,
        out_shape=jax.ShapeDtypeStruct((N_DEV,) + shard.shape, shard.dtype),
        scratch_shapes=[pltpu.SemaphoreType.DMA((7,)), pltpu.SemaphoreType.DMA((7,)), pltpu.SemaphoreType.DMA],
        name="weight_gather",
    )(shard)


def _small_gather(block):
    def body(b_ref, out_ref, send_sems, recv_sems):
        x, y, c = _me()
        me = 4 * x + 2 * y + c
        out_ref[me] = b_ref[...]
        sends = [pltpu.make_async_remote_copy(b_ref, out_ref.at[me], send_sems.at[k - 1], recv_sems.at[k - 1],
                                              device_id=_peer(k), device_id_type=MESH) for k in range(1, N_DEV)]
        for cp in sends:
            cp.start()
        for k in range(1, N_DEV):
            pltpu.make_async_remote_copy(b_ref, out_ref.at[me ^ k], send_sems.at[k - 1], recv_sems.at[k - 1],
                                         device_id=_peer(k), device_id_type=MESH).wait_recv()
        for cp in sends:
            cp.wait_send()

    vm = pl.BlockSpec(memory_space=pltpu.VMEM)
    return pl.pallas_call(
        body, in_specs=[vm], out_specs=vm,
        out_shape=jax.ShapeDtypeStruct((N_DEV,) + block.shape, block.dtype),
        scratch_shapes=[pltpu.SemaphoreType.DMA((N_DEV - 1,)), pltpu.SemaphoreType.DMA((N_DEV - 1,))],
        name="small_gather",
    )(block)


def _pack_small_params(b_ada, n1, n2, fg, qn, kn, sink):
    z = jnp.zeros((SMALL_ROWS, D), F32)
    z = z.at[0:6].set(b_ada.reshape(6, D)).at[6].set(n1.reshape(D)).at[7].set(n2.reshape(D)).at[8].set(fg.reshape(D))
    z = z.at[9, 0:HD].set(qn.reshape(HD)).at[9, HD:2 * HD].set(kn.reshape(HD)).at[10, 0:NH].set(sink.reshape(NH))
    return z


def _unpack_small(p):
    return (p[0:6].reshape(1, 6 * D), p[6].reshape(1, D), p[9, 0:HD].reshape(1, HD), p[9, HD:2 * HD].reshape(1, HD),
            p[10, 0:NH].reshape(1, NH), p[7].reshape(1, D), p[8].reshape(D))


def kernel(x, c, w_ada, b_ada, norm1_g, w_in, q_norm_a, k_norm_a, sink_b, w_branch, w_out, norm2_g, w_mlp_in, w_mlp_out, final_g, loss_target, m_w_ada, m_b_ada, m_norm1_g, m_w_in, m_q_norm_a, m_k_norm_a, m_sink_b, m_w_branch, m_w_out, m_norm2_g, m_w_mlp_in, m_w_mlp_out, m_final_g, v_w_ada, v_b_ada, v_norm1_g, v_w_in, v_q_norm_a, v_k_norm_a, v_sink_b, v_w_branch, v_w_out, v_norm2_g, v_w_mlp_in, v_w_mlp_out, v_final_g):
    S = x.shape[1]
    xs = x.reshape(S, D)
    tgt = loss_target.reshape(S, D)
    ax, ay, ac = lax.axis_index("x"), lax.axis_index("y"), lax.axis_index("c")
    me = 4 * ax + 2 * ay + ac
    me1 = me.reshape(1).astype(jnp.int32)
    NW = w_ada.shape[2]
    NI = w_in.shape[2]

    silu64, mod64 = _ada_exchange(c.reshape(1, D), w_ada.reshape(D, NW),
                                  jnp.repeat(b_ada.reshape(N_DEV, NW), 8, axis=0))
    silu_all = silu64[0::8]
    modv = mod64[0::8].reshape(6, D)

    win = _weight_gather(w_in[0].astype(BF16)).transpose(1, 0, 2).reshape(D, INW)
    rest_shards = tuple(w[0].astype(BF16) for w in (w_branch, w_out, w_mlp_in, w_mlp_out))

    tab_a, tab_b = _rope_tables(S)
    qg2 = jnp.tile(q_norm_a.reshape(1, HD), (1, 2))
    kg2 = jnp.tile(k_norm_a.reshape(1, HD), (1, 2))
    n1g = norm1_g.reshape(1, D)
    n2g = norm2_g.reshape(1, D)
    fg = final_g.reshape(1, D)
    sink2 = sink_b.reshape(1, NH) * LOG2E

    h, qar, kar, qa, ka, va, qb, kb, vb, ga, gb, qa_t, ka_t3, va_t3 = _in_proj(xs, modv, n1g, win, qg2, kg2, tab_a, tab_b)
    ya, lse_at, wb, wout, wmi, wmo = _attn_a_fwd(qa_t, ka, va_t3, rest_shards)
    lse_a = lse_at.T
    wb = wb.transpose(1, 2, 0, 3).reshape(2, BW, D)
    wout = wout.reshape(D, D)
    pad = ((WIN, WIN), (0, 0))
    kbp, vbp = jnp.pad(kb, pad), jnp.pad(vb, pad)
    tb = min(TQ_B, S)
    yb, lse_bt = _attn_b_fwd(qb, kbp, vbp.reshape((S + 2 * WIN) // tb, tb, 128).transpose(0, 2, 1), sink2)
    lse_b = lse_bt.T
    x1, merged, ua, ub = _merge_out(ya, yb, ga, gb, xs, modv, wb, wout)
    h2, hp, dx2, stf = _mlp_fwd(x1, modv, n2g, wmi, wmo, fg, tgt)

    dhp, dx1, st2 = _mlp_bwd(dx2, x1, hp, modv, n2g, wmi, wmo)
    m2 = _tn_matmul(hp, dx2, 2048, D, "dw_mlp_out", relu_sq=True)
    g_wmo, dg2 = _scale_gate(m2, wmo.reshape(FF, D), modv, 5, "gate2_grad")
    g_wmi = _tn_matmul(h2, dhp, D, 512, "dw_mlp_in", dev_major=True)
    dua, dub, dga, dgb, dya, dyb, dl_a, dl_b, dya_t = _merge_bwd(dx1, modv, ga, gb, ua, ub, ya, yb, wb, wout)
    m1 = _tn_matmul(merged, dx1, D, D, "dw_out")
    g_wout, dg1 = _scale_gate(m1, wout, modv, 2, "gate1_grad")
    g_wb0 = _tn_matmul(ya, dua, BW, D, "dw_branch_a")
    g_wb1 = _tn_matmul(yb, dub, BW, D, "dw_branch_b")
    g_wb = jnp.stack([g_wb0, g_wb1]).reshape(2, BW, N_DEV, 128).transpose(2, 0, 1, 3).reshape(N_DEV, 2 * BW, 128)
    g_wout = g_wout.reshape(N_DEV, 128, D)
    g_wmo = g_wmo.reshape(N_DEV, 512, D)
    dqa_t, dka_t3, dva_t3, r_wb, r_wout, r_wmi, r_wmo = _attn_a_bwd(qa, qa_t, ka_t3, va_t3, dya, dya_t, lse_a, dl_a,
                                                                    (g_wb, g_wout, g_wmi, g_wmo))
    dqb, dkb_t, dvb_t, dsink = _attn_b_bwd(qb, kbp, vbp, sink2, dyb, lse_b, dl_b)
    dkb = dkb_t.transpose(0, 2, 1).reshape(S + 2 * WIN, 128)[WIN:WIN + S]
    dvb = dvb_t.transpose(0, 2, 1).reshape(S + 2 * WIN, 128)[WIN:WIN + S]
    dproj, stqk = _qk_bwd(dqa_t, dka_t3, dva_t3, dqb, dkb, dvb, qar, kar, qg2, kg2, tab_a, tab_b, dga, dgb)
    g_win = _tn_matmul(h, dproj, D, 896, "dw_in")
    g_win = g_win.reshape(D, N_DEV, NI).transpose(1, 0, 2)
    grad_x, st1, r_win = _in_bwd(dproj, win, xs, dx1, modv, n1g, g_win.astype(BF16))

    def adam(name, own, recv, w, m, v):
        shape = w.shape
        w2, m2_, v2 = (a.reshape(own.shape[1:]) for a in (w, m, v))
        outs = _adamw_sum([(own, me1)] + [(recv, k) for k in range(N_DEV - 1)], w2, m2_, v2, name)
        return [a.reshape(shape) for a in outs]

    o_win = adam("adamw_w_in", g_win, r_win, w_in, m_w_in, v_w_in)
    o_wb = adam("adamw_w_branch", g_wb, r_wb, w_branch, m_w_branch, v_w_branch)
    o_wout = adam("adamw_w_out", g_wout, r_wout, w_out, m_w_out, v_w_out)
    o_wmi = adam("adamw_w_mlp_in", g_wmi, r_wmi, w_mlp_in, m_w_mlp_in, v_w_mlp_in)
    o_wmo = adam("adamw_w_mlp_out", g_wmo, r_wmo, w_mlp_out, m_w_mlp_out, v_w_mlp_out)

    small = _pack_small(st1, st2, stf, dg1, dg2, stqk, dsink, modv, n1g, n2g)
    small_all = _small_gather(small)
    sw = _pack_small_params(b_ada, norm1_g, norm2_g, final_g, q_norm_a, k_norm_a, sink_b)
    sm = _pack_small_params(m_b_ada, m_norm1_g, m_norm2_g, m_final_g, m_q_norm_a, m_k_norm_a, m_sink_b)
    sv = _pack_small_params(v_b_ada, v_norm1_g, v_norm2_g, v_final_g, v_q_norm_a, v_k_norm_a, v_sink_b)
    sm_out = _adamw_sum([(small_all, k) for k in range(N_DEV)], sw, sm, sv, "adamw_small")
    loss = sm_out[0][11, 0]
    sm_out = [_unpack_small(a) for a in sm_out]

    dmod_all = small_all[:, 0:6, :].reshape(N_DEV, 6 * D)
    dmod_cols = lax.dynamic_slice_in_dim(dmod_all, me * NW, NW, axis=1)
    g_wada = _wada_grad(silu_all, dmod_cols)
    ada = _adamw_sum([(g_wada, None)], w_ada.reshape(D, NW), m_w_ada.reshape(D, NW), v_w_ada.reshape(D, NW), "adamw_ada")
    ada = [a.reshape(1, D, NW) for a in ada]

    def leaves(k):
        b_, n1_, qn_, kn_, sk_, n2_, fg_ = sm_out[k]
        return [ada[k], b_, n1_, o_win[k], qn_, kn_, sk_, o_wb[k], o_wout[k], n2_, o_wmi[k], o_wmo[k], fg_]

    return (loss, grad_x.reshape(1, S, D), *leaves(0), *leaves(1), *leaves(2), *leaves(3))
```

```python
import jax
import jax.numpy as jnp
from jax import lax
from jax.experimental import pallas as pl
from jax.experimental.pallas import tpu as pltpu

F32, BF16 = jnp.float32, jnp.bfloat16
MESH = pl.DeviceIdType.MESH

D = 1024
HD = 64
NH = 8
GRP = 4
BW = 512
FF = 4096
INW = 3584
GRID_W = 64
WIN = 128
THETA = 10000.0
EPS = 1e-6
NEG = -1e30
N_DEV = 8
LOG2E = 1.4426950408889634
LN2 = 0.6931471805599453
QA_SCALE = 0.125 * LOG2E
SMALL_ROWS = 16
MLP_SHARDS = 4
V7X_VMEM_LIMIT = 56 * 1024 * 1024

ADAM_LR, ADAM_B1, ADAM_B2, ADAM_EPS, ADAM_WD, ADAM_STEP = 0.001, 0.9, 0.999, 1e-08, 0.01, 10

NT = (((1,), (1,)), ((), ()))
TN = (((0,), (0,)), ((), ()))


def _params(n_axes, vmem=V7X_VMEM_LIMIT):
    return pltpu.CompilerParams(dimension_semantics=("arbitrary",) * n_axes, vmem_limit_bytes=vmem)


def _const(shape):
    return pl.BlockSpec(shape, lambda *_: (0,) * len(shape))


def _rows(tm, width):
    return pl.BlockSpec((tm, width), lambda i, *_: (i, 0))


def _seg_matrix(n, seg):
    r = lax.broadcasted_iota(jnp.int32, (n, n), 0) // seg
    c = lax.broadcasted_iota(jnp.int32, (n, n), 1) // seg
    return (r == c).astype(BF16)


def _seg_sum(z, seg_mat):
    hi = z.astype(BF16)
    lo = (z - hi.astype(F32)).astype(BF16)
    return jnp.dot(hi, seg_mat, preferred_element_type=F32) + jnp.dot(lo, seg_mat, preferred_element_type=F32)


def _rope(z, t_ref, sh):
    return z * t_ref[0] + pltpu.roll(z, sh, 1) * t_ref[1] + pltpu.roll(z, 128 - sh, 1) * t_ref[2]


def _rope_t(dz, t_ref, sh):
    return dz * t_ref[0] + pltpu.roll(dz * t_ref[1], 128 - sh, 1) + pltpu.roll(dz * t_ref[2], sh, 1)


def _rope_tables(S):
    t = jnp.arange(S, dtype=jnp.int32)
    lane = jnp.arange(HD)

    def build(cos, sin, first):
        t0 = cos
        t1 = jnp.where(first[None, :], 0.0, sin)
        t2 = jnp.where(first[None, :], -sin, 0.0)
        return jnp.stack([jnp.tile(a, (1, 2)) for a in (t0, t1, t2)]).astype(F32)

    inv_a = THETA ** (-jnp.arange(0, HD // 2, 2, dtype=F32) / (HD // 2))
    ar = (t // GRID_W).astype(F32)[:, None] * inv_a[None, :]
    ac = (t % GRID_W).astype(F32)[:, None] * inv_a[None, :]
    cos_a = jnp.concatenate([jnp.cos(ar), jnp.cos(ar), jnp.cos(ac), jnp.cos(ac)], axis=1)
    sin_a = jnp.concatenate([jnp.sin(ar), jnp.sin(ar), jnp.sin(ac), jnp.sin(ac)], axis=1)
    tab_a = build(cos_a, sin_a, (lane % 32) < 16)
    inv_b = THETA ** (-jnp.arange(0, HD, 2, dtype=F32) / HD)
    ab = t.astype(F32)[:, None] * inv_b[None, :]
    cos_b = jnp.concatenate([jnp.cos(ab), jnp.cos(ab)], axis=1)
    sin_b = jnp.concatenate([jnp.sin(ab), jnp.sin(ab)], axis=1)
    tab_b = build(cos_b, sin_b, lane < 32)
    return tab_a, tab_b


def _in_proj(x, modv, n1g, win, qg2, kg2, tab_a, tab_b):
    S = x.shape[0]
    tm = min(512, S)
    tk = min(512, S)
    per = tk // tm
    ts = min(256, tm)

    def body(x_ref, mod_ref, g_ref, w_ref, qg_ref, kg_ref, ta_ref, tb_ref,
             h_ref, qar_ref, kar_ref, qa_ref, ka_ref, va_ref, qb_ref, kb_ref, vb_ref, ga_ref, gb_ref, qat_ref, kat_ref, vat_ref):
        seg = _seg_matrix(128, HD)

        def head_norm(z, g):
            ms = _seg_sum(z * z, seg) * (1.0 / HD)
            return (z * lax.rsqrt(ms + EPS)) * g

        subs = [slice(ts * u, ts * u + ts) for u in range(tm // ts)]
        hbs = []
        for rows in subs:
            xt = x_ref[rows, :]
            r = lax.rsqrt(jnp.mean(xt * xt, axis=-1, keepdims=True) + EPS)
            h = ((xt * r) * g_ref[...]) * (1.0 + mod_ref[1:2, :]) + mod_ref[0:1, :]
            hbs.append(h.astype(BF16))
            h_ref[rows, :] = hbs[-1]
        projs = [jnp.dot(hb, w_ref[...], preferred_element_type=F32) for hb in hbs]
        for rows, proj in zip(subs, projs):
            ta, tb = ta_ref[:, rows, :], tb_ref[:, rows, :]
            for p in range(4):
                z = proj[:, 128 * p:128 * p + 128]
                qar_ref[rows, 128 * p:128 * p + 128] = z.astype(BF16)
                qv = _rope(head_norm(z, qg_ref[...]), ta, 16) * QA_SCALE
                qa_ref[rows, 128 * p:128 * p + 128] = qv.astype(BF16)
                qat_ref[128 * p:128 * p + 128, rows] = qv.T.astype(BF16)
                zb = proj[:, 768 + 128 * p:768 + 128 * p + 128]
                qb_ref[rows, 128 * p:128 * p + 128] = (_rope(zb, tb, 32) * QA_SCALE).astype(BF16)
            z = proj[:, 512:640]
            kar_ref[rows, :] = z.astype(BF16)
            kv_ = _rope(head_norm(z, kg_ref[...]), ta, 16)
            ka_ref[rows, :] = kv_.astype(BF16)
            kat_ref[:, rows] = kv_.T.astype(BF16)
            va_ref[rows, :] = proj[:, 640:768].astype(BF16)
            vat_ref[:, rows] = proj[:, 640:768].T.astype(BF16)
            kb_ref[rows, :] = _rope(proj[:, 1280:1408], tb, 32).astype(BF16)
            vb_ref[rows, :] = proj[:, 1408:1536].astype(BF16)
            ga_ref[rows, :] = proj[:, 1536:2560].astype(BF16)
            gb_ref[rows, :] = proj[:, 2560:3584].astype(BF16)

    tab = pl.BlockSpec((3, tm, 128), lambda i: (0, i, 0))
    shapes = [(D, BF16), (BW, BF16), (128, BF16), (BW, BF16), (128, BF16), (128, BF16),
              (BW, BF16), (128, BF16), (128, BF16), (D, BF16), (D, BF16)]
    return pl.pallas_call(
        body, grid=(S // tm,),
        in_specs=[_rows(tm, D), _const((6, D)), _const((1, D)), _const((D, INW)), _const((1, 128)), _const((1, 128)), tab, tab],
        out_specs=[_rows(tm, w) for w, _ in shapes] + [pl.BlockSpec((BW, tm), lambda i: (0, i))]
        + [pl.BlockSpec((None, 128, tm), lambda i: (i // per, 0, i % per))] * 2,
        out_shape=[jax.ShapeDtypeStruct((S, w), dt) for w, dt in shapes] + [jax.ShapeDtypeStruct((BW, S), BF16)]
        + [jax.ShapeDtypeStruct((S // tk, 128, tk), BF16)] * 2,
        compiler_params=_params(1), name="in_proj",
    )(x, modv, n1g, win, qg2, kg2, tab_a, tab_b)


def _exchange_gather(block_refs, out_refs, send_sems, recv_sems, local_sems):
    x, y, c = _me()
    me = 4 * x + 2 * y + c

    def copies():
        own, out, arrive = [], [], []
        for a, (blk, dst) in enumerate(zip(block_refs, out_refs)):
            own.append(pltpu.make_async_copy(blk, dst.at[me], local_sems.at[a]))
            for k in range(1, N_DEV):
                sems = dict(send_sem=send_sems.at[a, k - 1], recv_sem=recv_sems.at[a, k - 1], device_id=_peer(k), device_id_type=MESH)
                out.append(pltpu.make_async_remote_copy(blk, dst.at[me], **sems))
                arrive.append(pltpu.make_async_remote_copy(blk, dst.at[me ^ k], **sems))
        return own, out, arrive

    def start():
        own, out, _ = copies()
        for cp in own + out:
            cp.start()

    def finish():
        own, out, arrive = copies()
        for cp in arrive:
            cp.wait_recv()
        for cp in out:
            cp.wait_send()
        for cp in own:
            cp.wait()

    return start, finish


def _exchange_scatter(chunk_refs, recv_refs, send_sems, recv_sems):
    x, y, c = _me()
    me = 4 * x + 2 * y + c

    def copies():
        return [pltpu.make_async_remote_copy(src.at[me ^ k], dst.at[k - 1], send_sems.at[a, k - 1], recv_sems.at[a, k - 1],
                                             device_id=_peer(k), device_id_type=MESH)
                for a, (src, dst) in enumerate(zip(chunk_refs, recv_refs)) for k in range(1, N_DEV)]

    def start():
        for cp in copies():
            cp.start()

    def finish():
        cps = copies()
        for cp in cps:
            cp.wait_recv()
        for cp in cps:
            cp.wait_send()

    return start, finish


def _exchange_sems(n):
    return [pltpu.SemaphoreType.DMA((n, N_DEV - 1)), pltpu.SemaphoreType.DMA((n, N_DEV - 1))]


def _attn_a_fwd(qt, k, vt3, shards):
    S = qt.shape[1]
    tq = min(512, S)
    nq = S // tq
    nk, _, tk = vt3.shape
    ONES = 16
    AHEAD = 2
    ns = len(shards)

    def body(q_ref, k_ref, vt_ref, *rest):
        w_hbm, (o_ref, lse_ref), wall_hbm = rest[:ns], rest[ns:ns + 2], rest[ns + 2:2 * ns + 2]
        st_sc, send_sems, recv_sems, local_sems = rest[2 * ns + 2:]
        start, finish = _exchange_gather(w_hbm, wall_hbm, send_sems, recv_sems, local_sems)
        pl.when(pl.program_id(0) == 0)(start)
        row8 = lax.broadcasted_iota(jnp.int32, (NH, tq), 0)
        lse_all = jnp.zeros((NH, tq), F32)
        ones = jnp.ones((ONES, tk), BF16)
        for kv in range(2):
            qts = [q_ref[HD * (GRP * kv + t):HD * (GRP * kv + t) + HD, :] for t in range(GRP)]

            def keys(j, kv=kv):
                return k_ref[pl.ds(pl.multiple_of(j * tk, tk), tk), :][:, HD * kv:HD * kv + HD]

            def scores(kj, t, qts=qts):
                return jnp.dot(kj, qts[t], preferred_element_type=F32)

            def step(j, carry, kv=kv):
                kj = keys(j)
                kn = keys(jnp.minimum(j + 1, nk - 1))
                v1 = jnp.concatenate([vt_ref[j, HD * kv:HD * kv + HD, :], ones], axis=0)
                sts = [st_sc[t] for t in range(AHEAD)]
                new = []
                for t in range(GRP):
                    m, acc = carry[2 * t], carry[2 * t + 1]
                    if t + AHEAD < GRP:
                        sts.append(scores(kj, t + AHEAD))
                    st = sts[t]
                    mn = jnp.maximum(m, jnp.max(st, axis=0, keepdims=True))
                    pt = jnp.exp2(st - mn)
                    if t + AHEAD >= GRP:
                        st_sc[t + AHEAD - GRP] = scores(kn, t + AHEAD - GRP)
                    acc = jnp.exp2(m - mn) * acc + jnp.dot(v1, pt.astype(BF16), preferred_element_type=F32)
                    new += [mn, acc]
                return tuple(new)

            k0 = keys(0)
            for t in range(AHEAD):
                st_sc[t] = scores(k0, t)
            init = (jnp.full((1, tq), NEG, F32), jnp.zeros((HD + ONES, tq), F32)) * GRP
            res = lax.fori_loop(0, nk, step, init)
            outs = []
            for t in range(GRP):
                m, acc = res[2 * t], res[2 * t + 1]
                l = acc[HD:HD + 1, :]
                outs.append((acc[:HD, :] / l).T)
                lse_all = jnp.where(row8 == GRP * kv + t, m + jnp.log2(l), lse_all)
            o_ref[:, 256 * kv:256 * kv + 256] = jnp.concatenate(outs, axis=1).astype(BF16)
        lse_ref[...] = lse_all
        pl.when(pl.program_id(0) == nq - 1)(finish)

    any_spec = pl.BlockSpec(memory_space=pl.ANY)
    return pl.pallas_call(
        body, grid=(nq,),
        in_specs=[pl.BlockSpec((BW, tq), lambda i: (0, i)), _const((S, 128)), _const((nk, 128, tk))] + [any_spec] * ns,
        out_specs=[_rows(tq, BW), pl.BlockSpec((NH, tq), lambda i: (0, i))] + [any_spec] * ns,
        out_shape=[jax.ShapeDtypeStruct((S, BW), BF16), jax.ShapeDtypeStruct((NH, S), F32)]
        + [jax.ShapeDtypeStruct((N_DEV,) + s.shape, s.dtype) for s in shards],
        scratch_shapes=[pltpu.VMEM((AHEAD, tk, tq), F32)] + _exchange_sems(ns) + [pltpu.SemaphoreType.DMA((ns,))],
        compiler_params=_params(1), name="attn_a_fwd",
    )(qt, k, vt3, *shards)


def _window_mask(i, tq, S):
    W = tq + 2 * WIN
    r = lax.broadcasted_iota(jnp.int32, (tq, W), 0)
    c = lax.broadcasted_iota(jnp.int32, (tq, W), 1)
    kpos = i * tq - WIN + c
    return (jnp.abs(c - WIN - r) <= WIN) & (kpos >= 0) & (kpos < S)


TQ_B = 256


def _attn_b_fwd(q, kp, vpt3, sink2):
    S = q.shape[0]
    tq = min(TQ_B, S)
    W = tq + 2 * WIN
    nc = vpt3.shape[0]
    ONES = 16

    def body(q_ref, k_ref, vt_ref, sink_ref, o_ref, lse_ref):
        i = pl.program_id(0)
        off = pl.multiple_of(i * tq, tq)
        r = lax.broadcasted_iota(jnp.int32, (W, tq), 1)
        c = lax.broadcasted_iota(jnp.int32, (W, tq), 0)
        kpos = i * tq - WIN + c
        valid = (jnp.abs(c - WIN - r) <= WIN) & (kpos >= 0) & (kpos < S)
        kw = k_ref[pl.ds(off, W), :]
        vt = jnp.concatenate([vt_ref[i + half] for half in range(W // tq)], axis=1)
        ones = jnp.ones((ONES, W), BF16)
        row8 = lax.broadcasted_iota(jnp.int32, (NH, tq), 0)
        lse_all = jnp.zeros((NH, tq), F32)
        qs = []
        for p in range(4):
            qp = q_ref[:, 128 * p:128 * p + 128]
            qs += [qp[:, :HD], qp[:, HD:]]
        khs = [kw[:, HD * kv:HD * kv + HD] for kv in range(2)]
        v1s = [jnp.concatenate([vt[HD * kv:HD * kv + HD, :], ones], axis=0) for kv in range(2)]

        def scores(h):
            return lax.dot_general(khs[h // GRP], qs[h], NT, preferred_element_type=F32)

        ss = [scores(0), scores(1)]
        outs = []
        for h in range(NH):
            if h + 2 < NH:
                ss.append(scores(h + 2))
            st = jnp.where(valid, ss[h], NEG)
            sk = sink_ref[:, h:h + 1]
            m = jnp.maximum(jnp.max(st, axis=0, keepdims=True), sk)
            acc = jnp.dot(v1s[h // GRP], jnp.exp2(st - m).astype(BF16), preferred_element_type=F32)
            l = acc[HD:HD + 1, :] + jnp.exp2(sk - m)
            outs.append((acc[:HD, :] / l).T)
            lse_all = jnp.where(row8 == h, m + jnp.log2(l), lse_all)
        for p in range(4):
            o_ref[:, 128 * p:128 * p + 128] = jnp.concatenate(outs[2 * p:2 * p + 2], axis=1).astype(BF16)
        lse_ref[...] = lse_all

    return pl.pallas_call(
        body, grid=(S // tq,),
        in_specs=[_rows(tq, BW), _const((S + 2 * WIN, 128)), _const((nc, 128, tq)), _const((1, NH))],
        out_specs=[_rows(tq, BW), pl.BlockSpec((NH, tq), lambda i: (0, i))],
        out_shape=[jax.ShapeDtypeStruct((S, BW), BF16), jax.ShapeDtypeStruct((NH, S), F32)],
        compiler_params=_params(1), name="attn_b_fwd",
    )(q, kp, vpt3, sink2)


def _merge_out(ya, yb, ga, gb, x, modv, wb, wout):
    S = x.shape[0]
    tm = min(256, S)

    def body(ya_ref, yb_ref, ga_ref, gb_ref, x_ref, mod_ref, wb_ref, wo_ref, x1_ref, mg_ref, ua_ref, ub_ref):
        ua = jnp.dot(ya_ref[...], wb_ref[0], preferred_element_type=F32)
        ub = jnp.dot(yb_ref[...], wb_ref[1], preferred_element_type=F32)
        merged = jax.nn.sigmoid(ga_ref[...].astype(F32)) * ua + jax.nn.sigmoid(gb_ref[...].astype(F32)) * ub
        mb = merged.astype(BF16)
        ua_ref[...] = ua.astype(BF16)
        ub_ref[...] = ub.astype(BF16)
        mg_ref[...] = mb
        x1_ref[...] = x_ref[...] + mod_ref[2:3, :] * jnp.dot(mb, wo_ref[...], preferred_element_type=F32)

    return pl.pallas_call(
        body, grid=(S // tm,),
        in_specs=[_rows(tm, BW), _rows(tm, BW), _rows(tm, D), _rows(tm, D), _rows(tm, D), _const((6, D)),
                  _const((2, BW, D)), _const((D, D))],
        out_specs=[_rows(tm, D)] * 4,
        out_shape=[jax.ShapeDtypeStruct((S, D), F32)] + [jax.ShapeDtypeStruct((S, D), BF16)] * 3,
        compiler_params=_params(1), name="merge_out",
    )(ya, yb, ga, gb, x, modv, wb, wout)


def _mlp_fwd(x1, modv, n2g, wmi, wmo, fg, target):
    S = x1.shape[0]
    tm = min(512, S)
    tf = wmi.shape[2]
    nj = wmi.shape[0] // MLP_SHARDS

    def body(x1_ref, mod_ref, g_ref, wi_ref, wo_ref, fg_ref, t_ref, h2_ref, hp_ref, dx2_ref, st_ref, acc_ref):
        i, j = pl.program_id(0), pl.program_id(1)

        @pl.when(j == 0)
        def _():
            xt = x1_ref[...]
            r = lax.rsqrt(jnp.mean(xt * xt, axis=-1, keepdims=True) + EPS)
            h2 = ((xt * r) * g_ref[...]) * (1.0 + mod_ref[4:5, :]) + mod_ref[3:4, :]
            h2_ref[...] = h2.astype(BF16)
            acc_ref[...] = jnp.zeros_like(acc_ref)

        @pl.when((i == 0) & (j == 0))
        def _():
            st_ref[...] = jnp.zeros_like(st_ref)

        out = None
        for u in range(MLP_SHARDS):
            hp = jnp.dot(h2_ref[...], wi_ref[u], preferred_element_type=F32)
            hp_ref[:, tf * u:tf * u + tf] = hp.astype(BF16)
            hid = jnp.square(jnp.maximum(hp, 0.0))
            part = jnp.dot(hid.astype(BF16), wo_ref[u], preferred_element_type=F32)
            out = part if out is None else out + part
        acc_ref[...] += out

        @pl.when(j == nj - 1)
        def _():
            x2 = x1_ref[...] + mod_ref[5:6, :] * acc_ref[...]
            r3 = lax.rsqrt(jnp.mean(x2 * x2, axis=-1, keepdims=True) + EPS)
            xn = x2 * r3
            err = xn * fg_ref[...] - t_ref[...]
            dy = err * (1.0 / D)
            gy = dy * fg_ref[...]
            dx2_ref[...] = r3 * (gy - xn * jnp.mean(gy * xn, axis=-1, keepdims=True))
            st_ref[0:1, :] += jnp.sum(dy * xn, axis=0, keepdims=True)
            st_ref[1:2, :] += jnp.sum(err * err, axis=0, keepdims=True) * (0.5 / D)

    return pl.pallas_call(
        body, grid=(S // tm, nj),
        in_specs=[pl.BlockSpec((tm, D), lambda i, j: (i, 0)), _const((6, D)), _const((1, D)),
                  pl.BlockSpec((MLP_SHARDS, D, tf), lambda i, j: (j, 0, 0)), pl.BlockSpec((MLP_SHARDS, tf, D), lambda i, j: (j, 0, 0)),
                  _const((1, D)), pl.BlockSpec((tm, D), lambda i, j: (i, 0))],
        out_specs=[pl.BlockSpec((tm, D), lambda i, j: (i, 0)), pl.BlockSpec((tm, MLP_SHARDS * tf), lambda i, j: (i, j)),
                   pl.BlockSpec((tm, D), lambda i, j: (i, 0)), _const((8, D))],
        out_shape=[jax.ShapeDtypeStruct((S, D), BF16), jax.ShapeDtypeStruct((S, wmi.shape[0] * tf), BF16),
                   jax.ShapeDtypeStruct((S, D), F32), jax.ShapeDtypeStruct((8, D), F32)],
        scratch_shapes=[pltpu.VMEM((tm, D), F32)],
        compiler_params=_params(2), name="mlp_fwd",
    )(x1, modv, n2g, wmi, wmo, fg, target)


def _mlp_bwd(dx2, x1, hp, modv, n2g, wmi, wmo):
    S = x1.shape[0]
    tm = min(512, S)
    tf = wmi.shape[2]
    nj = wmi.shape[0] // MLP_SHARDS

    def body(dx2_ref, x1_ref, hp_ref, mod_ref, g_ref, wi_ref, wo_ref, dhp_ref, dx1_ref, st_ref, dmo_ref, acc_ref):
        i, j = pl.program_id(0), pl.program_id(1)

        @pl.when(j == 0)
        def _():
            dmo_ref[...] = (mod_ref[5:6, :] * dx2_ref[...]).astype(BF16)
            acc_ref[...] = jnp.zeros_like(acc_ref)

        @pl.when((i == 0) & (j == 0))
        def _():
            st_ref[...] = jnp.zeros_like(st_ref)

        out = None
        for u in range(MLP_SHARDS):
            sl = slice(tf * u, tf * u + tf)
            dhid = lax.dot_general(dmo_ref[...], wo_ref[u], NT, preferred_element_type=F32)
            dhp = (dhid * (2.0 * jnp.maximum(hp_ref[:, sl].astype(F32), 0.0))).astype(BF16)
            dhp_ref[:, sl] = dhp
            part = lax.dot_general(dhp, wi_ref[u], NT, preferred_element_type=F32)
            out = part if out is None else out + part
        acc_ref[...] += out

        @pl.when(j == nj - 1)
        def _():
            dh2 = acc_ref[...]
            xt = x1_ref[...]
            r = lax.rsqrt(jnp.mean(xt * xt, axis=-1, keepdims=True) + EPS)
            xn = xt * r
            st_ref[0:1, :] += jnp.sum(dh2, axis=0, keepdims=True)
            st_ref[1:2, :] += jnp.sum(dh2 * xn, axis=0, keepdims=True)
            dxn = dh2 * (g_ref[...] * (1.0 + mod_ref[4:5, :]))
            dx1_ref[...] = dx2_ref[...] + r * (dxn - xn * jnp.mean(dxn * xn, axis=-1, keepdims=True))

    return pl.pallas_call(
        body, grid=(S // tm, nj),
        in_specs=[pl.BlockSpec((tm, D), lambda i, j: (i, 0)), pl.BlockSpec((tm, D), lambda i, j: (i, 0)),
                  pl.BlockSpec((tm, MLP_SHARDS * tf), lambda i, j: (i, j)), _const((6, D)), _const((1, D)),
                  pl.BlockSpec((MLP_SHARDS, D, tf), lambda i, j: (j, 0, 0)), pl.BlockSpec((MLP_SHARDS, tf, D), lambda i, j: (j, 0, 0))],
        out_specs=[pl.BlockSpec((tm, MLP_SHARDS * tf), lambda i, j: (i, j)), pl.BlockSpec((tm, D), lambda i, j: (i, 0)), _const((8, D))],
        out_shape=[jax.ShapeDtypeStruct((S, wmi.shape[0] * tf), BF16), jax.ShapeDtypeStruct((S, D), F32),
                   jax.ShapeDtypeStruct((8, D), F32)],
        scratch_shapes=[pltpu.VMEM((tm, D), BF16), pltpu.VMEM((tm, D), F32)],
        compiler_params=_params(2), name="mlp_bwd",
    )(dx2, x1, hp, modv, n2g, wmi, wmo)


def _tn_matmul(a, b, tk, tn, name, relu_sq=False, dev_major=False):
    S, K = a.shape
    N = b.shape[1]
    ts = min(1024, S)
    ns = S // ts

    def body(a_ref, b_ref, o_ref):
        @pl.when(pl.program_id(2) == 0)
        def _():
            o_ref[...] = jnp.zeros_like(o_ref)

        at = a_ref[...]
        if relu_sq:
            at = jnp.square(jnp.maximum(at.astype(F32), 0.0)).astype(BF16)
        o_ref[...] += lax.dot_general(at, b_ref[...].astype(BF16), TN, preferred_element_type=F32)

    if dev_major:
        out_spec = pl.BlockSpec((None, tk, tn), lambda k, n, s: (n, k, 0))
        out_shape = jax.ShapeDtypeStruct((N // tn, K, tn), F32)
    else:
        out_spec = pl.BlockSpec((tk, tn), lambda k, n, s: (k, n))
        out_shape = jax.ShapeDtypeStruct((K, N), F32)
    return pl.pallas_call(
        body, grid=(K // tk, N // tn, ns),
        in_specs=[pl.BlockSpec((ts, tk), lambda k, n, s: (s, k)), pl.BlockSpec((ts, tn), lambda k, n, s: (s, n))],
        out_specs=out_spec, out_shape=out_shape,
        compiler_params=_params(3), name=name,
    )(a, b)


def _scale_gate(m, w, g, row, name):
    K = m.shape[0]
    tk = min(512, K)

    def body(m_ref, w_ref, mod_ref, dw_ref, dg_ref):
        @pl.when(pl.program_id(0) == 0)
        def _():
            dg_ref[...] = jnp.zeros_like(dg_ref)

        mt = m_ref[...]
        dw_ref[...] = mt * mod_ref[row:row + 1, :]
        dg_ref[0:1, :] += jnp.sum(mt * w_ref[...].astype(F32), axis=0, keepdims=True)

    return pl.pallas_call(
        body, grid=(K // tk,),
        in_specs=[_rows(tk, D), _rows(tk, D), _const((6, D))],
        out_specs=[_rows(tk, D), _const((8, D))],
        out_shape=[jax.ShapeDtypeStruct((K, D), F32), jax.ShapeDtypeStruct((8, D), F32)],
        compiler_params=_params(1), name=name,
    )(m, w, g)


def _merge_bwd(dx1, modv, ga, gb, ua, ub, ya, yb, wb, wout):
    S = dx1.shape[0]
    tm = min(512, S)

    def body(dx1_ref, mod_ref, ga_ref, gb_ref, ua_ref, ub_ref, ya_ref, yb_ref, wb_ref, wo_ref,
             dua_ref, dub_ref, dga_ref, dgb_ref, dya_ref, dyb_ref, dla_ref, dlb_ref, dyat_ref):
        dao = (mod_ref[2:3, :] * dx1_ref[...]).astype(BF16)
        dm = lax.dot_general(dao, wo_ref[...], NT, preferred_element_type=F32)
        r = lax.broadcasted_iota(jnp.int32, (BW, NH), 0) // HD
        c = lax.broadcasted_iota(jnp.int32, (BW, NH), 1)
        head_of = (r == c).astype(BF16)
        for br, (g_ref, u_ref, y_ref, du_ref, dg_ref, dy_ref, dl_ref) in enumerate((
                (ga_ref, ua_ref, ya_ref, dua_ref, dga_ref, dya_ref, dla_ref),
                (gb_ref, ub_ref, yb_ref, dub_ref, dgb_ref, dyb_ref, dlb_ref))):
            sg = jax.nn.sigmoid(g_ref[...].astype(F32))
            du = (dm * sg).astype(BF16)
            du_ref[...] = du
            dg_ref[...] = (dm * u_ref[...].astype(F32) * sg * (1.0 - sg)).astype(BF16)
            dy = lax.dot_general(du, wb_ref[br], NT, preferred_element_type=F32)
            dyb16 = dy.astype(BF16)
            dy_ref[...] = dyb16
            if br == 0:
                dyat_ref[...] = dy.T.astype(BF16)
            prod = dyb16.astype(F32) * y_ref[...].astype(F32)
            hi = prod.astype(BF16)
            lo = (prod - hi.astype(F32)).astype(BF16)
            dl_ref[...] = (jnp.dot(hi, head_of, preferred_element_type=F32) + jnp.dot(lo, head_of, preferred_element_type=F32))

    return pl.pallas_call(
        body, grid=(S // tm,),
        in_specs=[_rows(tm, D), _const((6, D)), _rows(tm, D), _rows(tm, D), _rows(tm, D), _rows(tm, D),
                  _rows(tm, BW), _rows(tm, BW), _const((2, BW, D)), _const((D, D))],
        out_specs=[_rows(tm, D)] * 4 + [_rows(tm, BW)] * 2 + [_rows(tm, NH)] * 2 + [pl.BlockSpec((BW, tm), lambda i: (0, i))],
        out_shape=[jax.ShapeDtypeStruct((S, D), BF16)] * 4 + [jax.ShapeDtypeStruct((S, BW), BF16)] * 2
        + [jax.ShapeDtypeStruct((S, NH), F32)] * 2 + [jax.ShapeDtypeStruct((BW, S), BF16)],
        compiler_params=_params(1), name="merge_bwd",
    )(dx1, modv, ga, gb, ua, ub, ya, yb, wb, wout)


def _attn_a_bwd(q, qt, kt3, vt3, do, dot_, lse, delta, chunks):
    S = q.shape[0]
    tq = min(512, S)
    tk = min(512, S)
    nq, nk = S // tq, S // tk

    nc = len(chunks)

    def body(q_ref, qt_ref, do_ref, dot_ref, lse_ref, dl_ref, kt_ref, vt_ref, *rest):
        g_hbm, (dq_ref, dk_hbm, dv_hbm), recv_hbm = rest[:nc], rest[nc:nc + 3], rest[nc + 3:2 * nc + 3]
        dk_sc, dv_sc, sem, send_sems, recv_sems = rest[2 * nc + 3:]
        i = pl.program_id(0)
        start, finish = _exchange_scatter(g_hbm, recv_hbm, send_sems, recv_sems)

        @pl.when(i == 0)
        def _():
            start()
            dk_sc[...] = jnp.zeros_like(dk_sc)
            dv_sc[...] = jnp.zeros_like(dv_sc)

        for kv in range(2):
            qg = q_ref[:, 256 * kv:256 * kv + 256]
            dog = do_ref[:, 256 * kv:256 * kv + 256]
            heads = []
            for t in range(GRP):
                h = GRP * kv + t
                heads.append((qg[:, HD * t:HD * t + HD], dog[:, HD * t:HD * t + HD],
                              qt_ref[HD * h:HD * h + HD, :], dot_ref[HD * h:HD * h + HD, :],
                              lse_ref[:, h:h + 1], dl_ref[:, h:h + 1]))

            q2 = [jnp.concatenate([heads[2 * u][0], heads[2 * u + 1][0]], axis=0) for u in range(GRP // 2)]
            do2 = [jnp.concatenate([heads[2 * u][1], heads[2 * u + 1][1]], axis=0) for u in range(GRP // 2)]

            def step(j, carry, kv=kv, heads=heads, q2=q2, do2=do2):
                kjt = kt_ref[j, HD * kv:HD * kv + HD, :]
                vjt = vt_ref[j, HD * kv:HD * kv + HD, :]
                dkt = jnp.zeros((HD, tk), F32)
                dvt = jnp.zeros((HD, tk), F32)
                new = []

                def logits(u):
                    return (jnp.dot(q2[u], kjt, preferred_element_type=F32), jnp.dot(do2[u], vjt, preferred_element_type=F32))

                sd = [logits(0)]
                for t, (qh, doh, qth, doth, lse_h, dl_h) in enumerate(heads):
                    if t == 0:
                        sd.append(logits(1))
                    rows = slice(tq * (t % 2), tq * (t % 2) + tq)
                    s, dp = sd[t // 2][0][rows, :], sd[t // 2][1][rows, :]
                    pm = jnp.exp2(s - lse_h)
                    ds = (pm * (dp - dl_h)).astype(BF16)
                    dvt = dvt + jnp.dot(doth, pm.astype(BF16), preferred_element_type=F32)
                    dkt = dkt + jnp.dot(qth, ds, preferred_element_type=F32)
                    new.append(carry[t] + lax.dot_general(kjt, ds, NT, preferred_element_type=F32))
                dk_sc[j, HD * kv:HD * kv + HD, :] += dkt
                dv_sc[j, HD * kv:HD * kv + HD, :] += dvt
                return tuple(new)

            res = lax.fori_loop(0, nk, step, (jnp.zeros((HD, tq), F32),) * GRP)
            for t in range(GRP):
                dq_ref[HD * (GRP * kv + t):HD * (GRP * kv + t) + HD, :] = res[t]

        @pl.when(i == nq - 1)
        def _():
            c1 = pltpu.make_async_copy(dk_sc, dk_hbm, sem.at[0])
            c2 = pltpu.make_async_copy(dv_sc, dv_hbm, sem.at[1])
            c1.start()
            c2.start()
            c1.wait()
            c2.wait()
            finish()

    any_spec = pl.BlockSpec(memory_space=pl.ANY)
    cols = pl.BlockSpec((BW, tq), lambda i: (0, i))
    return pl.pallas_call(
        body, grid=(nq,),
        in_specs=[_rows(tq, BW), cols, _rows(tq, BW), cols, _rows(tq, NH), _rows(tq, NH), _const((nk, 128, tk)),
                  _const((nk, 128, tk))] + [any_spec] * nc,
        out_specs=[cols, any_spec, any_spec] + [any_spec] * nc,
        out_shape=[jax.ShapeDtypeStruct((BW, S), F32), jax.ShapeDtypeStruct((nk, 128, tk), F32),
                   jax.ShapeDtypeStruct((nk, 128, tk), F32)]
        + [jax.ShapeDtypeStruct((N_DEV - 1,) + c.shape[1:], c.dtype) for c in chunks],
        scratch_shapes=[pltpu.VMEM((nk, 128, tk), F32), pltpu.VMEM((nk, 128, tk), F32), pltpu.SemaphoreType.DMA((2,))]
        + _exchange_sems(nc),
        compiler_params=_params(1), name="attn_a_bwd",
    )(q, qt, do, dot_, lse, delta, kt3, vt3, *chunks)


def _attn_b_bwd(q, kp, vp, sink2, do, lse, delta):
    S = q.shape[0]
    tq = min(TQ_B, S)
    W = tq + 2 * WIN
    nq = S // tq
    nc = (S + 2 * WIN) // tq

    def body(q_ref, k_ref, v_ref, sink_ref, do_ref, lse_ref, dl_ref, dq_ref, dk_hbm, dv_hbm, ds_ref, dk_sc, dv_sc, sem):
        i = pl.program_id(0)

        @pl.when(i == 0)
        def _():
            dk_sc[...] = jnp.zeros_like(dk_sc)
            dv_sc[...] = jnp.zeros_like(dv_sc)
            ds_ref[...] = jnp.zeros_like(ds_ref)

        off = pl.multiple_of(i * tq, tq)
        valid = _window_mask(i, tq, S)
        kw = k_ref[pl.ds(off, W), :]
        vw = v_ref[pl.ds(off, W), :]
        lse_i = lse_ref[...]
        dl_i = dl_ref[...]
        qa = q_ref[...]
        doa = do_ref[...]
        qt = qa.astype(F32).T.astype(BF16)
        dot_ = doa.astype(F32).T.astype(BF16)
        khs = [kw[:, HD * kv:HD * kv + HD] for kv in range(2)]
        vhs = [vw[:, HD * kv:HD * kv + HD] for kv in range(2)]

        def logits(h):
            return (lax.dot_general(qa[:, HD * h:HD * h + HD], khs[h // GRP], NT, preferred_element_type=F32),
                    lax.dot_general(doa[:, HD * h:HD * h + HD], vhs[h // GRP], NT, preferred_element_type=F32))

        sd = [logits(0)]
        dqs = []
        dkt = [jnp.zeros((HD, W), F32), jnp.zeros((HD, W), F32)]
        dvt = [jnp.zeros((HD, W), F32), jnp.zeros((HD, W), F32)]
        for h in range(NH):
            kv = h // GRP
            if h + 1 < NH:
                sd.append(logits(h + 1))
            s, dp = sd[h]
            pm = jnp.exp2(jnp.where(valid, s, NEG) - lse_i[:, h:h + 1])
            ds = (pm * (dp - dl_i[:, h:h + 1])).astype(BF16)
            dvt[kv] = dvt[kv] + jnp.dot(dot_[HD * h:HD * h + HD, :], pm.astype(BF16), preferred_element_type=F32)
            dkt[kv] = dkt[kv] + jnp.dot(qt[HD * h:HD * h + HD, :], ds, preferred_element_type=F32)
            dqs.append(jnp.dot(ds, khs[kv], preferred_element_type=F32))
        for p in range(4):
            dq_ref[:, 128 * p:128 * p + 128] = jnp.concatenate(dqs[2 * p:2 * p + 2], axis=1)
        for half in range(W // tq):
            dk_sc[i + half] += jnp.concatenate([d[:, tq * half:tq * half + tq] for d in dkt], axis=0)
            dv_sc[i + half] += jnp.concatenate([d[:, tq * half:tq * half + tq] for d in dvt], axis=0)
        psd = jnp.exp2(sink_ref[...] - lse_i) * dl_i
        r = lax.broadcasted_iota(jnp.int32, (NH, 128), 0)
        c = lax.broadcasted_iota(jnp.int32, (NH, 128), 1)
        row = jnp.dot(jnp.sum(psd, axis=0, keepdims=True), (r == c).astype(F32),
                      preferred_element_type=F32, precision=lax.Precision.HIGHEST)
        ds_ref[...] -= jnp.broadcast_to(row, (8, 128))

        @pl.when(i == nq - 1)
        def _():
            c1 = pltpu.make_async_copy(dk_sc, dk_hbm, sem.at[0])
            c2 = pltpu.make_async_copy(dv_sc, dv_hbm, sem.at[1])
            c1.start()
            c2.start()
            c1.wait()
            c2.wait()

    any_spec = pl.BlockSpec(memory_space=pl.ANY)
    return pl.pallas_call(
        body, grid=(nq,),
        in_specs=[_rows(tq, BW), _const((S + 2 * WIN, 128)), _const((S + 2 * WIN, 128)), _const((1, NH)),
                  _rows(tq, BW), _rows(tq, NH), _rows(tq, NH)],
        out_specs=[_rows(tq, BW), any_spec, any_spec, _const((8, 128))],
        out_shape=[jax.ShapeDtypeStruct((S, BW), F32), jax.ShapeDtypeStruct((nc, 128, tq), F32),
                   jax.ShapeDtypeStruct((nc, 128, tq), F32), jax.ShapeDtypeStruct((8, 128), F32)],
        scratch_shapes=[pltpu.VMEM((nc, 128, tq), F32), pltpu.VMEM((nc, 128, tq), F32), pltpu.SemaphoreType.DMA((2,))],
        compiler_params=_params(1), name="attn_b_bwd",
    )(q, kp, vp, sink2, do, lse, delta)


def _qk_bwd(dqa_t, dka_t3, dva_t3, dqb, dkb, dvb, qar, kar, qg2, kg2, tab_a, tab_b, dga, dgb):
    S = dqb.shape[0]
    tm = min(256, S)
    per = dka_t3.shape[2] // tm

    def body(dqa_ref, dka_ref, dva_ref, dqb_ref, dkb_ref, dvb_ref, qar_ref, kar_ref, qg_ref, kg_ref, ta_ref, tb_ref,
             dga_ref, dgb_ref, dp_ref, st_ref):
        @pl.when(pl.program_id(0) == 0)
        def _():
            st_ref[...] = jnp.zeros_like(st_ref)

        seg = _seg_matrix(128, HD)

        def norm_bwd(dz_rot, raw, g):
            dzn = _rope_t(dz_rot, ta_ref, 16)
            raw = raw.astype(F32)
            rr = lax.rsqrt(_seg_sum(raw * raw, seg) * (1.0 / HD) + EPS)
            zhat = raw * rr
            dzh = dzn * g
            draw = rr * (dzh - zhat * (_seg_sum(dzh * zhat, seg) * (1.0 / HD)))
            return draw, jnp.sum(dzn * zhat, axis=0, keepdims=True)

        gq = jnp.zeros((1, 128), F32)
        for p in range(4):
            sl = slice(128 * p, 128 * p + 128)
            draw, gsum = norm_bwd(dqa_ref[sl, :].T * 0.125, qar_ref[:, sl], qg_ref[...])
            gq = gq + gsum
            dp_ref[:, sl] = draw.astype(BF16)
            dp_ref[:, 768 + 128 * p:768 + 128 * p + 128] = _rope_t(dqb_ref[:, sl] * 0.125, tb_ref, 32).astype(BF16)
        draw, gk = norm_bwd(dka_ref[...].T * LN2, kar_ref[...], kg_ref[...])
        dp_ref[:, 512:640] = draw.astype(BF16)
        dp_ref[:, 640:768] = dva_ref[...].T.astype(BF16)
        dp_ref[:, 1280:1408] = _rope_t(dkb_ref[...] * LN2, tb_ref, 32).astype(BF16)
        dp_ref[:, 1408:1536] = dvb_ref[...].astype(BF16)
        dp_ref[:, 1536:2560] = dga_ref[...]
        dp_ref[:, 2560:3584] = dgb_ref[...]
        st_ref[0:1, :] += gq
        st_ref[1:2, :] += gk

    tab = pl.BlockSpec((3, tm, 128), lambda i: (0, i, 0))
    chunk_t = pl.BlockSpec((None, 128, tm), lambda i: (i // per, 0, i % per))
    return pl.pallas_call(
        body, grid=(S // tm,),
        in_specs=[pl.BlockSpec((BW, tm), lambda i: (0, i)), chunk_t, chunk_t, _rows(tm, BW), _rows(tm, 128), _rows(tm, 128),
                  _rows(tm, BW), _rows(tm, 128), _const((1, 128)), _const((1, 128)), tab, tab, _rows(tm, D), _rows(tm, D)],
        out_specs=[_rows(tm, INW), _const((8, 128))],
        out_shape=[jax.ShapeDtypeStruct((S, INW), BF16), jax.ShapeDtypeStruct((8, 128), F32)],
        compiler_params=_params(1), name="qk_bwd",
    )(dqa_t, dka_t3, dva_t3, dqb, dkb, dvb, qar, kar, qg2, kg2, tab_a, tab_b, dga, dgb)


def _in_bwd(dproj, win, x, dx1, modv, n1g, chunks):
    S = x.shape[0]
    tm = min(512, S)
    n = S // tm
    ts = min(256, tm)

    def body(dp_ref, w_ref, x_ref, dx1_ref, mod_ref, g_ref, c_hbm, gx_ref, st_ref, recv_hbm, send_sems, recv_sems):
        start, finish = _exchange_scatter([c_hbm], [recv_hbm], send_sems, recv_sems)

        @pl.when(pl.program_id(0) == 0)
        def _():
            start()
            st_ref[...] = jnp.zeros_like(st_ref)

        subs = [slice(ts * u, ts * u + ts) for u in range(tm // ts)]
        dhs = [lax.dot_general(dp_ref[rows, :], w_ref[...], NT, preferred_element_type=F32) for rows in subs]
        for rows, dh in zip(subs, dhs):
            xt = x_ref[rows, :]
            r = lax.rsqrt(jnp.mean(xt * xt, axis=-1, keepdims=True) + EPS)
            xn = xt * r
            st_ref[0:1, :] += jnp.sum(dh, axis=0, keepdims=True)
            st_ref[1:2, :] += jnp.sum(dh * xn, axis=0, keepdims=True)
            dxn = dh * (g_ref[...] * (1.0 + mod_ref[1:2, :]))
            gx_ref[rows, :] = dx1_ref[rows, :] + r * (dxn - xn * jnp.mean(dxn * xn, axis=-1, keepdims=True))
        pl.when(pl.program_id(0) == n - 1)(finish)

    any_spec = pl.BlockSpec(memory_space=pl.ANY)
    return pl.pallas_call(
        body, grid=(n,),
        in_specs=[_rows(tm, INW), _const((D, INW)), _rows(tm, D), _rows(tm, D), _const((6, D)), _const((1, D)), any_spec],
        out_specs=[_rows(tm, D), _const((8, D)), any_spec],
        out_shape=[jax.ShapeDtypeStruct((S, D), F32), jax.ShapeDtypeStruct((8, D), F32),
                   jax.ShapeDtypeStruct((N_DEV - 1,) + chunks.shape[1:], chunks.dtype)],
        scratch_shapes=_exchange_sems(1),
        compiler_params=_params(1), name="in_bwd",
    )(dproj, win, x, dx1, modv, n1g, chunks)


def _pack_small(st1, st2, stf, dg1, dg2, stqk, dsink, modv, n1g, n2g):
    def body(st1_ref, st2_ref, stf_ref, dg1_ref, dg2_ref, qk_ref, ds_ref, mod_ref, g1_ref, g2_ref, o_ref):
        a1, b1 = st1_ref[0:1, :], st1_ref[1:2, :]
        a2, b2 = st2_ref[0:1, :], st2_ref[1:2, :]
        r = lax.broadcasted_iota(jnp.int32, (128, D), 0)
        c = lax.broadcasted_iota(jnp.int32, (128, D), 1)
        fold_q = (c == r % HD).astype(F32)
        fold_k = (c == HD + r % HD).astype(F32)
        keep = (c == r).astype(F32)

        def place(v, sel):
            return jnp.dot(v, sel, preferred_element_type=F32, precision=lax.Precision.HIGHEST)

        loss = jnp.sum(stf_ref[1:2, :], axis=1, keepdims=True)
        lane = lax.broadcasted_iota(jnp.int32, (1, D), 1)
        rows = [a1, g1_ref[...] * b1, dg1_ref[0:1, :], a2, g2_ref[...] * b2, dg2_ref[0:1, :],
                (1.0 + mod_ref[1:2, :]) * b1, (1.0 + mod_ref[4:5, :]) * b2, stf_ref[0:1, :],
                place(qk_ref[0:1, :], fold_q) + place(qk_ref[1:2, :], fold_k),
                place(ds_ref[0:1, :], keep),
                jnp.where(lane == 0, loss, 0.0)]
        rows += [jnp.zeros((1, D), F32)] * (SMALL_ROWS - len(rows))
        for n, v in enumerate(rows):
            o_ref[n:n + 1, :] = v

    return pl.pallas_call(
        body, out_shape=jax.ShapeDtypeStruct((SMALL_ROWS, D), F32),
        compiler_params=pltpu.CompilerParams(vmem_limit_bytes=V7X_VMEM_LIMIT), name="pack_small",
    )(st1, st2, stf, dg1, dg2, stqk, dsink, modv, n1g, n2g)


def _wada_grad(silu_all, dmod_cols):
    def body(a_ref, b_ref, o_ref):
        o_ref[...] = lax.dot_general(a_ref[...], b_ref[...], TN, preferred_element_type=F32, precision=lax.Precision.HIGHEST)

    return pl.pallas_call(
        body, out_shape=jax.ShapeDtypeStruct((D, dmod_cols.shape[1]), F32),
        compiler_params=pltpu.CompilerParams(vmem_limit_bytes=V7X_VMEM_LIMIT), name="wada_grad",
    )(silu_all, dmod_cols)


def _adamw_sum(parts, w, m, v, name):
    R, C = w.shape
    tr = R if R <= 64 else next(t for t in (256, 128, 64, 32, 16, 8) if R % t == 0)
    n = len(parts)
    dyn = [idx for _, idx in parts if idx is not None and not isinstance(idx, int)]
    b1c = 1.0 - ADAM_B1 ** ADAM_STEP
    b2c = 1.0 - ADAM_B2 ** ADAM_STEP

    def body(*refs):
        refs = refs[len(dyn):]
        g = refs[0][...].astype(F32)
        for k in range(1, n):
            g = g + refs[k][...].astype(F32)
        w_ref, m_ref, v_ref, g_out, d_out, m_out, v_out = refs[n:]
        mn = ADAM_B1 * m_ref[...] + (1.0 - ADAM_B1) * g
        vn = ADAM_B2 * v_ref[...] + (1.0 - ADAM_B2) * jnp.square(g)
        g_out[...] = g
        m_out[...] = mn
        v_out[...] = vn
        d_out[...] = -ADAM_LR * ((mn / b1c) / (jnp.sqrt(vn / b2c) + ADAM_EPS) + ADAM_WD * w_ref[...])

    in_specs = []
    nd = 0
    for a, idx in parts:
        if idx is None:
            in_specs.append(pl.BlockSpec((tr, C), lambda i, *s: (i, 0)))
        elif isinstance(idx, int):
            in_specs.append(pl.BlockSpec((None, tr, C), lambda i, *s, idx=idx: (idx, i, 0)))
        else:
            in_specs.append(pl.BlockSpec((None, tr, C), lambda i, *s, nd=nd: (s[nd][0], i, 0)))
            nd += 1
    blk = pl.BlockSpec((tr, C), lambda i, *s: (i, 0))
    grid_spec = pltpu.PrefetchScalarGridSpec(
        num_scalar_prefetch=len(dyn), grid=(R // tr,), in_specs=in_specs + [blk] * 3, out_specs=[blk] * 4)
    return pl.pallas_call(
        body, grid_spec=grid_spec, out_shape=[jax.ShapeDtypeStruct((R, C), F32)] * 4,
        compiler_params=_params(1), name=name,
    )(*dyn, *[a for a, _ in parts], w, m, v)


def _me():
    return lax.axis_index("x"), lax.axis_index("y"), lax.axis_index("c")


def _peer(k):
    x, y, c = _me()
    return (x ^ ((k >> 2) & 1), y ^ ((k >> 1) & 1), c ^ (k & 1))


def _ada_exchange(c_row, w_ada, b_rows):
    NW = w_ada.shape[1]

    def body(c_ref, w_ref, b_ref, sall_ref, mod_ref, src_ref, mp_ref, send1, recv1, send2, recv2):
        x, y, c = _me()
        me = 4 * x + 2 * y + c
        cv = c_ref[...]
        src_ref[...] = jnp.broadcast_to(cv * jax.nn.sigmoid(cv), (8, D))
        mine = pl.ds(pl.multiple_of(me * 8, 8), 8)
        sall_ref[mine, :] = src_ref[...]
        sends = [pltpu.make_async_remote_copy(src_ref, sall_ref.at[mine, :], send1.at[k - 1], recv1.at[k - 1],
                                              device_id=_peer(k), device_id_type=MESH) for k in range(1, N_DEV)]
        for cp in sends:
            cp.start()
        for k in range(1, N_DEV):
            theirs = pl.ds(pl.multiple_of((me ^ k) * 8, 8), 8)
            pltpu.make_async_remote_copy(src_ref, sall_ref.at[theirs, :], send1.at[k - 1], recv1.at[k - 1],
                                         device_id=_peer(k), device_id_type=MESH).wait_recv()
        for cp in sends:
            cp.wait_send()
        mp_ref[...] = jnp.dot(sall_ref[...], w_ref[...], preferred_element_type=F32, precision=lax.Precision.HIGHEST)
        mod_ref[mine, :] = mp_ref[mine, :] + b_ref[mine, :]
        sends = []
        for k in range(1, N_DEV):
            theirs = pl.ds(pl.multiple_of((me ^ k) * 8, 8), 8)
            sends.append(pltpu.make_async_remote_copy(mp_ref.at[theirs, :], mod_ref.at[mine, :], send2.at[k - 1], recv2.at[k - 1],
                                                      device_id=_peer(k), device_id_type=MESH))
        for cp in sends:
            cp.start()
        for k in range(1, N_DEV):
            theirs = pl.ds(pl.multiple_of((me ^ k) * 8, 8), 8)
            pltpu.make_async_remote_copy(mp_ref.at[mine, :], mod_ref.at[theirs, :], send2.at[k - 1], recv2.at[k - 1],
                                         device_id=_peer(k), device_id_type=MESH).wait_recv()
            mod_ref[theirs, :] = mod_ref[theirs, :] + b_ref[theirs, :]
        for cp in sends:
            cp.wait_send()

    vm = pl.BlockSpec(memory_space=pltpu.VMEM)
    return pl.pallas_call(
        body, in_specs=[vm, vm, vm], out_specs=[vm, vm],
        out_shape=[jax.ShapeDtypeStruct((8 * N_DEV, D), F32), jax.ShapeDtypeStruct((8 * N_DEV, NW), F32)],
        scratch_shapes=[pltpu.VMEM((8, D), F32), pltpu.VMEM((8 * N_DEV, NW), F32)]
        + [pltpu.SemaphoreType.DMA((N_DEV - 1,))] * 4,
        compiler_params=pltpu.CompilerParams(vmem_limit_bytes=V7X_VMEM_LIMIT), name="ada_exchange",
    )(c_row, w_ada, b_rows)


def _weight_gather(shard):
    def body(x_ref, out_ref, send_sems, recv_sems, local_sem):
        x, y, c = _me()
        me, sibling = (x, y, c), (x, y, 1 - c)
        chips = [(1 - x, y), (x, 1 - y), (1 - x, 1 - y)]

        def slot(px, py, pc):
            return out_ref.at[4 * px + 2 * py + pc]

        def copy(k, block, to, src=None):
            return pltpu.make_async_remote_copy(
                src_ref=slot(*block) if src is None else src, dst_ref=slot(*block),
                send_sem=send_sems.at[k], recv_sem=recv_sems.at[k], device_id=to, device_id_type=MESH)

        mine = pltpu.make_async_copy(x_ref, slot(*me), local_sem)
        mine.start()
        first = [copy(0, me, sibling, src=x_ref)]
        first += [copy(1 + j, me, (*chip, c), src=x_ref) for j, chip in enumerate(chips)]
        for cp in first:
            cp.start()
        passed = [copy(4 + j, (*chip, c), sibling) for j, chip in enumerate(chips)]
        for j, chip in enumerate(chips):
            copy(1 + j, (*chip, c), me).wait_recv()
            passed[j].start()
        copy(0, sibling, me).wait_recv()
        for j, chip in enumerate(chips):
            copy(4 + j, (*chip, 1 - c), me).wait_recv()
        for cp in first + passed:
            cp.wait_send()
        mine.wait()

    any_spec = pl.BlockSpec(memory_space=pl.ANY)
    return pl.pallas_call(
        body, in_specs=[any_spec], out_specs=any_spec,
        out_shape=jax.ShapeDtypeStruct((N_DEV,) + shard.shape, shard.dtype),
        scratch_shapes=[pltpu.SemaphoreType.DMA((7,)), pltpu.SemaphoreType.DMA((7,)), pltpu.SemaphoreType.DMA],
        name="weight_gather",
    )(shard)


def _small_gather(block):
    def body(b_ref, out_ref, send_sems, recv_sems):
        x, y, c = _me()
        me = 4 * x + 2 * y + c
        out_ref[me] = b_ref[...]
        sends = [pltpu.make_async_remote_copy(b_ref, out_ref.at[me], send_sems.at[k - 1], recv_sems.at[k - 1],
                                              device_id=_peer(k), device_id_type=MESH) for k in range(1, N_DEV)]
        for cp in sends:
            cp.start()
        for k in range(1, N_DEV):
            pltpu.make_async_remote_copy(b_ref, out_ref.at[me ^ k], send_sems.at[k - 1], recv_sems.at[k - 1],
                                         device_id=_peer(k), device_id_type=MESH).wait_recv()
        for cp in sends:
            cp.wait_send()

    vm = pl.BlockSpec(memory_space=pltpu.VMEM)
    return pl.pallas_call(
        body, in_specs=[vm], out_specs=vm,
        out_shape=jax.ShapeDtypeStruct((N_DEV,) + block.shape, block.dtype),
        scratch_shapes=[pltpu.SemaphoreType.DMA((N_DEV - 1,)), pltpu.SemaphoreType.DMA((N_DEV - 1,))],
        name="small_gather",
    )(block)


def _pack_small_params(b_ada, n1, n2, fg, qn, kn, sink):
    z = jnp.zeros((SMALL_ROWS, D), F32)
    z = z.at[0:6].set(b_ada.reshape(6, D)).at[6].set(n1.reshape(D)).at[7].set(n2.reshape(D)).at[8].set(fg.reshape(D))
    z = z.at[9, 0:HD].set(qn.reshape(HD)).at[9, HD:2 * HD].set(kn.reshape(HD)).at[10, 0:NH].set(sink.reshape(NH))
    return z


def _unpack_small(p):
    return (p[0:6].reshape(1, 6 * D), p[6].reshape(1, D), p[9, 0:HD].reshape(1, HD), p[9, HD:2 * HD].reshape(1, HD),
            p[10, 0:NH].reshape(1, NH), p[7].reshape(1, D), p[8].reshape(D))


def kernel(x, c, w_ada, b_ada, norm1_g, w_in, q_norm_a, k_norm_a, sink_b, w_branch, w_out, norm2_g, w_mlp_in, w_mlp_out, final_g, loss_target, m_w_ada, m_b_ada, m_norm1_g, m_w_in, m_q_norm_a, m_k_norm_a, m_sink_b, m_w_branch, m_w_out, m_norm2_g, m_w_mlp_in, m_w_mlp_out, m_final_g, v_w_ada, v_b_ada, v_norm1_g, v_w_in, v_q_norm_a, v_k_norm_a, v_sink_b, v_w_branch, v_w_out, v_norm2_g, v_w_mlp_in, v_w_mlp_out, v_final_g):
    S = x.shape[1]
    xs = x.reshape(S, D)
    tgt = loss_target.reshape(S, D)
    ax, ay, ac = lax.axis_index("x"), lax.axis_index("y"), lax.axis_index("c")
    me = 4 * ax + 2 * ay + ac
    me1 = me.reshape(1).astype(jnp.int32)
    NW = w_ada.shape[2]
    NI = w_in.shape[2]

    silu64, mod64 = _ada_exchange(c.reshape(1, D), w_ada.reshape(D, NW),
                                  jnp.repeat(b_ada.reshape(N_DEV, NW), 8, axis=0))
    silu_all = silu64[0::8]
    modv = mod64[0::8].reshape(6, D)

    win = _weight_gather(w_in[0].astype(BF16)).transpose(1, 0, 2).reshape(D, INW)
    rest_shards = tuple(w[0].astype(BF16) for w in (w_branch, w_out, w_mlp_in, w_mlp_out))

    tab_a, tab_b = _rope_tables(S)
    qg2 = jnp.tile(q_norm_a.reshape(1, HD), (1, 2))
    kg2 = jnp.tile(k_norm_a.reshape(1, HD), (1, 2))
    n1g = norm1_g.reshape(1, D)
    n2g = norm2_g.reshape(1, D)
    fg = final_g.reshape(1, D)
    sink2 = sink_b.reshape(1, NH) * LOG2E

    h, qar, kar, qa, ka, va, qb, kb, vb, ga, gb, qa_t, ka_t3, va_t3 = _in_proj(xs, modv, n1g, win, qg2, kg2, tab_a, tab_b)
    ya, lse_at, wb, wout, wmi, wmo = _attn_a_fwd(qa_t, ka, va_t3, rest_shards)
    lse_a = lse_at.T
    wb = wb.transpose(1, 2, 0, 3).reshape(2, BW, D)
    wout = wout.reshape(D, D)
    pad = ((WIN, WIN), (0, 0))
    kbp, vbp = jnp.pad(kb, pad), jnp.pad(vb, pad)
    tb = min(TQ_B, S)
    yb, lse_bt = _attn_b_fwd(qb, kbp, vbp.reshape((S + 2 * WIN) // tb, tb, 128).transpose(0, 2, 1), sink2)
    lse_b = lse_bt.T
    x1, merged, ua, ub = _merge_out(ya, yb, ga, gb, xs, modv, wb, wout)
    h2, hp, dx2, stf = _mlp_fwd(x1, modv, n2g, wmi, wmo, fg, tgt)

    dhp, dx1, st2 = _mlp_bwd(dx2, x1, hp, modv, n2g, wmi, wmo)
    m2 = _tn_matmul(hp, dx2, 2048, D, "dw_mlp_out", relu_sq=True)
    g_wmo, dg2 = _scale_gate(m2, wmo.reshape(FF, D), modv, 5, "gate2_grad")
    g_wmi = _tn_matmul(h2, dhp, D, 512, "dw_mlp_in", dev_major=True)
    dua, dub, dga, dgb, dya, dyb, dl_a, dl_b, dya_t = _merge_bwd(dx1, modv, ga, gb, ua, ub, ya, yb, wb, wout)
    m1 = _tn_matmul(merged, dx1, D, D, "dw_out")
    g_wout, dg1 = _scale_gate(m1, wout, modv, 2, "gate1_grad")
    g_wb0 = _tn_matmul(ya, dua, BW, D, "dw_branch_a")
    g_wb1 = _tn_matmul(yb, dub, BW, D, "dw_branch_b")
    g_wb = jnp.stack([g_wb0, g_wb1]).reshape(2, BW, N_DEV, 128).transpose(2, 0, 1, 3).reshape(N_DEV, 2 * BW, 128)
    g_wout = g_wout.reshape(N_DEV, 128, D)
    g_wmo = g_wmo.reshape(N_DEV, 512, D)
    dqa_t, dka_t3, dva_t3, r_wb, r_wout, r_wmi, r_wmo = _attn_a_bwd(qa, qa_t, ka_t3, va_t3, dya, dya_t, lse_a, dl_a,
                                                                    (g_wb, g_wout, g_wmi, g_wmo))
    dqb, dkb_t, dvb_t, dsink = _attn_b_bwd(qb, kbp, vbp, sink2, dyb, lse_b, dl_b)
    dkb = dkb_t.transpose(0, 2, 1).reshape(S + 2 * WIN, 128)[WIN:WIN + S]
    dvb = dvb_t.transpose(0, 2, 1).reshape(S + 2 * WIN, 128)[WIN:WIN + S]
    dproj, stqk = _qk_bwd(dqa_t, dka_t3, dva_t3, dqb, dkb, dvb, qar, kar, qg2, kg2, tab_a, tab_b, dga, dgb)
    g_win = _tn_matmul(h, dproj, D, 896, "dw_in")
    g_win = g_win.reshape(D, N_DEV, NI).transpose(1, 0, 2)
    grad_x, st1, r_win = _in_bwd(dproj, win, xs, dx1, modv, n1g, g_win.astype(BF16))

    def adam(name, own, recv, w, m, v):
        shape = w.shape
        w2, m2_, v2 = (a.reshape(own.shape[1:]) for a in (w, m, v))
        outs = _adamw_sum([(own, me1)] + [(recv, k) for k in range(N_DEV - 1)], w2, m2_, v2, name)
        return [a.reshape(shape) for a in outs]

    o_win = adam("adamw_w_in", g_win, r_win, w_in, m_w_in, v_w_in)
    o_wb = adam("adamw_w_branch", g_wb, r_wb, w_branch, m_w_branch, v_w_branch)
    o_wout = adam("adamw_w_out", g_wout, r_wout, w_out, m_w_out, v_w_out)
    o_wmi = adam("adamw_w_mlp_in", g_wmi, r_wmi, w_mlp_in, m_w_mlp_in, v_w_mlp_in)
    o_wmo = adam("adamw_w_mlp_out", g_wmo, r_wmo, w_mlp_out, m_w_mlp_out, v_w_mlp_out)

    small = _pack_small(st1, st2, stf, dg1, dg2, stqk, dsink, modv, n1g, n2g)
    small_all = _small_gather(small)
    sw = _pack_small_params(b_ada, norm1_g, norm2_g, final_g, q_norm_a, k_norm_a, sink_b)
    sm = _pack_small_params(m_b_ada, m_norm1_g, m_norm2_g, m_final_g, m_q_norm_a, m_k_norm_a, m_sink_b)
    sv = _pack_small_params(v_b_ada, v_norm1_g, v_norm2_g, v_final_g, v_q_norm_a, v_k_norm_a, v_sink_b)
    sm_out = _adamw_sum([(small_all, k) for k in range(N_DEV)], sw, sm, sv, "adamw_small")
    loss = sm_out[0][11, 0]
    sm_out = [_unpack_small(a) for a in sm_out]

    dmod_all = small_all[:, 0:6, :].reshape(N_DEV, 6 * D)
    dmod_cols = lax.dynamic_slice_in_dim(dmod_all, me * NW, NW, axis=1)
    g_wada = _wada_grad(silu_all, dmod_cols)
    ada = _adamw_sum([(g_wada, None)], w_ada.reshape(D, NW), m_w_ada.reshape(D, NW), v_w_ada.reshape(D, NW), "adamw_ada")
    ada = [a.reshape(1, D, NW) for a in ada]

    def leaves(k):
        b_, n1_, qn_, kn_, sk_, n2_, fg_ = sm_out[k]
        return [ada[k], b_, n1_, o_win[k], qn_, kn_, sk_, o_wb[k], o_wout[k], n2_, o_wmi[k], o_wmo[k], fg_]

    return (loss, grad_x.reshape(1, S, D), *leaves(0), *leaves(1), *leaves(2), *leaves(3))
```

```python
import jax
import jax.numpy as jnp
from jax import lax
from jax.experimental import pallas as pl
from jax.experimental.pallas import tpu as pltpu

F32, BF16 = jnp.float32, jnp.bfloat16
MESH = pl.DeviceIdType.MESH

D = 1024
HD = 64
NH = 8
GRP = 4
BW = 512
FF = 4096
INW = 3584
GRID_W = 64
WIN = 128
THETA = 10000.0
EPS = 1e-6
NEG = -1e30
N_DEV = 8
LOG2E = 1.4426950408889634
LN2 = 0.6931471805599453
QA_SCALE = 0.125 * LOG2E
SMALL_ROWS = 16
MLP_SHARDS = 4
TK_A = 512
TK_A_FWD = 1024
V7X_VMEM_LIMIT = 56 * 1024 * 1024

ADAM_LR, ADAM_B1, ADAM_B2, ADAM_EPS, ADAM_WD, ADAM_STEP = 0.001, 0.9, 0.999, 1e-08, 0.01, 10

NT = (((1,), (1,)), ((), ()))
TN = (((0,), (0,)), ((), ()))


def _params(n_axes, vmem=V7X_VMEM_LIMIT):
    return pltpu.CompilerParams(dimension_semantics=("arbitrary",) * n_axes, vmem_limit_bytes=vmem)


def _const(shape):
    return pl.BlockSpec(shape, lambda *_: (0,) * len(shape))


def _rows(tm, width):
    return pl.BlockSpec((tm, width), lambda i, *_: (i, 0))


def _seg_matrix(n, seg):
    r = lax.broadcasted_iota(jnp.int32, (n, n), 0) // seg
    c = lax.broadcasted_iota(jnp.int32, (n, n), 1) // seg
    return (r == c).astype(BF16)


def _seg_sum(z, seg_mat):
    hi = z.astype(BF16)
    lo = (z - hi.astype(F32)).astype(BF16)
    return jnp.dot(hi, seg_mat, preferred_element_type=F32) + jnp.dot(lo, seg_mat, preferred_element_type=F32)


def _rope(z, t_ref, sh):
    return z * t_ref[0] + pltpu.roll(z, sh, 1) * t_ref[1] + pltpu.roll(z, 128 - sh, 1) * t_ref[2]


def _rope_t(dz, t_ref, sh):
    return dz * t_ref[0] + pltpu.roll(dz * t_ref[1], 128 - sh, 1) + pltpu.roll(dz * t_ref[2], sh, 1)


def _rope_tables(S):
    t = jnp.arange(S, dtype=jnp.int32)
    lane = jnp.arange(HD)

    def build(cos, sin, first):
        t0 = cos
        t1 = jnp.where(first[None, :], 0.0, sin)
        t2 = jnp.where(first[None, :], -sin, 0.0)
        return jnp.stack([jnp.tile(a, (1, 2)) for a in (t0, t1, t2)]).astype(F32)

    inv_a = THETA ** (-jnp.arange(0, HD // 2, 2, dtype=F32) / (HD // 2))
    ar = (t // GRID_W).astype(F32)[:, None] * inv_a[None, :]
    ac = (t % GRID_W).astype(F32)[:, None] * inv_a[None, :]
    cos_a = jnp.concatenate([jnp.cos(ar), jnp.cos(ar), jnp.cos(ac), jnp.cos(ac)], axis=1)
    sin_a = jnp.concatenate([jnp.sin(ar), jnp.sin(ar), jnp.sin(ac), jnp.sin(ac)], axis=1)
    tab_a = build(cos_a, sin_a, (lane % 32) < 16)
    inv_b = THETA ** (-jnp.arange(0, HD, 2, dtype=F32) / HD)
    ab = t.astype(F32)[:, None] * inv_b[None, :]
    cos_b = jnp.concatenate([jnp.cos(ab), jnp.cos(ab)], axis=1)
    sin_b = jnp.concatenate([jnp.sin(ab), jnp.sin(ab)], axis=1)
    tab_b = build(cos_b, sin_b, lane < 32)
    return tab_a, tab_b


def _in_proj(x, modv, n1g, win, qg2, kg2, tab_a, tab_b):
    S = x.shape[0]
    tm = min(512, S)
    tk = min(TK_A, S)
    per = tk // tm
    ts = min(256, tm)

    def body(x_ref, mod_ref, g_ref, w_ref, qg_ref, kg_ref, ta_ref, tb_ref,
             h_ref, qar_ref, kar_ref, qa_ref, ka_ref, va_ref, qb_ref, kb_ref, vb_ref, ga_ref, gb_ref, qat_ref, kat_ref, vat_ref):
        seg = _seg_matrix(128, HD)

        def head_norm(z, g):
            ms = _seg_sum(z * z, seg) * (1.0 / HD)
            return (z * lax.rsqrt(ms + EPS)) * g

        subs = [slice(ts * u, ts * u + ts) for u in range(tm // ts)]
        hbs = []
        for rows in subs:
            xt = x_ref[rows, :]
            r = lax.rsqrt(jnp.mean(xt * xt, axis=-1, keepdims=True) + EPS)
            h = ((xt * r) * g_ref[...]) * (1.0 + mod_ref[1:2, :]) + mod_ref[0:1, :]
            hbs.append(h.astype(BF16))
            h_ref[rows, :] = hbs[-1]
        projs = [jnp.dot(hb, w_ref[...], preferred_element_type=F32) for hb in hbs]
        for rows, proj in zip(subs, projs):
            ta, tb = ta_ref[:, rows, :], tb_ref[:, rows, :]
            for p in range(4):
                z = proj[:, 128 * p:128 * p + 128]
                qar_ref[rows, 128 * p:128 * p + 128] = z.astype(BF16)
                qv = _rope(head_norm(z, qg_ref[...]), ta, 16) * QA_SCALE
                qa_ref[rows, 128 * p:128 * p + 128] = qv.astype(BF16)
                qat_ref[128 * p:128 * p + 128, rows] = qv.T.astype(BF16)
                zb = proj[:, 768 + 128 * p:768 + 128 * p + 128]
                qb_ref[rows, 128 * p:128 * p + 128] = (_rope(zb, tb, 32) * QA_SCALE).astype(BF16)
            z = proj[:, 512:640]
            kar_ref[rows, :] = z.astype(BF16)
            kv_ = _rope(head_norm(z, kg_ref[...]), ta, 16)
            ka_ref[rows, :] = kv_.astype(BF16)
            kat_ref[:, rows] = kv_.T.astype(BF16)
            va_ref[rows, :] = proj[:, 640:768].astype(BF16)
            vat_ref[:, rows] = proj[:, 640:768].T.astype(BF16)
            kb_ref[rows, :] = _rope(proj[:, 1280:1408], tb, 32).astype(BF16)
            vb_ref[rows, :] = proj[:, 1408:1536].astype(BF16)
            ga_ref[rows, :] = proj[:, 1536:2560].astype(BF16)
            gb_ref[rows, :] = proj[:, 2560:3584].astype(BF16)

    tab = pl.BlockSpec((3, tm, 128), lambda i: (0, i, 0))
    shapes = [(D, BF16), (BW, BF16), (128, BF16), (BW, BF16), (128, BF16), (128, BF16),
              (BW, BF16), (128, BF16), (128, BF16), (D, BF16), (D, BF16)]
    return pl.pallas_call(
        body, grid=(S // tm,),
        in_specs=[_rows(tm, D), _const((6, D)), _const((1, D)), _const((D, INW)), _const((1, 128)), _const((1, 128)), tab, tab],
        out_specs=[_rows(tm, w) for w, _ in shapes] + [pl.BlockSpec((BW, tm), lambda i: (0, i))]
        + [pl.BlockSpec((None, 128, tm), lambda i: (i // per, 0, i % per))] * 2,
        out_shape=[jax.ShapeDtypeStruct((S, w), dt) for w, dt in shapes] + [jax.ShapeDtypeStruct((BW, S), BF16)]
        + [jax.ShapeDtypeStruct((S // tk, 128, tk), BF16)] * 2,
        compiler_params=_params(1), name="in_proj",
    )(x, modv, n1g, win, qg2, kg2, tab_a, tab_b)


def _exchange_gather(block_refs, out_refs, send_sems, recv_sems, local_sems):
    x, y, c = _me()
    me = 4 * x + 2 * y + c

    def copies():
        own, out, arrive = [], [], []
        for a, (blk, dst) in enumerate(zip(block_refs, out_refs)):
            own.append(pltpu.make_async_copy(blk, dst.at[me], local_sems.at[a]))
            for k in range(1, N_DEV):
                sems = dict(send_sem=send_sems.at[a, k - 1], recv_sem=recv_sems.at[a, k - 1], device_id=_peer(k), device_id_type=MESH)
                out.append(pltpu.make_async_remote_copy(blk, dst.at[me], **sems))
                arrive.append(pltpu.make_async_remote_copy(blk, dst.at[me ^ k], **sems))
        return own, out, arrive

    def start():
        own, out, _ = copies()
        for cp in own + out:
            cp.start()

    def finish():
        own, out, arrive = copies()
        for cp in arrive:
            cp.wait_recv()
        for cp in out:
            cp.wait_send()
        for cp in own:
            cp.wait()

    return start, finish


def _exchange_scatter(chunk_refs, recv_refs, send_sems, recv_sems):
    x, y, c = _me()
    me = 4 * x + 2 * y + c

    def copies():
        return [pltpu.make_async_remote_copy(src.at[me ^ k], dst.at[k - 1], send_sems.at[a, k - 1], recv_sems.at[a, k - 1],
                                             device_id=_peer(k), device_id_type=MESH)
                for a, (src, dst) in enumerate(zip(chunk_refs, recv_refs)) for k in range(1, N_DEV)]

    def start():
        for cp in copies():
            cp.start()

    def finish():
        cps = copies()
        for cp in cps:
            cp.wait_recv()
        for cp in cps:
            cp.wait_send()

    return start, finish


def _exchange_sems(n):
    return [pltpu.SemaphoreType.DMA((n, N_DEV - 1)), pltpu.SemaphoreType.DMA((n, N_DEV - 1))]


def _attn_a_fwd(qt, k, vt3, shards):
    S = qt.shape[1]
    tq = min(512, S)
    nq = S // tq
    per = max(1, min(TK_A_FWD, S) // vt3.shape[2])
    tk = per * vt3.shape[2]
    nk = S // tk
    ONES = 16
    AHEAD = 2
    ns = len(shards)

    def body(q_ref, k_ref, vt_ref, *rest):
        w_hbm, (o_ref, lse_ref), wall_hbm = rest[:ns], rest[ns:ns + 2], rest[ns + 2:2 * ns + 2]
        st_sc, send_sems, recv_sems, local_sems = rest[2 * ns + 2:]
        start, finish = _exchange_gather(w_hbm, wall_hbm, send_sems, recv_sems, local_sems)
        pl.when(pl.program_id(0) == 0)(start)
        row8 = lax.broadcasted_iota(jnp.int32, (NH, tq), 0)
        lse_all = jnp.zeros((NH, tq), F32)
        ones = jnp.ones((ONES, tk), BF16)
        for kv in range(2):
            qts = [q_ref[HD * (GRP * kv + t):HD * (GRP * kv + t) + HD, :] for t in range(GRP)]

            def keys(j, kv=kv):
                return k_ref[pl.ds(pl.multiple_of(j * tk, tk), tk), :][:, HD * kv:HD * kv + HD]

            def scores(kj, t, qts=qts):
                return jnp.dot(kj, qts[t], preferred_element_type=F32)

            def step(j, carry, kv=kv):
                kj = keys(j)
                kn = keys(jnp.minimum(j + 1, nk - 1))
                vt = jnp.concatenate([vt_ref[per * j + u, HD * kv:HD * kv + HD, :] for u in range(per)], axis=1)
                v1 = jnp.concatenate([vt, ones], axis=0)
                sts = [st_sc[t] for t in range(AHEAD)]
                new = []
                for t in range(GRP):
                    m, acc = carry[2 * t], carry[2 * t + 1]
                    if t + AHEAD < GRP:
                        sts.append(scores(kj, t + AHEAD))
                    st = sts[t]
                    mn = jnp.maximum(m, jnp.max(st, axis=0, keepdims=True))
                    pt = jnp.exp2(st - mn)
                    if t + AHEAD >= GRP:
                        st_sc[t + AHEAD - GRP] = scores(kn, t + AHEAD - GRP)
                    acc = jnp.exp2(m - mn) * acc + jnp.dot(v1, pt.astype(BF16), preferred_element_type=F32)
                    new += [mn, acc]
                return tuple(new)

            k0 = keys(0)
            for t in range(AHEAD):
                st_sc[t] = scores(k0, t)
            init = (jnp.full((1, tq), NEG, F32), jnp.zeros((HD + ONES, tq), F32)) * GRP
            res = lax.fori_loop(0, nk, step, init)
            outs = []
            for t in range(GRP):
                m, acc = res[2 * t], res[2 * t + 1]
                l = acc[HD:HD + 1, :]
                outs.append((acc[:HD, :] / l).T)
                lse_all = jnp.where(row8 == GRP * kv + t, m + jnp.log2(l), lse_all)
            o_ref[:, 256 * kv:256 * kv + 256] = jnp.concatenate(outs, axis=1).astype(BF16)
        lse_ref[...] = lse_all
        pl.when(pl.program_id(0) == nq - 1)(finish)

    any_spec = pl.BlockSpec(memory_space=pl.ANY)
    return pl.pallas_call(
        body, grid=(nq,),
        in_specs=[pl.BlockSpec((BW, tq), lambda i: (0, i)), _const((S, 128)), _const(vt3.shape)] + [any_spec] * ns,
        out_specs=[_rows(tq, BW), pl.BlockSpec((NH, tq), lambda i: (0, i))] + [any_spec] * ns,
        out_shape=[jax.ShapeDtypeStruct((S, BW), BF16), jax.ShapeDtypeStruct((NH, S), F32)]
        + [jax.ShapeDtypeStruct((N_DEV,) + s.shape, s.dtype) for s in shards],
        scratch_shapes=[pltpu.VMEM((AHEAD, tk, tq), F32)] + _exchange_sems(ns) + [pltpu.SemaphoreType.DMA((ns,))],
        compiler_params=_params(1), name="attn_a_fwd",
    )(qt, k, vt3, *shards)


def _window_mask(i, tq, S):
    W = tq + 2 * WIN
    r = lax.broadcasted_iota(jnp.int32, (tq, W), 0)
    c = lax.broadcasted_iota(jnp.int32, (tq, W), 1)
    kpos = i * tq - WIN + c
    return (jnp.abs(c - WIN - r) <= WIN) & (kpos >= 0) & (kpos < S)


TQ_B = 256


def _attn_b_fwd(q, kp, vpt3, sink2):
    S = q.shape[0]
    tq = min(TQ_B, S)
    W = tq + 2 * WIN
    nc = vpt3.shape[0]
    ONES = 16

    def body(q_ref, k_ref, vt_ref, sink_ref, o_ref, lse_ref):
        i = pl.program_id(0)
        off = pl.multiple_of(i * tq, tq)
        r = lax.broadcasted_iota(jnp.int32, (W, tq), 1)
        c = lax.broadcasted_iota(jnp.int32, (W, tq), 0)
        kpos = i * tq - WIN + c
        valid = (jnp.abs(c - WIN - r) <= WIN) & (kpos >= 0) & (kpos < S)
        kw = k_ref[pl.ds(off, W), :]
        vt = jnp.concatenate([vt_ref[i + half] for half in range(W // tq)], axis=1)
        ones = jnp.ones((ONES, W), BF16)
        row8 = lax.broadcasted_iota(jnp.int32, (NH, tq), 0)
        lse_all = jnp.zeros((NH, tq), F32)
        qs = []
        for p in range(4):
            qp = q_ref[:, 128 * p:128 * p + 128]
            qs += [qp[:, :HD], qp[:, HD:]]
        khs = [kw[:, HD * kv:HD * kv + HD] for kv in range(2)]
        v1s = [jnp.concatenate([vt[HD * kv:HD * kv + HD, :], ones], axis=0) for kv in range(2)]

        def scores(h):
            return lax.dot_general(khs[h // GRP], qs[h], NT, preferred_element_type=F32)

        ss = [scores(0), scores(1)]
        outs = []
        for h in range(NH):
            if h + 2 < NH:
                ss.append(scores(h + 2))
            st = jnp.where(valid, ss[h], NEG)
            sk = sink_ref[:, h:h + 1]
            m = jnp.maximum(jnp.max(st, axis=0, keepdims=True), sk)
            acc = jnp.dot(v1s[h // GRP], jnp.exp2(st - m).astype(BF16), preferred_element_type=F32)
            l = acc[HD:HD + 1, :] + jnp.exp2(sk - m)
            outs.append((acc[:HD, :] / l).T)
            lse_all = jnp.where(row8 == h, m + jnp.log2(l), lse_all)
        for p in range(4):
            o_ref[:, 128 * p:128 * p + 128] = jnp.concatenate(outs[2 * p:2 * p + 2], axis=1).astype(BF16)
        lse_ref[...] = lse_all

    return pl.pallas_call(
        body, grid=(S // tq,),
        in_specs=[_rows(tq, BW), _const((S + 2 * WIN, 128)), _const((nc, 128, tq)), _const((1, NH))],
        out_specs=[_rows(tq, BW), pl.BlockSpec((NH, tq), lambda i: (0, i))],
        out_shape=[jax.ShapeDtypeStruct((S, BW), BF16), jax.ShapeDtypeStruct((NH, S), F32)],
        compiler_params=_params(1), name="attn_b_fwd",
    )(q, kp, vpt3, sink2)


def _merge_out(ya, yb, ga, gb, x, modv, wb, wout):
    S = x.shape[0]
    tm = min(256, S)

    def body(ya_ref, yb_ref, ga_ref, gb_ref, x_ref, mod_ref, wb_ref, wo_ref, x1_ref, mg_ref, ua_ref, ub_ref):
        ua = jnp.dot(ya_ref[...], wb_ref[0], preferred_element_type=F32)
        ub = jnp.dot(yb_ref[...], wb_ref[1], preferred_element_type=F32)
        merged = jax.nn.sigmoid(ga_ref[...].astype(F32)) * ua + jax.nn.sigmoid(gb_ref[...].astype(F32)) * ub
        mb = merged.astype(BF16)
        ua_ref[...] = ua.astype(BF16)
        ub_ref[...] = ub.astype(BF16)
        mg_ref[...] = mb
        x1_ref[...] = x_ref[...] + mod_ref[2:3, :] * jnp.dot(mb, wo_ref[...], preferred_element_type=F32)

    return pl.pallas_call(
        body, grid=(S // tm,),
        in_specs=[_rows(tm, BW), _rows(tm, BW), _rows(tm, D), _rows(tm, D), _rows(tm, D), _const((6, D)),
                  _const((2, BW, D)), _const((D, D))],
        out_specs=[_rows(tm, D)] * 4,
        out_shape=[jax.ShapeDtypeStruct((S, D), F32)] + [jax.ShapeDtypeStruct((S, D), BF16)] * 3,
        compiler_params=_params(1), name="merge_out",
    )(ya, yb, ga, gb, x, modv, wb, wout)


def _mlp_fwd(x1, modv, n2g, wmi, wmo, fg, target):
    S = x1.shape[0]
    tm = min(512, S)
    tf = wmi.shape[2]
    nj = wmi.shape[0] // MLP_SHARDS

    def body(x1_ref, mod_ref, g_ref, wi_ref, wo_ref, fg_ref, t_ref, h2_ref, hp_ref, dx2_ref, st_ref, acc_ref):
        i, j = pl.program_id(0), pl.program_id(1)

        @pl.when(j == 0)
        def _():
            xt = x1_ref[...]
            r = lax.rsqrt(jnp.mean(xt * xt, axis=-1, keepdims=True) + EPS)
            h2 = ((xt * r) * g_ref[...]) * (1.0 + mod_ref[4:5, :]) + mod_ref[3:4, :]
            h2_ref[...] = h2.astype(BF16)
            acc_ref[...] = jnp.zeros_like(acc_ref)

        @pl.when((i == 0) & (j == 0))
        def _():
            st_ref[...] = jnp.zeros_like(st_ref)

        out = None
        for u in range(MLP_SHARDS):
            hp = jnp.dot(h2_ref[...], wi_ref[u], preferred_element_type=F32)
            hp_ref[:, tf * u:tf * u + tf] = hp.astype(BF16)
            hid = jnp.square(jnp.maximum(hp, 0.0))
            part = jnp.dot(hid.astype(BF16), wo_ref[u], preferred_element_type=F32)
            out = part if out is None else out + part
        acc_ref[...] += out

        @pl.when(j == nj - 1)
        def _():
            x2 = x1_ref[...] + mod_ref[5:6, :] * acc_ref[...]
            r3 = lax.rsqrt(jnp.mean(x2 * x2, axis=-1, keepdims=True) + EPS)
            xn = x2 * r3
            err = xn * fg_ref[...] - t_ref[...]
            dy = err * (1.0 / D)
            gy = dy * fg_ref[...]
            dx2_ref[...] = r3 * (gy - xn * jnp.mean(gy * xn, axis=-1, keepdims=True))
            st_ref[0:1, :] += jnp.sum(dy * xn, axis=0, keepdims=True)
            st_ref[1:2, :] += jnp.sum(err * err, axis=0, keepdims=True) * (0.5 / D)

    return pl.pallas_call(
        body, grid=(S // tm, nj),
        in_specs=[pl.BlockSpec((tm, D), lambda i, j: (i, 0)), _const((6, D)), _const((1, D)),
                  pl.BlockSpec((MLP_SHARDS, D, tf), lambda i, j: (j, 0, 0)), pl.BlockSpec((MLP_SHARDS, tf, D), lambda i, j: (j, 0, 0)),
                  _const((1, D)), pl.BlockSpec((tm, D), lambda i, j: (i, 0))],
        out_specs=[pl.BlockSpec((tm, D), lambda i, j: (i, 0)), pl.BlockSpec((tm, MLP_SHARDS * tf), lambda i, j: (i, j)),
                   pl.BlockSpec((tm, D), lambda i, j: (i, 0)), _const((8, D))],
        out_shape=[jax.ShapeDtypeStruct((S, D), BF16), jax.ShapeDtypeStruct((S, wmi.shape[0] * tf), BF16),
                   jax.ShapeDtypeStruct((S, D), F32), jax.ShapeDtypeStruct((8, D), F32)],
        scratch_shapes=[pltpu.VMEM((tm, D), F32)],
        compiler_params=_params(2), name="mlp_fwd",
    )(x1, modv, n2g, wmi, wmo, fg, target)


def _mlp_bwd(dx2, x1, hp, modv, n2g, wmi, wmo):
    S = x1.shape[0]
    tm = min(512, S)
    tf = wmi.shape[2]
    nj = wmi.shape[0] // MLP_SHARDS

    def body(dx2_ref, x1_ref, hp_ref, mod_ref, g_ref, wi_ref, wo_ref, dhp_ref, dx1_ref, st_ref, dmo_ref, acc_ref):
        i, j = pl.program_id(0), pl.program_id(1)

        @pl.when(j == 0)
        def _():
            dmo_ref[...] = (mod_ref[5:6, :] * dx2_ref[...]).astype(BF16)
            acc_ref[...] = jnp.zeros_like(acc_ref)

        @pl.when((i == 0) & (j == 0))
        def _():
            st_ref[...] = jnp.zeros_like(st_ref)

        out = None
        for u in range(MLP_SHARDS):
            sl = slice(tf * u, tf * u + tf)
            dhid = lax.dot_general(dmo_ref[...], wo_ref[u], NT, preferred_element_type=F32)
            dhp = (dhid * (2.0 * jnp.maximum(hp_ref[:, sl].astype(F32), 0.0))).astype(BF16)
            dhp_ref[:, sl] = dhp
            part = lax.dot_general(dhp, wi_ref[u], NT, preferred_element_type=F32)
            out = part if out is None else out + part
        acc_ref[...] += out

        @pl.when(j == nj - 1)
        def _():
            dh2 = acc_ref[...]
            xt = x1_ref[...]
            r = lax.rsqrt(jnp.mean(xt * xt, axis=-1, keepdims=True) + EPS)
            xn = xt * r
            st_ref[0:1, :] += jnp.sum(dh2, axis=0, keepdims=True)
            st_ref[1:2, :] += jnp.sum(dh2 * xn, axis=0, keepdims=True)
            dxn = dh2 * (g_ref[...] * (1.0 + mod_ref[4:5, :]))
            dx1_ref[...] = dx2_ref[...] + r * (dxn - xn * jnp.mean(dxn * xn, axis=-1, keepdims=True))

    return pl.pallas_call(
        body, grid=(S // tm, nj),
        in_specs=[pl.BlockSpec((tm, D), lambda i, j: (i, 0)), pl.BlockSpec((tm, D), lambda i, j: (i, 0)),
                  pl.BlockSpec((tm, MLP_SHARDS * tf), lambda i, j: (i, j)), _const((6, D)), _const((1, D)),
                  pl.BlockSpec((MLP_SHARDS, D, tf), lambda i, j: (j, 0, 0)), pl.BlockSpec((MLP_SHARDS, tf, D), lambda i, j: (j, 0, 0))],
        out_specs=[pl.BlockSpec((tm, MLP_SHARDS * tf), lambda i, j: (i, j)), pl.BlockSpec((tm, D), lambda i, j: (i, 0)), _const((8, D))],
        out_shape=[jax.ShapeDtypeStruct((S, wmi.shape[0] * tf), BF16), jax.ShapeDtypeStruct((S, D), F32),
                   jax.ShapeDtypeStruct((8, D), F32)],
        scratch_shapes=[pltpu.VMEM((tm, D), BF16), pltpu.VMEM((tm, D), F32)],
        compiler_params=_params(2), name="mlp_bwd",
    )(dx2, x1, hp, modv, n2g, wmi, wmo)


def _tn_matmul(a, b, tk, tn, name, relu_sq=False, dev_major=False):
    S, K = a.shape
    N = b.shape[1]
    ts = min(1024, S)
    ns = S // ts

    def body(a_ref, b_ref, o_ref):
        @pl.when(pl.program_id(2) == 0)
        def _():
            o_ref[...] = jnp.zeros_like(o_ref)

        at = a_ref[...]
        if relu_sq:
            at = jnp.square(jnp.maximum(at.astype(F32), 0.0)).astype(BF16)
        o_ref[...] += lax.dot_general(at, b_ref[...].astype(BF16), TN, preferred_element_type=F32)

    if dev_major:
        out_spec = pl.BlockSpec((None, tk, tn), lambda k, n, s: (n, k, 0))
        out_shape = jax.ShapeDtypeStruct((N // tn, K, tn), F32)
    else:
        out_spec = pl.BlockSpec((tk, tn), lambda k, n, s: (k, n))
        out_shape = jax.ShapeDtypeStruct((K, N), F32)
    return pl.pallas_call(
        body, grid=(K // tk, N // tn, ns),
        in_specs=[pl.BlockSpec((ts, tk), lambda k, n, s: (s, k)), pl.BlockSpec((ts, tn), lambda k, n, s: (s, n))],
        out_specs=out_spec, out_shape=out_shape,
        compiler_params=_params(3), name=name,
    )(a, b)


def _scale_gate(m, w, g, row, name):
    K = m.shape[0]
    tk = min(512, K)

    def body(m_ref, w_ref, mod_ref, dw_ref, dg_ref):
        @pl.when(pl.program_id(0) == 0)
        def _():
            dg_ref[...] = jnp.zeros_like(dg_ref)

        mt = m_ref[...]
        dw_ref[...] = mt * mod_ref[row:row + 1, :]
        dg_ref[0:1, :] += jnp.sum(mt * w_ref[...].astype(F32), axis=0, keepdims=True)

    return pl.pallas_call(
        body, grid=(K // tk,),
        in_specs=[_rows(tk, D), _rows(tk, D), _const((6, D))],
        out_specs=[_rows(tk, D), _const((8, D))],
        out_shape=[jax.ShapeDtypeStruct((K, D), F32), jax.ShapeDtypeStruct((8, D), F32)],
        compiler_params=_params(1), name=name,
    )(m, w, g)


def _merge_bwd(dx1, modv, ga, gb, ua, ub, ya, yb, wb, wout):
    S = dx1.shape[0]
    tm = min(512, S)

    def body(dx1_ref, mod_ref, ga_ref, gb_ref, ua_ref, ub_ref, ya_ref, yb_ref, wb_ref, wo_ref,
             dua_ref, dub_ref, dga_ref, dgb_ref, dya_ref, dyb_ref, dla_ref, dlb_ref, dyat_ref):
        dao = (mod_ref[2:3, :] * dx1_ref[...]).astype(BF16)
        dm = lax.dot_general(dao, wo_ref[...], NT, preferred_element_type=F32)
        r = lax.broadcasted_iota(jnp.int32, (BW, NH), 0) // HD
        c = lax.broadcasted_iota(jnp.int32, (BW, NH), 1)
        head_of = (r == c).astype(BF16)
        for br, (g_ref, u_ref, y_ref, du_ref, dg_ref, dy_ref, dl_ref) in enumerate((
                (ga_ref, ua_ref, ya_ref, dua_ref, dga_ref, dya_ref, dla_ref),
                (gb_ref, ub_ref, yb_ref, dub_ref, dgb_ref, dyb_ref, dlb_ref))):
            sg = jax.nn.sigmoid(g_ref[...].astype(F32))
            du = (dm * sg).astype(BF16)
            du_ref[...] = du
            dg_ref[...] = (dm * u_ref[...].astype(F32) * sg * (1.0 - sg)).astype(BF16)
            dy = lax.dot_general(du, wb_ref[br], NT, preferred_element_type=F32)
            dyb16 = dy.astype(BF16)
            dy_ref[...] = dyb16
            if br == 0:
                dyat_ref[...] = dy.T.astype(BF16)
            prod = dyb16.astype(F32) * y_ref[...].astype(F32)
            hi = prod.astype(BF16)
            lo = (prod - hi.astype(F32)).astype(BF16)
            dl_ref[...] = (jnp.dot(hi, head_of, preferred_element_type=F32) + jnp.dot(lo, head_of, preferred_element_type=F32))

    return pl.pallas_call(
        body, grid=(S // tm,),
        in_specs=[_rows(tm, D), _const((6, D)), _rows(tm, D), _rows(tm, D), _rows(tm, D), _rows(tm, D),
                  _rows(tm, BW), _rows(tm, BW), _const((2, BW, D)), _const((D, D))],
        out_specs=[_rows(tm, D)] * 4 + [_rows(tm, BW)] * 2 + [_rows(tm, NH)] * 2 + [pl.BlockSpec((BW, tm), lambda i: (0, i))],
        out_shape=[jax.ShapeDtypeStruct((S, D), BF16)] * 4 + [jax.ShapeDtypeStruct((S, BW), BF16)] * 2
        + [jax.ShapeDtypeStruct((S, NH), F32)] * 2 + [jax.ShapeDtypeStruct((BW, S), BF16)],
        compiler_params=_params(1), name="merge_bwd",
    )(dx1, modv, ga, gb, ua, ub, ya, yb, wb, wout)


def _attn_a_bwd(q, qt, kt3, vt3, do, dot_, lse, delta, chunks):
    S = q.shape[0]
    tq = min(512, S)
    nk, _, tk = kt3.shape
    nq = S // tq
    nc = len(chunks)

    def body(q_ref, qt_ref, do_ref, dot_ref, lse_ref, dl_ref, kt_ref, vt_ref, *rest):
        g_hbm, (dq_ref, dk_hbm, dv_hbm), recv_hbm = rest[:nc], rest[nc:nc + 3], rest[nc + 3:2 * nc + 3]
        dk_sc, dv_sc, sem, send_sems, recv_sems = rest[2 * nc + 3:]
        i = pl.program_id(0)
        start, finish = _exchange_scatter(g_hbm, recv_hbm, send_sems, recv_sems)

        @pl.when(i == 0)
        def _():
            start()
            dk_sc[...] = jnp.zeros_like(dk_sc)
            dv_sc[...] = jnp.zeros_like(dv_sc)

        for kv in range(2):
            qg = q_ref[:, 256 * kv:256 * kv + 256]
            dog = do_ref[:, 256 * kv:256 * kv + 256]
            heads = []
            for t in range(GRP):
                h = GRP * kv + t
                heads.append((qg[:, HD * t:HD * t + HD], dog[:, HD * t:HD * t + HD],
                              qt_ref[HD * h:HD * h + HD, :], dot_ref[HD * h:HD * h + HD, :],
                              lse_ref[:, h:h + 1], dl_ref[:, h:h + 1]))

            q2 = [jnp.concatenate([heads[2 * u][0], heads[2 * u + 1][0]], axis=0) for u in range(GRP // 2)]
            do2 = [jnp.concatenate([heads[2 * u][1], heads[2 * u + 1][1]], axis=0) for u in range(GRP // 2)]

            def step(j, carry, kv=kv, heads=heads, q2=q2, do2=do2):
                kjt = kt_ref[j, HD * kv:HD * kv + HD, :]
                vjt = vt_ref[j, HD * kv:HD * kv + HD, :]
                dkt = jnp.zeros((HD, tk), F32)
                dvt = jnp.zeros((HD, tk), F32)
                new = []

                def logits(u):
                    return (jnp.dot(q2[u], kjt, preferred_element_type=F32), jnp.dot(do2[u], vjt, preferred_element_type=F32))

                sd = [logits(0)]
                for t, (qh, doh, qth, doth, lse_h, dl_h) in enumerate(heads):
                    if t == 0:
                        sd.append(logits(1))
                    rows = slice(tq * (t % 2), tq * (t % 2) + tq)
                    s, dp = sd[t // 2][0][rows, :], sd[t // 2][1][rows, :]
                    pm = jnp.exp2(s - lse_h)
                    ds = (pm * (dp - dl_h)).astype(BF16)
                    dvt = dvt + jnp.dot(doth, pm.astype(BF16), preferred_element_type=F32)
                    dkt = dkt + jnp.dot(qth, ds, preferred_element_type=F32)
                    new.append(carry[t] + lax.dot_general(kjt, ds, NT, preferred_element_type=F32))
                dk_sc[j, HD * kv:HD * kv + HD, :] += dkt
                dv_sc[j, HD * kv:HD * kv + HD, :] += dvt
                return tuple(new)

            res = lax.fori_loop(0, nk, step, (jnp.zeros((HD, tq), F32),) * GRP)
            for t in range(GRP):
                dq_ref[HD * (GRP * kv + t):HD * (GRP * kv + t) + HD, :] = res[t]

        @pl.when(i == nq - 1)
        def _():
            c1 = pltpu.make_async_copy(dk_sc, dk_hbm, sem.at[0])
            c2 = pltpu.make_async_copy(dv_sc, dv_hbm, sem.at[1])
            c1.start()
            c2.start()
            c1.wait()
            c2.wait()
            finish()

    any_spec = pl.BlockSpec(memory_space=pl.ANY)
    cols = pl.BlockSpec((BW, tq), lambda i: (0, i))
    return pl.pallas_call(
        body, grid=(nq,),
        in_specs=[_rows(tq, BW), cols, _rows(tq, BW), cols, _rows(tq, NH), _rows(tq, NH), _const((nk, 128, tk)),
                  _const((nk, 128, tk))] + [any_spec] * nc,
        out_specs=[cols, any_spec, any_spec] + [any_spec] * nc,
        out_shape=[jax.ShapeDtypeStruct((BW, S), F32), jax.ShapeDtypeStruct((nk, 128, tk), F32),
                   jax.ShapeDtypeStruct((nk, 128, tk), F32)]
        + [jax.ShapeDtypeStruct((N_DEV - 1,) + c.shape[1:], c.dtype) for c in chunks],
        scratch_shapes=[pltpu.VMEM((nk, 128, tk), F32), pltpu.VMEM((nk, 128, tk), F32), pltpu.SemaphoreType.DMA((2,))]
        + _exchange_sems(nc),
        compiler_params=_params(1), name="attn_a_bwd",
    )(q, qt, do, dot_, lse, delta, kt3, vt3, *chunks)


def _attn_b_bwd(q, kp, vp, sink2, do, lse, delta):
    S = q.shape[0]
    tq = min(TQ_B, S)
    W = tq + 2 * WIN
    nq = S // tq
    nc = (S + 2 * WIN) // tq

    def body(q_ref, k_ref, v_ref, sink_ref, do_ref, lse_ref, dl_ref, dq_ref, dk_hbm, dv_hbm, ds_ref, dk_sc, dv_sc, sem):
        i = pl.program_id(0)

        @pl.when(i == 0)
        def _():
            dk_sc[...] = jnp.zeros_like(dk_sc)
            dv_sc[...] = jnp.zeros_like(dv_sc)
            ds_ref[...] = jnp.zeros_like(ds_ref)

        off = pl.multiple_of(i * tq, tq)
        valid = _window_mask(i, tq, S)
        kw = k_ref[pl.ds(off, W), :]
        vw = v_ref[pl.ds(off, W), :]
        lse_i = lse_ref[...]
        dl_i = dl_ref[...]
        qa = q_ref[...]
        doa = do_ref[...]
        qt = qa.astype(F32).T.astype(BF16)
        dot_ = doa.astype(F32).T.astype(BF16)
        khs = [kw[:, HD * kv:HD * kv + HD] for kv in range(2)]
        vhs = [vw[:, HD * kv:HD * kv + HD] for kv in range(2)]

        def logits(h):
            return (lax.dot_general(qa[:, HD * h:HD * h + HD], khs[h // GRP], NT, preferred_element_type=F32),
                    lax.dot_general(doa[:, HD * h:HD * h + HD], vhs[h // GRP], NT, preferred_element_type=F32))

        sd = [logits(0)]
        dqs = []
        dkt = [jnp.zeros((HD, W), F32), jnp.zeros((HD, W), F32)]
        dvt = [jnp.zeros((HD, W), F32), jnp.zeros((HD, W), F32)]
        for h in range(NH):
            kv = h // GRP
            if h + 1 < NH:
                sd.append(logits(h + 1))
            s, dp = sd[h]
            pm = jnp.exp2(jnp.where(valid, s, NEG) - lse_i[:, h:h + 1])
            ds = (pm * (dp - dl_i[:, h:h + 1])).astype(BF16)
            dvt[kv] = dvt[kv] + jnp.dot(dot_[HD * h:HD * h + HD, :], pm.astype(BF16), preferred_element_type=F32)
            dkt[kv] = dkt[kv] + jnp.dot(qt[HD * h:HD * h + HD, :], ds, preferred_element_type=F32)
            dqs.append(jnp.dot(ds, khs[kv], preferred_element_type=F32))
        for p in range(4):
            dq_ref[:, 128 * p:128 * p + 128] = jnp.concatenate(dqs[2 * p:2 * p + 2], axis=1)
        for half in range(W // tq):
            dk_sc[i + half] += jnp.concatenate([d[:, tq * half:tq * half + tq] for d in dkt], axis=0)
            dv_sc[i + half] += jnp.concatenate([d[:, tq * half:tq * half + tq] for d in dvt], axis=0)
        psd = jnp.exp2(sink_ref[...] - lse_i) * dl_i
        r = lax.broadcasted_iota(jnp.int32, (NH, 128), 0)
        c = lax.broadcasted_iota(jnp.int32, (NH, 128), 1)
        row = jnp.dot(jnp.sum(psd, axis=0, keepdims=True), (r == c).astype(F32),
                      preferred_element_type=F32, precision=lax.Precision.HIGHEST)
        ds_ref[...] -= jnp.broadcast_to(row, (8, 128))

        @pl.when(i == nq - 1)
        def _():
            c1 = pltpu.make_async_copy(dk_sc, dk_hbm, sem.at[0])
            c2 = pltpu.make_async_copy(dv_sc, dv_hbm, sem.at[1])
            c1.start()
            c2.start()
            c1.wait()
            c2.wait()

    any_spec = pl.BlockSpec(memory_space=pl.ANY)
    return pl.pallas_call(
        body, grid=(nq,),
        in_specs=[_rows(tq, BW), _const((S + 2 * WIN, 128)), _const((S + 2 * WIN, 128)), _const((1, NH)),
                  _rows(tq, BW), _rows(tq, NH), _rows(tq, NH)],
        out_specs=[_rows(tq, BW), any_spec, any_spec, _const((8, 128))],
        out_shape=[jax.ShapeDtypeStruct((S, BW), F32), jax.ShapeDtypeStruct((nc, 128, tq), F32),
                   jax.ShapeDtypeStruct((nc, 128, tq), F32), jax.ShapeDtypeStruct((8, 128), F32)],
        scratch_shapes=[pltpu.VMEM((nc, 128, tq), F32), pltpu.VMEM((nc, 128, tq), F32), pltpu.SemaphoreType.DMA((2,))],
        compiler_params=_params(1), name="attn_b_bwd",
    )(q, kp, vp, sink2, do, lse, delta)


def _qk_bwd(dqa_t, dka_t3, dva_t3, dqb, dkb, dvb, qar, kar, qg2, kg2, tab_a, tab_b, dga, dgb):
    S = dqb.shape[0]
    tm = min(256, S)
    per = dka_t3.shape[2] // tm

    def body(dqa_ref, dka_ref, dva_ref, dqb_ref, dkb_ref, dvb_ref, qar_ref, kar_ref, qg_ref, kg_ref, ta_ref, tb_ref,
             dga_ref, dgb_ref, dp_ref, st_ref):
        @pl.when(pl.program_id(0) == 0)
        def _():
            st_ref[...] = jnp.zeros_like(st_ref)

        seg = _seg_matrix(128, HD)

        def norm_bwd(dz_rot, raw, g):
            dzn = _rope_t(dz_rot, ta_ref, 16)
            raw = raw.astype(F32)
            rr = lax.rsqrt(_seg_sum(raw * raw, seg) * (1.0 / HD) + EPS)
            zhat = raw * rr
            dzh = dzn * g
            draw = rr * (dzh - zhat * (_seg_sum(dzh * zhat, seg) * (1.0 / HD)))
            return draw, jnp.sum(dzn * zhat, axis=0, keepdims=True)

        gq = jnp.zeros((1, 128), F32)
        for p in range(4):
            sl = slice(128 * p, 128 * p + 128)
            draw, gsum = norm_bwd(dqa_ref[sl, :].T * 0.125, qar_ref[:, sl], qg_ref[...])
            gq = gq + gsum
            dp_ref[:, sl] = draw.astype(BF16)
            dp_ref[:, 768 + 128 * p:768 + 128 * p + 128] = _rope_t(dqb_ref[:, sl] * 0.125, tb_ref, 32).astype(BF16)
        draw, gk = norm_bwd(dka_ref[...].T * LN2, kar_ref[...], kg_ref[...])
        dp_ref[:, 512:640] = draw.astype(BF16)
        dp_ref[:, 640:768] = dva_ref[...].T.astype(BF16)
        dp_ref[:, 1280:1408] = _rope_t(dkb_ref[...] * LN2, tb_ref, 32).astype(BF16)
        dp_ref[:, 1408:1536] = dvb_ref[...].astype(BF16)
        dp_ref[:, 1536:2560] = dga_ref[...]
        dp_ref[:, 2560:3584] = dgb_ref[...]
        st_ref[0:1, :] += gq
        st_ref[1:2, :] += gk

    tab = pl.BlockSpec((3, tm, 128), lambda i: (0, i, 0))
    chunk_t = pl.BlockSpec((None, 128, tm), lambda i: (i // per, 0, i % per))
    return pl.pallas_call(
        body, grid=(S // tm,),
        in_specs=[pl.BlockSpec((BW, tm), lambda i: (0, i)), chunk_t, chunk_t, _rows(tm, BW), _rows(tm, 128), _rows(tm, 128),
                  _rows(tm, BW), _rows(tm, 128), _const((1, 128)), _const((1, 128)), tab, tab, _rows(tm, D), _rows(tm, D)],
        out_specs=[_rows(tm, INW), _const((8, 128))],
        out_shape=[jax.ShapeDtypeStruct((S, INW), BF16), jax.ShapeDtypeStruct((8, 128), F32)],
        compiler_params=_params(1), name="qk_bwd",
    )(dqa_t, dka_t3, dva_t3, dqb, dkb, dvb, qar, kar, qg2, kg2, tab_a, tab_b, dga, dgb)


def _in_bwd(dproj, win, x, dx1, modv, n1g, chunks):
    S = x.shape[0]
    tm = min(512, S)
    n = S // tm
    ts = min(256, tm)

    def body(dp_ref, w_ref, x_ref, dx1_ref, mod_ref, g_ref, c_hbm, gx_ref, st_ref, recv_hbm, send_sems, recv_sems):
        start, finish = _exchange_scatter([c_hbm], [recv_hbm], send_sems, recv_sems)

        @pl.when(pl.program_id(0) == 0)
        def _():
            start()
            st_ref[...] = jnp.zeros_like(st_ref)

        subs = [slice(ts * u, ts * u + ts) for u in range(tm // ts)]
        dhs = [lax.dot_general(dp_ref[rows, :], w_ref[...], NT, preferred_element_type=F32) for rows in subs]
        for rows, dh in zip(subs, dhs):
            xt = x_ref[rows, :]
            r = lax.rsqrt(jnp.mean(xt * xt, axis=-1, keepdims=True) + EPS)
            xn = xt * r
            st_ref[0:1, :] += jnp.sum(dh, axis=0, keepdims=True)
            st_ref[1:2, :] += jnp.sum(dh * xn, axis=0, keepdims=True)
            dxn = dh * (g_ref[...] * (1.0 + mod_ref[1:2, :]))
            gx_ref[rows, :] = dx1_ref[rows, :] + r * (dxn - xn * jnp.mean(dxn * xn, axis=-1, keepdims=True))
        pl.when(pl.program_id(0) == n - 1)(finish)

    any_spec = pl.BlockSpec(memory_space=pl.ANY)
    return pl.pallas_call(
        body, grid=(n,),
        in_specs=[_rows(tm, INW), _const((D, INW)), _rows(tm, D), _rows(tm, D), _const((6, D)), _const((1, D)), any_spec],
        out_specs=[_rows(tm, D), _const((8, D)), any_spec],
        out_shape=[jax.ShapeDtypeStruct((S, D), F32), jax.ShapeDtypeStruct((8, D), F32),
                   jax.ShapeDtypeStruct((N_DEV - 1,) + chunks.shape[1:], chunks.dtype)],
        scratch_shapes=_exchange_sems(1),
        compiler_params=_params(1), name="in_bwd",
    )(dproj, win, x, dx1, modv, n1g, chunks)


def _pack_small(st1, st2, stf, dg1, dg2, stqk, dsink, modv, n1g, n2g):
    def body(st1_ref, st2_ref, stf_ref, dg1_ref, dg2_ref, qk_ref, ds_ref, mod_ref, g1_ref, g2_ref, o_ref):
        a1, b1 = st1_ref[0:1, :], st1_ref[1:2, :]
        a2, b2 = st2_ref[0:1, :], st2_ref[1:2, :]
        r = lax.broadcasted_iota(jnp.int32, (128, D), 0)
        c = lax.broadcasted_iota(jnp.int32, (128, D), 1)
        fold_q = (c == r % HD).astype(F32)
        fold_k = (c == HD + r % HD).astype(F32)
        keep = (c == r).astype(F32)

        def place(v, sel):
            return jnp.dot(v, sel, preferred_element_type=F32, precision=lax.Precision.HIGHEST)

        loss = jnp.sum(stf_ref[1:2, :], axis=1, keepdims=True)
        lane = lax.broadcasted_iota(jnp.int32, (1, D), 1)
        rows = [a1, g1_ref[...] * b1, dg1_ref[0:1, :], a2, g2_ref[...] * b2, dg2_ref[0:1, :],
                (1.0 + mod_ref[1:2, :]) * b1, (1.0 + mod_ref[4:5, :]) * b2, stf_ref[0:1, :],
                place(qk_ref[0:1, :], fold_q) + place(qk_ref[1:2, :], fold_k),
                place(ds_ref[0:1, :], keep),
                jnp.where(lane == 0, loss, 0.0)]
        rows += [jnp.zeros((1, D), F32)] * (SMALL_ROWS - len(rows))
        for n, v in enumerate(rows):
            o_ref[n:n + 1, :] = v

    return pl.pallas_call(
        body, out_shape=jax.ShapeDtypeStruct((SMALL_ROWS, D), F32),
        compiler_params=pltpu.CompilerParams(vmem_limit_bytes=V7X_VMEM_LIMIT), name="pack_small",
    )(st1, st2, stf, dg1, dg2, stqk, dsink, modv, n1g, n2g)


def _wada_grad(silu_all, dmod_cols):
    def body(a_ref, b_ref, o_ref):
        o_ref[...] = lax.dot_general(a_ref[...], b_ref[...], TN, preferred_element_type=F32, precision=lax.Precision.HIGHEST)

    return pl.pallas_call(
        body, out_shape=jax.ShapeDtypeStruct((D, dmod_cols.shape[1]), F32),
        compiler_params=pltpu.CompilerParams(vmem_limit_bytes=V7X_VMEM_LIMIT), name="wada_grad",
    )(silu_all, dmod_cols)


def _adamw_sum(parts, w, m, v, name):
    R, C = w.shape
    tr = R if R <= 64 else next(t for t in (256, 128, 64, 32, 16, 8) if R % t == 0)
    n = len(parts)
    dyn = [idx for _, idx in parts if idx is not None and not isinstance(idx, int)]
    b1c = 1.0 - ADAM_B1 ** ADAM_STEP
    b2c = 1.0 - ADAM_B2 ** ADAM_STEP

    def body(*refs):
        refs = refs[len(dyn):]
        g = refs[0][...].astype(F32)
        for k in range(1, n):
            g = g + refs[k][...].astype(F32)
        w_ref, m_ref, v_ref, g_out, d_out, m_out, v_out = refs[n:]
        mn = ADAM_B1 * m_ref[...] + (1.0 - ADAM_B1) * g
        vn = ADAM_B2 * v_ref[...] + (1.0 - ADAM_B2) * jnp.square(g)
        g_out[...] = g
        m_out[...] = mn
        v_out[...] = vn
        d_out[...] = -ADAM_LR * ((mn / b1c) / (jnp.sqrt(vn / b2c) + ADAM_EPS) + ADAM_WD * w_ref[...])

    in_specs = []
    nd = 0
    for a, idx in parts:
        if idx is None:
            in_specs.append(pl.BlockSpec((tr, C), lambda i, *s: (i, 0)))
        elif isinstance(idx, int):
            in_specs.append(pl.BlockSpec((None, tr, C), lambda i, *s, idx=idx: (idx, i, 0)))
        else:
            in_specs.append(pl.BlockSpec((None, tr, C), lambda i, *s, nd=nd: (s[nd][0], i, 0)))
            nd += 1
    blk = pl.BlockSpec((tr, C), lambda i, *s: (i, 0))
    grid_spec = pltpu.PrefetchScalarGridSpec(
        num_scalar_prefetch=len(dyn), grid=(R // tr,), in_specs=in_specs + [blk] * 3, out_specs=[blk] * 4)
    return pl.pallas_call(
        body, grid_spec=grid_spec, out_shape=[jax.ShapeDtypeStruct((R, C), F32)] * 4,
        compiler_params=_params(1), name=name,
    )(*dyn, *[a for a, _ in parts], w, m, v)


def _me():
    return lax.axis_index("x"), lax.axis_index("y"), lax.axis_index("c")


def _peer(k):
    x, y, c = _me()
    return (x ^ ((k >> 2) & 1), y ^ ((k >> 1) & 1), c ^ (k & 1))


def _ada_exchange(c_row, w_ada, b_rows):
    NW = w_ada.shape[1]

    def body(c_ref, w_ref, b_ref, sall_ref, mod_ref, src_ref, mp_ref, send1, recv1, send2, recv2):
        x, y, c = _me()
        me = 4 * x + 2 * y + c
        cv = c_ref[...]
        src_ref[...] = jnp.broadcast_to(cv * jax.nn.sigmoid(cv), (8, D))
        mine = pl.ds(pl.multiple_of(me * 8, 8), 8)
        sall_ref[mine, :] = src_ref[...]
        sends = [pltpu.make_async_remote_copy(src_ref, sall_ref.at[mine, :], send1.at[k - 1], recv1.at[k - 1],
                                              device_id=_peer(k), device_id_type=MESH) for k in range(1, N_DEV)]
        for cp in sends:
            cp.start()
        for k in range(1, N_DEV):
            theirs = pl.ds(pl.multiple_of((me ^ k) * 8, 8), 8)
            pltpu.make_async_remote_copy(src_ref, sall_ref.at[theirs, :], send1.at[k - 1], recv1.at[k - 1],
                                         device_id=_peer(k), device_id_type=MESH).wait_recv()
        for cp in sends:
            cp.wait_send()
        mp_ref[...] = jnp.dot(sall_ref[...], w_ref[...], preferred_element_type=F32, precision=lax.Precision.HIGHEST)
        mod_ref[mine, :] = mp_ref[mine, :] + b_ref[mine, :]
        sends = []
        for k in range(1, N_DEV):
            theirs = pl.ds(pl.multiple_of((me ^ k) * 8, 8), 8)
            sends.append(pltpu.make_async_remote_copy(mp_ref.at[theirs, :], mod_ref.at[mine, :], send2.at[k - 1], recv2.at[k - 1],
                                                      device_id=_peer(k), device_id_type=MESH))
        for cp in sends:
            cp.start()
        for k in range(1, N_DEV):
            theirs = pl.ds(pl.multiple_of((me ^ k) * 8, 8), 8)
            pltpu.make_async_remote_copy(mp_ref.at[mine, :], mod_ref.at[theirs, :], send2.at[k - 1], recv2.at[k - 1],
                                         device_id=_peer(k), device_id_type=MESH).wait_recv()
            mod_ref[theirs, :] = mod_ref[theirs, :] + b_ref[theirs, :]
        for cp in sends:
            cp.wait_send()

    vm = pl.BlockSpec(memory_space=pltpu.VMEM)
    return pl.pallas_call(
        body, in_specs=[vm, vm, vm], out_specs=[vm, vm],
        out_shape=[jax.ShapeDtypeStruct((8 * N_DEV, D), F32), jax.ShapeDtypeStruct((8 * N_DEV, NW), F32)],
        scratch_shapes=[pltpu.VMEM((8, D), F32), pltpu.VMEM((8 * N_DEV, NW), F32)]
        + [pltpu.SemaphoreType.DMA((N_DEV - 1,))] * 4,
        compiler_params=pltpu.CompilerParams(vmem_limit_bytes=V7X_VMEM_LIMIT), name="ada_exchange",
    )(c_row, w_ada, b_rows)


def _weight_gather(shard):
    def body(x_ref, out_ref, send_sems, recv_sems, local_sem):
        x, y, c = _me()
        me, sibling = (x, y, c), (x, y, 1 - c)
        chips = [(1 - x, y), (x, 1 - y), (1 - x, 1 - y)]

        def slot(px, py, pc):
            return out_ref.at[4 * px + 2 * py + pc]

        def copy(k, block, to, src=None):
            return pltpu.make_async_remote_copy(
                src_ref=slot(*block) if src is None else src, dst_ref=slot(*block),
                send_sem=send_sems.at[k], recv_sem=recv_sems.at[k], device_id=to, device_id_type=MESH)

        mine = pltpu.make_async_copy(x_ref, slot(*me), local_sem)
        mine.start()
        first = [copy(0, me, sibling, src=x_ref)]
        first += [copy(1 + j, me, (*chip, c), src=x_ref) for j, chip in enumerate(chips)]
        for cp in first:
            cp.start()
        passed = [copy(4 + j, (*chip, c), sibling) for j, chip in enumerate(chips)]
        for j, chip in enumerate(chips):
            copy(1 + j, (*chip, c), me).wait_recv()
            passed[j].start()
        copy(0, sibling, me).wait_recv()
        for j, chip in enumerate(chips):
            copy(4 + j, (*chip, 1 - c), me).wait_recv()
        for cp in first + passed:
            cp.wait_send()
        mine.wait()

    any_spec = pl.BlockSpec(memory_space=pl.ANY)
    return pl.pallas_call(
        body, in_specs=[any_spec], out_specs=any_spec,
        out_shape=jax.ShapeDtypeStruct((N_DEV,) + shard.shape, shard.dtype),
        scratch_shapes=[pltpu.SemaphoreType.DMA((7,)), pltpu.SemaphoreType.DMA((7,)), pltpu.SemaphoreType.DMA],
        name="weight_gather",
    )(shard)


def _small_gather(block):
    def body(b_ref, out_ref, send_sems, recv_sems):
        x, y, c = _me()
        me = 4 * x + 2 * y + c
        out_ref[me] = b_ref[...]
        sends = [pltpu.make_async_remote_copy(b_ref, out_ref.at[me], send_sems.at[k - 1], recv_sems.at[k - 1],
                                              device_id=_peer(k), device_id_type=MESH) for k in range(1, N_DEV)]
        for cp in sends:
            cp.start()
        for k in range(1, N_DEV):
            pltpu.make_async_remote_copy(b_ref, out_ref.at[me ^ k], send_sems.at[k - 1], recv_sems.at[k - 1],
                                         device_id=_peer(k), device_id_type=MESH).wait_recv()
        for cp in sends:
            cp.wait_send()

    vm = pl.BlockSpec(memory_space=pltpu.VMEM)
    return pl.pallas_call(
        body, in_specs=[vm], out_specs=vm,
        out_shape=jax.ShapeDtypeStruct((N_DEV,) + block.shape, block.dtype),
        scratch_shapes=[pltpu.SemaphoreType.DMA((N_DEV - 1,)), pltpu.SemaphoreType.DMA((N_DEV - 1,))],
        name="small_gather",
    )(block)


def _pack_small_params(b_ada, n1, n2, fg, qn, kn, sink):
    z = jnp.zeros((SMALL_ROWS, D), F32)
    z = z.at[0:6].set(b_ada.reshape(6, D)).at[6].set(n1.reshape(D)).at[7].set(n2.reshape(D)).at[8].set(fg.reshape(D))
    z = z.at[9, 0:HD].set(qn.reshape(HD)).at[9, HD:2 * HD].set(kn.reshape(HD)).at[10, 0:NH].set(sink.reshape(NH))
    return z


def _unpack_small(p):
    return (p[0:6].reshape(1, 6 * D), p[6].reshape(1, D), p[9, 0:HD].reshape(1, HD), p[9, HD:2 * HD].reshape(1, HD),
            p[10, 0:NH].reshape(1, NH), p[7].reshape(1, D), p[8].reshape(D))


def kernel(x, c, w_ada, b_ada, norm1_g, w_in, q_norm_a, k_norm_a, sink_b, w_branch, w_out, norm2_g, w_mlp_in, w_mlp_out, final_g, loss_target, m_w_ada, m_b_ada, m_norm1_g, m_w_in, m_q_norm_a, m_k_norm_a, m_sink_b, m_w_branch, m_w_out, m_norm2_g, m_w_mlp_in, m_w_mlp_out, m_final_g, v_w_ada, v_b_ada, v_norm1_g, v_w_in, v_q_norm_a, v_k_norm_a, v_sink_b, v_w_branch, v_w_out, v_norm2_g, v_w_mlp_in, v_w_mlp_out, v_final_g):
    S = x.shape[1]
    xs = x.reshape(S, D)
    tgt = loss_target.reshape(S, D)
    ax, ay, ac = lax.axis_index("x"), lax.axis_index("y"), lax.axis_index("c")
    me = 4 * ax + 2 * ay + ac
    me1 = me.reshape(1).astype(jnp.int32)
    NW = w_ada.shape[2]
    NI = w_in.shape[2]

    silu64, mod64 = _ada_exchange(c.reshape(1, D), w_ada.reshape(D, NW),
                                  jnp.repeat(b_ada.reshape(N_DEV, NW), 8, axis=0))
    silu_all = silu64[0::8]
    modv = mod64[0::8].reshape(6, D)

    win = _weight_gather(w_in[0].astype(BF16)).transpose(1, 0, 2).reshape(D, INW)
    rest_shards = tuple(w[0].astype(BF16) for w in (w_branch, w_out, w_mlp_in, w_mlp_out))

    tab_a, tab_b = _rope_tables(S)
    qg2 = jnp.tile(q_norm_a.reshape(1, HD), (1, 2))
    kg2 = jnp.tile(k_norm_a.reshape(1, HD), (1, 2))
    n1g = norm1_g.reshape(1, D)
    n2g = norm2_g.reshape(1, D)
    fg = final_g.reshape(1, D)
    sink2 = sink_b.reshape(1, NH) * LOG2E

    h, qar, kar, qa, ka, va, qb, kb, vb, ga, gb, qa_t, ka_t3, va_t3 = _in_proj(xs, modv, n1g, win, qg2, kg2, tab_a, tab_b)
    ya, lse_at, wb, wout, wmi, wmo = _attn_a_fwd(qa_t, ka, va_t3, rest_shards)
    lse_a = lse_at.T
    wb = wb.transpose(1, 2, 0, 3).reshape(2, BW, D)
    wout = wout.reshape(D, D)
    pad = ((WIN, WIN), (0, 0))
    kbp, vbp = jnp.pad(kb, pad), jnp.pad(vb, pad)
    tb = min(TQ_B, S)
    yb, lse_bt = _attn_b_fwd(qb, kbp, vbp.reshape((S + 2 * WIN) // tb, tb, 128).transpose(0, 2, 1), sink2)
    lse_b = lse_bt.T
    x1, merged, ua, ub = _merge_out(ya, yb, ga, gb, xs, modv, wb, wout)
    h2, hp, dx2, stf = _mlp_fwd(x1, modv, n2g, wmi, wmo, fg, tgt)

    dhp, dx1, st2 = _mlp_bwd(dx2, x1, hp, modv, n2g, wmi, wmo)
    m2 = _tn_matmul(hp, dx2, 2048, D, "dw_mlp_out", relu_sq=True)
    g_wmo, dg2 = _scale_gate(m2, wmo.reshape(FF, D), modv, 5, "gate2_grad")
    g_wmi = _tn_matmul(h2, dhp, D, 512, "dw_mlp_in", dev_major=True)
    dua, dub, dga, dgb, dya, dyb, dl_a, dl_b, dya_t = _merge_bwd(dx1, modv, ga, gb, ua, ub, ya, yb, wb, wout)
    m1 = _tn_matmul(merged, dx1, D, D, "dw_out")
    g_wout, dg1 = _scale_gate(m1, wout, modv, 2, "gate1_grad")
    g_wb0 = _tn_matmul(ya, dua, BW, D, "dw_branch_a")
    g_wb1 = _tn_matmul(yb, dub, BW, D, "dw_branch_b")
    g_wb = jnp.stack([g_wb0, g_wb1]).reshape(2, BW, N_DEV, 128).transpose(2, 0, 1, 3).reshape(N_DEV, 2 * BW, 128)
    g_wout = g_wout.reshape(N_DEV, 128, D)
    g_wmo = g_wmo.reshape(N_DEV, 512, D)
    dqa_t, dka_t3, dva_t3, r_wb, r_wout, r_wmi, r_wmo = _attn_a_bwd(qa, qa_t, ka_t3, va_t3, dya, dya_t, lse_a, dl_a,
                                                                    (g_wb, g_wout, g_wmi, g_wmo))
    dqb, dkb_t, dvb_t, dsink = _attn_b_bwd(qb, kbp, vbp, sink2, dyb, lse_b, dl_b)
    dkb = dkb_t.transpose(0, 2, 1).reshape(S + 2 * WIN, 128)[WIN:WIN + S]
    dvb = dvb_t.transpose(0, 2, 1).reshape(S + 2 * WIN, 128)[WIN:WIN + S]
    dproj, stqk = _qk_bwd(dqa_t, dka_t3, dva_t3, dqb, dkb, dvb, qar, kar, qg2, kg2, tab_a, tab_b, dga, dgb)
    g_win = _tn_matmul(h, dproj, D, 896, "dw_in")
    g_win = g_win.reshape(D, N_DEV, NI).transpose(1, 0, 2)
    grad_x, st1, r_win = _in_bwd(dproj, win, xs, dx1, modv, n1g, g_win.astype(BF16))

    def adam(name, own, recv, w, m, v):
        shape = w.shape
        w2, m2_, v2 = (a.reshape(own.shape[1:]) for a in (w, m, v))
        outs = _adamw_sum([(own, me1)] + [(recv, k) for k in range(N_DEV - 1)], w2, m2_, v2, name)
        return [a.reshape(shape) for a in outs]

    o_win = adam("adamw_w_in", g_win, r_win, w_in, m_w_in, v_w_in)
    o_wb = adam("adamw_w_branch", g_wb, r_wb, w_branch, m_w_branch, v_w_branch)
    o_wout = adam("adamw_w_out", g_wout, r_wout, w_out, m_w_out, v_w_out)
    o_wmi = adam("adamw_w_mlp_in", g_wmi, r_wmi, w_mlp_in, m_w_mlp_in, v_w_mlp_in)
    o_wmo = adam("adamw_w_mlp_out", g_wmo, r_wmo, w_mlp_out, m_w_mlp_out, v_w_mlp_out)

    small = _pack_small(st1, st2, stf, dg1, dg2, stqk, dsink, modv, n1g, n2g)
    small_all = _small_gather(small)
    sw = _pack_small_params(b_ada, norm1_g, norm2_g, final_g, q_norm_a, k_norm_a, sink_b)
    sm = _pack_small_params(m_b_ada, m_norm1_g, m_norm2_g, m_final_g, m_q_norm_a, m_k_norm_a, m_sink_b)
    sv = _pack_small_params(v_b_ada, v_norm1_g, v_norm2_g, v_final_g, v_q_norm_a, v_k_norm_a, v_sink_b)
    sm_out = _adamw_sum([(small_all, k) for k in range(N_DEV)], sw, sm, sv, "adamw_small")
    loss = sm_out[0][11, 0]
    sm_out = [_unpack_small(a) for a in sm_out]

    dmod_all = small_all[:, 0:6, :].reshape(N_DEV, 6 * D)
    dmod_cols = lax.dynamic_slice_in_dim(dmod_all, me * NW, NW, axis=1)
    g_wada = _wada_grad(silu_all, dmod_cols)
    ada = _adamw_sum([(g_wada, None)], w_ada.reshape(D, NW), m_w_ada.reshape(D, NW), v_w_ada.reshape(D, NW), "adamw_ada")
    ada = [a.reshape(1, D, NW) for a in ada]

    def leaves(k):
        b_, n1_, qn_, kn_, sk_, n2_, fg_ = sm_out[k]
        return [ada[k], b_, n1_, o_win[k], qn_, kn_, sk_, o_wb[k], o_wout[k], n2_, o_wmi[k], o_wmo[k], fg_]

    return (loss, grad_x.reshape(1, S, D), *leaves(0), *leaves(1), *leaves(2), *leaves(3))
```

```python
import jax
import jax.numpy as jnp
from jax import lax
from jax.experimental import pallas as pl
from jax.experimental.pallas import tpu as pltpu

F32, BF16 = jnp.float32, jnp.bfloat16
MESH = pl.DeviceIdType.MESH

D = 1024
HD = 64
NH = 8
GRP = 4
BW = 512
FF = 4096
INW = 3584
GRID_W = 64
WIN = 128
THETA = 10000.0
EPS = 1e-6
NEG = -1e30
N_DEV = 8
LOG2E = 1.4426950408889634
LN2 = 0.6931471805599453
QA_SCALE = 0.125 * LOG2E
SMALL_ROWS = 16
MLP_SHARDS = 4
TK_A = 512
TK_A_FWD = 2048
V7X_VMEM_LIMIT = 56 * 1024 * 1024

ADAM_LR, ADAM_B1, ADAM_B2, ADAM_EPS, ADAM_WD, ADAM_STEP = 0.001, 0.9, 0.999, 1e-08, 0.01, 10

NT = (((1,), (1,)), ((), ()))
TN = (((0,), (0,)), ((), ()))


def _params(n_axes, vmem=V7X_VMEM_LIMIT):
    return pltpu.CompilerParams(dimension_semantics=("arbitrary",) * n_axes, vmem_limit_bytes=vmem)


def _const(shape):
    return pl.BlockSpec(shape, lambda *_: (0,) * len(shape))


def _rows(tm, width):
    return pl.BlockSpec((tm, width), lambda i, *_: (i, 0))


def _seg_matrix(n, seg):
    r = lax.broadcasted_iota(jnp.int32, (n, n), 0) // seg
    c = lax.broadcasted_iota(jnp.int32, (n, n), 1) // seg
    return (r == c).astype(BF16)


def _seg_sum(z, seg_mat):
    hi = z.astype(BF16)
    lo = (z - hi.astype(F32)).astype(BF16)
    return jnp.dot(hi, seg_mat, preferred_element_type=F32) + jnp.dot(lo, seg_mat, preferred_element_type=F32)


def _rope(z, t_ref, sh):
    return z * t_ref[0] + pltpu.roll(z, sh, 1) * t_ref[1] + pltpu.roll(z, 128 - sh, 1) * t_ref[2]


def _rope_t(dz, t_ref, sh):
    return dz * t_ref[0] + pltpu.roll(dz * t_ref[1], 128 - sh, 1) + pltpu.roll(dz * t_ref[2], sh, 1)


def _rope_tables(S):
    t = jnp.arange(S, dtype=jnp.int32)
    lane = jnp.arange(HD)

    def build(cos, sin, first):
        t0 = cos
        t1 = jnp.where(first[None, :], 0.0, sin)
        t2 = jnp.where(first[None, :], -sin, 0.0)
        return jnp.stack([jnp.tile(a, (1, 2)) for a in (t0, t1, t2)]).astype(F32)

    inv_a = THETA ** (-jnp.arange(0, HD // 2, 2, dtype=F32) / (HD // 2))
    ar = (t // GRID_W).astype(F32)[:, None] * inv_a[None, :]
    ac = (t % GRID_W).astype(F32)[:, None] * inv_a[None, :]
    cos_a = jnp.concatenate([jnp.cos(ar), jnp.cos(ar), jnp.cos(ac), jnp.cos(ac)], axis=1)
    sin_a = jnp.concatenate([jnp.sin(ar), jnp.sin(ar), jnp.sin(ac), jnp.sin(ac)], axis=1)
    tab_a = build(cos_a, sin_a, (lane % 32) < 16)
    inv_b = THETA ** (-jnp.arange(0, HD, 2, dtype=F32) / HD)
    ab = t.astype(F32)[:, None] * inv_b[None, :]
    cos_b = jnp.concatenate([jnp.cos(ab), jnp.cos(ab)], axis=1)
    sin_b = jnp.concatenate([jnp.sin(ab), jnp.sin(ab)], axis=1)
    tab_b = build(cos_b, sin_b, lane < 32)
    return tab_a, tab_b


def _in_proj(x, modv, n1g, win, qg2, kg2, tab_a, tab_b):
    S = x.shape[0]
    tm = min(512, S)
    tk = min(TK_A, S)
    per = tk // tm
    ts = min(256, tm)

    def body(x_ref, mod_ref, g_ref, w_ref, qg_ref, kg_ref, ta_ref, tb_ref,
             h_ref, qar_ref, kar_ref, qa_ref, ka_ref, va_ref, qb_ref, kb_ref, vb_ref, ga_ref, gb_ref, qat_ref, kat_ref, vat_ref):
        seg = _seg_matrix(128, HD)

        def head_norm(z, g):
            ms = _seg_sum(z * z, seg) * (1.0 / HD)
            return (z * lax.rsqrt(ms + EPS)) * g

        subs = [slice(ts * u, ts * u + ts) for u in range(tm // ts)]
        hbs = []
        for rows in subs:
            xt = x_ref[rows, :]
            r = lax.rsqrt(jnp.mean(xt * xt, axis=-1, keepdims=True) + EPS)
            h = ((xt * r) * g_ref[...]) * (1.0 + mod_ref[1:2, :]) + mod_ref[0:1, :]
            hbs.append(h.astype(BF16))
            h_ref[rows, :] = hbs[-1]
        projs = [jnp.dot(hb, w_ref[...], preferred_element_type=F32) for hb in hbs]
        for rows, proj in zip(subs, projs):
            ta, tb = ta_ref[:, rows, :], tb_ref[:, rows, :]
            for p in range(4):
                z = proj[:, 128 * p:128 * p + 128]
                qar_ref[rows, 128 * p:128 * p + 128] = z.astype(BF16)
                qv = _rope(head_norm(z, qg_ref[...]), ta, 16) * QA_SCALE
                qa_ref[rows, 128 * p:128 * p + 128] = qv.astype(BF16)
                qat_ref[128 * p:128 * p + 128, rows] = qv.T.astype(BF16)
                zb = proj[:, 768 + 128 * p:768 + 128 * p + 128]
                qb_ref[rows, 128 * p:128 * p + 128] = (_rope(zb, tb, 32) * QA_SCALE).astype(BF16)
            z = proj[:, 512:640]
            kar_ref[rows, :] = z.astype(BF16)
            kv_ = _rope(head_norm(z, kg_ref[...]), ta, 16)
            ka_ref[rows, :] = kv_.astype(BF16)
            kat_ref[:, rows] = kv_.T.astype(BF16)
            va_ref[rows, :] = proj[:, 640:768].astype(BF16)
            vat_ref[:, rows] = proj[:, 640:768].T.astype(BF16)
            kb_ref[rows, :] = _rope(proj[:, 1280:1408], tb, 32).astype(BF16)
            vb_ref[rows, :] = proj[:, 1408:1536].astype(BF16)
            ga_ref[rows, :] = proj[:, 1536:2560].astype(BF16)
            gb_ref[rows, :] = proj[:, 2560:3584].astype(BF16)

    tab = pl.BlockSpec((3, tm, 128), lambda i: (0, i, 0))
    shapes = [(D, BF16), (BW, BF16), (128, BF16), (BW, BF16), (128, BF16), (128, BF16),
              (BW, BF16), (128, BF16), (128, BF16), (D, BF16), (D, BF16)]
    return pl.pallas_call(
        body, grid=(S // tm,),
        in_specs=[_rows(tm, D), _const((6, D)), _const((1, D)), _const((D, INW)), _const((1, 128)), _const((1, 128)), tab, tab],
        out_specs=[_rows(tm, w) for w, _ in shapes] + [pl.BlockSpec((BW, tm), lambda i: (0, i))]
        + [pl.BlockSpec((None, 128, tm), lambda i: (i // per, 0, i % per))] * 2,
        out_shape=[jax.ShapeDtypeStruct((S, w), dt) for w, dt in shapes] + [jax.ShapeDtypeStruct((BW, S), BF16)]
        + [jax.ShapeDtypeStruct((S // tk, 128, tk), BF16)] * 2,
        compiler_params=_params(1), name="in_proj",
    )(x, modv, n1g, win, qg2, kg2, tab_a, tab_b)


def _exchange_gather(block_refs, out_refs, send_sems, recv_sems, local_sems):
    x, y, c = _me()
    me = 4 * x + 2 * y + c

    def copies():
        own, out, arrive = [], [], []
        for a, (blk, dst) in enumerate(zip(block_refs, out_refs)):
            own.append(pltpu.make_async_copy(blk, dst.at[me], local_sems.at[a]))
            for k in range(1, N_DEV):
                sems = dict(send_sem=send_sems.at[a, k - 1], recv_sem=recv_sems.at[a, k - 1], device_id=_peer(k), device_id_type=MESH)
                out.append(pltpu.make_async_remote_copy(blk, dst.at[me], **sems))
                arrive.append(pltpu.make_async_remote_copy(blk, dst.at[me ^ k], **sems))
        return own, out, arrive

    def start():
        own, out, _ = copies()
        for cp in own + out:
            cp.start()

    def finish():
        own, out, arrive = copies()
        for cp in arrive:
            cp.wait_recv()
        for cp in out:
            cp.wait_send()
        for cp in own:
            cp.wait()

    return start, finish


def _exchange_scatter(chunk_refs, recv_refs, send_sems, recv_sems):
    x, y, c = _me()
    me = 4 * x + 2 * y + c

    def copies():
        return [pltpu.make_async_remote_copy(src.at[me ^ k], dst.at[k - 1], send_sems.at[a, k - 1], recv_sems.at[a, k - 1],
                                             device_id=_peer(k), device_id_type=MESH)
                for a, (src, dst) in enumerate(zip(chunk_refs, recv_refs)) for k in range(1, N_DEV)]

    def start():
        for cp in copies():
            cp.start()

    def finish():
        cps = copies()
        for cp in cps:
            cp.wait_recv()
        for cp in cps:
            cp.wait_send()

    return start, finish


def _exchange_sems(n):
    return [pltpu.SemaphoreType.DMA((n, N_DEV - 1)), pltpu.SemaphoreType.DMA((n, N_DEV - 1))]


def _attn_a_fwd(qt, k, vt3, shards):
    S = qt.shape[1]
    tq = min(512, S)
    nq = S // tq
    per = max(1, min(TK_A_FWD, S) // vt3.shape[2])
    tk = per * vt3.shape[2]
    nk = S // tk
    ONES = 16
    AHEAD = 2
    ns = len(shards)

    def body(q_ref, k_ref, vt_ref, *rest):
        w_hbm, (o_ref, lse_ref), wall_hbm = rest[:ns], rest[ns:ns + 2], rest[ns + 2:2 * ns + 2]
        st_sc, send_sems, recv_sems, local_sems = rest[2 * ns + 2:]
        start, finish = _exchange_gather(w_hbm, wall_hbm, send_sems, recv_sems, local_sems)
        pl.when(pl.program_id(0) == 0)(start)
        row8 = lax.broadcasted_iota(jnp.int32, (NH, tq), 0)
        lse_all = jnp.zeros((NH, tq), F32)
        ones = jnp.ones((ONES, tk), BF16)
        for kv in range(2):
            qts = [q_ref[HD * (GRP * kv + t):HD * (GRP * kv + t) + HD, :] for t in range(GRP)]

            def keys(j, kv=kv):
                off = j * tk if isinstance(j, int) else pl.multiple_of(j * tk, tk)
                return k_ref[pl.ds(off, tk), :][:, HD * kv:HD * kv + HD]

            def scores(kj, t, qts=qts):
                return jnp.dot(kj, qts[t], preferred_element_type=F32)

            def step(j, carry, kv=kv, last=False):
                kj = keys(j)
                kn = None if last else keys(j + 1)
                vt = jnp.concatenate([vt_ref[per * j + u, HD * kv:HD * kv + HD, :] for u in range(per)], axis=1)
                v1 = jnp.concatenate([vt, ones], axis=0)
                sts = [st_sc[t] for t in range(AHEAD)]
                new = []
                for t in range(GRP):
                    m, acc = carry[2 * t], carry[2 * t + 1]
                    if t + AHEAD < GRP:
                        sts.append(scores(kj, t + AHEAD))
                    st = sts[t]
                    mn = jnp.maximum(m, jnp.max(st, axis=0, keepdims=True))
                    pt = jnp.exp2(st - mn)
                    if t + AHEAD >= GRP and not last:
                        st_sc[t + AHEAD - GRP] = scores(kn, t + AHEAD - GRP)
                    acc = jnp.exp2(m - mn) * acc + jnp.dot(v1, pt.astype(BF16), preferred_element_type=F32)
                    new += [mn, acc]
                return tuple(new)

            k0 = keys(0)
            for t in range(AHEAD):
                st_sc[t] = scores(k0, t)
            init = (jnp.full((1, tq), NEG, F32), jnp.zeros((HD + ONES, tq), F32)) * GRP
            res = step(nk - 1, lax.fori_loop(0, nk - 1, step, init), last=True)
            outs = []
            for t in range(GRP):
                m, acc = res[2 * t], res[2 * t + 1]
                l = acc[HD:HD + 1, :]
                outs.append((acc[:HD, :] / l).T)
                lse_all = jnp.where(row8 == GRP * kv + t, m + jnp.log2(l), lse_all)
            o_ref[:, 256 * kv:256 * kv + 256] = jnp.concatenate(outs, axis=1).astype(BF16)
        lse_ref[...] = lse_all
        pl.when(pl.program_id(0) == nq - 1)(finish)

    any_spec = pl.BlockSpec(memory_space=pl.ANY)
    return pl.pallas_call(
        body, grid=(nq,),
        in_specs=[pl.BlockSpec((BW, tq), lambda i: (0, i)), _const((S, 128)), _const(vt3.shape)] + [any_spec] * ns,
        out_specs=[_rows(tq, BW), pl.BlockSpec((NH, tq), lambda i: (0, i))] + [any_spec] * ns,
        out_shape=[jax.ShapeDtypeStruct((S, BW), BF16), jax.ShapeDtypeStruct((NH, S), F32)]
        + [jax.ShapeDtypeStruct((N_DEV,) + s.shape, s.dtype) for s in shards],
        scratch_shapes=[pltpu.VMEM((AHEAD, tk, tq), F32)] + _exchange_sems(ns) + [pltpu.SemaphoreType.DMA((ns,))],
        compiler_params=_params(1), name="attn_a_fwd",
    )(qt, k, vt3, *shards)


def _window_mask(i, tq, S):
    W = tq + 2 * WIN
    r = lax.broadcasted_iota(jnp.int32, (tq, W), 0)
    c = lax.broadcasted_iota(jnp.int32, (tq, W), 1)
    kpos = i * tq - WIN + c
    return (jnp.abs(c - WIN - r) <= WIN) & (kpos >= 0) & (kpos < S)


TQ_B = 256


def _attn_b_fwd(q, kp, vpt3, sink2):
    S = q.shape[0]
    tq = min(TQ_B, S)
    W = tq + 2 * WIN
    nc = vpt3.shape[0]
    ONES = 16

    def body(q_ref, k_ref, vt_ref, sink_ref, o_ref, lse_ref):
        i = pl.program_id(0)
        off = pl.multiple_of(i * tq, tq)
        r = lax.broadcasted_iota(jnp.int32, (W, tq), 1)
        c = lax.broadcasted_iota(jnp.int32, (W, tq), 0)
        kpos = i * tq - WIN + c
        valid = (jnp.abs(c - WIN - r) <= WIN) & (kpos >= 0) & (kpos < S)
        kw = k_ref[pl.ds(off, W), :]
        vt = jnp.concatenate([vt_ref[i + half] for half in range(W // tq)], axis=1)
        ones = jnp.ones((ONES, W), BF16)
        row8 = lax.broadcasted_iota(jnp.int32, (NH, tq), 0)
        lse_all = jnp.zeros((NH, tq), F32)
        qs = []
        for p in range(4):
            qp = q_ref[:, 128 * p:128 * p + 128]
            qs += [qp[:, :HD], qp[:, HD:]]
        khs = [kw[:, HD * kv:HD * kv + HD] for kv in range(2)]
        v1s = [jnp.concatenate([vt[HD * kv:HD * kv + HD, :], ones], axis=0) for kv in range(2)]

        def scores(h):
            return lax.dot_general(khs[h // GRP], qs[h], NT, preferred_element_type=F32)

        ss = [scores(0), scores(1)]
        outs = []
        for h in range(NH):
            if h + 2 < NH:
                ss.append(scores(h + 2))
            st = jnp.where(valid, ss[h], NEG)
            sk = sink_ref[:, h:h + 1]
            m = jnp.maximum(jnp.max(st, axis=0, keepdims=True), sk)
            acc = jnp.dot(v1s[h // GRP], jnp.exp2(st - m).astype(BF16), preferred_element_type=F32)
            l = acc[HD:HD + 1, :] + jnp.exp2(sk - m)
            outs.append((acc[:HD, :] / l).T)
            lse_all = jnp.where(row8 == h, m + jnp.log2(l), lse_all)
        for p in range(4):
            o_ref[:, 128 * p:128 * p + 128] = jnp.concatenate(outs[2 * p:2 * p + 2], axis=1).astype(BF16)
        lse_ref[...] = lse_all

    return pl.pallas_call(
        body, grid=(S // tq,),
        in_specs=[_rows(tq, BW), _const((S + 2 * WIN, 128)), _const((nc, 128, tq)), _const((1, NH))],
        out_specs=[_rows(tq, BW), pl.BlockSpec((NH, tq), lambda i: (0, i))],
        out_shape=[jax.ShapeDtypeStruct((S, BW), BF16), jax.ShapeDtypeStruct((NH, S), F32)],
        compiler_params=_params(1), name="attn_b_fwd",
    )(q, kp, vpt3, sink2)


def _merge_out(ya, yb, ga, gb, x, modv, wb, wout):
    S = x.shape[0]
    tm = min(256, S)

    def body(ya_ref, yb_ref, ga_ref, gb_ref, x_ref, mod_ref, wb_ref, wo_ref, x1_ref, mg_ref, ua_ref, ub_ref):
        ua = jnp.dot(ya_ref[...], wb_ref[0], preferred_element_type=F32)
        ub = jnp.dot(yb_ref[...], wb_ref[1], preferred_element_type=F32)
        merged = jax.nn.sigmoid(ga_ref[...].astype(F32)) * ua + jax.nn.sigmoid(gb_ref[...].astype(F32)) * ub
        mb = merged.astype(BF16)
        ua_ref[...] = ua.astype(BF16)
        ub_ref[...] = ub.astype(BF16)
        mg_ref[...] = mb
        x1_ref[...] = x_ref[...] + mod_ref[2:3, :] * jnp.dot(mb, wo_ref[...], preferred_element_type=F32)

    return pl.pallas_call(
        body, grid=(S // tm,),
        in_specs=[_rows(tm, BW), _rows(tm, BW), _rows(tm, D), _rows(tm, D), _rows(tm, D), _const((6, D)),
                  _const((2, BW, D)), _const((D, D))],
        out_specs=[_rows(tm, D)] * 4,
        out_shape=[jax.ShapeDtypeStruct((S, D), F32)] + [jax.ShapeDtypeStruct((S, D), BF16)] * 3,
        compiler_params=_params(1), name="merge_out",
    )(ya, yb, ga, gb, x, modv, wb, wout)


def _mlp_fwd(x1, modv, n2g, wmi, wmo, fg, target):
    S = x1.shape[0]
    tm = min(512, S)
    tf = wmi.shape[2]
    nj = wmi.shape[0] // MLP_SHARDS

    def body(x1_ref, mod_ref, g_ref, wi_ref, wo_ref, fg_ref, t_ref, h2_ref, hp_ref, dx2_ref, st_ref, acc_ref):
        i, j = pl.program_id(0), pl.program_id(1)

        @pl.when(j == 0)
        def _():
            xt = x1_ref[...]
            r = lax.rsqrt(jnp.mean(xt * xt, axis=-1, keepdims=True) + EPS)
            h2 = ((xt * r) * g_ref[...]) * (1.0 + mod_ref[4:5, :]) + mod_ref[3:4, :]
            h2_ref[...] = h2.astype(BF16)
            acc_ref[...] = jnp.zeros_like(acc_ref)

        @pl.when((i == 0) & (j == 0))
        def _():
            st_ref[...] = jnp.zeros_like(st_ref)

        out = None
        for u in range(MLP_SHARDS):
            hp = jnp.dot(h2_ref[...], wi_ref[u], preferred_element_type=F32)
            hp_ref[:, tf * u:tf * u + tf] = hp.astype(BF16)
            hid = jnp.square(jnp.maximum(hp, 0.0))
            part = jnp.dot(hid.astype(BF16), wo_ref[u], preferred_element_type=F32)
            out = part if out is None else out + part
        acc_ref[...] += out

        @pl.when(j == nj - 1)
        def _():
            x2 = x1_ref[...] + mod_ref[5:6, :] * acc_ref[...]
            r3 = lax.rsqrt(jnp.mean(x2 * x2, axis=-1, keepdims=True) + EPS)
            xn = x2 * r3
            err = xn * fg_ref[...] - t_ref[...]
            dy = err * (1.0 / D)
            gy = dy * fg_ref[...]
            dx2_ref[...] = r3 * (gy - xn * jnp.mean(gy * xn, axis=-1, keepdims=True))
            st_ref[0:1, :] += jnp.sum(dy * xn, axis=0, keepdims=True)
            st_ref[1:2, :] += jnp.sum(err * err, axis=0, keepdims=True) * (0.5 / D)

    return pl.pallas_call(
        body, grid=(S // tm, nj),
        in_specs=[pl.BlockSpec((tm, D), lambda i, j: (i, 0)), _const((6, D)), _const((1, D)),
                  pl.BlockSpec((MLP_SHARDS, D, tf), lambda i, j: (j, 0, 0)), pl.BlockSpec((MLP_SHARDS, tf, D), lambda i, j: (j, 0, 0)),
                  _const((1, D)), pl.BlockSpec((tm, D), lambda i, j: (i, 0))],
        out_specs=[pl.BlockSpec((tm, D), lambda i, j: (i, 0)), pl.BlockSpec((tm, MLP_SHARDS * tf), lambda i, j: (i, j)),
                   pl.BlockSpec((tm, D), lambda i, j: (i, 0)), _const((8, D))],
        out_shape=[jax.ShapeDtypeStruct((S, D), BF16), jax.ShapeDtypeStruct((S, wmi.shape[0] * tf), BF16),
                   jax.ShapeDtypeStruct((S, D), F32), jax.ShapeDtypeStruct((8, D), F32)],
        scratch_shapes=[pltpu.VMEM((tm, D), F32)],
        compiler_params=_params(2), name="mlp_fwd",
    )(x1, modv, n2g, wmi, wmo, fg, target)


def _mlp_bwd(dx2, x1, hp, modv, n2g, wmi, wmo):
    S = x1.shape[0]
    tm = min(512, S)
    tf = wmi.shape[2]
    nj = wmi.shape[0] // MLP_SHARDS

    def body(dx2_ref, x1_ref, hp_ref, mod_ref, g_ref, wi_ref, wo_ref, dhp_ref, dx1_ref, st_ref, dmo_ref, acc_ref):
        i, j = pl.program_id(0), pl.program_id(1)

        @pl.when(j == 0)
        def _():
            dmo_ref[...] = (mod_ref[5:6, :] * dx2_ref[...]).astype(BF16)
            acc_ref[...] = jnp.zeros_like(acc_ref)

        @pl.when((i == 0) & (j == 0))
        def _():
            st_ref[...] = jnp.zeros_like(st_ref)

        out = None
        for u in range(MLP_SHARDS):
            sl = slice(tf * u, tf * u + tf)
            dhid = lax.dot_general(dmo_ref[...], wo_ref[u], NT, preferred_element_type=F32)
            dhp = (dhid * (2.0 * jnp.maximum(hp_ref[:, sl].astype(F32), 0.0))).astype(BF16)
            dhp_ref[:, sl] = dhp
            part = lax.dot_general(dhp, wi_ref[u], NT, preferred_element_type=F32)
            out = part if out is None else out + part
        acc_ref[...] += out

        @pl.when(j == nj - 1)
        def _():
            dh2 = acc_ref[...]
            xt = x1_ref[...]
            r = lax.rsqrt(jnp.mean(xt * xt, axis=-1, keepdims=True) + EPS)
            xn = xt * r
            st_ref[0:1, :] += jnp.sum(dh2, axis=0, keepdims=True)
            st_ref[1:2, :] += jnp.sum(dh2 * xn, axis=0, keepdims=True)
            dxn = dh2 * (g_ref[...] * (1.0 + mod_ref[4:5, :]))
            dx1_ref[...] = dx2_ref[...] + r * (dxn - xn * jnp.mean(dxn * xn, axis=-1, keepdims=True))

    return pl.pallas_call(
        body, grid=(S // tm, nj),
        in_specs=[pl.BlockSpec((tm, D), lambda i, j: (i, 0)), pl.BlockSpec((tm, D), lambda i, j: (i, 0)),
                  pl.BlockSpec((tm, MLP_SHARDS * tf), lambda i, j: (i, j)), _const((6, D)), _const((1, D)),
                  pl.BlockSpec((MLP_SHARDS, D, tf), lambda i, j: (j, 0, 0)), pl.BlockSpec((MLP_SHARDS, tf, D), lambda i, j: (j, 0, 0))],
        out_specs=[pl.BlockSpec((tm, MLP_SHARDS * tf), lambda i, j: (i, j)), pl.BlockSpec((tm, D), lambda i, j: (i, 0)), _const((8, D))],
        out_shape=[jax.ShapeDtypeStruct((S, wmi.shape[0] * tf), BF16), jax.ShapeDtypeStruct((S, D), F32),
                   jax.ShapeDtypeStruct((8, D), F32)],
        scratch_shapes=[pltpu.VMEM((tm, D), BF16), pltpu.VMEM((tm, D), F32)],
        compiler_params=_params(2), name="mlp_bwd",
    )(dx2, x1, hp, modv, n2g, wmi, wmo)


def _tn_matmul(a, b, tk, tn, name, relu_sq=False, dev_major=False):
    S, K = a.shape
    N = b.shape[1]
    ts = min(1024, S)
    ns = S // ts

    def body(a_ref, b_ref, o_ref):
        @pl.when(pl.program_id(2) == 0)
        def _():
            o_ref[...] = jnp.zeros_like(o_ref)

        at = a_ref[...]
        if relu_sq:
            at = jnp.square(jnp.maximum(at.astype(F32), 0.0)).astype(BF16)
        o_ref[...] += lax.dot_general(at, b_ref[...].astype(BF16), TN, preferred_element_type=F32)

    if dev_major:
        out_spec = pl.BlockSpec((None, tk, tn), lambda k, n, s: (n, k, 0))
        out_shape = jax.ShapeDtypeStruct((N // tn, K, tn), F32)
    else:
        out_spec = pl.BlockSpec((tk, tn), lambda k, n, s: (k, n))
        out_shape = jax.ShapeDtypeStruct((K, N), F32)
    return pl.pallas_call(
        body, grid=(K // tk, N // tn, ns),
        in_specs=[pl.BlockSpec((ts, tk), lambda k, n, s: (s, k)), pl.BlockSpec((ts, tn), lambda k, n, s: (s, n))],
        out_specs=out_spec, out_shape=out_shape,
        compiler_params=_params(3), name=name,
    )(a, b)


def _scale_gate(m, w, g, row, name):
    K = m.shape[0]
    tk = min(512, K)

    def body(m_ref, w_ref, mod_ref, dw_ref, dg_ref):
        @pl.when(pl.program_id(0) == 0)
        def _():
            dg_ref[...] = jnp.zeros_like(dg_ref)

        mt = m_ref[...]
        dw_ref[...] = mt * mod_ref[row:row + 1, :]
        dg_ref[0:1, :] += jnp.sum(mt * w_ref[...].astype(F32), axis=0, keepdims=True)

    return pl.pallas_call(
        body, grid=(K // tk,),
        in_specs=[_rows(tk, D), _rows(tk, D), _const((6, D))],
        out_specs=[_rows(tk, D), _const((8, D))],
        out_shape=[jax.ShapeDtypeStruct((K, D), F32), jax.ShapeDtypeStruct((8, D), F32)],
        compiler_params=_params(1), name=name,
    )(m, w, g)


def _merge_bwd(dx1, modv, ga, gb, ua, ub, ya, yb, wb, wout):
    S = dx1.shape[0]
    tm = min(512, S)

    def body(dx1_ref, mod_ref, ga_ref, gb_ref, ua_ref, ub_ref, ya_ref, yb_ref, wb_ref, wo_ref,
             dua_ref, dub_ref, dga_ref, dgb_ref, dya_ref, dyb_ref, dla_ref, dlb_ref, dyat_ref):
        dao = (mod_ref[2:3, :] * dx1_ref[...]).astype(BF16)
        dm = lax.dot_general(dao, wo_ref[...], NT, preferred_element_type=F32)
        r = lax.broadcasted_iota(jnp.int32, (BW, NH), 0) // HD
        c = lax.broadcasted_iota(jnp.int32, (BW, NH), 1)
        head_of = (r == c).astype(BF16)
        for br, (g_ref, u_ref, y_ref, du_ref, dg_ref, dy_ref, dl_ref) in enumerate((
                (ga_ref, ua_ref, ya_ref, dua_ref, dga_ref, dya_ref, dla_ref),
                (gb_ref, ub_ref, yb_ref, dub_ref, dgb_ref, dyb_ref, dlb_ref))):
            sg = jax.nn.sigmoid(g_ref[...].astype(F32))
            du = (dm * sg).astype(BF16)
            du_ref[...] = du
            dg_ref[...] = (dm * u_ref[...].astype(F32) * sg * (1.0 - sg)).astype(BF16)
            dy = lax.dot_general(du, wb_ref[br], NT, preferred_element_type=F32)
            dyb16 = dy.astype(BF16)
            dy_ref[...] = dyb16
            if br == 0:
                dyat_ref[...] = dy.T.astype(BF16)
            prod = dyb16.astype(F32) * y_ref[...].astype(F32)
            hi = prod.astype(BF16)
            lo = (prod - hi.astype(F32)).astype(BF16)
            dl_ref[...] = (jnp.dot(hi, head_of, preferred_element_type=F32) + jnp.dot(lo, head_of, preferred_element_type=F32))

    return pl.pallas_call(
        body, grid=(S // tm,),
        in_specs=[_rows(tm, D), _const((6, D)), _rows(tm, D), _rows(tm, D), _rows(tm, D), _rows(tm, D),
                  _rows(tm, BW), _rows(tm, BW), _const((2, BW, D)), _const((D, D))],
        out_specs=[_rows(tm, D)] * 4 + [_rows(tm, BW)] * 2 + [_rows(tm, NH)] * 2 + [pl.BlockSpec((BW, tm), lambda i: (0, i))],
        out_shape=[jax.ShapeDtypeStruct((S, D), BF16)] * 4 + [jax.ShapeDtypeStruct((S, BW), BF16)] * 2
        + [jax.ShapeDtypeStruct((S, NH), F32)] * 2 + [jax.ShapeDtypeStruct((BW, S), BF16)],
        compiler_params=_params(1), name="merge_bwd",
    )(dx1, modv, ga, gb, ua, ub, ya, yb, wb, wout)


def _attn_a_bwd(q, qt, kt3, vt3, do, dot_, lse, delta, chunks):
    S = q.shape[0]
    tq = min(1024, S)
    nk, _, tk = kt3.shape
    nq = S // tq
    nc = len(chunks)

    def body(q_ref, qt_ref, do_ref, dot_ref, lse_ref, dl_ref, kt_ref, vt_ref, *rest):
        g_hbm, (dq_ref, dk_hbm, dv_hbm), recv_hbm = rest[:nc], rest[nc:nc + 3], rest[nc + 3:2 * nc + 3]
        dk_sc, dv_sc, sem, send_sems, recv_sems = rest[2 * nc + 3:]
        i = pl.program_id(0)
        start, finish = _exchange_scatter(g_hbm, recv_hbm, send_sems, recv_sems)

        @pl.when(i == 0)
        def _():
            start()
            dk_sc[...] = jnp.zeros_like(dk_sc)
            dv_sc[...] = jnp.zeros_like(dv_sc)

        for kv in range(2):
            qg = q_ref[:, 256 * kv:256 * kv + 256]
            dog = do_ref[:, 256 * kv:256 * kv + 256]
            heads = []
            for t in range(GRP):
                h = GRP * kv + t
                heads.append((qg[:, HD * t:HD * t + HD], dog[:, HD * t:HD * t + HD],
                              qt_ref[HD * h:HD * h + HD, :], dot_ref[HD * h:HD * h + HD, :],
                              lse_ref[:, h:h + 1], dl_ref[:, h:h + 1]))

            q2 = [jnp.concatenate([heads[2 * u][0], heads[2 * u + 1][0]], axis=0) for u in range(GRP // 2)]
            do2 = [jnp.concatenate([heads[2 * u][1], heads[2 * u + 1][1]], axis=0) for u in range(GRP // 2)]

            def step(j, carry, kv=kv, heads=heads, q2=q2, do2=do2):
                kjt = kt_ref[j, HD * kv:HD * kv + HD, :]
                vjt = vt_ref[j, HD * kv:HD * kv + HD, :]
                dkt = jnp.zeros((HD, tk), F32)
                dvt = jnp.zeros((HD, tk), F32)
                new = []

                def logits(u):
                    return (jnp.dot(q2[u], kjt, preferred_element_type=F32), jnp.dot(do2[u], vjt, preferred_element_type=F32))

                sd = [logits(0)]
                for t, (qh, doh, qth, doth, lse_h, dl_h) in enumerate(heads):
                    if t == 0:
                        sd.append(logits(1))
                    rows = slice(tq * (t % 2), tq * (t % 2) + tq)
                    s, dp = sd[t // 2][0][rows, :], sd[t // 2][1][rows, :]
                    pm = jnp.exp2(s - lse_h)
                    ds = (pm * (dp - dl_h)).astype(BF16)
                    dvt = dvt + jnp.dot(doth, pm.astype(BF16), preferred_element_type=F32)
                    dkt = dkt + jnp.dot(qth, ds, preferred_element_type=F32)
                    new.append(carry[t] + lax.dot_general(kjt, ds, NT, preferred_element_type=F32))
                dk_sc[j, HD * kv:HD * kv + HD, :] += dkt
                dv_sc[j, HD * kv:HD * kv + HD, :] += dvt
                return tuple(new)

            res = lax.fori_loop(0, nk, step, (jnp.zeros((HD, tq), F32),) * GRP)
            for t in range(GRP):
                dq_ref[HD * (GRP * kv + t):HD * (GRP * kv + t) + HD, :] = res[t]

        @pl.when(i == nq - 1)
        def _():
            c1 = pltpu.make_async_copy(dk_sc, dk_hbm, sem.at[0])
            c2 = pltpu.make_async_copy(dv_sc, dv_hbm, sem.at[1])
            c1.start()
            c2.start()
            c1.wait()
            c2.wait()
            finish()

    any_spec = pl.BlockSpec(memory_space=pl.ANY)
    cols = pl.BlockSpec((BW, tq), lambda i: (0, i))
    return pl.pallas_call(
        body, grid=(nq,),
        in_specs=[_rows(tq, BW), cols, _rows(tq, BW), cols, _rows(tq, NH), _rows(tq, NH), _const((nk, 128, tk)),
                  _const((nk, 128, tk))] + [any_spec] * nc,
        out_specs=[cols, any_spec, any_spec] + [any_spec] * nc,
        out_shape=[jax.ShapeDtypeStruct((BW, S), F32), jax.ShapeDtypeStruct((nk, 128, tk), F32),
                   jax.ShapeDtypeStruct((nk, 128, tk), F32)]
        + [jax.ShapeDtypeStruct((N_DEV - 1,) + c.shape[1:], c.dtype) for c in chunks],
        scratch_shapes=[pltpu.VMEM((nk, 128, tk), F32), pltpu.VMEM((nk, 128, tk), F32), pltpu.SemaphoreType.DMA((2,))]
        + _exchange_sems(nc),
        compiler_params=_params(1), name="attn_a_bwd",
    )(q, qt, do, dot_, lse, delta, kt3, vt3, *chunks)


def _attn_b_bwd(q, kp, vp, sink2, do, lse, delta):
    S = q.shape[0]
    tq = min(TQ_B, S)
    W = tq + 2 * WIN
    nq = S // tq
    nc = (S + 2 * WIN) // tq

    def body(q_ref, k_ref, v_ref, sink_ref, do_ref, lse_ref, dl_ref, dq_ref, dk_hbm, dv_hbm, ds_ref, dk_sc, dv_sc, sem):
        i = pl.program_id(0)

        @pl.when(i == 0)
        def _():
            dk_sc[...] = jnp.zeros_like(dk_sc)
            dv_sc[...] = jnp.zeros_like(dv_sc)
            ds_ref[...] = jnp.zeros_like(ds_ref)

        off = pl.multiple_of(i * tq, tq)
        valid = _window_mask(i, tq, S)
        kw = k_ref[pl.ds(off, W), :]
        vw = v_ref[pl.ds(off, W), :]
        lse_i = lse_ref[...]
        dl_i = dl_ref[...]
        qa = q_ref[...]
        doa = do_ref[...]
        qt = qa.astype(F32).T.astype(BF16)
        dot_ = doa.astype(F32).T.astype(BF16)
        khs = [kw[:, HD * kv:HD * kv + HD] for kv in range(2)]
        vhs = [vw[:, HD * kv:HD * kv + HD] for kv in range(2)]

        def logits(h):
            return (lax.dot_general(qa[:, HD * h:HD * h + HD], khs[h // GRP], NT, preferred_element_type=F32),
                    lax.dot_general(doa[:, HD * h:HD * h + HD], vhs[h // GRP], NT, preferred_element_type=F32))

        sd = [logits(0)]
        dqs = []
        dkt = [jnp.zeros((HD, W), F32), jnp.zeros((HD, W), F32)]
        dvt = [jnp.zeros((HD, W), F32), jnp.zeros((HD, W), F32)]
        for h in range(NH):
            kv = h // GRP
            if h + 1 < NH:
                sd.append(logits(h + 1))
            s, dp = sd[h]
            pm = jnp.exp2(jnp.where(valid, s, NEG) - lse_i[:, h:h + 1])
            ds = (pm * (dp - dl_i[:, h:h + 1])).astype(BF16)
            dvt[kv] = dvt[kv] + jnp.dot(dot_[HD * h:HD * h + HD, :], pm.astype(BF16), preferred_element_type=F32)
            dkt[kv] = dkt[kv] + jnp.dot(qt[HD * h:HD * h + HD, :], ds, preferred_element_type=F32)
            dqs.append(jnp.dot(ds, khs[kv], preferred_element_type=F32))
        for p in range(4):
            dq_ref[:, 128 * p:128 * p + 128] = jnp.concatenate(dqs[2 * p:2 * p + 2], axis=1)
        for half in range(W // tq):
            dk_sc[i + half] += jnp.concatenate([d[:, tq * half:tq * half + tq] for d in dkt], axis=0)
            dv_sc[i + half] += jnp.concatenate([d[:, tq * half:tq * half + tq] for d in dvt], axis=0)
        psd = jnp.exp2(sink_ref[...] - lse_i) * dl_i
        r = lax.broadcasted_iota(jnp.int32, (NH, 128), 0)
        c = lax.broadcasted_iota(jnp.int32, (NH, 128), 1)
        row = jnp.dot(jnp.sum(psd, axis=0, keepdims=True), (r == c).astype(F32),
                      preferred_element_type=F32, precision=lax.Precision.HIGHEST)
        ds_ref[...] -= jnp.broadcast_to(row, (8, 128))

        @pl.when(i == nq - 1)
        def _():
            c1 = pltpu.make_async_copy(dk_sc, dk_hbm, sem.at[0])
            c2 = pltpu.make_async_copy(dv_sc, dv_hbm, sem.at[1])
            c1.start()
            c2.start()
            c1.wait()
            c2.wait()

    any_spec = pl.BlockSpec(memory_space=pl.ANY)
    return pl.pallas_call(
        body, grid=(nq,),
        in_specs=[_rows(tq, BW), _const((S + 2 * WIN, 128)), _const((S + 2 * WIN, 128)), _const((1, NH)),
                  _rows(tq, BW), _rows(tq, NH), _rows(tq, NH)],
        out_specs=[_rows(tq, BW), any_spec, any_spec, _const((8, 128))],
        out_shape=[jax.ShapeDtypeStruct((S, BW), F32), jax.ShapeDtypeStruct((nc, 128, tq), F32),
                   jax.ShapeDtypeStruct((nc, 128, tq), F32), jax.ShapeDtypeStruct((8, 128), F32)],
        scratch_shapes=[pltpu.VMEM((nc, 128, tq), F32), pltpu.VMEM((nc, 128, tq), F32), pltpu.SemaphoreType.DMA((2,))],
        compiler_params=_params(1), name="attn_b_bwd",
    )(q, kp, vp, sink2, do, lse, delta)


def _qk_bwd(dqa_t, dka_t3, dva_t3, dqb, dkb, dvb, qar, kar, qg2, kg2, tab_a, tab_b, dga, dgb):
    S = dqb.shape[0]
    tm = min(256, S)
    per = dka_t3.shape[2] // tm

    def body(dqa_ref, dka_ref, dva_ref, dqb_ref, dkb_ref, dvb_ref, qar_ref, kar_ref, qg_ref, kg_ref, ta_ref, tb_ref,
             dga_ref, dgb_ref, dp_ref, st_ref):
        @pl.when(pl.program_id(0) == 0)
        def _():
            st_ref[...] = jnp.zeros_like(st_ref)

        seg = _seg_matrix(128, HD)

        def norm_bwd(dz_rot, raw, g):
            dzn = _rope_t(dz_rot, ta_ref, 16)
            raw = raw.astype(F32)
            rr = lax.rsqrt(_seg_sum(raw * raw, seg) * (1.0 / HD) + EPS)
            zhat = raw * rr
            dzh = dzn * g
            draw = rr * (dzh - zhat * (_seg_sum(dzh * zhat, seg) * (1.0 / HD)))
            return draw, jnp.sum(dzn * zhat, axis=0, keepdims=True)

        gq = jnp.zeros((1, 128), F32)
        for p in range(4):
            sl = slice(128 * p, 128 * p + 128)
            draw, gsum = norm_bwd(dqa_ref[sl, :].T * 0.125, qar_ref[:, sl], qg_ref[...])
            gq = gq + gsum
            dp_ref[:, sl] = draw.astype(BF16)
            dp_ref[:, 768 + 128 * p:768 + 128 * p + 128] = _rope_t(dqb_ref[:, sl] * 0.125, tb_ref, 32).astype(BF16)
        draw, gk = norm_bwd(dka_ref[...].T * LN2, kar_ref[...], kg_ref[...])
        dp_ref[:, 512:640] = draw.astype(BF16)
        dp_ref[:, 640:768] = dva_ref[...].T.astype(BF16)
        dp_ref[:, 1280:1408] = _rope_t(dkb_ref[...] * LN2, tb_ref, 32).astype(BF16)
        dp_ref[:, 1408:1536] = dvb_ref[...].astype(BF16)
        dp_ref[:, 1536:2560] = dga_ref[...]
        dp_ref[:, 2560:3584] = dgb_ref[...]
        st_ref[0:1, :] += gq
        st_ref[1:2, :] += gk

    tab = pl.BlockSpec((3, tm, 128), lambda i: (0, i, 0))
    chunk_t = pl.BlockSpec((None, 128, tm), lambda i: (i // per, 0, i % per))
    return pl.pallas_call(
        body, grid=(S // tm,),
        in_specs=[pl.BlockSpec((BW, tm), lambda i: (0, i)), chunk_t, chunk_t, _rows(tm, BW), _rows(tm, 128), _rows(tm, 128),
                  _rows(tm, BW), _rows(tm, 128), _const((1, 128)), _const((1, 128)), tab, tab, _rows(tm, D), _rows(tm, D)],
        out_specs=[_rows(tm, INW), _const((8, 128))],
        out_shape=[jax.ShapeDtypeStruct((S, INW), BF16), jax.ShapeDtypeStruct((8, 128), F32)],
        compiler_params=_params(1), name="qk_bwd",
    )(dqa_t, dka_t3, dva_t3, dqb, dkb, dvb, qar, kar, qg2, kg2, tab_a, tab_b, dga, dgb)


def _in_bwd(dproj, win, x, dx1, modv, n1g, chunks):
    S = x.shape[0]
    tm = min(512, S)
    n = S // tm
    ts = min(256, tm)

    def body(dp_ref, w_ref, x_ref, dx1_ref, mod_ref, g_ref, c_hbm, gx_ref, st_ref, recv_hbm, send_sems, recv_sems):
        start, finish = _exchange_scatter([c_hbm], [recv_hbm], send_sems, recv_sems)

        @pl.when(pl.program_id(0) == 0)
        def _():
            start()
            st_ref[...] = jnp.zeros_like(st_ref)

        subs = [slice(ts * u, ts * u + ts) for u in range(tm // ts)]
        dhs = [lax.dot_general(dp_ref[rows, :], w_ref[...], NT, preferred_element_type=F32) for rows in subs]
        for rows, dh in zip(subs, dhs):
            xt = x_ref[rows, :]
            r = lax.rsqrt(jnp.mean(xt * xt, axis=-1, keepdims=True) + EPS)
            xn = xt * r
            st_ref[0:1, :] += jnp.sum(dh, axis=0, keepdims=True)
            st_ref[1:2, :] += jnp.sum(dh * xn, axis=0, keepdims=True)
            dxn = dh * (g_ref[...] * (1.0 + mod_ref[1:2, :]))
            gx_ref[rows, :] = dx1_ref[rows, :] + r * (dxn - xn * jnp.mean(dxn * xn, axis=-1, keepdims=True))
        pl.when(pl.program_id(0) == n - 1)(finish)

    any_spec = pl.BlockSpec(memory_space=pl.ANY)
    return pl.pallas_call(
        body, grid=(n,),
        in_specs=[_rows(tm, INW), _const((D, INW)), _rows(tm, D), _rows(tm, D), _const((6, D)), _const((1, D)), any_spec],
        out_specs=[_rows(tm, D), _const((8, D)), any_spec],
        out_shape=[jax.ShapeDtypeStruct((S, D), F32), jax.ShapeDtypeStruct((8, D), F32),
                   jax.ShapeDtypeStruct((N_DEV - 1,) + chunks.shape[1:], chunks.dtype)],
        scratch_shapes=_exchange_sems(1),
        compiler_params=_params(1), name="in_bwd",
    )(dproj, win, x, dx1, modv, n1g, chunks)


def _pack_small(st1, st2, stf, dg1, dg2, stqk, dsink, modv, n1g, n2g):
    def body(st1_ref, st2_ref, stf_ref, dg1_ref, dg2_ref, qk_ref, ds_ref, mod_ref, g1_ref, g2_ref, o_ref):
        a1, b1 = st1_ref[0:1, :], st1_ref[1:2, :]
        a2, b2 = st2_ref[0:1, :], st2_ref[1:2, :]
        r = lax.broadcasted_iota(jnp.int32, (128, D), 0)
        c = lax.broadcasted_iota(jnp.int32, (128, D), 1)
        fold_q = (c == r % HD).astype(F32)
        fold_k = (c == HD + r % HD).astype(F32)
        keep = (c == r).astype(F32)

        def place(v, sel):
            return jnp.dot(v, sel, preferred_element_type=F32, precision=lax.Precision.HIGHEST)

        loss = jnp.sum(stf_ref[1:2, :], axis=1, keepdims=True)
        lane = lax.broadcasted_iota(jnp.int32, (1, D), 1)
        rows = [a1, g1_ref[...] * b1, dg1_ref[0:1, :], a2, g2_ref[...] * b2, dg2_ref[0:1, :],
                (1.0 + mod_ref[1:2, :]) * b1, (1.0 + mod_ref[4:5, :]) * b2, stf_ref[0:1, :],
                place(qk_ref[0:1, :], fold_q) + place(qk_ref[1:2, :], fold_k),
                place(ds_ref[0:1, :], keep),
                jnp.where(lane == 0, loss, 0.0)]
        rows += [jnp.zeros((1, D), F32)] * (SMALL_ROWS - len(rows))
        for n, v in enumerate(rows):
            o_ref[n:n + 1, :] = v

    return pl.pallas_call(
        body, out_shape=jax.ShapeDtypeStruct((SMALL_ROWS, D), F32),
        compiler_params=pltpu.CompilerParams(vmem_limit_bytes=V7X_VMEM_LIMIT), name="pack_small",
    )(st1, st2, stf, dg1, dg2, stqk, dsink, modv, n1g, n2g)


def _wada_grad(silu_all, dmod_cols):
    def body(a_ref, b_ref, o_ref):
        o_ref[...] = lax.dot_general(a_ref[...], b_ref[...], TN, preferred_element_type=F32, precision=lax.Precision.HIGHEST)

    return pl.pallas_call(
        body, out_shape=jax.ShapeDtypeStruct((D, dmod_cols.shape[1]), F32),
        compiler_params=pltpu.CompilerParams(vmem_limit_bytes=V7X_VMEM_LIMIT), name="wada_grad",
    )(silu_all, dmod_cols)


def _adamw_sum(parts, w, m, v, name):
    R, C = w.shape
    tr = R if R <= 64 else next(t for t in (256, 128, 64, 32, 16, 8) if R % t == 0)
    n = len(parts)
    dyn = [idx for _, idx in parts if idx is not None and not isinstance(idx, int)]
    b1c = 1.0 - ADAM_B1 ** ADAM_STEP
    b2c = 1.0 - ADAM_B2 ** ADAM_STEP

    def body(*refs):
        refs = refs[len(dyn):]
        g = refs[0][...].astype(F32)
        for k in range(1, n):
            g = g + refs[k][...].astype(F32)
        w_ref, m_ref, v_ref, g_out, d_out, m_out, v_out = refs[n:]
        mn = ADAM_B1 * m_ref[...] + (1.0 - ADAM_B1) * g
        vn = ADAM_B2 * v_ref[...] + (1.0 - ADAM_B2) * jnp.square(g)
        g_out[...] = g
        m_out[...] = mn
        v_out[...] = vn
        d_out[...] = -ADAM_LR * ((mn / b1c) / (jnp.sqrt(vn / b2c) + ADAM_EPS) + ADAM_WD * w_ref[...])

    in_specs = []
    nd = 0
    for a, idx in parts:
        if idx is None:
            in_specs.append(pl.BlockSpec((tr, C), lambda i, *s: (i, 0)))
        elif isinstance(idx, int):
            in_specs.append(pl.BlockSpec((None, tr, C), lambda i, *s, idx=idx: (idx, i, 0)))
        else:
            in_specs.append(pl.BlockSpec((None, tr, C), lambda i, *s, nd=nd: (s[nd][0], i, 0)))
            nd += 1
    blk = pl.BlockSpec((tr, C), lambda i, *s: (i, 0))
    grid_spec = pltpu.PrefetchScalarGridSpec(
        num_scalar_prefetch=len(dyn), grid=(R // tr,), in_specs=in_specs + [blk] * 3, out_specs=[blk] * 4)
    return pl.pallas_call(
        body, grid_spec=grid_spec, out_shape=[jax.ShapeDtypeStruct((R, C), F32)] * 4,
        compiler_params=_params(1), name=name,
    )(*dyn, *[a for a, _ in parts], w, m, v)


def _me():
    return lax.axis_index("x"), lax.axis_index("y"), lax.axis_index("c")


def _peer(k):
    x, y, c = _me()
    return (x ^ ((k >> 2) & 1), y ^ ((k >> 1) & 1), c ^ (k & 1))


def _ada_exchange(c_row, w_ada, b_rows):
    NW = w_ada.shape[1]

    def body(c_ref, w_ref, b_ref, sall_ref, mod_ref, src_ref, mp_ref, send1, recv1, send2, recv2):
        x, y, c = _me()
        me = 4 * x + 2 * y + c
        cv = c_ref[...]
        src_ref[...] = jnp.broadcast_to(cv * jax.nn.sigmoid(cv), (8, D))
        mine = pl.ds(pl.multiple_of(me * 8, 8), 8)
        sall_ref[mine, :] = src_ref[...]
        sends = [pltpu.make_async_remote_copy(src_ref, sall_ref.at[mine, :], send1.at[k - 1], recv1.at[k - 1],
                                              device_id=_peer(k), device_id_type=MESH) for k in range(1, N_DEV)]
        for cp in sends:
            cp.start()
        for k in range(1, N_DEV):
            theirs = pl.ds(pl.multiple_of((me ^ k) * 8, 8), 8)
            pltpu.make_async_remote_copy(src_ref, sall_ref.at[theirs, :], send1.at[k - 1], recv1.at[k - 1],
                                         device_id=_peer(k), device_id_type=MESH).wait_recv()
        for cp in sends:
            cp.wait_send()
        mp_ref[...] = jnp.dot(sall_ref[...], w_ref[...], preferred_element_type=F32, precision=lax.Precision.HIGHEST)
        mod_ref[mine, :] = mp_ref[mine, :] + b_ref[mine, :]
        sends = []
        for k in range(1, N_DEV):
            theirs = pl.ds(pl.multiple_of((me ^ k) * 8, 8), 8)
            sends.append(pltpu.make_async_remote_copy(mp_ref.at[theirs, :], mod_ref.at[mine, :], send2.at[k - 1], recv2.at[k - 1],
                                                      device_id=_peer(k), device_id_type=MESH))
        for cp in sends:
            cp.start()
        for k in range(1, N_DEV):
            theirs = pl.ds(pl.multiple_of((me ^ k) * 8, 8), 8)
            pltpu.make_async_remote_copy(mp_ref.at[mine, :], mod_ref.at[theirs, :], send2.at[k - 1], recv2.at[k - 1],
                                         device_id=_peer(k), device_id_type=MESH).wait_recv()
            mod_ref[theirs, :] = mod_ref[theirs, :] + b_ref[theirs, :]
        for cp in sends:
            cp.wait_send()

    vm = pl.BlockSpec(memory_space=pltpu.VMEM)
    return pl.pallas_call(
        body, in_specs=[vm, vm, vm], out_specs=[vm, vm],
        out_shape=[jax.ShapeDtypeStruct((8 * N_DEV, D), F32), jax.ShapeDtypeStruct((8 * N_DEV, NW), F32)],
        scratch_shapes=[pltpu.VMEM((8, D), F32), pltpu.VMEM((8 * N_DEV, NW), F32)]
        + [pltpu.SemaphoreType.DMA((N_DEV - 1,))] * 4,
        compiler_params=pltpu.CompilerParams(vmem_limit_bytes=V7X_VMEM_LIMIT), name="ada_exchange",
    )(c_row, w_ada, b_rows)


def _weight_gather(shard):
    def body(x_ref, out_ref, send_sems, recv_sems, local_sem):
        x, y, c = _me()
        me, sibling = (x, y, c), (x, y, 1 - c)
        chips = [(1 - x, y), (x, 1 - y), (1 - x, 1 - y)]

        def slot(px, py, pc):
            return out_ref.at[4 * px + 2 * py + pc]

        def copy(k, block, to, src=None):
            return pltpu.make_async_remote_copy(
                src_ref=slot(*block) if src is None else src, dst_ref=slot(*block),
                send_sem=send_sems.at[k], recv_sem=recv_sems.at[k], device_id=to, device_id_type=MESH)

        mine = pltpu.make_async_copy(x_ref, slot(*me), local_sem)
        mine.start()
        first = [copy(0, me, sibling, src=x_ref)]
        first += [copy(1 + j, me, (*chip, c), src=x_ref) for j, chip in enumerate(chips)]
        for cp in first:
            cp.start()
        passed = [copy(4 + j, (*chip, c), sibling) for j, chip in enumerate(chips)]
        for j, chip in enumerate(chips):
            copy(1 + j, (*chip, c), me).wait_recv()
            passed[j].start()
        copy(0, sibling, me).wait_recv()
        for j, chip in enumerate(chips):
            copy(4 + j, (*chip, 1 - c), me).wait_recv()
        for cp in first + passed:
            cp.wait_send()
        mine.wait()

    any_spec = pl.BlockSpec(memory_space=pl.ANY)
    return pl.pallas_call(
        body, in_specs=[any_spec], out_specs=any_spec,
        out_shape=jax.ShapeDtypeStruct((N_DEV,) + shard.shape, shard.dtype),
        scratch_shapes=[pltpu.SemaphoreType.DMA((7,)), pltpu.SemaphoreType.DMA((7,)), pltpu.SemaphoreType.DMA],
        name="weight_gather",
    )(shard)


def _small_gather(block):
    def body(b_ref, out_ref, send_sems, recv_sems):
        x, y, c = _me()
        me = 4 * x + 2 * y + c
        out_ref[me] = b_ref[...]
        sends = [pltpu.make_async_remote_copy(b_ref, out_ref.at[me], send_sems.at[k - 1], recv_sems.at[k - 1],
                                              device_id=_peer(k), device_id_type=MESH) for k in range(1, N_DEV)]
        for cp in sends:
            cp.start()
        for k in range(1, N_DEV):
            pltpu.make_async_remote_copy(b_ref, out_ref.at[me ^ k], send_sems.at[k - 1], recv_sems.at[k - 1],
                                         device_id=_peer(k), device_id_type=MESH).wait_recv()
        for cp in sends:
            cp.wait_send()

    vm = pl.BlockSpec(memory_space=pltpu.VMEM)
    return pl.pallas_call(
        body, in_specs=[vm], out_specs=vm,
        out_shape=jax.ShapeDtypeStruct((N_DEV,) + block.shape, block.dtype),
        scratch_shapes=[pltpu.SemaphoreType.DMA((N_DEV - 1,)), pltpu.SemaphoreType.DMA((N_DEV - 1,))],
        name="small_gather",
    )(block)


def _pack_small_params(b_ada, n1, n2, fg, qn, kn, sink):
    z = jnp.zeros((SMALL_ROWS, D), F32)
    z = z.at[0:6].set(b_ada.reshape(6, D)).at[6].set(n1.reshape(D)).at[7].set(n2.reshape(D)).at[8].set(fg.reshape(D))
    z = z.at[9, 0:HD].set(qn.reshape(HD)).at[9, HD:2 * HD].set(kn.reshape(HD)).at[10, 0:NH].set(sink.reshape(NH))
    return z


def _unpack_small(p):
    return (p[0:6].reshape(1, 6 * D), p[6].reshape(1, D), p[9, 0:HD].reshape(1, HD), p[9, HD:2 * HD].reshape(1, HD),
            p[10, 0:NH].reshape(1, NH), p[7].reshape(1, D), p[8].reshape(D))


def kernel(x, c, w_ada, b_ada, norm1_g, w_in, q_norm_a, k_norm_a, sink_b, w_branch, w_out, norm2_g, w_mlp_in, w_mlp_out, final_g, loss_target, m_w_ada, m_b_ada, m_norm1_g, m_w_in, m_q_norm_a, m_k_norm_a, m_sink_b, m_w_branch, m_w_out, m_norm2_g, m_w_mlp_in, m_w_mlp_out, m_final_g, v_w_ada, v_b_ada, v_norm1_g, v_w_in, v_q_norm_a, v_k_norm_a, v_sink_b, v_w_branch, v_w_out, v_norm2_g, v_w_mlp_in, v_w_mlp_out, v_final_g):
    S = x.shape[1]
    xs = x.reshape(S, D)
    tgt = loss_target.reshape(S, D)
    ax, ay, ac = lax.axis_index("x"), lax.axis_index("y"), lax.axis_index("c")
    me = 4 * ax + 2 * ay + ac
    me1 = me.reshape(1).astype(jnp.int32)
    NW = w_ada.shape[2]
    NI = w_in.shape[2]

    silu64, mod64 = _ada_exchange(c.reshape(1, D), w_ada.reshape(D, NW),
                                  jnp.repeat(b_ada.reshape(N_DEV, NW), 8, axis=0))
    silu_all = silu64[0::8]
    modv = mod64[0::8].reshape(6, D)

    win = _weight_gather(w_in[0].astype(BF16)).transpose(1, 0, 2).reshape(D, INW)
    rest_shards = tuple(w[0].astype(BF16) for w in (w_branch, w_out, w_mlp_in, w_mlp_out))

    tab_a, tab_b = _rope_tables(S)
    qg2 = jnp.tile(q_norm_a.reshape(1, HD), (1, 2))
    kg2 = jnp.tile(k_norm_a.reshape(1, HD), (1, 2))
    n1g = norm1_g.reshape(1, D)
    n2g = norm2_g.reshape(1, D)
    fg = final_g.reshape(1, D)
    sink2 = sink_b.reshape(1, NH) * LOG2E

    h, qar, kar, qa, ka, va, qb, kb, vb, ga, gb, qa_t, ka_t3, va_t3 = _in_proj(xs, modv, n1g, win, qg2, kg2, tab_a, tab_b)
    ya, lse_at, wb, wout, wmi, wmo = _attn_a_fwd(qa_t, ka, va_t3, rest_shards)
    lse_a = lse_at.T
    wb = wb.transpose(1, 2, 0, 3).reshape(2, BW, D)
    wout = wout.reshape(D, D)
    pad = ((WIN, WIN), (0, 0))
    kbp, vbp = jnp.pad(kb, pad), jnp.pad(vb, pad)
    tb = min(TQ_B, S)
    yb, lse_bt = _attn_b_fwd(qb, kbp, vbp.reshape((S + 2 * WIN) // tb, tb, 128).transpose(0, 2, 1), sink2)
    lse_b = lse_bt.T
    x1, merged, ua, ub = _merge_out(ya, yb, ga, gb, xs, modv, wb, wout)
    h2, hp, dx2, stf = _mlp_fwd(x1, modv, n2g, wmi, wmo, fg, tgt)

    dhp, dx1, st2 = _mlp_bwd(dx2, x1, hp, modv, n2g, wmi, wmo)
    m2 = _tn_matmul(hp, dx2, 2048, D, "dw_mlp_out", relu_sq=True)
    g_wmo, dg2 = _scale_gate(m2, wmo.reshape(FF, D), modv, 5, "gate2_grad")
    g_wmi = _tn_matmul(h2, dhp, D, 512, "dw_mlp_in", dev_major=True)
    dua, dub, dga, dgb, dya, dyb, dl_a, dl_b, dya_t = _merge_bwd(dx1, modv, ga, gb, ua, ub, ya, yb, wb, wout)
    m1 = _tn_matmul(merged, dx1, D, D, "dw_out")
    g_wout, dg1 = _scale_gate(m1, wout, modv, 2, "gate1_grad")
    g_wb0 = _tn_matmul(ya, dua, BW, D, "dw_branch_a")
    g_wb1 = _tn_matmul(yb, dub, BW, D, "dw_branch_b")
    g_wb = jnp.stack([g_wb0, g_wb1]).reshape(2, BW, N_DEV, 128).transpose(2, 0, 1, 3).reshape(N_DEV, 2 * BW, 128)
    g_wout = g_wout.reshape(N_DEV, 128, D)
    g_wmo = g_wmo.reshape(N_DEV, 512, D)
    dqa_t, dka_t3, dva_t3, r_wb, r_wout, r_wmi, r_wmo = _attn_a_bwd(qa, qa_t, ka_t3, va_t3, dya, dya_t, lse_a, dl_a,
                                                                    (g_wb, g_wout, g_wmi, g_wmo))
    dqb, dkb_t, dvb_t, dsink = _attn_b_bwd(qb, kbp, vbp, sink2, dyb, lse_b, dl_b)
    dkb = dkb_t.transpose(0, 2, 1).reshape(S + 2 * WIN, 128)[WIN:WIN + S]
    dvb = dvb_t.transpose(0, 2, 1).reshape(S + 2 * WIN, 128)[WIN:WIN + S]
    dproj, stqk = _qk_bwd(dqa_t, dka_t3, dva_t3, dqb, dkb, dvb, qar, kar, qg2, kg2, tab_a, tab_b, dga, dgb)
    g_win = _tn_matmul(h, dproj, D, 896, "dw_in")
    g_win = g_win.reshape(D, N_DEV, NI).transpose(1, 0, 2)
    grad_x, st1, r_win = _in_bwd(dproj, win, xs, dx1, modv, n1g, g_win.astype(BF16))

    def adam(name, own, recv, w, m, v):
        shape = w.shape
        w2, m2_, v2 = (a.reshape(own.shape[1:]) for a in (w, m, v))
        outs = _adamw_sum([(own, me1)] + [(recv, k) for k in range(N_DEV - 1)], w2, m2_, v2, name)
        return [a.reshape(shape) for a in outs]

    o_win = adam("adamw_w_in", g_win, r_win, w_in, m_w_in, v_w_in)
    o_wb = adam("adamw_w_branch", g_wb, r_wb, w_branch, m_w_branch, v_w_branch)
    o_wout = adam("adamw_w_out", g_wout, r_wout, w_out, m_w_out, v_w_out)
    o_wmi = adam("adamw_w_mlp_in", g_wmi, r_wmi, w_mlp_in, m_w_mlp_in, v_w_mlp_in)
    o_wmo = adam("adamw_w_mlp_out", g_wmo, r_wmo, w_mlp_out, m_w_mlp_out, v_w_mlp_out)

    small = _pack_small(st1, st2, stf, dg1, dg2, stqk, dsink, modv, n1g, n2g)
    small_all = _small_gather(small)
    sw = _pack_small_params(b_ada, norm1_g, norm2_g, final_g, q_norm_a, k_norm_a, sink_b)
    sm = _pack_small_params(m_b_ada, m_norm1_g, m_norm2_g, m_final_g, m_q_norm_a, m_k_norm_a, m_sink_b)
    sv = _pack_small_params(v_b_ada, v_norm1_g, v_norm2_g, v_final_g, v_q_norm_a, v_k_norm_a, v_sink_b)
    sm_out = _adamw_sum([(small_all, k) for k in range(N_DEV)], sw, sm, sv, "adamw_small")
    loss = sm_out[0][11, 0]
    sm_out = [_unpack_small(a) for a in sm_out]

    dmod_all = small_all[:, 0:6, :].reshape(N_DEV, 6 * D)
    dmod_cols = lax.dynamic_slice_in_dim(dmod_all, me * NW, NW, axis=1)
    g_wada = _wada_grad(silu_all, dmod_cols)
    ada = _adamw_sum([(g_wada, None)], w_ada.reshape(D, NW), m_w_ada.reshape(D, NW), v_w_ada.reshape(D, NW), "adamw_ada")
    ada = [a.reshape(1, D, NW) for a in ada]

    def leaves(k):
        b_, n1_, qn_, kn_, sk_, n2_, fg_ = sm_out[k]
        return [ada[k], b_, n1_, o_win[k], qn_, kn_, sk_, o_wb[k], o_wout[k], n2_, o_wmi[k], o_wmo[k], fg_]

    return (loss, grad_x.reshape(1, S, D), *leaves(0), *leaves(1), *leaves(2), *leaves(3))
```

```python
import jax
import jax.numpy as jnp
from jax import lax
from jax.experimental import pallas as pl
from jax.experimental.pallas import tpu as pltpu

F32, BF16 = jnp.float32, jnp.bfloat16
MESH = pl.DeviceIdType.MESH

D = 1024
HD = 64
NH = 8
GRP = 4
BW = 512
FF = 4096
INW = 3584
GRID_W = 64
WIN = 128
THETA = 10000.0
EPS = 1e-6
NEG = -1e30
N_DEV = 8
LOG2E = 1.4426950408889634
LN2 = 0.6931471805599453
QA_SCALE = 0.125 * LOG2E
SMALL_ROWS = 16
MLP_SHARDS = 8
MLP_ROWS = 256
TK_A = 512
TK_A_FWD = 2048
V7X_VMEM_LIMIT = 56 * 1024 * 1024

ADAM_LR, ADAM_B1, ADAM_B2, ADAM_EPS, ADAM_WD, ADAM_STEP = 0.001, 0.9, 0.999, 1e-08, 0.01, 10

NT = (((1,), (1,)), ((), ()))
TN = (((0,), (0,)), ((), ()))


def _params(n_axes, vmem=V7X_VMEM_LIMIT):
    return pltpu.CompilerParams(dimension_semantics=("arbitrary",) * n_axes, vmem_limit_bytes=vmem)


def _const(shape):
    return pl.BlockSpec(shape, lambda *_: (0,) * len(shape))


def _rows(tm, width):
    return pl.BlockSpec((tm, width), lambda i, *_: (i, 0))


def _seg_matrix(n, seg):
    r = lax.broadcasted_iota(jnp.int32, (n, n), 0) // seg
    c = lax.broadcasted_iota(jnp.int32, (n, n), 1) // seg
    return (r == c).astype(BF16)


def _seg_sum(z, seg_mat):
    hi = z.astype(BF16)
    lo = (z - hi.astype(F32)).astype(BF16)
    return jnp.dot(hi, seg_mat, preferred_element_type=F32) + jnp.dot(lo, seg_mat, preferred_element_type=F32)


def _rope(z, t_ref, sh):
    return z * t_ref[0] + pltpu.roll(z, sh, 1) * t_ref[1] + pltpu.roll(z, 128 - sh, 1) * t_ref[2]


def _rope_t(dz, t_ref, sh):
    return dz * t_ref[0] + pltpu.roll(dz * t_ref[1], 128 - sh, 1) + pltpu.roll(dz * t_ref[2], sh, 1)


def _rope_tables(S):
    t = jnp.arange(S, dtype=jnp.int32)[:, None]
    lane = jnp.arange(128, dtype=jnp.int32)[None, :] % HD

    def build(ang, first):
        cos, sin = jnp.cos(ang), jnp.sin(ang)
        return jnp.stack([cos, jnp.where(first, 0.0, sin), jnp.where(first, -sin, 0.0)]).astype(F32)

    inv_a = (THETA ** (-jnp.arange(0, HD // 2, 2, dtype=F32) / (HD // 2)))[lane % 16]
    pos_a = jnp.where(lane < HD // 2, t // GRID_W, t % GRID_W).astype(F32)
    tab_a = build(pos_a * inv_a, (lane % 32) < 16)
    inv_b = (THETA ** (-jnp.arange(0, HD, 2, dtype=F32) / HD))[lane % 32]
    tab_b = build(t.astype(F32) * inv_b, lane < 32)
    return tab_a, tab_b


def _in_proj(x, modv, n1g, win, qg2, kg2, tab_a, tab_b):
    S = x.shape[0]
    tm = min(512, S)
    tk = min(TK_A, S)
    per = tk // tm
    ts = min(256, tm)

    def body(x_ref, mod_ref, g_ref, w_ref, qg_ref, kg_ref, ta_ref, tb_ref,
             h_ref, qar_ref, kar_ref, qa_ref, ka_ref, va_ref, qb_ref, kb_ref, vb_ref, ga_ref, gb_ref, qat_ref, kat_ref, vat_ref):
        seg = _seg_matrix(128, HD)

        def head_norm(z, g):
            ms = _seg_sum(z * z, seg) * (1.0 / HD)
            return (z * lax.rsqrt(ms + EPS)) * g

        subs = [slice(ts * u, ts * u + ts) for u in range(tm // ts)]
        hbs = []
        for rows in subs:
            xt = x_ref[rows, :]
            r = lax.rsqrt(jnp.mean(xt * xt, axis=-1, keepdims=True) + EPS)
            h = ((xt * r) * g_ref[...]) * (1.0 + mod_ref[1:2, :]) + mod_ref[0:1, :]
            hbs.append(h.astype(BF16))
            h_ref[rows, :] = hbs[-1]
        projs = [jnp.dot(hb, w_ref[...], preferred_element_type=F32) for hb in hbs]
        for rows, proj in zip(subs, projs):
            ta, tb = ta_ref[:, rows, :], tb_ref[:, rows, :]
            for p in range(4):
                z = proj[:, 128 * p:128 * p + 128]
                qar_ref[rows, 128 * p:128 * p + 128] = z.astype(BF16)
                qv = _rope(head_norm(z, qg_ref[...]), ta, 16) * QA_SCALE
                qa_ref[rows, 128 * p:128 * p + 128] = qv.astype(BF16)
                qat_ref[128 * p:128 * p + 128, rows] = qv.T.astype(BF16)
                zb = proj[:, 768 + 128 * p:768 + 128 * p + 128]
                qb_ref[rows, 128 * p:128 * p + 128] = (_rope(zb, tb, 32) * QA_SCALE).astype(BF16)
            z = proj[:, 512:640]
            kar_ref[rows, :] = z.astype(BF16)
            kv_ = _rope(head_norm(z, kg_ref[...]), ta, 16)
            ka_ref[rows, :] = kv_.astype(BF16)
            kat_ref[:, rows] = kv_.T.astype(BF16)
            va_ref[rows, :] = proj[:, 640:768].astype(BF16)
            vat_ref[:, rows] = proj[:, 640:768].T.astype(BF16)
            kb_ref[rows, :] = _rope(proj[:, 1280:1408], tb, 32).astype(BF16)
            vb_ref[rows, :] = proj[:, 1408:1536].astype(BF16)
            ga_ref[rows, :] = proj[:, 1536:2560].astype(BF16)
            gb_ref[rows, :] = proj[:, 2560:3584].astype(BF16)

    tab = pl.BlockSpec((3, tm, 128), lambda i: (0, i, 0))
    shapes = [(D, BF16), (BW, BF16), (128, BF16), (BW, BF16), (128, BF16), (128, BF16),
              (BW, BF16), (128, BF16), (128, BF16), (D, BF16), (D, BF16)]
    return pl.pallas_call(
        body, grid=(S // tm,),
        in_specs=[_rows(tm, D), _const((6, D)), _const((1, D)), _const((D, INW)), _const((1, 128)), _const((1, 128)), tab, tab],
        out_specs=[_rows(tm, w) for w, _ in shapes] + [pl.BlockSpec((BW, tm), lambda i: (0, i))]
        + [pl.BlockSpec((None, 128, tm), lambda i: (i // per, 0, i % per))] * 2,
        out_shape=[jax.ShapeDtypeStruct((S, w), dt) for w, dt in shapes] + [jax.ShapeDtypeStruct((BW, S), BF16)]
        + [jax.ShapeDtypeStruct((S // tk, 128, tk), BF16)] * 2,
        compiler_params=_params(1), name="in_proj",
    )(x, modv, n1g, win, qg2, kg2, tab_a, tab_b)


def _exchange_gather(block_refs, out_refs, send_sems, recv_sems, local_sems):
    x, y, c = _me()
    me = 4 * x + 2 * y + c

    def copies():
        own, out, arrive = [], [], []
        for a, (blk, dst) in enumerate(zip(block_refs, out_refs)):
            own.append(pltpu.make_async_copy(blk, dst.at[me], local_sems.at[a]))
            for k in range(1, N_DEV):
                sems = dict(send_sem=send_sems.at[a, k - 1], recv_sem=recv_sems.at[a, k - 1], device_id=_peer(k), device_id_type=MESH)
                out.append(pltpu.make_async_remote_copy(blk, dst.at[me], **sems))
                arrive.append(pltpu.make_async_remote_copy(blk, dst.at[me ^ k], **sems))
        return own, out, arrive

    def start():
        own, out, _ = copies()
        for cp in own + out:
            cp.start()

    def finish():
        own, out, arrive = copies()
        for cp in arrive:
            cp.wait_recv()
        for cp in out:
            cp.wait_send()
        for cp in own:
            cp.wait()

    return start, finish


def _exchange_scatter(chunk_refs, recv_refs, send_sems, recv_sems):
    x, y, c = _me()
    me = 4 * x + 2 * y + c

    def copies():
        return [pltpu.make_async_remote_copy(src.at[me ^ k], dst.at[k - 1], send_sems.at[a, k - 1], recv_sems.at[a, k - 1],
                                             device_id=_peer(k), device_id_type=MESH)
                for a, (src, dst) in enumerate(zip(chunk_refs, recv_refs)) for k in range(1, N_DEV)]

    def start():
        for cp in copies():
            cp.start()

    def finish():
        cps = copies()
        for cp in cps:
            cp.wait_recv()
        for cp in cps:
            cp.wait_send()

    return start, finish


def _exchange_sems(n):
    return [pltpu.SemaphoreType.DMA((n, N_DEV - 1)), pltpu.SemaphoreType.DMA((n, N_DEV - 1))]


def _attn_a_fwd(qt, k, vt3, shards):
    S = qt.shape[1]
    tq = min(512, S)
    nq = S // tq
    per = max(1, min(TK_A_FWD, S) // vt3.shape[2])
    tk = per * vt3.shape[2]
    nk = S // tk
    ONES = 16
    AHEAD = 2
    ns = len(shards)

    def body(q_ref, k_ref, vt_ref, *rest):
        w_hbm, (o_ref, lse_ref), wall_hbm = rest[:ns], rest[ns:ns + 2], rest[ns + 2:2 * ns + 2]
        st_sc, send_sems, recv_sems, local_sems = rest[2 * ns + 2:]
        start, finish = _exchange_gather(w_hbm, wall_hbm, send_sems, recv_sems, local_sems)
        pl.when(pl.program_id(0) == 0)(start)
        row8 = lax.broadcasted_iota(jnp.int32, (NH, tq), 0)
        lse_all = jnp.zeros((NH, tq), F32)
        ones = jnp.ones((ONES, tk), BF16)
        for kv in range(2):
            qts = [q_ref[HD * (GRP * kv + t):HD * (GRP * kv + t) + HD, :] for t in range(GRP)]

            def keys(j, kv=kv):
                off = j * tk if isinstance(j, int) else pl.multiple_of(j * tk, tk)
                return k_ref[pl.ds(off, tk), :][:, HD * kv:HD * kv + HD]

            def scores(kj, t, qts=qts):
                return jnp.dot(kj, qts[t], preferred_element_type=F32)

            def step(j, carry, kv=kv, last=False):
                kj = keys(j)
                kn = None if last else keys(j + 1)
                vt = jnp.concatenate([vt_ref[per * j + u, HD * kv:HD * kv + HD, :] for u in range(per)], axis=1)
                v1 = jnp.concatenate([vt, ones], axis=0)
                sts = [st_sc[t] for t in range(AHEAD)]
                new = []
                for t in range(GRP):
                    m, acc = carry[2 * t], carry[2 * t + 1]
                    if t + AHEAD < GRP:
                        sts.append(scores(kj, t + AHEAD))
                    st = sts[t]
                    mn = jnp.maximum(m, jnp.max(st, axis=0, keepdims=True))
                    pt = jnp.exp2(st - mn)
                    if t + AHEAD >= GRP and not last:
                        st_sc[t + AHEAD - GRP] = scores(kn, t + AHEAD - GRP)
                    acc = jnp.exp2(m - mn) * acc + jnp.dot(v1, pt.astype(BF16), preferred_element_type=F32)
                    new += [mn, acc]
                return tuple(new)

            k0 = keys(0)
            for t in range(AHEAD):
                st_sc[t] = scores(k0, t)
            init = (jnp.full((1, tq), NEG, F32), jnp.zeros((HD + ONES, tq), F32)) * GRP
            res = step(nk - 1, lax.fori_loop(0, nk - 1, step, init), last=True)
            outs = []
            for t in range(GRP):
                m, acc = res[2 * t], res[2 * t + 1]
                l = acc[HD:HD + 1, :]
                outs.append((acc[:HD, :] / l).T)
                lse_all = jnp.where(row8 == GRP * kv + t, m + jnp.log2(l), lse_all)
            o_ref[:, 256 * kv:256 * kv + 256] = jnp.concatenate(outs, axis=1).astype(BF16)
        lse_ref[...] = lse_all
        pl.when(pl.program_id(0) == nq - 1)(finish)

    any_spec = pl.BlockSpec(memory_space=pl.ANY)
    return pl.pallas_call(
        body, grid=(nq,),
        in_specs=[pl.BlockSpec((BW, tq), lambda i: (0, i)), _const((S, 128)), _const(vt3.shape)] + [any_spec] * ns,
        out_specs=[_rows(tq, BW), pl.BlockSpec((NH, tq), lambda i: (0, i))] + [any_spec] * ns,
        out_shape=[jax.ShapeDtypeStruct((S, BW), BF16), jax.ShapeDtypeStruct((NH, S), F32)]
        + [jax.ShapeDtypeStruct((N_DEV,) + s.shape, s.dtype) for s in shards],
        scratch_shapes=[pltpu.VMEM((AHEAD, tk, tq), F32)] + _exchange_sems(ns) + [pltpu.SemaphoreType.DMA((ns,))],
        compiler_params=_params(1), name="attn_a_fwd",
    )(qt, k, vt3, *shards)


def _window_mask(i, tq, S):
    W = tq + 2 * WIN
    r = lax.broadcasted_iota(jnp.int32, (tq, W), 0)
    c = lax.broadcasted_iota(jnp.int32, (tq, W), 1)
    kpos = i * tq - WIN + c
    return (jnp.abs(c - WIN - r) <= WIN) & (kpos >= 0) & (kpos < S)


TQ_B = 256


def _attn_b_fwd(q, kp, vpt3, sink2):
    S = q.shape[0]
    tq = min(TQ_B, S)
    W = tq + 2 * WIN
    nc = vpt3.shape[0]
    ONES = 16

    def body(q_ref, k_ref, vt_ref, sink_ref, o_ref, lse_ref):
        i = pl.program_id(0)
        off = pl.multiple_of(i * tq, tq)
        r = lax.broadcasted_iota(jnp.int32, (W, tq), 1)
        c = lax.broadcasted_iota(jnp.int32, (W, tq), 0)
        kpos = i * tq - WIN + c
        valid = (jnp.abs(c - WIN - r) <= WIN) & (kpos >= 0) & (kpos < S)
        kw = k_ref[pl.ds(off, W), :]
        vt = jnp.concatenate([vt_ref[i + half] for half in range(W // tq)], axis=1)
        ones = jnp.ones((ONES, W), BF16)
        row8 = lax.broadcasted_iota(jnp.int32, (NH, tq), 0)
        lse_all = jnp.zeros((NH, tq), F32)
        qs = []
        for p in range(4):
            qp = q_ref[:, 128 * p:128 * p + 128]
            qs += [qp[:, :HD], qp[:, HD:]]
        khs = [kw[:, HD * kv:HD * kv + HD] for kv in range(2)]
        v1s = [jnp.concatenate([vt[HD * kv:HD * kv + HD, :], ones], axis=0) for kv in range(2)]

        def scores(h):
            return lax.dot_general(khs[h // GRP], qs[h], NT, preferred_element_type=F32)

        ss = [scores(0), scores(1)]
        outs = []
        for h in range(NH):
            if h + 2 < NH:
                ss.append(scores(h + 2))
            st = jnp.where(valid, ss[h], NEG)
            sk = sink_ref[:, h:h + 1]
            m = jnp.maximum(jnp.max(st, axis=0, keepdims=True), sk)
            acc = jnp.dot(v1s[h // GRP], jnp.exp2(st - m).astype(BF16), preferred_element_type=F32)
            l = acc[HD:HD + 1, :] + jnp.exp2(sk - m)
            outs.append((acc[:HD, :] / l).T)
            lse_all = jnp.where(row8 == h, m + jnp.log2(l), lse_all)
        for p in range(4):
            o_ref[:, 128 * p:128 * p + 128] = jnp.concatenate(outs[2 * p:2 * p + 2], axis=1).astype(BF16)
        lse_ref[...] = lse_all

    return pl.pallas_call(
        body, grid=(S // tq,),
        in_specs=[_rows(tq, BW), _const((S + 2 * WIN, 128)), _const((nc, 128, tq)), _const((1, NH))],
        out_specs=[_rows(tq, BW), pl.BlockSpec((NH, tq), lambda i: (0, i))],
        out_shape=[jax.ShapeDtypeStruct((S, BW), BF16), jax.ShapeDtypeStruct((NH, S), F32)],
        compiler_params=_params(1), name="attn_b_fwd",
    )(q, kp, vpt3, sink2)


def _merge_out(ya, yb, ga, gb, x, modv, wb, wout):
    S = x.shape[0]
    tm = min(256, S)

    def body(ya_ref, yb_ref, ga_ref, gb_ref, x_ref, mod_ref, wb_ref, wo_ref, x1_ref, mg_ref, ua_ref, ub_ref):
        ua = jnp.dot(ya_ref[...], wb_ref[0], preferred_element_type=F32)
        ub = jnp.dot(yb_ref[...], wb_ref[1], preferred_element_type=F32)
        merged = jax.nn.sigmoid(ga_ref[...].astype(F32)) * ua + jax.nn.sigmoid(gb_ref[...].astype(F32)) * ub
        mb = merged.astype(BF16)
        ua_ref[...] = ua.astype(BF16)
        ub_ref[...] = ub.astype(BF16)
        mg_ref[...] = mb
        x1_ref[...] = x_ref[...] + mod_ref[2:3, :] * jnp.dot(mb, wo_ref[...], preferred_element_type=F32)

    return pl.pallas_call(
        body, grid=(S // tm,),
        in_specs=[_rows(tm, BW), _rows(tm, BW), _rows(tm, D), _rows(tm, D), _rows(tm, D), _const((6, D)),
                  _const((2, BW, D)), _const((D, D))],
        out_specs=[_rows(tm, D)] * 4,
        out_shape=[jax.ShapeDtypeStruct((S, D), F32)] + [jax.ShapeDtypeStruct((S, D), BF16)] * 3,
        compiler_params=_params(1), name="merge_out",
    )(ya, yb, ga, gb, x, modv, wb, wout)


def _mlp_fwd(x1, modv, n2g, wmi, wmo, fg, target):
    S = x1.shape[0]
    tm = min(MLP_ROWS, S)
    tf = wmi.shape[2]
    nj = wmi.shape[0] // MLP_SHARDS

    def body(x1_ref, mod_ref, g_ref, wi_ref, wo_ref, fg_ref, t_ref, h2_ref, hp_ref, dx2_ref, st_ref, acc_ref):
        i, j = pl.program_id(0), pl.program_id(1)

        @pl.when(j == 0)
        def _():
            xt = x1_ref[...]
            r = lax.rsqrt(jnp.mean(xt * xt, axis=-1, keepdims=True) + EPS)
            h2 = ((xt * r) * g_ref[...]) * (1.0 + mod_ref[4:5, :]) + mod_ref[3:4, :]
            h2_ref[...] = h2.astype(BF16)
            acc_ref[...] = jnp.zeros_like(acc_ref)

        @pl.when((i == 0) & (j == 0))
        def _():
            st_ref[...] = jnp.zeros_like(st_ref)

        out = None
        for u in range(MLP_SHARDS):
            hp = jnp.dot(h2_ref[...], wi_ref[u], preferred_element_type=F32)
            hp_ref[:, tf * u:tf * u + tf] = hp.astype(BF16)
            hid = jnp.square(jnp.maximum(hp, 0.0))
            part = jnp.dot(hid.astype(BF16), wo_ref[u], preferred_element_type=F32)
            out = part if out is None else out + part
        acc_ref[...] += out

        @pl.when(j == nj - 1)
        def _():
            x2 = x1_ref[...] + mod_ref[5:6, :] * acc_ref[...]
            r3 = lax.rsqrt(jnp.mean(x2 * x2, axis=-1, keepdims=True) + EPS)
            xn = x2 * r3
            err = xn * fg_ref[...] - t_ref[...]
            dy = err * (1.0 / D)
            gy = dy * fg_ref[...]
            dx2_ref[...] = r3 * (gy - xn * jnp.mean(gy * xn, axis=-1, keepdims=True))
            st_ref[0:1, :] += jnp.sum(dy * xn, axis=0, keepdims=True)
            st_ref[1:2, :] += jnp.sum(err * err, axis=0, keepdims=True) * (0.5 / D)

    return pl.pallas_call(
        body, grid=(S // tm, nj),
        in_specs=[pl.BlockSpec((tm, D), lambda i, j: (i, 0)), _const((6, D)), _const((1, D)),
                  pl.BlockSpec((MLP_SHARDS, D, tf), lambda i, j: (j, 0, 0)), pl.BlockSpec((MLP_SHARDS, tf, D), lambda i, j: (j, 0, 0)),
                  _const((1, D)), pl.BlockSpec((tm, D), lambda i, j: (i, 0))],
        out_specs=[pl.BlockSpec((tm, D), lambda i, j: (i, 0)), pl.BlockSpec((tm, MLP_SHARDS * tf), lambda i, j: (i, j)),
                   pl.BlockSpec((tm, D), lambda i, j: (i, 0)), _const((8, D))],
        out_shape=[jax.ShapeDtypeStruct((S, D), BF16), jax.ShapeDtypeStruct((S, wmi.shape[0] * tf), BF16),
                   jax.ShapeDtypeStruct((S, D), F32), jax.ShapeDtypeStruct((8, D), F32)],
        scratch_shapes=[pltpu.VMEM((tm, D), F32)],
        compiler_params=_params(2), name="mlp_fwd",
    )(x1, modv, n2g, wmi, wmo, fg, target)


def _mlp_bwd(dx2, x1, hp, modv, n2g, wmi, wmo):
    S = x1.shape[0]
    tm = min(MLP_ROWS, S)
    tf = wmi.shape[2]
    nj = wmi.shape[0] // MLP_SHARDS

    def body(dx2_ref, x1_ref, hp_ref, mod_ref, g_ref, wi_ref, wo_ref, dhp_ref, dx1_ref, st_ref, dmo_ref, acc_ref):
        i, j = pl.program_id(0), pl.program_id(1)

        @pl.when(j == 0)
        def _():
            dmo_ref[...] = (mod_ref[5:6, :] * dx2_ref[...]).astype(BF16)
            acc_ref[...] = jnp.zeros_like(acc_ref)

        @pl.when((i == 0) & (j == 0))
        def _():
            st_ref[...] = jnp.zeros_like(st_ref)

        out = None
        for u in range(MLP_SHARDS):
            sl = slice(tf * u, tf * u + tf)
            dhid = lax.dot_general(dmo_ref[...], wo_ref[u], NT, preferred_element_type=F32)
            dhp = (dhid * (2.0 * jnp.maximum(hp_ref[:, sl].astype(F32), 0.0))).astype(BF16)
            dhp_ref[:, sl] = dhp
            part = lax.dot_general(dhp, wi_ref[u], NT, preferred_element_type=F32)
            out = part if out is None else out + part
        acc_ref[...] += out

        @pl.when(j == nj - 1)
        def _():
            dh2 = acc_ref[...]
            xt = x1_ref[...]
            r = lax.rsqrt(jnp.mean(xt * xt, axis=-1, keepdims=True) + EPS)
            xn = xt * r
            st_ref[0:1, :] += jnp.sum(dh2, axis=0, keepdims=True)
            st_ref[1:2, :] += jnp.sum(dh2 * xn, axis=0, keepdims=True)
            dxn = dh2 * (g_ref[...] * (1.0 + mod_ref[4:5, :]))
            dx1_ref[...] = dx2_ref[...] + r * (dxn - xn * jnp.mean(dxn * xn, axis=-1, keepdims=True))

    return pl.pallas_call(
        body, grid=(S // tm, nj),
        in_specs=[pl.BlockSpec((tm, D), lambda i, j: (i, 0)), pl.BlockSpec((tm, D), lambda i, j: (i, 0)),
                  pl.BlockSpec((tm, MLP_SHARDS * tf), lambda i, j: (i, j)), _const((6, D)), _const((1, D)),
                  pl.BlockSpec((MLP_SHARDS, D, tf), lambda i, j: (j, 0, 0)), pl.BlockSpec((MLP_SHARDS, tf, D), lambda i, j: (j, 0, 0))],
        out_specs=[pl.BlockSpec((tm, MLP_SHARDS * tf), lambda i, j: (i, j)), pl.BlockSpec((tm, D), lambda i, j: (i, 0)), _const((8, D))],
        out_shape=[jax.ShapeDtypeStruct((S, wmi.shape[0] * tf), BF16), jax.ShapeDtypeStruct((S, D), F32),
                   jax.ShapeDtypeStruct((8, D), F32)],
        scratch_shapes=[pltpu.VMEM((tm, D), BF16), pltpu.VMEM((tm, D), F32)],
        compiler_params=_params(2), name="mlp_bwd",
    )(dx2, x1, hp, modv, n2g, wmi, wmo)


def _tn_matmul(a, b, tk, tn, name, relu_sq=False, dev_major=False):
    S, K = a.shape
    N = b.shape[1]
    ts = min(1024, S)
    ns = S // ts

    def body(a_ref, b_ref, o_ref):
        @pl.when(pl.program_id(2) == 0)
        def _():
            o_ref[...] = jnp.zeros_like(o_ref)

        at = a_ref[...]
        if relu_sq:
            at = jnp.square(jnp.maximum(at.astype(F32), 0.0)).astype(BF16)
        o_ref[...] += lax.dot_general(at, b_ref[...].astype(BF16), TN, preferred_element_type=F32)

    if dev_major:
        out_spec = pl.BlockSpec((None, tk, tn), lambda k, n, s: (n, k, 0))
        out_shape = jax.ShapeDtypeStruct((N // tn, K, tn), F32)
    else:
        out_spec = pl.BlockSpec((tk, tn), lambda k, n, s: (k, n))
        out_shape = jax.ShapeDtypeStruct((K, N), F32)
    return pl.pallas_call(
        body, grid=(K // tk, N // tn, ns),
        in_specs=[pl.BlockSpec((ts, tk), lambda k, n, s: (s, k)), pl.BlockSpec((ts, tn), lambda k, n, s: (s, n))],
        out_specs=out_spec, out_shape=out_shape,
        compiler_params=_params(3), name=name,
    )(a, b)


def _scale_gate(m, w, g, row, name):
    K = m.shape[0]
    tk = min(512, K)

    def body(m_ref, w_ref, mod_ref, dw_ref, dg_ref):
        @pl.when(pl.program_id(0) == 0)
        def _():
            dg_ref[...] = jnp.zeros_like(dg_ref)

        mt = m_ref[...]
        dw_ref[...] = mt * mod_ref[row:row + 1, :]
        dg_ref[0:1, :] += jnp.sum(mt * w_ref[...].astype(F32), axis=0, keepdims=True)

    return pl.pallas_call(
        body, grid=(K // tk,),
        in_specs=[_rows(tk, D), _rows(tk, D), _const((6, D))],
        out_specs=[_rows(tk, D), _const((8, D))],
        out_shape=[jax.ShapeDtypeStruct((K, D), F32), jax.ShapeDtypeStruct((8, D), F32)],
        compiler_params=_params(1), name=name,
    )(m, w, g)


def _merge_bwd(dx1, modv, ga, gb, ua, ub, ya, yb, wb, wout):
    S = dx1.shape[0]
    tm = min(512, S)

    def body(dx1_ref, mod_ref, ga_ref, gb_ref, ua_ref, ub_ref, ya_ref, yb_ref, wb_ref, wo_ref,
             dua_ref, dub_ref, dga_ref, dgb_ref, dya_ref, dyb_ref, dla_ref, dlb_ref, dyat_ref):
        dao = (mod_ref[2:3, :] * dx1_ref[...]).astype(BF16)
        dm = lax.dot_general(dao, wo_ref[...], NT, preferred_element_type=F32)
        r = lax.broadcasted_iota(jnp.int32, (BW, NH), 0) // HD
        c = lax.broadcasted_iota(jnp.int32, (BW, NH), 1)
        head_of = (r == c).astype(BF16)
        for br, (g_ref, u_ref, y_ref, du_ref, dg_ref, dy_ref, dl_ref) in enumerate((
                (ga_ref, ua_ref, ya_ref, dua_ref, dga_ref, dya_ref, dla_ref),
                (gb_ref, ub_ref, yb_ref, dub_ref, dgb_ref, dyb_ref, dlb_ref))):
            sg = jax.nn.sigmoid(g_ref[...].astype(F32))
            du = (dm * sg).astype(BF16)
            du_ref[...] = du
            dg_ref[...] = (dm * u_ref[...].astype(F32) * sg * (1.0 - sg)).astype(BF16)
            dy = lax.dot_general(du, wb_ref[br], NT, preferred_element_type=F32)
            dyb16 = dy.astype(BF16)
            dy_ref[...] = dyb16
            if br == 0:
                dyat_ref[...] = dy.T.astype(BF16)
            prod = dyb16.astype(F32) * y_ref[...].astype(F32)
            hi = prod.astype(BF16)
            lo = (prod - hi.astype(F32)).astype(BF16)
            dl_ref[...] = (jnp.dot(hi, head_of, preferred_element_type=F32) + jnp.dot(lo, head_of, preferred_element_type=F32))

    return pl.pallas_call(
        body, grid=(S // tm,),
        in_specs=[_rows(tm, D), _const((6, D)), _rows(tm, D), _rows(tm, D), _rows(tm, D), _rows(tm, D),
                  _rows(tm, BW), _rows(tm, BW), _const((2, BW, D)), _const((D, D))],
        out_specs=[_rows(tm, D)] * 4 + [_rows(tm, BW)] * 2 + [_rows(tm, NH)] * 2 + [pl.BlockSpec((BW, tm), lambda i: (0, i))],
        out_shape=[jax.ShapeDtypeStruct((S, D), BF16)] * 4 + [jax.ShapeDtypeStruct((S, BW), BF16)] * 2
        + [jax.ShapeDtypeStruct((S, NH), F32)] * 2 + [jax.ShapeDtypeStruct((BW, S), BF16)],
        compiler_params=_params(1), name="merge_bwd",
    )(dx1, modv, ga, gb, ua, ub, ya, yb, wb, wout)


def _attn_a_bwd(q, qt, kt3, vt3, do, dot_, lse, delta, chunks):
    S = q.shape[0]
    tq = min(1024, S)
    nk, _, tk = kt3.shape
    nq = S // tq
    nc = len(chunks)

    def body(q_ref, qt_ref, do_ref, dot_ref, lse_ref, dl_ref, kt_ref, vt_ref, *rest):
        g_hbm, (dq_ref, dk_hbm, dv_hbm), recv_hbm = rest[:nc], rest[nc:nc + 3], rest[nc + 3:2 * nc + 3]
        dk_sc, dv_sc, sem, send_sems, recv_sems = rest[2 * nc + 3:]
        i = pl.program_id(0)
        start, finish = _exchange_scatter(g_hbm, recv_hbm, send_sems, recv_sems)

        @pl.when(i == 0)
        def _():
            start()
            dk_sc[...] = jnp.zeros_like(dk_sc)
            dv_sc[...] = jnp.zeros_like(dv_sc)

        for kv in range(2):
            qg = q_ref[:, 256 * kv:256 * kv + 256]
            dog = do_ref[:, 256 * kv:256 * kv + 256]
            heads = []
            for t in range(GRP):
                h = GRP * kv + t
                heads.append((qg[:, HD * t:HD * t + HD], dog[:, HD * t:HD * t + HD],
                              qt_ref[HD * h:HD * h + HD, :], dot_ref[HD * h:HD * h + HD, :],
                              lse_ref[:, h:h + 1], dl_ref[:, h:h + 1]))

            q2 = [jnp.concatenate([heads[2 * u][0], heads[2 * u + 1][0]], axis=0) for u in range(GRP // 2)]
            do2 = [jnp.concatenate([heads[2 * u][1], heads[2 * u + 1][1]], axis=0) for u in range(GRP // 2)]

            def step(j, carry, kv=kv, heads=heads, q2=q2, do2=do2):
                kjt = kt_ref[j, HD * kv:HD * kv + HD, :]
                vjt = vt_ref[j, HD * kv:HD * kv + HD, :]
                dkt = jnp.zeros((HD, tk), F32)
                dvt = jnp.zeros((HD, tk), F32)
                new = []

                def logits(u):
                    return (jnp.dot(q2[u], kjt, preferred_element_type=F32), jnp.dot(do2[u], vjt, preferred_element_type=F32))

                sd = [logits(0)]
                for t, (qh, doh, qth, doth, lse_h, dl_h) in enumerate(heads):
                    if t == 0:
                        sd.append(logits(1))
                    rows = slice(tq * (t % 2), tq * (t % 2) + tq)
                    s, dp = sd[t // 2][0][rows, :], sd[t // 2][1][rows, :]
                    pm = jnp.exp2(s - lse_h)
                    ds = (pm * (dp - dl_h)).astype(BF16)
                    dvt = dvt + jnp.dot(doth, pm.astype(BF16), preferred_element_type=F32)
                    dkt = dkt + jnp.dot(qth, ds, preferred_element_type=F32)
                    new.append(carry[t] + lax.dot_general(kjt, ds, NT, preferred_element_type=F32))
                dk_sc[j, HD * kv:HD * kv + HD, :] += dkt
                dv_sc[j, HD * kv:HD * kv + HD, :] += dvt
                return tuple(new)

            res = lax.fori_loop(0, nk, step, (jnp.zeros((HD, tq), F32),) * GRP)
            for t in range(GRP):
                dq_ref[HD * (GRP * kv + t):HD * (GRP * kv + t) + HD, :] = res[t]

        @pl.when(i == nq - 1)
        def _():
            c1 = pltpu.make_async_copy(dk_sc, dk_hbm, sem.at[0])
            c2 = pltpu.make_async_copy(dv_sc, dv_hbm, sem.at[1])
            c1.start()
            c2.start()
            c1.wait()
            c2.wait()
            finish()

    any_spec = pl.BlockSpec(memory_space=pl.ANY)
    cols = pl.BlockSpec((BW, tq), lambda i: (0, i))
    return pl.pallas_call(
        body, grid=(nq,),
        in_specs=[_rows(tq, BW), cols, _rows(tq, BW), cols, _rows(tq, NH), _rows(tq, NH), _const((nk, 128, tk)),
                  _const((nk, 128, tk))] + [any_spec] * nc,
        out_specs=[cols, any_spec, any_spec] + [any_spec] * nc,
        out_shape=[jax.ShapeDtypeStruct((BW, S), F32), jax.ShapeDtypeStruct((nk, 128, tk), F32),
                   jax.ShapeDtypeStruct((nk, 128, tk), F32)]
        + [jax.ShapeDtypeStruct((N_DEV - 1,) + c.shape[1:], c.dtype) for c in chunks],
        scratch_shapes=[pltpu.VMEM((nk, 128, tk), F32), pltpu.VMEM((nk, 128, tk), F32), pltpu.SemaphoreType.DMA((2,))]
        + _exchange_sems(nc),
        compiler_params=_params(1), name="attn_a_bwd",
    )(q, qt, do, dot_, lse, delta, kt3, vt3, *chunks)


def _attn_b_bwd(q, kp, vp, sink2, do, lse, delta):
    S = q.shape[0]
    tq = min(TQ_B, S)
    W = tq + 2 * WIN
    nq = S // tq
    nc = (S + 2 * WIN) // tq

    def body(q_ref, k_ref, v_ref, sink_ref, do_ref, lse_ref, dl_ref, dq_ref, dk_hbm, dv_hbm, ds_ref, dk_sc, dv_sc, sem):
        i = pl.program_id(0)

        @pl.when(i == 0)
        def _():
            dk_sc[...] = jnp.zeros_like(dk_sc)
            dv_sc[...] = jnp.zeros_like(dv_sc)
            ds_ref[...] = jnp.zeros_like(ds_ref)

        off = pl.multiple_of(i * tq, tq)
        valid = _window_mask(i, tq, S)
        kw = k_ref[pl.ds(off, W), :]
        vw = v_ref[pl.ds(off, W), :]
        lse_i = lse_ref[...]
        dl_i = dl_ref[...]
        qa = q_ref[...]
        doa = do_ref[...]
        qt = qa.astype(F32).T.astype(BF16)
        dot_ = doa.astype(F32).T.astype(BF16)
        khs = [kw[:, HD * kv:HD * kv + HD] for kv in range(2)]
        vhs = [vw[:, HD * kv:HD * kv + HD] for kv in range(2)]

        def logits(h):
            return (lax.dot_general(qa[:, HD * h:HD * h + HD], khs[h // GRP], NT, preferred_element_type=F32),
                    lax.dot_general(doa[:, HD * h:HD * h + HD], vhs[h // GRP], NT, preferred_element_type=F32))

        sd = [logits(0)]
        dqs = []
        dkt = [jnp.zeros((HD, W), F32), jnp.zeros((HD, W), F32)]
        dvt = [jnp.zeros((HD, W), F32), jnp.zeros((HD, W), F32)]
        for h in range(NH):
            kv = h // GRP
            if h + 1 < NH:
                sd.append(logits(h + 1))
            s, dp = sd[h]
            pm = jnp.exp2(jnp.where(valid, s, NEG) - lse_i[:, h:h + 1])
            ds = (pm * (dp - dl_i[:, h:h + 1])).astype(BF16)
            dvt[kv] = dvt[kv] + jnp.dot(dot_[HD * h:HD * h + HD, :], pm.astype(BF16), preferred_element_type=F32)
            dkt[kv] = dkt[kv] + jnp.dot(qt[HD * h:HD * h + HD, :], ds, preferred_element_type=F32)
            dqs.append(jnp.dot(ds, khs[kv], preferred_element_type=F32))
        for p in range(4):
            dq_ref[:, 128 * p:128 * p + 128] = jnp.concatenate(dqs[2 * p:2 * p + 2], axis=1)
        for half in range(W // tq):
            dk_sc[i + half] += jnp.concatenate([d[:, tq * half:tq * half + tq] for d in dkt], axis=0)
            dv_sc[i + half] += jnp.concatenate([d[:, tq * half:tq * half + tq] for d in dvt], axis=0)
        psd = jnp.exp2(sink_ref[...] - lse_i) * dl_i
        r = lax.broadcasted_iota(jnp.int32, (NH, 128), 0)
        c = lax.broadcasted_iota(jnp.int32, (NH, 128), 1)
        row = jnp.dot(jnp.sum(psd, axis=0, keepdims=True), (r == c).astype(F32),
                      preferred_element_type=F32, precision=lax.Precision.HIGHEST)
        ds_ref[...] -= jnp.broadcast_to(row, (8, 128))

        @pl.when(i == nq - 1)
        def _():
            c1 = pltpu.make_async_copy(dk_sc, dk_hbm, sem.at[0])
            c2 = pltpu.make_async_copy(dv_sc, dv_hbm, sem.at[1])
            c1.start()
            c2.start()
            c1.wait()
            c2.wait()

    any_spec = pl.BlockSpec(memory_space=pl.ANY)
    return pl.pallas_call(
        body, grid=(nq,),
        in_specs=[_rows(tq, BW), _const((S + 2 * WIN, 128)), _const((S + 2 * WIN, 128)), _const((1, NH)),
                  _rows(tq, BW), _rows(tq, NH), _rows(tq, NH)],
        out_specs=[_rows(tq, BW), any_spec, any_spec, _const((8, 128))],
        out_shape=[jax.ShapeDtypeStruct((S, BW), F32), jax.ShapeDtypeStruct((nc, 128, tq), F32),
                   jax.ShapeDtypeStruct((nc, 128, tq), F32), jax.ShapeDtypeStruct((8, 128), F32)],
        scratch_shapes=[pltpu.VMEM((nc, 128, tq), F32), pltpu.VMEM((nc, 128, tq), F32), pltpu.SemaphoreType.DMA((2,))],
        compiler_params=_params(1), name="attn_b_bwd",
    )(q, kp, vp, sink2, do, lse, delta)


def _qk_bwd(dqa_t, dka_t3, dva_t3, dqb, dkb, dvb, qar, kar, qg2, kg2, tab_a, tab_b, dga, dgb):
    S = dqb.shape[0]
    tm = min(256, S)
    per = dka_t3.shape[2] // tm

    def body(dqa_ref, dka_ref, dva_ref, dqb_ref, dkb_ref, dvb_ref, qar_ref, kar_ref, qg_ref, kg_ref, ta_ref, tb_ref,
             dga_ref, dgb_ref, dp_ref, st_ref):
        @pl.when(pl.program_id(0) == 0)
        def _():
            st_ref[...] = jnp.zeros_like(st_ref)

        seg = _seg_matrix(128, HD)

        def norm_bwd(dz_rot, raw, g):
            dzn = _rope_t(dz_rot, ta_ref, 16)
            raw = raw.astype(F32)
            rr = lax.rsqrt(_seg_sum(raw * raw, seg) * (1.0 / HD) + EPS)
            zhat = raw * rr
            dzh = dzn * g
            draw = rr * (dzh - zhat * (_seg_sum(dzh * zhat, seg) * (1.0 / HD)))
            return draw, jnp.sum(dzn * zhat, axis=0, keepdims=True)

        gq = jnp.zeros((1, 128), F32)
        for p in range(4):
            sl = slice(128 * p, 128 * p + 128)
            draw, gsum = norm_bwd(dqa_ref[sl, :].T * 0.125, qar_ref[:, sl], qg_ref[...])
            gq = gq + gsum
            dp_ref[:, sl] = draw.astype(BF16)
            dp_ref[:, 768 + 128 * p:768 + 128 * p + 128] = _rope_t(dqb_ref[:, sl] * 0.125, tb_ref, 32).astype(BF16)
        draw, gk = norm_bwd(dka_ref[...].T * LN2, kar_ref[...], kg_ref[...])
        dp_ref[:, 512:640] = draw.astype(BF16)
        dp_ref[:, 640:768] = dva_ref[...].T.astype(BF16)
        dp_ref[:, 1280:1408] = _rope_t(dkb_ref[...] * LN2, tb_ref, 32).astype(BF16)
        dp_ref[:, 1408:1536] = dvb_ref[...].astype(BF16)
        dp_ref[:, 1536:2560] = dga_ref[...]
        dp_ref[:, 2560:3584] = dgb_ref[...]
        st_ref[0:1, :] += gq
        st_ref[1:2, :] += gk

    tab = pl.BlockSpec((3, tm, 128), lambda i: (0, i, 0))
    chunk_t = pl.BlockSpec((None, 128, tm), lambda i: (i // per, 0, i % per))
    return pl.pallas_call(
        body, grid=(S // tm,),
        in_specs=[pl.BlockSpec((BW, tm), lambda i: (0, i)), chunk_t, chunk_t, _rows(tm, BW), _rows(tm, 128), _rows(tm, 128),
                  _rows(tm, BW), _rows(tm, 128), _const((1, 128)), _const((1, 128)), tab, tab, _rows(tm, D), _rows(tm, D)],
        out_specs=[_rows(tm, INW), _const((8, 128))],
        out_shape=[jax.ShapeDtypeStruct((S, INW), BF16), jax.ShapeDtypeStruct((8, 128), F32)],
        compiler_params=_params(1), name="qk_bwd",
    )(dqa_t, dka_t3, dva_t3, dqb, dkb, dvb, qar, kar, qg2, kg2, tab_a, tab_b, dga, dgb)


def _in_bwd(dproj, win, x, dx1, modv, n1g, chunks):
    S = x.shape[0]
    tm = min(512, S)
    n = S // tm
    ts = min(256, tm)

    def body(dp_ref, w_ref, x_ref, dx1_ref, mod_ref, g_ref, c_hbm, gx_ref, st_ref, recv_hbm, send_sems, recv_sems):
        start, finish = _exchange_scatter([c_hbm], [recv_hbm], send_sems, recv_sems)

        @pl.when(pl.program_id(0) == 0)
        def _():
            start()
            st_ref[...] = jnp.zeros_like(st_ref)

        subs = [slice(ts * u, ts * u + ts) for u in range(tm // ts)]
        dhs = [lax.dot_general(dp_ref[rows, :], w_ref[...], NT, preferred_element_type=F32) for rows in subs]
        for rows, dh in zip(subs, dhs):
            xt = x_ref[rows, :]
            r = lax.rsqrt(jnp.mean(xt * xt, axis=-1, keepdims=True) + EPS)
            xn = xt * r
            st_ref[0:1, :] += jnp.sum(dh, axis=0, keepdims=True)
            st_ref[1:2, :] += jnp.sum(dh * xn, axis=0, keepdims=True)
            dxn = dh * (g_ref[...] * (1.0 + mod_ref[1:2, :]))
            gx_ref[rows, :] = dx1_ref[rows, :] + r * (dxn - xn * jnp.mean(dxn * xn, axis=-1, keepdims=True))
        pl.when(pl.program_id(0) == n - 1)(finish)

    any_spec = pl.BlockSpec(memory_space=pl.ANY)
    return pl.pallas_call(
        body, grid=(n,),
        in_specs=[_rows(tm, INW), _const((D, INW)), _rows(tm, D), _rows(tm, D), _const((6, D)), _const((1, D)), any_spec],
        out_specs=[_rows(tm, D), _const((8, D)), any_spec],
        out_shape=[jax.ShapeDtypeStruct((S, D), F32), jax.ShapeDtypeStruct((8, D), F32),
                   jax.ShapeDtypeStruct((N_DEV - 1,) + chunks.shape[1:], chunks.dtype)],
        scratch_shapes=_exchange_sems(1),
        compiler_params=_params(1), name="in_bwd",
    )(dproj, win, x, dx1, modv, n1g, chunks)


def _pack_small(st1, st2, stf, dg1, dg2, stqk, dsink, modv, n1g, n2g):
    def body(st1_ref, st2_ref, stf_ref, dg1_ref, dg2_ref, qk_ref, ds_ref, mod_ref, g1_ref, g2_ref, o_ref):
        a1, b1 = st1_ref[0:1, :], st1_ref[1:2, :]
        a2, b2 = st2_ref[0:1, :], st2_ref[1:2, :]
        r = lax.broadcasted_iota(jnp.int32, (128, D), 0)
        c = lax.broadcasted_iota(jnp.int32, (128, D), 1)
        fold_q = (c == r % HD).astype(F32)
        fold_k = (c == HD + r % HD).astype(F32)
        keep = (c == r).astype(F32)

        def place(v, sel):
            return jnp.dot(v, sel, preferred_element_type=F32, precision=lax.Precision.HIGHEST)

        loss = jnp.sum(stf_ref[1:2, :], axis=1, keepdims=True)
        lane = lax.broadcasted_iota(jnp.int32, (1, D), 1)
        rows = [a1, g1_ref[...] * b1, dg1_ref[0:1, :], a2, g2_ref[...] * b2, dg2_ref[0:1, :],
                (1.0 + mod_ref[1:2, :]) * b1, (1.0 + mod_ref[4:5, :]) * b2, stf_ref[0:1, :],
                place(qk_ref[0:1, :], fold_q) + place(qk_ref[1:2, :], fold_k),
                place(ds_ref[0:1, :], keep),
                jnp.where(lane == 0, loss, 0.0)]
        rows += [jnp.zeros((1, D), F32)] * (SMALL_ROWS - len(rows))
        for n, v in enumerate(rows):
            o_ref[n:n + 1, :] = v

    return pl.pallas_call(
        body, out_shape=jax.ShapeDtypeStruct((SMALL_ROWS, D), F32),
        compiler_params=pltpu.CompilerParams(vmem_limit_bytes=V7X_VMEM_LIMIT), name="pack_small",
    )(st1, st2, stf, dg1, dg2, stqk, dsink, modv, n1g, n2g)


def _wada_grad(silu_all, dmod_cols):
    def body(a_ref, b_ref, o_ref):
        o_ref[...] = lax.dot_general(a_ref[...], b_ref[...], TN, preferred_element_type=F32, precision=lax.Precision.HIGHEST)

    return pl.pallas_call(
        body, out_shape=jax.ShapeDtypeStruct((D, dmod_cols.shape[1]), F32),
        compiler_params=pltpu.CompilerParams(vmem_limit_bytes=V7X_VMEM_LIMIT), name="wada_grad",
    )(silu_all, dmod_cols)


def _adamw_sum(parts, w, m, v, name):
    R, C = w.shape
    tr = R if R <= 64 else next(t for t in (256, 128, 64, 32, 16, 8) if R % t == 0)
    n = len(parts)
    dyn = [idx for _, idx in parts if idx is not None and not isinstance(idx, int)]
    b1c = 1.0 - ADAM_B1 ** ADAM_STEP
    b2c = 1.0 - ADAM_B2 ** ADAM_STEP

    def body(*refs):
        refs = refs[len(dyn):]
        g = refs[0][...].astype(F32)
        for k in range(1, n):
            g = g + refs[k][...].astype(F32)
        w_ref, m_ref, v_ref, g_out, d_out, m_out, v_out = refs[n:]
        mn = ADAM_B1 * m_ref[...] + (1.0 - ADAM_B1) * g
        vn = ADAM_B2 * v_ref[...] + (1.0 - ADAM_B2) * jnp.square(g)
        g_out[...] = g
        m_out[...] = mn
        v_out[...] = vn
        d_out[...] = -ADAM_LR * ((mn / b1c) / (jnp.sqrt(vn / b2c) + ADAM_EPS) + ADAM_WD * w_ref[...])

    in_specs = []
    nd = 0
    for a, idx in parts:
        if idx is None:
            in_specs.append(pl.BlockSpec((tr, C), lambda i, *s: (i, 0)))
        elif isinstance(idx, int):
            in_specs.append(pl.BlockSpec((None, tr, C), lambda i, *s, idx=idx: (idx, i, 0)))
        else:
            in_specs.append(pl.BlockSpec((None, tr, C), lambda i, *s, nd=nd: (s[nd][0], i, 0)))
            nd += 1
    blk = pl.BlockSpec((tr, C), lambda i, *s: (i, 0))
    grid_spec = pltpu.PrefetchScalarGridSpec(
        num_scalar_prefetch=len(dyn), grid=(R // tr,), in_specs=in_specs + [blk] * 3, out_specs=[blk] * 4)
    return pl.pallas_call(
        body, grid_spec=grid_spec, out_shape=[jax.ShapeDtypeStruct((R, C), F32)] * 4,
        compiler_params=_params(1), name=name,
    )(*dyn, *[a for a, _ in parts], w, m, v)


def _me():
    return lax.axis_index("x"), lax.axis_index("y"), lax.axis_index("c")


def _peer(k):
    x, y, c = _me()
    return (x ^ ((k >> 2) & 1), y ^ ((k >> 1) & 1), c ^ (k & 1))


def _ada_exchange(c_row, w_ada, b_rows):
    NW = w_ada.shape[1]

    def body(c_ref, w_ref, b_ref, sall_ref, mod_ref, src_ref, mp_ref, send1, recv1, send2, recv2):
        x, y, c = _me()
        me = 4 * x + 2 * y + c
        cv = c_ref[...]
        src_ref[...] = jnp.broadcast_to(cv * jax.nn.sigmoid(cv), (8, D))
        mine = pl.ds(pl.multiple_of(me * 8, 8), 8)
        sall_ref[mine, :] = src_ref[...]
        sends = [pltpu.make_async_remote_copy(src_ref, sall_ref.at[mine, :], send1.at[k - 1], recv1.at[k - 1],
                                              device_id=_peer(k), device_id_type=MESH) for k in range(1, N_DEV)]
        for cp in sends:
            cp.start()
        for k in range(1, N_DEV):
            theirs = pl.ds(pl.multiple_of((me ^ k) * 8, 8), 8)
            pltpu.make_async_remote_copy(src_ref, sall_ref.at[theirs, :], send1.at[k - 1], recv1.at[k - 1],
                                         device_id=_peer(k), device_id_type=MESH).wait_recv()
        for cp in sends:
            cp.wait_send()
        mp_ref[...] = jnp.dot(sall_ref[...], w_ref[...], preferred_element_type=F32, precision=lax.Precision.HIGHEST)
        mod_ref[mine, :] = mp_ref[mine, :] + b_ref[mine, :]
        sends = []
        for k in range(1, N_DEV):
            theirs = pl.ds(pl.multiple_of((me ^ k) * 8, 8), 8)
            sends.append(pltpu.make_async_remote_copy(mp_ref.at[theirs, :], mod_ref.at[mine, :], send2.at[k - 1], recv2.at[k - 1],
                                                      device_id=_peer(k), device_id_type=MESH))
        for cp in sends:
            cp.start()
        for k in range(1, N_DEV):
            theirs = pl.ds(pl.multiple_of((me ^ k) * 8, 8), 8)
            pltpu.make_async_remote_copy(mp_ref.at[mine, :], mod_ref.at[theirs, :], send2.at[k - 1], recv2.at[k - 1],
                                         device_id=_peer(k), device_id_type=MESH).wait_recv()
            mod_ref[theirs, :] = mod_ref[theirs, :] + b_ref[theirs, :]
        for cp in sends:
            cp.wait_send()

    vm = pl.BlockSpec(memory_space=pltpu.VMEM)
    return pl.pallas_call(
        body, in_specs=[vm, vm, vm], out_specs=[vm, vm],
        out_shape=[jax.ShapeDtypeStruct((8 * N_DEV, D), F32), jax.ShapeDtypeStruct((8 * N_DEV, NW), F32)],
        scratch_shapes=[pltpu.VMEM((8, D), F32), pltpu.VMEM((8 * N_DEV, NW), F32)]
        + [pltpu.SemaphoreType.DMA((N_DEV - 1,))] * 4,
        compiler_params=pltpu.CompilerParams(vmem_limit_bytes=V7X_VMEM_LIMIT), name="ada_exchange",
    )(c_row, w_ada, b_rows)


def _weight_gather(shard):
    def body(x_ref, out_ref, send_sems, recv_sems, local_sem):
        x, y, c = _me()
        me, sibling = (x, y, c), (x, y, 1 - c)
        chips = [(1 - x, y), (x, 1 - y), (1 - x, 1 - y)]

        def slot(px, py, pc):
            return out_ref.at[4 * px + 2 * py + pc]

        def copy(k, block, to, src=None):
            return pltpu.make_async_remote_copy(
                src_ref=slot(*block) if src is None else src, dst_ref=slot(*block),
                send_sem=send_sems.at[k], recv_sem=recv_sems.at[k], device_id=to, device_id_type=MESH)

        mine = pltpu.make_async_copy(x_ref, slot(*me), local_sem)
        mine.start()
        first = [copy(0, me, sibling, src=x_ref)]
        first += [copy(1 + j, me, (*chip, c), src=x_ref) for j, chip in enumerate(chips)]
        for cp in first:
            cp.start()
        passed = [copy(4 + j, (*chip, c), sibling) for j, chip in enumerate(chips)]
        for j, chip in enumerate(chips):
            copy(1 + j, (*chip, c), me).wait_recv()
            passed[j].start()
        copy(0, sibling, me).wait_recv()
        for j, chip in enumerate(chips):
            copy(4 + j, (*chip, 1 - c), me).wait_recv()
        for cp in first + passed:
            cp.wait_send()
        mine.wait()

    any_spec = pl.BlockSpec(memory_space=pl.ANY)
    return pl.pallas_call(
        body, in_specs=[any_spec], out_specs=any_spec,
        out_shape=jax.ShapeDtypeStruct((N_DEV,) + shard.shape, shard.dtype),
        scratch_shapes=[pltpu.SemaphoreType.DMA((7,)), pltpu.SemaphoreType.DMA((7,)), pltpu.SemaphoreType.DMA],
        name="weight_gather",
    )(shard)


def _small_gather(block):
    def body(b_ref, out_ref, send_sems, recv_sems):
        x, y, c = _me()
        me = 4 * x + 2 * y + c
        out_ref[me] = b_ref[...]
        sends = [pltpu.make_async_remote_copy(b_ref, out_ref.at[me], send_sems.at[k - 1], recv_sems.at[k - 1],
                                              device_id=_peer(k), device_id_type=MESH) for k in range(1, N_DEV)]
        for cp in sends:
            cp.start()
        for k in range(1, N_DEV):
            pltpu.make_async_remote_copy(b_ref, out_ref.at[me ^ k], send_sems.at[k - 1], recv_sems.at[k - 1],
                                         device_id=_peer(k), device_id_type=MESH).wait_recv()
        for cp in sends:
            cp.wait_send()

    vm = pl.BlockSpec(memory_space=pltpu.VMEM)
    return pl.pallas_call(
        body, in_specs=[vm], out_specs=vm,
        out_shape=jax.ShapeDtypeStruct((N_DEV,) + block.shape, block.dtype),
        scratch_shapes=[pltpu.SemaphoreType.DMA((N_DEV - 1,)), pltpu.SemaphoreType.DMA((N_DEV - 1,))],
        name="small_gather",
    )(block)


def _pack_small_params(b_ada, n1, n2, fg, qn, kn, sink):
    qk = jnp.concatenate([qn.reshape(1, HD), kn.reshape(1, HD), jnp.zeros((1, D - 2 * HD), F32)], axis=1)
    sk = jnp.concatenate([sink.reshape(1, NH), jnp.zeros((1, D - NH), F32)], axis=1)
    return jnp.concatenate([b_ada.reshape(6, D), n1.reshape(1, D), n2.reshape(1, D), fg.reshape(1, D), qk, sk,
                            jnp.zeros((SMALL_ROWS - 11, D), F32)], axis=0)


def _unpack_small(p):
    return (p[0:6].reshape(1, 6 * D), p[6].reshape(1, D), p[9, 0:HD].reshape(1, HD), p[9, HD:2 * HD].reshape(1, HD),
            p[10, 0:NH].reshape(1, NH), p[7].reshape(1, D), p[8].reshape(D))


def kernel(x, c, w_ada, b_ada, norm1_g, w_in, q_norm_a, k_norm_a, sink_b, w_branch, w_out, norm2_g, w_mlp_in, w_mlp_out, final_g, loss_target, m_w_ada, m_b_ada, m_norm1_g, m_w_in, m_q_norm_a, m_k_norm_a, m_sink_b, m_w_branch, m_w_out, m_norm2_g, m_w_mlp_in, m_w_mlp_out, m_final_g, v_w_ada, v_b_ada, v_norm1_g, v_w_in, v_q_norm_a, v_k_norm_a, v_sink_b, v_w_branch, v_w_out, v_norm2_g, v_w_mlp_in, v_w_mlp_out, v_final_g):
    S = x.shape[1]
    xs = x.reshape(S, D)
    tgt = loss_target.reshape(S, D)
    ax, ay, ac = lax.axis_index("x"), lax.axis_index("y"), lax.axis_index("c")
    me = 4 * ax + 2 * ay + ac
    me1 = me.reshape(1).astype(jnp.int32)
    NW = w_ada.shape[2]
    NI = w_in.shape[2]

    silu64, mod64 = _ada_exchange(c.reshape(1, D), w_ada.reshape(D, NW),
                                  jnp.repeat(b_ada.reshape(N_DEV, NW), 8, axis=0))
    silu_all = silu64[0::8]
    modv = mod64[0::8].reshape(6, D)

    win = _weight_gather(w_in[0].astype(BF16)).transpose(1, 0, 2).reshape(D, INW)
    rest_shards = tuple(w[0].astype(BF16) for w in (w_branch, w_out, w_mlp_in, w_mlp_out))

    tab_a, tab_b = _rope_tables(S)
    qg2 = jnp.tile(q_norm_a.reshape(1, HD), (1, 2))
    kg2 = jnp.tile(k_norm_a.reshape(1, HD), (1, 2))
    n1g = norm1_g.reshape(1, D)
    n2g = norm2_g.reshape(1, D)
    fg = final_g.reshape(1, D)
    sink2 = sink_b.reshape(1, NH) * LOG2E

    h, qar, kar, qa, ka, va, qb, kb, vb, ga, gb, qa_t, ka_t3, va_t3 = _in_proj(xs, modv, n1g, win, qg2, kg2, tab_a, tab_b)
    ya, lse_at, wb, wout, wmi, wmo = _attn_a_fwd(qa_t, ka, va_t3, rest_shards)
    lse_a = lse_at.T
    wb = wb.transpose(1, 2, 0, 3).reshape(2, BW, D)
    wout = wout.reshape(D, D)
    pad = ((WIN, WIN), (0, 0))
    kbp, vbp = jnp.pad(kb, pad), jnp.pad(vb, pad)
    tb = min(TQ_B, S)
    yb, lse_bt = _attn_b_fwd(qb, kbp, vbp.reshape((S + 2 * WIN) // tb, tb, 128).transpose(0, 2, 1), sink2)
    lse_b = lse_bt.T
    x1, merged, ua, ub = _merge_out(ya, yb, ga, gb, xs, modv, wb, wout)
    h2, hp, dx2, stf = _mlp_fwd(x1, modv, n2g, wmi, wmo, fg, tgt)

    dhp, dx1, st2 = _mlp_bwd(dx2, x1, hp, modv, n2g, wmi, wmo)
    m2 = _tn_matmul(hp, dx2, 2048, D, "dw_mlp_out", relu_sq=True)
    g_wmo, dg2 = _scale_gate(m2, wmo.reshape(FF, D), modv, 5, "gate2_grad")
    g_wmi = _tn_matmul(h2, dhp, D, 512, "dw_mlp_in", dev_major=True)
    dua, dub, dga, dgb, dya, dyb, dl_a, dl_b, dya_t = _merge_bwd(dx1, modv, ga, gb, ua, ub, ya, yb, wb, wout)
    m1 = _tn_matmul(merged, dx1, D, D, "dw_out")
    g_wout, dg1 = _scale_gate(m1, wout, modv, 2, "gate1_grad")
    g_wb0 = _tn_matmul(ya, dua, BW, D, "dw_branch_a")
    g_wb1 = _tn_matmul(yb, dub, BW, D, "dw_branch_b")
    g_wb = jnp.stack([g_wb0, g_wb1]).reshape(2, BW, N_DEV, 128).transpose(2, 0, 1, 3).reshape(N_DEV, 2 * BW, 128)
    g_wout = g_wout.reshape(N_DEV, 128, D)
    g_wmo = g_wmo.reshape(N_DEV, 512, D)
    dqa_t, dka_t3, dva_t3, r_wb, r_wout, r_wmi, r_wmo = _attn_a_bwd(qa, qa_t, ka_t3, va_t3, dya, dya_t, lse_a, dl_a,
                                                                    (g_wb, g_wout, g_wmi, g_wmo))
    dqb, dkb_t, dvb_t, dsink = _attn_b_bwd(qb, kbp, vbp, sink2, dyb, lse_b, dl_b)
    dkb = dkb_t.transpose(0, 2, 1).reshape(S + 2 * WIN, 128)[WIN:WIN + S]
    dvb = dvb_t.transpose(0, 2, 1).reshape(S + 2 * WIN, 128)[WIN:WIN + S]
    dproj, stqk = _qk_bwd(dqa_t, dka_t3, dva_t3, dqb, dkb, dvb, qar, kar, qg2, kg2, tab_a, tab_b, dga, dgb)
    g_win = _tn_matmul(h, dproj, D, 896, "dw_in")
    g_win = g_win.reshape(D, N_DEV, NI).transpose(1, 0, 2)
    grad_x, st1, r_win = _in_bwd(dproj, win, xs, dx1, modv, n1g, g_win.astype(BF16))

    def adam(name, own, recv, w, m, v):
        shape = w.shape
        w2, m2_, v2 = (a.reshape(own.shape[1:]) for a in (w, m, v))
        outs = _adamw_sum([(own, me1)] + [(recv, k) for k in range(N_DEV - 1)], w2, m2_, v2, name)
        return [a.reshape(shape) for a in outs]

    o_win = adam("adamw_w_in", g_win, r_win, w_in, m_w_in, v_w_in)
    o_wb = adam("adamw_w_branch", g_wb, r_wb, w_branch, m_w_branch, v_w_branch)
    o_wout = adam("adamw_w_out", g_wout, r_wout, w_out, m_w_out, v_w_out)
    o_wmi = adam("adamw_w_mlp_in", g_wmi, r_wmi, w_mlp_in, m_w_mlp_in, v_w_mlp_in)
    o_wmo = adam("adamw_w_mlp_out", g_wmo, r_wmo, w_mlp_out, m_w_mlp_out, v_w_mlp_out)

    small = _pack_small(st1, st2, stf, dg1, dg2, stqk, dsink, modv, n1g, n2g)
    small_all = _small_gather(small)
    sw = _pack_small_params(b_ada, norm1_g, norm2_g, final_g, q_norm_a, k_norm_a, sink_b)
    sm = _pack_small_params(m_b_ada, m_norm1_g, m_norm2_g, m_final_g, m_q_norm_a, m_k_norm_a, m_sink_b)
    sv = _pack_small_params(v_b_ada, v_norm1_g, v_norm2_g, v_final_g, v_q_norm_a, v_k_norm_a, v_sink_b)
    sm_out = _adamw_sum([(small_all, k) for k in range(N_DEV)], sw, sm, sv, "adamw_small")
    loss = sm_out[0][11, 0]
    sm_out = [_unpack_small(a) for a in sm_out]

    dmod_all = small_all[:, 0:6, :].reshape(N_DEV, 6 * D)
    dmod_cols = lax.dynamic_slice_in_dim(dmod_all, me * NW, NW, axis=1)
    g_wada = _wada_grad(silu_all, dmod_cols)
    ada = _adamw_sum([(g_wada, None)], w_ada.reshape(D, NW), m_w_ada.reshape(D, NW), v_w_ada.reshape(D, NW), "adamw_ada")
    ada = [a.reshape(1, D, NW) for a in ada]

    def leaves(k):
        b_, n1_, qn_, kn_, sk_, n2_, fg_ = sm_out[k]
        return [ada[k], b_, n1_, o_win[k], qn_, kn_, sk_, o_wb[k], o_wout[k], n2_, o_wmi[k], o_wmo[k], fg_]

    return (loss, grad_x.reshape(1, S, D), *leaves(0), *leaves(1), *leaves(2), *leaves(3))
```

```python
import jax
import jax.numpy as jnp
from jax import lax
from jax.experimental import pallas as pl
from jax.experimental.pallas import tpu as pltpu

F32, BF16 = jnp.float32, jnp.bfloat16
MESH = pl.DeviceIdType.MESH

D = 1024
HD = 64
NH = 8
GRP = 4
BW = 512
FF = 4096
INW = 3584
GRID_W = 64
WIN = 128
THETA = 10000.0
EPS = 1e-6
NEG = -1e30
N_DEV = 8
LOG2E = 1.4426950408889634
LN2 = 0.6931471805599453
QA_SCALE = 0.125 * LOG2E
SMALL_ROWS = 16
MLP_SHARDS = 4
MLP_ROWS = 512
TK_A = 512
TK_A_FWD = 2048
V7X_VMEM_LIMIT = 56 * 1024 * 1024

ADAM_LR, ADAM_B1, ADAM_B2, ADAM_EPS, ADAM_WD, ADAM_STEP = 0.001, 0.9, 0.999, 1e-08, 0.01, 10

NT = (((1,), (1,)), ((), ()))
TN = (((0,), (0,)), ((), ()))


def _params(n_axes, vmem=V7X_VMEM_LIMIT):
    return pltpu.CompilerParams(dimension_semantics=("arbitrary",) * n_axes, vmem_limit_bytes=vmem)


def _const(shape):
    return pl.BlockSpec(shape, lambda *_: (0,) * len(shape))


def _rows(tm, width):
    return pl.BlockSpec((tm, width), lambda i, *_: (i, 0))


def _seg_matrix(n, seg):
    r = lax.broadcasted_iota(jnp.int32, (n, n), 0) // seg
    c = lax.broadcasted_iota(jnp.int32, (n, n), 1) // seg
    return (r == c).astype(BF16)


def _seg_sum(z, seg_mat):
    hi = z.astype(BF16)
    lo = (z - hi.astype(F32)).astype(BF16)
    return jnp.dot(hi, seg_mat, preferred_element_type=F32) + jnp.dot(lo, seg_mat, preferred_element_type=F32)


def _rope(z, t_ref, sh):
    return z * t_ref[0] + pltpu.roll(z, sh, 1) * t_ref[1] + pltpu.roll(z, 128 - sh, 1) * t_ref[2]


def _rope_t(dz, t_ref, sh):
    return dz * t_ref[0] + pltpu.roll(dz * t_ref[1], 128 - sh, 1) + pltpu.roll(dz * t_ref[2], sh, 1)


def _rope_tables(S):
    t = jnp.arange(S, dtype=jnp.int32)[:, None]
    lane = jnp.arange(128, dtype=jnp.int32)[None, :] % HD

    def build(ang, first):
        cos, sin = jnp.cos(ang), jnp.sin(ang)
        return jnp.stack([cos, jnp.where(first, 0.0, sin), jnp.where(first, -sin, 0.0)]).astype(F32)

    inv_a = (THETA ** (-jnp.arange(0, HD // 2, 2, dtype=F32) / (HD // 2)))[lane % 16]
    pos_a = jnp.where(lane < HD // 2, t // GRID_W, t % GRID_W).astype(F32)
    tab_a = build(pos_a * inv_a, (lane % 32) < 16)
    inv_b = (THETA ** (-jnp.arange(0, HD, 2, dtype=F32) / HD))[lane % 32]
    tab_b = build(t.astype(F32) * inv_b, lane < 32)
    return tab_a, tab_b


def _in_proj(x, modv, n1g, win, qg2, kg2, tab_a, tab_b):
    S = x.shape[0]
    tm = min(512, S)
    tk = min(TK_A, S)
    per = tk // tm
    ts = min(256, tm)

    def body(x_ref, mod_ref, g_ref, w_ref, qg_ref, kg_ref, ta_ref, tb_ref,
             h_ref, qar_ref, kar_ref, qa_ref, ka_ref, va_ref, qb_ref, kb_ref, vb_ref, ga_ref, gb_ref, qat_ref, kat_ref, vat_ref):
        seg = _seg_matrix(128, HD)

        def head_norm(z, g):
            ms = _seg_sum(z * z, seg) * (1.0 / HD)
            return (z * lax.rsqrt(ms + EPS)) * g

        subs = [slice(ts * u, ts * u + ts) for u in range(tm // ts)]
        hbs = []
        for rows in subs:
            xt = x_ref[rows, :]
            r = lax.rsqrt(jnp.mean(xt * xt, axis=-1, keepdims=True) + EPS)
            h = ((xt * r) * g_ref[...]) * (1.0 + mod_ref[1:2, :]) + mod_ref[0:1, :]
            hbs.append(h.astype(BF16))
            h_ref[rows, :] = hbs[-1]
        projs = [jnp.dot(hb, w_ref[...], preferred_element_type=F32) for hb in hbs]
        for rows, proj in zip(subs, projs):
            ta, tb = ta_ref[:, rows, :], tb_ref[:, rows, :]
            for p in range(4):
                z = proj[:, 128 * p:128 * p + 128]
                qar_ref[rows, 128 * p:128 * p + 128] = z.astype(BF16)
                qv = _rope(head_norm(z, qg_ref[...]), ta, 16) * QA_SCALE
                qa_ref[rows, 128 * p:128 * p + 128] = qv.astype(BF16)
                qat_ref[128 * p:128 * p + 128, rows] = qv.T.astype(BF16)
                zb = proj[:, 768 + 128 * p:768 + 128 * p + 128]
                qb_ref[rows, 128 * p:128 * p + 128] = (_rope(zb, tb, 32) * QA_SCALE).astype(BF16)
            z = proj[:, 512:640]
            kar_ref[rows, :] = z.astype(BF16)
            kv_ = _rope(head_norm(z, kg_ref[...]), ta, 16)
            ka_ref[rows, :] = kv_.astype(BF16)
            kat_ref[:, rows] = kv_.T.astype(BF16)
            va_ref[rows, :] = proj[:, 640:768].astype(BF16)
            vat_ref[:, rows] = proj[:, 640:768].T.astype(BF16)
            kb_ref[rows, :] = _rope(proj[:, 1280:1408], tb, 32).astype(BF16)
            vb_ref[rows, :] = proj[:, 1408:1536].astype(BF16)
            ga_ref[rows, :] = proj[:, 1536:2560].astype(BF16)
            gb_ref[rows, :] = proj[:, 2560:3584].astype(BF16)

    tab = pl.BlockSpec((3, tm, 128), lambda i: (0, i, 0))
    shapes = [(D, BF16), (BW, BF16), (128, BF16), (BW, BF16), (128, BF16), (128, BF16),
              (BW, BF16), (128, BF16), (128, BF16), (D, BF16), (D, BF16)]
    return pl.pallas_call(
        body, grid=(S // tm,),
        in_specs=[_rows(tm, D), _const((6, D)), _const((1, D)), _const((D, INW)), _const((1, 128)), _const((1, 128)), tab, tab],
        out_specs=[_rows(tm, w) for w, _ in shapes] + [pl.BlockSpec((BW, tm), lambda i: (0, i))]
        + [pl.BlockSpec((None, 128, tm), lambda i: (i // per, 0, i % per))] * 2,
        out_shape=[jax.ShapeDtypeStruct((S, w), dt) for w, dt in shapes] + [jax.ShapeDtypeStruct((BW, S), BF16)]
        + [jax.ShapeDtypeStruct((S // tk, 128, tk), BF16)] * 2,
        compiler_params=_params(1), name="in_proj",
    )(x, modv, n1g, win, qg2, kg2, tab_a, tab_b)


def _exchange_gather(block_refs, out_refs, send_sems, recv_sems, local_sems):
    x, y, c = _me()
    me = 4 * x + 2 * y + c

    def copies():
        own, out, arrive = [], [], []
        for a, (blk, dst) in enumerate(zip(block_refs, out_refs)):
            own.append(pltpu.make_async_copy(blk, dst.at[me], local_sems.at[a]))
            for k in range(1, N_DEV):
                sems = dict(send_sem=send_sems.at[a, k - 1], recv_sem=recv_sems.at[a, k - 1], device_id=_peer(k), device_id_type=MESH)
                out.append(pltpu.make_async_remote_copy(blk, dst.at[me], **sems))
                arrive.append(pltpu.make_async_remote_copy(blk, dst.at[me ^ k], **sems))
        return own, out, arrive

    def start():
        own, out, _ = copies()
        for cp in own + out:
            cp.start()

    def finish():
        own, out, arrive = copies()
        for cp in arrive:
            cp.wait_recv()
        for cp in out:
            cp.wait_send()
        for cp in own:
            cp.wait()

    return start, finish


def _exchange_scatter(chunk_refs, recv_refs, send_sems, recv_sems):
    x, y, c = _me()
    me = 4 * x + 2 * y + c

    def copies():
        return [pltpu.make_async_remote_copy(src.at[me ^ k], dst.at[k - 1], send_sems.at[a, k - 1], recv_sems.at[a, k - 1],
                                             device_id=_peer(k), device_id_type=MESH)
                for a, (src, dst) in enumerate(zip(chunk_refs, recv_refs)) for k in range(1, N_DEV)]

    def start():
        for cp in copies():
            cp.start()

    def finish():
        cps = copies()
        for cp in cps:
            cp.wait_recv()
        for cp in cps:
            cp.wait_send()

    return start, finish


def _exchange_sems(n):
    return [pltpu.SemaphoreType.DMA((n, N_DEV - 1)), pltpu.SemaphoreType.DMA((n, N_DEV - 1))]


def _attn_a_fwd(qt, k, vt3, shards):
    S = qt.shape[1]
    tq = min(512, S)
    nq = S // tq
    per = max(1, min(TK_A_FWD, S) // vt3.shape[2])
    tk = per * vt3.shape[2]
    nk = S // tk
    ONES = 16
    AHEAD = 2
    ns = len(shards)

    def body(q_ref, k_ref, vt_ref, *rest):
        w_hbm, (o_ref, lse_ref), wall_hbm = rest[:ns], rest[ns:ns + 2], rest[ns + 2:2 * ns + 2]
        st_sc, send_sems, recv_sems, local_sems = rest[2 * ns + 2:]
        start, finish = _exchange_gather(w_hbm, wall_hbm, send_sems, recv_sems, local_sems)
        pl.when(pl.program_id(0) == 0)(start)
        row8 = lax.broadcasted_iota(jnp.int32, (NH, tq), 0)
        lse_all = jnp.zeros((NH, tq), F32)
        ones = jnp.ones((ONES, tk), BF16)
        for kv in range(2):
            qts = [q_ref[HD * (GRP * kv + t):HD * (GRP * kv + t) + HD, :] for t in range(GRP)]

            def keys(j, kv=kv):
                off = j * tk if isinstance(j, int) else pl.multiple_of(j * tk, tk)
                return k_ref[pl.ds(off, tk), :][:, HD * kv:HD * kv + HD]

            def scores(kj, t, qts=qts):
                return jnp.dot(kj, qts[t], preferred_element_type=F32)

            def step(j, carry, kv=kv, last=False):
                kj = keys(j)
                kn = None if last else keys(j + 1)
                vt = jnp.concatenate([vt_ref[per * j + u, HD * kv:HD * kv + HD, :] for u in range(per)], axis=1)
                v1 = jnp.concatenate([vt, ones], axis=0)
                sts = [st_sc[t] for t in range(AHEAD)]
                new = []
                for t in range(GRP):
                    m, acc = carry[2 * t], carry[2 * t + 1]
                    if t + AHEAD < GRP:
                        sts.append(scores(kj, t + AHEAD))
                    st = sts[t]
                    mn = jnp.maximum(m, jnp.max(st, axis=0, keepdims=True))
                    pt = jnp.exp2(st - mn)
                    if t + AHEAD >= GRP and not last:
                        st_sc[t + AHEAD - GRP] = scores(kn, t + AHEAD - GRP)
                    acc = jnp.exp2(m - mn) * acc + jnp.dot(v1, pt.astype(BF16), preferred_element_type=F32)
                    new += [mn, acc]
                return tuple(new)

            k0 = keys(0)
            for t in range(AHEAD):
                st_sc[t] = scores(k0, t)
            init = (jnp.full((1, tq), NEG, F32), jnp.zeros((HD + ONES, tq), F32)) * GRP
            res = step(nk - 1, lax.fori_loop(0, nk - 1, step, init), last=True)
            outs = []
            for t in range(GRP):
                m, acc = res[2 * t], res[2 * t + 1]
                l = acc[HD:HD + 1, :]
                outs.append((acc[:HD, :] / l).T)
                lse_all = jnp.where(row8 == GRP * kv + t, m + jnp.log2(l), lse_all)
            o_ref[:, 256 * kv:256 * kv + 256] = jnp.concatenate(outs, axis=1).astype(BF16)
        lse_ref[...] = lse_all
        pl.when(pl.program_id(0) == nq - 1)(finish)

    any_spec = pl.BlockSpec(memory_space=pl.ANY)
    return pl.pallas_call(
        body, grid=(nq,),
        in_specs=[pl.BlockSpec((BW, tq), lambda i: (0, i)), _const((S, 128)), _const(vt3.shape)] + [any_spec] * ns,
        out_specs=[_rows(tq, BW), pl.BlockSpec((NH, tq), lambda i: (0, i))] + [any_spec] * ns,
        out_shape=[jax.ShapeDtypeStruct((S, BW), BF16), jax.ShapeDtypeStruct((NH, S), F32)]
        + [jax.ShapeDtypeStruct((N_DEV,) + s.shape, s.dtype) for s in shards],
        scratch_shapes=[pltpu.VMEM((AHEAD, tk, tq), F32)] + _exchange_sems(ns) + [pltpu.SemaphoreType.DMA((ns,))],
        compiler_params=_params(1), name="attn_a_fwd",
    )(qt, k, vt3, *shards)


def _window_mask(i, tq, S):
    W = tq + 2 * WIN
    r = lax.broadcasted_iota(jnp.int32, (tq, W), 0)
    c = lax.broadcasted_iota(jnp.int32, (tq, W), 1)
    kpos = i * tq - WIN + c
    return (jnp.abs(c - WIN - r) <= WIN) & (kpos >= 0) & (kpos < S)


TQ_B = 256


def _attn_b_fwd(q, kp, vpt3, sink2):
    S = q.shape[0]
    tq = min(TQ_B, S)
    W = tq + 2 * WIN
    nc = vpt3.shape[0]
    ONES = 16

    def body(q_ref, k_ref, vt_ref, sink_ref, o_ref, lse_ref):
        i = pl.program_id(0)
        off = pl.multiple_of(i * tq, tq)
        r = lax.broadcasted_iota(jnp.int32, (W, tq), 1)
        c = lax.broadcasted_iota(jnp.int32, (W, tq), 0)
        kpos = i * tq - WIN + c
        valid = (jnp.abs(c - WIN - r) <= WIN) & (kpos >= 0) & (kpos < S)
        kw = k_ref[pl.ds(off, W), :]
        vt = jnp.concatenate([vt_ref[i + half] for half in range(W // tq)], axis=1)
        ones = jnp.ones((ONES, W), BF16)
        row8 = lax.broadcasted_iota(jnp.int32, (NH, tq), 0)
        lse_all = jnp.zeros((NH, tq), F32)
        qs = []
        for p in range(4):
            qp = q_ref[:, 128 * p:128 * p + 128]
            qs += [qp[:, :HD], qp[:, HD:]]
        khs = [kw[:, HD * kv:HD * kv + HD] for kv in range(2)]
        v1s = [jnp.concatenate([vt[HD * kv:HD * kv + HD, :], ones], axis=0) for kv in range(2)]

        def scores(h):
            return lax.dot_general(khs[h // GRP], qs[h], NT, preferred_element_type=F32)

        ss = [scores(0), scores(1)]
        outs = []
        for h in range(NH):
            if h + 2 < NH:
                ss.append(scores(h + 2))
            st = jnp.where(valid, ss[h], NEG)
            sk = sink_ref[:, h:h + 1]
            m = jnp.maximum(jnp.max(st, axis=0, keepdims=True), sk)
            acc = jnp.dot(v1s[h // GRP], jnp.exp2(st - m).astype(BF16), preferred_element_type=F32)
            l = acc[HD:HD + 1, :] + jnp.exp2(sk - m)
            outs.append((acc[:HD, :] / l).T)
            lse_all = jnp.where(row8 == h, m + jnp.log2(l), lse_all)
        for p in range(4):
            o_ref[:, 128 * p:128 * p + 128] = jnp.concatenate(outs[2 * p:2 * p + 2], axis=1).astype(BF16)
        lse_ref[...] = lse_all

    return pl.pallas_call(
        body, grid=(S // tq,),
        in_specs=[_rows(tq, BW), _const((S + 2 * WIN, 128)), _const((nc, 128, tq)), _const((1, NH))],
        out_specs=[_rows(tq, BW), pl.BlockSpec((NH, tq), lambda i: (0, i))],
        out_shape=[jax.ShapeDtypeStruct((S, BW), BF16), jax.ShapeDtypeStruct((NH, S), F32)],
        compiler_params=_params(1), name="attn_b_fwd",
    )(q, kp, vpt3, sink2)


def _merge_out(ya, yb, ga, gb, x, modv, wb, wout):
    S = x.shape[0]
    tm = min(256, S)

    def body(ya_ref, yb_ref, ga_ref, gb_ref, x_ref, mod_ref, wb_ref, wo_ref, x1_ref, mg_ref, ua_ref, ub_ref):
        ua = jnp.dot(ya_ref[...], wb_ref[0], preferred_element_type=F32)
        ub = jnp.dot(yb_ref[...], wb_ref[1], preferred_element_type=F32)
        merged = jax.nn.sigmoid(ga_ref[...].astype(F32)) * ua + jax.nn.sigmoid(gb_ref[...].astype(F32)) * ub
        mb = merged.astype(BF16)
        ua_ref[...] = ua.astype(BF16)
        ub_ref[...] = ub.astype(BF16)
        mg_ref[...] = mb
        x1_ref[...] = x_ref[...] + mod_ref[2:3, :] * jnp.dot(mb, wo_ref[...], preferred_element_type=F32)

    return pl.pallas_call(
        body, grid=(S // tm,),
        in_specs=[_rows(tm, BW), _rows(tm, BW), _rows(tm, D), _rows(tm, D), _rows(tm, D), _const((6, D)),
                  _const((2, BW, D)), _const((D, D))],
        out_specs=[_rows(tm, D)] * 4,
        out_shape=[jax.ShapeDtypeStruct((S, D), F32)] + [jax.ShapeDtypeStruct((S, D), BF16)] * 3,
        compiler_params=_params(1), name="merge_out",
    )(ya, yb, ga, gb, x, modv, wb, wout)


def _mlp_fwd(x1, modv, n2g, wmi, wmo, fg, target):
    S = x1.shape[0]
    tm = min(MLP_ROWS, S)
    tf = wmi.shape[2]
    nj = wmi.shape[0] // MLP_SHARDS

    def body(x1_ref, mod_ref, g_ref, wi_ref, wo_ref, fg_ref, t_ref, h2_ref, hp_ref, dx2_ref, st_ref, acc_ref):
        i, j = pl.program_id(0), pl.program_id(1)

        @pl.when(j == 0)
        def _():
            xt = x1_ref[...]
            r = lax.rsqrt(jnp.mean(xt * xt, axis=-1, keepdims=True) + EPS)
            h2 = ((xt * r) * g_ref[...]) * (1.0 + mod_ref[4:5, :]) + mod_ref[3:4, :]
            h2_ref[...] = h2.astype(BF16)
            acc_ref[...] = jnp.zeros_like(acc_ref)

        @pl.when((i == 0) & (j == 0))
        def _():
            st_ref[...] = jnp.zeros_like(st_ref)

        out = None
        for u in range(MLP_SHARDS):
            hp = jnp.dot(h2_ref[...], wi_ref[u], preferred_element_type=F32)
            hp_ref[:, tf * u:tf * u + tf] = hp.astype(BF16)
            hid = jnp.square(jnp.maximum(hp, 0.0))
            part = jnp.dot(hid.astype(BF16), wo_ref[u], preferred_element_type=F32)
            out = part if out is None else out + part
        acc_ref[...] += out

        @pl.when(j == nj - 1)
        def _():
            x2 = x1_ref[...] + mod_ref[5:6, :] * acc_ref[...]
            r3 = lax.rsqrt(jnp.mean(x2 * x2, axis=-1, keepdims=True) + EPS)
            xn = x2 * r3
            err = xn * fg_ref[...] - t_ref[...]
            dy = err * (1.0 / D)
            gy = dy * fg_ref[...]
            dx2_ref[...] = r3 * (gy - xn * jnp.mean(gy * xn, axis=-1, keepdims=True))
            st_ref[0:1, :] += jnp.sum(dy * xn, axis=0, keepdims=True)
            st_ref[1:2, :] += jnp.sum(err * err, axis=0, keepdims=True) * (0.5 / D)

    return pl.pallas_call(
        body, grid=(S // tm, nj),
        in_specs=[pl.BlockSpec((tm, D), lambda i, j: (i, 0)), _const((6, D)), _const((1, D)),
                  pl.BlockSpec((MLP_SHARDS, D, tf), lambda i, j: (j, 0, 0)), pl.BlockSpec((MLP_SHARDS, tf, D), lambda i, j: (j, 0, 0)),
                  _const((1, D)), pl.BlockSpec((tm, D), lambda i, j: (i, 0))],
        out_specs=[pl.BlockSpec((tm, D), lambda i, j: (i, 0)), pl.BlockSpec((tm, MLP_SHARDS * tf), lambda i, j: (i, j)),
                   pl.BlockSpec((tm, D), lambda i, j: (i, 0)), _const((8, D))],
        out_shape=[jax.ShapeDtypeStruct((S, D), BF16), jax.ShapeDtypeStruct((S, wmi.shape[0] * tf), BF16),
                   jax.ShapeDtypeStruct((S, D), F32), jax.ShapeDtypeStruct((8, D), F32)],
        scratch_shapes=[pltpu.VMEM((tm, D), F32)],
        compiler_params=_params(2), name="mlp_fwd",
    )(x1, modv, n2g, wmi, wmo, fg, target)


def _mlp_bwd(dx2, x1, hp, modv, n2g, wmi, wmo):
    S = x1.shape[0]
    tm = min(MLP_ROWS, S)
    tf = wmi.shape[2]
    nj = wmi.shape[0] // MLP_SHARDS

    def body(dx2_ref, x1_ref, hp_ref, mod_ref, g_ref, wi_ref, wo_ref, dhp_ref, dx1_ref, st_ref, dmo_ref, acc_ref):
        i, j = pl.program_id(0), pl.program_id(1)

        @pl.when(j == 0)
        def _():
            dmo_ref[...] = (mod_ref[5:6, :] * dx2_ref[...]).astype(BF16)
            acc_ref[...] = jnp.zeros_like(acc_ref)

        @pl.when((i == 0) & (j == 0))
        def _():
            st_ref[...] = jnp.zeros_like(st_ref)

        out = None
        for u in range(MLP_SHARDS):
            sl = slice(tf * u, tf * u + tf)
            dhid = lax.dot_general(dmo_ref[...], wo_ref[u], NT, preferred_element_type=F32)
            dhp = (dhid * (2.0 * jnp.maximum(hp_ref[:, sl].astype(F32), 0.0))).astype(BF16)
            dhp_ref[:, sl] = dhp
            part = lax.dot_general(dhp, wi_ref[u], NT, preferred_element_type=F32)
            out = part if out is None else out + part
        acc_ref[...] += out

        @pl.when(j == nj - 1)
        def _():
            dh2 = acc_ref[...]
            xt = x1_ref[...]
            r = lax.rsqrt(jnp.mean(xt * xt, axis=-1, keepdims=True) + EPS)
            xn = xt * r
            st_ref[0:1, :] += jnp.sum(dh2, axis=0, keepdims=True)
            st_ref[1:2, :] += jnp.sum(dh2 * xn, axis=0, keepdims=True)
            dxn = dh2 * (g_ref[...] * (1.0 + mod_ref[4:5, :]))
            dx1_ref[...] = dx2_ref[...] + r * (dxn - xn * jnp.mean(dxn * xn, axis=-1, keepdims=True))

    return pl.pallas_call(
        body, grid=(S // tm, nj),
        in_specs=[pl.BlockSpec((tm, D), lambda i, j: (i, 0)), pl.BlockSpec((tm, D), lambda i, j: (i, 0)),
                  pl.BlockSpec((tm, MLP_SHARDS * tf), lambda i, j: (i, j)), _const((6, D)), _const((1, D)),
                  pl.BlockSpec((MLP_SHARDS, D, tf), lambda i, j: (j, 0, 0)), pl.BlockSpec((MLP_SHARDS, tf, D), lambda i, j: (j, 0, 0))],
        out_specs=[pl.BlockSpec((tm, MLP_SHARDS * tf), lambda i, j: (i, j)), pl.BlockSpec((tm, D), lambda i, j: (i, 0)), _const((8, D))],
        out_shape=[jax.ShapeDtypeStruct((S, wmi.shape[0] * tf), BF16), jax.ShapeDtypeStruct((S, D), F32),
                   jax.ShapeDtypeStruct((8, D), F32)],
        scratch_shapes=[pltpu.VMEM((tm, D), BF16), pltpu.VMEM((tm, D), F32)],
        compiler_params=_params(2), name="mlp_bwd",
    )(dx2, x1, hp, modv, n2g, wmi, wmo)


def _tn_matmul(a, b, tk, tn, name, relu_sq=False, dev_major=False, rows=1024):
    S, K = a.shape
    N = b.shape[1]
    ts = min(rows, S)
    ns = S // ts

    def body(a_ref, b_ref, o_ref):
        @pl.when(pl.program_id(2) == 0)
        def _():
            o_ref[...] = jnp.zeros_like(o_ref)

        at = a_ref[...]
        if relu_sq:
            at = jnp.square(jnp.maximum(at.astype(F32), 0.0)).astype(BF16)
        o_ref[...] += lax.dot_general(at, b_ref[...].astype(BF16), TN, preferred_element_type=F32)

    if dev_major:
        out_spec = pl.BlockSpec((None, tk, tn), lambda k, n, s: (n, k, 0))
        out_shape = jax.ShapeDtypeStruct((N // tn, K, tn), F32)
    else:
        out_spec = pl.BlockSpec((tk, tn), lambda k, n, s: (k, n))
        out_shape = jax.ShapeDtypeStruct((K, N), F32)
    return pl.pallas_call(
        body, grid=(K // tk, N // tn, ns),
        in_specs=[pl.BlockSpec((ts, tk), lambda k, n, s: (s, k)), pl.BlockSpec((ts, tn), lambda k, n, s: (s, n))],
        out_specs=out_spec, out_shape=out_shape,
        compiler_params=_params(3), name=name,
    )(a, b)


def _scale_gate(m, w, g, row, name):
    K = m.shape[0]
    tk = min(512, K)

    def body(m_ref, w_ref, mod_ref, dw_ref, dg_ref):
        @pl.when(pl.program_id(0) == 0)
        def _():
            dg_ref[...] = jnp.zeros_like(dg_ref)

        mt = m_ref[...]
        dw_ref[...] = mt * mod_ref[row:row + 1, :]
        dg_ref[0:1, :] += jnp.sum(mt * w_ref[...].astype(F32), axis=0, keepdims=True)

    return pl.pallas_call(
        body, grid=(K // tk,),
        in_specs=[_rows(tk, D), _rows(tk, D), _const((6, D))],
        out_specs=[_rows(tk, D), _const((8, D))],
        out_shape=[jax.ShapeDtypeStruct((K, D), F32), jax.ShapeDtypeStruct((8, D), F32)],
        compiler_params=_params(1), name=name,
    )(m, w, g)


def _merge_bwd(dx1, modv, ga, gb, ua, ub, ya, yb, wb, wout):
    S = dx1.shape[0]
    tm = min(512, S)

    def body(dx1_ref, mod_ref, ga_ref, gb_ref, ua_ref, ub_ref, ya_ref, yb_ref, wb_ref, wo_ref,
             dua_ref, dub_ref, dga_ref, dgb_ref, dya_ref, dyb_ref, dla_ref, dlb_ref, dyat_ref):
        dao = (mod_ref[2:3, :] * dx1_ref[...]).astype(BF16)
        dm = lax.dot_general(dao, wo_ref[...], NT, preferred_element_type=F32)
        r = lax.broadcasted_iota(jnp.int32, (BW, NH), 0) // HD
        c = lax.broadcasted_iota(jnp.int32, (BW, NH), 1)
        head_of = (r == c).astype(BF16)
        for br, (g_ref, u_ref, y_ref, du_ref, dg_ref, dy_ref, dl_ref) in enumerate((
                (ga_ref, ua_ref, ya_ref, dua_ref, dga_ref, dya_ref, dla_ref),
                (gb_ref, ub_ref, yb_ref, dub_ref, dgb_ref, dyb_ref, dlb_ref))):
            sg = jax.nn.sigmoid(g_ref[...].astype(F32))
            du = (dm * sg).astype(BF16)
            du_ref[...] = du
            dg_ref[...] = (dm * u_ref[...].astype(F32) * sg * (1.0 - sg)).astype(BF16)
            dy = lax.dot_general(du, wb_ref[br], NT, preferred_element_type=F32)
            dyb16 = dy.astype(BF16)
            dy_ref[...] = dyb16
            if br == 0:
                dyat_ref[...] = dy.T.astype(BF16)
            prod = dyb16.astype(F32) * y_ref[...].astype(F32)
            hi = prod.astype(BF16)
            lo = (prod - hi.astype(F32)).astype(BF16)
            dl_ref[...] = (jnp.dot(hi, head_of, preferred_element_type=F32) + jnp.dot(lo, head_of, preferred_element_type=F32))

    return pl.pallas_call(
        body, grid=(S // tm,),
        in_specs=[_rows(tm, D), _const((6, D)), _rows(tm, D), _rows(tm, D), _rows(tm, D), _rows(tm, D),
                  _rows(tm, BW), _rows(tm, BW), _const((2, BW, D)), _const((D, D))],
        out_specs=[_rows(tm, D)] * 4 + [_rows(tm, BW)] * 2 + [_rows(tm, NH)] * 2 + [pl.BlockSpec((BW, tm), lambda i: (0, i))],
        out_shape=[jax.ShapeDtypeStruct((S, D), BF16)] * 4 + [jax.ShapeDtypeStruct((S, BW), BF16)] * 2
        + [jax.ShapeDtypeStruct((S, NH), F32)] * 2 + [jax.ShapeDtypeStruct((BW, S), BF16)],
        compiler_params=_params(1), name="merge_bwd",
    )(dx1, modv, ga, gb, ua, ub, ya, yb, wb, wout)


def _attn_a_bwd(q, qt, kt3, vt3, do, dot_, lse, delta, chunks):
    S = q.shape[0]
    tq = min(1024, S)
    nk, _, tk = kt3.shape
    nq = S // tq
    nc = len(chunks)

    def body(q_ref, qt_ref, do_ref, dot_ref, lse_ref, dl_ref, kt_ref, vt_ref, *rest):
        g_hbm, (dq_ref, dk_hbm, dv_hbm), recv_hbm = rest[:nc], rest[nc:nc + 3], rest[nc + 3:2 * nc + 3]
        dk_sc, dv_sc, sem, send_sems, recv_sems = rest[2 * nc + 3:]
        i = pl.program_id(0)
        start, finish = _exchange_scatter(g_hbm, recv_hbm, send_sems, recv_sems)

        @pl.when(i == 0)
        def _():
            start()
            dk_sc[...] = jnp.zeros_like(dk_sc)
            dv_sc[...] = jnp.zeros_like(dv_sc)

        for kv in range(2):
            qg = q_ref[:, 256 * kv:256 * kv + 256]
            dog = do_ref[:, 256 * kv:256 * kv + 256]
            heads = []
            for t in range(GRP):
                h = GRP * kv + t
                heads.append((qg[:, HD * t:HD * t + HD], dog[:, HD * t:HD * t + HD],
                              qt_ref[HD * h:HD * h + HD, :], dot_ref[HD * h:HD * h + HD, :],
                              lse_ref[:, h:h + 1], dl_ref[:, h:h + 1]))

            q2 = [jnp.concatenate([heads[2 * u][0], heads[2 * u + 1][0]], axis=0) for u in range(GRP // 2)]
            do2 = [jnp.concatenate([heads[2 * u][1], heads[2 * u + 1][1]], axis=0) for u in range(GRP // 2)]

            def step(j, carry, kv=kv, heads=heads, q2=q2, do2=do2):
                kjt = kt_ref[j, HD * kv:HD * kv + HD, :]
                vjt = vt_ref[j, HD * kv:HD * kv + HD, :]
                dkt = jnp.zeros((HD, tk), F32)
                dvt = jnp.zeros((HD, tk), F32)
                new = []

                def logits(u):
                    return (jnp.dot(q2[u], kjt, preferred_element_type=F32), jnp.dot(do2[u], vjt, preferred_element_type=F32))

                sd = [logits(0)]
                for t, (qh, doh, qth, doth, lse_h, dl_h) in enumerate(heads):
                    if t == 0:
                        sd.append(logits(1))
                    rows = slice(tq * (t % 2), tq * (t % 2) + tq)
                    s, dp = sd[t // 2][0][rows, :], sd[t // 2][1][rows, :]
                    pm = jnp.exp2(s - lse_h)
                    ds = (pm * (dp - dl_h)).astype(BF16)
                    dvt = dvt + jnp.dot(doth, pm.astype(BF16), preferred_element_type=F32)
                    dkt = dkt + jnp.dot(qth, ds, preferred_element_type=F32)
                    new.append(carry[t] + lax.dot_general(kjt, ds, NT, preferred_element_type=F32))
                dk_sc[j, HD * kv:HD * kv + HD, :] += dkt
                dv_sc[j, HD * kv:HD * kv + HD, :] += dvt
                return tuple(new)

            res = lax.fori_loop(0, nk, step, (jnp.zeros((HD, tq), F32),) * GRP)
            for t in range(GRP):
                dq_ref[HD * (GRP * kv + t):HD * (GRP * kv + t) + HD, :] = res[t]

        @pl.when(i == nq - 1)
        def _():
            c1 = pltpu.make_async_copy(dk_sc, dk_hbm, sem.at[0])
            c2 = pltpu.make_async_copy(dv_sc, dv_hbm, sem.at[1])
            c1.start()
            c2.start()
            c1.wait()
            c2.wait()
            finish()

    any_spec = pl.BlockSpec(memory_space=pl.ANY)
    cols = pl.BlockSpec((BW, tq), lambda i: (0, i))
    return pl.pallas_call(
        body, grid=(nq,),
        in_specs=[_rows(tq, BW), cols, _rows(tq, BW), cols, _rows(tq, NH), _rows(tq, NH), _const((nk, 128, tk)),
                  _const((nk, 128, tk))] + [any_spec] * nc,
        out_specs=[cols, any_spec, any_spec] + [any_spec] * nc,
        out_shape=[jax.ShapeDtypeStruct((BW, S), F32), jax.ShapeDtypeStruct((nk, 128, tk), F32),
                   jax.ShapeDtypeStruct((nk, 128, tk), F32)]
        + [jax.ShapeDtypeStruct((N_DEV - 1,) + c.shape[1:], c.dtype) for c in chunks],
        scratch_shapes=[pltpu.VMEM((nk, 128, tk), F32), pltpu.VMEM((nk, 128, tk), F32), pltpu.SemaphoreType.DMA((2,))]
        + _exchange_sems(nc),
        compiler_params=_params(1), name="attn_a_bwd",
    )(q, qt, do, dot_, lse, delta, kt3, vt3, *chunks)


def _attn_b_bwd(q, kp, vp, sink2, do, lse, delta):
    S = q.shape[0]
    tq = min(TQ_B, S)
    W = tq + 2 * WIN
    nq = S // tq
    nc = (S + 2 * WIN) // tq

    def body(q_ref, k_ref, v_ref, sink_ref, do_ref, lse_ref, dl_ref, dq_ref, dk_hbm, dv_hbm, ds_ref, dk_sc, dv_sc, sem):
        i = pl.program_id(0)

        @pl.when(i == 0)
        def _():
            dk_sc[...] = jnp.zeros_like(dk_sc)
            dv_sc[...] = jnp.zeros_like(dv_sc)
            ds_ref[...] = jnp.zeros_like(ds_ref)

        off = pl.multiple_of(i * tq, tq)
        valid = _window_mask(i, tq, S)
        kw = k_ref[pl.ds(off, W), :]
        vw = v_ref[pl.ds(off, W), :]
        lse_i = lse_ref[...]
        dl_i = dl_ref[...]
        qa = q_ref[...]
        doa = do_ref[...]
        qt = qa.astype(F32).T.astype(BF16)
        dot_ = doa.astype(F32).T.astype(BF16)
        khs = [kw[:, HD * kv:HD * kv + HD] for kv in range(2)]
        vhs = [vw[:, HD * kv:HD * kv + HD] for kv in range(2)]

        def logits(h):
            return (lax.dot_general(qa[:, HD * h:HD * h + HD], khs[h // GRP], NT, preferred_element_type=F32),
                    lax.dot_general(doa[:, HD * h:HD * h + HD], vhs[h // GRP], NT, preferred_element_type=F32))

        sd = [logits(0)]
        dqs = []
        dkt = [jnp.zeros((HD, W), F32), jnp.zeros((HD, W), F32)]
        dvt = [jnp.zeros((HD, W), F32), jnp.zeros((HD, W), F32)]
        for h in range(NH):
            kv = h // GRP
            if h + 1 < NH:
                sd.append(logits(h + 1))
            s, dp = sd[h]
            pm = jnp.exp2(jnp.where(valid, s, NEG) - lse_i[:, h:h + 1])
            ds = (pm * (dp - dl_i[:, h:h + 1])).astype(BF16)
            dvt[kv] = dvt[kv] + jnp.dot(dot_[HD * h:HD * h + HD, :], pm.astype(BF16), preferred_element_type=F32)
            dkt[kv] = dkt[kv] + jnp.dot(qt[HD * h:HD * h + HD, :], ds, preferred_element_type=F32)
            dqs.append(jnp.dot(ds, khs[kv], preferred_element_type=F32))
        for p in range(4):
            dq_ref[:, 128 * p:128 * p + 128] = jnp.concatenate(dqs[2 * p:2 * p + 2], axis=1)
        for half in range(W // tq):
            dk_sc[i + half] += jnp.concatenate([d[:, tq * half:tq * half + tq] for d in dkt], axis=0)
            dv_sc[i + half] += jnp.concatenate([d[:, tq * half:tq * half + tq] for d in dvt], axis=0)
        psd = jnp.exp2(sink_ref[...] - lse_i) * dl_i
        r = lax.broadcasted_iota(jnp.int32, (NH, 128), 0)
        c = lax.broadcasted_iota(jnp.int32, (NH, 128), 1)
        row = jnp.dot(jnp.sum(psd, axis=0, keepdims=True), (r == c).astype(F32),
                      preferred_element_type=F32, precision=lax.Precision.HIGHEST)
        ds_ref[...] -= jnp.broadcast_to(row, (8, 128))

        @pl.when(i == nq - 1)
        def _():
            c1 = pltpu.make_async_copy(dk_sc, dk_hbm, sem.at[0])
            c2 = pltpu.make_async_copy(dv_sc, dv_hbm, sem.at[1])
            c1.start()
            c2.start()
            c1.wait()
            c2.wait()

    any_spec = pl.BlockSpec(memory_space=pl.ANY)
    return pl.pallas_call(
        body, grid=(nq,),
        in_specs=[_rows(tq, BW), _const((S + 2 * WIN, 128)), _const((S + 2 * WIN, 128)), _const((1, NH)),
                  _rows(tq, BW), _rows(tq, NH), _rows(tq, NH)],
        out_specs=[_rows(tq, BW), any_spec, any_spec, _const((8, 128))],
        out_shape=[jax.ShapeDtypeStruct((S, BW), F32), jax.ShapeDtypeStruct((nc, 128, tq), F32),
                   jax.ShapeDtypeStruct((nc, 128, tq), F32), jax.ShapeDtypeStruct((8, 128), F32)],
        scratch_shapes=[pltpu.VMEM((nc, 128, tq), F32), pltpu.VMEM((nc, 128, tq), F32), pltpu.SemaphoreType.DMA((2,))],
        compiler_params=_params(1), name="attn_b_bwd",
    )(q, kp, vp, sink2, do, lse, delta)


def _qk_bwd(dqa_t, dka_t3, dva_t3, dqb, dkb, dvb, qar, kar, qg2, kg2, tab_a, tab_b, dga, dgb):
    S = dqb.shape[0]
    tm = min(256, S)
    per = dka_t3.shape[2] // tm

    def body(dqa_ref, dka_ref, dva_ref, dqb_ref, dkb_ref, dvb_ref, qar_ref, kar_ref, qg_ref, kg_ref, ta_ref, tb_ref,
             dga_ref, dgb_ref, dp_ref, st_ref):
        @pl.when(pl.program_id(0) == 0)
        def _():
            st_ref[...] = jnp.zeros_like(st_ref)

        seg = _seg_matrix(128, HD)

        def norm_bwd(dz_rot, raw, g):
            dzn = _rope_t(dz_rot, ta_ref, 16)
            raw = raw.astype(F32)
            rr = lax.rsqrt(_seg_sum(raw * raw, seg) * (1.0 / HD) + EPS)
            zhat = raw * rr
            dzh = dzn * g
            draw = rr * (dzh - zhat * (_seg_sum(dzh * zhat, seg) * (1.0 / HD)))
            return draw, jnp.sum(dzn * zhat, axis=0, keepdims=True)

        gq = jnp.zeros((1, 128), F32)
        for p in range(4):
            sl = slice(128 * p, 128 * p + 128)
            draw, gsum = norm_bwd(dqa_ref[sl, :].T * 0.125, qar_ref[:, sl], qg_ref[...])
            gq = gq + gsum
            dp_ref[:, sl] = draw.astype(BF16)
            dp_ref[:, 768 + 128 * p:768 + 128 * p + 128] = _rope_t(dqb_ref[:, sl] * 0.125, tb_ref, 32).astype(BF16)
        draw, gk = norm_bwd(dka_ref[...].T * LN2, kar_ref[...], kg_ref[...])
        dp_ref[:, 512:640] = draw.astype(BF16)
        dp_ref[:, 640:768] = dva_ref[...].T.astype(BF16)
        dp_ref[:, 1280:1408] = _rope_t(dkb_ref[...] * LN2, tb_ref, 32).astype(BF16)
        dp_ref[:, 1408:1536] = dvb_ref[...].astype(BF16)
        dp_ref[:, 1536:2560] = dga_ref[...]
        dp_ref[:, 2560:3584] = dgb_ref[...]
        st_ref[0:1, :] += gq
        st_ref[1:2, :] += gk

    tab = pl.BlockSpec((3, tm, 128), lambda i: (0, i, 0))
    chunk_t = pl.BlockSpec((None, 128, tm), lambda i: (i // per, 0, i % per))
    return pl.pallas_call(
        body, grid=(S // tm,),
        in_specs=[pl.BlockSpec((BW, tm), lambda i: (0, i)), chunk_t, chunk_t, _rows(tm, BW), _rows(tm, 128), _rows(tm, 128),
                  _rows(tm, BW), _rows(tm, 128), _const((1, 128)), _const((1, 128)), tab, tab, _rows(tm, D), _rows(tm, D)],
        out_specs=[_rows(tm, INW), _const((8, 128))],
        out_shape=[jax.ShapeDtypeStruct((S, INW), BF16), jax.ShapeDtypeStruct((8, 128), F32)],
        compiler_params=_params(1), name="qk_bwd",
    )(dqa_t, dka_t3, dva_t3, dqb, dkb, dvb, qar, kar, qg2, kg2, tab_a, tab_b, dga, dgb)


def _in_bwd(dproj, win, x, dx1, modv, n1g, chunks):
    S = x.shape[0]
    tm = min(512, S)
    n = S // tm
    ts = min(256, tm)

    def body(dp_ref, w_ref, x_ref, dx1_ref, mod_ref, g_ref, c_hbm, gx_ref, st_ref, recv_hbm, send_sems, recv_sems):
        start, finish = _exchange_scatter([c_hbm], [recv_hbm], send_sems, recv_sems)

        @pl.when(pl.program_id(0) == 0)
        def _():
            start()
            st_ref[...] = jnp.zeros_like(st_ref)

        subs = [slice(ts * u, ts * u + ts) for u in range(tm // ts)]
        dhs = [lax.dot_general(dp_ref[rows, :], w_ref[...], NT, preferred_element_type=F32) for rows in subs]
        for rows, dh in zip(subs, dhs):
            xt = x_ref[rows, :]
            r = lax.rsqrt(jnp.mean(xt * xt, axis=-1, keepdims=True) + EPS)
            xn = xt * r
            st_ref[0:1, :] += jnp.sum(dh, axis=0, keepdims=True)
            st_ref[1:2, :] += jnp.sum(dh * xn, axis=0, keepdims=True)
            dxn = dh * (g_ref[...] * (1.0 + mod_ref[1:2, :]))
            gx_ref[rows, :] = dx1_ref[rows, :] + r * (dxn - xn * jnp.mean(dxn * xn, axis=-1, keepdims=True))
        pl.when(pl.program_id(0) == n - 1)(finish)

    any_spec = pl.BlockSpec(memory_space=pl.ANY)
    return pl.pallas_call(
        body, grid=(n,),
        in_specs=[_rows(tm, INW), _const((D, INW)), _rows(tm, D), _rows(tm, D), _const((6, D)), _const((1, D)), any_spec],
        out_specs=[_rows(tm, D), _const((8, D)), any_spec],
        out_shape=[jax.ShapeDtypeStruct((S, D), F32), jax.ShapeDtypeStruct((8, D), F32),
                   jax.ShapeDtypeStruct((N_DEV - 1,) + chunks.shape[1:], chunks.dtype)],
        scratch_shapes=_exchange_sems(1),
        compiler_params=_params(1), name="in_bwd",
    )(dproj, win, x, dx1, modv, n1g, chunks)


def _pack_small(st1, st2, stf, dg1, dg2, stqk, dsink, modv, n1g, n2g):
    def body(st1_ref, st2_ref, stf_ref, dg1_ref, dg2_ref, qk_ref, ds_ref, mod_ref, g1_ref, g2_ref, o_ref):
        a1, b1 = st1_ref[0:1, :], st1_ref[1:2, :]
        a2, b2 = st2_ref[0:1, :], st2_ref[1:2, :]
        r = lax.broadcasted_iota(jnp.int32, (128, D), 0)
        c = lax.broadcasted_iota(jnp.int32, (128, D), 1)
        fold_q = (c == r % HD).astype(F32)
        fold_k = (c == HD + r % HD).astype(F32)
        keep = (c == r).astype(F32)

        def place(v, sel):
            return jnp.dot(v, sel, preferred_element_type=F32, precision=lax.Precision.HIGHEST)

        loss = jnp.sum(stf_ref[1:2, :], axis=1, keepdims=True)
        lane = lax.broadcasted_iota(jnp.int32, (1, D), 1)
        rows = [a1, g1_ref[...] * b1, dg1_ref[0:1, :], a2, g2_ref[...] * b2, dg2_ref[0:1, :],
                (1.0 + mod_ref[1:2, :]) * b1, (1.0 + mod_ref[4:5, :]) * b2, stf_ref[0:1, :],
                place(qk_ref[0:1, :], fold_q) + place(qk_ref[1:2, :], fold_k),
                place(ds_ref[0:1, :], keep),
                jnp.where(lane == 0, loss, 0.0)]
        rows += [jnp.zeros((1, D), F32)] * (SMALL_ROWS - len(rows))
        for n, v in enumerate(rows):
            o_ref[n:n + 1, :] = v

    return pl.pallas_call(
        body, out_shape=jax.ShapeDtypeStruct((SMALL_ROWS, D), F32),
        compiler_params=pltpu.CompilerParams(vmem_limit_bytes=V7X_VMEM_LIMIT), name="pack_small",
    )(st1, st2, stf, dg1, dg2, stqk, dsink, modv, n1g, n2g)


def _wada_grad(silu_all, dmod_cols):
    def body(a_ref, b_ref, o_ref):
        o_ref[...] = lax.dot_general(a_ref[...], b_ref[...], TN, preferred_element_type=F32, precision=lax.Precision.HIGHEST)

    return pl.pallas_call(
        body, out_shape=jax.ShapeDtypeStruct((D, dmod_cols.shape[1]), F32),
        compiler_params=pltpu.CompilerParams(vmem_limit_bytes=V7X_VMEM_LIMIT), name="wada_grad",
    )(silu_all, dmod_cols)


def _adamw_sum(parts, w, m, v, name):
    R, C = w.shape
    tr = R if R <= 64 else next(t for t in (256, 128, 64, 32, 16, 8) if R % t == 0)
    n = len(parts)
    dyn = [idx for _, idx in parts if idx is not None and not isinstance(idx, int)]
    b1c = 1.0 - ADAM_B1 ** ADAM_STEP
    b2c = 1.0 - ADAM_B2 ** ADAM_STEP

    def body(*refs):
        refs = refs[len(dyn):]
        g = refs[0][...].astype(F32)
        for k in range(1, n):
            g = g + refs[k][...].astype(F32)
        w_ref, m_ref, v_ref, g_out, d_out, m_out, v_out = refs[n:]
        mn = ADAM_B1 * m_ref[...] + (1.0 - ADAM_B1) * g
        vn = ADAM_B2 * v_ref[...] + (1.0 - ADAM_B2) * jnp.square(g)
        g_out[...] = g
        m_out[...] = mn
        v_out[...] = vn
        d_out[...] = -ADAM_LR * ((mn / b1c) / (jnp.sqrt(vn / b2c) + ADAM_EPS) + ADAM_WD * w_ref[...])

    in_specs = []
    nd = 0
    for a, idx in parts:
        if idx is None:
            in_specs.append(pl.BlockSpec((tr, C), lambda i, *s: (i, 0)))
        elif isinstance(idx, int):
            in_specs.append(pl.BlockSpec((None, tr, C), lambda i, *s, idx=idx: (idx, i, 0)))
        else:
            in_specs.append(pl.BlockSpec((None, tr, C), lambda i, *s, nd=nd: (s[nd][0], i, 0)))
            nd += 1
    blk = pl.BlockSpec((tr, C), lambda i, *s: (i, 0))
    grid_spec = pltpu.PrefetchScalarGridSpec(
        num_scalar_prefetch=len(dyn), grid=(R // tr,), in_specs=in_specs + [blk] * 3, out_specs=[blk] * 4)
    return pl.pallas_call(
        body, grid_spec=grid_spec, out_shape=[jax.ShapeDtypeStruct((R, C), F32)] * 4,
        compiler_params=_params(1), name=name,
    )(*dyn, *[a for a, _ in parts], w, m, v)


def _me():
    return lax.axis_index("x"), lax.axis_index("y"), lax.axis_index("c")


def _peer(k):
    x, y, c = _me()
    return (x ^ ((k >> 2) & 1), y ^ ((k >> 1) & 1), c ^ (k & 1))


def _ada_exchange(c_row, w_ada, b_rows):
    NW = w_ada.shape[1]

    def body(c_ref, w_ref, b_ref, sall_ref, mod_ref, src_ref, mp_ref, send1, recv1, send2, recv2):
        x, y, c = _me()
        me = 4 * x + 2 * y + c
        cv = c_ref[...]
        src_ref[...] = jnp.broadcast_to(cv * jax.nn.sigmoid(cv), (8, D))
        mine = pl.ds(pl.multiple_of(me * 8, 8), 8)
        sall_ref[mine, :] = src_ref[...]
        sends = [pltpu.make_async_remote_copy(src_ref, sall_ref.at[mine, :], send1.at[k - 1], recv1.at[k - 1],
                                              device_id=_peer(k), device_id_type=MESH) for k in range(1, N_DEV)]
        for cp in sends:
            cp.start()
        for k in range(1, N_DEV):
            theirs = pl.ds(pl.multiple_of((me ^ k) * 8, 8), 8)
            pltpu.make_async_remote_copy(src_ref, sall_ref.at[theirs, :], send1.at[k - 1], recv1.at[k - 1],
                                         device_id=_peer(k), device_id_type=MESH).wait_recv()
        for cp in sends:
            cp.wait_send()
        mp_ref[...] = jnp.dot(sall_ref[...], w_ref[...], preferred_element_type=F32, precision=lax.Precision.HIGHEST)
        mod_ref[mine, :] = mp_ref[mine, :] + b_ref[mine, :]
        sends = []
        for k in range(1, N_DEV):
            theirs = pl.ds(pl.multiple_of((me ^ k) * 8, 8), 8)
            sends.append(pltpu.make_async_remote_copy(mp_ref.at[theirs, :], mod_ref.at[mine, :], send2.at[k - 1], recv2.at[k - 1],
                                                      device_id=_peer(k), device_id_type=MESH))
        for cp in sends:
            cp.start()
        for k in range(1, N_DEV):
            theirs = pl.ds(pl.multiple_of((me ^ k) * 8, 8), 8)
            pltpu.make_async_remote_copy(mp_ref.at[mine, :], mod_ref.at[theirs, :], send2.at[k - 1], recv2.at[k - 1],
                                         device_id=_peer(k), device_id_type=MESH).wait_recv()
            mod_ref[theirs, :] = mod_ref[theirs, :] + b_ref[theirs, :]
        for cp in sends:
            cp.wait_send()

    vm = pl.BlockSpec(memory_space=pltpu.VMEM)
    return pl.pallas_call(
        body, in_specs=[vm, vm, vm], out_specs=[vm, vm],
        out_shape=[jax.ShapeDtypeStruct((8 * N_DEV, D), F32), jax.ShapeDtypeStruct((8 * N_DEV, NW), F32)],
        scratch_shapes=[pltpu.VMEM((8, D), F32), pltpu.VMEM((8 * N_DEV, NW), F32)]
        + [pltpu.SemaphoreType.DMA((N_DEV - 1,))] * 4,
        compiler_params=pltpu.CompilerParams(vmem_limit_bytes=V7X_VMEM_LIMIT), name="ada_exchange",
    )(c_row, w_ada, b_rows)


def _weight_gather(shard):
    def body(x_ref, out_ref, send_sems, recv_sems, local_sem):
        x, y, c = _me()
        me, sibling = (x, y, c), (x, y, 1 - c)
        chips = [(1 - x, y), (x, 1 - y), (1 - x, 1 - y)]

        def slot(px, py, pc):
            return out_ref.at[4 * px + 2 * py + pc]

        def copy(k, block, to, src=None):
            return pltpu.make_async_remote_copy(
                src_ref=slot(*block) if src is None else src, dst_ref=slot(*block),
                send_sem=send_sems.at[k], recv_sem=recv_sems.at[k], device_id=to, device_id_type=MESH)

        mine = pltpu.make_async_copy(x_ref, slot(*me), local_sem)
        mine.start()
        first = [copy(0, me, sibling, src=x_ref)]
        first += [copy(1 + j, me, (*chip, c), src=x_ref) for j, chip in enumerate(chips)]
        for cp in first:
            cp.start()
        passed = [copy(4 + j, (*chip, c), sibling) for j, chip in enumerate(chips)]
        for j, chip in enumerate(chips):
            copy(1 + j, (*chip, c), me).wait_recv()
            passed[j].start()
        copy(0, sibling, me).wait_recv()
        for j, chip in enumerate(chips):
            copy(4 + j, (*chip, 1 - c), me).wait_recv()
        for cp in first + passed:
            cp.wait_send()
        mine.wait()

    any_spec = pl.BlockSpec(memory_space=pl.ANY)
    return pl.pallas_call(
        body, in_specs=[any_spec], out_specs=any_spec,
        out_shape=jax.ShapeDtypeStruct((N_DEV,) + shard.shape, shard.dtype),
        scratch_shapes=[pltpu.SemaphoreType.DMA((7,)), pltpu.SemaphoreType.DMA((7,)), pltpu.SemaphoreType.DMA],
        name="weight_gather",
    )(shard)


def _small_gather(block):
    def body(b_ref, out_ref, send_sems, recv_sems):
        x, y, c = _me()
        me = 4 * x + 2 * y + c
        out_ref[me] = b_ref[...]
        sends = [pltpu.make_async_remote_copy(b_ref, out_ref.at[me], send_sems.at[k - 1], recv_sems.at[k - 1],
                                              device_id=_peer(k), device_id_type=MESH) for k in range(1, N_DEV)]
        for cp in sends:
            cp.start()
        for k in range(1, N_DEV):
            pltpu.make_async_remote_copy(b_ref, out_ref.at[me ^ k], send_sems.at[k - 1], recv_sems.at[k - 1],
                                         device_id=_peer(k), device_id_type=MESH).wait_recv()
        for cp in sends:
            cp.wait_send()

    vm = pl.BlockSpec(memory_space=pltpu.VMEM)
    return pl.pallas_call(
        body, in_specs=[vm], out_specs=vm,
        out_shape=jax.ShapeDtypeStruct((N_DEV,) + block.shape, block.dtype),
        scratch_shapes=[pltpu.SemaphoreType.DMA((N_DEV - 1,)), pltpu.SemaphoreType.DMA((N_DEV - 1,))],
        name="small_gather",
    )(block)


def _pack_small_params(b_ada, n1, n2, fg, qn, kn, sink):
    qk = jnp.concatenate([qn.reshape(1, HD), kn.reshape(1, HD), jnp.zeros((1, D - 2 * HD), F32)], axis=1)
    sk = jnp.concatenate([sink.reshape(1, NH), jnp.zeros((1, D - NH), F32)], axis=1)
    return jnp.concatenate([b_ada.reshape(6, D), n1.reshape(1, D), n2.reshape(1, D), fg.reshape(1, D), qk, sk,
                            jnp.zeros((SMALL_ROWS - 11, D), F32)], axis=0)


def _unpack_small(p):
    return (p[0:6].reshape(1, 6 * D), p[6].reshape(1, D), p[9, 0:HD].reshape(1, HD), p[9, HD:2 * HD].reshape(1, HD),
            p[10, 0:NH].reshape(1, NH), p[7].reshape(1, D), p[8].reshape(D))


def kernel(x, c, w_ada, b_ada, norm1_g, w_in, q_norm_a, k_norm_a, sink_b, w_branch, w_out, norm2_g, w_mlp_in, w_mlp_out, final_g, loss_target, m_w_ada, m_b_ada, m_norm1_g, m_w_in, m_q_norm_a, m_k_norm_a, m_sink_b, m_w_branch, m_w_out, m_norm2_g, m_w_mlp_in, m_w_mlp_out, m_final_g, v_w_ada, v_b_ada, v_norm1_g, v_w_in, v_q_norm_a, v_k_norm_a, v_sink_b, v_w_branch, v_w_out, v_norm2_g, v_w_mlp_in, v_w_mlp_out, v_final_g):
    S = x.shape[1]
    xs = x.reshape(S, D)
    tgt = loss_target.reshape(S, D)
    ax, ay, ac = lax.axis_index("x"), lax.axis_index("y"), lax.axis_index("c")
    me = 4 * ax + 2 * ay + ac
    me1 = me.reshape(1).astype(jnp.int32)
    NW = w_ada.shape[2]
    NI = w_in.shape[2]

    silu64, mod64 = _ada_exchange(c.reshape(1, D), w_ada.reshape(D, NW),
                                  jnp.repeat(b_ada.reshape(N_DEV, NW), 8, axis=0))
    silu_all = silu64[0::8]
    modv = mod64[0::8].reshape(6, D)

    win = _weight_gather(w_in[0].astype(BF16)).transpose(1, 0, 2).reshape(D, INW)
    rest_shards = tuple(w[0].astype(BF16) for w in (w_branch, w_out, w_mlp_in, w_mlp_out))

    tab_a, tab_b = _rope_tables(S)
    qg2 = jnp.tile(q_norm_a.reshape(1, HD), (1, 2))
    kg2 = jnp.tile(k_norm_a.reshape(1, HD), (1, 2))
    n1g = norm1_g.reshape(1, D)
    n2g = norm2_g.reshape(1, D)
    fg = final_g.reshape(1, D)
    sink2 = sink_b.reshape(1, NH) * LOG2E

    h, qar, kar, qa, ka, va, qb, kb, vb, ga, gb, qa_t, ka_t3, va_t3 = _in_proj(xs, modv, n1g, win, qg2, kg2, tab_a, tab_b)
    ya, lse_at, wb, wout, wmi, wmo = _attn_a_fwd(qa_t, ka, va_t3, rest_shards)
    lse_a = lse_at.T
    wb = wb.transpose(1, 2, 0, 3).reshape(2, BW, D)
    wout = wout.reshape(D, D)
    pad = ((WIN, WIN), (0, 0))
    kbp, vbp = jnp.pad(kb, pad), jnp.pad(vb, pad)
    tb = min(TQ_B, S)
    yb, lse_bt = _attn_b_fwd(qb, kbp, vbp.reshape((S + 2 * WIN) // tb, tb, 128).transpose(0, 2, 1), sink2)
    lse_b = lse_bt.T
    x1, merged, ua, ub = _merge_out(ya, yb, ga, gb, xs, modv, wb, wout)
    h2, hp, dx2, stf = _mlp_fwd(x1, modv, n2g, wmi, wmo, fg, tgt)

    dhp, dx1, st2 = _mlp_bwd(dx2, x1, hp, modv, n2g, wmi, wmo)
    m2 = _tn_matmul(hp, dx2, 2048, D, "dw_mlp_out", relu_sq=True)
    g_wmo, dg2 = _scale_gate(m2, wmo.reshape(FF, D), modv, 5, "gate2_grad")
    g_wmi = _tn_matmul(h2, dhp, D, 512, "dw_mlp_in", dev_major=True, rows=2048)
    dua, dub, dga, dgb, dya, dyb, dl_a, dl_b, dya_t = _merge_bwd(dx1, modv, ga, gb, ua, ub, ya, yb, wb, wout)
    m1 = _tn_matmul(merged, dx1, D, D, "dw_out", rows=2048)
    g_wout, dg1 = _scale_gate(m1, wout, modv, 2, "gate1_grad")
    g_wb0 = _tn_matmul(ya, dua, BW, D, "dw_branch_a")
    g_wb1 = _tn_matmul(yb, dub, BW, D, "dw_branch_b")
    g_wb = jnp.stack([g_wb0, g_wb1]).reshape(2, BW, N_DEV, 128).transpose(2, 0, 1, 3).reshape(N_DEV, 2 * BW, 128)
    g_wout = g_wout.reshape(N_DEV, 128, D)
    g_wmo = g_wmo.reshape(N_DEV, 512, D)
    dqa_t, dka_t3, dva_t3, r_wb, r_wout, r_wmi, r_wmo = _attn_a_bwd(qa, qa_t, ka_t3, va_t3, dya, dya_t, lse_a, dl_a,
                                                                    (g_wb, g_wout, g_wmi, g_wmo))
    dqb, dkb_t, dvb_t, dsink = _attn_b_bwd(qb, kbp, vbp, sink2, dyb, lse_b, dl_b)
    dkb = dkb_t.transpose(0, 2, 1).reshape(S + 2 * WIN, 128)[WIN:WIN + S]
    dvb = dvb_t.transpose(0, 2, 1).reshape(S + 2 * WIN, 128)[WIN:WIN + S]
    dproj, stqk = _qk_bwd(dqa_t, dka_t3, dva_t3, dqb, dkb, dvb, qar, kar, qg2, kg2, tab_a, tab_b, dga, dgb)
    g_win = _tn_matmul(h, dproj, D, 896, "dw_in", rows=2048)
    g_win = g_win.reshape(D, N_DEV, NI).transpose(1, 0, 2)
    grad_x, st1, r_win = _in_bwd(dproj, win, xs, dx1, modv, n1g, g_win.astype(BF16))

    def adam(name, own, recv, w, m, v):
        shape = w.shape
        w2, m2_, v2 = (a.reshape(own.shape[1:]) for a in (w, m, v))
        outs = _adamw_sum([(own, me1)] + [(recv, k) for k in range(N_DEV - 1)], w2, m2_, v2, name)
        return [a.reshape(shape) for a in outs]

    o_win = adam("adamw_w_in", g_win, r_win, w_in, m_w_in, v_w_in)
    o_wb = adam("adamw_w_branch", g_wb, r_wb, w_branch, m_w_branch, v_w_branch)
    o_wout = adam("adamw_w_out", g_wout, r_wout, w_out, m_w_out, v_w_out)
    o_wmi = adam("adamw_w_mlp_in", g_wmi, r_wmi, w_mlp_in, m_w_mlp_in, v_w_mlp_in)
    o_wmo = adam("adamw_w_mlp_out", g_wmo, r_wmo, w_mlp_out, m_w_mlp_out, v_w_mlp_out)

    small = _pack_small(st1, st2, stf, dg1, dg2, stqk, dsink, modv, n1g, n2g)
    small_all = _small_gather(small)
    sw = _pack_small_params(b_ada, norm1_g, norm2_g, final_g, q_norm_a, k_norm_a, sink_b)
    sm = _pack_small_params(m_b_ada, m_norm1_g, m_norm2_g, m_final_g, m_q_norm_a, m_k_norm_a, m_sink_b)
    sv = _pack_small_params(v_b_ada, v_norm1_g, v_norm2_g, v_final_g, v_q_norm_a, v_k_norm_a, v_sink_b)
    sm_out = _adamw_sum([(small_all, k) for k in range(N_DEV)], sw, sm, sv, "adamw_small")
    loss = sm_out[0][11, 0]
    sm_out = [_unpack_small(a) for a in sm_out]

    dmod_all = small_all[:, 0:6, :].reshape(N_DEV, 6 * D)
    dmod_cols = lax.dynamic_slice_in_dim(dmod_all, me * NW, NW, axis=1)
    g_wada = _wada_grad(silu_all, dmod_cols)
    ada = _adamw_sum([(g_wada, None)], w_ada.reshape(D, NW), m_w_ada.reshape(D, NW), v_w_ada.reshape(D, NW), "adamw_ada")
    ada = [a.reshape(1, D, NW) for a in ada]

    def leaves(k):
        b_, n1_, qn_, kn_, sk_, n2_, fg_ = sm_out[k]
        return [ada[k], b_, n1_, o_win[k], qn_, kn_, sk_, o_wb[k], o_wout[k], n2_, o_wmi[k], o_wmo[k], fg_]

    return (loss, grad_x.reshape(1, S, D), *leaves(0), *leaves(1), *leaves(2), *leaves(3))
```

```python
import jax
import jax.numpy as jnp
from jax import lax
from jax.experimental import pallas as pl
from jax.experimental.pallas import tpu as pltpu

F32, BF16 = jnp.float32, jnp.bfloat16
MESH = pl.DeviceIdType.MESH

D = 1024
HD = 64
NH = 8
GRP = 4
BW = 512
FF = 4096
INW = 3584
GRID_W = 64
WIN = 128
THETA = 10000.0
EPS = 1e-6
NEG = -1e30
N_DEV = 8
LOG2E = 1.4426950408889634
LN2 = 0.6931471805599453
QA_SCALE = 0.125 * LOG2E
SMALL_ROWS = 16
MLP_SHARDS = 4
MLP_ROWS = 512
TK_A = 512
TK_A_FWD = 2048
TQ_A_FWD = 512
V7X_VMEM_LIMIT = 56 * 1024 * 1024

ADAM_LR, ADAM_B1, ADAM_B2, ADAM_EPS, ADAM_WD, ADAM_STEP = 0.001, 0.9, 0.999, 1e-08, 0.01, 10

NT = (((1,), (1,)), ((), ()))
TN = (((0,), (0,)), ((), ()))


def _params(n_axes, vmem=V7X_VMEM_LIMIT):
    return pltpu.CompilerParams(dimension_semantics=("arbitrary",) * n_axes, vmem_limit_bytes=vmem)


def _const(shape):
    return pl.BlockSpec(shape, lambda *_: (0,) * len(shape))


def _rows(tm, width):
    return pl.BlockSpec((tm, width), lambda i, *_: (i, 0))


def _seg_matrix(n, seg):
    r = lax.broadcasted_iota(jnp.int32, (n, n), 0) // seg
    c = lax.broadcasted_iota(jnp.int32, (n, n), 1) // seg
    return (r == c).astype(BF16)


def _seg_sum(z, seg_mat):
    hi = z.astype(BF16)
    lo = (z - hi.astype(F32)).astype(BF16)
    return jnp.dot(hi, seg_mat, preferred_element_type=F32) + jnp.dot(lo, seg_mat, preferred_element_type=F32)


def _rope(z, t_ref, sh):
    return z * t_ref[0] + pltpu.roll(z, sh, 1) * t_ref[1] + pltpu.roll(z, 128 - sh, 1) * t_ref[2]


def _rope_t(dz, t_ref, sh):
    return dz * t_ref[0] + pltpu.roll(dz * t_ref[1], 128 - sh, 1) + pltpu.roll(dz * t_ref[2], sh, 1)


def _rope_tables(S):
    t = jnp.arange(S, dtype=jnp.int32)[:, None]
    lane = jnp.arange(128, dtype=jnp.int32)[None, :] % HD

    def build(ang, first):
        cos, sin = jnp.cos(ang), jnp.sin(ang)
        return jnp.stack([cos, jnp.where(first, 0.0, sin), jnp.where(first, -sin, 0.0)]).astype(F32)

    inv_a = (THETA ** (-jnp.arange(0, HD // 2, 2, dtype=F32) / (HD // 2)))[lane % 16]
    pos_a = jnp.where(lane < HD // 2, t // GRID_W, t % GRID_W).astype(F32)
    tab_a = build(pos_a * inv_a, (lane % 32) < 16)
    inv_b = (THETA ** (-jnp.arange(0, HD, 2, dtype=F32) / HD))[lane % 32]
    tab_b = build(t.astype(F32) * inv_b, lane < 32)
    return tab_a, tab_b


def _in_proj(x, modv, n1g, win, qg2, kg2, tab_a, tab_b):
    S = x.shape[0]
    tm = min(512, S)
    tk = min(TK_A, S)
    per = tk // tm
    ts = min(256, tm)

    def body(x_ref, mod_ref, g_ref, w_ref, qg_ref, kg_ref, ta_ref, tb_ref,
             h_ref, qar_ref, kar_ref, qa_ref, ka_ref, va_ref, qb_ref, kb_ref, vb_ref, ga_ref, gb_ref, qat_ref, kat_ref, vat_ref):
        seg = _seg_matrix(128, HD)

        def head_norm(z, g):
            ms = _seg_sum(z * z, seg) * (1.0 / HD)
            return (z * lax.rsqrt(ms + EPS)) * g

        subs = [slice(ts * u, ts * u + ts) for u in range(tm // ts)]
        hbs = []
        for rows in subs:
            xt = x_ref[rows, :]
            r = lax.rsqrt(jnp.mean(xt * xt, axis=-1, keepdims=True) + EPS)
            h = ((xt * r) * g_ref[...]) * (1.0 + mod_ref[1:2, :]) + mod_ref[0:1, :]
            hbs.append(h.astype(BF16))
            h_ref[rows, :] = hbs[-1]
        projs = [jnp.dot(hb, w_ref[...], preferred_element_type=F32) for hb in hbs]
        for rows, proj in zip(subs, projs):
            ta, tb = ta_ref[:, rows, :], tb_ref[:, rows, :]
            for p in range(4):
                z = proj[:, 128 * p:128 * p + 128]
                qar_ref[rows, 128 * p:128 * p + 128] = z.astype(BF16)
                qv = _rope(head_norm(z, qg_ref[...]), ta, 16) * QA_SCALE
                qa_ref[rows, 128 * p:128 * p + 128] = qv.astype(BF16)
                qat_ref[128 * p:128 * p + 128, rows] = qv.T.astype(BF16)
                zb = proj[:, 768 + 128 * p:768 + 128 * p + 128]
                qb_ref[rows, 128 * p:128 * p + 128] = (_rope(zb, tb, 32) * QA_SCALE).astype(BF16)
            z = proj[:, 512:640]
            kar_ref[rows, :] = z.astype(BF16)
            kv_ = _rope(head_norm(z, kg_ref[...]), ta, 16)
            ka_ref[rows, :] = kv_.astype(BF16)
            kat_ref[:, rows] = kv_.T.astype(BF16)
            va_ref[rows, :] = proj[:, 640:768].astype(BF16)
            vat_ref[:, rows] = proj[:, 640:768].T.astype(BF16)
            kb_ref[rows, :] = _rope(proj[:, 1280:1408], tb, 32).astype(BF16)
            vb_ref[rows, :] = proj[:, 1408:1536].astype(BF16)
            ga_ref[rows, :] = proj[:, 1536:2560].astype(BF16)
            gb_ref[rows, :] = proj[:, 2560:3584].astype(BF16)

    tab = pl.BlockSpec((3, tm, 128), lambda i: (0, i, 0))
    shapes = [(D, BF16), (BW, BF16), (128, BF16), (BW, BF16), (128, BF16), (128, BF16),
              (BW, BF16), (128, BF16), (128, BF16), (D, BF16), (D, BF16)]
    return pl.pallas_call(
        body, grid=(S // tm,),
        in_specs=[_rows(tm, D), _const((6, D)), _const((1, D)), _const((D, INW)), _const((1, 128)), _const((1, 128)), tab, tab],
        out_specs=[_rows(tm, w) for w, _ in shapes] + [pl.BlockSpec((BW, tm), lambda i: (0, i))]
        + [pl.BlockSpec((None, 128, tm), lambda i: (i // per, 0, i % per))] * 2,
        out_shape=[jax.ShapeDtypeStruct((S, w), dt) for w, dt in shapes] + [jax.ShapeDtypeStruct((BW, S), BF16)]
        + [jax.ShapeDtypeStruct((S // tk, 128, tk), BF16)] * 2,
        compiler_params=_params(1), name="in_proj",
    )(x, modv, n1g, win, qg2, kg2, tab_a, tab_b)


def _exchange_gather(block_refs, out_refs, send_sems, recv_sems, local_sems):
    x, y, c = _me()
    me = 4 * x + 2 * y + c

    def copies():
        own, out, arrive = [], [], []
        for a, (blk, dst) in enumerate(zip(block_refs, out_refs)):
            own.append(pltpu.make_async_copy(blk, dst.at[me], local_sems.at[a]))
            for k in range(1, N_DEV):
                sems = dict(send_sem=send_sems.at[a, k - 1], recv_sem=recv_sems.at[a, k - 1], device_id=_peer(k), device_id_type=MESH)
                out.append(pltpu.make_async_remote_copy(blk, dst.at[me], **sems))
                arrive.append(pltpu.make_async_remote_copy(blk, dst.at[me ^ k], **sems))
        return own, out, arrive

    def start():
        own, out, _ = copies()
        for cp in own + out:
            cp.start()

    def finish():
        own, out, arrive = copies()
        for cp in arrive:
            cp.wait_recv()
        for cp in out:
            cp.wait_send()
        for cp in own:
            cp.wait()

    return start, finish


def _exchange_scatter(chunk_refs, recv_refs, send_sems, recv_sems):
    x, y, c = _me()
    me = 4 * x + 2 * y + c

    def copies():
        return [pltpu.make_async_remote_copy(src.at[me ^ k], dst.at[k - 1], send_sems.at[a, k - 1], recv_sems.at[a, k - 1],
                                             device_id=_peer(k), device_id_type=MESH)
                for a, (src, dst) in enumerate(zip(chunk_refs, recv_refs)) for k in range(1, N_DEV)]

    def start():
        for cp in copies():
            cp.start()

    def finish():
        cps = copies()
        for cp in cps:
            cp.wait_recv()
        for cp in cps:
            cp.wait_send()

    return start, finish


def _exchange_sems(n):
    return [pltpu.SemaphoreType.DMA((n, N_DEV - 1)), pltpu.SemaphoreType.DMA((n, N_DEV - 1))]


def _attn_a_fwd(qt, k, vt3, shards):
    S = qt.shape[1]
    tq = min(TQ_A_FWD, S)
    nq = S // tq
    per = max(1, min(TK_A_FWD, S) // vt3.shape[2])
    tk = per * vt3.shape[2]
    nk = S // tk
    ONES = 16
    AHEAD = 2
    ns = len(shards)

    def body(q_ref, k_ref, vt_ref, *rest):
        w_hbm, (o_ref, lse_ref), wall_hbm = rest[:ns], rest[ns:ns + 2], rest[ns + 2:2 * ns + 2]
        st_sc, send_sems, recv_sems, local_sems = rest[2 * ns + 2:]
        start, finish = _exchange_gather(w_hbm, wall_hbm, send_sems, recv_sems, local_sems)
        pl.when(pl.program_id(0) == 0)(start)
        row8 = lax.broadcasted_iota(jnp.int32, (NH, tq), 0)
        lse_all = jnp.zeros((NH, tq), F32)
        ones = jnp.ones((ONES, tk), BF16)
        for kv in range(2):
            qts = [q_ref[HD * (GRP * kv + t):HD * (GRP * kv + t) + HD, :] for t in range(GRP)]

            def keys(j, kv=kv):
                off = j * tk if isinstance(j, int) else pl.multiple_of(j * tk, tk)
                return k_ref[pl.ds(off, tk), :][:, HD * kv:HD * kv + HD]

            def scores(kj, t, qts=qts):
                return jnp.dot(kj, qts[t], preferred_element_type=F32)

            def step(j, carry, kv=kv, last=False):
                kj = keys(j)
                kn = None if last else keys(j + 1)
                vt = jnp.concatenate([vt_ref[per * j + u, HD * kv:HD * kv + HD, :] for u in range(per)], axis=1)
                v1 = jnp.concatenate([vt, ones], axis=0)
                sts = [st_sc[t] for t in range(AHEAD)]
                new = []
                for t in range(GRP):
                    m, acc = carry[2 * t], carry[2 * t + 1]
                    if t + AHEAD < GRP:
                        sts.append(scores(kj, t + AHEAD))
                    st = sts[t]
                    mn = jnp.maximum(m, jnp.max(st, axis=0, keepdims=True))
                    pt = jnp.exp2(st - mn)
                    if t + AHEAD >= GRP and not last:
                        st_sc[t + AHEAD - GRP] = scores(kn, t + AHEAD - GRP)
                    acc = jnp.exp2(m - mn) * acc + jnp.dot(v1, pt.astype(BF16), preferred_element_type=F32)
                    new += [mn, acc]
                return tuple(new)

            k0 = keys(0)
            for t in range(AHEAD):
                st_sc[t] = scores(k0, t)
            init = (jnp.full((1, tq), NEG, F32), jnp.zeros((HD + ONES, tq), F32)) * GRP
            res = step(nk - 1, lax.fori_loop(0, nk - 1, step, init), last=True)
            outs = []
            for t in range(GRP):
                m, acc = res[2 * t], res[2 * t + 1]
                l = acc[HD:HD + 1, :]
                outs.append((acc[:HD, :] / l).T)
                lse_all = jnp.where(row8 == GRP * kv + t, m + jnp.log2(l), lse_all)
            o_ref[:, 256 * kv:256 * kv + 256] = jnp.concatenate(outs, axis=1).astype(BF16)
        lse_ref[...] = lse_all
        pl.when(pl.program_id(0) == nq - 1)(finish)

    any_spec = pl.BlockSpec(memory_space=pl.ANY)
    return pl.pallas_call(
        body, grid=(nq,),
        in_specs=[pl.BlockSpec((BW, tq), lambda i: (0, i)), _const((S, 128)), _const(vt3.shape)] + [any_spec] * ns,
        out_specs=[_rows(tq, BW), pl.BlockSpec((NH, tq), lambda i: (0, i))] + [any_spec] * ns,
        out_shape=[jax.ShapeDtypeStruct((S, BW), BF16), jax.ShapeDtypeStruct((NH, S), F32)]
        + [jax.ShapeDtypeStruct((N_DEV,) + s.shape, s.dtype) for s in shards],
        scratch_shapes=[pltpu.VMEM((AHEAD, tk, tq), F32)] + _exchange_sems(ns) + [pltpu.SemaphoreType.DMA((ns,))],
        compiler_params=_params(1), name="attn_a_fwd",
    )(qt, k, vt3, *shards)


def _window_mask(i, tq, S):
    W = tq + 2 * WIN
    r = lax.broadcasted_iota(jnp.int32, (tq, W), 0)
    c = lax.broadcasted_iota(jnp.int32, (tq, W), 1)
    kpos = i * tq - WIN + c
    return (jnp.abs(c - WIN - r) <= WIN) & (kpos >= 0) & (kpos < S)


TQ_B = 256


def _attn_b_fwd(q, kp, vpt3, sink2):
    S = q.shape[0]
    tq = min(TQ_B, S)
    W = tq + 2 * WIN
    nc = vpt3.shape[0]
    ONES = 16

    def body(q_ref, k_ref, vt_ref, sink_ref, o_ref, lse_ref):
        i = pl.program_id(0)
        off = pl.multiple_of(i * tq, tq)
        r = lax.broadcasted_iota(jnp.int32, (W, tq), 1)
        c = lax.broadcasted_iota(jnp.int32, (W, tq), 0)
        kpos = i * tq - WIN + c
        valid = (jnp.abs(c - WIN - r) <= WIN) & (kpos >= 0) & (kpos < S)
        kw = k_ref[pl.ds(off, W), :]
        vt = jnp.concatenate([vt_ref[i + half] for half in range(W // tq)], axis=1)
        ones = jnp.ones((ONES, W), BF16)
        row8 = lax.broadcasted_iota(jnp.int32, (NH, tq), 0)
        lse_all = jnp.zeros((NH, tq), F32)
        qs = []
        for p in range(4):
            qp = q_ref[:, 128 * p:128 * p + 128]
            qs += [qp[:, :HD], qp[:, HD:]]
        khs = [kw[:, HD * kv:HD * kv + HD] for kv in range(2)]
        v1s = [jnp.concatenate([vt[HD * kv:HD * kv + HD, :], ones], axis=0) for kv in range(2)]

        def scores(h):
            return lax.dot_general(khs[h // GRP], qs[h], NT, preferred_element_type=F32)

        ss = [scores(0), scores(1)]
        outs = []
        for h in range(NH):
            if h + 2 < NH:
                ss.append(scores(h + 2))
            st = jnp.where(valid, ss[h], NEG)
            sk = sink_ref[:, h:h + 1]
            m = jnp.maximum(jnp.max(st, axis=0, keepdims=True), sk)
            acc = jnp.dot(v1s[h // GRP], jnp.exp2(st - m).astype(BF16), preferred_element_type=F32)
            l = acc[HD:HD + 1, :] + jnp.exp2(sk - m)
            outs.append((acc[:HD, :] / l).T)
            lse_all = jnp.where(row8 == h, m + jnp.log2(l), lse_all)
        for p in range(4):
            o_ref[:, 128 * p:128 * p + 128] = jnp.concatenate(outs[2 * p:2 * p + 2], axis=1).astype(BF16)
        lse_ref[...] = lse_all

    return pl.pallas_call(
        body, grid=(S // tq,),
        in_specs=[_rows(tq, BW), _const((S + 2 * WIN, 128)), _const((nc, 128, tq)), _const((1, NH))],
        out_specs=[_rows(tq, BW), pl.BlockSpec((NH, tq), lambda i: (0, i))],
        out_shape=[jax.ShapeDtypeStruct((S, BW), BF16), jax.ShapeDtypeStruct((NH, S), F32)],
        compiler_params=_params(1), name="attn_b_fwd",
    )(q, kp, vpt3, sink2)


def _merge_out(ya, yb, ga, gb, x, modv, wb, wout):
    S = x.shape[0]
    tm = min(256, S)

    def body(ya_ref, yb_ref, ga_ref, gb_ref, x_ref, mod_ref, wb_ref, wo_ref, x1_ref, mg_ref, ua_ref, ub_ref):
        ua = jnp.dot(ya_ref[...], wb_ref[0], preferred_element_type=F32)
        ub = jnp.dot(yb_ref[...], wb_ref[1], preferred_element_type=F32)
        merged = jax.nn.sigmoid(ga_ref[...].astype(F32)) * ua + jax.nn.sigmoid(gb_ref[...].astype(F32)) * ub
        mb = merged.astype(BF16)
        ua_ref[...] = ua.astype(BF16)
        ub_ref[...] = ub.astype(BF16)
        mg_ref[...] = mb
        x1_ref[...] = x_ref[...] + mod_ref[2:3, :] * jnp.dot(mb, wo_ref[...], preferred_element_type=F32)

    return pl.pallas_call(
        body, grid=(S // tm,),
        in_specs=[_rows(tm, BW), _rows(tm, BW), _rows(tm, D), _rows(tm, D), _rows(tm, D), _const((6, D)),
                  _const((2, BW, D)), _const((D, D))],
        out_specs=[_rows(tm, D)] * 4,
        out_shape=[jax.ShapeDtypeStruct((S, D), F32)] + [jax.ShapeDtypeStruct((S, D), BF16)] * 3,
        compiler_params=_params(1), name="merge_out",
    )(ya, yb, ga, gb, x, modv, wb, wout)


def _mlp_fwd(x1, modv, n2g, wmi, wmo, fg, target):
    S = x1.shape[0]
    tm = min(MLP_ROWS, S)
    tf = wmi.shape[2]
    nj = wmi.shape[0] // MLP_SHARDS

    def body(x1_ref, mod_ref, g_ref, wi_ref, wo_ref, fg_ref, t_ref, h2_ref, hp_ref, dx2_ref, st_ref, acc_ref):
        i, j = pl.program_id(0), pl.program_id(1)

        @pl.when(j == 0)
        def _():
            xt = x1_ref[...]
            r = lax.rsqrt(jnp.mean(xt * xt, axis=-1, keepdims=True) + EPS)
            h2 = ((xt * r) * g_ref[...]) * (1.0 + mod_ref[4:5, :]) + mod_ref[3:4, :]
            h2_ref[...] = h2.astype(BF16)
            acc_ref[...] = jnp.zeros_like(acc_ref)

        @pl.when((i == 0) & (j == 0))
        def _():
            st_ref[...] = jnp.zeros_like(st_ref)

        out = None
        for u in range(MLP_SHARDS):
            hp = jnp.dot(h2_ref[...], wi_ref[u], preferred_element_type=F32)
            hp_ref[:, tf * u:tf * u + tf] = hp.astype(BF16)
            hid = jnp.square(jnp.maximum(hp, 0.0))
            part = jnp.dot(hid.astype(BF16), wo_ref[u], preferred_element_type=F32)
            out = part if out is None else out + part
        acc_ref[...] += out

        @pl.when(j == nj - 1)
        def _():
            x2 = x1_ref[...] + mod_ref[5:6, :] * acc_ref[...]
            r3 = lax.rsqrt(jnp.mean(x2 * x2, axis=-1, keepdims=True) + EPS)
            xn = x2 * r3
            err = xn * fg_ref[...] - t_ref[...]
            dy = err * (1.0 / D)
            gy = dy * fg_ref[...]
            dx2_ref[...] = r3 * (gy - xn * jnp.mean(gy * xn, axis=-1, keepdims=True))
            st_ref[0:1, :] += jnp.sum(dy * xn, axis=0, keepdims=True)
            st_ref[1:2, :] += jnp.sum(err * err, axis=0, keepdims=True) * (0.5 / D)

    return pl.pallas_call(
        body, grid=(S // tm, nj),
        in_specs=[pl.BlockSpec((tm, D), lambda i, j: (i, 0)), _const((6, D)), _const((1, D)),
                  pl.BlockSpec((MLP_SHARDS, D, tf), lambda i, j: (j, 0, 0)), pl.BlockSpec((MLP_SHARDS, tf, D), lambda i, j: (j, 0, 0)),
                  _const((1, D)), pl.BlockSpec((tm, D), lambda i, j: (i, 0))],
        out_specs=[pl.BlockSpec((tm, D), lambda i, j: (i, 0)), pl.BlockSpec((tm, MLP_SHARDS * tf), lambda i, j: (i, j)),
                   pl.BlockSpec((tm, D), lambda i, j: (i, 0)), _const((8, D))],
        out_shape=[jax.ShapeDtypeStruct((S, D), BF16), jax.ShapeDtypeStruct((S, wmi.shape[0] * tf), BF16),
                   jax.ShapeDtypeStruct((S, D), F32), jax.ShapeDtypeStruct((8, D), F32)],
        scratch_shapes=[pltpu.VMEM((tm, D), F32)],
        compiler_params=_params(2), name="mlp_fwd",
    )(x1, modv, n2g, wmi, wmo, fg, target)


def _mlp_bwd(dx2, x1, hp, modv, n2g, wmi, wmo):
    S = x1.shape[0]
    tm = min(MLP_ROWS, S)
    tf = wmi.shape[2]
    nj = wmi.shape[0] // MLP_SHARDS

    def body(dx2_ref, x1_ref, hp_ref, mod_ref, g_ref, wi_ref, wo_ref, dhp_ref, dx1_ref, st_ref, dmo_ref, acc_ref):
        i, j = pl.program_id(0), pl.program_id(1)

        @pl.when(j == 0)
        def _():
            dmo_ref[...] = (mod_ref[5:6, :] * dx2_ref[...]).astype(BF16)
            acc_ref[...] = jnp.zeros_like(acc_ref)

        @pl.when((i == 0) & (j == 0))
        def _():
            st_ref[...] = jnp.zeros_like(st_ref)

        out = None
        for u in range(MLP_SHARDS):
            sl = slice(tf * u, tf * u + tf)
            dhid = lax.dot_general(dmo_ref[...], wo_ref[u], NT, preferred_element_type=F32)
            dhp = (dhid * (2.0 * jnp.maximum(hp_ref[:, sl].astype(F32), 0.0))).astype(BF16)
            dhp_ref[:, sl] = dhp
            part = lax.dot_general(dhp, wi_ref[u], NT, preferred_element_type=F32)
            out = part if out is None else out + part
        acc_ref[...] += out

        @pl.when(j == nj - 1)
        def _():
            dh2 = acc_ref[...]
            xt = x1_ref[...]
            r = lax.rsqrt(jnp.mean(xt * xt, axis=-1, keepdims=True) + EPS)
            xn = xt * r
            st_ref[0:1, :] += jnp.sum(dh2, axis=0, keepdims=True)
            st_ref[1:2, :] += jnp.sum(dh2 * xn, axis=0, keepdims=True)
            dxn = dh2 * (g_ref[...] * (1.0 + mod_ref[4:5, :]))
            dx1_ref[...] = dx2_ref[...] + r * (dxn - xn * jnp.mean(dxn * xn, axis=-1, keepdims=True))

    return pl.pallas_call(
        body, grid=(S // tm, nj),
        in_specs=[pl.BlockSpec((tm, D), lambda i, j: (i, 0)), pl.BlockSpec((tm, D), lambda i, j: (i, 0)),
                  pl.BlockSpec((tm, MLP_SHARDS * tf), lambda i, j: (i, j)), _const((6, D)), _const((1, D)),
                  pl.BlockSpec((MLP_SHARDS, D, tf), lambda i, j: (j, 0, 0)), pl.BlockSpec((MLP_SHARDS, tf, D), lambda i, j: (j, 0, 0))],
        out_specs=[pl.BlockSpec((tm, MLP_SHARDS * tf), lambda i, j: (i, j)), pl.BlockSpec((tm, D), lambda i, j: (i, 0)), _const((8, D))],
        out_shape=[jax.ShapeDtypeStruct((S, wmi.shape[0] * tf), BF16), jax.ShapeDtypeStruct((S, D), F32),
                   jax.ShapeDtypeStruct((8, D), F32)],
        scratch_shapes=[pltpu.VMEM((tm, D), BF16), pltpu.VMEM((tm, D), F32)],
        compiler_params=_params(2), name="mlp_bwd",
    )(dx2, x1, hp, modv, n2g, wmi, wmo)


def _tn_matmul(a, b, tk, tn, name, relu_sq=False, dev_major=False, rows=1024):
    S, K = a.shape
    N = b.shape[1]
    ts = min(rows, S)
    ns = S // ts

    def body(a_ref, b_ref, o_ref):
        @pl.when(pl.program_id(2) == 0)
        def _():
            o_ref[...] = jnp.zeros_like(o_ref)

        at = a_ref[...]
        if relu_sq:
            at = jnp.square(jnp.maximum(at.astype(F32), 0.0)).astype(BF16)
        o_ref[...] += lax.dot_general(at, b_ref[...].astype(BF16), TN, preferred_element_type=F32)

    if dev_major:
        out_spec = pl.BlockSpec((None, tk, tn), lambda k, n, s: (n, k, 0))
        out_shape = jax.ShapeDtypeStruct((N // tn, K, tn), F32)
    else:
        out_spec = pl.BlockSpec((tk, tn), lambda k, n, s: (k, n))
        out_shape = jax.ShapeDtypeStruct((K, N), F32)
    return pl.pallas_call(
        body, grid=(K // tk, N // tn, ns),
        in_specs=[pl.BlockSpec((ts, tk), lambda k, n, s: (s, k)), pl.BlockSpec((ts, tn), lambda k, n, s: (s, n))],
        out_specs=out_spec, out_shape=out_shape,
        compiler_params=_params(3), name=name,
    )(a, b)


def _scale_gate(m, w, g, row, name):
    K = m.shape[0]
    tk = min(512, K)

    def body(m_ref, w_ref, mod_ref, dw_ref, dg_ref):
        @pl.when(pl.program_id(0) == 0)
        def _():
            dg_ref[...] = jnp.zeros_like(dg_ref)

        mt = m_ref[...]
        dw_ref[...] = mt * mod_ref[row:row + 1, :]
        dg_ref[0:1, :] += jnp.sum(mt * w_ref[...].astype(F32), axis=0, keepdims=True)

    return pl.pallas_call(
        body, grid=(K // tk,),
        in_specs=[_rows(tk, D), _rows(tk, D), _const((6, D))],
        out_specs=[_rows(tk, D), _const((8, D))],
        out_shape=[jax.ShapeDtypeStruct((K, D), F32), jax.ShapeDtypeStruct((8, D), F32)],
        compiler_params=_params(1), name=name,
    )(m, w, g)


def _merge_bwd(dx1, modv, ga, gb, ua, ub, ya, yb, wb, wout):
    S = dx1.shape[0]
    tm = min(512, S)

    def body(dx1_ref, mod_ref, ga_ref, gb_ref, ua_ref, ub_ref, ya_ref, yb_ref, wb_ref, wo_ref,
             dua_ref, dub_ref, dga_ref, dgb_ref, dya_ref, dyb_ref, dla_ref, dlb_ref, dyat_ref):
        dao = (mod_ref[2:3, :] * dx1_ref[...]).astype(BF16)
        dm = lax.dot_general(dao, wo_ref[...], NT, preferred_element_type=F32)
        r = lax.broadcasted_iota(jnp.int32, (BW, NH), 0) // HD
        c = lax.broadcasted_iota(jnp.int32, (BW, NH), 1)
        head_of = (r == c).astype(BF16)
        for br, (g_ref, u_ref, y_ref, du_ref, dg_ref, dy_ref, dl_ref) in enumerate((
                (ga_ref, ua_ref, ya_ref, dua_ref, dga_ref, dya_ref, dla_ref),
                (gb_ref, ub_ref, yb_ref, dub_ref, dgb_ref, dyb_ref, dlb_ref))):
            sg = jax.nn.sigmoid(g_ref[...].astype(F32))
            du = (dm * sg).astype(BF16)
            du_ref[...] = du
            dg_ref[...] = (dm * u_ref[...].astype(F32) * sg * (1.0 - sg)).astype(BF16)
            dy = lax.dot_general(du, wb_ref[br], NT, preferred_element_type=F32)
            dyb16 = dy.astype(BF16)
            dy_ref[...] = dyb16
            if br == 0:
                dyat_ref[...] = dy.T.astype(BF16)
            prod = dyb16.astype(F32) * y_ref[...].astype(F32)
            hi = prod.astype(BF16)
            lo = (prod - hi.astype(F32)).astype(BF16)
            dl_ref[...] = (jnp.dot(hi, head_of, preferred_element_type=F32) + jnp.dot(lo, head_of, preferred_element_type=F32))

    return pl.pallas_call(
        body, grid=(S // tm,),
        in_specs=[_rows(tm, D), _const((6, D)), _rows(tm, D), _rows(tm, D), _rows(tm, D), _rows(tm, D),
                  _rows(tm, BW), _rows(tm, BW), _const((2, BW, D)), _const((D, D))],
        out_specs=[_rows(tm, D)] * 4 + [_rows(tm, BW)] * 2 + [_rows(tm, NH)] * 2 + [pl.BlockSpec((BW, tm), lambda i: (0, i))],
        out_shape=[jax.ShapeDtypeStruct((S, D), BF16)] * 4 + [jax.ShapeDtypeStruct((S, BW), BF16)] * 2
        + [jax.ShapeDtypeStruct((S, NH), F32)] * 2 + [jax.ShapeDtypeStruct((BW, S), BF16)],
        compiler_params=_params(1), name="merge_bwd",
    )(dx1, modv, ga, gb, ua, ub, ya, yb, wb, wout)


def _attn_a_bwd(q, qt, kt3, vt3, do, dot_, lse, delta, chunks):
    S = q.shape[0]
    tq = min(1024, S)
    nk, _, tk = kt3.shape
    nq = S // tq
    nc = len(chunks)

    def body(q_ref, qt_ref, do_ref, dot_ref, lse_ref, dl_ref, kt_ref, vt_ref, *rest):
        g_hbm, (dq_ref, dk_hbm, dv_hbm), recv_hbm = rest[:nc], rest[nc:nc + 3], rest[nc + 3:2 * nc + 3]
        dk_sc, dv_sc, sem, send_sems, recv_sems = rest[2 * nc + 3:]
        i = pl.program_id(0)
        start, finish = _exchange_scatter(g_hbm, recv_hbm, send_sems, recv_sems)

        @pl.when(i == 0)
        def _():
            start()
            dk_sc[...] = jnp.zeros_like(dk_sc)
            dv_sc[...] = jnp.zeros_like(dv_sc)

        for kv in range(2):
            qg = q_ref[:, 256 * kv:256 * kv + 256]
            dog = do_ref[:, 256 * kv:256 * kv + 256]
            heads = []
            for t in range(GRP):
                h = GRP * kv + t
                heads.append((qg[:, HD * t:HD * t + HD], dog[:, HD * t:HD * t + HD],
                              qt_ref[HD * h:HD * h + HD, :], dot_ref[HD * h:HD * h + HD, :],
                              lse_ref[:, h:h + 1], dl_ref[:, h:h + 1]))

            q2 = [jnp.concatenate([heads[2 * u][0], heads[2 * u + 1][0]], axis=0) for u in range(GRP // 2)]
            do2 = [jnp.concatenate([heads[2 * u][1], heads[2 * u + 1][1]], axis=0) for u in range(GRP // 2)]

            def step(j, carry, kv=kv, heads=heads, q2=q2, do2=do2):
                kjt = kt_ref[j, HD * kv:HD * kv + HD, :]
                vjt = vt_ref[j, HD * kv:HD * kv + HD, :]
                dkt = jnp.zeros((HD, tk), F32)
                dvt = jnp.zeros((HD, tk), F32)
                new = []

                def logits(u):
                    return (jnp.dot(q2[u], kjt, preferred_element_type=F32), jnp.dot(do2[u], vjt, preferred_element_type=F32))

                sd = [logits(0)]
                for t, (qh, doh, qth, doth, lse_h, dl_h) in enumerate(heads):
                    if t == 0:
                        sd.append(logits(1))
                    rows = slice(tq * (t % 2), tq * (t % 2) + tq)
                    s, dp = sd[t // 2][0][rows, :], sd[t // 2][1][rows, :]
                    pm = jnp.exp2(s - lse_h)
                    ds = (pm * (dp - dl_h)).astype(BF16)
                    dvt = dvt + jnp.dot(doth, pm.astype(BF16), preferred_element_type=F32)
                    dkt = dkt + jnp.dot(qth, ds, preferred_element_type=F32)
                    new.append(carry[t] + lax.dot_general(kjt, ds, NT, preferred_element_type=F32))
                dk_sc[j, HD * kv:HD * kv + HD, :] += dkt
                dv_sc[j, HD * kv:HD * kv + HD, :] += dvt
                return tuple(new)

            res = lax.fori_loop(0, nk, step, (jnp.zeros((HD, tq), F32),) * GRP)
            for t in range(GRP):
                dq_ref[HD * (GRP * kv + t):HD * (GRP * kv + t) + HD, :] = res[t]

        @pl.when(i == nq - 1)
        def _():
            c1 = pltpu.make_async_copy(dk_sc, dk_hbm, sem.at[0])
            c2 = pltpu.make_async_copy(dv_sc, dv_hbm, sem.at[1])
            c1.start()
            c2.start()
            c1.wait()
            c2.wait()
            finish()

    any_spec = pl.BlockSpec(memory_space=pl.ANY)
    cols = pl.BlockSpec((BW, tq), lambda i: (0, i))
    return pl.pallas_call(
        body, grid=(nq,),
        in_specs=[_rows(tq, BW), cols, _rows(tq, BW), cols, _rows(tq, NH), _rows(tq, NH), _const((nk, 128, tk)),
                  _const((nk, 128, tk))] + [any_spec] * nc,
        out_specs=[cols, any_spec, any_spec] + [any_spec] * nc,
        out_shape=[jax.ShapeDtypeStruct((BW, S), F32), jax.ShapeDtypeStruct((nk, 128, tk), F32),
                   jax.ShapeDtypeStruct((nk, 128, tk), F32)]
        + [jax.ShapeDtypeStruct((N_DEV - 1,) + c.shape[1:], c.dtype) for c in chunks],
        scratch_shapes=[pltpu.VMEM((nk, 128, tk), F32), pltpu.VMEM((nk, 128, tk), F32), pltpu.SemaphoreType.DMA((2,))]
        + _exchange_sems(nc),
        compiler_params=_params(1), name="attn_a_bwd",
    )(q, qt, do, dot_, lse, delta, kt3, vt3, *chunks)


def _attn_b_bwd(q, kp, vp, sink2, do, lse, delta):
    S = q.shape[0]
    tq = min(TQ_B, S)
    W = tq + 2 * WIN
    nq = S // tq
    nc = (S + 2 * WIN) // tq

    def body(q_ref, k_ref, v_ref, sink_ref, do_ref, lse_ref, dl_ref, dq_ref, dk_hbm, dv_hbm, ds_ref, dk_sc, dv_sc, sem):
        i = pl.program_id(0)

        @pl.when(i == 0)
        def _():
            dk_sc[...] = jnp.zeros_like(dk_sc)
            dv_sc[...] = jnp.zeros_like(dv_sc)
            ds_ref[...] = jnp.zeros_like(ds_ref)

        off = pl.multiple_of(i * tq, tq)
        valid = _window_mask(i, tq, S)
        kw = k_ref[pl.ds(off, W), :]
        vw = v_ref[pl.ds(off, W), :]
        lse_i = lse_ref[...]
        dl_i = dl_ref[...]
        qa = q_ref[...]
        doa = do_ref[...]
        qt = qa.astype(F32).T.astype(BF16)
        dot_ = doa.astype(F32).T.astype(BF16)
        khs = [kw[:, HD * kv:HD * kv + HD] for kv in range(2)]
        vhs = [vw[:, HD * kv:HD * kv + HD] for kv in range(2)]

        def logits(h):
            return (lax.dot_general(qa[:, HD * h:HD * h + HD], khs[h // GRP], NT, preferred_element_type=F32),
                    lax.dot_general(doa[:, HD * h:HD * h + HD], vhs[h // GRP], NT, preferred_element_type=F32))

        sd = [logits(0)]
        dqs = []
        dkt = [jnp.zeros((HD, W), F32), jnp.zeros((HD, W), F32)]
        dvt = [jnp.zeros((HD, W), F32), jnp.zeros((HD, W), F32)]
        for h in range(NH):
            kv = h // GRP
            if h + 1 < NH:
                sd.append(logits(h + 1))
            s, dp = sd[h]
            pm = jnp.exp2(jnp.where(valid, s, NEG) - lse_i[:, h:h + 1])
            ds = (pm * (dp - dl_i[:, h:h + 1])).astype(BF16)
            dvt[kv] = dvt[kv] + jnp.dot(dot_[HD * h:HD * h + HD, :], pm.astype(BF16), preferred_element_type=F32)
            dkt[kv] = dkt[kv] + jnp.dot(qt[HD * h:HD * h + HD, :], ds, preferred_element_type=F32)
            dqs.append(jnp.dot(ds, khs[kv], preferred_element_type=F32))
        for p in range(4):
            dq_ref[:, 128 * p:128 * p + 128] = jnp.concatenate(dqs[2 * p:2 * p + 2], axis=1)
        for half in range(W // tq):
            dk_sc[i + half] += jnp.concatenate([d[:, tq * half:tq * half + tq] for d in dkt], axis=0)
            dv_sc[i + half] += jnp.concatenate([d[:, tq * half:tq * half + tq] for d in dvt], axis=0)
        psd = jnp.exp2(sink_ref[...] - lse_i) * dl_i
        r = lax.broadcasted_iota(jnp.int32, (NH, 128), 0)
        c = lax.broadcasted_iota(jnp.int32, (NH, 128), 1)
        row = jnp.dot(jnp.sum(psd, axis=0, keepdims=True), (r == c).astype(F32),
                      preferred_element_type=F32, precision=lax.Precision.HIGHEST)
        ds_ref[...] -= jnp.broadcast_to(row, (8, 128))

        @pl.when(i == nq - 1)
        def _():
            c1 = pltpu.make_async_copy(dk_sc, dk_hbm, sem.at[0])
            c2 = pltpu.make_async_copy(dv_sc, dv_hbm, sem.at[1])
            c1.start()
            c2.start()
            c1.wait()
            c2.wait()

    any_spec = pl.BlockSpec(memory_space=pl.ANY)
    return pl.pallas_call(
        body, grid=(nq,),
        in_specs=[_rows(tq, BW), _const((S + 2 * WIN, 128)), _const((S + 2 * WIN, 128)), _const((1, NH)),
                  _rows(tq, BW), _rows(tq, NH), _rows(tq, NH)],
        out_specs=[_rows(tq, BW), any_spec, any_spec, _const((8, 128))],
        out_shape=[jax.ShapeDtypeStruct((S, BW), F32), jax.ShapeDtypeStruct((nc, 128, tq), F32),
                   jax.ShapeDtypeStruct((nc, 128, tq), F32), jax.ShapeDtypeStruct((8, 128), F32)],
        scratch_shapes=[pltpu.VMEM((nc, 128, tq), F32), pltpu.VMEM((nc, 128, tq), F32), pltpu.SemaphoreType.DMA((2,))],
        compiler_params=_params(1), name="attn_b_bwd",
    )(q, kp, vp, sink2, do, lse, delta)


def _qk_bwd(dqa_t, dka_t3, dva_t3, dqb, dkb, dvb, qar, kar, qg2, kg2, tab_a, tab_b, dga, dgb):
    S = dqb.shape[0]
    tm = min(256, S)
    per = dka_t3.shape[2] // tm

    def body(dqa_ref, dka_ref, dva_ref, dqb_ref, dkb_ref, dvb_ref, qar_ref, kar_ref, qg_ref, kg_ref, ta_ref, tb_ref,
             dga_ref, dgb_ref, dp_ref, st_ref):
        @pl.when(pl.program_id(0) == 0)
        def _():
            st_ref[...] = jnp.zeros_like(st_ref)

        seg = _seg_matrix(128, HD)

        def norm_bwd(dz_rot, raw, g):
            dzn = _rope_t(dz_rot, ta_ref, 16)
            raw = raw.astype(F32)
            rr = lax.rsqrt(_seg_sum(raw * raw, seg) * (1.0 / HD) + EPS)
            zhat = raw * rr
            dzh = dzn * g
            draw = rr * (dzh - zhat * (_seg_sum(dzh * zhat, seg) * (1.0 / HD)))
            return draw, jnp.sum(dzn * zhat, axis=0, keepdims=True)

        gq = jnp.zeros((1, 128), F32)
        for p in range(4):
            sl = slice(128 * p, 128 * p + 128)
            draw, gsum = norm_bwd(dqa_ref[sl, :].T * 0.125, qar_ref[:, sl], qg_ref[...])
            gq = gq + gsum
            dp_ref[:, sl] = draw.astype(BF16)
            dp_ref[:, 768 + 128 * p:768 + 128 * p + 128] = _rope_t(dqb_ref[:, sl] * 0.125, tb_ref, 32).astype(BF16)
        draw, gk = norm_bwd(dka_ref[...].T * LN2, kar_ref[...], kg_ref[...])
        dp_ref[:, 512:640] = draw.astype(BF16)
        dp_ref[:, 640:768] = dva_ref[...].T.astype(BF16)
        dp_ref[:, 1280:1408] = _rope_t(dkb_ref[...] * LN2, tb_ref, 32).astype(BF16)
        dp_ref[:, 1408:1536] = dvb_ref[...].astype(BF16)
        dp_ref[:, 1536:2560] = dga_ref[...]
        dp_ref[:, 2560:3584] = dgb_ref[...]
        st_ref[0:1, :] += gq
        st_ref[1:2, :] += gk

    tab = pl.BlockSpec((3, tm, 128), lambda i: (0, i, 0))
    chunk_t = pl.BlockSpec((None, 128, tm), lambda i: (i // per, 0, i % per))
    return pl.pallas_call(
        body, grid=(S // tm,),
        in_specs=[pl.BlockSpec((BW, tm), lambda i: (0, i)), chunk_t, chunk_t, _rows(tm, BW), _rows(tm, 128), _rows(tm, 128),
                  _rows(tm, BW), _rows(tm, 128), _const((1, 128)), _const((1, 128)), tab, tab, _rows(tm, D), _rows(tm, D)],
        out_specs=[_rows(tm, INW), _const((8, 128))],
        out_shape=[jax.ShapeDtypeStruct((S, INW), BF16), jax.ShapeDtypeStruct((8, 128), F32)],
        compiler_params=_params(1), name="qk_bwd",
    )(dqa_t, dka_t3, dva_t3, dqb, dkb, dvb, qar, kar, qg2, kg2, tab_a, tab_b, dga, dgb)


def _in_bwd(dproj, win, x, dx1, modv, n1g, chunks):
    S = x.shape[0]
    tm = min(512, S)
    n = S // tm
    ts = min(256, tm)

    def body(dp_ref, w_ref, x_ref, dx1_ref, mod_ref, g_ref, c_hbm, gx_ref, st_ref, recv_hbm, send_sems, recv_sems):
        start, finish = _exchange_scatter([c_hbm], [recv_hbm], send_sems, recv_sems)

        @pl.when(pl.program_id(0) == 0)
        def _():
            start()
            st_ref[...] = jnp.zeros_like(st_ref)

        subs = [slice(ts * u, ts * u + ts) for u in range(tm // ts)]
        dhs = [lax.dot_general(dp_ref[rows, :], w_ref[...], NT, preferred_element_type=F32) for rows in subs]
        for rows, dh in zip(subs, dhs):
            xt = x_ref[rows, :]
            r = lax.rsqrt(jnp.mean(xt * xt, axis=-1, keepdims=True) + EPS)
            xn = xt * r
            st_ref[0:1, :] += jnp.sum(dh, axis=0, keepdims=True)
            st_ref[1:2, :] += jnp.sum(dh * xn, axis=0, keepdims=True)
            dxn = dh * (g_ref[...] * (1.0 + mod_ref[1:2, :]))
            gx_ref[rows, :] = dx1_ref[rows, :] + r * (dxn - xn * jnp.mean(dxn * xn, axis=-1, keepdims=True))
        pl.when(pl.program_id(0) == n - 1)(finish)

    any_spec = pl.BlockSpec(memory_space=pl.ANY)
    return pl.pallas_call(
        body, grid=(n,),
        in_specs=[_rows(tm, INW), _const((D, INW)), _rows(tm, D), _rows(tm, D), _const((6, D)), _const((1, D)), any_spec],
        out_specs=[_rows(tm, D), _const((8, D)), any_spec],
        out_shape=[jax.ShapeDtypeStruct((S, D), F32), jax.ShapeDtypeStruct((8, D), F32),
                   jax.ShapeDtypeStruct((N_DEV - 1,) + chunks.shape[1:], chunks.dtype)],
        scratch_shapes=_exchange_sems(1),
        compiler_params=_params(1), name="in_bwd",
    )(dproj, win, x, dx1, modv, n1g, chunks)


def _pack_small(st1, st2, stf, dg1, dg2, stqk, dsink, modv, n1g, n2g):
    def body(st1_ref, st2_ref, stf_ref, dg1_ref, dg2_ref, qk_ref, ds_ref, mod_ref, g1_ref, g2_ref, o_ref):
        a1, b1 = st1_ref[0:1, :], st1_ref[1:2, :]
        a2, b2 = st2_ref[0:1, :], st2_ref[1:2, :]
        r = lax.broadcasted_iota(jnp.int32, (128, D), 0)
        c = lax.broadcasted_iota(jnp.int32, (128, D), 1)
        fold_q = (c == r % HD).astype(F32)
        fold_k = (c == HD + r % HD).astype(F32)
        keep = (c == r).astype(F32)

        def place(v, sel):
            return jnp.dot(v, sel, preferred_element_type=F32, precision=lax.Precision.HIGHEST)

        loss = jnp.sum(stf_ref[1:2, :], axis=1, keepdims=True)
        lane = lax.broadcasted_iota(jnp.int32, (1, D), 1)
        rows = [a1, g1_ref[...] * b1, dg1_ref[0:1, :], a2, g2_ref[...] * b2, dg2_ref[0:1, :],
                (1.0 + mod_ref[1:2, :]) * b1, (1.0 + mod_ref[4:5, :]) * b2, stf_ref[0:1, :],
                place(qk_ref[0:1, :], fold_q) + place(qk_ref[1:2, :], fold_k),
                place(ds_ref[0:1, :], keep),
                jnp.where(lane == 0, loss, 0.0)]
        rows += [jnp.zeros((1, D), F32)] * (SMALL_ROWS - len(rows))
        for n, v in enumerate(rows):
            o_ref[n:n + 1, :] = v

    return pl.pallas_call(
        body, out_shape=jax.ShapeDtypeStruct((SMALL_ROWS, D), F32),
        compiler_params=pltpu.CompilerParams(vmem_limit_bytes=V7X_VMEM_LIMIT), name="pack_small",
    )(st1, st2, stf, dg1, dg2, stqk, dsink, modv, n1g, n2g)


def _wada_grad(silu_all, dmod_cols):
    def body(a_ref, b_ref, o_ref):
        o_ref[...] = lax.dot_general(a_ref[...], b_ref[...], TN, preferred_element_type=F32, precision=lax.Precision.HIGHEST)

    return pl.pallas_call(
        body, out_shape=jax.ShapeDtypeStruct((D, dmod_cols.shape[1]), F32),
        compiler_params=pltpu.CompilerParams(vmem_limit_bytes=V7X_VMEM_LIMIT), name="wada_grad",
    )(silu_all, dmod_cols)


def _adamw_sum(parts, w, m, v, name):
    R, C = w.shape
    tr = R if R <= 64 else next(t for t in (256, 128, 64, 32, 16, 8) if R % t == 0)
    n = len(parts)
    dyn = [idx for _, idx in parts if idx is not None and not isinstance(idx, int)]
    b1c = 1.0 - ADAM_B1 ** ADAM_STEP
    b2c = 1.0 - ADAM_B2 ** ADAM_STEP

    def body(*refs):
        refs = refs[len(dyn):]
        g = refs[0][...].astype(F32)
        for k in range(1, n):
            g = g + refs[k][...].astype(F32)
        w_ref, m_ref, v_ref, g_out, d_out, m_out, v_out = refs[n:]
        mn = ADAM_B1 * m_ref[...] + (1.0 - ADAM_B1) * g
        vn = ADAM_B2 * v_ref[...] + (1.0 - ADAM_B2) * jnp.square(g)
        g_out[...] = g
        m_out[...] = mn
        v_out[...] = vn
        d_out[...] = -ADAM_LR * ((mn / b1c) / (jnp.sqrt(vn / b2c) + ADAM_EPS) + ADAM_WD * w_ref[...])

    in_specs = []
    nd = 0
    for a, idx in parts:
        if idx is None:
            in_specs.append(pl.BlockSpec((tr, C), lambda i, *s: (i, 0)))
        elif isinstance(idx, int):
            in_specs.append(pl.BlockSpec((None, tr, C), lambda i, *s, idx=idx: (idx, i, 0)))
        else:
            in_specs.append(pl.BlockSpec((None, tr, C), lambda i, *s, nd=nd: (s[nd][0], i, 0)))
            nd += 1
    blk = pl.BlockSpec((tr, C), lambda i, *s: (i, 0))
    grid_spec = pltpu.PrefetchScalarGridSpec(
        num_scalar_prefetch=len(dyn), grid=(R // tr,), in_specs=in_specs + [blk] * 3, out_specs=[blk] * 4)
    return pl.pallas_call(
        body, grid_spec=grid_spec, out_shape=[jax.ShapeDtypeStruct((R, C), F32)] * 4,
        compiler_params=_params(1), name=name,
    )(*dyn, *[a for a, _ in parts], w, m, v)


def _me():
    return lax.axis_index("x"), lax.axis_index("y"), lax.axis_index("c")


def _peer(k):
    x, y, c = _me()
    return (x ^ ((k >> 2) & 1), y ^ ((k >> 1) & 1), c ^ (k & 1))


def _ada_exchange(c_row, w_ada, b_rows):
    NW = w_ada.shape[1]

    def body(c_ref, w_ref, b_ref, sall_ref, mod_ref, src_ref, mp_ref, send1, recv1, send2, recv2):
        x, y, c = _me()
        me = 4 * x + 2 * y + c
        cv = c_ref[...]
        src_ref[...] = jnp.broadcast_to(cv * jax.nn.sigmoid(cv), (8, D))
        mine = pl.ds(pl.multiple_of(me * 8, 8), 8)
        sall_ref[mine, :] = src_ref[...]
        sends = [pltpu.make_async_remote_copy(src_ref, sall_ref.at[mine, :], send1.at[k - 1], recv1.at[k - 1],
                                              device_id=_peer(k), device_id_type=MESH) for k in range(1, N_DEV)]
        for cp in sends:
            cp.start()
        for k in range(1, N_DEV):
            theirs = pl.ds(pl.multiple_of((me ^ k) * 8, 8), 8)
            pltpu.make_async_remote_copy(src_ref, sall_ref.at[theirs, :], send1.at[k - 1], recv1.at[k - 1],
                                         device_id=_peer(k), device_id_type=MESH).wait_recv()
        for cp in sends:
            cp.wait_send()
        mp_ref[...] = jnp.dot(sall_ref[...], w_ref[...], preferred_element_type=F32, precision=lax.Precision.HIGHEST)
        mod_ref[mine, :] = mp_ref[mine, :] + b_ref[mine, :]
        sends = []
        for k in range(1, N_DEV):
            theirs = pl.ds(pl.multiple_of((me ^ k) * 8, 8), 8)
            sends.append(pltpu.make_async_remote_copy(mp_ref.at[theirs, :], mod_ref.at[mine, :], send2.at[k - 1], recv2.at[k - 1],
                                                      device_id=_peer(k), device_id_type=MESH))
        for cp in sends:
            cp.start()
        for k in range(1, N_DEV):
            theirs = pl.ds(pl.multiple_of((me ^ k) * 8, 8), 8)
            pltpu.make_async_remote_copy(mp_ref.at[mine, :], mod_ref.at[theirs, :], send2.at[k - 1], recv2.at[k - 1],
                                         device_id=_peer(k), device_id_type=MESH).wait_recv()
            mod_ref[theirs, :] = mod_ref[theirs, :] + b_ref[theirs, :]
        for cp in sends:
            cp.wait_send()

    vm = pl.BlockSpec(memory_space=pltpu.VMEM)
    return pl.pallas_call(
        body, in_specs=[vm, vm, vm], out_specs=[vm, vm],
        out_shape=[jax.ShapeDtypeStruct((8 * N_DEV, D), F32), jax.ShapeDtypeStruct((8 * N_DEV, NW), F32)],
        scratch_shapes=[pltpu.VMEM((8, D), F32), pltpu.VMEM((8 * N_DEV, NW), F32)]
        + [pltpu.SemaphoreType.DMA((N_DEV - 1,))] * 4,
        compiler_params=pltpu.CompilerParams(vmem_limit_bytes=V7X_VMEM_LIMIT), name="ada_exchange",
    )(c_row, w_ada, b_rows)


def _weight_gather(shard):
    def body(x_ref, out_ref, send_sems, recv_sems, local_sem):
        x, y, c = _me()
        me, sibling = (x, y, c), (x, y, 1 - c)
        chips = [(1 - x, y), (x, 1 - y), (1 - x, 1 - y)]

        def slot(px, py, pc):
            return out_ref.at[4 * px + 2 * py + pc]

        def copy(k, block, to, src=None):
            return pltpu.make_async_remote_copy(
                src_ref=slot(*block) if src is None else src, dst_ref=slot(*block),
                send_sem=send_sems.at[k], recv_sem=recv_sems.at[k], device_id=to, device_id_type=MESH)

        mine = pltpu.make_async_copy(x_ref, slot(*me), local_sem)
        mine.start()
        first = [copy(0, me, sibling, src=x_ref)]
        first += [copy(1 + j, me, (*chip, c), src=x_ref) for j, chip in enumerate(chips)]
        for cp in first:
            cp.start()
        passed = [copy(4 + j, (*chip, c), sibling) for j, chip in enumerate(chips)]
        for j, chip in enumerate(chips):
            copy(1 + j, (*chip, c), me).wait_recv()
            passed[j].start()
        copy(0, sibling, me).wait_recv()
        for j, chip in enumerate(chips):
            copy(4 + j, (*chip, 1 - c), me).wait_recv()
        for cp in first + passed:
            cp.wait_send()
        mine.wait()

    any_spec = pl.BlockSpec(memory_space=pl.ANY)
    return pl.pallas_call(
        body, in_specs=[any_spec], out_specs=any_spec,
        out_shape=jax.ShapeDtypeStruct((N_DEV,) + shard.shape, shard.dtype),
        scratch_shapes=[pltpu.SemaphoreType.DMA((7,)), pltpu.SemaphoreType.DMA((7,)), pltpu.SemaphoreType.DMA],
        name="weight_gather",
    )(shard)


def _small_gather(block):
    def body(b_ref, out_ref, send_sems, recv_sems):
        x, y, c = _me()
        me = 4 * x + 2 * y + c
        out_ref[me] = b_ref[...]
        sends = [pltpu.make_async_remote_copy(b_ref, out_ref.at[me], send_sems.at[k - 1], recv_sems.at[k - 1],
                                              device_id=_peer(k), device_id_type=MESH) for k in range(1, N_DEV)]
        for cp in sends:
            cp.start()
        for k in range(1, N_DEV):
            pltpu.make_async_remote_copy(b_ref, out_ref.at[me ^ k], send_sems.at[k - 1], recv_sems.at[k - 1],
                                         device_id=_peer(k), device_id_type=MESH).wait_recv()
        for cp in sends:
            cp.wait_send()

    vm = pl.BlockSpec(memory_space=pltpu.VMEM)
    return pl.pallas_call(
        body, in_specs=[vm], out_specs=vm,
        out_shape=jax.ShapeDtypeStruct((N_DEV,) + block.shape, block.dtype),
        scratch_shapes=[pltpu.SemaphoreType.DMA((N_DEV - 1,)), pltpu.SemaphoreType.DMA((N_DEV - 1,))],
        name="small_gather",
    )(block)


def _pack_small_params(b_ada, n1, n2, fg, qn, kn, sink):
    qk = jnp.concatenate([qn.reshape(1, HD), kn.reshape(1, HD), jnp.zeros((1, D - 2 * HD), F32)], axis=1)
    sk = jnp.concatenate([sink.reshape(1, NH), jnp.zeros((1, D - NH), F32)], axis=1)
    return jnp.concatenate([b_ada.reshape(6, D), n1.reshape(1, D), n2.reshape(1, D), fg.reshape(1, D), qk, sk,
                            jnp.zeros((SMALL_ROWS - 11, D), F32)], axis=0)


def _unpack_small(p):
    return (p[0:6].reshape(1, 6 * D), p[6].reshape(1, D), p[9, 0:HD].reshape(1, HD), p[9, HD:2 * HD].reshape(1, HD),
            p[10, 0:NH].reshape(1, NH), p[7].reshape(1, D), p[8].reshape(D))


def kernel(x, c, w_ada, b_ada, norm1_g, w_in, q_norm_a, k_norm_a, sink_b, w_branch, w_out, norm2_g, w_mlp_in, w_mlp_out, final_g, loss_target, m_w_ada, m_b_ada, m_norm1_g, m_w_in, m_q_norm_a, m_k_norm_a, m_sink_b, m_w_branch, m_w_out, m_norm2_g, m_w_mlp_in, m_w_mlp_out, m_final_g, v_w_ada, v_b_ada, v_norm1_g, v_w_in, v_q_norm_a, v_k_norm_a, v_sink_b, v_w_branch, v_w_out, v_norm2_g, v_w_mlp_in, v_w_mlp_out, v_final_g):
    S = x.shape[1]
    xs = x.reshape(S, D)
    tgt = loss_target.reshape(S, D)
    ax, ay, ac = lax.axis_index("x"), lax.axis_index("y"), lax.axis_index("c")
    me = 4 * ax + 2 * ay + ac
    me1 = me.reshape(1).astype(jnp.int32)
    NW = w_ada.shape[2]
    NI = w_in.shape[2]

    silu64, mod64 = _ada_exchange(c.reshape(1, D), w_ada.reshape(D, NW),
                                  jnp.repeat(b_ada.reshape(N_DEV, NW), 8, axis=0))
    silu_all = silu64[0::8]
    modv = mod64[0::8].reshape(6, D)

    win = _weight_gather(w_in[0].astype(BF16)).transpose(1, 0, 2).reshape(D, INW)
    rest_shards = tuple(w[0].astype(BF16) for w in (w_branch, w_out, w_mlp_in, w_mlp_out))

    tab_a, tab_b = _rope_tables(S)
    qg2 = jnp.tile(q_norm_a.reshape(1, HD), (1, 2))
    kg2 = jnp.tile(k_norm_a.reshape(1, HD), (1, 2))
    n1g = norm1_g.reshape(1, D)
    n2g = norm2_g.reshape(1, D)
    fg = final_g.reshape(1, D)
    sink2 = sink_b.reshape(1, NH) * LOG2E

    h, qar, kar, qa, ka, va, qb, kb, vb, ga, gb, qa_t, ka_t3, va_t3 = _in_proj(xs, modv, n1g, win, qg2, kg2, tab_a, tab_b)
    ya, lse_at, wb, wout, wmi, wmo = _attn_a_fwd(qa_t, ka, va_t3, rest_shards)
    lse_a = lse_at.T
    wb = wb.transpose(1, 2, 0, 3).reshape(2, BW, D)
    wout = wout.reshape(D, D)
    pad = ((WIN, WIN), (0, 0))
    kbp, vbp = jnp.pad(kb, pad), jnp.pad(vb, pad)
    tb = min(TQ_B, S)
    yb, lse_bt = _attn_b_fwd(qb, kbp, vbp.reshape((S + 2 * WIN) // tb, tb, 128).transpose(0, 2, 1), sink2)
    lse_b = lse_bt.T
    x1, merged, ua, ub = _merge_out(ya, yb, ga, gb, xs, modv, wb, wout)
    h2, hp, dx2, stf = _mlp_fwd(x1, modv, n2g, wmi, wmo, fg, tgt)

    dhp, dx1, st2 = _mlp_bwd(dx2, x1, hp, modv, n2g, wmi, wmo)
    m2 = _tn_matmul(hp, dx2, 2048, D, "dw_mlp_out", relu_sq=True)
    g_wmo, dg2 = _scale_gate(m2, wmo.reshape(FF, D), modv, 5, "gate2_grad")
    g_wmi = _tn_matmul(h2, dhp, D, 512, "dw_mlp_in", dev_major=True, rows=4096)
    dua, dub, dga, dgb, dya, dyb, dl_a, dl_b, dya_t = _merge_bwd(dx1, modv, ga, gb, ua, ub, ya, yb, wb, wout)
    m1 = _tn_matmul(merged, dx1, D, D, "dw_out", rows=2048)
    g_wout, dg1 = _scale_gate(m1, wout, modv, 2, "gate1_grad")
    g_wb0 = _tn_matmul(ya, dua, BW, D, "dw_branch_a")
    g_wb1 = _tn_matmul(yb, dub, BW, D, "dw_branch_b")
    g_wb = jnp.stack([g_wb0, g_wb1]).reshape(2, BW, N_DEV, 128).transpose(2, 0, 1, 3).reshape(N_DEV, 2 * BW, 128)
    g_wout = g_wout.reshape(N_DEV, 128, D)
    g_wmo = g_wmo.reshape(N_DEV, 512, D)
    dqa_t, dka_t3, dva_t3, r_wb, r_wout, r_wmi, r_wmo = _attn_a_bwd(qa, qa_t, ka_t3, va_t3, dya, dya_t, lse_a, dl_a,
                                                                    (g_wb, g_wout, g_wmi, g_wmo))
    dqb, dkb_t, dvb_t, dsink = _attn_b_bwd(qb, kbp, vbp, sink2, dyb, lse_b, dl_b)
    dkb = dkb_t.transpose(0, 2, 1).reshape(S + 2 * WIN, 128)[WIN:WIN + S]
    dvb = dvb_t.transpose(0, 2, 1).reshape(S + 2 * WIN, 128)[WIN:WIN + S]
    dproj, stqk = _qk_bwd(dqa_t, dka_t3, dva_t3, dqb, dkb, dvb, qar, kar, qg2, kg2, tab_a, tab_b, dga, dgb)
    g_win = _tn_matmul(h, dproj, D, 896, "dw_in", rows=4096)
    g_win = g_win.reshape(D, N_DEV, NI).transpose(1, 0, 2)
    grad_x, st1, r_win = _in_bwd(dproj, win, xs, dx1, modv, n1g, g_win.astype(BF16))

    def adam(name, own, recv, w, m, v):
        shape = w.shape
        w2, m2_, v2 = (a.reshape(own.shape[1:]) for a in (w, m, v))
        outs = _adamw_sum([(own, me1)] + [(recv, k) for k in range(N_DEV - 1)], w2, m2_, v2, name)
        return [a.reshape(shape) for a in outs]

    o_win = adam("adamw_w_in", g_win, r_win, w_in, m_w_in, v_w_in)
    o_wb = adam("adamw_w_branch", g_wb, r_wb, w_branch, m_w_branch, v_w_branch)
    o_wout = adam("adamw_w_out", g_wout, r_wout, w_out, m_w_out, v_w_out)
    o_wmi = adam("adamw_w_mlp_in", g_wmi, r_wmi, w_mlp_in, m_w_mlp_in, v_w_mlp_in)
    o_wmo = adam("adamw_w_mlp_out", g_wmo, r_wmo, w_mlp_out, m_w_mlp_out, v_w_mlp_out)

    small = _pack_small(st1, st2, stf, dg1, dg2, stqk, dsink, modv, n1g, n2g)
    small_all = _small_gather(small)
    sw = _pack_small_params(b_ada, norm1_g, norm2_g, final_g, q_norm_a, k_norm_a, sink_b)
    sm = _pack_small_params(m_b_ada, m_norm1_g, m_norm2_g, m_final_g, m_q_norm_a, m_k_norm_a, m_sink_b)
    sv = _pack_small_params(v_b_ada, v_norm1_g, v_norm2_g, v_final_g, v_q_norm_a, v_k_norm_a, v_sink_b)
    sm_out = _adamw_sum([(small_all, k) for k in range(N_DEV)], sw, sm, sv, "adamw_small")
    loss = sm_out[0][11, 0]
    sm_out = [_unpack_small(a) for a in sm_out]

    dmod_all = small_all[:, 0:6, :].reshape(N_DEV, 6 * D)
    dmod_cols = lax.dynamic_slice_in_dim(dmod_all, me * NW, NW, axis=1)
    g_wada = _wada_grad(silu_all, dmod_cols)
    ada = _adamw_sum([(g_wada, None)], w_ada.reshape(D, NW), m_w_ada.reshape(D, NW), v_w_ada.reshape(D, NW), "adamw_ada")
    ada = [a.reshape(1, D, NW) for a in ada]

    def leaves(k):
        b_, n1_, qn_, kn_, sk_, n2_, fg_ = sm_out[k]
        return [ada[k], b_, n1_, o_win[k], qn_, kn_, sk_, o_wb[k], o_wout[k], n2_, o_wmi[k], o_wmo[k], fg_]

    return (loss, grad_x.reshape(1, S, D), *leaves(0), *leaves(1), *leaves(2), *leaves(3))
```

```python
import jax
import jax.numpy as jnp
from jax import lax
from jax.experimental import pallas as pl
from jax.experimental.pallas import tpu as pltpu

F32, BF16 = jnp.float32, jnp.bfloat16
MESH = pl.DeviceIdType.MESH

D = 1024
HD = 64
NH = 8
GRP = 4
BW = 512
FF = 4096
INW = 3584
GRID_W = 64
WIN = 128
THETA = 10000.0
EPS = 1e-6
NEG = -1e30
N_DEV = 8
LOG2E = 1.4426950408889634
LN2 = 0.6931471805599453
QA_SCALE = 0.125 * LOG2E
SMALL_ROWS = 16
MLP_SHARDS = 4
MLP_ROWS = 512
TK_A = 512
TK_A_FWD = 2048
TQ_A_FWD = 512
V7X_VMEM_LIMIT = 56 * 1024 * 1024

ADAM_LR, ADAM_B1, ADAM_B2, ADAM_EPS, ADAM_WD, ADAM_STEP = 0.001, 0.9, 0.999, 1e-08, 0.01, 10

NT = (((1,), (1,)), ((), ()))
TN = (((0,), (0,)), ((), ()))


def _params(n_axes, vmem=V7X_VMEM_LIMIT):
    return pltpu.CompilerParams(dimension_semantics=("arbitrary",) * n_axes, vmem_limit_bytes=vmem)


def _const(shape):
    return pl.BlockSpec(shape, lambda *_: (0,) * len(shape))


def _rows(tm, width):
    return pl.BlockSpec((tm, width), lambda i, *_: (i, 0))


def _seg_matrix(n, seg):
    r = lax.broadcasted_iota(jnp.int32, (n, n), 0) // seg
    c = lax.broadcasted_iota(jnp.int32, (n, n), 1) // seg
    return (r == c).astype(BF16)


def _seg_sum(z, seg_mat):
    hi = z.astype(BF16)
    lo = (z - hi.astype(F32)).astype(BF16)
    return jnp.dot(hi, seg_mat, preferred_element_type=F32) + jnp.dot(lo, seg_mat, preferred_element_type=F32)


def _rope(z, t_ref, sh):
    return z * t_ref[0] + pltpu.roll(z, sh, 1) * t_ref[1] + pltpu.roll(z, 128 - sh, 1) * t_ref[2]


def _rope_t(dz, t_ref, sh):
    return dz * t_ref[0] + pltpu.roll(dz * t_ref[1], 128 - sh, 1) + pltpu.roll(dz * t_ref[2], sh, 1)


def _rope_tables(S):
    t = jnp.arange(S, dtype=jnp.int32)[:, None]
    lane = jnp.arange(128, dtype=jnp.int32)[None, :] % HD

    def build(ang, first):
        cos, sin = jnp.cos(ang), jnp.sin(ang)
        return jnp.stack([cos, jnp.where(first, 0.0, sin), jnp.where(first, -sin, 0.0)]).astype(F32)

    inv_a = (THETA ** (-jnp.arange(0, HD // 2, 2, dtype=F32) / (HD // 2)))[lane % 16]
    pos_a = jnp.where(lane < HD // 2, t // GRID_W, t % GRID_W).astype(F32)
    tab_a = build(pos_a * inv_a, (lane % 32) < 16)
    inv_b = (THETA ** (-jnp.arange(0, HD, 2, dtype=F32) / HD))[lane % 32]
    tab_b = build(t.astype(F32) * inv_b, lane < 32)
    return tab_a, tab_b


def _in_proj(x, modv, n1g, win, qg2, kg2, tab_a, tab_b):
    S = x.shape[0]
    tm = min(512, S)
    tk = min(TK_A, S)
    per = tk // tm
    ts = min(256, tm)

    def body(x_ref, mod_ref, g_ref, w_ref, qg_ref, kg_ref, ta_ref, tb_ref,
             h_ref, qar_ref, kar_ref, qa_ref, ka_ref, va_ref, qb_ref, kb_ref, vb_ref, ga_ref, gb_ref, qat_ref, kat_ref, vat_ref):
        seg = _seg_matrix(128, HD)

        def head_norm(z, g):
            ms = _seg_sum(z * z, seg) * (1.0 / HD)
            return (z * lax.rsqrt(ms + EPS)) * g

        subs = [slice(ts * u, ts * u + ts) for u in range(tm // ts)]
        hbs = []
        for rows in subs:
            xt = x_ref[rows, :]
            r = lax.rsqrt(jnp.mean(xt * xt, axis=-1, keepdims=True) + EPS)
            h = ((xt * r) * g_ref[...]) * (1.0 + mod_ref[1:2, :]) + mod_ref[0:1, :]
            hbs.append(h.astype(BF16))
            h_ref[rows, :] = hbs[-1]
        projs = [jnp.dot(hb, w_ref[...], preferred_element_type=F32) for hb in hbs]
        for rows, proj in zip(subs, projs):
            ta, tb = ta_ref[:, rows, :], tb_ref[:, rows, :]
            for p in range(4):
                z = proj[:, 128 * p:128 * p + 128]
                qar_ref[rows, 128 * p:128 * p + 128] = z.astype(BF16)
                qv = _rope(head_norm(z, qg_ref[...]), ta, 16) * QA_SCALE
                qa_ref[rows, 128 * p:128 * p + 128] = qv.astype(BF16)
                qat_ref[128 * p:128 * p + 128, rows] = qv.T.astype(BF16)
                zb = proj[:, 768 + 128 * p:768 + 128 * p + 128]
                qb_ref[rows, 128 * p:128 * p + 128] = (_rope(zb, tb, 32) * QA_SCALE).astype(BF16)
            z = proj[:, 512:640]
            kar_ref[rows, :] = z.astype(BF16)
            kv_ = _rope(head_norm(z, kg_ref[...]), ta, 16)
            ka_ref[rows, :] = kv_.astype(BF16)
            kat_ref[:, rows] = kv_.T.astype(BF16)
            va_ref[rows, :] = proj[:, 640:768].astype(BF16)
            vat_ref[:, rows] = proj[:, 640:768].T.astype(BF16)
            kb_ref[rows, :] = _rope(proj[:, 1280:1408], tb, 32).astype(BF16)
            vb_ref[rows, :] = proj[:, 1408:1536].astype(BF16)
            ga_ref[rows, :] = proj[:, 1536:2560].astype(BF16)
            gb_ref[rows, :] = proj[:, 2560:3584].astype(BF16)

    tab = pl.BlockSpec((3, tm, 128), lambda i: (0, i, 0))
    shapes = [(D, BF16), (BW, BF16), (128, BF16), (BW, BF16), (128, BF16), (128, BF16),
              (BW, BF16), (128, BF16), (128, BF16), (D, BF16), (D, BF16)]
    return pl.pallas_call(
        body, grid=(S // tm,),
        in_specs=[_rows(tm, D), _const((6, D)), _const((1, D)), _const((D, INW)), _const((1, 128)), _const((1, 128)), tab, tab],
        out_specs=[_rows(tm, w) for w, _ in shapes] + [pl.BlockSpec((BW, tm), lambda i: (0, i))]
        + [pl.BlockSpec((None, 128, tm), lambda i: (i // per, 0, i % per))] * 2,
        out_shape=[jax.ShapeDtypeStruct((S, w), dt) for w, dt in shapes] + [jax.ShapeDtypeStruct((BW, S), BF16)]
        + [jax.ShapeDtypeStruct((S // tk, 128, tk), BF16)] * 2,
        compiler_params=_params(1), name="in_proj",
    )(x, modv, n1g, win, qg2, kg2, tab_a, tab_b)


def _exchange_gather(block_refs, out_refs, send_sems, recv_sems, local_sems):
    x, y, c = _me()
    me = 4 * x + 2 * y + c

    def copies():
        own, out, arrive = [], [], []
        for a, (blk, dst) in enumerate(zip(block_refs, out_refs)):
            own.append(pltpu.make_async_copy(blk, dst.at[me], local_sems.at[a]))
            for k in range(1, N_DEV):
                sems = dict(send_sem=send_sems.at[a, k - 1], recv_sem=recv_sems.at[a, k - 1], device_id=_peer(k), device_id_type=MESH)
                out.append(pltpu.make_async_remote_copy(blk, dst.at[me], **sems))
                arrive.append(pltpu.make_async_remote_copy(blk, dst.at[me ^ k], **sems))
        return own, out, arrive

    def start():
        own, out, _ = copies()
        for cp in own + out:
            cp.start()

    def finish():
        own, out, arrive = copies()
        for cp in arrive:
            cp.wait_recv()
        for cp in out:
            cp.wait_send()
        for cp in own:
            cp.wait()

    return start, finish


def _exchange_scatter(chunk_refs, recv_refs, send_sems, recv_sems):
    x, y, c = _me()
    me = 4 * x + 2 * y + c

    def copies():
        return [pltpu.make_async_remote_copy(src.at[me ^ k], dst.at[k - 1], send_sems.at[a, k - 1], recv_sems.at[a, k - 1],
                                             device_id=_peer(k), device_id_type=MESH)
                for a, (src, dst) in enumerate(zip(chunk_refs, recv_refs)) for k in range(1, N_DEV)]

    def start():
        for cp in copies():
            cp.start()

    def finish():
        cps = copies()
        for cp in cps:
            cp.wait_recv()
        for cp in cps:
            cp.wait_send()

    return start, finish


def _exchange_sems(n):
    return [pltpu.SemaphoreType.DMA((n, N_DEV - 1)), pltpu.SemaphoreType.DMA((n, N_DEV - 1))]


def _attn_a_fwd(qt, k, vt3, shards):
    S = qt.shape[1]
    tq = min(TQ_A_FWD, S)
    nq = S // tq
    per = max(1, min(TK_A_FWD, S) // vt3.shape[2])
    tk = per * vt3.shape[2]
    nk = S // tk
    ONES = 16
    AHEAD = 2
    ns = len(shards)

    def body(q_ref, k_ref, vt_ref, *rest):
        w_hbm, (o_ref, lse_ref), wall_hbm = rest[:ns], rest[ns:ns + 2], rest[ns + 2:2 * ns + 2]
        st_sc, send_sems, recv_sems, local_sems = rest[2 * ns + 2:]
        start, finish = _exchange_gather(w_hbm, wall_hbm, send_sems, recv_sems, local_sems)
        pl.when(pl.program_id(0) == 0)(start)
        row8 = lax.broadcasted_iota(jnp.int32, (NH, tq), 0)
        lse_all = jnp.zeros((NH, tq), F32)
        ones = jnp.ones((ONES, tk), BF16)
        for kv in range(2):
            qts = [q_ref[HD * (GRP * kv + t):HD * (GRP * kv + t) + HD, :] for t in range(GRP)]

            def keys(j, kv=kv):
                off = j * tk if isinstance(j, int) else pl.multiple_of(j * tk, tk)
                return k_ref[pl.ds(off, tk), :][:, HD * kv:HD * kv + HD]

            def scores(kj, t, qts=qts):
                return jnp.dot(kj, qts[t], preferred_element_type=F32)

            def step(j, carry, kv=kv, last=False):
                kj = keys(j)
                kn = None if last else keys(j + 1)
                vt = jnp.concatenate([vt_ref[per * j + u, HD * kv:HD * kv + HD, :] for u in range(per)], axis=1)
                v1 = jnp.concatenate([vt, ones], axis=0)
                sts = [st_sc[t] for t in range(AHEAD)]
                new = []
                for t in range(GRP):
                    m, acc = carry[2 * t], carry[2 * t + 1]
                    if t + AHEAD < GRP:
                        sts.append(scores(kj, t + AHEAD))
                    st = sts[t]
                    mn = jnp.maximum(m, jnp.max(st, axis=0, keepdims=True))
                    pt = jnp.exp2(st - mn)
                    if t + AHEAD >= GRP and not last:
                        st_sc[t + AHEAD - GRP] = scores(kn, t + AHEAD - GRP)
                    acc = jnp.exp2(m - mn) * acc + jnp.dot(v1, pt.astype(BF16), preferred_element_type=F32)
                    new += [mn, acc]
                return tuple(new)

            k0 = keys(0)
            for t in range(AHEAD):
                st_sc[t] = scores(k0, t)
            init = (jnp.full((1, tq), NEG, F32), jnp.zeros((HD + ONES, tq), F32)) * GRP
            res = step(nk - 1, lax.fori_loop(0, nk - 1, step, init), last=True)
            outs = []
            for t in range(GRP):
                m, acc = res[2 * t], res[2 * t + 1]
                l = acc[HD:HD + 1, :]
                outs.append((acc[:HD, :] / l).T)
                lse_all = jnp.where(row8 == GRP * kv + t, m + jnp.log2(l), lse_all)
            o_ref[:, 256 * kv:256 * kv + 256] = jnp.concatenate(outs, axis=1).astype(BF16)
        lse_ref[...] = lse_all
        pl.when(pl.program_id(0) == nq - 1)(finish)

    any_spec = pl.BlockSpec(memory_space=pl.ANY)
    return pl.pallas_call(
        body, grid=(nq,),
        in_specs=[pl.BlockSpec((BW, tq), lambda i: (0, i)), _const((S, 128)), _const(vt3.shape)] + [any_spec] * ns,
        out_specs=[_rows(tq, BW), pl.BlockSpec((NH, tq), lambda i: (0, i))] + [any_spec] * ns,
        out_shape=[jax.ShapeDtypeStruct((S, BW), BF16), jax.ShapeDtypeStruct((NH, S), F32)]
        + [jax.ShapeDtypeStruct((N_DEV,) + s.shape, s.dtype) for s in shards],
        scratch_shapes=[pltpu.VMEM((AHEAD, tk, tq), F32)] + _exchange_sems(ns) + [pltpu.SemaphoreType.DMA((ns,))],
        compiler_params=_params(1), name="attn_a_fwd",
    )(qt, k, vt3, *shards)


def _window_mask(i, tq, S):
    W = tq + 2 * WIN
    r = lax.broadcasted_iota(jnp.int32, (tq, W), 0)
    c = lax.broadcasted_iota(jnp.int32, (tq, W), 1)
    kpos = i * tq - WIN + c
    return (jnp.abs(c - WIN - r) <= WIN) & (kpos >= 0) & (kpos < S)


TQ_B = 256


def _attn_b_fwd(q, kp, vpt3, sink2):
    S = q.shape[0]
    tq = min(TQ_B, S)
    W = tq + 2 * WIN
    nc = vpt3.shape[0]
    ONES = 16

    def body(q_ref, k_ref, vt_ref, sink_ref, o_ref, lse_ref):
        i = pl.program_id(0)
        off = pl.multiple_of(i * tq, tq)
        r = lax.broadcasted_iota(jnp.int32, (W, tq), 1)
        c = lax.broadcasted_iota(jnp.int32, (W, tq), 0)
        kpos = i * tq - WIN + c
        valid = (jnp.abs(c - WIN - r) <= WIN) & (kpos >= 0) & (kpos < S)
        kw = k_ref[pl.ds(off, W), :]
        vt = jnp.concatenate([vt_ref[i + half] for half in range(W // tq)], axis=1)
        ones = jnp.ones((ONES, W), BF16)
        row8 = lax.broadcasted_iota(jnp.int32, (NH, tq), 0)
        lse_all = jnp.zeros((NH, tq), F32)
        qs = []
        for p in range(4):
            qp = q_ref[:, 128 * p:128 * p + 128]
            qs += [qp[:, :HD], qp[:, HD:]]
        khs = [kw[:, HD * kv:HD * kv + HD] for kv in range(2)]
        v1s = [jnp.concatenate([vt[HD * kv:HD * kv + HD, :], ones], axis=0) for kv in range(2)]

        def scores(h):
            return lax.dot_general(khs[h // GRP], qs[h], NT, preferred_element_type=F32)

        ss = [scores(0), scores(1)]
        outs = []
        for h in range(NH):
            if h + 2 < NH:
                ss.append(scores(h + 2))
            st = jnp.where(valid, ss[h], NEG)
            sk = sink_ref[:, h:h + 1]
            m = jnp.maximum(jnp.max(st, axis=0, keepdims=True), sk)
            acc = jnp.dot(v1s[h // GRP], jnp.exp2(st - m).astype(BF16), preferred_element_type=F32)
            l = acc[HD:HD + 1, :] + jnp.exp2(sk - m)
            outs.append((acc[:HD, :] / l).T)
            lse_all = jnp.where(row8 == h, m + jnp.log2(l), lse_all)
        for p in range(4):
            o_ref[:, 128 * p:128 * p + 128] = jnp.concatenate(outs[2 * p:2 * p + 2], axis=1).astype(BF16)
        lse_ref[...] = lse_all

    return pl.pallas_call(
        body, grid=(S // tq,),
        in_specs=[_rows(tq, BW), _const((S + 2 * WIN, 128)), _const((nc, 128, tq)), _const((1, NH))],
        out_specs=[_rows(tq, BW), pl.BlockSpec((NH, tq), lambda i: (0, i))],
        out_shape=[jax.ShapeDtypeStruct((S, BW), BF16), jax.ShapeDtypeStruct((NH, S), F32)],
        compiler_params=_params(1), name="attn_b_fwd",
    )(q, kp, vpt3, sink2)


def _merge_out(ya, yb, ga, gb, x, modv, wb, wout):
    S = x.shape[0]
    tm = min(256, S)

    def body(ya_ref, yb_ref, ga_ref, gb_ref, x_ref, mod_ref, wb_ref, wo_ref, x1_ref, mg_ref, ua_ref, ub_ref):
        ua = jnp.dot(ya_ref[...], wb_ref[0], preferred_element_type=F32)
        ub = jnp.dot(yb_ref[...], wb_ref[1], preferred_element_type=F32)
        merged = jax.nn.sigmoid(ga_ref[...].astype(F32)) * ua + jax.nn.sigmoid(gb_ref[...].astype(F32)) * ub
        mb = merged.astype(BF16)
        ua_ref[...] = ua.astype(BF16)
        ub_ref[...] = ub.astype(BF16)
        mg_ref[...] = mb
        x1_ref[...] = x_ref[...] + mod_ref[2:3, :] * jnp.dot(mb, wo_ref[...], preferred_element_type=F32)

    return pl.pallas_call(
        body, grid=(S // tm,),
        in_specs=[_rows(tm, BW), _rows(tm, BW), _rows(tm, D), _rows(tm, D), _rows(tm, D), _const((6, D)),
                  _const((2, BW, D)), _const((D, D))],
        out_specs=[_rows(tm, D)] * 4,
        out_shape=[jax.ShapeDtypeStruct((S, D), F32)] + [jax.ShapeDtypeStruct((S, D), BF16)] * 3,
        compiler_params=_params(1), name="merge_out",
    )(ya, yb, ga, gb, x, modv, wb, wout)


def _mlp_fwd(x1, modv, n2g, wmi, wmo, fg, target):
    S = x1.shape[0]
    tm = min(MLP_ROWS, S)
    tf = wmi.shape[2]
    nj = wmi.shape[0] // MLP_SHARDS

    def body(x1_ref, mod_ref, g_ref, wi_ref, wo_ref, fg_ref, t_ref, h2_ref, hp_ref, dx2_ref, st_ref, acc_ref):
        i, j = pl.program_id(0), pl.program_id(1)

        @pl.when(j == 0)
        def _():
            xt = x1_ref[...]
            r = lax.rsqrt(jnp.mean(xt * xt, axis=-1, keepdims=True) + EPS)
            h2 = ((xt * r) * g_ref[...]) * (1.0 + mod_ref[4:5, :]) + mod_ref[3:4, :]
            h2_ref[...] = h2.astype(BF16)
            acc_ref[...] = jnp.zeros_like(acc_ref)

        @pl.when((i == 0) & (j == 0))
        def _():
            st_ref[...] = jnp.zeros_like(st_ref)

        out = None
        hps = [jnp.dot(h2_ref[...], wi_ref[0], preferred_element_type=F32)]
        for u in range(MLP_SHARDS):
            if u + 1 < MLP_SHARDS:
                hps.append(jnp.dot(h2_ref[...], wi_ref[u + 1], preferred_element_type=F32))
            hp = hps[u]
            hp_ref[:, tf * u:tf * u + tf] = hp.astype(BF16)
            hid = jnp.square(jnp.maximum(hp, 0.0))
            part = jnp.dot(hid.astype(BF16), wo_ref[u], preferred_element_type=F32)
            out = part if out is None else out + part
        acc_ref[...] += out

        @pl.when(j == nj - 1)
        def _():
            x2 = x1_ref[...] + mod_ref[5:6, :] * acc_ref[...]
            r3 = lax.rsqrt(jnp.mean(x2 * x2, axis=-1, keepdims=True) + EPS)
            xn = x2 * r3
            err = xn * fg_ref[...] - t_ref[...]
            dy = err * (1.0 / D)
            gy = dy * fg_ref[...]
            dx2_ref[...] = r3 * (gy - xn * jnp.mean(gy * xn, axis=-1, keepdims=True))
            st_ref[0:1, :] += jnp.sum(dy * xn, axis=0, keepdims=True)
            st_ref[1:2, :] += jnp.sum(err * err, axis=0, keepdims=True) * (0.5 / D)

    return pl.pallas_call(
        body, grid=(S // tm, nj),
        in_specs=[pl.BlockSpec((tm, D), lambda i, j: (i, 0)), _const((6, D)), _const((1, D)),
                  pl.BlockSpec((MLP_SHARDS, D, tf), lambda i, j: (j, 0, 0)), pl.BlockSpec((MLP_SHARDS, tf, D), lambda i, j: (j, 0, 0)),
                  _const((1, D)), pl.BlockSpec((tm, D), lambda i, j: (i, 0))],
        out_specs=[pl.BlockSpec((tm, D), lambda i, j: (i, 0)), pl.BlockSpec((tm, MLP_SHARDS * tf), lambda i, j: (i, j)),
                   pl.BlockSpec((tm, D), lambda i, j: (i, 0)), _const((8, D))],
        out_shape=[jax.ShapeDtypeStruct((S, D), BF16), jax.ShapeDtypeStruct((S, wmi.shape[0] * tf), BF16),
                   jax.ShapeDtypeStruct((S, D), F32), jax.ShapeDtypeStruct((8, D), F32)],
        scratch_shapes=[pltpu.VMEM((tm, D), F32)],
        compiler_params=_params(2), name="mlp_fwd",
    )(x1, modv, n2g, wmi, wmo, fg, target)


def _mlp_bwd(dx2, x1, hp, modv, n2g, wmi, wmo):
    S = x1.shape[0]
    tm = min(MLP_ROWS, S)
    tf = wmi.shape[2]
    nj = wmi.shape[0] // MLP_SHARDS

    def body(dx2_ref, x1_ref, hp_ref, mod_ref, g_ref, wi_ref, wo_ref, dhp_ref, dx1_ref, st_ref, dmo_ref, acc_ref):
        i, j = pl.program_id(0), pl.program_id(1)

        @pl.when(j == 0)
        def _():
            dmo_ref[...] = (mod_ref[5:6, :] * dx2_ref[...]).astype(BF16)
            acc_ref[...] = jnp.zeros_like(acc_ref)

        @pl.when((i == 0) & (j == 0))
        def _():
            st_ref[...] = jnp.zeros_like(st_ref)

        out = None
        dhids = [lax.dot_general(dmo_ref[...], wo_ref[0], NT, preferred_element_type=F32)]
        for u in range(MLP_SHARDS):
            sl = slice(tf * u, tf * u + tf)
            if u + 1 < MLP_SHARDS:
                dhids.append(lax.dot_general(dmo_ref[...], wo_ref[u + 1], NT, preferred_element_type=F32))
            dhid = dhids[u]
            dhp = (dhid * (2.0 * jnp.maximum(hp_ref[:, sl].astype(F32), 0.0))).astype(BF16)
            dhp_ref[:, sl] = dhp
            part = lax.dot_general(dhp, wi_ref[u], NT, preferred_element_type=F32)
            out = part if out is None else out + part
        acc_ref[...] += out

        @pl.when(j == nj - 1)
        def _():
            dh2 = acc_ref[...]
            xt = x1_ref[...]
            r = lax.rsqrt(jnp.mean(xt * xt, axis=-1, keepdims=True) + EPS)
            xn = xt * r
            st_ref[0:1, :] += jnp.sum(dh2, axis=0, keepdims=True)
            st_ref[1:2, :] += jnp.sum(dh2 * xn, axis=0, keepdims=True)
            dxn = dh2 * (g_ref[...] * (1.0 + mod_ref[4:5, :]))
            dx1_ref[...] = dx2_ref[...] + r * (dxn - xn * jnp.mean(dxn * xn, axis=-1, keepdims=True))

    return pl.pallas_call(
        body, grid=(S // tm, nj),
        in_specs=[pl.BlockSpec((tm, D), lambda i, j: (i, 0)), pl.BlockSpec((tm, D), lambda i, j: (i, 0)),
                  pl.BlockSpec((tm, MLP_SHARDS * tf), lambda i, j: (i, j)), _const((6, D)), _const((1, D)),
                  pl.BlockSpec((MLP_SHARDS, D, tf), lambda i, j: (j, 0, 0)), pl.BlockSpec((MLP_SHARDS, tf, D), lambda i, j: (j, 0, 0))],
        out_specs=[pl.BlockSpec((tm, MLP_SHARDS * tf), lambda i, j: (i, j)), pl.BlockSpec((tm, D), lambda i, j: (i, 0)), _const((8, D))],
        out_shape=[jax.ShapeDtypeStruct((S, wmi.shape[0] * tf), BF16), jax.ShapeDtypeStruct((S, D), F32),
                   jax.ShapeDtypeStruct((8, D), F32)],
        scratch_shapes=[pltpu.VMEM((tm, D), BF16), pltpu.VMEM((tm, D), F32)],
        compiler_params=_params(2), name="mlp_bwd",
    )(dx2, x1, hp, modv, n2g, wmi, wmo)


def _tn_matmul(a, b, tk, tn, name, relu_sq=False, dev_major=False, rows=1024):
    S, K = a.shape
    N = b.shape[1]
    ts = min(rows, S)
    ns = S // ts

    def body(a_ref, b_ref, o_ref):
        @pl.when(pl.program_id(2) == 0)
        def _():
            o_ref[...] = jnp.zeros_like(o_ref)

        at = a_ref[...]
        if relu_sq:
            at = jnp.square(jnp.maximum(at.astype(F32), 0.0)).astype(BF16)
        o_ref[...] += lax.dot_general(at, b_ref[...].astype(BF16), TN, preferred_element_type=F32)

    if dev_major:
        out_spec = pl.BlockSpec((None, tk, tn), lambda k, n, s: (n, k, 0))
        out_shape = jax.ShapeDtypeStruct((N // tn, K, tn), F32)
    else:
        out_spec = pl.BlockSpec((tk, tn), lambda k, n, s: (k, n))
        out_shape = jax.ShapeDtypeStruct((K, N), F32)
    return pl.pallas_call(
        body, grid=(K // tk, N // tn, ns),
        in_specs=[pl.BlockSpec((ts, tk), lambda k, n, s: (s, k)), pl.BlockSpec((ts, tn), lambda k, n, s: (s, n))],
        out_specs=out_spec, out_shape=out_shape,
        compiler_params=_params(3), name=name,
    )(a, b)


def _scale_gate(m, w, g, row, name):
    K = m.shape[0]
    tk = min(512, K)

    def body(m_ref, w_ref, mod_ref, dw_ref, dg_ref):
        @pl.when(pl.program_id(0) == 0)
        def _():
            dg_ref[...] = jnp.zeros_like(dg_ref)

        mt = m_ref[...]
        dw_ref[...] = mt * mod_ref[row:row + 1, :]
        dg_ref[0:1, :] += jnp.sum(mt * w_ref[...].astype(F32), axis=0, keepdims=True)

    return pl.pallas_call(
        body, grid=(K // tk,),
        in_specs=[_rows(tk, D), _rows(tk, D), _const((6, D))],
        out_specs=[_rows(tk, D), _const((8, D))],
        out_shape=[jax.ShapeDtypeStruct((K, D), F32), jax.ShapeDtypeStruct((8, D), F32)],
        compiler_params=_params(1), name=name,
    )(m, w, g)


def _merge_bwd(dx1, modv, ga, gb, ua, ub, ya, yb, wb, wout):
    S = dx1.shape[0]
    tm = min(512, S)

    def body(dx1_ref, mod_ref, ga_ref, gb_ref, ua_ref, ub_ref, ya_ref, yb_ref, wb_ref, wo_ref,
             dua_ref, dub_ref, dga_ref, dgb_ref, dya_ref, dyb_ref, dla_ref, dlb_ref, dyat_ref):
        dao = (mod_ref[2:3, :] * dx1_ref[...]).astype(BF16)
        dm = lax.dot_general(dao, wo_ref[...], NT, preferred_element_type=F32)
        r = lax.broadcasted_iota(jnp.int32, (BW, NH), 0) // HD
        c = lax.broadcasted_iota(jnp.int32, (BW, NH), 1)
        head_of = (r == c).astype(BF16)
        for br, (g_ref, u_ref, y_ref, du_ref, dg_ref, dy_ref, dl_ref) in enumerate((
                (ga_ref, ua_ref, ya_ref, dua_ref, dga_ref, dya_ref, dla_ref),
                (gb_ref, ub_ref, yb_ref, dub_ref, dgb_ref, dyb_ref, dlb_ref))):
            sg = jax.nn.sigmoid(g_ref[...].astype(F32))
            du = (dm * sg).astype(BF16)
            du_ref[...] = du
            dg_ref[...] = (dm * u_ref[...].astype(F32) * sg * (1.0 - sg)).astype(BF16)
            dy = lax.dot_general(du, wb_ref[br], NT, preferred_element_type=F32)
            dyb16 = dy.astype(BF16)
            dy_ref[...] = dyb16
            if br == 0:
                dyat_ref[...] = dy.T.astype(BF16)
            prod = dyb16.astype(F32) * y_ref[...].astype(F32)
            hi = prod.astype(BF16)
            lo = (prod - hi.astype(F32)).astype(BF16)
            dl_ref[...] = (jnp.dot(hi, head_of, preferred_element_type=F32) + jnp.dot(lo, head_of, preferred_element_type=F32))

    return pl.pallas_call(
        body, grid=(S // tm,),
        in_specs=[_rows(tm, D), _const((6, D)), _rows(tm, D), _rows(tm, D), _rows(tm, D), _rows(tm, D),
                  _rows(tm, BW), _rows(tm, BW), _const((2, BW, D)), _const((D, D))],
        out_specs=[_rows(tm, D)] * 4 + [_rows(tm, BW)] * 2 + [_rows(tm, NH)] * 2 + [pl.BlockSpec((BW, tm), lambda i: (0, i))],
        out_shape=[jax.ShapeDtypeStruct((S, D), BF16)] * 4 + [jax.ShapeDtypeStruct((S, BW), BF16)] * 2
        + [jax.ShapeDtypeStruct((S, NH), F32)] * 2 + [jax.ShapeDtypeStruct((BW, S), BF16)],
        compiler_params=_params(1), name="merge_bwd",
    )(dx1, modv, ga, gb, ua, ub, ya, yb, wb, wout)


def _attn_a_bwd(q, qt, kt3, vt3, do, dot_, lse, delta, chunks):
    S = q.shape[0]
    tq = min(1024, S)
    nk, _, tk = kt3.shape
    nq = S // tq
    nc = len(chunks)

    def body(q_ref, qt_ref, do_ref, dot_ref, lse_ref, dl_ref, kt_ref, vt_ref, *rest):
        g_hbm, (dq_ref, dk_hbm, dv_hbm), recv_hbm = rest[:nc], rest[nc:nc + 3], rest[nc + 3:2 * nc + 3]
        dk_sc, dv_sc, sem, send_sems, recv_sems = rest[2 * nc + 3:]
        i = pl.program_id(0)
        start, finish = _exchange_scatter(g_hbm, recv_hbm, send_sems, recv_sems)

        @pl.when(i == 0)
        def _():
            start()
            dk_sc[...] = jnp.zeros_like(dk_sc)
            dv_sc[...] = jnp.zeros_like(dv_sc)

        for kv in range(2):
            qg = q_ref[:, 256 * kv:256 * kv + 256]
            dog = do_ref[:, 256 * kv:256 * kv + 256]
            heads = []
            for t in range(GRP):
                h = GRP * kv + t
                heads.append((qg[:, HD * t:HD * t + HD], dog[:, HD * t:HD * t + HD],
                              qt_ref[HD * h:HD * h + HD, :], dot_ref[HD * h:HD * h + HD, :],
                              lse_ref[:, h:h + 1], dl_ref[:, h:h + 1]))

            q2 = [jnp.concatenate([heads[2 * u][0], heads[2 * u + 1][0]], axis=0) for u in range(GRP // 2)]
            do2 = [jnp.concatenate([heads[2 * u][1], heads[2 * u + 1][1]], axis=0) for u in range(GRP // 2)]

            def step(j, carry, kv=kv, heads=heads, q2=q2, do2=do2):
                kjt = kt_ref[j, HD * kv:HD * kv + HD, :]
                vjt = vt_ref[j, HD * kv:HD * kv + HD, :]
                dkt = jnp.zeros((HD, tk), F32)
                dvt = jnp.zeros((HD, tk), F32)
                new = []

                def logits(u):
                    return (jnp.dot(q2[u], kjt, preferred_element_type=F32), jnp.dot(do2[u], vjt, preferred_element_type=F32))

                sd = [logits(0)]
                for t, (qh, doh, qth, doth, lse_h, dl_h) in enumerate(heads):
                    if t == 0:
                        sd.append(logits(1))
                    rows = slice(tq * (t % 2), tq * (t % 2) + tq)
                    s, dp = sd[t // 2][0][rows, :], sd[t // 2][1][rows, :]
                    pm = jnp.exp2(s - lse_h)
                    ds = (pm * (dp - dl_h)).astype(BF16)
                    dvt = dvt + jnp.dot(doth, pm.astype(BF16), preferred_element_type=F32)
                    dkt = dkt + jnp.dot(qth, ds, preferred_element_type=F32)
                    new.append(carry[t] + lax.dot_general(kjt, ds, NT, preferred_element_type=F32))
                dk_sc[j, HD * kv:HD * kv + HD, :] += dkt
                dv_sc[j, HD * kv:HD * kv + HD, :] += dvt
                return tuple(new)

            res = lax.fori_loop(0, nk, step, (jnp.zeros((HD, tq), F32),) * GRP)
            for t in range(GRP):
                dq_ref[HD * (GRP * kv + t):HD * (GRP * kv + t) + HD, :] = res[t]

        @pl.when(i == nq - 1)
        def _():
            c1 = pltpu.make_async_copy(dk_sc, dk_hbm, sem.at[0])
            c2 = pltpu.make_async_copy(dv_sc, dv_hbm, sem.at[1])
            c1.start()
            c2.start()
            c1.wait()
            c2.wait()
            finish()

    any_spec = pl.BlockSpec(memory_space=pl.ANY)
    cols = pl.BlockSpec((BW, tq), lambda i: (0, i))
    return pl.pallas_call(
        body, grid=(nq,),
        in_specs=[_rows(tq, BW), cols, _rows(tq, BW), cols, _rows(tq, NH), _rows(tq, NH), _const((nk, 128, tk)),
                  _const((nk, 128, tk))] + [any_spec] * nc,
        out_specs=[cols, any_spec, any_spec] + [any_spec] * nc,
        out_shape=[jax.ShapeDtypeStruct((BW, S), F32), jax.ShapeDtypeStruct((nk, 128, tk), F32),
                   jax.ShapeDtypeStruct((nk, 128, tk), F32)]
        + [jax.ShapeDtypeStruct((N_DEV - 1,) + c.shape[1:], c.dtype) for c in chunks],
        scratch_shapes=[pltpu.VMEM((nk, 128, tk), F32), pltpu.VMEM((nk, 128, tk), F32), pltpu.SemaphoreType.DMA((2,))]
        + _exchange_sems(nc),
        compiler_params=_params(1), name="attn_a_bwd",
    )(q, qt, do, dot_, lse, delta, kt3, vt3, *chunks)


def _attn_b_bwd(q, kp, vp, sink2, do, lse, delta):
    S = q.shape[0]
    tq = min(TQ_B, S)
    W = tq + 2 * WIN
    nq = S // tq
    nc = (S + 2 * WIN) // tq

    def body(q_ref, k_ref, v_ref, sink_ref, do_ref, lse_ref, dl_ref, dq_ref, dk_hbm, dv_hbm, ds_ref, dk_sc, dv_sc, sem):
        i = pl.program_id(0)

        @pl.when(i == 0)
        def _():
            dk_sc[...] = jnp.zeros_like(dk_sc)
            dv_sc[...] = jnp.zeros_like(dv_sc)
            ds_ref[...] = jnp.zeros_like(ds_ref)

        off = pl.multiple_of(i * tq, tq)
        valid = _window_mask(i, tq, S)
        kw = k_ref[pl.ds(off, W), :]
        vw = v_ref[pl.ds(off, W), :]
        lse_i = lse_ref[...]
        dl_i = dl_ref[...]
        qa = q_ref[...]
        doa = do_ref[...]
        qt = qa.astype(F32).T.astype(BF16)
        dot_ = doa.astype(F32).T.astype(BF16)
        khs = [kw[:, HD * kv:HD * kv + HD] for kv in range(2)]
        vhs = [vw[:, HD * kv:HD * kv + HD] for kv in range(2)]

        def logits(h):
            return (lax.dot_general(qa[:, HD * h:HD * h + HD], khs[h // GRP], NT, preferred_element_type=F32),
                    lax.dot_general(doa[:, HD * h:HD * h + HD], vhs[h // GRP], NT, preferred_element_type=F32))

        sd = [logits(0)]
        dqs = []
        dkt = [jnp.zeros((HD, W), F32), jnp.zeros((HD, W), F32)]
        dvt = [jnp.zeros((HD, W), F32), jnp.zeros((HD, W), F32)]
        for h in range(NH):
            kv = h // GRP
            if h + 1 < NH:
                sd.append(logits(h + 1))
            s, dp = sd[h]
            pm = jnp.exp2(jnp.where(valid, s, NEG) - lse_i[:, h:h + 1])
            ds = (pm * (dp - dl_i[:, h:h + 1])).astype(BF16)
            dvt[kv] = dvt[kv] + jnp.dot(dot_[HD * h:HD * h + HD, :], pm.astype(BF16), preferred_element_type=F32)
            dkt[kv] = dkt[kv] + jnp.dot(qt[HD * h:HD * h + HD, :], ds, preferred_element_type=F32)
            dqs.append(jnp.dot(ds, khs[kv], preferred_element_type=F32))
        for p in range(4):
            dq_ref[:, 128 * p:128 * p + 128] = jnp.concatenate(dqs[2 * p:2 * p + 2], axis=1)
        for half in range(W // tq):
            dk_sc[i + half] += jnp.concatenate([d[:, tq * half:tq * half + tq] for d in dkt], axis=0)
            dv_sc[i + half] += jnp.concatenate([d[:, tq * half:tq * half + tq] for d in dvt], axis=0)
        psd = jnp.exp2(sink_ref[...] - lse_i) * dl_i
        r = lax.broadcasted_iota(jnp.int32, (NH, 128), 0)
        c = lax.broadcasted_iota(jnp.int32, (NH, 128), 1)
        row = jnp.dot(jnp.sum(psd, axis=0, keepdims=True), (r == c).astype(F32),
                      preferred_element_type=F32, precision=lax.Precision.HIGHEST)
        ds_ref[...] -= jnp.broadcast_to(row, (8, 128))

        @pl.when(i == nq - 1)
        def _():
            c1 = pltpu.make_async_copy(dk_sc, dk_hbm, sem.at[0])
            c2 = pltpu.make_async_copy(dv_sc, dv_hbm, sem.at[1])
            c1.start()
            c2.start()
            c1.wait()
            c2.wait()

    any_spec = pl.BlockSpec(memory_space=pl.ANY)
    return pl.pallas_call(
        body, grid=(nq,),
        in_specs=[_rows(tq, BW), _const((S + 2 * WIN, 128)), _const((S + 2 * WIN, 128)), _const((1, NH)),
                  _rows(tq, BW), _rows(tq, NH), _rows(tq, NH)],
        out_specs=[_rows(tq, BW), any_spec, any_spec, _const((8, 128))],
        out_shape=[jax.ShapeDtypeStruct((S, BW), F32), jax.ShapeDtypeStruct((nc, 128, tq), F32),
                   jax.ShapeDtypeStruct((nc, 128, tq), F32), jax.ShapeDtypeStruct((8, 128), F32)],
        scratch_shapes=[pltpu.VMEM((nc, 128, tq), F32), pltpu.VMEM((nc, 128, tq), F32), pltpu.SemaphoreType.DMA((2,))],
        compiler_params=_params(1), name="attn_b_bwd",
    )(q, kp, vp, sink2, do, lse, delta)


def _qk_bwd(dqa_t, dka_t3, dva_t3, dqb, dkb, dvb, qar, kar, qg2, kg2, tab_a, tab_b, dga, dgb):
    S = dqb.shape[0]
    tm = min(256, S)
    per = dka_t3.shape[2] // tm

    def body(dqa_ref, dka_ref, dva_ref, dqb_ref, dkb_ref, dvb_ref, qar_ref, kar_ref, qg_ref, kg_ref, ta_ref, tb_ref,
             dga_ref, dgb_ref, dp_ref, st_ref):
        @pl.when(pl.program_id(0) == 0)
        def _():
            st_ref[...] = jnp.zeros_like(st_ref)

        seg = _seg_matrix(128, HD)

        def norm_bwd(dz_rot, raw, g):
            dzn = _rope_t(dz_rot, ta_ref, 16)
            raw = raw.astype(F32)
            rr = lax.rsqrt(_seg_sum(raw * raw, seg) * (1.0 / HD) + EPS)
            zhat = raw * rr
            dzh = dzn * g
            draw = rr * (dzh - zhat * (_seg_sum(dzh * zhat, seg) * (1.0 / HD)))
            return draw, jnp.sum(dzn * zhat, axis=0, keepdims=True)

        gq = jnp.zeros((1, 128), F32)
        for p in range(4):
            sl = slice(128 * p, 128 * p + 128)
            draw, gsum = norm_bwd(dqa_ref[sl, :].T * 0.125, qar_ref[:, sl], qg_ref[...])
            gq = gq + gsum
            dp_ref[:, sl] = draw.astype(BF16)
            dp_ref[:, 768 + 128 * p:768 + 128 * p + 128] = _rope_t(dqb_ref[:, sl] * 0.125, tb_ref, 32).astype(BF16)
        draw, gk = norm_bwd(dka_ref[...].T * LN2, kar_ref[...], kg_ref[...])
        dp_ref[:, 512:640] = draw.astype(BF16)
        dp_ref[:, 640:768] = dva_ref[...].T.astype(BF16)
        dp_ref[:, 1280:1408] = _rope_t(dkb_ref[...] * LN2, tb_ref, 32).astype(BF16)
        dp_ref[:, 1408:1536] = dvb_ref[...].astype(BF16)
        dp_ref[:, 1536:2560] = dga_ref[...]
        dp_ref[:, 2560:3584] = dgb_ref[...]
        st_ref[0:1, :] += gq
        st_ref[1:2, :] += gk

    tab = pl.BlockSpec((3, tm, 128), lambda i: (0, i, 0))
    chunk_t = pl.BlockSpec((None, 128, tm), lambda i: (i // per, 0, i % per))
    return pl.pallas_call(
        body, grid=(S // tm,),
        in_specs=[pl.BlockSpec((BW, tm), lambda i: (0, i)), chunk_t, chunk_t, _rows(tm, BW), _rows(tm, 128), _rows(tm, 128),
                  _rows(tm, BW), _rows(tm, 128), _const((1, 128)), _const((1, 128)), tab, tab, _rows(tm, D), _rows(tm, D)],
        out_specs=[_rows(tm, INW), _const((8, 128))],
        out_shape=[jax.ShapeDtypeStruct((S, INW), BF16), jax.ShapeDtypeStruct((8, 128), F32)],
        compiler_params=_params(1), name="qk_bwd",
    )(dqa_t, dka_t3, dva_t3, dqb, dkb, dvb, qar, kar, qg2, kg2, tab_a, tab_b, dga, dgb)


def _in_bwd(dproj, win, x, dx1, modv, n1g, chunks):
    S = x.shape[0]
    tm = min(512, S)
    n = S // tm
    ts = min(256, tm)

    def body(dp_ref, w_ref, x_ref, dx1_ref, mod_ref, g_ref, c_hbm, gx_ref, st_ref, recv_hbm, send_sems, recv_sems):
        start, finish = _exchange_scatter([c_hbm], [recv_hbm], send_sems, recv_sems)

        @pl.when(pl.program_id(0) == 0)
        def _():
            start()
            st_ref[...] = jnp.zeros_like(st_ref)

        subs = [slice(ts * u, ts * u + ts) for u in range(tm // ts)]
        dhs = [lax.dot_general(dp_ref[rows, :], w_ref[...], NT, preferred_element_type=F32) for rows in subs]
        for rows, dh in zip(subs, dhs):
            xt = x_ref[rows, :]
            r = lax.rsqrt(jnp.mean(xt * xt, axis=-1, keepdims=True) + EPS)
            xn = xt * r
            st_ref[0:1, :] += jnp.sum(dh, axis=0, keepdims=True)
            st_ref[1:2, :] += jnp.sum(dh * xn, axis=0, keepdims=True)
            dxn = dh * (g_ref[...] * (1.0 + mod_ref[1:2, :]))
            gx_ref[rows, :] = dx1_ref[rows, :] + r * (dxn - xn * jnp.mean(dxn * xn, axis=-1, keepdims=True))
        pl.when(pl.program_id(0) == n - 1)(finish)

    any_spec = pl.BlockSpec(memory_space=pl.ANY)
    return pl.pallas_call(
        body, grid=(n,),
        in_specs=[_rows(tm, INW), _const((D, INW)), _rows(tm, D), _rows(tm, D), _const((6, D)), _const((1, D)), any_spec],
        out_specs=[_rows(tm, D), _const((8, D)), any_spec],
        out_shape=[jax.ShapeDtypeStruct((S, D), F32), jax.ShapeDtypeStruct((8, D), F32),
                   jax.ShapeDtypeStruct((N_DEV - 1,) + chunks.shape[1:], chunks.dtype)],
        scratch_shapes=_exchange_sems(1),
        compiler_params=_params(1), name="in_bwd",
    )(dproj, win, x, dx1, modv, n1g, chunks)


def _pack_small(st1, st2, stf, dg1, dg2, stqk, dsink, modv, n1g, n2g):
    def body(st1_ref, st2_ref, stf_ref, dg1_ref, dg2_ref, qk_ref, ds_ref, mod_ref, g1_ref, g2_ref, o_ref):
        a1, b1 = st1_ref[0:1, :], st1_ref[1:2, :]
        a2, b2 = st2_ref[0:1, :], st2_ref[1:2, :]
        r = lax.broadcasted_iota(jnp.int32, (128, D), 0)
        c = lax.broadcasted_iota(jnp.int32, (128, D), 1)
        fold_q = (c == r % HD).astype(F32)
        fold_k = (c == HD + r % HD).astype(F32)
        keep = (c == r).astype(F32)

        def place(v, sel):
            return jnp.dot(v, sel, preferred_element_type=F32, precision=lax.Precision.HIGHEST)

        loss = jnp.sum(stf_ref[1:2, :], axis=1, keepdims=True)
        lane = lax.broadcasted_iota(jnp.int32, (1, D), 1)
        rows = [a1, g1_ref[...] * b1, dg1_ref[0:1, :], a2, g2_ref[...] * b2, dg2_ref[0:1, :],
                (1.0 + mod_ref[1:2, :]) * b1, (1.0 + mod_ref[4:5, :]) * b2, stf_ref[0:1, :],
                place(qk_ref[0:1, :], fold_q) + place(qk_ref[1:2, :], fold_k),
                place(ds_ref[0:1, :], keep),
                jnp.where(lane == 0, loss, 0.0)]
        rows += [jnp.zeros((1, D), F32)] * (SMALL_ROWS - len(rows))
        for n, v in enumerate(rows):
            o_ref[n:n + 1, :] = v

    return pl.pallas_call(
        body, out_shape=jax.ShapeDtypeStruct((SMALL_ROWS, D), F32),
        compiler_params=pltpu.CompilerParams(vmem_limit_bytes=V7X_VMEM_LIMIT), name="pack_small",
    )(st1, st2, stf, dg1, dg2, stqk, dsink, modv, n1g, n2g)


def _wada_grad(silu_all, dmod_cols):
    def body(a_ref, b_ref, o_ref):
        o_ref[...] = lax.dot_general(a_ref[...], b_ref[...], TN, preferred_element_type=F32, precision=lax.Precision.HIGHEST)

    return pl.pallas_call(
        body, out_shape=jax.ShapeDtypeStruct((D, dmod_cols.shape[1]), F32),
        compiler_params=pltpu.CompilerParams(vmem_limit_bytes=V7X_VMEM_LIMIT), name="wada_grad",
    )(silu_all, dmod_cols)


def _adamw_sum(parts, w, m, v, name):
    R, C = w.shape
    tr = R if R <= 64 else next(t for t in (256, 128, 64, 32, 16, 8) if R % t == 0)
    n = len(parts)
    dyn = [idx for _, idx in parts if idx is not None and not isinstance(idx, int)]
    b1c = 1.0 - ADAM_B1 ** ADAM_STEP
    b2c = 1.0 - ADAM_B2 ** ADAM_STEP

    def body(*refs):
        refs = refs[len(dyn):]
        g = refs[0][...].astype(F32)
        for k in range(1, n):
            g = g + refs[k][...].astype(F32)
        w_ref, m_ref, v_ref, g_out, d_out, m_out, v_out = refs[n:]
        mn = ADAM_B1 * m_ref[...] + (1.0 - ADAM_B1) * g
        vn = ADAM_B2 * v_ref[...] + (1.0 - ADAM_B2) * jnp.square(g)
        g_out[...] = g
        m_out[...] = mn
        v_out[...] = vn
        d_out[...] = -ADAM_LR * ((mn / b1c) / (jnp.sqrt(vn / b2c) + ADAM_EPS) + ADAM_WD * w_ref[...])

    in_specs = []
    nd = 0
    for a, idx in parts:
        if idx is None:
            in_specs.append(pl.BlockSpec((tr, C), lambda i, *s: (i, 0)))
        elif isinstance(idx, int):
            in_specs.append(pl.BlockSpec((None, tr, C), lambda i, *s, idx=idx: (idx, i, 0)))
        else:
            in_specs.append(pl.BlockSpec((None, tr, C), lambda i, *s, nd=nd: (s[nd][0], i, 0)))
            nd += 1
    blk = pl.BlockSpec((tr, C), lambda i, *s: (i, 0))
    grid_spec = pltpu.PrefetchScalarGridSpec(
        num_scalar_prefetch=len(dyn), grid=(R // tr,), in_specs=in_specs + [blk] * 3, out_specs=[blk] * 4)
    return pl.pallas_call(
        body, grid_spec=grid_spec, out_shape=[jax.ShapeDtypeStruct((R, C), F32)] * 4,
        compiler_params=_params(1), name=name,
    )(*dyn, *[a for a, _ in parts], w, m, v)


def _me():
    return lax.axis_index("x"), lax.axis_index("y"), lax.axis_index("c")


def _peer(k):
    x, y, c = _me()
    return (x ^ ((k >> 2) & 1), y ^ ((k >> 1) & 1), c ^ (k & 1))


def _ada_exchange(c_row, w_ada, b_rows):
    NW = w_ada.shape[1]

    def body(c_ref, w_ref, b_ref, sall_ref, mod_ref, src_ref, mp_ref, send1, recv1, send2, recv2):
        x, y, c = _me()
        me = 4 * x + 2 * y + c
        cv = c_ref[...]
        src_ref[...] = jnp.broadcast_to(cv * jax.nn.sigmoid(cv), (8, D))
        mine = pl.ds(pl.multiple_of(me * 8, 8), 8)
        sall_ref[mine, :] = src_ref[...]
        sends = [pltpu.make_async_remote_copy(src_ref, sall_ref.at[mine, :], send1.at[k - 1], recv1.at[k - 1],
                                              device_id=_peer(k), device_id_type=MESH) for k in range(1, N_DEV)]
        for cp in sends:
            cp.start()
        for k in range(1, N_DEV):
            theirs = pl.ds(pl.multiple_of((me ^ k) * 8, 8), 8)
            pltpu.make_async_remote_copy(src_ref, sall_ref.at[theirs, :], send1.at[k - 1], recv1.at[k - 1],
                                         device_id=_peer(k), device_id_type=MESH).wait_recv()
        for cp in sends:
            cp.wait_send()
        mp_ref[...] = jnp.dot(sall_ref[...], w_ref[...], preferred_element_type=F32, precision=lax.Precision.HIGHEST)
        mod_ref[mine, :] = mp_ref[mine, :] + b_ref[mine, :]
        sends = []
        for k in range(1, N_DEV):
            theirs = pl.ds(pl.multiple_of((me ^ k) * 8, 8), 8)
            sends.append(pltpu.make_async_remote_copy(mp_ref.at[theirs, :], mod_ref.at[mine, :], send2.at[k - 1], recv2.at[k - 1],
                                                      device_id=_peer(k), device_id_type=MESH))
        for cp in sends:
            cp.start()
        for k in range(1, N_DEV):
            theirs = pl.ds(pl.multiple_of((me ^ k) * 8, 8), 8)
            pltpu.make_async_remote_copy(mp_ref.at[mine, :], mod_ref.at[theirs, :], send2.at[k - 1], recv2.at[k - 1],
                                         device_id=_peer(k), device_id_type=MESH).wait_recv()
            mod_ref[theirs, :] = mod_ref[theirs, :] + b_ref[theirs, :]
        for cp in sends:
            cp.wait_send()

    vm = pl.BlockSpec(memory_space=pltpu.VMEM)
    return pl.pallas_call(
        body, in_specs=[vm, vm, vm], out_specs=[vm, vm],
        out_shape=[jax.ShapeDtypeStruct((8 * N_DEV, D), F32), jax.ShapeDtypeStruct((8 * N_DEV, NW), F32)],
        scratch_shapes=[pltpu.VMEM((8, D), F32), pltpu.VMEM((8 * N_DEV, NW), F32)]
        + [pltpu.SemaphoreType.DMA((N_DEV - 1,))] * 4,
        compiler_params=pltpu.CompilerParams(vmem_limit_bytes=V7X_VMEM_LIMIT), name="ada_exchange",
    )(c_row, w_ada, b_rows)


def _weight_gather(shard):
    def body(x_ref, out_ref, send_sems, recv_sems, local_sem):
        x, y, c = _me()
        me, sibling = (x, y, c), (x, y, 1 - c)
        chips = [(1 - x, y), (x, 1 - y), (1 - x, 1 - y)]

        def slot(px, py, pc):
            return out_ref.at[4 * px + 2 * py + pc]

        def copy(k, block, to, src=None):
            return pltpu.make_async_remote_copy(
                src_ref=slot(*block) if src is None else src, dst_ref=slot(*block),
                send_sem=send_sems.at[k], recv_sem=recv_sems.at[k], device_id=to, device_id_type=MESH)

        mine = pltpu.make_async_copy(x_ref, slot(*me), local_sem)
        mine.start()
        first = [copy(0, me, sibling, src=x_ref)]
        first += [copy(1 + j, me, (*chip, c), src=x_ref) for j, chip in enumerate(chips)]
        for cp in first:
            cp.start()
        passed = [copy(4 + j, (*chip, c), sibling) for j, chip in enumerate(chips)]
        for j, chip in enumerate(chips):
            copy(1 + j, (*chip, c), me).wait_recv()
            passed[j].start()
        copy(0, sibling, me).wait_recv()
        for j, chip in enumerate(chips):
            copy(4 + j, (*chip, 1 - c), me).wait_recv()
        for cp in first + passed:
            cp.wait_send()
        mine.wait()

    any_spec = pl.BlockSpec(memory_space=pl.ANY)
    return pl.pallas_call(
        body, in_specs=[any_spec], out_specs=any_spec,
        out_shape=jax.ShapeDtypeStruct((N_DEV,) + shard.shape, shard.dtype),
        scratch_shapes=[pltpu.SemaphoreType.DMA((7,)), pltpu.SemaphoreType.DMA((7,)), pltpu.SemaphoreType.DMA],
        name="weight_gather",
    )(shard)


def _small_gather(block):
    def body(b_ref, out_ref, send_sems, recv_sems):
        x, y, c = _me()
        me = 4 * x + 2 * y + c
        out_ref[me] = b_ref[...]
        sends = [pltpu.make_async_remote_copy(b_ref, out_ref.at[me], send_sems.at[k - 1], recv_sems.at[k - 1],
                                              device_id=_peer(k), device_id_type=MESH) for k in range(1, N_DEV)]
        for cp in sends:
            cp.start()
        for k in range(1, N_DEV):
            pltpu.make_async_remote_copy(b_ref, out_ref.at[me ^ k], send_sems.at[k - 1], recv_sems.at[k - 1],
                                         device_id=_peer(k), device_id_type=MESH).wait_recv()
        for cp in sends:
            cp.wait_send()

    vm = pl.BlockSpec(memory_space=pltpu.VMEM)
    return pl.pallas_call(
        body, in_specs=[vm], out_specs=vm,
        out_shape=jax.ShapeDtypeStruct((N_DEV,) + block.shape, block.dtype),
        scratch_shapes=[pltpu.SemaphoreType.DMA((N_DEV - 1,)), pltpu.SemaphoreType.DMA((N_DEV - 1,))],
        name="small_gather",
    )(block)


def _pack_small_params(b_ada, n1, n2, fg, qn, kn, sink):
    qk = jnp.concatenate([qn.reshape(1, HD), kn.reshape(1, HD), jnp.zeros((1, D - 2 * HD), F32)], axis=1)
    sk = jnp.concatenate([sink.reshape(1, NH), jnp.zeros((1, D - NH), F32)], axis=1)
    return jnp.concatenate([b_ada.reshape(6, D), n1.reshape(1, D), n2.reshape(1, D), fg.reshape(1, D), qk, sk,
                            jnp.zeros((SMALL_ROWS - 11, D), F32)], axis=0)


def _unpack_small(p):
    return (p[0:6].reshape(1, 6 * D), p[6].reshape(1, D), p[9, 0:HD].reshape(1, HD), p[9, HD:2 * HD].reshape(1, HD),
            p[10, 0:NH].reshape(1, NH), p[7].reshape(1, D), p[8].reshape(D))


def kernel(x, c, w_ada, b_ada, norm1_g, w_in, q_norm_a, k_norm_a, sink_b, w_branch, w_out, norm2_g, w_mlp_in, w_mlp_out, final_g, loss_target, m_w_ada, m_b_ada, m_norm1_g, m_w_in, m_q_norm_a, m_k_norm_a, m_sink_b, m_w_branch, m_w_out, m_norm2_g, m_w_mlp_in, m_w_mlp_out, m_final_g, v_w_ada, v_b_ada, v_norm1_g, v_w_in, v_q_norm_a, v_k_norm_a, v_sink_b, v_w_branch, v_w_out, v_norm2_g, v_w_mlp_in, v_w_mlp_out, v_final_g):
    S = x.shape[1]
    xs = x.reshape(S, D)
    tgt = loss_target.reshape(S, D)
    ax, ay, ac = lax.axis_index("x"), lax.axis_index("y"), lax.axis_index("c")
    me = 4 * ax + 2 * ay + ac
    me1 = me.reshape(1).astype(jnp.int32)
    NW = w_ada.shape[2]
    NI = w_in.shape[2]

    silu64, mod64 = _ada_exchange(c.reshape(1, D), w_ada.reshape(D, NW),
                                  jnp.repeat(b_ada.reshape(N_DEV, NW), 8, axis=0))
    silu_all = silu64[0::8]
    modv = mod64[0::8].reshape(6, D)

    win = _weight_gather(w_in[0].astype(BF16)).transpose(1, 0, 2).reshape(D, INW)
    rest_shards = tuple(w[0].astype(BF16) for w in (w_branch, w_out, w_mlp_in, w_mlp_out))

    tab_a, tab_b = _rope_tables(S)
    qg2 = jnp.tile(q_norm_a.reshape(1, HD), (1, 2))
    kg2 = jnp.tile(k_norm_a.reshape(1, HD), (1, 2))
    n1g = norm1_g.reshape(1, D)
    n2g = norm2_g.reshape(1, D)
    fg = final_g.reshape(1, D)
    sink2 = sink_b.reshape(1, NH) * LOG2E

    h, qar, kar, qa, ka, va, qb, kb, vb, ga, gb, qa_t, ka_t3, va_t3 = _in_proj(xs, modv, n1g, win, qg2, kg2, tab_a, tab_b)
    ya, lse_at, wb, wout, wmi, wmo = _attn_a_fwd(qa_t, ka, va_t3, rest_shards)
    lse_a = lse_at.T
    wb = wb.transpose(1, 2, 0, 3).reshape(2, BW, D)
    wout = wout.reshape(D, D)
    pad = ((WIN, WIN), (0, 0))
    kbp, vbp = jnp.pad(kb, pad), jnp.pad(vb, pad)
    tb = min(TQ_B, S)
    yb, lse_bt = _attn_b_fwd(qb, kbp, vbp.reshape((S + 2 * WIN) // tb, tb, 128).transpose(0, 2, 1), sink2)
    lse_b = lse_bt.T
    x1, merged, ua, ub = _merge_out(ya, yb, ga, gb, xs, modv, wb, wout)
    h2, hp, dx2, stf = _mlp_fwd(x1, modv, n2g, wmi, wmo, fg, tgt)

    dhp, dx1, st2 = _mlp_bwd(dx2, x1, hp, modv, n2g, wmi, wmo)
    m2 = _tn_matmul(hp, dx2, 1024, D, "dw_mlp_out", relu_sq=True, rows=2048)
    g_wmo, dg2 = _scale_gate(m2, wmo.reshape(FF, D), modv, 5, "gate2_grad")
    g_wmi = _tn_matmul(h2, dhp, D, 512, "dw_mlp_in", dev_major=True, rows=4096)
    dua, dub, dga, dgb, dya, dyb, dl_a, dl_b, dya_t = _merge_bwd(dx1, modv, ga, gb, ua, ub, ya, yb, wb, wout)
    m1 = _tn_matmul(merged, dx1, D, D, "dw_out", rows=2048)
    g_wout, dg1 = _scale_gate(m1, wout, modv, 2, "gate1_grad")
    g_wb0 = _tn_matmul(ya, dua, BW, D, "dw_branch_a")
    g_wb1 = _tn_matmul(yb, dub, BW, D, "dw_branch_b")
    g_wb = jnp.stack([g_wb0, g_wb1]).reshape(2, BW, N_DEV, 128).transpose(2, 0, 1, 3).reshape(N_DEV, 2 * BW, 128)
    g_wout = g_wout.reshape(N_DEV, 128, D)
    g_wmo = g_wmo.reshape(N_DEV, 512, D)
    dqa_t, dka_t3, dva_t3, r_wb, r_wout, r_wmi, r_wmo = _attn_a_bwd(qa, qa_t, ka_t3, va_t3, dya, dya_t, lse_a, dl_a,
                                                                    (g_wb, g_wout, g_wmi, g_wmo))
    dqb, dkb_t, dvb_t, dsink = _attn_b_bwd(qb, kbp, vbp, sink2, dyb, lse_b, dl_b)
    dkb = dkb_t.transpose(0, 2, 1).reshape(S + 2 * WIN, 128)[WIN:WIN + S]
    dvb = dvb_t.transpose(0, 2, 1).reshape(S + 2 * WIN, 128)[WIN:WIN + S]
    dproj, stqk = _qk_bwd(dqa_t, dka_t3, dva_t3, dqb, dkb, dvb, qar, kar, qg2, kg2, tab_a, tab_b, dga, dgb)
    g_win = _tn_matmul(h, dproj, D, 896, "dw_in", rows=2048)
    g_win = g_win.reshape(D, N_DEV, NI).transpose(1, 0, 2)
    grad_x, st1, r_win = _in_bwd(dproj, win, xs, dx1, modv, n1g, g_win.astype(BF16))

    def adam(name, own, recv, w, m, v):
        shape = w.shape
        w2, m2_, v2 = (a.reshape(own.shape[1:]) for a in (w, m, v))
        outs = _adamw_sum([(own, me1)] + [(recv, k) for k in range(N_DEV - 1)], w2, m2_, v2, name)
        return [a.reshape(shape) for a in outs]

    o_win = adam("adamw_w_in", g_win, r_win, w_in, m_w_in, v_w_in)
    o_wb = adam("adamw_w_branch", g_wb, r_wb, w_branch, m_w_branch, v_w_branch)
    o_wout = adam("adamw_w_out", g_wout, r_wout, w_out, m_w_out, v_w_out)
    o_wmi = adam("adamw_w_mlp_in", g_wmi, r_wmi, w_mlp_in, m_w_mlp_in, v_w_mlp_in)
    o_wmo = adam("adamw_w_mlp_out", g_wmo, r_wmo, w_mlp_out, m_w_mlp_out, v_w_mlp_out)

    small = _pack_small(st1, st2, stf, dg1, dg2, stqk, dsink, modv, n1g, n2g)
    small_all = _small_gather(small)
    sw = _pack_small_params(b_ada, norm1_g, norm2_g, final_g, q_norm_a, k_norm_a, sink_b)
    sm = _pack_small_params(m_b_ada, m_norm1_g, m_norm2_g, m_final_g, m_q_norm_a, m_k_norm_a, m_sink_b)
    sv = _pack_small_params(v_b_ada, v_norm1_g, v_norm2_g, v_final_g, v_q_norm_a, v_k_norm_a, v_sink_b)
    sm_out = _adamw_sum([(small_all, k) for k in range(N_DEV)], sw, sm, sv, "adamw_small")
    loss = sm_out[0][11, 0]
    sm_out = [_unpack_small(a) for a in sm_out]

    dmod_all = small_all[:, 0:6, :].reshape(N_DEV, 6 * D)
    dmod_cols = lax.dynamic_slice_in_dim(dmod_all, me * NW, NW, axis=1)
    g_wada = _wada_grad(silu_all, dmod_cols)
    ada = _adamw_sum([(g_wada, None)], w_ada.reshape(D, NW), m_w_ada.reshape(D, NW), v_w_ada.reshape(D, NW), "adamw_ada")
    ada = [a.reshape(1, D, NW) for a in ada]

    def leaves(k):
        b_, n1_, qn_, kn_, sk_, n2_, fg_ = sm_out[k]
        return [ada[k], b_, n1_, o_win[k], qn_, kn_, sk_, o_wb[k], o_wout[k], n2_, o_wmi[k], o_wmo[k], fg_]

    return (loss, grad_x.reshape(1, S, D), *leaves(0), *leaves(1), *leaves(2), *leaves(3))
```

```python
import jax
import jax.numpy as jnp
from jax import lax
from jax.experimental import pallas as pl
from jax.experimental.pallas import tpu as pltpu

F32, BF16 = jnp.float32, jnp.bfloat16
MESH = pl.DeviceIdType.MESH

D = 1024
HD = 64
NH = 8
GRP = 4
BW = 512
FF = 4096
INW = 3584
GRID_W = 64
WIN = 128
THETA = 10000.0
EPS = 1e-6
NEG = -1e30
N_DEV = 8
LOG2E = 1.4426950408889634
LN2 = 0.6931471805599453
QA_SCALE = 0.125 * LOG2E
SMALL_ROWS = 16
MLP_SHARDS = 4
MLP_ROWS = 512
TK_A = 512
TK_A_FWD = 2048
V7X_VMEM_LIMIT = 56 * 1024 * 1024

ADAM_LR, ADAM_B1, ADAM_B2, ADAM_EPS, ADAM_WD, ADAM_STEP = 0.001, 0.9, 0.999, 1e-08, 0.01, 10

NT = (((1,), (1,)), ((), ()))
TN = (((0,), (0,)), ((), ()))


def _params(n_axes, vmem=V7X_VMEM_LIMIT):
    return pltpu.CompilerParams(dimension_semantics=("arbitrary",) * n_axes, vmem_limit_bytes=vmem)


def _const(shape):
    return pl.BlockSpec(shape, lambda *_: (0,) * len(shape))


def _rows(tm, width):
    return pl.BlockSpec((tm, width), lambda i, *_: (i, 0))


def _seg_matrix(n, seg):
    r = lax.broadcasted_iota(jnp.int32, (n, n), 0) // seg
    c = lax.broadcasted_iota(jnp.int32, (n, n), 1) // seg
    return (r == c).astype(BF16)


def _seg_sum(z, seg_mat):
    hi = z.astype(BF16)
    lo = (z - hi.astype(F32)).astype(BF16)
    return jnp.dot(hi, seg_mat, preferred_element_type=F32) + jnp.dot(lo, seg_mat, preferred_element_type=F32)


def _rope(z, t_ref, sh):
    return z * t_ref[0] + pltpu.roll(z, sh, 1) * t_ref[1] + pltpu.roll(z, 128 - sh, 1) * t_ref[2]


def _rope_t(dz, t_ref, sh):
    return dz * t_ref[0] + pltpu.roll(dz * t_ref[1], 128 - sh, 1) + pltpu.roll(dz * t_ref[2], sh, 1)


def _rope_tables(S):
    t = jnp.arange(S, dtype=jnp.int32)[:, None]
    lane = jnp.arange(128, dtype=jnp.int32)[None, :] % HD

    def build(ang, first):
        cos, sin = jnp.cos(ang), jnp.sin(ang)
        return jnp.stack([cos, jnp.where(first, 0.0, sin), jnp.where(first, -sin, 0.0)]).astype(F32)

    inv_a = (THETA ** (-jnp.arange(0, HD // 2, 2, dtype=F32) / (HD // 2)))[lane % 16]
    pos_a = jnp.where(lane < HD // 2, t // GRID_W, t % GRID_W).astype(F32)
    tab_a = build(pos_a * inv_a, (lane % 32) < 16)
    inv_b = (THETA ** (-jnp.arange(0, HD, 2, dtype=F32) / HD))[lane % 32]
    tab_b = build(t.astype(F32) * inv_b, lane < 32)
    return tab_a, tab_b


def _in_proj(x, modv, n1g, win, qg2, kg2, tab_a, tab_b):
    S = x.shape[0]
    tm = min(512, S)
    tk = min(TK_A, S)
    per = tk // tm
    ts = min(256, tm)

    def body(x_ref, mod_ref, g_ref, w_ref, qg_ref, kg_ref, ta_ref, tb_ref,
             h_ref, qar_ref, kar_ref, qa_ref, ka_ref, va_ref, qb_ref, kb_ref, vb_ref, ga_ref, gb_ref, qat_ref, kat_ref, vat_ref):
        seg = _seg_matrix(128, HD)

        def head_norm(z, g):
            ms = _seg_sum(z * z, seg) * (1.0 / HD)
            return (z * lax.rsqrt(ms + EPS)) * g

        subs = [slice(ts * u, ts * u + ts) for u in range(tm // ts)]
        hbs = []
        for rows in subs:
            xt = x_ref[rows, :]
            r = lax.rsqrt(jnp.mean(xt * xt, axis=-1, keepdims=True) + EPS)
            h = ((xt * r) * g_ref[...]) * (1.0 + mod_ref[1:2, :]) + mod_ref[0:1, :]
            hbs.append(h.astype(BF16))
            h_ref[rows, :] = hbs[-1]
        projs = [jnp.dot(hb, w_ref[...], preferred_element_type=F32) for hb in hbs]
        for rows, proj in zip(subs, projs):
            ta, tb = ta_ref[:, rows, :], tb_ref[:, rows, :]
            for p in range(4):
                z = proj[:, 128 * p:128 * p + 128]
                qar_ref[rows, 128 * p:128 * p + 128] = z.astype(BF16)
                qv = _rope(head_norm(z, qg_ref[...]), ta, 16) * QA_SCALE
                qa_ref[rows, 128 * p:128 * p + 128] = qv.astype(BF16)
                qat_ref[128 * p:128 * p + 128, rows] = qv.T.astype(BF16)
                zb = proj[:, 768 + 128 * p:768 + 128 * p + 128]
                qb_ref[rows, 128 * p:128 * p + 128] = (_rope(zb, tb, 32) * QA_SCALE).astype(BF16)
            z = proj[:, 512:640]
            kar_ref[rows, :] = z.astype(BF16)
            kv_ = _rope(head_norm(z, kg_ref[...]), ta, 16)
            ka_ref[rows, :] = kv_.astype(BF16)
            kat_ref[:, rows] = kv_.T.astype(BF16)
            va_ref[rows, :] = proj[:, 640:768].astype(BF16)
            vat_ref[:, rows] = proj[:, 640:768].T.astype(BF16)
            kb_ref[rows, :] = _rope(proj[:, 1280:1408], tb, 32).astype(BF16)
            vb_ref[rows, :] = proj[:, 1408:1536].astype(BF16)
            ga_ref[rows, :] = proj[:, 1536:2560].astype(BF16)
            gb_ref[rows, :] = proj[:, 2560:3584].astype(BF16)

    tab = pl.BlockSpec((3, tm, 128), lambda i: (0, i, 0))
    shapes = [(D, BF16), (BW, BF16), (128, BF16), (BW, BF16), (128, BF16), (128, BF16),
              (BW, BF16), (128, BF16), (128, BF16), (D, BF16), (D, BF16)]
    return pl.pallas_call(
        body, grid=(S // tm,),
        in_specs=[_rows(tm, D), _const((6, D)), _const((1, D)), _const((D, INW)), _const((1, 128)), _const((1, 128)), tab, tab],
        out_specs=[_rows(tm, w) for w, _ in shapes] + [pl.BlockSpec((BW, tm), lambda i: (0, i))]
        + [pl.BlockSpec((None, 128, tm), lambda i: (i // per, 0, i % per))] * 2,
        out_shape=[jax.ShapeDtypeStruct((S, w), dt) for w, dt in shapes] + [jax.ShapeDtypeStruct((BW, S), BF16)]
        + [jax.ShapeDtypeStruct((S // tk, 128, tk), BF16)] * 2,
        compiler_params=_params(1), name="in_proj",
    )(x, modv, n1g, win, qg2, kg2, tab_a, tab_b)


def _exchange_gather(block_refs, out_refs, send_sems, recv_sems, local_sems):
    x, y, c = _me()
    me = 4 * x + 2 * y + c

    def copies():
        own, out, arrive = [], [], []
        for a, (blk, dst) in enumerate(zip(block_refs, out_refs)):
            own.append(pltpu.make_async_copy(blk, dst.at[me], local_sems.at[a]))
            for k in range(1, N_DEV):
                sems = dict(send_sem=send_sems.at[a, k - 1], recv_sem=recv_sems.at[a, k - 1], device_id=_peer(k), device_id_type=MESH)
                out.append(pltpu.make_async_remote_copy(blk, dst.at[me], **sems))
                arrive.append(pltpu.make_async_remote_copy(blk, dst.at[me ^ k], **sems))
        return own, out, arrive

    def start():
        own, out, _ = copies()
        for cp in own + out:
            cp.start()

    def finish():
        own, out, arrive = copies()
        for cp in arrive:
            cp.wait_recv()
        for cp in out:
            cp.wait_send()
        for cp in own:
            cp.wait()

    return start, finish


def _exchange_scatter(chunk_refs, recv_refs, send_sems, recv_sems):
    x, y, c = _me()
    me = 4 * x + 2 * y + c

    def copies():
        return [pltpu.make_async_remote_copy(src.at[me ^ k], dst.at[k - 1], send_sems.at[a, k - 1], recv_sems.at[a, k - 1],
                                             device_id=_peer(k), device_id_type=MESH)
                for a, (src, dst) in enumerate(zip(chunk_refs, recv_refs)) for k in range(1, N_DEV)]

    def start():
        for cp in copies():
            cp.start()

    def finish():
        cps = copies()
        for cp in cps:
            cp.wait_recv()
        for cp in cps:
            cp.wait_send()

    return start, finish


def _exchange_sems(n):
    return [pltpu.SemaphoreType.DMA((n, N_DEV - 1)), pltpu.SemaphoreType.DMA((n, N_DEV - 1))]


def _attn_a_fwd(qt, k, vt3, shards):
    S = qt.shape[1]
    tq = min(512, S)
    nq = S // tq
    per = max(1, min(TK_A_FWD, S) // vt3.shape[2])
    tk = per * vt3.shape[2]
    nk = S // tk
    ONES = 16
    AHEAD = 2
    ns = len(shards)

    def body(q_ref, k_ref, vt_ref, *rest):
        w_hbm, (o_ref, lse_ref), wall_hbm = rest[:ns], rest[ns:ns + 2], rest[ns + 2:2 * ns + 2]
        st_sc, send_sems, recv_sems, local_sems = rest[2 * ns + 2:]
        start, finish = _exchange_gather(w_hbm, wall_hbm, send_sems, recv_sems, local_sems)
        pl.when(pl.program_id(0) == 0)(start)
        row8 = lax.broadcasted_iota(jnp.int32, (NH, tq), 0)
        lse_all = jnp.zeros((NH, tq), F32)
        ones = jnp.ones((ONES, tk), BF16)
        for kv in range(2):
            qts = [q_ref[HD * (GRP * kv + t):HD * (GRP * kv + t) + HD, :] for t in range(GRP)]

            def keys(j, kv=kv):
                off = j * tk if isinstance(j, int) else pl.multiple_of(j * tk, tk)
                return k_ref[pl.ds(off, tk), :][:, HD * kv:HD * kv + HD]

            def scores(kj, t, qts=qts):
                return jnp.dot(kj, qts[t], preferred_element_type=F32)

            def step(j, carry, kv=kv, last=False):
                kj = keys(j)
                kn = None if last else keys(j + 1)
                vt = jnp.concatenate([vt_ref[per * j + u, HD * kv:HD * kv + HD, :] for u in range(per)], axis=1)
                v1 = jnp.concatenate([vt, ones], axis=0)
                sts = [st_sc[t] for t in range(AHEAD)]
                new = []
                for t in range(GRP):
                    m, acc = carry[2 * t], carry[2 * t + 1]
                    if t + AHEAD < GRP:
                        sts.append(scores(kj, t + AHEAD))
                    st = sts[t]
                    mn = jnp.maximum(m, jnp.max(st, axis=0, keepdims=True))
                    pt = jnp.exp2(st - mn)
                    if t + AHEAD >= GRP and not last:
                        st_sc[t + AHEAD - GRP] = scores(kn, t + AHEAD - GRP)
                    acc = jnp.exp2(m - mn) * acc + jnp.dot(v1, pt.astype(BF16), preferred_element_type=F32)
                    new += [mn, acc]
                return tuple(new)

            k0 = keys(0)
            for t in range(AHEAD):
                st_sc[t] = scores(k0, t)
            init = (jnp.full((1, tq), NEG, F32), jnp.zeros((HD + ONES, tq), F32)) * GRP
            res = step(nk - 1, lax.fori_loop(0, nk - 1, step, init), last=True)
            outs = []
            for t in range(GRP):
                m, acc = res[2 * t], res[2 * t + 1]
                l = acc[HD:HD + 1, :]
                outs.append((acc[:HD, :] / l).T)
                lse_all = jnp.where(row8 == GRP * kv + t, m + jnp.log2(l), lse_all)
            o_ref[:, 256 * kv:256 * kv + 256] = jnp.concatenate(outs, axis=1).astype(BF16)
        lse_ref[...] = lse_all
        pl.when(pl.program_id(0) == nq - 1)(finish)

    any_spec = pl.BlockSpec(memory_space=pl.ANY)
    return pl.pallas_call(
        body, grid=(nq,),
        in_specs=[pl.BlockSpec((BW, tq), lambda i: (0, i)), _const((S, 128)), _const(vt3.shape)] + [any_spec] * ns,
        out_specs=[_rows(tq, BW), pl.BlockSpec((NH, tq), lambda i: (0, i))] + [any_spec] * ns,
        out_shape=[jax.ShapeDtypeStruct((S, BW), BF16), jax.ShapeDtypeStruct((NH, S), F32)]
        + [jax.ShapeDtypeStruct((N_DEV,) + s.shape, s.dtype) for s in shards],
        scratch_shapes=[pltpu.VMEM((AHEAD, tk, tq), F32)] + _exchange_sems(ns) + [pltpu.SemaphoreType.DMA((ns,))],
        compiler_params=_params(1), name="attn_a_fwd",
    )(qt, k, vt3, *shards)


def _window_mask(i, tq, S):
    W = tq + 2 * WIN
    r = lax.broadcasted_iota(jnp.int32, (tq, W), 0)
    c = lax.broadcasted_iota(jnp.int32, (tq, W), 1)
    kpos = i * tq - WIN + c
    return (jnp.abs(c - WIN - r) <= WIN) & (kpos >= 0) & (kpos < S)


TQ_B = 256


def _attn_b_fwd(q, kp, vpt3, sink2):
    S = q.shape[0]
    tq = min(TQ_B, S)
    W = tq + 2 * WIN
    nc = vpt3.shape[0]
    ONES = 16

    def body(q_ref, k_ref, vt_ref, sink_ref, o_ref, lse_ref):
        i = pl.program_id(0)
        off = pl.multiple_of(i * tq, tq)
        r = lax.broadcasted_iota(jnp.int32, (W, tq), 1)
        c = lax.broadcasted_iota(jnp.int32, (W, tq), 0)
        kpos = i * tq - WIN + c
        valid = (jnp.abs(c - WIN - r) <= WIN) & (kpos >= 0) & (kpos < S)
        kw = k_ref[pl.ds(off, W), :]
        vt = jnp.concatenate([vt_ref[i + half] for half in range(W // tq)], axis=1)
        ones = jnp.ones((ONES, W), BF16)
        row8 = lax.broadcasted_iota(jnp.int32, (NH, tq), 0)
        lse_all = jnp.zeros((NH, tq), F32)
        qs = []
        for p in range(4):
            qp = q_ref[:, 128 * p:128 * p + 128]
            qs += [qp[:, :HD], qp[:, HD:]]
        khs = [kw[:, HD * kv:HD * kv + HD] for kv in range(2)]
        v1s = [jnp.concatenate([vt[HD * kv:HD * kv + HD, :], ones], axis=0) for kv in range(2)]

        def scores(h):
            return lax.dot_general(khs[h // GRP], qs[h], NT, preferred_element_type=F32)

        ss = [scores(0), scores(1)]
        outs = []
        for h in range(NH):
            if h + 2 < NH:
                ss.append(scores(h + 2))
            st = jnp.where(valid, ss[h], NEG)
            sk = sink_ref[:, h:h + 1]
            m = jnp.maximum(jnp.max(st, axis=0, keepdims=True), sk)
            acc = jnp.dot(v1s[h // GRP], jnp.exp2(st - m).astype(BF16), preferred_element_type=F32)
            l = acc[HD:HD + 1, :] + jnp.exp2(sk - m)
            outs.append((acc[:HD, :] / l).T)
            lse_all = jnp.where(row8 == h, m + jnp.log2(l), lse_all)
        for p in range(4):
            o_ref[:, 128 * p:128 * p + 128] = jnp.concatenate(outs[2 * p:2 * p + 2], axis=1).astype(BF16)
        lse_ref[...] = lse_all

    return pl.pallas_call(
        body, grid=(S // tq,),
        in_specs=[_rows(tq, BW), _const((S + 2 * WIN, 128)), _const((nc, 128, tq)), _const((1, NH))],
        out_specs=[_rows(tq, BW), pl.BlockSpec((NH, tq), lambda i: (0, i))],
        out_shape=[jax.ShapeDtypeStruct((S, BW), BF16), jax.ShapeDtypeStruct((NH, S), F32)],
        compiler_params=_params(1), name="attn_b_fwd",
    )(q, kp, vpt3, sink2)


def _merge_out(ya, yb, ga, gb, x, modv, wb, wout):
    S = x.shape[0]
    tm = min(256, S)

    def body(ya_ref, yb_ref, ga_ref, gb_ref, x_ref, mod_ref, wb_ref, wo_ref, x1_ref, mg_ref, ua_ref, ub_ref):
        ua = jnp.dot(ya_ref[...], wb_ref[0], preferred_element_type=F32)
        ub = jnp.dot(yb_ref[...], wb_ref[1], preferred_element_type=F32)
        merged = jax.nn.sigmoid(ga_ref[...].astype(F32)) * ua + jax.nn.sigmoid(gb_ref[...].astype(F32)) * ub
        mb = merged.astype(BF16)
        ua_ref[...] = ua.astype(BF16)
        ub_ref[...] = ub.astype(BF16)
        mg_ref[...] = mb
        x1_ref[...] = x_ref[...] + mod_ref[2:3, :] * jnp.dot(mb, wo_ref[...], preferred_element_type=F32)

    return pl.pallas_call(
        body, grid=(S // tm,),
        in_specs=[_rows(tm, BW), _rows(tm, BW), _rows(tm, D), _rows(tm, D), _rows(tm, D), _const((6, D)),
                  _const((2, BW, D)), _const((D, D))],
        out_specs=[_rows(tm, D)] * 4,
        out_shape=[jax.ShapeDtypeStruct((S, D), F32)] + [jax.ShapeDtypeStruct((S, D), BF16)] * 3,
        compiler_params=_params(1), name="merge_out",
    )(ya, yb, ga, gb, x, modv, wb, wout)


def _mlp_fwd(x1, modv, n2g, wmi, wmo, fg, target):
    S = x1.shape[0]
    tm = min(MLP_ROWS, S)
    tf = wmi.shape[2]
    nj = wmi.shape[0] // MLP_SHARDS

    def body(x1_ref, mod_ref, g_ref, wi_ref, wo_ref, fg_ref, t_ref, h2_ref, hp_ref, dx2_ref, st_ref, acc_ref):
        i, j = pl.program_id(0), pl.program_id(1)

        @pl.when(j == 0)
        def _():
            xt = x1_ref[...]
            r = lax.rsqrt(jnp.mean(xt * xt, axis=-1, keepdims=True) + EPS)
            h2 = ((xt * r) * g_ref[...]) * (1.0 + mod_ref[4:5, :]) + mod_ref[3:4, :]
            h2_ref[...] = h2.astype(BF16)
            acc_ref[...] = jnp.zeros_like(acc_ref)

        @pl.when((i == 0) & (j == 0))
        def _():
            st_ref[...] = jnp.zeros_like(st_ref)

        out = None
        for u in range(MLP_SHARDS):
            hp = jnp.dot(h2_ref[...], wi_ref[u], preferred_element_type=F32)
            hp_ref[:, tf * u:tf * u + tf] = hp.astype(BF16)
            hid = jnp.square(jnp.maximum(hp, 0.0))
            part = jnp.dot(hid.astype(BF16), wo_ref[u], preferred_element_type=F32)
            out = part if out is None else out + part
        acc_ref[...] += out

        @pl.when(j == nj - 1)
        def _():
            x2 = x1_ref[...] + mod_ref[5:6, :] * acc_ref[...]
            r3 = lax.rsqrt(jnp.mean(x2 * x2, axis=-1, keepdims=True) + EPS)
            xn = x2 * r3
            err = xn * fg_ref[...] - t_ref[...]
            dy = err * (1.0 / D)
            gy = dy * fg_ref[...]
            dx2_ref[...] = r3 * (gy - xn * jnp.mean(gy * xn, axis=-1, keepdims=True))
            st_ref[0:1, :] += jnp.sum(dy * xn, axis=0, keepdims=True)
            st_ref[1:2, :] += jnp.sum(err * err, axis=0, keepdims=True) * (0.5 / D)

    return pl.pallas_call(
        body, grid=(S // tm, nj),
        in_specs=[pl.BlockSpec((tm, D), lambda i, j: (i, 0)), _const((6, D)), _const((1, D)),
                  pl.BlockSpec((MLP_SHARDS, D, tf), lambda i, j: (j, 0, 0)), pl.BlockSpec((MLP_SHARDS, tf, D), lambda i, j: (j, 0, 0)),
                  _const((1, D)), pl.BlockSpec((tm, D), lambda i, j: (i, 0))],
        out_specs=[pl.BlockSpec((tm, D), lambda i, j: (i, 0)), pl.BlockSpec((tm, MLP_SHARDS * tf), lambda i, j: (i, j)),
                   pl.BlockSpec((tm, D), lambda i, j: (i, 0)), _const((8, D))],
        out_shape=[jax.ShapeDtypeStruct((S, D), BF16), jax.ShapeDtypeStruct((S, wmi.shape[0] * tf), BF16),
                   jax.ShapeDtypeStruct((S, D), F32), jax.ShapeDtypeStruct((8, D), F32)],
        scratch_shapes=[pltpu.VMEM((tm, D), F32)],
        compiler_params=_params(2), name="mlp_fwd",
    )(x1, modv, n2g, wmi, wmo, fg, target)


def _mlp_bwd(dx2, x1, hp, modv, n2g, wmi, wmo):
    S = x1.shape[0]
    tm = min(MLP_ROWS, S)
    tf = wmi.shape[2]
    nj = wmi.shape[0] // MLP_SHARDS

    def body(dx2_ref, x1_ref, hp_ref, mod_ref, g_ref, wi_ref, wo_ref, dhp_ref, dx1_ref, st_ref, dmo_ref, acc_ref):
        i, j = pl.program_id(0), pl.program_id(1)

        @pl.when(j == 0)
        def _():
            dmo_ref[...] = (mod_ref[5:6, :] * dx2_ref[...]).astype(BF16)
            acc_ref[...] = jnp.zeros_like(acc_ref)

        @pl.when((i == 0) & (j == 0))
        def _():
            st_ref[...] = jnp.zeros_like(st_ref)

        out = None
        for u in range(MLP_SHARDS):
            sl = slice(tf * u, tf * u + tf)
            dhid = lax.dot_general(dmo_ref[...], wo_ref[u], NT, preferred_element_type=F32)
            dhp = (dhid * (2.0 * jnp.maximum(hp_ref[:, sl].astype(F32), 0.0))).astype(BF16)
            dhp_ref[:, sl] = dhp
            part = lax.dot_general(dhp, wi_ref[u], NT, preferred_element_type=F32)
            out = part if out is None else out + part
        acc_ref[...] += out

        @pl.when(j == nj - 1)
        def _():
            dh2 = acc_ref[...]
            xt = x1_ref[...]
            r = lax.rsqrt(jnp.mean(xt * xt, axis=-1, keepdims=True) + EPS)
            xn = xt * r
            st_ref[0:1, :] += jnp.sum(dh2, axis=0, keepdims=True)
            st_ref[1:2, :] += jnp.sum(dh2 * xn, axis=0, keepdims=True)
            dxn = dh2 * (g_ref[...] * (1.0 + mod_ref[4:5, :]))
            dx1_ref[...] = dx2_ref[...] + r * (dxn - xn * jnp.mean(dxn * xn, axis=-1, keepdims=True))

    return pl.pallas_call(
        body, grid=(S // tm, nj),
        in_specs=[pl.BlockSpec((tm, D), lambda i, j: (i, 0)), pl.BlockSpec((tm, D), lambda i, j: (i, 0)),
                  pl.BlockSpec((tm, MLP_SHARDS * tf), lambda i, j: (i, j)), _const((6, D)), _const((1, D)),
                  pl.BlockSpec((MLP_SHARDS, D, tf), lambda i, j: (j, 0, 0)), pl.BlockSpec((MLP_SHARDS, tf, D), lambda i, j: (j, 0, 0))],
        out_specs=[pl.BlockSpec((tm, MLP_SHARDS * tf), lambda i, j: (i, j)), pl.BlockSpec((tm, D), lambda i, j: (i, 0)), _const((8, D))],
        out_shape=[jax.ShapeDtypeStruct((S, wmi.shape[0] * tf), BF16), jax.ShapeDtypeStruct((S, D), F32),
                   jax.ShapeDtypeStruct((8, D), F32)],
        scratch_shapes=[pltpu.VMEM((tm, D), BF16), pltpu.VMEM((tm, D), F32)],
        compiler_params=_params(2), name="mlp_bwd",
    )(dx2, x1, hp, modv, n2g, wmi, wmo)


def _tn_matmul(a, b, tk, tn, name, relu_sq=False, dev_major=False, rows=1024):
    S, K = a.shape
    N = b.shape[1]
    ts = min(rows, S)
    ns = S // ts

    def body(a_ref, b_ref, o_ref):
        @pl.when(pl.program_id(2) == 0)
        def _():
            o_ref[...] = jnp.zeros_like(o_ref)

        at = a_ref[...]
        if relu_sq:
            at = jnp.square(jnp.maximum(at.astype(F32), 0.0)).astype(BF16)
        o_ref[...] += lax.dot_general(at, b_ref[...].astype(BF16), TN, preferred_element_type=F32)

    if dev_major:
        out_spec = pl.BlockSpec((None, tk, tn), lambda k, n, s: (n, k, 0))
        out_shape = jax.ShapeDtypeStruct((N // tn, K, tn), F32)
    else:
        out_spec = pl.BlockSpec((tk, tn), lambda k, n, s: (k, n))
        out_shape = jax.ShapeDtypeStruct((K, N), F32)
    return pl.pallas_call(
        body, grid=(K // tk, N // tn, ns),
        in_specs=[pl.BlockSpec((ts, tk), lambda k, n, s: (s, k)), pl.BlockSpec((ts, tn), lambda k, n, s: (s, n))],
        out_specs=out_spec, out_shape=out_shape,
        compiler_params=_params(3), name=name,
    )(a, b)


def _scale_gate(m, w, g, row, name):
    K = m.shape[0]
    tk = min(512, K)

    def body(m_ref, w_ref, mod_ref, dw_ref, dg_ref):
        @pl.when(pl.program_id(0) == 0)
        def _():
            dg_ref[...] = jnp.zeros_like(dg_ref)

        mt = m_ref[...]
        dw_ref[...] = mt * mod_ref[row:row + 1, :]
        dg_ref[0:1, :] += jnp.sum(mt * w_ref[...].astype(F32), axis=0, keepdims=True)

    return pl.pallas_call(
        body, grid=(K // tk,),
        in_specs=[_rows(tk, D), _rows(tk, D), _const((6, D))],
        out_specs=[_rows(tk, D), _const((8, D))],
        out_shape=[jax.ShapeDtypeStruct((K, D), F32), jax.ShapeDtypeStruct((8, D), F32)],
        compiler_params=_params(1), name=name,
    )(m, w, g)


def _merge_bwd(dx1, modv, ga, gb, ua, ub, ya, yb, wb, wout):
    S = dx1.shape[0]
    tm = min(512, S)

    def body(dx1_ref, mod_ref, ga_ref, gb_ref, ua_ref, ub_ref, ya_ref, yb_ref, wb_ref, wo_ref,
             dua_ref, dub_ref, dga_ref, dgb_ref, dya_ref, dyb_ref, dla_ref, dlb_ref, dyat_ref):
        dao = (mod_ref[2:3, :] * dx1_ref[...]).astype(BF16)
        dm = lax.dot_general(dao, wo_ref[...], NT, preferred_element_type=F32)
        r = lax.broadcasted_iota(jnp.int32, (BW, NH), 0) // HD
        c = lax.broadcasted_iota(jnp.int32, (BW, NH), 1)
        head_of = (r == c).astype(BF16)
        for br, (g_ref, u_ref, y_ref, du_ref, dg_ref, dy_ref, dl_ref) in enumerate((
                (ga_ref, ua_ref, ya_ref, dua_ref, dga_ref, dya_ref, dla_ref),
                (gb_ref, ub_ref, yb_ref, dub_ref, dgb_ref, dyb_ref, dlb_ref))):
            sg = jax.nn.sigmoid(g_ref[...].astype(F32))
            du = (dm * sg).astype(BF16)
            du_ref[...] = du
            dg_ref[...] = (dm * u_ref[...].astype(F32) * sg * (1.0 - sg)).astype(BF16)
            dy = lax.dot_general(du, wb_ref[br], NT, preferred_element_type=F32)
            dyb16 = dy.astype(BF16)
            dy_ref[...] = dyb16
            if br == 0:
                dyat_ref[...] = dy.T.astype(BF16)
            prod = dyb16.astype(F32) * y_ref[...].astype(F32)
            hi = prod.astype(BF16)
            lo = (prod - hi.astype(F32)).astype(BF16)
            dl_ref[...] = (jnp.dot(hi, head_of, preferred_element_type=F32) + jnp.dot(lo, head_of, preferred_element_type=F32))

    return pl.pallas_call(
        body, grid=(S // tm,),
        in_specs=[_rows(tm, D), _const((6, D)), _rows(tm, D), _rows(tm, D), _rows(tm, D), _rows(tm, D),
                  _rows(tm, BW), _rows(tm, BW), _const((2, BW, D)), _const((D, D))],
        out_specs=[_rows(tm, D)] * 4 + [_rows(tm, BW)] * 2 + [_rows(tm, NH)] * 2 + [pl.BlockSpec((BW, tm), lambda i: (0, i))],
        out_shape=[jax.ShapeDtypeStruct((S, D), BF16)] * 4 + [jax.ShapeDtypeStruct((S, BW), BF16)] * 2
        + [jax.ShapeDtypeStruct((S, NH), F32)] * 2 + [jax.ShapeDtypeStruct((BW, S), BF16)],
        compiler_params=_params(1), name="merge_bwd",
    )(dx1, modv, ga, gb, ua, ub, ya, yb, wb, wout)


def _attn_a_bwd(q, qt, k, kt3, vt3, do, dot_, lse, delta, chunks):
    S = q.shape[0]
    tq = min(1024, S)
    nk, _, tk = kt3.shape
    nq = S // tq
    nc = len(chunks)

    def body(q_ref, qt_ref, do_ref, dot_ref, lse_ref, dl_ref, kt_ref, vt_ref, k_ref, *rest):
        g_hbm, (dq_ref, dk_hbm, dv_hbm), recv_hbm = rest[:nc], rest[nc:nc + 3], rest[nc + 3:2 * nc + 3]
        dk_sc, dv_sc, sem, send_sems, recv_sems = rest[2 * nc + 3:]
        i = pl.program_id(0)
        start, finish = _exchange_scatter(g_hbm, recv_hbm, send_sems, recv_sems)

        @pl.when(i == 0)
        def _():
            start()
            dk_sc[...] = jnp.zeros_like(dk_sc)
            dv_sc[...] = jnp.zeros_like(dv_sc)

        for kv in range(2):
            qg = q_ref[:, 256 * kv:256 * kv + 256]
            dog = do_ref[:, 256 * kv:256 * kv + 256]
            heads = []
            for t in range(GRP):
                h = GRP * kv + t
                heads.append((qg[:, HD * t:HD * t + HD], dog[:, HD * t:HD * t + HD],
                              qt_ref[HD * h:HD * h + HD, :], dot_ref[HD * h:HD * h + HD, :],
                              lse_ref[:, h:h + 1], dl_ref[:, h:h + 1]))

            q2 = [jnp.concatenate([heads[2 * u][0], heads[2 * u + 1][0]], axis=0) for u in range(GRP // 2)]
            do2 = [jnp.concatenate([heads[2 * u][1], heads[2 * u + 1][1]], axis=0) for u in range(GRP // 2)]

            def step(j, carry, kv=kv, heads=heads, q2=q2, do2=do2):
                kjt = kt_ref[j, HD * kv:HD * kv + HD, :]
                vjt = vt_ref[j, HD * kv:HD * kv + HD, :]
                kj = k_ref[pl.ds(pl.multiple_of(j * tk, tk), tk), :][:, HD * kv:HD * kv + HD]
                dkt = jnp.zeros((HD, tk), F32)
                dvt = jnp.zeros((HD, tk), F32)
                new = []

                def logits(u):
                    return (jnp.dot(q2[u], kjt, preferred_element_type=F32), jnp.dot(do2[u], vjt, preferred_element_type=F32))

                sd = [logits(0)]
                for t, (qh, doh, qth, doth, lse_h, dl_h) in enumerate(heads):
                    if t == 0:
                        sd.append(logits(1))
                    rows = slice(tq * (t % 2), tq * (t % 2) + tq)
                    s, dp = sd[t // 2][0][rows, :], sd[t // 2][1][rows, :]
                    pm = jnp.exp2(s - lse_h)
                    ds = (pm * (dp - dl_h)).astype(BF16)
                    dvt = dvt + jnp.dot(doth, pm.astype(BF16), preferred_element_type=F32)
                    dkt = dkt + jnp.dot(qth, ds, preferred_element_type=F32)
                    new.append(carry[t] + jnp.dot(ds, kj, preferred_element_type=F32))
                dk_sc[j, HD * kv:HD * kv + HD, :] += dkt
                dv_sc[j, HD * kv:HD * kv + HD, :] += dvt
                return tuple(new)

            res = lax.fori_loop(0, nk, step, (jnp.zeros((tq, HD), F32),) * GRP)
            for t in range(GRP):
                dq_ref[HD * (GRP * kv + t):HD * (GRP * kv + t) + HD, :] = res[t].T

        @pl.when(i == nq - 1)
        def _():
            c1 = pltpu.make_async_copy(dk_sc, dk_hbm, sem.at[0])
            c2 = pltpu.make_async_copy(dv_sc, dv_hbm, sem.at[1])
            c1.start()
            c2.start()
            c1.wait()
            c2.wait()
            finish()

    any_spec = pl.BlockSpec(memory_space=pl.ANY)
    cols = pl.BlockSpec((BW, tq), lambda i: (0, i))
    return pl.pallas_call(
        body, grid=(nq,),
        in_specs=[_rows(tq, BW), cols, _rows(tq, BW), cols, _rows(tq, NH), _rows(tq, NH), _const((nk, 128, tk)),
                  _const((nk, 128, tk)), _const((S, 128))] + [any_spec] * nc,
        out_specs=[cols, any_spec, any_spec] + [any_spec] * nc,
        out_shape=[jax.ShapeDtypeStruct((BW, S), F32), jax.ShapeDtypeStruct((nk, 128, tk), F32),
                   jax.ShapeDtypeStruct((nk, 128, tk), F32)]
        + [jax.ShapeDtypeStruct((N_DEV - 1,) + c.shape[1:], c.dtype) for c in chunks],
        scratch_shapes=[pltpu.VMEM((nk, 128, tk), F32), pltpu.VMEM((nk, 128, tk), F32), pltpu.SemaphoreType.DMA((2,))]
        + _exchange_sems(nc),
        compiler_params=_params(1), name="attn_a_bwd",
    )(q, qt, do, dot_, lse, delta, kt3, vt3, k, *chunks)


def _attn_b_bwd(q, kp, vp, sink2, do, lse, delta):
    S = q.shape[0]
    tq = min(TQ_B, S)
    W = tq + 2 * WIN
    nq = S // tq
    nc = (S + 2 * WIN) // tq

    def body(q_ref, k_ref, v_ref, sink_ref, do_ref, lse_ref, dl_ref, dq_ref, dk_hbm, dv_hbm, ds_ref, dk_sc, dv_sc, sem):
        i = pl.program_id(0)

        @pl.when(i == 0)
        def _():
            dk_sc[...] = jnp.zeros_like(dk_sc)
            dv_sc[...] = jnp.zeros_like(dv_sc)
            ds_ref[...] = jnp.zeros_like(ds_ref)

        off = pl.multiple_of(i * tq, tq)
        valid = _window_mask(i, tq, S)
        kw = k_ref[pl.ds(off, W), :]
        vw = v_ref[pl.ds(off, W), :]
        lse_i = lse_ref[...]
        dl_i = dl_ref[...]
        qa = q_ref[...]
        doa = do_ref[...]
        qt = qa.astype(F32).T.astype(BF16)
        dot_ = doa.astype(F32).T.astype(BF16)
        khs = [kw[:, HD * kv:HD * kv + HD] for kv in range(2)]
        vhs = [vw[:, HD * kv:HD * kv + HD] for kv in range(2)]

        def logits(h):
            return (lax.dot_general(qa[:, HD * h:HD * h + HD], khs[h // GRP], NT, preferred_element_type=F32),
                    lax.dot_general(doa[:, HD * h:HD * h + HD], vhs[h // GRP], NT, preferred_element_type=F32))

        sd = [logits(0)]
        dqs = []
        dkt = [jnp.zeros((HD, W), F32), jnp.zeros((HD, W), F32)]
        dvt = [jnp.zeros((HD, W), F32), jnp.zeros((HD, W), F32)]
        for h in range(NH):
            kv = h // GRP
            if h + 1 < NH:
                sd.append(logits(h + 1))
            s, dp = sd[h]
            pm = jnp.exp2(jnp.where(valid, s, NEG) - lse_i[:, h:h + 1])
            ds = (pm * (dp - dl_i[:, h:h + 1])).astype(BF16)
            dvt[kv] = dvt[kv] + jnp.dot(dot_[HD * h:HD * h + HD, :], pm.astype(BF16), preferred_element_type=F32)
            dkt[kv] = dkt[kv] + jnp.dot(qt[HD * h:HD * h + HD, :], ds, preferred_element_type=F32)
            dqs.append(jnp.dot(ds, khs[kv], preferred_element_type=F32))
        for p in range(4):
            dq_ref[:, 128 * p:128 * p + 128] = jnp.concatenate(dqs[2 * p:2 * p + 2], axis=1)
        for half in range(W // tq):
            dk_sc[i + half] += jnp.concatenate([d[:, tq * half:tq * half + tq] for d in dkt], axis=0)
            dv_sc[i + half] += jnp.concatenate([d[:, tq * half:tq * half + tq] for d in dvt], axis=0)
        psd = jnp.exp2(sink_ref[...] - lse_i) * dl_i
        r = lax.broadcasted_iota(jnp.int32, (NH, 128), 0)
        c = lax.broadcasted_iota(jnp.int32, (NH, 128), 1)
        row = jnp.dot(jnp.sum(psd, axis=0, keepdims=True), (r == c).astype(F32),
                      preferred_element_type=F32, precision=lax.Precision.HIGHEST)
        ds_ref[...] -= jnp.broadcast_to(row, (8, 128))

        @pl.when(i == nq - 1)
        def _():
            c1 = pltpu.make_async_copy(dk_sc, dk_hbm, sem.at[0])
            c2 = pltpu.make_async_copy(dv_sc, dv_hbm, sem.at[1])
            c1.start()
            c2.start()
            c1.wait()
            c2.wait()

    any_spec = pl.BlockSpec(memory_space=pl.ANY)
    return pl.pallas_call(
        body, grid=(nq,),
        in_specs=[_rows(tq, BW), _const((S + 2 * WIN, 128)), _const((S + 2 * WIN, 128)), _const((1, NH)),
                  _rows(tq, BW), _rows(tq, NH), _rows(tq, NH)],
        out_specs=[_rows(tq, BW), any_spec, any_spec, _const((8, 128))],
        out_shape=[jax.ShapeDtypeStruct((S, BW), F32), jax.ShapeDtypeStruct((nc, 128, tq), F32),
                   jax.ShapeDtypeStruct((nc, 128, tq), F32), jax.ShapeDtypeStruct((8, 128), F32)],
        scratch_shapes=[pltpu.VMEM((nc, 128, tq), F32), pltpu.VMEM((nc, 128, tq), F32), pltpu.SemaphoreType.DMA((2,))],
        compiler_params=_params(1), name="attn_b_bwd",
    )(q, kp, vp, sink2, do, lse, delta)


def _qk_bwd(dqa_t, dka_t3, dva_t3, dqb, dkb, dvb, qar, kar, qg2, kg2, tab_a, tab_b, dga, dgb):
    S = dqb.shape[0]
    tm = min(256, S)
    per = dka_t3.shape[2] // tm

    def body(dqa_ref, dka_ref, dva_ref, dqb_ref, dkb_ref, dvb_ref, qar_ref, kar_ref, qg_ref, kg_ref, ta_ref, tb_ref,
             dga_ref, dgb_ref, dp_ref, st_ref):
        @pl.when(pl.program_id(0) == 0)
        def _():
            st_ref[...] = jnp.zeros_like(st_ref)

        seg = _seg_matrix(128, HD)

        def norm_bwd(dz_rot, raw, g):
            dzn = _rope_t(dz_rot, ta_ref, 16)
            raw = raw.astype(F32)
            rr = lax.rsqrt(_seg_sum(raw * raw, seg) * (1.0 / HD) + EPS)
            zhat = raw * rr
            dzh = dzn * g
            draw = rr * (dzh - zhat * (_seg_sum(dzh * zhat, seg) * (1.0 / HD)))
            return draw, jnp.sum(dzn * zhat, axis=0, keepdims=True)

        gq = jnp.zeros((1, 128), F32)
        for p in range(4):
            sl = slice(128 * p, 128 * p + 128)
            draw, gsum = norm_bwd(dqa_ref[sl, :].T * 0.125, qar_ref[:, sl], qg_ref[...])
            gq = gq + gsum
            dp_ref[:, sl] = draw.astype(BF16)
            dp_ref[:, 768 + 128 * p:768 + 128 * p + 128] = _rope_t(dqb_ref[:, sl] * 0.125, tb_ref, 32).astype(BF16)
        draw, gk = norm_bwd(dka_ref[...].T * LN2, kar_ref[...], kg_ref[...])
        dp_ref[:, 512:640] = draw.astype(BF16)
        dp_ref[:, 640:768] = dva_ref[...].T.astype(BF16)
        dp_ref[:, 1280:1408] = _rope_t(dkb_ref[...] * LN2, tb_ref, 32).astype(BF16)
        dp_ref[:, 1408:1536] = dvb_ref[...].astype(BF16)
        dp_ref[:, 1536:2560] = dga_ref[...]
        dp_ref[:, 2560:3584] = dgb_ref[...]
        st_ref[0:1, :] += gq
        st_ref[1:2, :] += gk

    tab = pl.BlockSpec((3, tm, 128), lambda i: (0, i, 0))
    chunk_t = pl.BlockSpec((None, 128, tm), lambda i: (i // per, 0, i % per))
    return pl.pallas_call(
        body, grid=(S // tm,),
        in_specs=[pl.BlockSpec((BW, tm), lambda i: (0, i)), chunk_t, chunk_t, _rows(tm, BW), _rows(tm, 128), _rows(tm, 128),
                  _rows(tm, BW), _rows(tm, 128), _const((1, 128)), _const((1, 128)), tab, tab, _rows(tm, D), _rows(tm, D)],
        out_specs=[_rows(tm, INW), _const((8, 128))],
        out_shape=[jax.ShapeDtypeStruct((S, INW), BF16), jax.ShapeDtypeStruct((8, 128), F32)],
        compiler_params=_params(1), name="qk_bwd",
    )(dqa_t, dka_t3, dva_t3, dqb, dkb, dvb, qar, kar, qg2, kg2, tab_a, tab_b, dga, dgb)


def _in_bwd(dproj, win, x, dx1, modv, n1g, chunks):
    S = x.shape[0]
    tm = min(512, S)
    n = S // tm
    ts = min(256, tm)

    def body(dp_ref, w_ref, x_ref, dx1_ref, mod_ref, g_ref, c_hbm, gx_ref, st_ref, recv_hbm, send_sems, recv_sems):
        start, finish = _exchange_scatter([c_hbm], [recv_hbm], send_sems, recv_sems)

        @pl.when(pl.program_id(0) == 0)
        def _():
            start()
            st_ref[...] = jnp.zeros_like(st_ref)

        subs = [slice(ts * u, ts * u + ts) for u in range(tm // ts)]
        dhs = [lax.dot_general(dp_ref[rows, :], w_ref[...], NT, preferred_element_type=F32) for rows in subs]
        for rows, dh in zip(subs, dhs):
            xt = x_ref[rows, :]
            r = lax.rsqrt(jnp.mean(xt * xt, axis=-1, keepdims=True) + EPS)
            xn = xt * r
            st_ref[0:1, :] += jnp.sum(dh, axis=0, keepdims=True)
            st_ref[1:2, :] += jnp.sum(dh * xn, axis=0, keepdims=True)
            dxn = dh * (g_ref[...] * (1.0 + mod_ref[1:2, :]))
            gx_ref[rows, :] = dx1_ref[rows, :] + r * (dxn - xn * jnp.mean(dxn * xn, axis=-1, keepdims=True))
        pl.when(pl.program_id(0) == n - 1)(finish)

    any_spec = pl.BlockSpec(memory_space=pl.ANY)
    return pl.pallas_call(
        body, grid=(n,),
        in_specs=[_rows(tm, INW), _const((D, INW)), _rows(tm, D), _rows(tm, D), _const((6, D)), _const((1, D)), any_spec],
        out_specs=[_rows(tm, D), _const((8, D)), any_spec],
        out_shape=[jax.ShapeDtypeStruct((S, D), F32), jax.ShapeDtypeStruct((8, D), F32),
                   jax.ShapeDtypeStruct((N_DEV - 1,) + chunks.shape[1:], chunks.dtype)],
        scratch_shapes=_exchange_sems(1),
        compiler_params=_params(1), name="in_bwd",
    )(dproj, win, x, dx1, modv, n1g, chunks)


def _pack_small(st1, st2, stf, dg1, dg2, stqk, dsink, modv, n1g, n2g):
    def body(st1_ref, st2_ref, stf_ref, dg1_ref, dg2_ref, qk_ref, ds_ref, mod_ref, g1_ref, g2_ref, o_ref):
        a1, b1 = st1_ref[0:1, :], st1_ref[1:2, :]
        a2, b2 = st2_ref[0:1, :], st2_ref[1:2, :]
        r = lax.broadcasted_iota(jnp.int32, (128, D), 0)
        c = lax.broadcasted_iota(jnp.int32, (128, D), 1)
        fold_q = (c == r % HD).astype(F32)
        fold_k = (c == HD + r % HD).astype(F32)
        keep = (c == r).astype(F32)

        def place(v, sel):
            return jnp.dot(v, sel, preferred_element_type=F32, precision=lax.Precision.HIGHEST)

        loss = jnp.sum(stf_ref[1:2, :], axis=1, keepdims=True)
        lane = lax.broadcasted_iota(jnp.int32, (1, D), 1)
        rows = [a1, g1_ref[...] * b1, dg1_ref[0:1, :], a2, g2_ref[...] * b2, dg2_ref[0:1, :],
                (1.0 + mod_ref[1:2, :]) * b1, (1.0 + mod_ref[4:5, :]) * b2, stf_ref[0:1, :],
                place(qk_ref[0:1, :], fold_q) + place(qk_ref[1:2, :], fold_k),
                place(ds_ref[0:1, :], keep),
                jnp.where(lane == 0, loss, 0.0)]
        rows += [jnp.zeros((1, D), F32)] * (SMALL_ROWS - len(rows))
        for n, v in enumerate(rows):
            o_ref[n:n + 1, :] = v

    return pl.pallas_call(
        body, out_shape=jax.ShapeDtypeStruct((SMALL_ROWS, D), F32),
        compiler_params=pltpu.CompilerParams(vmem_limit_bytes=V7X_VMEM_LIMIT), name="pack_small",
    )(st1, st2, stf, dg1, dg2, stqk, dsink, modv, n1g, n2g)


def _wada_grad(silu_all, dmod_cols):
    def body(a_ref, b_ref, o_ref):
        o_ref[...] = lax.dot_general(a_ref[...], b_ref[...], TN, preferred_element_type=F32, precision=lax.Precision.HIGHEST)

    return pl.pallas_call(
        body, out_shape=jax.ShapeDtypeStruct((D, dmod_cols.shape[1]), F32),
        compiler_params=pltpu.CompilerParams(vmem_limit_bytes=V7X_VMEM_LIMIT), name="wada_grad",
    )(silu_all, dmod_cols)


def _adamw_sum(parts, w, m, v, name):
    R, C = w.shape
    tr = R if R <= 64 else next(t for t in (256, 128, 64, 32, 16, 8) if R % t == 0)
    n = len(parts)
    dyn = [idx for _, idx in parts if idx is not None and not isinstance(idx, int)]
    b1c = 1.0 - ADAM_B1 ** ADAM_STEP
    b2c = 1.0 - ADAM_B2 ** ADAM_STEP

    def body(*refs):
        refs = refs[len(dyn):]
        g = refs[0][...].astype(F32)
        for k in range(1, n):
            g = g + refs[k][...].astype(F32)
        w_ref, m_ref, v_ref, g_out, d_out, m_out, v_out = refs[n:]
        mn = ADAM_B1 * m_ref[...] + (1.0 - ADAM_B1) * g
        vn = ADAM_B2 * v_ref[...] + (1.0 - ADAM_B2) * jnp.square(g)
        g_out[...] = g
        m_out[...] = mn
        v_out[...] = vn
        d_out[...] = -ADAM_LR * ((mn / b1c) / (jnp.sqrt(vn / b2c) + ADAM_EPS) + ADAM_WD * w_ref[...])

    in_specs = []
    nd = 0
    for a, idx in parts:
        if idx is None:
            in_specs.append(pl.BlockSpec((tr, C), lambda i, *s: (i, 0)))
        elif isinstance(idx, int):
            in_specs.append(pl.BlockSpec((None, tr, C), lambda i, *s, idx=idx: (idx, i, 0)))
        else:
            in_specs.append(pl.BlockSpec((None, tr, C), lambda i, *s, nd=nd: (s[nd][0], i, 0)))
            nd += 1
    blk = pl.BlockSpec((tr, C), lambda i, *s: (i, 0))
    grid_spec = pltpu.PrefetchScalarGridSpec(
        num_scalar_prefetch=len(dyn), grid=(R // tr,), in_specs=in_specs + [blk] * 3, out_specs=[blk] * 4)
    return pl.pallas_call(
        body, grid_spec=grid_spec, out_shape=[jax.ShapeDtypeStruct((R, C), F32)] * 4,
        compiler_params=_params(1), name=name,
    )(*dyn, *[a for a, _ in parts], w, m, v)


def _me():
    return lax.axis_index("x"), lax.axis_index("y"), lax.axis_index("c")


def _peer(k):
    x, y, c = _me()
    return (x ^ ((k >> 2) & 1), y ^ ((k >> 1) & 1), c ^ (k & 1))


def _ada_exchange(c_row, w_ada, b_rows):
    NW = w_ada.shape[1]

    def body(c_ref, w_ref, b_ref, sall_ref, mod_ref, src_ref, mp_ref, send1, recv1, send2, recv2):
        x, y, c = _me()
        me = 4 * x + 2 * y + c
        cv = c_ref[...]
        src_ref[...] = jnp.broadcast_to(cv * jax.nn.sigmoid(cv), (8, D))
        mine = pl.ds(pl.multiple_of(me * 8, 8), 8)
        sall_ref[mine, :] = src_ref[...]
        sends = [pltpu.make_async_remote_copy(src_ref, sall_ref.at[mine, :], send1.at[k - 1], recv1.at[k - 1],
                                              device_id=_peer(k), device_id_type=MESH) for k in range(1, N_DEV)]
        for cp in sends:
            cp.start()
        for k in range(1, N_DEV):
            theirs = pl.ds(pl.multiple_of((me ^ k) * 8, 8), 8)
            pltpu.make_async_remote_copy(src_ref, sall_ref.at[theirs, :], send1.at[k - 1], recv1.at[k - 1],
                                         device_id=_peer(k), device_id_type=MESH).wait_recv()
        for cp in sends:
            cp.wait_send()
        mp_ref[...] = jnp.dot(sall_ref[...], w_ref[...], preferred_element_type=F32, precision=lax.Precision.HIGHEST)
        mod_ref[mine, :] = mp_ref[mine, :] + b_ref[mine, :]
        sends = []
        for k in range(1, N_DEV):
            theirs = pl.ds(pl.multiple_of((me ^ k) * 8, 8), 8)
            sends.append(pltpu.make_async_remote_copy(mp_ref.at[theirs, :], mod_ref.at[mine, :], send2.at[k - 1], recv2.at[k - 1],
                                                      device_id=_peer(k), device_id_type=MESH))
        for cp in sends:
            cp.start()
        for k in range(1, N_DEV):
            theirs = pl.ds(pl.multiple_of((me ^ k) * 8, 8), 8)
            pltpu.make_async_remote_copy(mp_ref.at[mine, :], mod_ref.at[theirs, :], send2.at[k - 1], recv2.at[k - 1],
                                         device_id=_peer(k), device_id_type=MESH).wait_recv()
            mod_ref[theirs, :] = mod_ref[theirs, :] + b_ref[theirs, :]
        for cp in sends:
            cp.wait_send()

    vm = pl.BlockSpec(memory_space=pltpu.VMEM)
    return pl.pallas_call(
        body, in_specs=[vm, vm, vm], out_specs=[vm, vm],
        out_shape=[jax.ShapeDtypeStruct((8 * N_DEV, D), F32), jax.ShapeDtypeStruct((8 * N_DEV, NW), F32)],
        scratch_shapes=[pltpu.VMEM((8, D), F32), pltpu.VMEM((8 * N_DEV, NW), F32)]
        + [pltpu.SemaphoreType.DMA((N_DEV - 1,))] * 4,
        compiler_params=pltpu.CompilerParams(vmem_limit_bytes=V7X_VMEM_LIMIT), name="ada_exchange",
    )(c_row, w_ada, b_rows)


def _weight_gather(shard):
    def body(x_ref, out_ref, send_sems, recv_sems, local_sem):
        x, y, c = _me()
        me, sibling = (x, y, c), (x, y, 1 - c)
        chips = [(1 - x, y), (x, 1 - y), (1 - x, 1 - y)]

        def slot(px, py, pc):
            return out_ref.at[4 * px + 2 * py + pc]

        def copy(k, block, to, src=None):
            return pltpu.make_async_remote_copy(
                src_ref=slot(*block) if src is None else src, dst_ref=slot(*block),
                send_sem=send_sems.at[k], recv_sem=recv_sems.at[k], device_id=to, device_id_type=MESH)

        mine = pltpu.make_async_copy(x_ref, slot(*me), local_sem)
        mine.start()
        first = [copy(0, me, sibling, src=x_ref)]
        first += [copy(1 + j, me, (*chip, c), src=x_ref) for j, chip in enumerate(chips)]
        for cp in first:
            cp.start()
        passed = [copy(4 + j, (*chip, c), sibling) for j, chip in enumerate(chips)]
        for j, chip in enumerate(chips):
            copy(1 + j, (*chip, c), me).wait_recv()
            passed[j].start()
        copy(0, sibling, me).wait_recv()
        for j, chip in enumerate(chips):
            copy(4 + j, (*chip, 1 - c), me).wait_recv()
        for cp in first + passed:
            cp.wait_send()
        mine.wait()

    any_spec = pl.BlockSpec(memory_space=pl.ANY)
    return pl.pallas_call(
        body, in_specs=[any_spec], out_specs=any_spec,
        out_shape=jax.ShapeDtypeStruct((N_DEV,) + shard.shape, shard.dtype),
        scratch_shapes=[pltpu.SemaphoreType.DMA((7,)), pltpu.SemaphoreType.DMA((7,)), pltpu.SemaphoreType.DMA],
        name="weight_gather",
    )(shard)


def _small_gather(block):
    def body(b_ref, out_ref, send_sems, recv_sems):
        x, y, c = _me()
        me = 4 * x + 2 * y + c
        out_ref[me] = b_ref[...]
        sends = [pltpu.make_async_remote_copy(b_ref, out_ref.at[me], send_sems.at[k - 1], recv_sems.at[k - 1],
                                              device_id=_peer(k), device_id_type=MESH) for k in range(1, N_DEV)]
        for cp in sends:
            cp.start()
        for k in range(1, N_DEV):
            pltpu.make_async_remote_copy(b_ref, out_ref.at[me ^ k], send_sems.at[k - 1], recv_sems.at[k - 1],
                                         device_id=_peer(k), device_id_type=MESH).wait_recv()
        for cp in sends:
            cp.wait_send()

    vm = pl.BlockSpec(memory_space=pltpu.VMEM)
    return pl.pallas_call(
        body, in_specs=[vm], out_specs=vm,
        out_shape=jax.ShapeDtypeStruct((N_DEV,) + block.shape, block.dtype),
        scratch_shapes=[pltpu.SemaphoreType.DMA((N_DEV - 1,)), pltpu.SemaphoreType.DMA((N_DEV - 1,))],
        name="small_gather",
    )(block)


def _pack_small_params(b_ada, n1, n2, fg, qn, kn, sink):
    qk = jnp.concatenate([qn.reshape(1, HD), kn.reshape(1, HD), jnp.zeros((1, D - 2 * HD), F32)], axis=1)
    sk = jnp.concatenate([sink.reshape(1, NH), jnp.zeros((1, D - NH), F32)], axis=1)
    return jnp.concatenate([b_ada.reshape(6, D), n1.reshape(1, D), n2.reshape(1, D), fg.reshape(1, D), qk, sk,
                            jnp.zeros((SMALL_ROWS - 11, D), F32)], axis=0)


def _unpack_small(p):
    return (p[0:6].reshape(1, 6 * D), p[6].reshape(1, D), p[9, 0:HD].reshape(1, HD), p[9, HD:2 * HD].reshape(1, HD),
            p[10, 0:NH].reshape(1, NH), p[7].reshape(1, D), p[8].reshape(D))


def kernel(x, c, w_ada, b_ada, norm1_g, w_in, q_norm_a, k_norm_a, sink_b, w_branch, w_out, norm2_g, w_mlp_in, w_mlp_out, final_g, loss_target, m_w_ada, m_b_ada, m_norm1_g, m_w_in, m_q_norm_a, m_k_norm_a, m_sink_b, m_w_branch, m_w_out, m_norm2_g, m_w_mlp_in, m_w_mlp_out, m_final_g, v_w_ada, v_b_ada, v_norm1_g, v_w_in, v_q_norm_a, v_k_norm_a, v_sink_b, v_w_branch, v_w_out, v_norm2_g, v_w_mlp_in, v_w_mlp_out, v_final_g):
    S = x.shape[1]
    xs = x.reshape(S, D)
    tgt = loss_target.reshape(S, D)
    ax, ay, ac = lax.axis_index("x"), lax.axis_index("y"), lax.axis_index("c")
    me = 4 * ax + 2 * ay + ac
    me1 = me.reshape(1).astype(jnp.int32)
    NW = w_ada.shape[2]
    NI = w_in.shape[2]

    silu64, mod64 = _ada_exchange(c.reshape(1, D), w_ada.reshape(D, NW),
                                  jnp.repeat(b_ada.reshape(N_DEV, NW), 8, axis=0))
    silu_all = silu64[0::8]
    modv = mod64[0::8].reshape(6, D)

    win = _weight_gather(w_in[0].astype(BF16)).transpose(1, 0, 2).reshape(D, INW)
    rest_shards = tuple(w[0].astype(BF16) for w in (w_branch, w_out, w_mlp_in, w_mlp_out))

    tab_a, tab_b = _rope_tables(S)
    qg2 = jnp.tile(q_norm_a.reshape(1, HD), (1, 2))
    kg2 = jnp.tile(k_norm_a.reshape(1, HD), (1, 2))
    n1g = norm1_g.reshape(1, D)
    n2g = norm2_g.reshape(1, D)
    fg = final_g.reshape(1, D)
    sink2 = sink_b.reshape(1, NH) * LOG2E

    h, qar, kar, qa, ka, va, qb, kb, vb, ga, gb, qa_t, ka_t3, va_t3 = _in_proj(xs, modv, n1g, win, qg2, kg2, tab_a, tab_b)
    ya, lse_at, wb, wout, wmi, wmo = _attn_a_fwd(qa_t, ka, va_t3, rest_shards)
    lse_a = lse_at.T
    wb = wb.transpose(1, 2, 0, 3).reshape(2, BW, D)
    wout = wout.reshape(D, D)
    pad = ((WIN, WIN), (0, 0))
    kbp, vbp = jnp.pad(kb, pad), jnp.pad(vb, pad)
    tb = min(TQ_B, S)
    yb, lse_bt = _attn_b_fwd(qb, kbp, vbp.reshape((S + 2 * WIN) // tb, tb, 128).transpose(0, 2, 1), sink2)
    lse_b = lse_bt.T
    x1, merged, ua, ub = _merge_out(ya, yb, ga, gb, xs, modv, wb, wout)
    h2, hp, dx2, stf = _mlp_fwd(x1, modv, n2g, wmi, wmo, fg, tgt)

    dhp, dx1, st2 = _mlp_bwd(dx2, x1, hp, modv, n2g, wmi, wmo)
    m2 = _tn_matmul(hp, dx2, 2048, D, "dw_mlp_out", relu_sq=True)
    g_wmo, dg2 = _scale_gate(m2, wmo.reshape(FF, D), modv, 5, "gate2_grad")
    g_wmi = _tn_matmul(h2, dhp, D, 512, "dw_mlp_in", dev_major=True, rows=2048)
    dua, dub, dga, dgb, dya, dyb, dl_a, dl_b, dya_t = _merge_bwd(dx1, modv, ga, gb, ua, ub, ya, yb, wb, wout)
    m1 = _tn_matmul(merged, dx1, D, D, "dw_out", rows=2048)
    g_wout, dg1 = _scale_gate(m1, wout, modv, 2, "gate1_grad")
    g_wb0 = _tn_matmul(ya, dua, BW, D, "dw_branch_a")
    g_wb1 = _tn_matmul(yb, dub, BW, D, "dw_branch_b")
    g_wb = jnp.stack([g_wb0, g_wb1]).reshape(2, BW, N_DEV, 128).transpose(2, 0, 1, 3).reshape(N_DEV, 2 * BW, 128)
    g_wout = g_wout.reshape(N_DEV, 128, D)
    g_wmo = g_wmo.reshape(N_DEV, 512, D)
    dqa_t, dka_t3, dva_t3, r_wb, r_wout, r_wmi, r_wmo = _attn_a_bwd(qa, qa_t, ka, ka_t3, va_t3, dya, dya_t, lse_a, dl_a,
                                                                    (g_wb, g_wout, g_wmi, g_wmo))
    dqb, dkb_t, dvb_t, dsink = _attn_b_bwd(qb, kbp, vbp, sink2, dyb, lse_b, dl_b)
    dkb = dkb_t.transpose(0, 2, 1).reshape(S + 2 * WIN, 128)[WIN:WIN + S]
    dvb = dvb_t.transpose(0, 2, 1).reshape(S + 2 * WIN, 128)[WIN:WIN + S]
    dproj, stqk = _qk_bwd(dqa_t, dka_t3, dva_t3, dqb, dkb, dvb, qar, kar, qg2, kg2, tab_a, tab_b, dga, dgb)
    g_win = _tn_matmul(h, dproj, D, 896, "dw_in", rows=2048)
    g_win = g_win.reshape(D, N_DEV, NI).transpose(1, 0, 2)
    grad_x, st1, r_win = _in_bwd(dproj, win, xs, dx1, modv, n1g, g_win.astype(BF16))

    def adam(name, own, recv, w, m, v):
        shape = w.shape
        w2, m2_, v2 = (a.reshape(own.shape[1:]) for a in (w, m, v))
        outs = _adamw_sum([(own, me1)] + [(recv, k) for k in range(N_DEV - 1)], w2, m2_, v2, name)
        return [a.reshape(shape) for a in outs]

    o_win = adam("adamw_w_in", g_win, r_win, w_in, m_w_in, v_w_in)
    o_wb = adam("adamw_w_branch", g_wb, r_wb, w_branch, m_w_branch, v_w_branch)
    o_wout = adam("adamw_w_out", g_wout, r_wout, w_out, m_w_out, v_w_out)
    o_wmi = adam("adamw_w_mlp_in", g_wmi, r_wmi, w_mlp_in, m_w_mlp_in, v_w_mlp_in)
    o_wmo = adam("adamw_w_mlp_out", g_wmo, r_wmo, w_mlp_out, m_w_mlp_out, v_w_mlp_out)

    small = _pack_small(st1, st2, stf, dg1, dg2, stqk, dsink, modv, n1g, n2g)
    small_all = _small_gather(small)
    sw = _pack_small_params(b_ada, norm1_g, norm2_g, final_g, q_norm_a, k_norm_a, sink_b)
    sm = _pack_small_params(m_b_ada, m_norm1_g, m_norm2_g, m_final_g, m_q_norm_a, m_k_norm_a, m_sink_b)
    sv = _pack_small_params(v_b_ada, v_norm1_g, v_norm2_g, v_final_g, v_q_norm_a, v_k_norm_a, v_sink_b)
    sm_out = _adamw_sum([(small_all, k) for k in range(N_DEV)], sw, sm, sv, "adamw_small")
    loss = sm_out[0][11, 0]
    sm_out = [_unpack_small(a) for a in sm_out]

    dmod_all = small_all[:, 0:6, :].reshape(N_DEV, 6 * D)
    dmod_cols = lax.dynamic_slice_in_dim(dmod_all, me * NW, NW, axis=1)
    g_wada = _wada_grad(silu_all, dmod_cols)
    ada = _adamw_sum([(g_wada, None)], w_ada.reshape(D, NW), m_w_ada.reshape(D, NW), v_w_ada.reshape(D, NW), "adamw_ada")
    ada = [a.reshape(1, D, NW) for a in ada]

    def leaves(k):
        b_, n1_, qn_, kn_, sk_, n2_, fg_ = sm_out[k]
        return [ada[k], b_, n1_, o_win[k], qn_, kn_, sk_, o_wb[k], o_wout[k], n2_, o_wmi[k], o_wmo[k], fg_]

    return (loss, grad_x.reshape(1, S, D), *leaves(0), *leaves(1), *leaves(2), *leaves(3))
```

```python
import jax
import jax.numpy as jnp
from jax import lax
from jax.experimental import pallas as pl
from jax.experimental.pallas import tpu as pltpu

F32, BF16 = jnp.float32, jnp.bfloat16
MESH = pl.DeviceIdType.MESH

D = 1024
HD = 64
NH = 8
GRP = 4
BW = 512
FF = 4096
INW = 3584
GRID_W = 64
WIN = 128
THETA = 10000.0
EPS = 1e-6
NEG = -1e30
N_DEV = 8
LOG2E = 1.4426950408889634
LN2 = 0.6931471805599453
QA_SCALE = 0.125 * LOG2E
SMALL_ROWS = 16
MLP_SHARDS = 4
MLP_ROWS = 512
TK_A = 512
TK_A_FWD = 2048
V7X_VMEM_LIMIT = 56 * 1024 * 1024

ADAM_LR, ADAM_B1, ADAM_B2, ADAM_EPS, ADAM_WD, ADAM_STEP = 0.001, 0.9, 0.999, 1e-08, 0.01, 10

NT = (((1,), (1,)), ((), ()))
TN = (((0,), (0,)), ((), ()))


def _params(n_axes, vmem=V7X_VMEM_LIMIT):
    return pltpu.CompilerParams(dimension_semantics=("arbitrary",) * n_axes, vmem_limit_bytes=vmem)


def _const(shape):
    return pl.BlockSpec(shape, lambda *_: (0,) * len(shape))


def _rows(tm, width):
    return pl.BlockSpec((tm, width), lambda i, *_: (i, 0))


def _seg_matrix(n, seg):
    r = lax.broadcasted_iota(jnp.int32, (n, n), 0) // seg
    c = lax.broadcasted_iota(jnp.int32, (n, n), 1) // seg
    return (r == c).astype(BF16)


def _seg_sum(z, seg_mat):
    hi = z.astype(BF16)
    lo = (z - hi.astype(F32)).astype(BF16)
    return jnp.dot(hi, seg_mat, preferred_element_type=F32) + jnp.dot(lo, seg_mat, preferred_element_type=F32)


def _rope(z, t_ref, sh):
    return z * t_ref[0] + pltpu.roll(z, sh, 1) * t_ref[1] + pltpu.roll(z, 128 - sh, 1) * t_ref[2]


def _rope_t(dz, t_ref, sh):
    return dz * t_ref[0] + pltpu.roll(dz * t_ref[1], 128 - sh, 1) + pltpu.roll(dz * t_ref[2], sh, 1)


def _rope_tables(S):
    t = jnp.arange(S, dtype=jnp.int32)[:, None]
    lane = jnp.arange(128, dtype=jnp.int32)[None, :] % HD

    def build(ang, first):
        cos, sin = jnp.cos(ang), jnp.sin(ang)
        return jnp.stack([cos, jnp.where(first, 0.0, sin), jnp.where(first, -sin, 0.0)]).astype(F32)

    inv_a = (THETA ** (-jnp.arange(0, HD // 2, 2, dtype=F32) / (HD // 2)))[lane % 16]
    pos_a = jnp.where(lane < HD // 2, t // GRID_W, t % GRID_W).astype(F32)
    tab_a = build(pos_a * inv_a, (lane % 32) < 16)
    inv_b = (THETA ** (-jnp.arange(0, HD, 2, dtype=F32) / HD))[lane % 32]
    tab_b = build(t.astype(F32) * inv_b, lane < 32)
    return tab_a, tab_b


def _in_proj(x, modv, n1g, win, qg2, kg2, tab_a, tab_b):
    S = x.shape[0]
    tm = min(512, S)
    tk = min(TK_A, S)
    per = tk // tm
    ts = min(256, tm)

    def body(x_ref, mod_ref, g_ref, w_ref, qg_ref, kg_ref, ta_ref, tb_ref,
             h_ref, qar_ref, kar_ref, qa_ref, ka_ref, va_ref, qb_ref, kb_ref, vb_ref, ga_ref, gb_ref, qat_ref, kat_ref, vat_ref):
        seg = _seg_matrix(128, HD)

        def head_norm(z, g):
            ms = _seg_sum(z * z, seg) * (1.0 / HD)
            return (z * lax.rsqrt(ms + EPS)) * g

        subs = [slice(ts * u, ts * u + ts) for u in range(tm // ts)]
        hbs = []
        for rows in subs:
            xt = x_ref[rows, :]
            r = lax.rsqrt(jnp.mean(xt * xt, axis=-1, keepdims=True) + EPS)
            h = ((xt * r) * g_ref[...]) * (1.0 + mod_ref[1:2, :]) + mod_ref[0:1, :]
            hbs.append(h.astype(BF16))
            h_ref[rows, :] = hbs[-1]
        projs = [jnp.dot(hb, w_ref[...], preferred_element_type=F32) for hb in hbs]
        for rows, proj in zip(subs, projs):
            ta, tb = ta_ref[:, rows, :], tb_ref[:, rows, :]
            for p in range(4):
                z = proj[:, 128 * p:128 * p + 128]
                qar_ref[rows, 128 * p:128 * p + 128] = z.astype(BF16)
                qv = _rope(head_norm(z, qg_ref[...]), ta, 16) * QA_SCALE
                qa_ref[rows, 128 * p:128 * p + 128] = qv.astype(BF16)
                qat_ref[128 * p:128 * p + 128, rows] = qv.T.astype(BF16)
                zb = proj[:, 768 + 128 * p:768 + 128 * p + 128]
                qb_ref[rows, 128 * p:128 * p + 128] = (_rope(zb, tb, 32) * QA_SCALE).astype(BF16)
            z = proj[:, 512:640]
            kar_ref[rows, :] = z.astype(BF16)
            kv_ = _rope(head_norm(z, kg_ref[...]), ta, 16)
            ka_ref[rows, :] = kv_.astype(BF16)
            kat_ref[:, rows] = kv_.T.astype(BF16)
            va_ref[rows, :] = proj[:, 640:768].astype(BF16)
            vat_ref[:, rows] = proj[:, 640:768].T.astype(BF16)
            kb_ref[rows, :] = _rope(proj[:, 1280:1408], tb, 32).astype(BF16)
            vb_ref[rows, :] = proj[:, 1408:1536].astype(BF16)
            ga_ref[rows, :] = proj[:, 1536:2560].astype(BF16)
            gb_ref[rows, :] = proj[:, 2560:3584].astype(BF16)

    tab = pl.BlockSpec((3, tm, 128), lambda i: (0, i, 0))
    shapes = [(D, BF16), (BW, BF16), (128, BF16), (BW, BF16), (128, BF16), (128, BF16),
              (BW, BF16), (128, BF16), (128, BF16), (D, BF16), (D, BF16)]
    return pl.pallas_call(
        body, grid=(S // tm,),
        in_specs=[_rows(tm, D), _const((6, D)), _const((1, D)), _const((D, INW)), _const((1, 128)), _const((1, 128)), tab, tab],
        out_specs=[_rows(tm, w) for w, _ in shapes] + [pl.BlockSpec((BW, tm), lambda i: (0, i))]
        + [pl.BlockSpec((None, 128, tm), lambda i: (i // per, 0, i % per))] * 2,
        out_shape=[jax.ShapeDtypeStruct((S, w), dt) for w, dt in shapes] + [jax.ShapeDtypeStruct((BW, S), BF16)]
        + [jax.ShapeDtypeStruct((S // tk, 128, tk), BF16)] * 2,
        compiler_params=_params(1), name="in_proj",
    )(x, modv, n1g, win, qg2, kg2, tab_a, tab_b)


def _exchange_gather(block_refs, out_refs, send_sems, recv_sems, local_sems):
    x, y, c = _me()
    me = 4 * x + 2 * y + c

    def copies():
        own, out, arrive = [], [], []
        for a, (blk, dst) in enumerate(zip(block_refs, out_refs)):
            own.append(pltpu.make_async_copy(blk, dst.at[me], local_sems.at[a]))
            for k in range(1, N_DEV):
                sems = dict(send_sem=send_sems.at[a, k - 1], recv_sem=recv_sems.at[a, k - 1], device_id=_peer(k), device_id_type=MESH)
                out.append(pltpu.make_async_remote_copy(blk, dst.at[me], **sems))
                arrive.append(pltpu.make_async_remote_copy(blk, dst.at[me ^ k], **sems))
        return own, out, arrive

    def start():
        own, out, _ = copies()
        for cp in own + out:
            cp.start()

    def finish():
        own, out, arrive = copies()
        for cp in arrive:
            cp.wait_recv()
        for cp in out:
            cp.wait_send()
        for cp in own:
            cp.wait()

    return start, finish


def _exchange_scatter(chunk_refs, recv_refs, send_sems, recv_sems):
    x, y, c = _me()
    me = 4 * x + 2 * y + c

    def copies():
        return [pltpu.make_async_remote_copy(src.at[me ^ k], dst.at[k - 1], send_sems.at[a, k - 1], recv_sems.at[a, k - 1],
                                             device_id=_peer(k), device_id_type=MESH)
                for a, (src, dst) in enumerate(zip(chunk_refs, recv_refs)) for k in range(1, N_DEV)]

    def start():
        for cp in copies():
            cp.start()

    def finish():
        cps = copies()
        for cp in cps:
            cp.wait_recv()
        for cp in cps:
            cp.wait_send()

    return start, finish


def _exchange_sems(n):
    return [pltpu.SemaphoreType.DMA((n, N_DEV - 1)), pltpu.SemaphoreType.DMA((n, N_DEV - 1))]


def _attn_a_fwd(qt, k, vt3, shards):
    S = qt.shape[1]
    tq = min(512, S)
    nq = S // tq
    per = max(1, min(TK_A_FWD, S) // vt3.shape[2])
    tk = per * vt3.shape[2]
    nk = S // tk
    ONES = 16
    AHEAD = 2
    HALVES = 2 if tq >= 512 else 1
    tw = tq // HALVES
    NCH = GRP * HALVES
    ns = len(shards)

    def body(q_ref, k_ref, vt_ref, *rest):
        w_hbm, (o_ref, lse_ref), wall_hbm = rest[:ns], rest[ns:ns + 2], rest[ns + 2:2 * ns + 2]
        st_sc, send_sems, recv_sems, local_sems = rest[2 * ns + 2:]
        start, finish = _exchange_gather(w_hbm, wall_hbm, send_sems, recv_sems, local_sems)
        pl.when(pl.program_id(0) == 0)(start)
        row8 = lax.broadcasted_iota(jnp.int32, (NH, tq), 0)
        lse_all = jnp.zeros((NH, tq), F32)
        ones = jnp.ones((ONES, tk), BF16)
        for kv in range(2):
            qts = [q_ref[HD * (GRP * kv + t):HD * (GRP * kv + t) + HD, tw * u:tw * u + tw]
                   for t in range(GRP) for u in range(HALVES)]

            def keys(j, kv=kv):
                off = j * tk if isinstance(j, int) else pl.multiple_of(j * tk, tk)
                return k_ref[pl.ds(off, tk), :][:, HD * kv:HD * kv + HD]

            def scores(kj, t, qts=qts):
                return jnp.dot(kj, qts[t], preferred_element_type=F32)

            def step(j, carry, kv=kv, last=False):
                kj = keys(j)
                kn = None if last else keys(j + 1)
                vt = jnp.concatenate([vt_ref[per * j + u, HD * kv:HD * kv + HD, :] for u in range(per)], axis=1)
                v1 = jnp.concatenate([vt, ones], axis=0)
                sts = [st_sc[t] for t in range(AHEAD)]
                new = []
                for t in range(NCH):
                    m, acc = carry[2 * t], carry[2 * t + 1]
                    if t + AHEAD < NCH:
                        sts.append(scores(kj, t + AHEAD))
                    st = sts[t]
                    mn = jnp.maximum(m, jnp.max(st, axis=0, keepdims=True))
                    pt = jnp.exp2(st - mn)
                    if t + AHEAD >= NCH and not last:
                        st_sc[t + AHEAD - NCH] = scores(kn, t + AHEAD - NCH)
                    acc = jnp.exp2(m - mn) * acc + jnp.dot(v1, pt.astype(BF16), preferred_element_type=F32)
                    new += [mn, acc]
                return tuple(new)

            k0 = keys(0)
            for t in range(AHEAD):
                st_sc[t] = scores(k0, t)
            init = (jnp.full((1, tw), NEG, F32), jnp.zeros((HD + ONES, tw), F32)) * NCH
            res = step(nk - 1, lax.fori_loop(0, nk - 1, step, init), last=True)
            outs = []
            for t in range(GRP):
                o_parts, lse_parts = [], []
                for u in range(HALVES):
                    m, acc = res[2 * (HALVES * t + u)], res[2 * (HALVES * t + u) + 1]
                    l = acc[HD:HD + 1, :]
                    o_parts.append((acc[:HD, :] / l).T)
                    lse_parts.append(m + jnp.log2(l))
                outs.append(jnp.concatenate(o_parts, axis=0))
                lse_all = jnp.where(row8 == GRP * kv + t, jnp.concatenate(lse_parts, axis=1), lse_all)
            o_ref[:, 256 * kv:256 * kv + 256] = jnp.concatenate(outs, axis=1).astype(BF16)
        lse_ref[...] = lse_all
        pl.when(pl.program_id(0) == nq - 1)(finish)

    any_spec = pl.BlockSpec(memory_space=pl.ANY)
    return pl.pallas_call(
        body, grid=(nq,),
        in_specs=[pl.BlockSpec((BW, tq), lambda i: (0, i)), _const((S, 128)), _const(vt3.shape)] + [any_spec] * ns,
        out_specs=[_rows(tq, BW), pl.BlockSpec((NH, tq), lambda i: (0, i))] + [any_spec] * ns,
        out_shape=[jax.ShapeDtypeStruct((S, BW), BF16), jax.ShapeDtypeStruct((NH, S), F32)]
        + [jax.ShapeDtypeStruct((N_DEV,) + s.shape, s.dtype) for s in shards],
        scratch_shapes=[pltpu.VMEM((AHEAD, tk, tw), F32)] + _exchange_sems(ns) + [pltpu.SemaphoreType.DMA((ns,))],
        compiler_params=_params(1), name="attn_a_fwd",
    )(qt, k, vt3, *shards)


def _window_mask(i, tq, S):
    W = tq + 2 * WIN
    r = lax.broadcasted_iota(jnp.int32, (tq, W), 0)
    c = lax.broadcasted_iota(jnp.int32, (tq, W), 1)
    kpos = i * tq - WIN + c
    return (jnp.abs(c - WIN - r) <= WIN) & (kpos >= 0) & (kpos < S)


TQ_B = 256


def _attn_b_fwd(q, kp, vpt3, sink2):
    S = q.shape[0]
    tq = min(TQ_B, S)
    W = tq + 2 * WIN
    nc = vpt3.shape[0]
    ONES = 16

    def body(q_ref, k_ref, vt_ref, sink_ref, o_ref, lse_ref):
        i = pl.program_id(0)
        off = pl.multiple_of(i * tq, tq)
        r = lax.broadcasted_iota(jnp.int32, (W, tq), 1)
        c = lax.broadcasted_iota(jnp.int32, (W, tq), 0)
        kpos = i * tq - WIN + c
        valid = (jnp.abs(c - WIN - r) <= WIN) & (kpos >= 0) & (kpos < S)
        kw = k_ref[pl.ds(off, W), :]
        vt = jnp.concatenate([vt_ref[i + half] for half in range(W // tq)], axis=1)
        ones = jnp.ones((ONES, W), BF16)
        row8 = lax.broadcasted_iota(jnp.int32, (NH, tq), 0)
        lse_all = jnp.zeros((NH, tq), F32)
        qs = []
        for p in range(4):
            qp = q_ref[:, 128 * p:128 * p + 128]
            qs += [qp[:, :HD], qp[:, HD:]]
        khs = [kw[:, HD * kv:HD * kv + HD] for kv in range(2)]
        v1s = [jnp.concatenate([vt[HD * kv:HD * kv + HD, :], ones], axis=0) for kv in range(2)]

        def scores(h):
            return lax.dot_general(khs[h // GRP], qs[h], NT, preferred_element_type=F32)

        ss = [scores(0), scores(1)]
        outs = []
        for h in range(NH):
            if h + 2 < NH:
                ss.append(scores(h + 2))
            st = jnp.where(valid, ss[h], NEG)
            sk = sink_ref[:, h:h + 1]
            m = jnp.maximum(jnp.max(st, axis=0, keepdims=True), sk)
            acc = jnp.dot(v1s[h // GRP], jnp.exp2(st - m).astype(BF16), preferred_element_type=F32)
            l = acc[HD:HD + 1, :] + jnp.exp2(sk - m)
            outs.append((acc[:HD, :] / l).T)
            lse_all = jnp.where(row8 == h, m + jnp.log2(l), lse_all)
        for p in range(4):
            o_ref[:, 128 * p:128 * p + 128] = jnp.concatenate(outs[2 * p:2 * p + 2], axis=1).astype(BF16)
        lse_ref[...] = lse_all

    return pl.pallas_call(
        body, grid=(S // tq,),
        in_specs=[_rows(tq, BW), _const((S + 2 * WIN, 128)), _const((nc, 128, tq)), _const((1, NH))],
        out_specs=[_rows(tq, BW), pl.BlockSpec((NH, tq), lambda i: (0, i))],
        out_shape=[jax.ShapeDtypeStruct((S, BW), BF16), jax.ShapeDtypeStruct((NH, S), F32)],
        compiler_params=_params(1), name="attn_b_fwd",
    )(q, kp, vpt3, sink2)


def _merge_out(ya, yb, ga, gb, x, modv, wb, wout):
    S = x.shape[0]
    tm = min(256, S)

    def body(ya_ref, yb_ref, ga_ref, gb_ref, x_ref, mod_ref, wb_ref, wo_ref, x1_ref, mg_ref, ua_ref, ub_ref):
        ua = jnp.dot(ya_ref[...], wb_ref[0], preferred_element_type=F32)
        ub = jnp.dot(yb_ref[...], wb_ref[1], preferred_element_type=F32)
        merged = jax.nn.sigmoid(ga_ref[...].astype(F32)) * ua + jax.nn.sigmoid(gb_ref[...].astype(F32)) * ub
        mb = merged.astype(BF16)
        ua_ref[...] = ua.astype(BF16)
        ub_ref[...] = ub.astype(BF16)
        mg_ref[...] = mb
        x1_ref[...] = x_ref[...] + mod_ref[2:3, :] * jnp.dot(mb, wo_ref[...], preferred_element_type=F32)

    return pl.pallas_call(
        body, grid=(S // tm,),
        in_specs=[_rows(tm, BW), _rows(tm, BW), _rows(tm, D), _rows(tm, D), _rows(tm, D), _const((6, D)),
                  _const((2, BW, D)), _const((D, D))],
        out_specs=[_rows(tm, D)] * 4,
        out_shape=[jax.ShapeDtypeStruct((S, D), F32)] + [jax.ShapeDtypeStruct((S, D), BF16)] * 3,
        compiler_params=_params(1), name="merge_out",
    )(ya, yb, ga, gb, x, modv, wb, wout)


def _mlp_fwd(x1, modv, n2g, wmi, wmo, fg, target):
    S = x1.shape[0]
    tm = min(MLP_ROWS, S)
    tf = wmi.shape[2]
    nj = wmi.shape[0] // MLP_SHARDS

    def body(x1_ref, mod_ref, g_ref, wi_ref, wo_ref, fg_ref, t_ref, h2_ref, hp_ref, dx2_ref, st_ref, acc_ref):
        i, j = pl.program_id(0), pl.program_id(1)

        @pl.when(j == 0)
        def _():
            xt = x1_ref[...]
            r = lax.rsqrt(jnp.mean(xt * xt, axis=-1, keepdims=True) + EPS)
            h2 = ((xt * r) * g_ref[...]) * (1.0 + mod_ref[4:5, :]) + mod_ref[3:4, :]
            h2_ref[...] = h2.astype(BF16)
            acc_ref[...] = jnp.zeros_like(acc_ref)

        @pl.when((i == 0) & (j == 0))
        def _():
            st_ref[...] = jnp.zeros_like(st_ref)

        out = None
        for u in range(MLP_SHARDS):
            hp = jnp.dot(h2_ref[...], wi_ref[u], preferred_element_type=F32)
            hp_ref[:, tf * u:tf * u + tf] = hp.astype(BF16)
            hid = jnp.square(jnp.maximum(hp, 0.0))
            part = jnp.dot(hid.astype(BF16), wo_ref[u], preferred_element_type=F32)
            out = part if out is None else out + part
        acc_ref[...] += out

        @pl.when(j == nj - 1)
        def _():
            x2 = x1_ref[...] + mod_ref[5:6, :] * acc_ref[...]
            r3 = lax.rsqrt(jnp.mean(x2 * x2, axis=-1, keepdims=True) + EPS)
            xn = x2 * r3
            err = xn * fg_ref[...] - t_ref[...]
            dy = err * (1.0 / D)
            gy = dy * fg_ref[...]
            dx2_ref[...] = r3 * (gy - xn * jnp.mean(gy * xn, axis=-1, keepdims=True))
            st_ref[0:1, :] += jnp.sum(dy * xn, axis=0, keepdims=True)
            st_ref[1:2, :] += jnp.sum(err * err, axis=0, keepdims=True) * (0.5 / D)

    return pl.pallas_call(
        body, grid=(S // tm, nj),
        in_specs=[pl.BlockSpec((tm, D), lambda i, j: (i, 0)), _const((6, D)), _const((1, D)),
                  pl.BlockSpec((MLP_SHARDS, D, tf), lambda i, j: (j, 0, 0)), pl.BlockSpec((MLP_SHARDS, tf, D), lambda i, j: (j, 0, 0)),
                  _const((1, D)), pl.BlockSpec((tm, D), lambda i, j: (i, 0))],
        out_specs=[pl.BlockSpec((tm, D), lambda i, j: (i, 0)), pl.BlockSpec((tm, MLP_SHARDS * tf), lambda i, j: (i, j)),
                   pl.BlockSpec((tm, D), lambda i, j: (i, 0)), _const((8, D))],
        out_shape=[jax.ShapeDtypeStruct((S, D), BF16), jax.ShapeDtypeStruct((S, wmi.shape[0] * tf), BF16),
                   jax.ShapeDtypeStruct((S, D), F32), jax.ShapeDtypeStruct((8, D), F32)],
        scratch_shapes=[pltpu.VMEM((tm, D), F32)],
        compiler_params=_params(2), name="mlp_fwd",
    )(x1, modv, n2g, wmi, wmo, fg, target)


def _mlp_bwd(dx2, x1, hp, modv, n2g, wmi, wmo):
    S = x1.shape[0]
    tm = min(MLP_ROWS, S)
    tf = wmi.shape[2]
    nj = wmi.shape[0] // MLP_SHARDS

    def body(dx2_ref, x1_ref, hp_ref, mod_ref, g_ref, wi_ref, wo_ref, dhp_ref, dx1_ref, st_ref, dmo_ref, acc_ref):
        i, j = pl.program_id(0), pl.program_id(1)

        @pl.when(j == 0)
        def _():
            dmo_ref[...] = (mod_ref[5:6, :] * dx2_ref[...]).astype(BF16)
            acc_ref[...] = jnp.zeros_like(acc_ref)

        @pl.when((i == 0) & (j == 0))
        def _():
            st_ref[...] = jnp.zeros_like(st_ref)

        out = None
        for u in range(MLP_SHARDS):
            sl = slice(tf * u, tf * u + tf)
            dhid = lax.dot_general(dmo_ref[...], wo_ref[u], NT, preferred_element_type=F32)
            dhp = (dhid * (2.0 * jnp.maximum(hp_ref[:, sl].astype(F32), 0.0))).astype(BF16)
            dhp_ref[:, sl] = dhp
            part = lax.dot_general(dhp, wi_ref[u], NT, preferred_element_type=F32)
            out = part if out is None else out + part
        acc_ref[...] += out

        @pl.when(j == nj - 1)
        def _():
            dh2 = acc_ref[...]
            xt = x1_ref[...]
            r = lax.rsqrt(jnp.mean(xt * xt, axis=-1, keepdims=True) + EPS)
            xn = xt * r
            st_ref[0:1, :] += jnp.sum(dh2, axis=0, keepdims=True)
            st_ref[1:2, :] += jnp.sum(dh2 * xn, axis=0, keepdims=True)
            dxn = dh2 * (g_ref[...] * (1.0 + mod_ref[4:5, :]))
            dx1_ref[...] = dx2_ref[...] + r * (dxn - xn * jnp.mean(dxn * xn, axis=-1, keepdims=True))

    return pl.pallas_call(
        body, grid=(S // tm, nj),
        in_specs=[pl.BlockSpec((tm, D), lambda i, j: (i, 0)), pl.BlockSpec((tm, D), lambda i, j: (i, 0)),
                  pl.BlockSpec((tm, MLP_SHARDS * tf), lambda i, j: (i, j)), _const((6, D)), _const((1, D)),
                  pl.BlockSpec((MLP_SHARDS, D, tf), lambda i, j: (j, 0, 0)), pl.BlockSpec((MLP_SHARDS, tf, D), lambda i, j: (j, 0, 0))],
        out_specs=[pl.BlockSpec((tm, MLP_SHARDS * tf), lambda i, j: (i, j)), pl.BlockSpec((tm, D), lambda i, j: (i, 0)), _const((8, D))],
        out_shape=[jax.ShapeDtypeStruct((S, wmi.shape[0] * tf), BF16), jax.ShapeDtypeStruct((S, D), F32),
                   jax.ShapeDtypeStruct((8, D), F32)],
        scratch_shapes=[pltpu.VMEM((tm, D), BF16), pltpu.VMEM((tm, D), F32)],
        compiler_params=_params(2), name="mlp_bwd",
    )(dx2, x1, hp, modv, n2g, wmi, wmo)


def _tn_matmul(a, b, tk, tn, name, relu_sq=False, dev_major=False, rows=1024):
    S, K = a.shape
    N = b.shape[1]
    ts = min(rows, S)
    ns = S // ts

    def body(a_ref, b_ref, o_ref):
        @pl.when(pl.program_id(2) == 0)
        def _():
            o_ref[...] = jnp.zeros_like(o_ref)

        at = a_ref[...]
        if relu_sq:
            at = jnp.square(jnp.maximum(at.astype(F32), 0.0)).astype(BF16)
        o_ref[...] += lax.dot_general(at, b_ref[...].astype(BF16), TN, preferred_element_type=F32)

    if dev_major:
        out_spec = pl.BlockSpec((None, tk, tn), lambda k, n, s: (n, k, 0))
        out_shape = jax.ShapeDtypeStruct((N // tn, K, tn), F32)
    else:
        out_spec = pl.BlockSpec((tk, tn), lambda k, n, s: (k, n))
        out_shape = jax.ShapeDtypeStruct((K, N), F32)
    return pl.pallas_call(
        body, grid=(K // tk, N // tn, ns),
        in_specs=[pl.BlockSpec((ts, tk), lambda k, n, s: (s, k)), pl.BlockSpec((ts, tn), lambda k, n, s: (s, n))],
        out_specs=out_spec, out_shape=out_shape,
        compiler_params=_params(3), name=name,
    )(a, b)


def _scale_gate(m, w, g, row, name):
    K = m.shape[0]
    tk = min(512, K)

    def body(m_ref, w_ref, mod_ref, dw_ref, dg_ref):
        @pl.when(pl.program_id(0) == 0)
        def _():
            dg_ref[...] = jnp.zeros_like(dg_ref)

        mt = m_ref[...]
        dw_ref[...] = mt * mod_ref[row:row + 1, :]
        dg_ref[0:1, :] += jnp.sum(mt * w_ref[...].astype(F32), axis=0, keepdims=True)

    return pl.pallas_call(
        body, grid=(K // tk,),
        in_specs=[_rows(tk, D), _rows(tk, D), _const((6, D))],
        out_specs=[_rows(tk, D), _const((8, D))],
        out_shape=[jax.ShapeDtypeStruct((K, D), F32), jax.ShapeDtypeStruct((8, D), F32)],
        compiler_params=_params(1), name=name,
    )(m, w, g)


def _merge_bwd(dx1, modv, ga, gb, ua, ub, ya, yb, wb, wout):
    S = dx1.shape[0]
    tm = min(512, S)

    def body(dx1_ref, mod_ref, ga_ref, gb_ref, ua_ref, ub_ref, ya_ref, yb_ref, wb_ref, wo_ref,
             dua_ref, dub_ref, dga_ref, dgb_ref, dya_ref, dyb_ref, dla_ref, dlb_ref, dyat_ref):
        dao = (mod_ref[2:3, :] * dx1_ref[...]).astype(BF16)
        dm = lax.dot_general(dao, wo_ref[...], NT, preferred_element_type=F32)
        r = lax.broadcasted_iota(jnp.int32, (BW, NH), 0) // HD
        c = lax.broadcasted_iota(jnp.int32, (BW, NH), 1)
        head_of = (r == c).astype(BF16)
        for br, (g_ref, u_ref, y_ref, du_ref, dg_ref, dy_ref, dl_ref) in enumerate((
                (ga_ref, ua_ref, ya_ref, dua_ref, dga_ref, dya_ref, dla_ref),
                (gb_ref, ub_ref, yb_ref, dub_ref, dgb_ref, dyb_ref, dlb_ref))):
            sg = jax.nn.sigmoid(g_ref[...].astype(F32))
            du = (dm * sg).astype(BF16)
            du_ref[...] = du
            dg_ref[...] = (dm * u_ref[...].astype(F32) * sg * (1.0 - sg)).astype(BF16)
            dy = lax.dot_general(du, wb_ref[br], NT, preferred_element_type=F32)
            dyb16 = dy.astype(BF16)
            dy_ref[...] = dyb16
            if br == 0:
                dyat_ref[...] = dy.T.astype(BF16)
            prod = dyb16.astype(F32) * y_ref[...].astype(F32)
            hi = prod.astype(BF16)
            lo = (prod - hi.astype(F32)).astype(BF16)
            dl_ref[...] = (jnp.dot(hi, head_of, preferred_element_type=F32) + jnp.dot(lo, head_of, preferred_element_type=F32))

    return pl.pallas_call(
        body, grid=(S // tm,),
        in_specs=[_rows(tm, D), _const((6, D)), _rows(tm, D), _rows(tm, D), _rows(tm, D), _rows(tm, D),
                  _rows(tm, BW), _rows(tm, BW), _const((2, BW, D)), _const((D, D))],
        out_specs=[_rows(tm, D)] * 4 + [_rows(tm, BW)] * 2 + [_rows(tm, NH)] * 2 + [pl.BlockSpec((BW, tm), lambda i: (0, i))],
        out_shape=[jax.ShapeDtypeStruct((S, D), BF16)] * 4 + [jax.ShapeDtypeStruct((S, BW), BF16)] * 2
        + [jax.ShapeDtypeStruct((S, NH), F32)] * 2 + [jax.ShapeDtypeStruct((BW, S), BF16)],
        compiler_params=_params(1), name="merge_bwd",
    )(dx1, modv, ga, gb, ua, ub, ya, yb, wb, wout)


def _attn_a_bwd(q, qt, kt3, vt3, do, dot_, lse, delta, chunks):
    S = q.shape[0]
    tq = min(1024, S)
    nk, _, tk = kt3.shape
    nq = S // tq
    nc = len(chunks)

    def body(q_ref, qt_ref, do_ref, dot_ref, lse_ref, dl_ref, kt_ref, vt_ref, *rest):
        g_hbm, (dq_ref, dk_hbm, dv_hbm), recv_hbm = rest[:nc], rest[nc:nc + 3], rest[nc + 3:2 * nc + 3]
        dk_sc, dv_sc, sem, send_sems, recv_sems = rest[2 * nc + 3:]
        i = pl.program_id(0)
        start, finish = _exchange_scatter(g_hbm, recv_hbm, send_sems, recv_sems)

        @pl.when(i == 0)
        def _():
            start()
            dk_sc[...] = jnp.zeros_like(dk_sc)
            dv_sc[...] = jnp.zeros_like(dv_sc)

        for kv in range(2):
            qg = q_ref[:, 256 * kv:256 * kv + 256]
            dog = do_ref[:, 256 * kv:256 * kv + 256]
            heads = []
            for t in range(GRP):
                h = GRP * kv + t
                heads.append((qg[:, HD * t:HD * t + HD], dog[:, HD * t:HD * t + HD],
                              qt_ref[HD * h:HD * h + HD, :], dot_ref[HD * h:HD * h + HD, :],
                              lse_ref[:, h:h + 1], dl_ref[:, h:h + 1]))

            q2 = [jnp.concatenate([heads[2 * u][0], heads[2 * u + 1][0]], axis=0) for u in range(GRP // 2)]
            do2 = [jnp.concatenate([heads[2 * u][1], heads[2 * u + 1][1]], axis=0) for u in range(GRP // 2)]

            def step(j, carry, kv=kv, heads=heads, q2=q2, do2=do2):
                kjt = kt_ref[j, HD * kv:HD * kv + HD, :]
                vjt = vt_ref[j, HD * kv:HD * kv + HD, :]
                dkt = jnp.zeros((HD, tk), F32)
                dvt = jnp.zeros((HD, tk), F32)
                new = []

                def logits(u):
                    return (jnp.dot(q2[u], kjt, preferred_element_type=F32), jnp.dot(do2[u], vjt, preferred_element_type=F32))

                sd = [logits(0)]
                for t, (qh, doh, qth, doth, lse_h, dl_h) in enumerate(heads):
                    if t == 0:
                        sd.append(logits(1))
                    rows = slice(tq * (t % 2), tq * (t % 2) + tq)
                    s, dp = sd[t // 2][0][rows, :], sd[t // 2][1][rows, :]
                    pm = jnp.exp2(s - lse_h)
                    ds = (pm * (dp - dl_h)).astype(BF16)
                    dvt = dvt + jnp.dot(doth, pm.astype(BF16), preferred_element_type=F32)
                    dkt = dkt + jnp.dot(qth, ds, preferred_element_type=F32)
                    new.append(carry[t] + lax.dot_general(kjt, ds, NT, preferred_element_type=F32))
                dk_sc[j, HD * kv:HD * kv + HD, :] += dkt
                dv_sc[j, HD * kv:HD * kv + HD, :] += dvt
                return tuple(new)

            res = lax.fori_loop(0, nk, step, (jnp.zeros((HD, tq), F32),) * GRP)
            for t in range(GRP):
                dq_ref[HD * (GRP * kv + t):HD * (GRP * kv + t) + HD, :] = res[t]

        @pl.when(i == nq - 1)
        def _():
            c1 = pltpu.make_async_copy(dk_sc, dk_hbm, sem.at[0])
            c2 = pltpu.make_async_copy(dv_sc, dv_hbm, sem.at[1])
            c1.start()
            c2.start()
            c1.wait()
            c2.wait()
            finish()

    any_spec = pl.BlockSpec(memory_space=pl.ANY)
    cols = pl.BlockSpec((BW, tq), lambda i: (0, i))
    return pl.pallas_call(
        body, grid=(nq,),
        in_specs=[_rows(tq, BW), cols, _rows(tq, BW), cols, _rows(tq, NH), _rows(tq, NH), _const((nk, 128, tk)),
                  _const((nk, 128, tk))] + [any_spec] * nc,
        out_specs=[cols, any_spec, any_spec] + [any_spec] * nc,
        out_shape=[jax.ShapeDtypeStruct((BW, S), F32), jax.ShapeDtypeStruct((nk, 128, tk), F32),
                   jax.ShapeDtypeStruct((nk, 128, tk), F32)]
        + [jax.ShapeDtypeStruct((N_DEV - 1,) + c.shape[1:], c.dtype) for c in chunks],
        scratch_shapes=[pltpu.VMEM((nk, 128, tk), F32), pltpu.VMEM((nk, 128, tk), F32), pltpu.SemaphoreType.DMA((2,))]
        + _exchange_sems(nc),
        compiler_params=_params(1), name="attn_a_bwd",
    )(q, qt, do, dot_, lse, delta, kt3, vt3, *chunks)


def _attn_b_bwd(q, kp, vp, sink2, do, lse, delta):
    S = q.shape[0]
    tq = min(TQ_B, S)
    W = tq + 2 * WIN
    nq = S // tq
    nc = (S + 2 * WIN) // tq

    def body(q_ref, k_ref, v_ref, sink_ref, do_ref, lse_ref, dl_ref, dq_ref, dk_hbm, dv_hbm, ds_ref, dk_sc, dv_sc, sem):
        i = pl.program_id(0)

        @pl.when(i == 0)
        def _():
            dk_sc[...] = jnp.zeros_like(dk_sc)
            dv_sc[...] = jnp.zeros_like(dv_sc)
            ds_ref[...] = jnp.zeros_like(ds_ref)

        off = pl.multiple_of(i * tq, tq)
        valid = _window_mask(i, tq, S)
        kw = k_ref[pl.ds(off, W), :]
        vw = v_ref[pl.ds(off, W), :]
        lse_i = lse_ref[...]
        dl_i = dl_ref[...]
        qa = q_ref[...]
        doa = do_ref[...]
        qt = qa.astype(F32).T.astype(BF16)
        dot_ = doa.astype(F32).T.astype(BF16)
        khs = [kw[:, HD * kv:HD * kv + HD] for kv in range(2)]
        vhs = [vw[:, HD * kv:HD * kv + HD] for kv in range(2)]

        def logits(h):
            return (lax.dot_general(qa[:, HD * h:HD * h + HD], khs[h // GRP], NT, preferred_element_type=F32),
                    lax.dot_general(doa[:, HD * h:HD * h + HD], vhs[h // GRP], NT, preferred_element_type=F32))

        sd = [logits(0)]
        dqs = []
        dkt = [jnp.zeros((HD, W), F32), jnp.zeros((HD, W), F32)]
        dvt = [jnp.zeros((HD, W), F32), jnp.zeros((HD, W), F32)]
        for h in range(NH):
            kv = h // GRP
            if h + 1 < NH:
                sd.append(logits(h + 1))
            s, dp = sd[h]
            pm = jnp.exp2(jnp.where(valid, s, NEG) - lse_i[:, h:h + 1])
            ds = (pm * (dp - dl_i[:, h:h + 1])).astype(BF16)
            dvt[kv] = dvt[kv] + jnp.dot(dot_[HD * h:HD * h + HD, :], pm.astype(BF16), preferred_element_type=F32)
            dkt[kv] = dkt[kv] + jnp.dot(qt[HD * h:HD * h + HD, :], ds, preferred_element_type=F32)
            dqs.append(jnp.dot(ds, khs[kv], preferred_element_type=F32))
        for p in range(4):
            dq_ref[:, 128 * p:128 * p + 128] = jnp.concatenate(dqs[2 * p:2 * p + 2], axis=1)
        for half in range(W // tq):
            dk_sc[i + half] += jnp.concatenate([d[:, tq * half:tq * half + tq] for d in dkt], axis=0)
            dv_sc[i + half] += jnp.concatenate([d[:, tq * half:tq * half + tq] for d in dvt], axis=0)
        psd = jnp.exp2(sink_ref[...] - lse_i) * dl_i
        r = lax.broadcasted_iota(jnp.int32, (NH, 128), 0)
        c = lax.broadcasted_iota(jnp.int32, (NH, 128), 1)
        row = jnp.dot(jnp.sum(psd, axis=0, keepdims=True), (r == c).astype(F32),
                      preferred_element_type=F32, precision=lax.Precision.HIGHEST)
        ds_ref[...] -= jnp.broadcast_to(row, (8, 128))

        @pl.when(i == nq - 1)
        def _():
            c1 = pltpu.make_async_copy(dk_sc, dk_hbm, sem.at[0])
            c2 = pltpu.make_async_copy(dv_sc, dv_hbm, sem.at[1])
            c1.start()
            c2.start()
            c1.wait()
            c2.wait()

    any_spec = pl.BlockSpec(memory_space=pl.ANY)
    return pl.pallas_call(
        body, grid=(nq,),
        in_specs=[_rows(tq, BW), _const((S + 2 * WIN, 128)), _const((S + 2 * WIN, 128)), _const((1, NH)),
                  _rows(tq, BW), _rows(tq, NH), _rows(tq, NH)],
        out_specs=[_rows(tq, BW), any_spec, any_spec, _const((8, 128))],
        out_shape=[jax.ShapeDtypeStruct((S, BW), F32), jax.ShapeDtypeStruct((nc, 128, tq), F32),
                   jax.ShapeDtypeStruct((nc, 128, tq), F32), jax.ShapeDtypeStruct((8, 128), F32)],
        scratch_shapes=[pltpu.VMEM((nc, 128, tq), F32), pltpu.VMEM((nc, 128, tq), F32), pltpu.SemaphoreType.DMA((2,))],
        compiler_params=_params(1), name="attn_b_bwd",
    )(q, kp, vp, sink2, do, lse, delta)


def _qk_bwd(dqa_t, dka_t3, dva_t3, dqb, dkb, dvb, qar, kar, qg2, kg2, tab_a, tab_b, dga, dgb):
    S = dqb.shape[0]
    tm = min(256, S)
    per = dka_t3.shape[2] // tm

    def body(dqa_ref, dka_ref, dva_ref, dqb_ref, dkb_ref, dvb_ref, qar_ref, kar_ref, qg_ref, kg_ref, ta_ref, tb_ref,
             dga_ref, dgb_ref, dp_ref, st_ref):
        @pl.when(pl.program_id(0) == 0)
        def _():
            st_ref[...] = jnp.zeros_like(st_ref)

        seg = _seg_matrix(128, HD)

        def norm_bwd(dz_rot, raw, g):
            dzn = _rope_t(dz_rot, ta_ref, 16)
            raw = raw.astype(F32)
            rr = lax.rsqrt(_seg_sum(raw * raw, seg) * (1.0 / HD) + EPS)
            zhat = raw * rr
            dzh = dzn * g
            draw = rr * (dzh - zhat * (_seg_sum(dzh * zhat, seg) * (1.0 / HD)))
            return draw, jnp.sum(dzn * zhat, axis=0, keepdims=True)

        gq = jnp.zeros((1, 128), F32)
        for p in range(4):
            sl = slice(128 * p, 128 * p + 128)
            draw, gsum = norm_bwd(dqa_ref[sl, :].T * 0.125, qar_ref[:, sl], qg_ref[...])
            gq = gq + gsum
            dp_ref[:, sl] = draw.astype(BF16)
            dp_ref[:, 768 + 128 * p:768 + 128 * p + 128] = _rope_t(dqb_ref[:, sl] * 0.125, tb_ref, 32).astype(BF16)
        draw, gk = norm_bwd(dka_ref[...].T * LN2, kar_ref[...], kg_ref[...])
        dp_ref[:, 512:640] = draw.astype(BF16)
        dp_ref[:, 640:768] = dva_ref[...].T.astype(BF16)
        dp_ref[:, 1280:1408] = _rope_t(dkb_ref[...] * LN2, tb_ref, 32).astype(BF16)
        dp_ref[:, 1408:1536] = dvb_ref[...].astype(BF16)
        dp_ref[:, 1536:2560] = dga_ref[...]
        dp_ref[:, 2560:3584] = dgb_ref[...]
        st_ref[0:1, :] += gq
        st_ref[1:2, :] += gk

    tab = pl.BlockSpec((3, tm, 128), lambda i: (0, i, 0))
    chunk_t = pl.BlockSpec((None, 128, tm), lambda i: (i // per, 0, i % per))
    return pl.pallas_call(
        body, grid=(S // tm,),
        in_specs=[pl.BlockSpec((BW, tm), lambda i: (0, i)), chunk_t, chunk_t, _rows(tm, BW), _rows(tm, 128), _rows(tm, 128),
                  _rows(tm, BW), _rows(tm, 128), _const((1, 128)), _const((1, 128)), tab, tab, _rows(tm, D), _rows(tm, D)],
        out_specs=[_rows(tm, INW), _const((8, 128))],
        out_shape=[jax.ShapeDtypeStruct((S, INW), BF16), jax.ShapeDtypeStruct((8, 128), F32)],
        compiler_params=_params(1), name="qk_bwd",
    )(dqa_t, dka_t3, dva_t3, dqb, dkb, dvb, qar, kar, qg2, kg2, tab_a, tab_b, dga, dgb)


def _in_bwd(dproj, win, x, dx1, modv, n1g, chunks):
    S = x.shape[0]
    tm = min(512, S)
    n = S // tm
    ts = min(256, tm)

    def body(dp_ref, w_ref, x_ref, dx1_ref, mod_ref, g_ref, c_hbm, gx_ref, st_ref, recv_hbm, send_sems, recv_sems):
        start, finish = _exchange_scatter([c_hbm], [recv_hbm], send_sems, recv_sems)

        @pl.when(pl.program_id(0) == 0)
        def _():
            start()
            st_ref[...] = jnp.zeros_like(st_ref)

        subs = [slice(ts * u, ts * u + ts) for u in range(tm // ts)]
        dhs = [lax.dot_general(dp_ref[rows, :], w_ref[...], NT, preferred_element_type=F32) for rows in subs]
        for rows, dh in zip(subs, dhs):
            xt = x_ref[rows, :]
            r = lax.rsqrt(jnp.mean(xt * xt, axis=-1, keepdims=True) + EPS)
            xn = xt * r
            st_ref[0:1, :] += jnp.sum(dh, axis=0, keepdims=True)
            st_ref[1:2, :] += jnp.sum(dh * xn, axis=0, keepdims=True)
            dxn = dh * (g_ref[...] * (1.0 + mod_ref[1:2, :]))
            gx_ref[rows, :] = dx1_ref[rows, :] + r * (dxn - xn * jnp.mean(dxn * xn, axis=-1, keepdims=True))
        pl.when(pl.program_id(0) == n - 1)(finish)

    any_spec = pl.BlockSpec(memory_space=pl.ANY)
    return pl.pallas_call(
        body, grid=(n,),
        in_specs=[_rows(tm, INW), _const((D, INW)), _rows(tm, D), _rows(tm, D), _const((6, D)), _const((1, D)), any_spec],
        out_specs=[_rows(tm, D), _const((8, D)), any_spec],
        out_shape=[jax.ShapeDtypeStruct((S, D), F32), jax.ShapeDtypeStruct((8, D), F32),
                   jax.ShapeDtypeStruct((N_DEV - 1,) + chunks.shape[1:], chunks.dtype)],
        scratch_shapes=_exchange_sems(1),
        compiler_params=_params(1), name="in_bwd",
    )(dproj, win, x, dx1, modv, n1g, chunks)


def _pack_small(st1, st2, stf, dg1, dg2, stqk, dsink, modv, n1g, n2g):
    def body(st1_ref, st2_ref, stf_ref, dg1_ref, dg2_ref, qk_ref, ds_ref, mod_ref, g1_ref, g2_ref, o_ref):
        a1, b1 = st1_ref[0:1, :], st1_ref[1:2, :]
        a2, b2 = st2_ref[0:1, :], st2_ref[1:2, :]
        r = lax.broadcasted_iota(jnp.int32, (128, D), 0)
        c = lax.broadcasted_iota(jnp.int32, (128, D), 1)
        fold_q = (c == r % HD).astype(F32)
        fold_k = (c == HD + r % HD).astype(F32)
        keep = (c == r).astype(F32)

        def place(v, sel):
            return jnp.dot(v, sel, preferred_element_type=F32, precision=lax.Precision.HIGHEST)

        loss = jnp.sum(stf_ref[1:2, :], axis=1, keepdims=True)
        lane = lax.broadcasted_iota(jnp.int32, (1, D), 1)
        rows = [a1, g1_ref[...] * b1, dg1_ref[0:1, :], a2, g2_ref[...] * b2, dg2_ref[0:1, :],
                (1.0 + mod_ref[1:2, :]) * b1, (1.0 + mod_ref[4:5, :]) * b2, stf_ref[0:1, :],
                place(qk_ref[0:1, :], fold_q) + place(qk_ref[1:2, :], fold_k),
                place(ds_ref[0:1, :], keep),
                jnp.where(lane == 0, loss, 0.0)]
        rows += [jnp.zeros((1, D), F32)] * (SMALL_ROWS - len(rows))
        for n, v in enumerate(rows):
            o_ref[n:n + 1, :] = v

    return pl.pallas_call(
        body, out_shape=jax.ShapeDtypeStruct((SMALL_ROWS, D), F32),
        compiler_params=pltpu.CompilerParams(vmem_limit_bytes=V7X_VMEM_LIMIT), name="pack_small",
    )(st1, st2, stf, dg1, dg2, stqk, dsink, modv, n1g, n2g)


def _wada_grad(silu_all, dmod_cols):
    def body(a_ref, b_ref, o_ref):
        o_ref[...] = lax.dot_general(a_ref[...], b_ref[...], TN, preferred_element_type=F32, precision=lax.Precision.HIGHEST)

    return pl.pallas_call(
        body, out_shape=jax.ShapeDtypeStruct((D, dmod_cols.shape[1]), F32),
        compiler_params=pltpu.CompilerParams(vmem_limit_bytes=V7X_VMEM_LIMIT), name="wada_grad",
    )(silu_all, dmod_cols)


def _adamw_sum(parts, w, m, v, name):
    R, C = w.shape
    tr = R if R <= 64 else next(t for t in (256, 128, 64, 32, 16, 8) if R % t == 0)
    n = len(parts)
    dyn = [idx for _, idx in parts if idx is not None and not isinstance(idx, int)]
    b1c = 1.0 - ADAM_B1 ** ADAM_STEP
    b2c = 1.0 - ADAM_B2 ** ADAM_STEP

    def body(*refs):
        refs = refs[len(dyn):]
        g = refs[0][...].astype(F32)
        for k in range(1, n):
            g = g + refs[k][...].astype(F32)
        w_ref, m_ref, v_ref, g_out, d_out, m_out, v_out = refs[n:]
        mn = ADAM_B1 * m_ref[...] + (1.0 - ADAM_B1) * g
        vn = ADAM_B2 * v_ref[...] + (1.0 - ADAM_B2) * jnp.square(g)
        g_out[...] = g
        m_out[...] = mn
        v_out[...] = vn
        d_out[...] = -ADAM_LR * ((mn / b1c) / (jnp.sqrt(vn / b2c) + ADAM_EPS) + ADAM_WD * w_ref[...])

    in_specs = []
    nd = 0
    for a, idx in parts:
        if idx is None:
            in_specs.append(pl.BlockSpec((tr, C), lambda i, *s: (i, 0)))
        elif isinstance(idx, int):
            in_specs.append(pl.BlockSpec((None, tr, C), lambda i, *s, idx=idx: (idx, i, 0)))
        else:
            in_specs.append(pl.BlockSpec((None, tr, C), lambda i, *s, nd=nd: (s[nd][0], i, 0)))
            nd += 1
    blk = pl.BlockSpec((tr, C), lambda i, *s: (i, 0))
    grid_spec = pltpu.PrefetchScalarGridSpec(
        num_scalar_prefetch=len(dyn), grid=(R // tr,), in_specs=in_specs + [blk] * 3, out_specs=[blk] * 4)
    return pl.pallas_call(
        body, grid_spec=grid_spec, out_shape=[jax.ShapeDtypeStruct((R, C), F32)] * 4,
        compiler_params=_params(1), name=name,
    )(*dyn, *[a for a, _ in parts], w, m, v)


def _me():
    return lax.axis_index("x"), lax.axis_index("y"), lax.axis_index("c")


def _peer(k):
    x, y, c = _me()
    return (x ^ ((k >> 2) & 1), y ^ ((k >> 1) & 1), c ^ (k & 1))


def _ada_exchange(c_row, w_ada, b_rows):
    NW = w_ada.shape[1]

    def body(c_ref, w_ref, b_ref, sall_ref, mod_ref, src_ref, mp_ref, send1, recv1, send2, recv2):
        x, y, c = _me()
        me = 4 * x + 2 * y + c
        cv = c_ref[...]
        src_ref[...] = jnp.broadcast_to(cv * jax.nn.sigmoid(cv), (8, D))
        mine = pl.ds(pl.multiple_of(me * 8, 8), 8)
        sall_ref[mine, :] = src_ref[...]
        sends = [pltpu.make_async_remote_copy(src_ref, sall_ref.at[mine, :], send1.at[k - 1], recv1.at[k - 1],
                                              device_id=_peer(k), device_id_type=MESH) for k in range(1, N_DEV)]
        for cp in sends:
            cp.start()
        for k in range(1, N_DEV):
            theirs = pl.ds(pl.multiple_of((me ^ k) * 8, 8), 8)
            pltpu.make_async_remote_copy(src_ref, sall_ref.at[theirs, :], send1.at[k - 1], recv1.at[k - 1],
                                         device_id=_peer(k), device_id_type=MESH).wait_recv()
        for cp in sends:
            cp.wait_send()
        mp_ref[...] = jnp.dot(sall_ref[...], w_ref[...], preferred_element_type=F32, precision=lax.Precision.HIGHEST)
        mod_ref[mine, :] = mp_ref[mine, :] + b_ref[mine, :]
        sends = []
        for k in range(1, N_DEV):
            theirs = pl.ds(pl.multiple_of((me ^ k) * 8, 8), 8)
            sends.append(pltpu.make_async_remote_copy(mp_ref.at[theirs, :], mod_ref.at[mine, :], send2.at[k - 1], recv2.at[k - 1],
                                                      device_id=_peer(k), device_id_type=MESH))
        for cp in sends:
            cp.start()
        for k in range(1, N_DEV):
            theirs = pl.ds(pl.multiple_of((me ^ k) * 8, 8), 8)
            pltpu.make_async_remote_copy(mp_ref.at[mine, :], mod_ref.at[theirs, :], send2.at[k - 1], recv2.at[k - 1],
                                         device_id=_peer(k), device_id_type=MESH).wait_recv()
            mod_ref[theirs, :] = mod_ref[theirs, :] + b_ref[theirs, :]
        for cp in sends:
            cp.wait_send()

    vm = pl.BlockSpec(memory_space=pltpu.VMEM)
    return pl.pallas_call(
        body, in_specs=[vm, vm, vm], out_specs=[vm, vm],
        out_shape=[jax.ShapeDtypeStruct((8 * N_DEV, D), F32), jax.ShapeDtypeStruct((8 * N_DEV, NW), F32)],
        scratch_shapes=[pltpu.VMEM((8, D), F32), pltpu.VMEM((8 * N_DEV, NW), F32)]
        + [pltpu.SemaphoreType.DMA((N_DEV - 1,))] * 4,
        compiler_params=pltpu.CompilerParams(vmem_limit_bytes=V7X_VMEM_LIMIT), name="ada_exchange",
    )(c_row, w_ada, b_rows)


def _weight_gather(shard):
    def body(x_ref, out_ref, send_sems, recv_sems, local_sem):
        x, y, c = _me()
        me, sibling = (x, y, c), (x, y, 1 - c)
        chips = [(1 - x, y), (x, 1 - y), (1 - x, 1 - y)]

        def slot(px, py, pc):
            return out_ref.at[4 * px + 2 * py + pc]

        def copy(k, block, to, src=None):
            return pltpu.make_async_remote_copy(
                src_ref=slot(*block) if src is None else src, dst_ref=slot(*block),
                send_sem=send_sems.at[k], recv_sem=recv_sems.at[k], device_id=to, device_id_type=MESH)

        mine = pltpu.make_async_copy(x_ref, slot(*me), local_sem)
        mine.start()
        first = [copy(0, me, sibling, src=x_ref)]
        first += [copy(1 + j, me, (*chip, c), src=x_ref) for j, chip in enumerate(chips)]
        for cp in first:
            cp.start()
        passed = [copy(4 + j, (*chip, c), sibling) for j, chip in enumerate(chips)]
        for j, chip in enumerate(chips):
            copy(1 + j, (*chip, c), me).wait_recv()
            passed[j].start()
        copy(0, sibling, me).wait_recv()
        for j, chip in enumerate(chips):
            copy(4 + j, (*chip, 1 - c), me).wait_recv()
        for cp in first + passed:
            cp.wait_send()
        mine.wait()

    any_spec = pl.BlockSpec(memory_space=pl.ANY)
    return pl.pallas_call(
        body, in_specs=[any_spec], out_specs=any_spec,
        out_shape=jax.ShapeDtypeStruct((N_DEV,) + shard.shape, shard.dtype),
        scratch_shapes=[pltpu.SemaphoreType.DMA((7,)), pltpu.SemaphoreType.DMA((7,)), pltpu.SemaphoreType.DMA],
        name="weight_gather",
    )(shard)


def _small_gather(block):
    def body(b_ref, out_ref, send_sems, recv_sems):
        x, y, c = _me()
        me = 4 * x + 2 * y + c
        out_ref[me] = b_ref[...]
        sends = [pltpu.make_async_remote_copy(b_ref, out_ref.at[me], send_sems.at[k - 1], recv_sems.at[k - 1],
                                              device_id=_peer(k), device_id_type=MESH) for k in range(1, N_DEV)]
        for cp in sends:
            cp.start()
        for k in range(1, N_DEV):
            pltpu.make_async_remote_copy(b_ref, out_ref.at[me ^ k], send_sems.at[k - 1], recv_sems.at[k - 1],
                                         device_id=_peer(k), device_id_type=MESH).wait_recv()
        for cp in sends:
            cp.wait_send()

    vm = pl.BlockSpec(memory_space=pltpu.VMEM)
    return pl.pallas_call(
        body, in_specs=[vm], out_specs=vm,
        out_shape=jax.ShapeDtypeStruct((N_DEV,) + block.shape, block.dtype),
        scratch_shapes=[pltpu.SemaphoreType.DMA((N_DEV - 1,)), pltpu.SemaphoreType.DMA((N_DEV - 1,))],
        name="small_gather",
    )(block)


def _pack_small_params(b_ada, n1, n2, fg, qn, kn, sink):
    qk = jnp.concatenate([qn.reshape(1, HD), kn.reshape(1, HD), jnp.zeros((1, D - 2 * HD), F32)], axis=1)
    sk = jnp.concatenate([sink.reshape(1, NH), jnp.zeros((1, D - NH), F32)], axis=1)
    return jnp.concatenate([b_ada.reshape(6, D), n1.reshape(1, D), n2.reshape(1, D), fg.reshape(1, D), qk, sk,
                            jnp.zeros((SMALL_ROWS - 11, D), F32)], axis=0)


def _unpack_small(p):
    return (p[0:6].reshape(1, 6 * D), p[6].reshape(1, D), p[9, 0:HD].reshape(1, HD), p[9, HD:2 * HD].reshape(1, HD),
            p[10, 0:NH].reshape(1, NH), p[7].reshape(1, D), p[8].reshape(D))


def kernel(x, c, w_ada, b_ada, norm1_g, w_in, q_norm_a, k_norm_a, sink_b, w_branch, w_out, norm2_g, w_mlp_in, w_mlp_out, final_g, loss_target, m_w_ada, m_b_ada, m_norm1_g, m_w_in, m_q_norm_a, m_k_norm_a, m_sink_b, m_w_branch, m_w_out, m_norm2_g, m_w_mlp_in, m_w_mlp_out, m_final_g, v_w_ada, v_b_ada, v_norm1_g, v_w_in, v_q_norm_a, v_k_norm_a, v_sink_b, v_w_branch, v_w_out, v_norm2_g, v_w_mlp_in, v_w_mlp_out, v_final_g):
    S = x.shape[1]
    xs = x.reshape(S, D)
    tgt = loss_target.reshape(S, D)
    ax, ay, ac = lax.axis_index("x"), lax.axis_index("y"), lax.axis_index("c")
    me = 4 * ax + 2 * ay + ac
    me1 = me.reshape(1).astype(jnp.int32)
    NW = w_ada.shape[2]
    NI = w_in.shape[2]

    silu64, mod64 = _ada_exchange(c.reshape(1, D), w_ada.reshape(D, NW),
                                  jnp.repeat(b_ada.reshape(N_DEV, NW), 8, axis=0))
    silu_all = silu64[0::8]
    modv = mod64[0::8].reshape(6, D)

    win = _weight_gather(w_in[0].astype(BF16)).transpose(1, 0, 2).reshape(D, INW)
    rest_shards = tuple(w[0].astype(BF16) for w in (w_branch, w_out, w_mlp_in, w_mlp_out))

    tab_a, tab_b = _rope_tables(S)
    qg2 = jnp.tile(q_norm_a.reshape(1, HD), (1, 2))
    kg2 = jnp.tile(k_norm_a.reshape(1, HD), (1, 2))
    n1g = norm1_g.reshape(1, D)
    n2g = norm2_g.reshape(1, D)
    fg = final_g.reshape(1, D)
    sink2 = sink_b.reshape(1, NH) * LOG2E

    h, qar, kar, qa, ka, va, qb, kb, vb, ga, gb, qa_t, ka_t3, va_t3 = _in_proj(xs, modv, n1g, win, qg2, kg2, tab_a, tab_b)
    ya, lse_at, wb, wout, wmi, wmo = _attn_a_fwd(qa_t, ka, va_t3, rest_shards)
    lse_a = lse_at.T
    wb = wb.transpose(1, 2, 0, 3).reshape(2, BW, D)
    wout = wout.reshape(D, D)
    pad = ((WIN, WIN), (0, 0))
    kbp, vbp = jnp.pad(kb, pad), jnp.pad(vb, pad)
    tb = min(TQ_B, S)
    yb, lse_bt = _attn_b_fwd(qb, kbp, vbp.reshape((S + 2 * WIN) // tb, tb, 128).transpose(0, 2, 1), sink2)
    lse_b = lse_bt.T
    x1, merged, ua, ub = _merge_out(ya, yb, ga, gb, xs, modv, wb, wout)
    h2, hp, dx2, stf = _mlp_fwd(x1, modv, n2g, wmi, wmo, fg, tgt)

    dhp, dx1, st2 = _mlp_bwd(dx2, x1, hp, modv, n2g, wmi, wmo)
    m2 = _tn_matmul(hp, dx2, 2048, D, "dw_mlp_out", relu_sq=True)
    g_wmo, dg2 = _scale_gate(m2, wmo.reshape(FF, D), modv, 5, "gate2_grad")
    g_wmi = _tn_matmul(h2, dhp, D, 512, "dw_mlp_in", dev_major=True, rows=2048)
    dua, dub, dga, dgb, dya, dyb, dl_a, dl_b, dya_t = _merge_bwd(dx1, modv, ga, gb, ua, ub, ya, yb, wb, wout)
    m1 = _tn_matmul(merged, dx1, D, D, "dw_out", rows=2048)
    g_wout, dg1 = _scale_gate(m1, wout, modv, 2, "gate1_grad")
    g_wb0 = _tn_matmul(ya, dua, BW, D, "dw_branch_a")
    g_wb1 = _tn_matmul(yb, dub, BW, D, "dw_branch_b")
    g_wb = jnp.stack([g_wb0, g_wb1]).reshape(2, BW, N_DEV, 128).transpose(2, 0, 1, 3).reshape(N_DEV, 2 * BW, 128)
    g_wout = g_wout.reshape(N_DEV, 128, D)
    g_wmo = g_wmo.reshape(N_DEV, 512, D)
    dqa_t, dka_t3, dva_t3, r_wb, r_wout, r_wmi, r_wmo = _attn_a_bwd(qa, qa_t, ka_t3, va_t3, dya, dya_t, lse_a, dl_a,
                                                                    (g_wb, g_wout, g_wmi, g_wmo))
    dqb, dkb_t, dvb_t, dsink = _attn_b_bwd(qb, kbp, vbp, sink2, dyb, lse_b, dl_b)
    dkb = dkb_t.transpose(0, 2, 1).reshape(S + 2 * WIN, 128)[WIN:WIN + S]
    dvb = dvb_t.transpose(0, 2, 1).reshape(S + 2 * WIN, 128)[WIN:WIN + S]
    dproj, stqk = _qk_bwd(dqa_t, dka_t3, dva_t3, dqb, dkb, dvb, qar, kar, qg2, kg2, tab_a, tab_b, dga, dgb)
    g_win = _tn_matmul(h, dproj, D, 896, "dw_in", rows=2048)
    g_win = g_win.reshape(D, N_DEV, NI).transpose(1, 0, 2)
    grad_x, st1, r_win = _in_bwd(dproj, win, xs, dx1, modv, n1g, g_win.astype(BF16))

    def adam(name, own, recv, w, m, v):
        shape = w.shape
        w2, m2_, v2 = (a.reshape(own.shape[1:]) for a in (w, m, v))
        outs = _adamw_sum([(own, me1)] + [(recv, k) for k in range(N_DEV - 1)], w2, m2_, v2, name)
        return [a.reshape(shape) for a in outs]

    o_win = adam("adamw_w_in", g_win, r_win, w_in, m_w_in, v_w_in)
    o_wb = adam("adamw_w_branch", g_wb, r_wb, w_branch, m_w_branch, v_w_branch)
    o_wout = adam("adamw_w_out", g_wout, r_wout, w_out, m_w_out, v_w_out)
    o_wmi = adam("adamw_w_mlp_in", g_wmi, r_wmi, w_mlp_in, m_w_mlp_in, v_w_mlp_in)
    o_wmo = adam("adamw_w_mlp_out", g_wmo, r_wmo, w_mlp_out, m_w_mlp_out, v_w_mlp_out)

    small = _pack_small(st1, st2, stf, dg1, dg2, stqk, dsink, modv, n1g, n2g)
    small_all = _small_gather(small)
    sw = _pack_small_params(b_ada, norm1_g, norm2_g, final_g, q_norm_a, k_norm_a, sink_b)
    sm = _pack_small_params(m_b_ada, m_norm1_g, m_norm2_g, m_final_g, m_q_norm_a, m_k_norm_a, m_sink_b)
    sv = _pack_small_params(v_b_ada, v_norm1_g, v_norm2_g, v_final_g, v_q_norm_a, v_k_norm_a, v_sink_b)
    sm_out = _adamw_sum([(small_all, k) for k in range(N_DEV)], sw, sm, sv, "adamw_small")
    loss = sm_out[0][11, 0]
    sm_out = [_unpack_small(a) for a in sm_out]

    dmod_all = small_all[:, 0:6, :].reshape(N_DEV, 6 * D)
    dmod_cols = lax.dynamic_slice_in_dim(dmod_all, me * NW, NW, axis=1)
    g_wada = _wada_grad(silu_all, dmod_cols)
    ada = _adamw_sum([(g_wada, None)], w_ada.reshape(D, NW), m_w_ada.reshape(D, NW), v_w_ada.reshape(D, NW), "adamw_ada")
    ada = [a.reshape(1, D, NW) for a in ada]

    def leaves(k):
        b_, n1_, qn_, kn_, sk_, n2_, fg_ = sm_out[k]
        return [ada[k], b_, n1_, o_win[k], qn_, kn_, sk_, o_wb[k], o_wout[k], n2_, o_wmi[k], o_wmo[k], fg_]

    return (loss, grad_x.reshape(1, S, D), *leaves(0), *leaves(1), *leaves(2), *leaves(3))
```

```python
import jax
import jax.numpy as jnp
from jax import lax
from jax.experimental import pallas as pl
from jax.experimental.pallas import tpu as pltpu

F32, BF16 = jnp.float32, jnp.bfloat16
MESH = pl.DeviceIdType.MESH

D = 1024
HD = 64
NH = 8
GRP = 4
BW = 512
FF = 4096
INW = 3584
GRID_W = 64
WIN = 128
THETA = 10000.0
EPS = 1e-6
NEG = -1e30
N_DEV = 8
LOG2E = 1.4426950408889634
LN2 = 0.6931471805599453
QA_SCALE = 0.125 * LOG2E
SMALL_ROWS = 16
MLP_SHARDS = 4
MLP_ROWS = 512
TK_A = 512
TK_A_FWD = 2048
V7X_VMEM_LIMIT = 56 * 1024 * 1024

ADAM_LR, ADAM_B1, ADAM_B2, ADAM_EPS, ADAM_WD, ADAM_STEP = 0.001, 0.9, 0.999, 1e-08, 0.01, 10

NT = (((1,), (1,)), ((), ()))
TN = (((0,), (0,)), ((), ()))


def _params(n_axes, vmem=V7X_VMEM_LIMIT):
    return pltpu.CompilerParams(dimension_semantics=("arbitrary",) * n_axes, vmem_limit_bytes=vmem)


def _const(shape):
    return pl.BlockSpec(shape, lambda *_: (0,) * len(shape))


def _rows(tm, width):
    return pl.BlockSpec((tm, width), lambda i, *_: (i, 0))


def _seg_matrix(n, seg):
    r = lax.broadcasted_iota(jnp.int32, (n, n), 0) // seg
    c = lax.broadcasted_iota(jnp.int32, (n, n), 1) // seg
    return (r == c).astype(BF16)


def _seg_sum(z, seg_mat):
    hi = z.astype(BF16)
    lo = (z - hi.astype(F32)).astype(BF16)
    return jnp.dot(hi, seg_mat, preferred_element_type=F32) + jnp.dot(lo, seg_mat, preferred_element_type=F32)


def _rope(z, t_ref, sh):
    return z * t_ref[0] + pltpu.roll(z, sh, 1) * t_ref[1] + pltpu.roll(z, 128 - sh, 1) * t_ref[2]


def _rope_t(dz, t_ref, sh):
    return dz * t_ref[0] + pltpu.roll(dz * t_ref[1], 128 - sh, 1) + pltpu.roll(dz * t_ref[2], sh, 1)


def _rope_tables(S):
    t = jnp.arange(S, dtype=jnp.int32)[:, None]
    lane = jnp.arange(128, dtype=jnp.int32)[None, :] % HD

    def build(ang, first):
        cos, sin = jnp.cos(ang), jnp.sin(ang)
        return jnp.stack([cos, jnp.where(first, 0.0, sin), jnp.where(first, -sin, 0.0)]).astype(F32)

    inv_a = (THETA ** (-jnp.arange(0, HD // 2, 2, dtype=F32) / (HD // 2)))[lane % 16]
    pos_a = jnp.where(lane < HD // 2, t // GRID_W, t % GRID_W).astype(F32)
    tab_a = build(pos_a * inv_a, (lane % 32) < 16)
    inv_b = (THETA ** (-jnp.arange(0, HD, 2, dtype=F32) / HD))[lane % 32]
    tab_b = build(t.astype(F32) * inv_b, lane < 32)
    return tab_a, tab_b


def _in_proj(x, modv, n1g, win, qg2, kg2, tab_a, tab_b):
    S = x.shape[0]
    tm = min(512, S)
    tk = min(TK_A, S)
    per = tk // tm
    ts = min(256, tm)

    def body(x_ref, mod_ref, g_ref, w_ref, qg_ref, kg_ref, ta_ref, tb_ref,
             h_ref, qar_ref, kar_ref, qa_ref, ka_ref, va_ref, qb_ref, kb_ref, vb_ref, ga_ref, gb_ref, qat_ref, kat_ref, vat_ref):
        seg = _seg_matrix(128, HD)

        def head_norm(z, g):
            ms = _seg_sum(z * z, seg) * (1.0 / HD)
            return (z * lax.rsqrt(ms + EPS)) * g

        subs = [slice(ts * u, ts * u + ts) for u in range(tm // ts)]
        hbs = []
        for rows in subs:
            xt = x_ref[rows, :]
            r = lax.rsqrt(jnp.mean(xt * xt, axis=-1, keepdims=True) + EPS)
            h = ((xt * r) * g_ref[...]) * (1.0 + mod_ref[1:2, :]) + mod_ref[0:1, :]
            hbs.append(h.astype(BF16))
            h_ref[rows, :] = hbs[-1]
        projs = [jnp.dot(hb, w_ref[...], preferred_element_type=F32) for hb in hbs]
        for rows, proj in zip(subs, projs):
            ta, tb = ta_ref[:, rows, :], tb_ref[:, rows, :]
            for p in range(4):
                z = proj[:, 128 * p:128 * p + 128]
                qar_ref[rows, 128 * p:128 * p + 128] = z.astype(BF16)
                qv = _rope(head_norm(z, qg_ref[...]), ta, 16) * QA_SCALE
                qa_ref[rows, 128 * p:128 * p + 128] = qv.astype(BF16)
                qat_ref[128 * p:128 * p + 128, rows] = qv.T.astype(BF16)
                zb = proj[:, 768 + 128 * p:768 + 128 * p + 128]
                qb_ref[rows, 128 * p:128 * p + 128] = (_rope(zb, tb, 32) * QA_SCALE).astype(BF16)
            z = proj[:, 512:640]
            kar_ref[rows, :] = z.astype(BF16)
            kv_ = _rope(head_norm(z, kg_ref[...]), ta, 16)
            ka_ref[rows, :] = kv_.astype(BF16)
            kat_ref[:, rows] = kv_.T.astype(BF16)
            va_ref[rows, :] = proj[:, 640:768].astype(BF16)
            vat_ref[:, rows] = proj[:, 640:768].T.astype(BF16)
            kb_ref[rows, :] = _rope(proj[:, 1280:1408], tb, 32).astype(BF16)
            vb_ref[rows, :] = proj[:, 1408:1536].astype(BF16)
            ga_ref[rows, :] = proj[:, 1536:2560].astype(BF16)
            gb_ref[rows, :] = proj[:, 2560:3584].astype(BF16)

    tab = pl.BlockSpec((3, tm, 128), lambda i: (0, i, 0))
    shapes = [(D, BF16), (BW, BF16), (128, BF16), (BW, BF16), (128, BF16), (128, BF16),
              (BW, BF16), (128, BF16), (128, BF16), (D, BF16), (D, BF16)]
    return pl.pallas_call(
        body, grid=(S // tm,),
        in_specs=[_rows(tm, D), _const((6, D)), _const((1, D)), _const((D, INW)), _const((1, 128)), _const((1, 128)), tab, tab],
        out_specs=[_rows(tm, w) for w, _ in shapes] + [pl.BlockSpec((BW, tm), lambda i: (0, i))]
        + [pl.BlockSpec((None, 128, tm), lambda i: (i // per, 0, i % per))] * 2,
        out_shape=[jax.ShapeDtypeStruct((S, w), dt) for w, dt in shapes] + [jax.ShapeDtypeStruct((BW, S), BF16)]
        + [jax.ShapeDtypeStruct((S // tk, 128, tk), BF16)] * 2,
        compiler_params=_params(1), name="in_proj",
    )(x, modv, n1g, win, qg2, kg2, tab_a, tab_b)


def _exchange_gather(block_refs, out_refs, send_sems, recv_sems, local_sems):
    x, y, c = _me()
    me = 4 * x + 2 * y + c

    def copies():
        own, out, arrive = [], [], []
        for a, (blk, dst) in enumerate(zip(block_refs, out_refs)):
            own.append(pltpu.make_async_copy(blk, dst.at[me], local_sems.at[a]))
            for k in range(1, N_DEV):
                sems = dict(send_sem=send_sems.at[a, k - 1], recv_sem=recv_sems.at[a, k - 1], device_id=_peer(k), device_id_type=MESH)
                out.append(pltpu.make_async_remote_copy(blk, dst.at[me], **sems))
                arrive.append(pltpu.make_async_remote_copy(blk, dst.at[me ^ k], **sems))
        return own, out, arrive

    def start():
        own, out, _ = copies()
        for cp in own + out:
            cp.start()

    def finish():
        own, out, arrive = copies()
        for cp in arrive:
            cp.wait_recv()
        for cp in out:
            cp.wait_send()
        for cp in own:
            cp.wait()

    return start, finish


def _exchange_scatter(chunk_refs, recv_refs, send_sems, recv_sems):
    x, y, c = _me()
    me = 4 * x + 2 * y + c

    def copies():
        return [pltpu.make_async_remote_copy(src.at[me ^ k], dst.at[k - 1], send_sems.at[a, k - 1], recv_sems.at[a, k - 1],
                                             device_id=_peer(k), device_id_type=MESH)
                for a, (src, dst) in enumerate(zip(chunk_refs, recv_refs)) for k in range(1, N_DEV)]

    def start():
        for cp in copies():
            cp.start()

    def finish():
        cps = copies()
        for cp in cps:
            cp.wait_recv()
        for cp in cps:
            cp.wait_send()

    return start, finish


def _exchange_sems(n):
    return [pltpu.SemaphoreType.DMA((n, N_DEV - 1)), pltpu.SemaphoreType.DMA((n, N_DEV - 1))]


def _attn_a_fwd(qt, k, vt3, shards):
    S = qt.shape[1]
    tq = min(512, S)
    nq = S // tq
    per = max(1, min(TK_A_FWD, S) // vt3.shape[2])
    tk = per * vt3.shape[2]
    nk = S // tk
    ONES = 16
    AHEAD = 4
    HALVES = 2 if tq >= 512 else 1
    tw = tq // HALVES
    NCH = GRP * HALVES
    ns = len(shards)

    def body(q_ref, k_ref, vt_ref, *rest):
        w_hbm, (o_ref, lse_ref), wall_hbm = rest[:ns], rest[ns:ns + 2], rest[ns + 2:2 * ns + 2]
        st_sc, send_sems, recv_sems, local_sems = rest[2 * ns + 2:]
        start, finish = _exchange_gather(w_hbm, wall_hbm, send_sems, recv_sems, local_sems)
        pl.when(pl.program_id(0) == 0)(start)
        row8 = lax.broadcasted_iota(jnp.int32, (NH, tq), 0)
        lse_all = jnp.zeros((NH, tq), F32)
        ones = jnp.ones((ONES, tk), BF16)
        for kv in range(2):
            qts = [q_ref[HD * (GRP * kv + t):HD * (GRP * kv + t) + HD, tw * u:tw * u + tw]
                   for t in range(GRP) for u in range(HALVES)]

            def keys(j, kv=kv):
                off = j * tk if isinstance(j, int) else pl.multiple_of(j * tk, tk)
                return k_ref[pl.ds(off, tk), :][:, HD * kv:HD * kv + HD]

            def scores(kj, t, qts=qts):
                return jnp.dot(kj, qts[t], preferred_element_type=F32)

            def step(j, carry, kv=kv, last=False):
                kj = keys(j)
                kn = None if last else keys(j + 1)
                vt = jnp.concatenate([vt_ref[per * j + u, HD * kv:HD * kv + HD, :] for u in range(per)], axis=1)
                v1 = jnp.concatenate([vt, ones], axis=0)
                sts = [st_sc[t] for t in range(AHEAD)]
                new = []
                for t in range(NCH):
                    m, acc = carry[2 * t], carry[2 * t + 1]
                    if t + AHEAD < NCH:
                        sts.append(scores(kj, t + AHEAD))
                    st = sts[t]
                    mn = jnp.maximum(m, jnp.max(st, axis=0, keepdims=True))
                    pt = jnp.exp2(st - mn)
                    if t + AHEAD >= NCH and not last:
                        st_sc[t + AHEAD - NCH] = scores(kn, t + AHEAD - NCH)
                    acc = jnp.exp2(m - mn) * acc + jnp.dot(v1, pt.astype(BF16), preferred_element_type=F32)
                    new += [mn, acc]
                return tuple(new)

            k0 = keys(0)
            for t in range(AHEAD):
                st_sc[t] = scores(k0, t)
            init = (jnp.full((1, tw), NEG, F32), jnp.zeros((HD + ONES, tw), F32)) * NCH
            res = step(nk - 1, lax.fori_loop(0, nk - 1, step, init), last=True)
            outs = []
            for t in range(GRP):
                o_parts, lse_parts = [], []
                for u in range(HALVES):
                    m, acc = res[2 * (HALVES * t + u)], res[2 * (HALVES * t + u) + 1]
                    l = acc[HD:HD + 1, :]
                    o_parts.append((acc[:HD, :] / l).T)
                    lse_parts.append(m + jnp.log2(l))
                outs.append(jnp.concatenate(o_parts, axis=0))
                lse_all = jnp.where(row8 == GRP * kv + t, jnp.concatenate(lse_parts, axis=1), lse_all)
            o_ref[:, 256 * kv:256 * kv + 256] = jnp.concatenate(outs, axis=1).astype(BF16)
        lse_ref[...] = lse_all
        pl.when(pl.program_id(0) == nq - 1)(finish)

    any_spec = pl.BlockSpec(memory_space=pl.ANY)
    return pl.pallas_call(
        body, grid=(nq,),
        in_specs=[pl.BlockSpec((BW, tq), lambda i: (0, i)), _const((S, 128)), _const(vt3.shape)] + [any_spec] * ns,
        out_specs=[_rows(tq, BW), pl.BlockSpec((NH, tq), lambda i: (0, i))] + [any_spec] * ns,
        out_shape=[jax.ShapeDtypeStruct((S, BW), BF16), jax.ShapeDtypeStruct((NH, S), F32)]
        + [jax.ShapeDtypeStruct((N_DEV,) + s.shape, s.dtype) for s in shards],
        scratch_shapes=[pltpu.VMEM((AHEAD, tk, tw), F32)] + _exchange_sems(ns) + [pltpu.SemaphoreType.DMA((ns,))],
        compiler_params=_params(1), name="attn_a_fwd",
    )(qt, k, vt3, *shards)


def _window_mask(i, tq, S):
    W = tq + 2 * WIN
    r = lax.broadcasted_iota(jnp.int32, (tq, W), 0)
    c = lax.broadcasted_iota(jnp.int32, (tq, W), 1)
    kpos = i * tq - WIN + c
    return (jnp.abs(c - WIN - r) <= WIN) & (kpos >= 0) & (kpos < S)


TQ_B = 256


def _attn_b_fwd(q, kp, vpt3, sink2):
    S = q.shape[0]
    tq = min(TQ_B, S)
    W = tq + 2 * WIN
    nc = vpt3.shape[0]
    ONES = 16

    def body(q_ref, k_ref, vt_ref, sink_ref, o_ref, lse_ref):
        i = pl.program_id(0)
        off = pl.multiple_of(i * tq, tq)
        r = lax.broadcasted_iota(jnp.int32, (W, tq), 1)
        c = lax.broadcasted_iota(jnp.int32, (W, tq), 0)
        kpos = i * tq - WIN + c
        valid = (jnp.abs(c - WIN - r) <= WIN) & (kpos >= 0) & (kpos < S)
        kw = k_ref[pl.ds(off, W), :]
        vt = jnp.concatenate([vt_ref[i + half] for half in range(W // tq)], axis=1)
        ones = jnp.ones((ONES, W), BF16)
        row8 = lax.broadcasted_iota(jnp.int32, (NH, tq), 0)
        lse_all = jnp.zeros((NH, tq), F32)
        qs = []
        for p in range(4):
            qp = q_ref[:, 128 * p:128 * p + 128]
            qs += [qp[:, :HD], qp[:, HD:]]
        khs = [kw[:, HD * kv:HD * kv + HD] for kv in range(2)]
        v1s = [jnp.concatenate([vt[HD * kv:HD * kv + HD, :], ones], axis=0) for kv in range(2)]

        def scores(h):
            return lax.dot_general(khs[h // GRP], qs[h], NT, preferred_element_type=F32)

        ss = [scores(0), scores(1)]
        outs = []
        for h in range(NH):
            if h + 2 < NH:
                ss.append(scores(h + 2))
            st = jnp.where(valid, ss[h], NEG)
            sk = sink_ref[:, h:h + 1]
            m = jnp.maximum(jnp.max(st, axis=0, keepdims=True), sk)
            acc = jnp.dot(v1s[h // GRP], jnp.exp2(st - m).astype(BF16), preferred_element_type=F32)
            l = acc[HD:HD + 1, :] + jnp.exp2(sk - m)
            outs.append((acc[:HD, :] / l).T)
            lse_all = jnp.where(row8 == h, m + jnp.log2(l), lse_all)
        for p in range(4):
            o_ref[:, 128 * p:128 * p + 128] = jnp.concatenate(outs[2 * p:2 * p + 2], axis=1).astype(BF16)
        lse_ref[...] = lse_all

    return pl.pallas_call(
        body, grid=(S // tq,),
        in_specs=[_rows(tq, BW), _const((S + 2 * WIN, 128)), _const((nc, 128, tq)), _const((1, NH))],
        out_specs=[_rows(tq, BW), pl.BlockSpec((NH, tq), lambda i: (0, i))],
        out_shape=[jax.ShapeDtypeStruct((S, BW), BF16), jax.ShapeDtypeStruct((NH, S), F32)],
        compiler_params=_params(1), name="attn_b_fwd",
    )(q, kp, vpt3, sink2)


def _merge_out(ya, yb, ga, gb, x, modv, wb, wout):
    S = x.shape[0]
    tm = min(256, S)

    def body(ya_ref, yb_ref, ga_ref, gb_ref, x_ref, mod_ref, wb_ref, wo_ref, x1_ref, mg_ref, ua_ref, ub_ref):
        ua = jnp.dot(ya_ref[...], wb_ref[0], preferred_element_type=F32)
        ub = jnp.dot(yb_ref[...], wb_ref[1], preferred_element_type=F32)
        merged = jax.nn.sigmoid(ga_ref[...].astype(F32)) * ua + jax.nn.sigmoid(gb_ref[...].astype(F32)) * ub
        mb = merged.astype(BF16)
        ua_ref[...] = ua.astype(BF16)
        ub_ref[...] = ub.astype(BF16)
        mg_ref[...] = mb
        x1_ref[...] = x_ref[...] + mod_ref[2:3, :] * jnp.dot(mb, wo_ref[...], preferred_element_type=F32)

    return pl.pallas_call(
        body, grid=(S // tm,),
        in_specs=[_rows(tm, BW), _rows(tm, BW), _rows(tm, D), _rows(tm, D), _rows(tm, D), _const((6, D)),
                  _const((2, BW, D)), _const((D, D))],
        out_specs=[_rows(tm, D)] * 4,
        out_shape=[jax.ShapeDtypeStruct((S, D), F32)] + [jax.ShapeDtypeStruct((S, D), BF16)] * 3,
        compiler_params=_params(1), name="merge_out",
    )(ya, yb, ga, gb, x, modv, wb, wout)


def _mlp_fwd(x1, modv, n2g, wmi, wmo, fg, target):
    S = x1.shape[0]
    tm = min(MLP_ROWS, S)
    tf = wmi.shape[2]
    nj = wmi.shape[0] // MLP_SHARDS

    def body(x1_ref, mod_ref, g_ref, wi_ref, wo_ref, fg_ref, t_ref, h2_ref, hp_ref, dx2_ref, st_ref, acc_ref):
        i, j = pl.program_id(0), pl.program_id(1)

        @pl.when(j == 0)
        def _():
            xt = x1_ref[...]
            r = lax.rsqrt(jnp.mean(xt * xt, axis=-1, keepdims=True) + EPS)
            h2 = ((xt * r) * g_ref[...]) * (1.0 + mod_ref[4:5, :]) + mod_ref[3:4, :]
            h2_ref[...] = h2.astype(BF16)
            acc_ref[...] = jnp.zeros_like(acc_ref)

        @pl.when((i == 0) & (j == 0))
        def _():
            st_ref[...] = jnp.zeros_like(st_ref)

        out = None
        for u in range(MLP_SHARDS):
            hp = jnp.dot(h2_ref[...], wi_ref[u], preferred_element_type=F32)
            hp_ref[:, tf * u:tf * u + tf] = hp.astype(BF16)
            hid = jnp.square(jnp.maximum(hp, 0.0))
            part = jnp.dot(hid.astype(BF16), wo_ref[u], preferred_element_type=F32)
            out = part if out is None else out + part
        acc_ref[...] += out

        @pl.when(j == nj - 1)
        def _():
            x2 = x1_ref[...] + mod_ref[5:6, :] * acc_ref[...]
            r3 = lax.rsqrt(jnp.mean(x2 * x2, axis=-1, keepdims=True) + EPS)
            xn = x2 * r3
            err = xn * fg_ref[...] - t_ref[...]
            dy = err * (1.0 / D)
            gy = dy * fg_ref[...]
            dx2_ref[...] = r3 * (gy - xn * jnp.mean(gy * xn, axis=-1, keepdims=True))
            st_ref[0:1, :] += jnp.sum(dy * xn, axis=0, keepdims=True)
            st_ref[1:2, :] += jnp.sum(err * err, axis=0, keepdims=True) * (0.5 / D)

    return pl.pallas_call(
        body, grid=(S // tm, nj),
        in_specs=[pl.BlockSpec((tm, D), lambda i, j: (i, 0)), _const((6, D)), _const((1, D)),
                  pl.BlockSpec((MLP_SHARDS, D, tf), lambda i, j: (j, 0, 0)), pl.BlockSpec((MLP_SHARDS, tf, D), lambda i, j: (j, 0, 0)),
                  _const((1, D)), pl.BlockSpec((tm, D), lambda i, j: (i, 0))],
        out_specs=[pl.BlockSpec((tm, D), lambda i, j: (i, 0)), pl.BlockSpec((tm, MLP_SHARDS * tf), lambda i, j: (i, j)),
                   pl.BlockSpec((tm, D), lambda i, j: (i, 0)), _const((8, D))],
        out_shape=[jax.ShapeDtypeStruct((S, D), BF16), jax.ShapeDtypeStruct((S, wmi.shape[0] * tf), BF16),
                   jax.ShapeDtypeStruct((S, D), F32), jax.ShapeDtypeStruct((8, D), F32)],
        scratch_shapes=[pltpu.VMEM((tm, D), F32)],
        compiler_params=_params(2), name="mlp_fwd",
    )(x1, modv, n2g, wmi, wmo, fg, target)


def _mlp_bwd(dx2, x1, hp, modv, n2g, wmi, wmo):
    S = x1.shape[0]
    tm = min(MLP_ROWS, S)
    tf = wmi.shape[2]
    nj = wmi.shape[0] // MLP_SHARDS

    def body(dx2_ref, x1_ref, hp_ref, mod_ref, g_ref, wi_ref, wo_ref, dhp_ref, dx1_ref, st_ref, dmo_ref, acc_ref):
        i, j = pl.program_id(0), pl.program_id(1)

        @pl.when(j == 0)
        def _():
            dmo_ref[...] = (mod_ref[5:6, :] * dx2_ref[...]).astype(BF16)
            acc_ref[...] = jnp.zeros_like(acc_ref)

        @pl.when((i == 0) & (j == 0))
        def _():
            st_ref[...] = jnp.zeros_like(st_ref)

        out = None
        for u in range(MLP_SHARDS):
            sl = slice(tf * u, tf * u + tf)
            dhid = lax.dot_general(dmo_ref[...], wo_ref[u], NT, preferred_element_type=F32)
            dhp = (dhid * (2.0 * jnp.maximum(hp_ref[:, sl].astype(F32), 0.0))).astype(BF16)
            dhp_ref[:, sl] = dhp
            part = lax.dot_general(dhp, wi_ref[u], NT, preferred_element_type=F32)
            out = part if out is None else out + part
        acc_ref[...] += out

        @pl.when(j == nj - 1)
        def _():
            dh2 = acc_ref[...]
            xt = x1_ref[...]
            r = lax.rsqrt(jnp.mean(xt * xt, axis=-1, keepdims=True) + EPS)
            xn = xt * r
            st_ref[0:1, :] += jnp.sum(dh2, axis=0, keepdims=True)
            st_ref[1:2, :] += jnp.sum(dh2 * xn, axis=0, keepdims=True)
            dxn = dh2 * (g_ref[...] * (1.0 + mod_ref[4:5, :]))
            dx1_ref[...] = dx2_ref[...] + r * (dxn - xn * jnp.mean(dxn * xn, axis=-1, keepdims=True))

    return pl.pallas_call(
        body, grid=(S // tm, nj),
        in_specs=[pl.BlockSpec((tm, D), lambda i, j: (i, 0)), pl.BlockSpec((tm, D), lambda i, j: (i, 0)),
                  pl.BlockSpec((tm, MLP_SHARDS * tf), lambda i, j: (i, j)), _const((6, D)), _const((1, D)),
                  pl.BlockSpec((MLP_SHARDS, D, tf), lambda i, j: (j, 0, 0)), pl.BlockSpec((MLP_SHARDS, tf, D), lambda i, j: (j, 0, 0))],
        out_specs=[pl.BlockSpec((tm, MLP_SHARDS * tf), lambda i, j: (i, j)), pl.BlockSpec((tm, D), lambda i, j: (i, 0)), _const((8, D))],
        out_shape=[jax.ShapeDtypeStruct((S, wmi.shape[0] * tf), BF16), jax.ShapeDtypeStruct((S, D), F32),
                   jax.ShapeDtypeStruct((8, D), F32)],
        scratch_shapes=[pltpu.VMEM((tm, D), BF16), pltpu.VMEM((tm, D), F32)],
        compiler_params=_params(2), name="mlp_bwd",
    )(dx2, x1, hp, modv, n2g, wmi, wmo)


def _tn_matmul(a, b, tk, tn, name, relu_sq=False, dev_major=False, rows=1024):
    S, K = a.shape
    N = b.shape[1]
    ts = min(rows, S)
    ns = S // ts

    def body(a_ref, b_ref, o_ref):
        @pl.when(pl.program_id(2) == 0)
        def _():
            o_ref[...] = jnp.zeros_like(o_ref)

        at = a_ref[...]
        if relu_sq:
            at = jnp.square(jnp.maximum(at.astype(F32), 0.0)).astype(BF16)
        o_ref[...] += lax.dot_general(at, b_ref[...].astype(BF16), TN, preferred_element_type=F32)

    if dev_major:
        out_spec = pl.BlockSpec((None, tk, tn), lambda k, n, s: (n, k, 0))
        out_shape = jax.ShapeDtypeStruct((N // tn, K, tn), F32)
    else:
        out_spec = pl.BlockSpec((tk, tn), lambda k, n, s: (k, n))
        out_shape = jax.ShapeDtypeStruct((K, N), F32)
    return pl.pallas_call(
        body, grid=(K // tk, N // tn, ns),
        in_specs=[pl.BlockSpec((ts, tk), lambda k, n, s: (s, k)), pl.BlockSpec((ts, tn), lambda k, n, s: (s, n))],
        out_specs=out_spec, out_shape=out_shape,
        compiler_params=_params(3), name=name,
    )(a, b)


def _scale_gate(m, w, g, row, name):
    K = m.shape[0]
    tk = min(512, K)

    def body(m_ref, w_ref, mod_ref, dw_ref, dg_ref):
        @pl.when(pl.program_id(0) == 0)
        def _():
            dg_ref[...] = jnp.zeros_like(dg_ref)

        mt = m_ref[...]
        dw_ref[...] = mt * mod_ref[row:row + 1, :]
        dg_ref[0:1, :] += jnp.sum(mt * w_ref[...].astype(F32), axis=0, keepdims=True)

    return pl.pallas_call(
        body, grid=(K // tk,),
        in_specs=[_rows(tk, D), _rows(tk, D), _const((6, D))],
        out_specs=[_rows(tk, D), _const((8, D))],
        out_shape=[jax.ShapeDtypeStruct((K, D), F32), jax.ShapeDtypeStruct((8, D), F32)],
        compiler_params=_params(1), name=name,
    )(m, w, g)


def _merge_bwd(dx1, modv, ga, gb, ua, ub, ya, yb, wb, wout):
    S = dx1.shape[0]
    tm = min(512, S)

    def body(dx1_ref, mod_ref, ga_ref, gb_ref, ua_ref, ub_ref, ya_ref, yb_ref, wb_ref, wo_ref,
             dua_ref, dub_ref, dga_ref, dgb_ref, dya_ref, dyb_ref, dla_ref, dlb_ref, dyat_ref):
        dao = (mod_ref[2:3, :] * dx1_ref[...]).astype(BF16)
        dm = lax.dot_general(dao, wo_ref[...], NT, preferred_element_type=F32)
        r = lax.broadcasted_iota(jnp.int32, (BW, NH), 0) // HD
        c = lax.broadcasted_iota(jnp.int32, (BW, NH), 1)
        head_of = (r == c).astype(BF16)
        for br, (g_ref, u_ref, y_ref, du_ref, dg_ref, dy_ref, dl_ref) in enumerate((
                (ga_ref, ua_ref, ya_ref, dua_ref, dga_ref, dya_ref, dla_ref),
                (gb_ref, ub_ref, yb_ref, dub_ref, dgb_ref, dyb_ref, dlb_ref))):
            sg = jax.nn.sigmoid(g_ref[...].astype(F32))
            du = (dm * sg).astype(BF16)
            du_ref[...] = du
            dg_ref[...] = (dm * u_ref[...].astype(F32) * sg * (1.0 - sg)).astype(BF16)
            dy = lax.dot_general(du, wb_ref[br], NT, preferred_element_type=F32)
            dyb16 = dy.astype(BF16)
            dy_ref[...] = dyb16
            if br == 0:
                dyat_ref[...] = dy.T.astype(BF16)
            prod = dyb16.astype(F32) * y_ref[...].astype(F32)
            hi = prod.astype(BF16)
            lo = (prod - hi.astype(F32)).astype(BF16)
            dl_ref[...] = (jnp.dot(hi, head_of, preferred_element_type=F32) + jnp.dot(lo, head_of, preferred_element_type=F32))

    return pl.pallas_call(
        body, grid=(S // tm,),
        in_specs=[_rows(tm, D), _const((6, D)), _rows(tm, D), _rows(tm, D), _rows(tm, D), _rows(tm, D),
                  _rows(tm, BW), _rows(tm, BW), _const((2, BW, D)), _const((D, D))],
        out_specs=[_rows(tm, D)] * 4 + [_rows(tm, BW)] * 2 + [_rows(tm, NH)] * 2 + [pl.BlockSpec((BW, tm), lambda i: (0, i))],
        out_shape=[jax.ShapeDtypeStruct((S, D), BF16)] * 4 + [jax.ShapeDtypeStruct((S, BW), BF16)] * 2
        + [jax.ShapeDtypeStruct((S, NH), F32)] * 2 + [jax.ShapeDtypeStruct((BW, S), BF16)],
        compiler_params=_params(1), name="merge_bwd",
    )(dx1, modv, ga, gb, ua, ub, ya, yb, wb, wout)


def _attn_a_bwd(q, qt, kt3, vt3, do, dot_, lse, delta, chunks):
    S = q.shape[0]
    tq = min(1024, S)
    nk, _, tk = kt3.shape
    nq = S // tq
    nc = len(chunks)

    def body(q_ref, qt_ref, do_ref, dot_ref, lse_ref, dl_ref, kt_ref, vt_ref, *rest):
        g_hbm, (dq_ref, dk_hbm, dv_hbm), recv_hbm = rest[:nc], rest[nc:nc + 3], rest[nc + 3:2 * nc + 3]
        dk_sc, dv_sc, sem, send_sems, recv_sems = rest[2 * nc + 3:]
        i = pl.program_id(0)
        start, finish = _exchange_scatter(g_hbm, recv_hbm, send_sems, recv_sems)

        @pl.when(i == 0)
        def _():
            start()
            dk_sc[...] = jnp.zeros_like(dk_sc)
            dv_sc[...] = jnp.zeros_like(dv_sc)

        for kv in range(2):
            qg = q_ref[:, 256 * kv:256 * kv + 256]
            dog = do_ref[:, 256 * kv:256 * kv + 256]
            heads = []
            for t in range(GRP):
                h = GRP * kv + t
                heads.append((qg[:, HD * t:HD * t + HD], dog[:, HD * t:HD * t + HD],
                              qt_ref[HD * h:HD * h + HD, :], dot_ref[HD * h:HD * h + HD, :],
                              lse_ref[:, h:h + 1], dl_ref[:, h:h + 1]))

            q2 = [jnp.concatenate([heads[2 * u][0], heads[2 * u + 1][0]], axis=0) for u in range(GRP // 2)]
            do2 = [jnp.concatenate([heads[2 * u][1], heads[2 * u + 1][1]], axis=0) for u in range(GRP // 2)]

            def step(j, carry, kv=kv, heads=heads, q2=q2, do2=do2):
                kjt = kt_ref[j, HD * kv:HD * kv + HD, :]
                vjt = vt_ref[j, HD * kv:HD * kv + HD, :]
                dkt = jnp.zeros((HD, tk), F32)
                dvt = jnp.zeros((HD, tk), F32)
                new = []

                def logits(u):
                    return (jnp.dot(q2[u], kjt, preferred_element_type=F32), jnp.dot(do2[u], vjt, preferred_element_type=F32))

                sd = [logits(0)]
                for t, (qh, doh, qth, doth, lse_h, dl_h) in enumerate(heads):
                    if t == 0:
                        sd.append(logits(1))
                    rows = slice(tq * (t % 2), tq * (t % 2) + tq)
                    s, dp = sd[t // 2][0][rows, :], sd[t // 2][1][rows, :]
                    pm = jnp.exp2(s - lse_h)
                    ds = (pm * (dp - dl_h)).astype(BF16)
                    dvt = dvt + jnp.dot(doth, pm.astype(BF16), preferred_element_type=F32)
                    dkt = dkt + jnp.dot(qth, ds, preferred_element_type=F32)
                    new.append(carry[t] + lax.dot_general(kjt, ds, NT, preferred_element_type=F32))
                dk_sc[j, HD * kv:HD * kv + HD, :] += dkt
                dv_sc[j, HD * kv:HD * kv + HD, :] += dvt
                return tuple(new)

            res = lax.fori_loop(0, nk, step, (jnp.zeros((HD, tq), F32),) * GRP)
            for t in range(GRP):
                dq_ref[HD * (GRP * kv + t):HD * (GRP * kv + t) + HD, :] = res[t]

        @pl.when(i == nq - 1)
        def _():
            c1 = pltpu.make_async_copy(dk_sc, dk_hbm, sem.at[0])
            c2 = pltpu.make_async_copy(dv_sc, dv_hbm, sem.at[1])
            c1.start()
            c2.start()
            c1.wait()
            c2.wait()
            finish()

    any_spec = pl.BlockSpec(memory_space=pl.ANY)
    cols = pl.BlockSpec((BW, tq), lambda i: (0, i))
    return pl.pallas_call(
        body, grid=(nq,),
        in_specs=[_rows(tq, BW), cols, _rows(tq, BW), cols, _rows(tq, NH), _rows(tq, NH), _const((nk, 128, tk)),
                  _const((nk, 128, tk))] + [any_spec] * nc,
        out_specs=[cols, any_spec, any_spec] + [any_spec] * nc,
        out_shape=[jax.ShapeDtypeStruct((BW, S), F32), jax.ShapeDtypeStruct((nk, 128, tk), F32),
                   jax.ShapeDtypeStruct((nk, 128, tk), F32)]
        + [jax.ShapeDtypeStruct((N_DEV - 1,) + c.shape[1:], c.dtype) for c in chunks],
        scratch_shapes=[pltpu.VMEM((nk, 128, tk), F32), pltpu.VMEM((nk, 128, tk), F32), pltpu.SemaphoreType.DMA((2,))]
        + _exchange_sems(nc),
        compiler_params=_params(1), name="attn_a_bwd",
    )(q, qt, do, dot_, lse, delta, kt3, vt3, *chunks)


def _attn_b_bwd(q, kp, vp, sink2, do, lse, delta):
    S = q.shape[0]
    tq = min(TQ_B, S)
    W = tq + 2 * WIN
    nq = S // tq
    nc = (S + 2 * WIN) // tq

    def body(q_ref, k_ref, v_ref, sink_ref, do_ref, lse_ref, dl_ref, dq_ref, dk_hbm, dv_hbm, ds_ref, dk_sc, dv_sc, sem):
        i = pl.program_id(0)

        @pl.when(i == 0)
        def _():
            dk_sc[...] = jnp.zeros_like(dk_sc)
            dv_sc[...] = jnp.zeros_like(dv_sc)
            ds_ref[...] = jnp.zeros_like(ds_ref)

        off = pl.multiple_of(i * tq, tq)
        valid = _window_mask(i, tq, S)
        kw = k_ref[pl.ds(off, W), :]
        vw = v_ref[pl.ds(off, W), :]
        lse_i = lse_ref[...]
        dl_i = dl_ref[...]
        qa = q_ref[...]
        doa = do_ref[...]
        qt = qa.astype(F32).T.astype(BF16)
        dot_ = doa.astype(F32).T.astype(BF16)
        khs = [kw[:, HD * kv:HD * kv + HD] for kv in range(2)]
        vhs = [vw[:, HD * kv:HD * kv + HD] for kv in range(2)]

        def logits(h):
            return (lax.dot_general(qa[:, HD * h:HD * h + HD], khs[h // GRP], NT, preferred_element_type=F32),
                    lax.dot_general(doa[:, HD * h:HD * h + HD], vhs[h // GRP], NT, preferred_element_type=F32))

        sd = [logits(0)]
        dqs = []
        dkt = [jnp.zeros((HD, W), F32), jnp.zeros((HD, W), F32)]
        dvt = [jnp.zeros((HD, W), F32), jnp.zeros((HD, W), F32)]
        for h in range(NH):
            kv = h // GRP
            if h + 1 < NH:
                sd.append(logits(h + 1))
            s, dp = sd[h]
            pm = jnp.exp2(jnp.where(valid, s, NEG) - lse_i[:, h:h + 1])
            ds = (pm * (dp - dl_i[:, h:h + 1])).astype(BF16)
            dvt[kv] = dvt[kv] + jnp.dot(dot_[HD * h:HD * h + HD, :], pm.astype(BF16), preferred_element_type=F32)
            dkt[kv] = dkt[kv] + jnp.dot(qt[HD * h:HD * h + HD, :], ds, preferred_element_type=F32)
            dqs.append(jnp.dot(ds, khs[kv], preferred_element_type=F32))
        for p in range(4):
            dq_ref[:, 128 * p:128 * p + 128] = jnp.concatenate(dqs[2 * p:2 * p + 2], axis=1)
        for half in range(W // tq):
            dk_sc[i + half] += jnp.concatenate([d[:, tq * half:tq * half + tq] for d in dkt], axis=0)
            dv_sc[i + half] += jnp.concatenate([d[:, tq * half:tq * half + tq] for d in dvt], axis=0)
        psd = jnp.exp2(sink_ref[...] - lse_i) * dl_i
        r = lax.broadcasted_iota(jnp.int32, (NH, 128), 0)
        c = lax.broadcasted_iota(jnp.int32, (NH, 128), 1)
        row = jnp.dot(jnp.sum(psd, axis=0, keepdims=True), (r == c).astype(F32),
                      preferred_element_type=F32, precision=lax.Precision.HIGHEST)
        ds_ref[...] -= jnp.broadcast_to(row, (8, 128))

        @pl.when(i == nq - 1)
        def _():
            c1 = pltpu.make_async_copy(dk_sc, dk_hbm, sem.at[0])
            c2 = pltpu.make_async_copy(dv_sc, dv_hbm, sem.at[1])
            c1.start()
            c2.start()
            c1.wait()
            c2.wait()

    any_spec = pl.BlockSpec(memory_space=pl.ANY)
    return pl.pallas_call(
        body, grid=(nq,),
        in_specs=[_rows(tq, BW), _const((S + 2 * WIN, 128)), _const((S + 2 * WIN, 128)), _const((1, NH)),
                  _rows(tq, BW), _rows(tq, NH), _rows(tq, NH)],
        out_specs=[_rows(tq, BW), any_spec, any_spec, _const((8, 128))],
        out_shape=[jax.ShapeDtypeStruct((S, BW), F32), jax.ShapeDtypeStruct((nc, 128, tq), F32),
                   jax.ShapeDtypeStruct((nc, 128, tq), F32), jax.ShapeDtypeStruct((8, 128), F32)],
        scratch_shapes=[pltpu.VMEM((nc, 128, tq), F32), pltpu.VMEM((nc, 128, tq), F32), pltpu.SemaphoreType.DMA((2,))],
        compiler_params=_params(1), name="attn_b_bwd",
    )(q, kp, vp, sink2, do, lse, delta)


def _qk_bwd(dqa_t, dka_t3, dva_t3, dqb, dkb, dvb, qar, kar, qg2, kg2, tab_a, tab_b, dga, dgb):
    S = dqb.shape[0]
    tm = min(256, S)
    per = dka_t3.shape[2] // tm

    def body(dqa_ref, dka_ref, dva_ref, dqb_ref, dkb_ref, dvb_ref, qar_ref, kar_ref, qg_ref, kg_ref, ta_ref, tb_ref,
             dga_ref, dgb_ref, dp_ref, st_ref):
        @pl.when(pl.program_id(0) == 0)
        def _():
            st_ref[...] = jnp.zeros_like(st_ref)

        seg = _seg_matrix(128, HD)

        def norm_bwd(dz_rot, raw, g):
            dzn = _rope_t(dz_rot, ta_ref, 16)
            raw = raw.astype(F32)
            rr = lax.rsqrt(_seg_sum(raw * raw, seg) * (1.0 / HD) + EPS)
            zhat = raw * rr
            dzh = dzn * g
            draw = rr * (dzh - zhat * (_seg_sum(dzh * zhat, seg) * (1.0 / HD)))
            return draw, jnp.sum(dzn * zhat, axis=0, keepdims=True)

        gq = jnp.zeros((1, 128), F32)
        for p in range(4):
            sl = slice(128 * p, 128 * p + 128)
            draw, gsum = norm_bwd(dqa_ref[sl, :].T * 0.125, qar_ref[:, sl], qg_ref[...])
            gq = gq + gsum
            dp_ref[:, sl] = draw.astype(BF16)
            dp_ref[:, 768 + 128 * p:768 + 128 * p + 128] = _rope_t(dqb_ref[:, sl] * 0.125, tb_ref, 32).astype(BF16)
        draw, gk = norm_bwd(dka_ref[...].T * LN2, kar_ref[...], kg_ref[...])
        dp_ref[:, 512:640] = draw.astype(BF16)
        dp_ref[:, 640:768] = dva_ref[...].T.astype(BF16)
        dp_ref[:, 1280:1408] = _rope_t(dkb_ref[...] * LN2, tb_ref, 32).astype(BF16)
        dp_ref[:, 1408:1536] = dvb_ref[...].astype(BF16)
        dp_ref[:, 1536:2560] = dga_ref[...]
        dp_ref[:, 2560:3584] = dgb_ref[...]
        st_ref[0:1, :] += gq
        st_ref[1:2, :] += gk

    tab = pl.BlockSpec((3, tm, 128), lambda i: (0, i, 0))
    chunk_t = pl.BlockSpec((None, 128, tm), lambda i: (i // per, 0, i % per))
    return pl.pallas_call(
        body, grid=(S // tm,),
        in_specs=[pl.BlockSpec((BW, tm), lambda i: (0, i)), chunk_t, chunk_t, _rows(tm, BW), _rows(tm, 128), _rows(tm, 128),
                  _rows(tm, BW), _rows(tm, 128), _const((1, 128)), _const((1, 128)), tab, tab, _rows(tm, D), _rows(tm, D)],
        out_specs=[_rows(tm, INW), _const((8, 128))],
        out_shape=[jax.ShapeDtypeStruct((S, INW), BF16), jax.ShapeDtypeStruct((8, 128), F32)],
        compiler_params=_params(1), name="qk_bwd",
    )(dqa_t, dka_t3, dva_t3, dqb, dkb, dvb, qar, kar, qg2, kg2, tab_a, tab_b, dga, dgb)


def _in_bwd(dproj, win, x, dx1, modv, n1g, chunks):
    S = x.shape[0]
    tm = min(512, S)
    n = S // tm
    ts = min(256, tm)

    def body(dp_ref, w_ref, x_ref, dx1_ref, mod_ref, g_ref, c_hbm, gx_ref, st_ref, recv_hbm, send_sems, recv_sems):
        start, finish = _exchange_scatter([c_hbm], [recv_hbm], send_sems, recv_sems)

        @pl.when(pl.program_id(0) == 0)
        def _():
            start()
            st_ref[...] = jnp.zeros_like(st_ref)

        subs = [slice(ts * u, ts * u + ts) for u in range(tm // ts)]
        dhs = [lax.dot_general(dp_ref[rows, :], w_ref[...], NT, preferred_element_type=F32) for rows in subs]
        for rows, dh in zip(subs, dhs):
            xt = x_ref[rows, :]
            r = lax.rsqrt(jnp.mean(xt * xt, axis=-1, keepdims=True) + EPS)
            xn = xt * r
            st_ref[0:1, :] += jnp.sum(dh, axis=0, keepdims=True)
            st_ref[1:2, :] += jnp.sum(dh * xn, axis=0, keepdims=True)
            dxn = dh * (g_ref[...] * (1.0 + mod_ref[1:2, :]))
            gx_ref[rows, :] = dx1_ref[rows, :] + r * (dxn - xn * jnp.mean(dxn * xn, axis=-1, keepdims=True))
        pl.when(pl.program_id(0) == n - 1)(finish)

    any_spec = pl.BlockSpec(memory_space=pl.ANY)
    return pl.pallas_call(
        body, grid=(n,),
        in_specs=[_rows(tm, INW), _const((D, INW)), _rows(tm, D), _rows(tm, D), _const((6, D)), _const((1, D)), any_spec],
        out_specs=[_rows(tm, D), _const((8, D)), any_spec],
        out_shape=[jax.ShapeDtypeStruct((S, D), F32), jax.ShapeDtypeStruct((8, D), F32),
                   jax.ShapeDtypeStruct((N_DEV - 1,) + chunks.shape[1:], chunks.dtype)],
        scratch_shapes=_exchange_sems(1),
        compiler_params=_params(1), name="in_bwd",
    )(dproj, win, x, dx1, modv, n1g, chunks)


def _pack_small(st1, st2, stf, dg1, dg2, stqk, dsink, modv, n1g, n2g):
    def body(st1_ref, st2_ref, stf_ref, dg1_ref, dg2_ref, qk_ref, ds_ref, mod_ref, g1_ref, g2_ref, o_ref):
        a1, b1 = st1_ref[0:1, :], st1_ref[1:2, :]
        a2, b2 = st2_ref[0:1, :], st2_ref[1:2, :]
        r = lax.broadcasted_iota(jnp.int32, (128, D), 0)
        c = lax.broadcasted_iota(jnp.int32, (128, D), 1)
        fold_q = (c == r % HD).astype(F32)
        fold_k = (c == HD + r % HD).astype(F32)
        keep = (c == r).astype(F32)

        def place(v, sel):
            return jnp.dot(v, sel, preferred_element_type=F32, precision=lax.Precision.HIGHEST)

        loss = jnp.sum(stf_ref[1:2, :], axis=1, keepdims=True)
        lane = lax.broadcasted_iota(jnp.int32, (1, D), 1)
        rows = [a1, g1_ref[...] * b1, dg1_ref[0:1, :], a2, g2_ref[...] * b2, dg2_ref[0:1, :],
                (1.0 + mod_ref[1:2, :]) * b1, (1.0 + mod_ref[4:5, :]) * b2, stf_ref[0:1, :],
                place(qk_ref[0:1, :], fold_q) + place(qk_ref[1:2, :], fold_k),
                place(ds_ref[0:1, :], keep),
                jnp.where(lane == 0, loss, 0.0)]
        rows += [jnp.zeros((1, D), F32)] * (SMALL_ROWS - len(rows))
        for n, v in enumerate(rows):
            o_ref[n:n + 1, :] = v

    return pl.pallas_call(
        body, out_shape=jax.ShapeDtypeStruct((SMALL_ROWS, D), F32),
        compiler_params=pltpu.CompilerParams(vmem_limit_bytes=V7X_VMEM_LIMIT), name="pack_small",
    )(st1, st2, stf, dg1, dg2, stqk, dsink, modv, n1g, n2g)


def _wada_grad(silu_all, dmod_cols):
    def body(a_ref, b_ref, o_ref):
        o_ref[...] = lax.dot_general(a_ref[...], b_ref[...], TN, preferred_element_type=F32, precision=lax.Precision.HIGHEST)

    return pl.pallas_call(
        body, out_shape=jax.ShapeDtypeStruct((D, dmod_cols.shape[1]), F32),
        compiler_params=pltpu.CompilerParams(vmem_limit_bytes=V7X_VMEM_LIMIT), name="wada_grad",
    )(silu_all, dmod_cols)


def _adamw_sum(parts, w, m, v, name):
    R, C = w.shape
    tr = R if R <= 64 else next(t for t in (256, 128, 64, 32, 16, 8) if R % t == 0)
    n = len(parts)
    dyn = [idx for _, idx in parts if idx is not None and not isinstance(idx, int)]
    b1c = 1.0 - ADAM_B1 ** ADAM_STEP
    b2c = 1.0 - ADAM_B2 ** ADAM_STEP

    def body(*refs):
        refs = refs[len(dyn):]
        g = refs[0][...].astype(F32)
        for k in range(1, n):
            g = g + refs[k][...].astype(F32)
        w_ref, m_ref, v_ref, g_out, d_out, m_out, v_out = refs[n:]
        mn = ADAM_B1 * m_ref[...] + (1.0 - ADAM_B1) * g
        vn = ADAM_B2 * v_ref[...] + (1.0 - ADAM_B2) * jnp.square(g)
        g_out[...] = g
        m_out[...] = mn
        v_out[...] = vn
        d_out[...] = -ADAM_LR * ((mn / b1c) / (jnp.sqrt(vn / b2c) + ADAM_EPS) + ADAM_WD * w_ref[...])

    in_specs = []
    nd = 0
    for a, idx in parts:
        if idx is None:
            in_specs.append(pl.BlockSpec((tr, C), lambda i, *s: (i, 0)))
        elif isinstance(idx, int):
            in_specs.append(pl.BlockSpec((None, tr, C), lambda i, *s, idx=idx: (idx, i, 0)))
        else:
            in_specs.append(pl.BlockSpec((None, tr, C), lambda i, *s, nd=nd: (s[nd][0], i, 0)))
            nd += 1
    blk = pl.BlockSpec((tr, C), lambda i, *s: (i, 0))
    grid_spec = pltpu.PrefetchScalarGridSpec(
        num_scalar_prefetch=len(dyn), grid=(R // tr,), in_specs=in_specs + [blk] * 3, out_specs=[blk] * 4)
    return pl.pallas_call(
        body, grid_spec=grid_spec, out_shape=[jax.ShapeDtypeStruct((R, C), F32)] * 4,
        compiler_params=_params(1), name=name,
    )(*dyn, *[a for a, _ in parts], w, m, v)


def _me():
    return lax.axis_index("x"), lax.axis_index("y"), lax.axis_index("c")


def _peer(k):
    x, y, c = _me()
    return (x ^ ((k >> 2) & 1), y ^ ((k >> 1) & 1), c ^ (k & 1))


def _ada_exchange(c_row, w_ada, b_rows):
    NW = w_ada.shape[1]

    def body(c_ref, w_ref, b_ref, sall_ref, mod_ref, src_ref, mp_ref, send1, recv1, send2, recv2):
        x, y, c = _me()
        me = 4 * x + 2 * y + c
        cv = c_ref[...]
        src_ref[...] = jnp.broadcast_to(cv * jax.nn.sigmoid(cv), (8, D))
        mine = pl.ds(pl.multiple_of(me * 8, 8), 8)
        sall_ref[mine, :] = src_ref[...]
        sends = [pltpu.make_async_remote_copy(src_ref, sall_ref.at[mine, :], send1.at[k - 1], recv1.at[k - 1],
                                              device_id=_peer(k), device_id_type=MESH) for k in range(1, N_DEV)]
        for cp in sends:
            cp.start()
        for k in range(1, N_DEV):
            theirs = pl.ds(pl.multiple_of((me ^ k) * 8, 8), 8)
            pltpu.make_async_remote_copy(src_ref, sall_ref.at[theirs, :], send1.at[k - 1], recv1.at[k - 1],
                                         device_id=_peer(k), device_id_type=MESH).wait_recv()
        for cp in sends:
            cp.wait_send()
        mp_ref[...] = jnp.dot(sall_ref[...], w_ref[...], preferred_element_type=F32, precision=lax.Precision.HIGHEST)
        mod_ref[mine, :] = mp_ref[mine, :] + b_ref[mine, :]
        sends = []
        for k in range(1, N_DEV):
            theirs = pl.ds(pl.multiple_of((me ^ k) * 8, 8), 8)
            sends.append(pltpu.make_async_remote_copy(mp_ref.at[theirs, :], mod_ref.at[mine, :], send2.at[k - 1], recv2.at[k - 1],
                                                      device_id=_peer(k), device_id_type=MESH))
        for cp in sends:
            cp.start()
        for k in range(1, N_DEV):
            theirs = pl.ds(pl.multiple_of((me ^ k) * 8, 8), 8)
            pltpu.make_async_remote_copy(mp_ref.at[mine, :], mod_ref.at[theirs, :], send2.at[k - 1], recv2.at[k - 1],
                                         device_id=_peer(k), device_id_type=MESH).wait_recv()
            mod_ref[theirs, :] = mod_ref[theirs, :] + b_ref[theirs, :]
        for cp in sends:
            cp.wait_send()

    vm = pl.BlockSpec(memory_space=pltpu.VMEM)
    return pl.pallas_call(
        body, in_specs=[vm, vm, vm], out_specs=[vm, vm],
        out_shape=[jax.ShapeDtypeStruct((8 * N_DEV, D), F32), jax.ShapeDtypeStruct((8 * N_DEV, NW), F32)],
        scratch_shapes=[pltpu.VMEM((8, D), F32), pltpu.VMEM((8 * N_DEV, NW), F32)]
        + [pltpu.SemaphoreType.DMA((N_DEV - 1,))] * 4,
        compiler_params=pltpu.CompilerParams(vmem_limit_bytes=V7X_VMEM_LIMIT), name="ada_exchange",
    )(c_row, w_ada, b_rows)


def _weight_gather(shard):
    def body(x_ref, out_ref, send_sems, recv_sems, local_sem):
        x, y, c = _me()
        me, sibling = (x, y, c), (x, y, 1 - c)
        chips = [(1 - x, y), (x, 1 - y), (1 - x, 1 - y)]

        def slot(px, py, pc):
            return out_ref.at[4 * px + 2 * py + pc]

        def copy(k, block, to, src=None):
            return pltpu.make_async_remote_copy(
                src_ref=slot(*block) if src is None else src, dst_ref=slot(*block),
                send_sem=send_sems.at[k], recv_sem=recv_sems.at[k], device_id=to, device_id_type=MESH)

        mine = pltpu.make_async_copy(x_ref, slot(*me), local_sem)
        mine.start()
        first = [copy(0, me, sibling, src=x_ref)]
        first += [copy(1 + j, me, (*chip, c), src=x_ref) for j, chip in enumerate(chips)]
        for cp in first:
            cp.start()
        passed = [copy(4 + j, (*chip, c), sibling) for j, chip in enumerate(chips)]
        for j, chip in enumerate(chips):
            copy(1 + j, (*chip, c), me).wait_recv()
            passed[j].start()
        copy(0, sibling, me).wait_recv()
        for j, chip in enumerate(chips):
            copy(4 + j, (*chip, 1 - c), me).wait_recv()
        for cp in first + passed:
            cp.wait_send()
        mine.wait()

    any_spec = pl.BlockSpec(memory_space=pl.ANY)
    return pl.pallas_call(
        body, in_specs=[any_spec], out_specs=any_spec,
        out_shape=jax.ShapeDtypeStruct((N_DEV,) + shard.shape, shard.dtype),
        scratch_shapes=[pltpu.SemaphoreType.DMA((7,)), pltpu.SemaphoreType.DMA((7,)), pltpu.SemaphoreType.DMA],
        name="weight_gather",
    )(shard)


def _small_gather(block):
    def body(b_ref, out_ref, send_sems, recv_sems):
        x, y, c = _me()
        me = 4 * x + 2 * y + c
        out_ref[me] = b_ref[...]
        sends = [pltpu.make_async_remote_copy(b_ref, out_ref.at[me], send_sems.at[k - 1], recv_sems.at[k - 1],
                                              device_id=_peer(k), device_id_type=MESH) for k in range(1, N_DEV)]
        for cp in sends:
            cp.start()
        for k in range(1, N_DEV):
            pltpu.make_async_remote_copy(b_ref, out_ref.at[me ^ k], send_sems.at[k - 1], recv_sems.at[k - 1],
                                         device_id=_peer(k), device_id_type=MESH).wait_recv()
        for cp in sends:
            cp.wait_send()

    vm = pl.BlockSpec(memory_space=pltpu.VMEM)
    return pl.pallas_call(
        body, in_specs=[vm], out_specs=vm,
        out_shape=jax.ShapeDtypeStruct((N_DEV,) + block.shape, block.dtype),
        scratch_shapes=[pltpu.SemaphoreType.DMA((N_DEV - 1,)), pltpu.SemaphoreType.DMA((N_DEV - 1,))],
        name="small_gather",
    )(block)


def _pack_small_params(b_ada, n1, n2, fg, qn, kn, sink):
    qk = jnp.concatenate([qn.reshape(1, HD), kn.reshape(1, HD), jnp.zeros((1, D - 2 * HD), F32)], axis=1)
    sk = jnp.concatenate([sink.reshape(1, NH), jnp.zeros((1, D - NH), F32)], axis=1)
    return jnp.concatenate([b_ada.reshape(6, D), n1.reshape(1, D), n2.reshape(1, D), fg.reshape(1, D), qk, sk,
                            jnp.zeros((SMALL_ROWS - 11, D), F32)], axis=0)


def _unpack_small(p):
    return (p[0:6].reshape(1, 6 * D), p[6].reshape(1, D), p[9, 0:HD].reshape(1, HD), p[9, HD:2 * HD].reshape(1, HD),
            p[10, 0:NH].reshape(1, NH), p[7].reshape(1, D), p[8].reshape(D))


def kernel(x, c, w_ada, b_ada, norm1_g, w_in, q_norm_a, k_norm_a, sink_b, w_branch, w_out, norm2_g, w_mlp_in, w_mlp_out, final_g, loss_target, m_w_ada, m_b_ada, m_norm1_g, m_w_in, m_q_norm_a, m_k_norm_a, m_sink_b, m_w_branch, m_w_out, m_norm2_g, m_w_mlp_in, m_w_mlp_out, m_final_g, v_w_ada, v_b_ada, v_norm1_g, v_w_in, v_q_norm_a, v_k_norm_a, v_sink_b, v_w_branch, v_w_out, v_norm2_g, v_w_mlp_in, v_w_mlp_out, v_final_g):
    S = x.shape[1]
    xs = x.reshape(S, D)
    tgt = loss_target.reshape(S, D)
    ax, ay, ac = lax.axis_index("x"), lax.axis_index("y"), lax.axis_index("c")
    me = 4 * ax + 2 * ay + ac
    me1 = me.reshape(1).astype(jnp.int32)
    NW = w_ada.shape[2]
    NI = w_in.shape[2]

    silu64, mod64 = _ada_exchange(c.reshape(1, D), w_ada.reshape(D, NW),
                                  jnp.repeat(b_ada.reshape(N_DEV, NW), 8, axis=0))
    silu_all = silu64[0::8]
    modv = mod64[0::8].reshape(6, D)

    win = _weight_gather(w_in[0].astype(BF16)).transpose(1, 0, 2).reshape(D, INW)
    rest_shards = tuple(w[0].astype(BF16) for w in (w_branch, w_out, w_mlp_in, w_mlp_out))

    tab_a, tab_b = _rope_tables(S)
    qg2 = jnp.tile(q_norm_a.reshape(1, HD), (1, 2))
    kg2 = jnp.tile(k_norm_a.reshape(1, HD), (1, 2))
    n1g = norm1_g.reshape(1, D)
    n2g = norm2_g.reshape(1, D)
    fg = final_g.reshape(1, D)
    sink2 = sink_b.reshape(1, NH) * LOG2E

    h, qar, kar, qa, ka, va, qb, kb, vb, ga, gb, qa_t, ka_t3, va_t3 = _in_proj(xs, modv, n1g, win, qg2, kg2, tab_a, tab_b)
    ya, lse_at, wb, wout, wmi, wmo = _attn_a_fwd(qa_t, ka, va_t3, rest_shards)
    lse_a = lse_at.T
    wb = wb.transpose(1, 2, 0, 3).reshape(2, BW, D)
    wout = wout.reshape(D, D)
    pad = ((WIN, WIN), (0, 0))
    kbp, vbp = jnp.pad(kb, pad), jnp.pad(vb, pad)
    tb = min(TQ_B, S)
    yb, lse_bt = _attn_b_fwd(qb, kbp, vbp.reshape((S + 2 * WIN) // tb, tb, 128).transpose(0, 2, 1), sink2)
    lse_b = lse_bt.T
    x1, merged, ua, ub = _merge_out(ya, yb, ga, gb, xs, modv, wb, wout)
    h2, hp, dx2, stf = _mlp_fwd(x1, modv, n2g, wmi, wmo, fg, tgt)

    dhp, dx1, st2 = _mlp_bwd(dx2, x1, hp, modv, n2g, wmi, wmo)
    m2 = _tn_matmul(hp, dx2, 2048, D, "dw_mlp_out", relu_sq=True)
    g_wmo, dg2 = _scale_gate(m2, wmo.reshape(FF, D), modv, 5, "gate2_grad")
    g_wmi = _tn_matmul(h2, dhp, D, 512, "dw_mlp_in", dev_major=True, rows=2048)
    dua, dub, dga, dgb, dya, dyb, dl_a, dl_b, dya_t = _merge_bwd(dx1, modv, ga, gb, ua, ub, ya, yb, wb, wout)
    m1 = _tn_matmul(merged, dx1, D, D, "dw_out", rows=2048)
    g_wout, dg1 = _scale_gate(m1, wout, modv, 2, "gate1_grad")
    g_wb0 = _tn_matmul(ya, dua, BW, D, "dw_branch_a")
    g_wb1 = _tn_matmul(yb, dub, BW, D, "dw_branch_b")
    g_wb = jnp.stack([g_wb0, g_wb1]).reshape(2, BW, N_DEV, 128).transpose(2, 0, 1, 3).reshape(N_DEV, 2 * BW, 128)
    g_wout = g_wout.reshape(N_DEV, 128, D)
    g_wmo = g_wmo.reshape(N_DEV, 512, D)
    dqa_t, dka_t3, dva_t3, r_wb, r_wout, r_wmi, r_wmo = _attn_a_bwd(qa, qa_t, ka_t3, va_t3, dya, dya_t, lse_a, dl_a,
                                                                    (g_wb, g_wout, g_wmi, g_wmo))
    dqb, dkb_t, dvb_t, dsink = _attn_b_bwd(qb, kbp, vbp, sink2, dyb, lse_b, dl_b)
    dkb = dkb_t.transpose(0, 2, 1).reshape(S + 2 * WIN, 128)[WIN:WIN + S]
    dvb = dvb_t.transpose(0, 2, 1).reshape(S + 2 * WIN, 128)[WIN:WIN + S]
    dproj, stqk = _qk_bwd(dqa_t, dka_t3, dva_t3, dqb, dkb, dvb, qar, kar, qg2, kg2, tab_a, tab_b, dga, dgb)
    g_win = _tn_matmul(h, dproj, D, 896, "dw_in", rows=2048)
    g_win = g_win.reshape(D, N_DEV, NI).transpose(1, 0, 2)
    grad_x, st1, r_win = _in_bwd(dproj, win, xs, dx1, modv, n1g, g_win.astype(BF16))

    def adam(name, own, recv, w, m, v):
        shape = w.shape
        w2, m2_, v2 = (a.reshape(own.shape[1:]) for a in (w, m, v))
        outs = _adamw_sum([(own, me1)] + [(recv, k) for k in range(N_DEV - 1)], w2, m2_, v2, name)
        return [a.reshape(shape) for a in outs]

    o_win = adam("adamw_w_in", g_win, r_win, w_in, m_w_in, v_w_in)
    o_wb = adam("adamw_w_branch", g_wb, r_wb, w_branch, m_w_branch, v_w_branch)
    o_wout = adam("adamw_w_out", g_wout, r_wout, w_out, m_w_out, v_w_out)
    o_wmi = adam("adamw_w_mlp_in", g_wmi, r_wmi, w_mlp_in, m_w_mlp_in, v_w_mlp_in)
    o_wmo = adam("adamw_w_mlp_out", g_wmo, r_wmo, w_mlp_out, m_w_mlp_out, v_w_mlp_out)

    small = _pack_small(st1, st2, stf, dg1, dg2, stqk, dsink, modv, n1g, n2g)
    small_all = _small_gather(small)
    sw = _pack_small_params(b_ada, norm1_g, norm2_g, final_g, q_norm_a, k_norm_a, sink_b)
    sm = _pack_small_params(m_b_ada, m_norm1_g, m_norm2_g, m_final_g, m_q_norm_a, m_k_norm_a, m_sink_b)
    sv = _pack_small_params(v_b_ada, v_norm1_g, v_norm2_g, v_final_g, v_q_norm_a, v_k_norm_a, v_sink_b)
    sm_out = _adamw_sum([(small_all, k) for k in range(N_DEV)], sw, sm, sv, "adamw_small")
    loss = sm_out[0][11, 0]
    sm_out = [_unpack_small(a) for a in sm_out]

    dmod_all = small_all[:, 0:6, :].reshape(N_DEV, 6 * D)
    dmod_cols = lax.dynamic_slice_in_dim(dmod_all, me * NW, NW, axis=1)
    g_wada = _wada_grad(silu_all, dmod_cols)
    ada = _adamw_sum([(g_wada, None)], w_ada.reshape(D, NW), m_w_ada.reshape(D, NW), v_w_ada.reshape(D, NW), "adamw_ada")
    ada = [a.reshape(1, D, NW) for a in ada]

    def leaves(k):
        b_, n1_, qn_, kn_, sk_, n2_, fg_ = sm_out[k]
        return [ada[k], b_, n1_, o_win[k], qn_, kn_, sk_, o_wb[k], o_wout[k], n2_, o_wmi[k], o_wmo[k], fg_]

    return (loss, grad_x.reshape(1, S, D), *leaves(0), *leaves(1), *leaves(2), *leaves(3))
```
